```python
import jax, jax.numpy as jnp
from jax import lax
import numpy as np

D_MODEL = 1024
BATCH = 8
SEQ = 8192
DEPTH = 2

N_MEM = 256
D_A = D_MODEL // 2
D_B = D_MODEL // 2
A_KERNEL = 31
B_KERNEL = 3
D_EVEN_IN = 2 * D_A + 3 * D_B
CHUNK = 128
C_GROUPS = 8
D_C = D_MODEL
C_GROUP_DIM = D_C // C_GROUPS
XA_HEADS = 4
XA_HEAD_DIM = D_MODEL // XA_HEADS
D_FF = ((8 * D_MODEL // 3 + 255) // 256) * 256
N_EVEN = (DEPTH + 1) // 2
N_ODD = DEPTH // 2
RMS_EPS = 1e-6
LN_EPS = 1e-5

kernel_name = 'hybrid_conv_sgu_memxattn_encoder'


def rms_norm(x, g):
    xf = x.astype(jnp.float32)
    y = xf * lax.rsqrt(jnp.mean(xf * xf, axis=-1, keepdims=True) + RMS_EPS)
    return (y * g.astype(jnp.float32)).astype(x.dtype)


def layer_norm(x, g, b):
    xf = x.astype(jnp.float32)
    mu = jnp.mean(xf, axis=-1, keepdims=True)
    var = jnp.mean(jnp.square(xf - mu), axis=-1, keepdims=True)
    y = (xf - mu) * lax.rsqrt(var + LN_EPS)
    return (y * g.astype(jnp.float32) + b.astype(jnp.float32)).astype(x.dtype)


def depthwise_conv(x, w, b):
    k = w.shape[0]
    pad = k // 2
    y = lax.conv_general_dilated(
        x, w[:, None, :].astype(x.dtype), window_strides=(1,),
        padding=[(pad, pad)], dimension_numbers=('NWC', 'WIO', 'NWC'),
        feature_group_count=x.shape[-1])
    return y + b


def conv_pair_mixer(n, w_in, a_conv_w, a_conv_b, a_ln_g, a_ln_b, b_conv_w, b_conv_b, w_out):
    z = n @ w_in
    a_val, a_gate, b_h, b_gb, b_gc = jnp.split(
        z, [D_A, 2 * D_A, 2 * D_A + D_B, 2 * D_A + 2 * D_B], axis=-1)
    a = a_val * jax.nn.sigmoid(a_gate)
    a = depthwise_conv(a, a_conv_w, a_conv_b)
    a = jax.nn.silu(layer_norm(a, a_ln_g, a_ln_b))
    b = b_gb * depthwise_conv(b_gc * b_h, b_conv_w, b_conv_b)
    return jnp.concatenate([a, b], axis=-1) @ w_out


def chunked_sgu_mixer(n, w_in, c_ln_g, c_ln_b, w_s, b_s, w_out):
    bsz, s, _ = n.shape
    z = jax.nn.gelu(n @ w_in)
    u, v = jnp.split(z, 2, axis=-1)
    v = layer_norm(v, c_ln_g, c_ln_b)
    v = v.reshape(bsz, s // CHUNK, CHUNK, C_GROUPS, C_GROUP_DIM)
    sv = jnp.einsum('gpq,bnqgc->bnpgc', w_s, v) + jnp.transpose(b_s)[:, :, None]
    y = u * sv.reshape(bsz, s, D_C)
    return y @ w_out


def memory_cross_attention(n, mem_n, w_q, w_k, w_v, w_o):
    bsz, s, _ = n.shape
    m = mem_n.shape[1]
    q = (n @ w_q).reshape(bsz, s, XA_HEADS, XA_HEAD_DIM)
    k = (mem_n @ w_k).reshape(bsz, m, XA_HEADS, XA_HEAD_DIM)
    v = (mem_n @ w_v).reshape(bsz, m, XA_HEADS, XA_HEAD_DIM)
    scores = jnp.einsum('bshd,bmhd->bhsm', q, k).astype(jnp.float32) * (XA_HEAD_DIM ** -0.5)
    p = jax.nn.softmax(scores, axis=-1).astype(v.dtype)
    o = jnp.einsum('bhsm,bmhd->bshd', p, v).reshape(bsz, s, D_MODEL)
    return o @ w_o


def swiglu(n, w_gate, w_up, w_down):
    return (jax.nn.silu(n @ w_gate) * (n @ w_up)) @ w_down


def _fwd_setup_inputs(seed: int = 0) -> dict:
    key = jax.random.key(seed)
    ks = iter(jax.random.split(key, 40))

    def nrm(shape, scale):
        return jax.random.normal(next(ks), shape, jnp.float32) * scale

    def gain(shape):
        return 1.0 + 0.1 * jax.random.normal(next(ks), shape, jnp.float32)

    D = D_MODEL
    return {
        'x': nrm((BATCH, SEQ, D), 1.0),
        'mem': nrm((BATCH, N_MEM, D), 1.0),
        'g_mix': gain((DEPTH, D)),
        'g_xattn': gain((DEPTH, D)),
        'g_mem': gain((DEPTH, D)),
        'g_ffn': gain((DEPTH, D)),
        'g_final': gain((D,)),
        'ev_w_in': nrm((N_EVEN, D, D_EVEN_IN), D ** -0.5),
        'ev_a_conv_w': nrm((N_EVEN, A_KERNEL, D_A), A_KERNEL ** -0.5),
        'ev_a_conv_b': nrm((N_EVEN, D_A), 0.02),
        'ev_a_ln_g': gain((N_EVEN, D_A)),
        'ev_a_ln_b': nrm((N_EVEN, D_A), 0.02),
        'ev_b_conv_w': nrm((N_EVEN, B_KERNEL, D_B), B_KERNEL ** -0.5),
        'ev_b_conv_b': nrm((N_EVEN, D_B), 0.02),
        'ev_w_out': nrm((N_EVEN, D_A + D_B, D), (D_A + D_B) ** -0.5),
        'od_w_in': nrm((N_ODD, D, 2 * D_C), D ** -0.5),
        'od_c_ln_g': gain((N_ODD, D_C)),
        'od_c_ln_b': nrm((N_ODD, D_C), 0.02),
        'od_w_s': nrm((N_ODD, C_GROUPS, CHUNK, CHUNK), CHUNK ** -0.5),
        'od_b_s': gain((N_ODD, C_GROUPS, CHUNK)),
        'od_w_out': nrm((N_ODD, D_C, D), D_C ** -0.5),
        'xa_w_q': nrm((DEPTH, D, D), D ** -0.5),
        'xa_w_k': nrm((DEPTH, D, D), D ** -0.5),
        'xa_w_v': nrm((DEPTH, D, D), D ** -0.5),
        'xa_w_o': nrm((DEPTH, D, D), D ** -0.5),
        'ffn_w_gate': nrm((DEPTH, D, D_FF), D ** -0.5),
        'ffn_w_up': nrm((DEPTH, D, D_FF), D ** -0.5),
        'ffn_w_down': nrm((DEPTH, D_FF, D), D_FF ** -0.5),
    }


def _fwd_reference(x, mem, g_mix, g_xattn, g_mem, g_ffn, g_final,
              ev_w_in, ev_a_conv_w, ev_a_conv_b, ev_a_ln_g, ev_a_ln_b,
              ev_b_conv_w, ev_b_conv_b, ev_w_out,
              od_w_in, od_c_ln_g, od_c_ln_b, od_w_s, od_b_s, od_w_out,
              xa_w_q, xa_w_k, xa_w_v, xa_w_o,
              ffn_w_gate, ffn_w_up, ffn_w_down):
    h = x
    for i in range(DEPTH):
        j = i // 2
        n = rms_norm(h, g_mix[i])
        if i % 2 == 0:
            h = h + conv_pair_mixer(n, ev_w_in[j], ev_a_conv_w[j], ev_a_conv_b[j],
                                    ev_a_ln_g[j], ev_a_ln_b[j], ev_b_conv_w[j],
                                    ev_b_conv_b[j], ev_w_out[j])
        else:
            h = h + chunked_sgu_mixer(n, od_w_in[j], od_c_ln_g[j], od_c_ln_b[j],
                                      od_w_s[j], od_b_s[j], od_w_out[j])
        mem_n = rms_norm(mem, g_mem[i])
        h = h + memory_cross_attention(rms_norm(h, g_xattn[i]), mem_n,
                                       xa_w_q[i], xa_w_k[i], xa_w_v[i], xa_w_o[i])
        h = h + swiglu(rms_norm(h, g_ffn[i]), ffn_w_gate[i], ffn_w_up[i], ffn_w_down[i])
    return rms_norm(h, g_final)


import jax as _jax
import jax.numpy as _jnp

TWIN_FORMAT = 'train_step'
FWD_PARAMS = ['x', 'mem', 'g_mix', 'g_xattn', 'g_mem', 'g_ffn', 'g_final', 'ev_w_in', 'ev_a_conv_w', 'ev_a_conv_b', 'ev_a_ln_g', 'ev_a_ln_b', 'ev_b_conv_w', 'ev_b_conv_b', 'ev_w_out', 'od_w_in', 'od_c_ln_g', 'od_c_ln_b', 'od_w_s', 'od_b_s', 'od_w_out', 'xa_w_q', 'xa_w_k', 'xa_w_v', 'xa_w_o', 'ffn_w_gate', 'ffn_w_up', 'ffn_w_down']
TWIN_WEIGHTS = ['g_mix', 'g_xattn', 'g_mem', 'g_ffn', 'g_final', 'ev_w_in', 'ev_a_conv_w', 'ev_a_conv_b', 'ev_a_ln_g', 'ev_a_ln_b', 'ev_b_conv_w', 'ev_b_conv_b', 'ev_w_out', 'od_w_in', 'od_c_ln_g', 'od_c_ln_b', 'od_w_s', 'od_b_s', 'od_w_out', 'xa_w_q', 'xa_w_k', 'xa_w_v', 'xa_w_o', 'ffn_w_gate', 'ffn_w_up', 'ffn_w_down']
TWIN_DIFF_INPUT = 'x'
TWIN_INPUTS = ['x', 'mem', 'g_mix', 'g_xattn', 'g_mem', 'g_ffn', 'g_final', 'ev_w_in', 'ev_a_conv_w', 'ev_a_conv_b', 'ev_a_ln_g', 'ev_a_ln_b', 'ev_b_conv_w', 'ev_b_conv_b', 'ev_w_out', 'od_w_in', 'od_c_ln_g', 'od_c_ln_b', 'od_w_s', 'od_b_s', 'od_w_out', 'xa_w_q', 'xa_w_k', 'xa_w_v', 'xa_w_o', 'ffn_w_gate', 'ffn_w_up', 'ffn_w_down', 'loss_target', 'm_g_mix', 'm_g_xattn', 'm_g_mem', 'm_g_ffn', 'm_g_final', 'm_ev_w_in', 'm_ev_a_conv_w', 'm_ev_a_conv_b', 'm_ev_a_ln_g', 'm_ev_a_ln_b', 'm_ev_b_conv_w', 'm_ev_b_conv_b', 'm_ev_w_out', 'm_od_w_in', 'm_od_c_ln_g', 'm_od_c_ln_b', 'm_od_w_s', 'm_od_b_s', 'm_od_w_out', 'm_xa_w_q', 'm_xa_w_k', 'm_xa_w_v', 'm_xa_w_o', 'm_ffn_w_gate', 'm_ffn_w_up', 'm_ffn_w_down', 'v_g_mix', 'v_g_xattn', 'v_g_mem', 'v_g_ffn', 'v_g_final', 'v_ev_w_in', 'v_ev_a_conv_w', 'v_ev_a_conv_b', 'v_ev_a_ln_g', 'v_ev_a_ln_b', 'v_ev_b_conv_w', 'v_ev_b_conv_b', 'v_ev_w_out', 'v_od_w_in', 'v_od_c_ln_g', 'v_od_c_ln_b', 'v_od_w_s', 'v_od_b_s', 'v_od_w_out', 'v_xa_w_q', 'v_xa_w_k', 'v_xa_w_v', 'v_xa_w_o', 'v_ffn_w_gate', 'v_ffn_w_up', 'v_ffn_w_down']
TWIN_OUTPUTS = ['loss', 'grad_x', 'grad_g_mix', 'grad_g_xattn', 'grad_g_mem', 'grad_g_ffn', 'grad_g_final', 'grad_ev_w_in', 'grad_ev_a_conv_w', 'grad_ev_a_conv_b', 'grad_ev_a_ln_g', 'grad_ev_a_ln_b', 'grad_ev_b_conv_w', 'grad_ev_b_conv_b', 'grad_ev_w_out', 'grad_od_w_in', 'grad_od_c_ln_g', 'grad_od_c_ln_b', 'grad_od_w_s', 'grad_od_b_s', 'grad_od_w_out', 'grad_xa_w_q', 'grad_xa_w_k', 'grad_xa_w_v', 'grad_xa_w_o', 'grad_ffn_w_gate', 'grad_ffn_w_up', 'grad_ffn_w_down', 'delta_g_mix', 'delta_g_xattn', 'delta_g_mem', 'delta_g_ffn', 'delta_g_final', 'delta_ev_w_in', 'delta_ev_a_conv_w', 'delta_ev_a_conv_b', 'delta_ev_a_ln_g', 'delta_ev_a_ln_b', 'delta_ev_b_conv_w', 'delta_ev_b_conv_b', 'delta_ev_w_out', 'delta_od_w_in', 'delta_od_c_ln_g', 'delta_od_c_ln_b', 'delta_od_w_s', 'delta_od_b_s', 'delta_od_w_out', 'delta_xa_w_q', 'delta_xa_w_k', 'delta_xa_w_v', 'delta_xa_w_o', 'delta_ffn_w_gate', 'delta_ffn_w_up', 'delta_ffn_w_down', 'new_m_g_mix', 'new_m_g_xattn', 'new_m_g_mem', 'new_m_g_ffn', 'new_m_g_final', 'new_m_ev_w_in', 'new_m_ev_a_conv_w', 'new_m_ev_a_conv_b', 'new_m_ev_a_ln_g', 'new_m_ev_a_ln_b', 'new_m_ev_b_conv_w', 'new_m_ev_b_conv_b', 'new_m_ev_w_out', 'new_m_od_w_in', 'new_m_od_c_ln_g', 'new_m_od_c_ln_b', 'new_m_od_w_s', 'new_m_od_b_s', 'new_m_od_w_out', 'new_m_xa_w_q', 'new_m_xa_w_k', 'new_m_xa_w_v', 'new_m_xa_w_o', 'new_m_ffn_w_gate', 'new_m_ffn_w_up', 'new_m_ffn_w_down', 'new_v_g_mix', 'new_v_g_xattn', 'new_v_g_mem', 'new_v_g_ffn', 'new_v_g_final', 'new_v_ev_w_in', 'new_v_ev_a_conv_w', 'new_v_ev_a_conv_b', 'new_v_ev_a_ln_g', 'new_v_ev_a_ln_b', 'new_v_ev_b_conv_w', 'new_v_ev_b_conv_b', 'new_v_ev_w_out', 'new_v_od_w_in', 'new_v_od_c_ln_g', 'new_v_od_c_ln_b', 'new_v_od_w_s', 'new_v_od_b_s', 'new_v_od_w_out', 'new_v_xa_w_q', 'new_v_xa_w_k', 'new_v_xa_w_v', 'new_v_xa_w_o', 'new_v_ffn_w_gate', 'new_v_ffn_w_up', 'new_v_ffn_w_down']
TWIN_LEAF_KINDS = {'loss': 'loss', 'grad_x': 'grad_x', 'grad_g_mix': 'grad_w', 'grad_g_xattn': 'grad_w', 'grad_g_mem': 'grad_w', 'grad_g_ffn': 'grad_w', 'grad_g_final': 'grad_w', 'grad_ev_w_in': 'grad_w', 'grad_ev_a_conv_w': 'grad_w', 'grad_ev_a_conv_b': 'grad_w', 'grad_ev_a_ln_g': 'grad_w', 'grad_ev_a_ln_b': 'grad_w', 'grad_ev_b_conv_w': 'grad_w', 'grad_ev_b_conv_b': 'grad_w', 'grad_ev_w_out': 'grad_w', 'grad_od_w_in': 'grad_w', 'grad_od_c_ln_g': 'grad_w', 'grad_od_c_ln_b': 'grad_w', 'grad_od_w_s': 'grad_w', 'grad_od_b_s': 'grad_w', 'grad_od_w_out': 'grad_w', 'grad_xa_w_q': 'grad_w', 'grad_xa_w_k': 'grad_w', 'grad_xa_w_v': 'grad_w', 'grad_xa_w_o': 'grad_w', 'grad_ffn_w_gate': 'grad_w', 'grad_ffn_w_up': 'grad_w', 'grad_ffn_w_down': 'grad_w', 'delta_g_mix': 'delta_w', 'delta_g_xattn': 'delta_w', 'delta_g_mem': 'delta_w', 'delta_g_ffn': 'delta_w', 'delta_g_final': 'delta_w', 'delta_ev_w_in': 'delta_w', 'delta_ev_a_conv_w': 'delta_w', 'delta_ev_a_conv_b': 'delta_w', 'delta_ev_a_ln_g': 'delta_w', 'delta_ev_a_ln_b': 'delta_w', 'delta_ev_b_conv_w': 'delta_w', 'delta_ev_b_conv_b': 'delta_w', 'delta_ev_w_out': 'delta_w', 'delta_od_w_in': 'delta_w', 'delta_od_c_ln_g': 'delta_w', 'delta_od_c_ln_b': 'delta_w', 'delta_od_w_s': 'delta_w', 'delta_od_b_s': 'delta_w', 'delta_od_w_out': 'delta_w', 'delta_xa_w_q': 'delta_w', 'delta_xa_w_k': 'delta_w', 'delta_xa_w_v': 'delta_w', 'delta_xa_w_o': 'delta_w', 'delta_ffn_w_gate': 'delta_w', 'delta_ffn_w_up': 'delta_w', 'delta_ffn_w_down': 'delta_w', 'new_m_g_mix': 'new_m', 'new_m_g_xattn': 'new_m', 'new_m_g_mem': 'new_m', 'new_m_g_ffn': 'new_m', 'new_m_g_final': 'new_m', 'new_m_ev_w_in': 'new_m', 'new_m_ev_a_conv_w': 'new_m', 'new_m_ev_a_conv_b': 'new_m', 'new_m_ev_a_ln_g': 'new_m', 'new_m_ev_a_ln_b': 'new_m', 'new_m_ev_b_conv_w': 'new_m', 'new_m_ev_b_conv_b': 'new_m', 'new_m_ev_w_out': 'new_m', 'new_m_od_w_in': 'new_m', 'new_m_od_c_ln_g': 'new_m', 'new_m_od_c_ln_b': 'new_m', 'new_m_od_w_s': 'new_m', 'new_m_od_b_s': 'new_m', 'new_m_od_w_out': 'new_m', 'new_m_xa_w_q': 'new_m', 'new_m_xa_w_k': 'new_m', 'new_m_xa_w_v': 'new_m', 'new_m_xa_w_o': 'new_m', 'new_m_ffn_w_gate': 'new_m', 'new_m_ffn_w_up': 'new_m', 'new_m_ffn_w_down': 'new_m', 'new_v_g_mix': 'new_v', 'new_v_g_xattn': 'new_v', 'new_v_g_mem': 'new_v', 'new_v_g_ffn': 'new_v', 'new_v_g_final': 'new_v', 'new_v_ev_w_in': 'new_v', 'new_v_ev_a_conv_w': 'new_v', 'new_v_ev_a_conv_b': 'new_v', 'new_v_ev_a_ln_g': 'new_v', 'new_v_ev_a_ln_b': 'new_v', 'new_v_ev_b_conv_w': 'new_v', 'new_v_ev_b_conv_b': 'new_v', 'new_v_ev_w_out': 'new_v', 'new_v_od_w_in': 'new_v', 'new_v_od_c_ln_g': 'new_v', 'new_v_od_c_ln_b': 'new_v', 'new_v_od_w_s': 'new_v', 'new_v_od_b_s': 'new_v', 'new_v_od_w_out': 'new_v', 'new_v_xa_w_q': 'new_v', 'new_v_xa_w_k': 'new_v', 'new_v_xa_w_v': 'new_v', 'new_v_xa_w_o': 'new_v', 'new_v_ffn_w_gate': 'new_v', 'new_v_ffn_w_up': 'new_v', 'new_v_ffn_w_down': 'new_v'}


def _forward(args):
    return _fwd_reference(*[args[k] for k in FWD_PARAMS])


def _output_shape():
    def fwd():
        inp = _fwd_setup_inputs(0)
        return _fwd_reference(*[inp[k] for k in FWD_PARAMS])
    out = _jax.eval_shape(fwd)
    return out.shape, out.dtype

N_MICROBATCH = 1
ADAM_LR = 0.001
ADAM_B1 = 0.9
ADAM_B2 = 0.999
ADAM_EPS = 1e-08
ADAM_WD = 0.01
ADAM_STEP = 10
PER_EXAMPLE_BATCH_AXIS = {'x': 0, 'mem': 0, 'loss_target': 0}
SHARED_INPUTS = []
_WEIGHT_DTYPES = {'g_mix': _jnp.float32, 'g_xattn': _jnp.float32, 'g_mem': _jnp.float32, 'g_ffn': _jnp.float32, 'g_final': _jnp.float32, 'ev_w_in': _jnp.float32, 'ev_a_conv_w': _jnp.float32, 'ev_a_conv_b': _jnp.float32, 'ev_a_ln_g': _jnp.float32, 'ev_a_ln_b': _jnp.float32, 'ev_b_conv_w': _jnp.float32, 'ev_b_conv_b': _jnp.float32, 'ev_w_out': _jnp.float32, 'od_w_in': _jnp.float32, 'od_c_ln_g': _jnp.float32, 'od_c_ln_b': _jnp.float32, 'od_w_s': _jnp.float32, 'od_b_s': _jnp.float32, 'od_w_out': _jnp.float32, 'xa_w_q': _jnp.float32, 'xa_w_k': _jnp.float32, 'xa_w_v': _jnp.float32, 'xa_w_o': _jnp.float32, 'ffn_w_gate': _jnp.float32, 'ffn_w_up': _jnp.float32, 'ffn_w_down': _jnp.float32}
MOMENT_SCALE = {'g_mix': 2.783349e-01, 'g_xattn': 2.285098e-02, 'g_mem': 4.134969e-02, 'g_ffn': 1.663350e-01, 'g_final': 6.431559e+01, 'ev_w_in': 2.100282e-01, 'ev_a_conv_w': 1.669349e-01, 'ev_a_conv_b': 9.380304e-01, 'ev_a_ln_g': 3.701588e-01, 'ev_a_ln_b': 5.330939e-01, 'ev_b_conv_w': 2.792381e-01, 'ev_b_conv_b': 3.031321e-01, 'ev_w_out': 2.438129e-01, 'od_w_in': 1.451364e-01, 'od_c_ln_g': 1.104338e-01, 'od_c_ln_b': 1.068902e-01, 'od_w_s': 1.041357e-01, 'od_b_s': 1.082415e-01, 'od_w_out': 2.791607e-01, 'xa_w_q': 2.306231e-02, 'xa_w_k': 2.318377e-02, 'xa_w_v': 2.818203e-02, 'xa_w_o': 2.833706e-02, 'ffn_w_gate': 6.803652e-02, 'ffn_w_up': 6.892955e-02, 'ffn_w_down': 1.153156e-01}


def _to_microbatches(a, axis):
    t = _jnp.moveaxis(a, axis, 0)
    t = t.reshape((N_MICROBATCH, t.shape[0] // N_MICROBATCH) + t.shape[1:])
    return _jnp.moveaxis(t, 1, axis + 1)


def setup_inputs(seed: int = 0) -> dict:
    inp = _fwd_setup_inputs(seed)
    key = _jax.random.fold_in(_jax.random.key(seed), 7919)
    shape, _ = _output_shape()
    out = dict(inp)
    out["loss_target"] = _jax.random.normal(_jax.random.fold_in(key, 0), shape, _jnp.float32)
    for i, name in enumerate(TWIN_WEIGHTS):
        w = inp[name].astype(_jnp.float32)
        if MOMENT_SCALE is None:
            s = _jnp.sqrt(_jnp.mean(_jnp.square(w)) + 1e-30)
        else:
            s = MOMENT_SCALE[name]
        km, kv = _jax.random.split(_jax.random.fold_in(key, i + 1))
        out[name] = w
        out["m_" + name] = s * _jax.random.normal(km, w.shape, _jnp.float32)
        out["v_" + name] = (s * s) * _jax.random.uniform(kv, w.shape, _jnp.float32, 0.5, 1.5)
    if N_MICROBATCH > 1:
        for name, axis in PER_EXAMPLE_BATCH_AXIS.items():
            out[name] = _to_microbatches(out[name], axis)
    return {'x': out['x'], 'mem': out['mem'], 'g_mix': out['g_mix'], 'g_xattn': out['g_xattn'], 'g_mem': out['g_mem'], 'g_ffn': out['g_ffn'], 'g_final': out['g_final'], 'ev_w_in': out['ev_w_in'], 'ev_a_conv_w': out['ev_a_conv_w'], 'ev_a_conv_b': out['ev_a_conv_b'], 'ev_a_ln_g': out['ev_a_ln_g'], 'ev_a_ln_b': out['ev_a_ln_b'], 'ev_b_conv_w': out['ev_b_conv_w'], 'ev_b_conv_b': out['ev_b_conv_b'], 'ev_w_out': out['ev_w_out'], 'od_w_in': out['od_w_in'], 'od_c_ln_g': out['od_c_ln_g'], 'od_c_ln_b': out['od_c_ln_b'], 'od_w_s': out['od_w_s'], 'od_b_s': out['od_b_s'], 'od_w_out': out['od_w_out'], 'xa_w_q': out['xa_w_q'], 'xa_w_k': out['xa_w_k'], 'xa_w_v': out['xa_w_v'], 'xa_w_o': out['xa_w_o'], 'ffn_w_gate': out['ffn_w_gate'], 'ffn_w_up': out['ffn_w_up'], 'ffn_w_down': out['ffn_w_down'], 'loss_target': out['loss_target'], 'm_g_mix': out['m_g_mix'], 'm_g_xattn': out['m_g_xattn'], 'm_g_mem': out['m_g_mem'], 'm_g_ffn': out['m_g_ffn'], 'm_g_final': out['m_g_final'], 'm_ev_w_in': out['m_ev_w_in'], 'm_ev_a_conv_w': out['m_ev_a_conv_w'], 'm_ev_a_conv_b': out['m_ev_a_conv_b'], 'm_ev_a_ln_g': out['m_ev_a_ln_g'], 'm_ev_a_ln_b': out['m_ev_a_ln_b'], 'm_ev_b_conv_w': out['m_ev_b_conv_w'], 'm_ev_b_conv_b': out['m_ev_b_conv_b'], 'm_ev_w_out': out['m_ev_w_out'], 'm_od_w_in': out['m_od_w_in'], 'm_od_c_ln_g': out['m_od_c_ln_g'], 'm_od_c_ln_b': out['m_od_c_ln_b'], 'm_od_w_s': out['m_od_w_s'], 'm_od_b_s': out['m_od_b_s'], 'm_od_w_out': out['m_od_w_out'], 'm_xa_w_q': out['m_xa_w_q'], 'm_xa_w_k': out['m_xa_w_k'], 'm_xa_w_v': out['m_xa_w_v'], 'm_xa_w_o': out['m_xa_w_o'], 'm_ffn_w_gate': out['m_ffn_w_gate'], 'm_ffn_w_up': out['m_ffn_w_up'], 'm_ffn_w_down': out['m_ffn_w_down'], 'v_g_mix': out['v_g_mix'], 'v_g_xattn': out['v_g_xattn'], 'v_g_mem': out['v_g_mem'], 'v_g_ffn': out['v_g_ffn'], 'v_g_final': out['v_g_final'], 'v_ev_w_in': out['v_ev_w_in'], 'v_ev_a_conv_w': out['v_ev_a_conv_w'], 'v_ev_a_conv_b': out['v_ev_a_conv_b'], 'v_ev_a_ln_g': out['v_ev_a_ln_g'], 'v_ev_a_ln_b': out['v_ev_a_ln_b'], 'v_ev_b_conv_w': out['v_ev_b_conv_w'], 'v_ev_b_conv_b': out['v_ev_b_conv_b'], 'v_ev_w_out': out['v_ev_w_out'], 'v_od_w_in': out['v_od_w_in'], 'v_od_c_ln_g': out['v_od_c_ln_g'], 'v_od_c_ln_b': out['v_od_c_ln_b'], 'v_od_w_s': out['v_od_w_s'], 'v_od_b_s': out['v_od_b_s'], 'v_od_w_out': out['v_od_w_out'], 'v_xa_w_q': out['v_xa_w_q'], 'v_xa_w_k': out['v_xa_w_k'], 'v_xa_w_v': out['v_xa_w_v'], 'v_xa_w_o': out['v_xa_w_o'], 'v_ffn_w_gate': out['v_ffn_w_gate'], 'v_ffn_w_up': out['v_ffn_w_up'], 'v_ffn_w_down': out['v_ffn_w_down']}


def _loss(weights, diff, rest, loss_target):
    with _jax.named_scope("forward"):
        args = {**rest, TWIN_DIFF_INPUT: diff, **{k: w.astype(_WEIGHT_DTYPES[k]) for k, w in weights.items()}}
        y = _forward(args)
    with _jax.named_scope("loss_head"):
        err = _jnp.square(y.astype(_jnp.float32) - loss_target)
        return 0.5 * _jnp.sum(_jnp.mean(err, axis=-1)) if err.ndim else 0.5 * err


def _adamw(w, g, m, v):
    m = ADAM_B1 * m + (1.0 - ADAM_B1) * g
    v = ADAM_B2 * v + (1.0 - ADAM_B2) * _jnp.square(g)
    m_hat = m / (1.0 - ADAM_B1 ** ADAM_STEP)
    v_hat = v / (1.0 - ADAM_B2 ** ADAM_STEP)
    delta = -ADAM_LR * (m_hat / (_jnp.sqrt(v_hat) + ADAM_EPS) + ADAM_WD * w)
    return delta, m, v


def reference(x, mem, g_mix, g_xattn, g_mem, g_ffn, g_final, ev_w_in, ev_a_conv_w, ev_a_conv_b, ev_a_ln_g, ev_a_ln_b, ev_b_conv_w, ev_b_conv_b, ev_w_out, od_w_in, od_c_ln_g, od_c_ln_b, od_w_s, od_b_s, od_w_out, xa_w_q, xa_w_k, xa_w_v, xa_w_o, ffn_w_gate, ffn_w_up, ffn_w_down, loss_target, m_g_mix, m_g_xattn, m_g_mem, m_g_ffn, m_g_final, m_ev_w_in, m_ev_a_conv_w, m_ev_a_conv_b, m_ev_a_ln_g, m_ev_a_ln_b, m_ev_b_conv_w, m_ev_b_conv_b, m_ev_w_out, m_od_w_in, m_od_c_ln_g, m_od_c_ln_b, m_od_w_s, m_od_b_s, m_od_w_out, m_xa_w_q, m_xa_w_k, m_xa_w_v, m_xa_w_o, m_ffn_w_gate, m_ffn_w_up, m_ffn_w_down, v_g_mix, v_g_xattn, v_g_mem, v_g_ffn, v_g_final, v_ev_w_in, v_ev_a_conv_w, v_ev_a_conv_b, v_ev_a_ln_g, v_ev_a_ln_b, v_ev_b_conv_w, v_ev_b_conv_b, v_ev_w_out, v_od_w_in, v_od_c_ln_g, v_od_c_ln_b, v_od_w_s, v_od_b_s, v_od_w_out, v_xa_w_q, v_xa_w_k, v_xa_w_v, v_xa_w_o, v_ffn_w_gate, v_ffn_w_up, v_ffn_w_down):
    given = dict(x=x, mem=mem, g_mix=g_mix, g_xattn=g_xattn, g_mem=g_mem, g_ffn=g_ffn, g_final=g_final, ev_w_in=ev_w_in, ev_a_conv_w=ev_a_conv_w, ev_a_conv_b=ev_a_conv_b, ev_a_ln_g=ev_a_ln_g, ev_a_ln_b=ev_a_ln_b, ev_b_conv_w=ev_b_conv_w, ev_b_conv_b=ev_b_conv_b, ev_w_out=ev_w_out, od_w_in=od_w_in, od_c_ln_g=od_c_ln_g, od_c_ln_b=od_c_ln_b, od_w_s=od_w_s, od_b_s=od_b_s, od_w_out=od_w_out, xa_w_q=xa_w_q, xa_w_k=xa_w_k, xa_w_v=xa_w_v, xa_w_o=xa_w_o, ffn_w_gate=ffn_w_gate, ffn_w_up=ffn_w_up, ffn_w_down=ffn_w_down, loss_target=loss_target, m_g_mix=m_g_mix, m_g_xattn=m_g_xattn, m_g_mem=m_g_mem, m_g_ffn=m_g_ffn, m_g_final=m_g_final, m_ev_w_in=m_ev_w_in, m_ev_a_conv_w=m_ev_a_conv_w, m_ev_a_conv_b=m_ev_a_conv_b, m_ev_a_ln_g=m_ev_a_ln_g, m_ev_a_ln_b=m_ev_a_ln_b, m_ev_b_conv_w=m_ev_b_conv_w, m_ev_b_conv_b=m_ev_b_conv_b, m_ev_w_out=m_ev_w_out, m_od_w_in=m_od_w_in, m_od_c_ln_g=m_od_c_ln_g, m_od_c_ln_b=m_od_c_ln_b, m_od_w_s=m_od_w_s, m_od_b_s=m_od_b_s, m_od_w_out=m_od_w_out, m_xa_w_q=m_xa_w_q, m_xa_w_k=m_xa_w_k, m_xa_w_v=m_xa_w_v, m_xa_w_o=m_xa_w_o, m_ffn_w_gate=m_ffn_w_gate, m_ffn_w_up=m_ffn_w_up, m_ffn_w_down=m_ffn_w_down, v_g_mix=v_g_mix, v_g_xattn=v_g_xattn, v_g_mem=v_g_mem, v_g_ffn=v_g_ffn, v_g_final=v_g_final, v_ev_w_in=v_ev_w_in, v_ev_a_conv_w=v_ev_a_conv_w, v_ev_a_conv_b=v_ev_a_conv_b, v_ev_a_ln_g=v_ev_a_ln_g, v_ev_a_ln_b=v_ev_a_ln_b, v_ev_b_conv_w=v_ev_b_conv_w, v_ev_b_conv_b=v_ev_b_conv_b, v_ev_w_out=v_ev_w_out, v_od_w_in=v_od_w_in, v_od_c_ln_g=v_od_c_ln_g, v_od_c_ln_b=v_od_c_ln_b, v_od_w_s=v_od_w_s, v_od_b_s=v_od_b_s, v_od_w_out=v_od_w_out, v_xa_w_q=v_xa_w_q, v_xa_w_k=v_xa_w_k, v_xa_w_v=v_xa_w_v, v_xa_w_o=v_xa_w_o, v_ffn_w_gate=v_ffn_w_gate, v_ffn_w_up=v_ffn_w_up, v_ffn_w_down=v_ffn_w_down)
    weights = {n: given[n] for n in TWIN_WEIGHTS}
    shared = {n: given[n] for n in SHARED_INPUTS}
    per_example = {n: given[n] for n in ['x', 'mem']}
    grad_fn = _jax.value_and_grad(_loss, argnums=(0, 1))

    def one_microbatch(ex, loss_target):
        ex = dict(ex)
        diff = ex.pop(TWIN_DIFF_INPUT)
        return grad_fn(weights, diff, {**shared, **ex}, loss_target)

    if N_MICROBATCH == 1:
        loss, (grad_w, grad_x) = one_microbatch(per_example, given["loss_target"])
    else:
        def body(carry, xs):
            loss_sum, grad_sum = carry
            l_k, (gw_k, gx_k) = one_microbatch(xs[0], xs[1])
            with _jax.named_scope("update"):
                return (loss_sum + l_k, _jax.tree.map(_jnp.add, grad_sum, gw_k)), gx_k

        init = (_jnp.zeros((), _jnp.float32), _jax.tree.map(_jnp.zeros_like, weights))
        (loss, grad_w), grad_x = _jax.lax.scan(body, init, (per_example, given["loss_target"]))
    with _jax.named_scope("update"):
        delta_w, new_m, new_v = {}, {}, {}
        for n in TWIN_WEIGHTS:
            delta_w[n], new_m[n], new_v[n] = _adamw(weights[n], grad_w[n], given["m_" + n], given["v_" + n])
    return (loss, grad_x, *[grad_w[n] for n in TWIN_WEIGHTS], *[delta_w[n] for n in TWIN_WEIGHTS],
            *[new_m[n] for n in TWIN_WEIGHTS], *[new_v[n] for n in TWIN_WEIGHTS])
```

```python
import jax
import jax.numpy as jnp
from jax import lax
from jax.experimental import pallas as pl
from jax.experimental.pallas import tpu as pltpu

BF = jnp.bfloat16
F32 = jnp.float32

D = 1024
DA = 512
DZ = 2560
DFF = 2816
NSH = 4
FSH = DFF // NSH
FH = DFF // 2
FPAD = 768
NMEM = 256
HEADS = 4
HD = D // HEADS
CHUNK = 128
GROUPS = 8
KA = 31
KB = 3
HALO = 16
RC = 32
RMS_EPS = 1e-6
LN_EPS = 1e-5
ATT_SCALE = HD ** -0.5
TM_FWD = 512
TM_BWD = 256

ADAM_LR = 0.001
ADAM_B1 = 0.9
ADAM_B2 = 0.999
ADAM_EPS = 1e-08
ADAM_WD = 0.01
ADAM_STEP = 10

ROW_ITEM0 = 1536
ROW_ITEMS = ("xa_w_q0", "xa_w_o0", "xa_w_q1", "xa_w_o1", "xa_w_k0", "xa_w_v0", "xa_w_k1", "xa_w_v1",
             "ev_w_out", "od_w_out")
ROW_TOTAL = ROW_ITEM0 + 256 * len(ROW_ITEMS)
COL_OD = 4 * FPAD
COL_EV = 3840
COL_TOTAL = COL_EV + 640
SP_ROWS = 48
SA_ROWS = 64

MESH = pl.DeviceIdType.MESH
ANY = pl.BlockSpec(memory_space=pl.ANY)


def _row_item(name):
    return ROW_ITEM0 // 256 + ROW_ITEMS.index(name)


def _row_pair(name):
    k = _row_item(name)
    assert k % 2 == 0
    return k // 2


def _dot(a, b):
    return jnp.dot(a, b, preferred_element_type=F32)


def _dot_nt(a, b):
    return lax.dot_general(a, b, (((1,), (1,)), ((), ())), preferred_element_type=F32)


def _dot_tn(a, b):
    return lax.dot_general(a, b, (((0,), (0,)), ((), ())), preferred_element_type=F32)


def _rms(h):
    r = lax.rsqrt(jnp.mean(h * h, axis=-1, keepdims=True) + RMS_EPS)
    return h * r, r


def _rms_bwd(dn, hhat, r, g):
    dhh = dn * g
    return r * (dhh - hhat * jnp.mean(dhh * hhat, axis=-1, keepdims=True))


def _ln(x):
    mu = jnp.mean(x, axis=-1, keepdims=True)
    xc = x - mu
    rstd = lax.rsqrt(jnp.mean(xc * xc, axis=-1, keepdims=True) + LN_EPS)
    return xc * rstd, rstd


def _ln_bwd(dy, y, rstd):
    return rstd * (dy - jnp.mean(dy, axis=-1, keepdims=True) - y * jnp.mean(dy * y, axis=-1, keepdims=True))


def _colsum(x):
    return jnp.sum(x, axis=0, keepdims=True)


def _sigmoid(x):
    return 1.0 / (1.0 + jnp.exp(-x))


_GELU_C = 0.7978845608028654
_GELU_A = 0.044715


def _gelu(x):
    t = jnp.tanh(_GELU_C * (x + _GELU_A * x * x * x))
    return 0.5 * x * (1.0 + t)


def _gelu_grad(x):
    t = jnp.tanh(_GELU_C * (x + _GELU_A * x * x * x))
    return 0.5 * (1.0 + t) + 0.5 * x * (1.0 - t * t) * _GELU_C * (1.0 + 3.0 * _GELU_A * x * x)


def _first(i):
    return i == 0


def _last(i):
    return i == pl.num_programs(0) - 1


def _const(shape, idx):
    return pl.BlockSpec(shape, lambda *_, _idx=tuple(idx): _idx, pipeline_mode=pl.Buffered(1))


def _rows(tm, width, col=0):
    return pl.BlockSpec((tm, width), lambda i, _c=col: (i, _c))


def _params(ndim=1, vmem_mb=56):
    return pltpu.CompilerParams(dimension_semantics=("arbitrary",) * ndim, vmem_limit_bytes=vmem_mb << 20)


def _sds(shape, dtype):
    return jax.ShapeDtypeStruct(shape, dtype)


def _ev_in_fwd(h, g, wev, tm):
    s = h.shape[0]

    def body(h_ref, g_ref, w_ref, z_ref):
        hhat, _ = _rms(h_ref[...])
        n = (hhat * g_ref[...]).astype(BF)
        z_ref[...] = _dot(n, w_ref[...]).astype(BF)

    return pl.pallas_call(
        body, name="ev_in_fwd", grid=(s // tm,),
        in_specs=[_rows(tm, D), _const((1, D), (0, 0)), _const((D, DZ), (0, 0))],
        out_specs=_rows(tm, DZ),
        out_shape=_sds((s, DZ), BF),
        compiler_params=_params(),
    )(h, g, wev)


def _halo_specs(tm, width, s, col=0):
    per = tm // HALO
    nh = s // HALO
    prev = pl.BlockSpec((HALO, width), lambda i, _c=col: (jnp.maximum(i * per - 1, 0), _c))
    nxt = pl.BlockSpec((HALO, width), lambda i, _c=col: (jnp.minimum((i + 1) * per, nh - 1), _c))
    return prev, nxt


def _fill_pad(pad_ref, prev, main, nxt, i, tm):
    pad_ref[pl.ds(0, HALO), :] = jnp.where(_first(i), 0.0, prev)
    pad_ref[pl.ds(HALO, tm), :] = main
    pad_ref[pl.ds(HALO + tm, HALO), :] = jnp.where(_last(i), 0.0, nxt)


def _glu_a(z):
    return z[:, :DA] * _sigmoid(z[:, DA:2 * DA])


def _gate_c(z):
    return z[:, 2048:] * z[:, 1024:1536]


def _ev_mix_fwd(h, z, wa, ba, lng, lnb, wb, bb, gr, tm):
    s = h.shape[0]

    def body(h_ref, z_ref, zp_ref, zn_ref, wa_ref, ba_ref, lng_ref, lnb_ref, wb_ref, bb_ref, wo_ref,
             h1_ref, a2_ref, c2_ref, cat_ref, apad, cpad, c2f):
        i = pl.program_id(0)
        zm = z_ref[...].astype(F32)
        zp = zp_ref[...].astype(F32)
        zn = zn_ref[...].astype(F32)
        _fill_pad(apad, _glu_a(zp), _glu_a(zm), _glu_a(zn), i, tm)
        _fill_pad(cpad, _gate_c(zp), _gate_c(zm), _gate_c(zn), i, tm)
        for r in range(0, tm, RC):
            acc = jnp.zeros((RC, DA), F32) + ba_ref[...]
            for k in range(KA):
                acc = acc + apad[pl.ds(r + k + 1, RC), :] * wa_ref[pl.ds(k, 1), :]
            a2_ref[pl.ds(r, RC), :] = acc
            acc = jnp.zeros((RC, DA), F32) + bb_ref[...]
            for k in range(KB):
                acc = acc + cpad[pl.ds(r + HALO - 1 + k, RC), :] * wb_ref[pl.ds(k, 1), :]
            c2f[pl.ds(r, RC), :] = acc
        y, _ = _ln(a2_ref[...])
        a3 = y * lng_ref[...] + lnb_ref[...]
        cat_ref[:, :DA] = (a3 * _sigmoid(a3)).astype(BF)
        c2 = c2f[...]
        c2_ref[...] = c2.astype(BF)
        cat_ref[:, DA:] = (zm[:, 1536:2048] * c2).astype(BF)
        h1_ref[...] = h_ref[...] + _dot(cat_ref[...], wo_ref[...].reshape(D, D))

    zp_spec, zn_spec = _halo_specs(tm, DZ, s)
    small = lambda rows: _const((rows, DA), (0, 0))
    return pl.pallas_call(
        body, name="ev_mix_fwd", grid=(s // tm,),
        in_specs=[_rows(tm, D), _rows(tm, DZ), zp_spec, zn_spec, small(32), small(1), small(1), small(1),
                  small(8), small(1), _const((NSH, 256, D), (0, _row_item("ev_w_out"), 0))],
        out_specs=[_rows(tm, D), _rows(tm, DA), _rows(tm, DA), _rows(tm, D)],
        out_shape=[_sds((s, D), F32), _sds((s, DA), F32), _sds((s, DA), BF), _sds((s, D), BF)],
        scratch_shapes=[pltpu.VMEM((tm + 2 * HALO, DA), F32), pltpu.VMEM((tm + 2 * HALO, DA), F32),
                        pltpu.VMEM((tm, DA), F32)],
        compiler_params=_params(),
    )(h, z, z, z, wa, ba, lng, lnb, wb, bb, gr)


def _ev_mix_bwd(dh, cat, a2, c2, z, lng, lnb, gr, gpr, tm):
    s = dh.shape[0]

    def body(dh_ref, cat_ref, a2_ref, c2_ref, zgb_ref, lng_ref, lnb_ref, wo_ref, gpr_in,
             da2_ref, dc2_ref, dgb_ref, dlng_ref, dlnb_ref, gpr_ref, wacc):
        i = pl.program_id(0)

        @pl.when(_first(i))
        def _():
            wacc[...] = jnp.zeros_like(wacc)
            dlng_ref[...] = jnp.zeros_like(dlng_ref)
            dlnb_ref[...] = jnp.zeros_like(dlnb_ref)

        dhb = dh_ref[...].astype(BF)
        dcat = _dot_nt(dhb, wo_ref[...].reshape(D, D))
        wacc[...] += _dot_tn(cat_ref[...], dhb)
        y, rstd = _ln(a2_ref[...])
        a3 = y * lng_ref[...] + lnb_ref[...]
        sg = _sigmoid(a3)
        da3 = dcat[:, :DA] * (sg * (1.0 + a3 * (1.0 - sg)))
        dlng_ref[...] += _colsum(da3 * y)
        dlnb_ref[...] += _colsum(da3)
        da2_ref[...] = _ln_bwd(da3 * lng_ref[...], y, rstd).astype(BF)
        db = dcat[:, DA:]
        dc2_ref[...] = (db * zgb_ref[...].astype(F32)).astype(BF)
        dgb_ref[...] = (db * c2_ref[...].astype(F32)).astype(BF)

        @pl.when(_last(i))
        def _():
            gpr_ref[...] = wacc[...].astype(BF).reshape(NSH, 256, D)

    small = _const((1, DA), (0, 0))
    item = _row_item("ev_w_out")
    return pl.pallas_call(
        body, name="ev_mix_bwd", grid=(s // tm,),
        in_specs=[_rows(tm, D), _rows(tm, D), _rows(tm, DA), _rows(tm, DA), _rows(tm, DA, 3), small, small,
                  _const((NSH, 256, D), (0, item, 0)), ANY],
        out_specs=[_rows(tm, DA), _rows(tm, DA), _rows(tm, DA), small, small,
                   _const((NSH, 256, D), (0, item, 0))],
        out_shape=[_sds((s, DA), BF), _sds((s, DA), BF), _sds((s, DA), BF), _sds((1, DA), F32),
                   _sds((1, DA), F32), _sds(gpr.shape, BF)],
        scratch_shapes=[pltpu.VMEM((D, D), F32)],
        input_output_aliases={8: 5},
        compiler_params=_params(),
    )(dh, cat, a2, c2, z, lng, lnb, gr, gpr)


def _ev_in_bwd(h, dh, z, da2, dc2, dgb, g, wa, wb, wev, gpc, tm):
    s = h.shape[0]

    def body(h_ref, dh_ref, z_ref, zp_ref, zn_ref, da_ref, dap_ref, dan_ref, dc_ref, dcp_ref, dcn_ref,
             dgb_ref, g_ref, wa_ref, wb_ref, w_ref, gpc_in,
             dh0_ref, dwa_ref, dba_ref, dwb_ref, dbb_ref, dg_ref, gpc_ref,
             apad, cpad, dapad, dcpad, dz, wacc, dwa_acc, dwb_acc):
        i = pl.program_id(0)

        @pl.when(_first(i))
        def _():
            wacc[...] = jnp.zeros_like(wacc)
            dwa_acc[...] = jnp.zeros_like(dwa_acc)
            dwb_acc[...] = jnp.zeros_like(dwb_acc)
            dba_ref[...] = jnp.zeros_like(dba_ref)
            dbb_ref[...] = jnp.zeros_like(dbb_ref)
            dg_ref[...] = jnp.zeros_like(dg_ref)

        zm = z_ref[...].astype(F32)
        zp = zp_ref[...].astype(F32)
        zn = zn_ref[...].astype(F32)
        _fill_pad(apad, _glu_a(zp), _glu_a(zm), _glu_a(zn), i, tm)
        _fill_pad(cpad, _gate_c(zp), _gate_c(zm), _gate_c(zn), i, tm)
        da2 = da_ref[...].astype(F32)
        dc2 = dc_ref[...].astype(F32)
        _fill_pad(dapad, dap_ref[...].astype(F32), da2, dan_ref[...].astype(F32), i, tm)
        _fill_pad(dcpad, dcp_ref[...].astype(F32), dc2, dcn_ref[...].astype(F32), i, tm)
        dba_ref[...] += _colsum(da2)
        dbb_ref[...] += _colsum(dc2)

        sg = _sigmoid(zm[:, DA:2 * DA])
        aval = zm[:, :DA]
        for r in range(0, tm, RC):
            acc = jnp.zeros((RC, DA), F32)
            for k in range(KA):
                acc = acc + dapad[pl.ds(r + 2 * HALO - 1 - k, RC), :] * wa_ref[pl.ds(k, 1), :]
            sgr = sg[r:r + RC]
            dz[pl.ds(r, RC), 0:DA] = (acc * sgr).astype(BF)
            dz[pl.ds(r, RC), DA:2 * DA] = (acc * aval[r:r + RC] * sgr * (1.0 - sgr)).astype(BF)
            acc = jnp.zeros((RC, DA), F32)
            for k in range(KB):
                acc = acc + dcpad[pl.ds(r + HALO + 1 - k, RC), :] * wb_ref[pl.ds(k, 1), :]
            dz[pl.ds(r, RC), 1024:1536] = (acc * zm[r:r + RC, 2048:]).astype(BF)
            dz[pl.ds(r, RC), 2048:2560] = (acc * zm[r:r + RC, 1024:1536]).astype(BF)
        dz[:, 1536:2048] = dgb_ref[...]

        for k in range(KA):
            acc = jnp.zeros((8, DA), F32)
            for r in range(0, tm, RC):
                p = apad[pl.ds(r + k + 1, RC), :] * dapad[pl.ds(r + HALO, RC), :]
                acc = acc + p.reshape(RC // 8, 8, DA).sum(axis=0)
            dwa_acc[k] += acc
        for k in range(KB):
            acc = jnp.zeros((8, DA), F32)
            for r in range(0, tm, RC):
                p = cpad[pl.ds(r + HALO - 1 + k, RC), :] * dcpad[pl.ds(r + HALO, RC), :]
                acc = acc + p.reshape(RC // 8, 8, DA).sum(axis=0)
            dwb_acc[k] += acc

        gg = g_ref[...]
        hhat, rr = _rms(h_ref[...])
        n = (hhat * gg).astype(BF)
        dzb = dz[...]
        wacc[...] += _dot_tn(n, dzb)
        dn = _dot_nt(dzb, w_ref[...])
        dg_ref[...] += _colsum(dn * hhat)
        dh0_ref[...] = dh_ref[...] + _rms_bwd(dn, hhat, rr, gg)

        @pl.when(_last(i))
        def _():
            for k in range(NSH):
                gpc_ref[k] = wacc[:, 640 * k:640 * (k + 1)].astype(BF)
            dwa_ref[...] = jnp.sum(dwa_acc[...], axis=1)
            dwb_ref[...] = jnp.sum(dwb_acc[...], axis=1)

    zp_spec, zn_spec = _halo_specs(tm, DZ, s)
    hp_spec, hn_spec = _halo_specs(tm, DA, s)
    small = lambda rows: _const((rows, DA), (0, 0))
    wspec = _const((NSH, D, 640), (0, 0, COL_EV // 640))
    return pl.pallas_call(
        body, name="ev_in_bwd", grid=(s // tm,),
        in_specs=[_rows(tm, D), _rows(tm, D), _rows(tm, DZ), zp_spec, zn_spec,
                  _rows(tm, DA), hp_spec, hn_spec, _rows(tm, DA), hp_spec, hn_spec, _rows(tm, DA),
                  _const((1, D), (0, 0)), small(32), small(8), _const((D, DZ), (0, 0)), ANY],
        out_specs=[_rows(tm, D), small(32), small(1), small(8), small(1), _const((1, D), (0, 0)), wspec],
        out_shape=[_sds((s, D), F32), _sds((32, DA), F32), _sds((1, DA), F32), _sds((8, DA), F32),
                   _sds((1, DA), F32), _sds((1, D), F32), _sds(gpc.shape, BF)],
        scratch_shapes=[pltpu.VMEM((tm + 2 * HALO, DA), F32)] * 4 + [
            pltpu.VMEM((tm, DZ), BF), pltpu.VMEM((D, DZ), F32),
            pltpu.VMEM((32, 8, DA), F32), pltpu.VMEM((8, 8, DA), F32)],
        input_output_aliases={16: 6},
        compiler_params=_params(),
    )(h, dh, z, z, z, da2, da2, da2, dc2, dc2, dc2, dgb, g, wa, wb, wev, gpc)


def _full(shape):
    return pl.BlockSpec(shape, lambda i: (0,) * len(shape))


def _kv_proj(mem, g, gr, layer):
    def body(mem_ref, g_ref, wk_ref, wv_ref, k_ref, v_ref):
        mhat, _ = _rms(mem_ref[...])
        mn = (mhat * g_ref[...]).astype(BF)
        k_ref[...] = _dot(mn, wk_ref[...].reshape(D, D)).astype(BF)
        v_ref[...] = _dot(mn, wv_ref[...].reshape(D, D)).astype(BF)

    return pl.pallas_call(
        body, name=f"kv_proj{layer}", grid=(1,),
        in_specs=[_full((NMEM, D)), _full((1, D)),
                  _const((NSH, 256, D), (0, _row_item(f"xa_w_k{layer}"), 0)),
                  _const((NSH, 256, D), (0, _row_item(f"xa_w_v{layer}"), 0))],
        out_specs=[_full((NMEM, D)), _full((NMEM, D))],
        out_shape=[_sds((NMEM, D), BF), _sds((NMEM, D), BF)],
        compiler_params=_params(),
    )(mem, g, gr, gr)


def _softmax_rows(sc):
    m = jnp.max(sc, axis=-1, keepdims=True)
    e = jnp.exp(sc - m)
    return e / jnp.sum(e, axis=-1, keepdims=True)


def _xattn_fwd(h, g, k, v, gr, layer, tm):
    s = h.shape[0]

    def body(h_ref, g_ref, k_ref, v_ref, wq_ref, wo_ref, ho_ref, q_ref, o_scr):
        hh = h_ref[...]
        hhat, _ = _rms(hh)
        n = (hhat * g_ref[...]).astype(BF)
        q = _dot(n, wq_ref[...].reshape(D, D)).astype(BF)
        q_ref[...] = q
        for hd in range(HEADS):
            cs = slice(HD * hd, HD * (hd + 1))
            p = _softmax_rows(_dot_nt(q[:, cs], k_ref[:, cs]) * ATT_SCALE)
            o_scr[:, cs] = _dot(p.astype(BF), v_ref[:, cs]).astype(BF)
        ho_ref[...] = hh + _dot(o_scr[...], wo_ref[...].reshape(D, D))

    return pl.pallas_call(
        body, name=f"xattn_fwd{layer}", grid=(s // tm,),
        in_specs=[_rows(tm, D), _const((1, D), (0, 0)), _const((NMEM, D), (0, 0)), _const((NMEM, D), (0, 0)),
                  _const((NSH, 256, D), (0, _row_item(f"xa_w_q{layer}"), 0)),
                  _const((NSH, 256, D), (0, _row_item(f"xa_w_o{layer}"), 0))],
        out_specs=[_rows(tm, D), _rows(tm, D)],
        out_shape=[_sds((s, D), F32), _sds((s, D), BF)],
        scratch_shapes=[pltpu.VMEM((tm, D), BF)],
        compiler_params=_params(),
    )(h, g, k, v, gr, gr)


def _xattn_bwd(h, dh, q, g, k, v, gr, gpr, layer, tm):
    s = h.shape[0]

    def body(h_ref, dh_ref, q_ref, g_ref, k_ref, v_ref, wq_ref, wo_ref, gpr_in,
             dhi_ref, dk_ref, dv_ref, dg_ref, gqo_ref, o_scr, dq_scr, qacc, oacc):
        i = pl.program_id(0)

        @pl.when(_first(i))
        def _():
            qacc[...] = jnp.zeros_like(qacc)
            oacc[...] = jnp.zeros_like(oacc)
            dk_ref[...] = jnp.zeros_like(dk_ref)
            dv_ref[...] = jnp.zeros_like(dv_ref)
            dg_ref[...] = jnp.zeros_like(dg_ref)

        gg = g_ref[...]
        dho = dh_ref[...]
        dhb = dho.astype(BF)
        q = q_ref[...]
        do = _dot_nt(dhb, wo_ref[...].reshape(D, D)).astype(BF)
        for hd in range(HEADS):
            cs = slice(HD * hd, HD * (hd + 1))
            kh = k_ref[:, cs]
            vh = v_ref[:, cs]
            p = _softmax_rows(_dot_nt(q[:, cs], kh) * ATT_SCALE)
            pb = p.astype(BF)
            o_scr[:, cs] = _dot(pb, vh).astype(BF)
            doh = do[:, cs]
            dp = _dot_nt(doh, vh)
            dv_ref[:, cs] += _dot_tn(pb, doh)
            ds = (p * (dp - jnp.sum(dp * p, axis=-1, keepdims=True)) * ATT_SCALE).astype(BF)
            dq_scr[:, cs] = _dot(ds, kh).astype(BF)
            dk_ref[:, cs] += _dot_tn(ds, q[:, cs])
        oacc[...] += _dot_tn(o_scr[...], dhb)
        hhat, rr = _rms(h_ref[...])
        n = (hhat * gg).astype(BF)
        dq = dq_scr[...]
        qacc[...] += _dot_tn(n, dq)
        dn = _dot_nt(dq, wq_ref[...].reshape(D, D))
        dg_ref[...] += _colsum(dn * hhat)
        dhi_ref[...] = dho + _rms_bwd(dn, hhat, rr, gg)

        @pl.when(_last(i))
        def _():
            gqo_ref[:, 0:256, :] = qacc[...].astype(BF).reshape(NSH, 256, D)
            gqo_ref[:, 256:512, :] = oacc[...].astype(BF).reshape(NSH, 256, D)

    qs = _const((NSH, 256, D), (0, _row_item(f"xa_w_q{layer}"), 0))
    os_ = _const((NSH, 256, D), (0, _row_item(f"xa_w_o{layer}"), 0))
    kvs = _const((NMEM, D), (0, 0))
    return pl.pallas_call(
        body, name=f"xattn_bwd{layer}", grid=(s // tm,),
        in_specs=[_rows(tm, D), _rows(tm, D), _rows(tm, D), _const((1, D), (0, 0)), kvs, kvs, qs, os_, ANY],
        out_specs=[_rows(tm, D), kvs, kvs, _const((1, D), (0, 0)),
                   _const((NSH, 512, D), (0, _row_pair(f"xa_w_q{layer}"), 0))],
        out_shape=[_sds((s, D), F32), _sds((NMEM, D), F32), _sds((NMEM, D), F32), _sds((1, D), F32),
                   _sds(gpr.shape, BF)],
        scratch_shapes=[pltpu.VMEM((tm, D), BF), pltpu.VMEM((tm, D), BF), pltpu.VMEM((D, D), F32),
                        pltpu.VMEM((D, D), F32)],
        input_output_aliases={8: 4},
        compiler_params=_params(),
    )(h, dh, q, g, k, v, gr, gr, gpr)


def _kv_proj_bwd(mem, g, dk, dv, gr, gpr, layer):
    def body(mem_ref, g_ref, dk_ref, dv_ref, wk_ref, wv_ref, gpr_in, dg_ref, gkv_ref):
        mhat, _ = _rms(mem_ref[...])
        mn = (mhat * g_ref[...]).astype(BF)
        dkb = dk_ref[...].astype(BF)
        dvb = dv_ref[...].astype(BF)
        gkv_ref[:, 0:256, :] = _dot_tn(mn, dkb).astype(BF).reshape(NSH, 256, D)
        gkv_ref[:, 256:512, :] = _dot_tn(mn, dvb).astype(BF).reshape(NSH, 256, D)
        dmn = _dot_nt(dkb, wk_ref[...].reshape(D, D)) + _dot_nt(dvb, wv_ref[...].reshape(D, D))
        dg_ref[...] = _colsum(dmn * mhat)

    return pl.pallas_call(
        body, name=f"kv_proj_bwd{layer}", grid=(1,),
        in_specs=[_full((NMEM, D)), _full((1, D)), _full((NMEM, D)), _full((NMEM, D)),
                  _const((NSH, 256, D), (0, _row_item(f"xa_w_k{layer}"), 0)),
                  _const((NSH, 256, D), (0, _row_item(f"xa_w_v{layer}"), 0)), ANY],
        out_specs=[_full((1, D)), _const((NSH, 512, D), (0, _row_pair(f"xa_w_k{layer}"), 0))],
        out_shape=[_sds((1, D), F32), _sds(gpr.shape, BF)],
        input_output_aliases={6: 1},
        compiler_params=_params(),
    )(mem, g, dk, dv, gr, gr, gpr)


def _ffn_fwd(h, g, wg, wu, gr, layer, tm):
    s = h.shape[0]

    def body(h_ref, g_ref, wg_ref, wu_ref, wd_ref, ho_ref, gt_ref, up_ref):
        hh = h_ref[...]
        hhat, _ = _rms(hh)
        n = (hhat * g_ref[...]).astype(BF)
        gt = _dot(n, wg_ref[...])
        up = _dot(n, wu_ref[...])
        gt_ref[...] = gt.astype(BF)
        up_ref[...] = up.astype(BF)
        act = (gt * _sigmoid(gt) * up).astype(BF)
        ho_ref[...] = hh + _dot(act, wd_ref[...].reshape(DFF, D))

    return pl.pallas_call(
        body, name=f"ffn_fwd{layer}", grid=(s // tm,),
        in_specs=[_rows(tm, D), _const((1, D), (0, 0)), _const((D, DFF), (0, 0)), _const((D, DFF), (0, 0)),
                  _const((NSH, FSH, D), (0, layer, 0))],
        out_specs=[_rows(tm, D), _rows(tm, DFF), _rows(tm, DFF)],
        out_shape=[_sds((s, D), F32), _sds((s, DFF), BF), _sds((s, DFF), BF)],
        compiler_params=_params(),
    )(h, g, wg, wu, gr)


def _ffn_bwd_dx(h, dh, gt, up, g, wg, wu, gr, layer, tm):
    s = h.shape[0]

    def body(h_ref, dh_ref, gt_ref, up_ref, g_ref, wg_ref, wu_ref, wd_ref, dhi_ref, dg_ref, dgt_ref, dup_ref):
        i = pl.program_id(0)

        @pl.when(_first(i))
        def _():
            dg_ref[...] = jnp.zeros_like(dg_ref)

        gg = g_ref[...]
        dho = dh_ref[...]
        dhb = dho.astype(BF)
        for half in range(2):
            cs = slice(FH * half, FH * (half + 1))
            gtv = gt_ref[:, cs].astype(F32)
            upv = up_ref[:, cs].astype(F32)
            sg = _sigmoid(gtv)
            dact = _dot_nt(dhb, wd_ref[2 * half:2 * half + 2].reshape(FH, D))
            dgt_ref[:, cs] = (dact * upv * (sg * (1.0 + gtv * (1.0 - sg)))).astype(BF)
            dup_ref[:, cs] = (dact * gtv * sg).astype(BF)
        dn = _dot_nt(dgt_ref[...], wg_ref[...]) + _dot_nt(dup_ref[...], wu_ref[...])
        hhat, rr = _rms(h_ref[...])
        dg_ref[...] += _colsum(dn * hhat)
        dhi_ref[...] = dho + _rms_bwd(dn, hhat, rr, gg)

    return pl.pallas_call(
        body, name=f"ffn_bwd_dx{layer}", grid=(s // tm,),
        in_specs=[_rows(tm, D), _rows(tm, D), _rows(tm, DFF), _rows(tm, DFF), _const((1, D), (0, 0)),
                  _const((D, DFF), (0, 0)), _const((D, DFF), (0, 0)), _const((NSH, FSH, D), (0, layer, 0))],
        out_specs=[_rows(tm, D), _const((1, D), (0, 0)), _rows(tm, DFF), _rows(tm, DFF)],
        out_shape=[_sds((s, D), F32), _sds((1, D), F32), _sds((s, DFF), BF), _sds((s, DFF), BF)],
        compiler_params=_params(),
    )(h, dh, gt, up, g, wg, wu, gr)


def _ffn_bwd_dwgu(h, g, dgt, dup, gpc, layer, tm):
    s = h.shape[0]

    def body(h_ref, g_ref, dgt_ref, dup_ref, gpc_in, ggu_ref, gacc, uacc):
        i = pl.program_id(0)

        @pl.when(_first(i))
        def _():
            gacc[...] = jnp.zeros_like(gacc)
            uacc[...] = jnp.zeros_like(uacc)

        hhat, _ = _rms(h_ref[...])
        n = (hhat * g_ref[...]).astype(BF)
        gacc[...] += _dot_tn(n, dgt_ref[...])
        uacc[...] += _dot_tn(n, dup_ref[...])

        @pl.when(_last(i))
        def _():
            pad = jnp.zeros((D, FPAD - FSH), BF)
            for k in range(NSH):
                ggu_ref[k, :, 0:FSH] = gacc[:, FSH * k:FSH * (k + 1)].astype(BF)
                ggu_ref[k, :, FSH:FPAD] = pad
                ggu_ref[k, :, FPAD:FPAD + FSH] = uacc[:, FSH * k:FSH * (k + 1)].astype(BF)
                ggu_ref[k, :, FPAD + FSH:2 * FPAD] = pad

    return pl.pallas_call(
        body, name=f"ffn_bwd_dwgu{layer}", grid=(s // tm,),
        in_specs=[_rows(tm, D), _const((1, D), (0, 0)), _rows(tm, DFF), _rows(tm, DFF), ANY],
        out_specs=_const((NSH, D, 2 * FPAD), (0, 0, layer)),
        out_shape=_sds(gpc.shape, BF),
        scratch_shapes=[pltpu.VMEM((D, DFF), F32), pltpu.VMEM((D, DFF), F32)],
        input_output_aliases={4: 0},
        compiler_params=_params(),
    )(h, g, dgt, dup, gpc)


def _ffn_bwd_dwd(gt, up, dh, gpr, layer, tm):
    s = dh.shape[0]

    def body(gt_ref, up_ref, dh_ref, gpr_in, gd_ref, dacc):
        i = pl.program_id(0)

        @pl.when(_first(i))
        def _():
            dacc[...] = jnp.zeros_like(dacc)

        gtv = gt_ref[...].astype(F32)
        act = (gtv * _sigmoid(gtv) * up_ref[...].astype(F32)).astype(BF)
        dacc[...] += _dot_tn(act, dh_ref[...].astype(BF))

        @pl.when(_last(i))
        def _():
            gd_ref[...] = dacc[...].astype(BF).reshape(NSH, FSH, D)

    return pl.pallas_call(
        body, name=f"ffn_bwd_dwd{layer}", grid=(s // tm,),
        in_specs=[_rows(tm, DFF), _rows(tm, DFF), _rows(tm, D), ANY],
        out_specs=_const((NSH, FSH, D), (0, layer, 0)),
        out_shape=_sds(gpr.shape, BF),
        scratch_shapes=[pltpu.VMEM((DFF, D), F32)],
        input_output_aliases={3: 0},
        compiler_params=_params(),
    )(gt, up, dh, gpr)


def _sgu_fwd(h, g, lng, lnb, ws, bsb, wod, gr, tm):
    s = h.shape[0]

    def body(h_ref, g_ref, lng_ref, lnb_ref, ws_ref, bsb_ref, wi_ref, wo_ref, ho_ref, zp_ref, y_scr):
        hh = h_ref[...]
        hhat, _ = _rms(hh)
        n = (hhat * g_ref[...]).astype(BF)
        zpre = _dot(n, wi_ref[...])
        zp_ref[...] = zpre.astype(BF)
        z = _gelu(zpre)
        u = z[:, :D]
        y, _ = _ln(z[:, D:])
        vn = (y * lng_ref[...] + lnb_ref[...]).astype(BF)
        for c in range(tm // CHUNK):
            rs = slice(CHUNK * c, CHUNK * (c + 1))
            for gi in range(GROUPS):
                cs = slice(CHUNK * gi, CHUNK * (gi + 1))
                sv = _dot(ws_ref[gi], vn[rs, cs]) + bsb_ref[gi]
                y_scr[rs, cs] = (u[rs, cs] * sv).astype(BF)
        ho_ref[...] = hh + _dot(y_scr[...], wo_ref[...].reshape(D, D))

    vec = _const((1, D), (0, 0))
    sq = _const((GROUPS, CHUNK, CHUNK), (0, 0, 0))
    return pl.pallas_call(
        body, name="sgu_fwd", grid=(s // tm,),
        in_specs=[_rows(tm, D), vec, vec, vec, sq, sq, _const((D, 2 * D), (0, 0)),
                  _const((NSH, 256, D), (0, _row_item("od_w_out"), 0))],
        out_specs=[_rows(tm, D), _rows(tm, 2 * D)],
        out_shape=[_sds((s, D), F32), _sds((s, 2 * D), BF)],
        scratch_shapes=[pltpu.VMEM((tm, D), BF)],
        compiler_params=_params(),
    )(h, g, lng, lnb, ws, bsb, wod, gr)


def _sgu_bwd(h, dh, zpre, g, lng, lnb, ws, wst, bsb, wod, gr, gpc, gpr, tm):
    s = h.shape[0]

    def body(h_ref, dh_ref, zp_ref, g_ref, lng_ref, lnb_ref, ws_ref, wst_ref, bsb_ref, wi_ref, wo_ref,
             gpc_in, gpr_in,
             dhi_ref, dws_ref, dbs_ref, dlng_ref, dlnb_ref, dg_ref, gwi_ref, gwo_ref,
             y_scr, dz_scr, dvn_scr, iacc, oacc, bacc):
        i = pl.program_id(0)

        @pl.when(_first(i))
        def _():
            iacc[...] = jnp.zeros_like(iacc)
            oacc[...] = jnp.zeros_like(oacc)
            bacc[...] = jnp.zeros_like(bacc)
            dws_ref[...] = jnp.zeros_like(dws_ref)
            dlng_ref[...] = jnp.zeros_like(dlng_ref)
            dlnb_ref[...] = jnp.zeros_like(dlnb_ref)
            dg_ref[...] = jnp.zeros_like(dg_ref)

        gg = g_ref[...]
        dho = dh_ref[...]
        dhb = dho.astype(BF)
        zpre_v = zp_ref[...].astype(F32)
        z = _gelu(zpre_v)
        u = z[:, :D]
        yl, rstd = _ln(z[:, D:])
        vn = (yl * lng_ref[...] + lnb_ref[...]).astype(BF)
        dy = _dot_nt(dhb, wo_ref[...].reshape(D, D))
        for c in range(tm // CHUNK):
            rs = slice(CHUNK * c, CHUNK * (c + 1))
            for gi in range(GROUPS):
                cs = slice(CHUNK * gi, CHUNK * (gi + 1))
                vb = vn[rs, cs]
                sv = _dot(ws_ref[gi], vb) + bsb_ref[gi]
                ub = u[rs, cs]
                dyb = dy[rs, cs]
                y_scr[rs, cs] = (ub * sv).astype(BF)
                dz_scr[rs, cs] = dyb * sv
                dsv = dyb * ub
                bacc[gi] += dsv
                dsvb = dsv.astype(BF)
                dws_ref[gi] += _dot_nt(dsvb, vb)
                dvn_scr[rs, cs] = _dot(wst_ref[gi], dsvb)
        oacc[...] += _dot_tn(y_scr[...], dhb)
        dvn = dvn_scr[...]
        dlng_ref[...] += _colsum(dvn * yl)
        dlnb_ref[...] += _colsum(dvn)
        dz_scr[:, D:] = _ln_bwd(dvn * lng_ref[...], yl, rstd)
        dzb = (dz_scr[...] * _gelu_grad(zpre_v)).astype(BF)
        hhat, rr = _rms(h_ref[...])
        n = (hhat * gg).astype(BF)
        iacc[...] += _dot_tn(n, dzb)
        dn = _dot_nt(dzb, wi_ref[...])
        dg_ref[...] += _colsum(dn * hhat)
        dhi_ref[...] = dho + _rms_bwd(dn, hhat, rr, gg)

        @pl.when(_last(i))
        def _():
            for k in range(NSH):
                gwi_ref[k] = iacc[:, 512 * k:512 * (k + 1)].astype(BF)
            gwo_ref[...] = oacc[...].astype(BF).reshape(NSH, 256, D)
            dbs_ref[...] = jnp.sum(bacc[...], axis=-1)

    vec = _const((1, D), (0, 0))
    sq = _const((GROUPS, CHUNK, CHUNK), (0, 0, 0))
    wi_spec = _const((NSH, D, 512), (0, 0, COL_OD // 512))
    wo_spec = _const((NSH, 256, D), (0, _row_item("od_w_out"), 0))
    return pl.pallas_call(
        body, name="sgu_bwd", grid=(s // tm,),
        in_specs=[_rows(tm, D), _rows(tm, D), _rows(tm, 2 * D), vec, vec, vec, sq, sq, sq,
                  _const((D, 2 * D), (0, 0)), wo_spec, ANY, ANY],
        out_specs=[_rows(tm, D), sq, _const((GROUPS, CHUNK), (0, 0)), vec, vec, vec, wi_spec, wo_spec],
        out_shape=[_sds((s, D), F32), _sds((GROUPS, CHUNK, CHUNK), F32), _sds((GROUPS, CHUNK), F32),
                   _sds((1, D), F32), _sds((1, D), F32), _sds((1, D), F32), _sds(gpc.shape, BF),
                   _sds(gpr.shape, BF)],
        scratch_shapes=[pltpu.VMEM((tm, D), BF), pltpu.VMEM((tm, 2 * D), F32),
                        pltpu.VMEM((tm, D), F32), pltpu.VMEM((D, 2 * D), F32), pltpu.VMEM((D, D), F32),
                        pltpu.VMEM((GROUPS, CHUNK, CHUNK), F32)],
        input_output_aliases={11: 6, 12: 7},
        compiler_params=_params(),
    )(h, dh, zpre, g, lng, lnb, ws, wst, bsb, wod, gr, gpc, gpr)


def _final_loss(h, tgt, g, tm):
    s = h.shape[0]

    def body(h_ref, t_ref, g_ref, dh_ref, loss_ref, dg_ref):
        i = pl.program_id(0)

        @pl.when(_first(i))
        def _():
            loss_ref[...] = jnp.zeros_like(loss_ref)
            dg_ref[...] = jnp.zeros_like(dg_ref)

        gg = g_ref[...]
        hhat, rr = _rms(h_ref[...])
        diff = hhat * gg - t_ref[...]
        loss_ref[...] += jnp.sum(diff * diff) * (0.5 / D)
        dout = diff * (1.0 / D)
        dg_ref[...] += _colsum(dout * hhat)
        dh_ref[...] = _rms_bwd(dout, hhat, rr, gg)

    return pl.pallas_call(
        body, name="final_loss", grid=(s // tm,),
        in_specs=[_rows(tm, D), _rows(tm, D), _const((1, D), (0, 0))],
        out_specs=[_rows(tm, D), _const((1, 128), (0, 0)), _const((1, D), (0, 0))],
        out_shape=[_sds((s, D), F32), _sds((1, 128), F32), _sds((1, D), F32)],
        compiler_params=_params(),
    )(h, tgt, g)


def _pack_rows(w_down, xq, xo, xk, xv, ev_out, od_out):
    def body(wd_ref, xq_ref, xo_ref, xk_ref, xv_ref, ev_ref, od_ref, out_ref):
        out_ref[0:FSH, :] = wd_ref[0].astype(BF)
        out_ref[FSH:2 * FSH, :] = wd_ref[1].astype(BF)
        out_ref[2 * FSH:ROW_ITEM0, :] = jnp.zeros((ROW_ITEM0 - 2 * FSH, D), BF)
        srcs = {"xa_w_q": xq_ref, "xa_w_o": xo_ref, "xa_w_k": xk_ref, "xa_w_v": xv_ref}
        for j, name in enumerate(ROW_ITEMS):
            r0 = ROW_ITEM0 + 256 * j
            if name == "ev_w_out":
                val = ev_ref[0]
            elif name == "od_w_out":
                val = od_ref[0]
            else:
                val = srcs[name[:-1]][int(name[-1])]
            out_ref[r0:r0 + 256, :] = val.astype(BF)

    vm = pl.BlockSpec(memory_space=pltpu.VMEM)
    return pl.pallas_call(
        body, name="pack_rows", in_specs=[vm] * 7, out_specs=vm,
        out_shape=_sds((ROW_TOTAL, D), BF),
        compiler_params=pltpu.CompilerParams(vmem_limit_bytes=56 << 20),
    )(w_down, xq, xo, xk, xv, ev_out, od_out)


def _pack_cols(w_gate, w_up, od_in, ev_in):
    def body(wg_ref, wu_ref, od_ref, ev_ref, out_ref):
        pad = jnp.zeros((D, FPAD - FSH), BF)
        for layer in range(2):
            for j, ref in enumerate((wg_ref, wu_ref)):
                c0 = (2 * layer + j) * FPAD
                out_ref[:, c0:c0 + FSH] = ref[layer].astype(BF)
                out_ref[:, c0 + FSH:c0 + FPAD] = pad
        out_ref[:, COL_OD:COL_OD + 512] = od_ref[0].astype(BF)
        out_ref[:, COL_OD + 512:COL_EV] = jnp.zeros((D, COL_EV - COL_OD - 512), BF)
        out_ref[:, COL_EV:COL_TOTAL] = ev_ref[0].astype(BF)

    vm = pl.BlockSpec(memory_space=pltpu.VMEM)
    return pl.pallas_call(
        body, name="pack_cols", in_specs=[vm] * 4, out_specs=vm,
        out_shape=_sds((D, COL_TOTAL), BF),
        compiler_params=pltpu.CompilerParams(vmem_limit_bytes=56 << 20),
    )(w_gate, w_up, od_in, ev_in)


def _assemble_cols(gc):
    tr = 256

    def body(g_ref, wg0, wu0, wg1, wu1, wod, wev):
        for k in range(NSH):
            for j, ref in enumerate((wg0, wu0, wg1, wu1)):
                ref[:, FSH * k:FSH * (k + 1)] = g_ref[k, :, j * FPAD:j * FPAD + FSH]
            wod[:, 512 * k:512 * (k + 1)] = g_ref[k, :, COL_OD:COL_OD + 512]
            wev[:, 640 * k:640 * (k + 1)] = g_ref[k, :, COL_EV:COL_TOTAL]

    ffn = pl.BlockSpec((tr, DFF), lambda i: (i, 0))
    return pl.pallas_call(
        body, name="assemble_cols", grid=(D // tr,),
        in_specs=[pl.BlockSpec((NSH, tr, COL_TOTAL), lambda i: (0, i, 0))],
        out_specs=[ffn, ffn, ffn, ffn, pl.BlockSpec((tr, 2 * D), lambda i: (i, 0)),
                   pl.BlockSpec((tr, DZ), lambda i: (i, 0))],
        out_shape=[_sds((D, DFF), BF)] * 4 + [_sds((D, 2 * D), BF), _sds((D, DZ), BF)],
        compiler_params=_params(),
    )(gc)


def _pos():
    return lax.axis_index("x"), lax.axis_index("y"), lax.axis_index("c")


def _other_chips(x, y):
    return [(1 - x, y), (x, 1 - y), (1 - x, 1 - y)]


def _half(ref, hc, lead=()):
    r = ref.shape[-2] // 2
    return ref.at[(*lead, pl.ds(pl.multiple_of(hc * r, 8), r), slice(None))]


def _rcopy(src, dst, ssem, rsem, to):
    return pltpu.make_async_remote_copy(src_ref=src, dst_ref=dst, send_sem=ssem, recv_sem=rsem,
                                        device_id=to, device_id_type=MESH)


def _gather_packs(pr, pc, sp):
    srcs = (pr, pc, sp)
    n = len(srcs)

    def body(*refs):
        src = refs[:n]
        dst = refs[n:2 * n]
        ici_s, ici_r, d2d_s, d2d_r, loc = refs[2 * n:]
        x, y, c = _pos()
        me = 2 * x + y
        sib = (x, y, 1 - c)
        chips = _other_chips(x, y)
        local = [pltpu.make_async_copy(src[t], dst[t].at[me], loc.at[t]) for t in range(n)]
        for cp in local:
            cp.start()
        first = []
        for k, (cx, cy) in enumerate(chips):
            for t in range(n):
                cp = _rcopy(_half(src[t], c), _half(dst[t], c, (me,)), ici_s.at[3 * t + k], ici_r.at[3 * t + k],
                            (cx, cy, c))
                cp.start()
                first.append(cp)
        passed = []
        for k, (cx, cy) in enumerate(chips):
            them = 2 * cx + cy
            for t in range(n):
                landed = _half(dst[t], c, (them,))
                _rcopy(landed, landed, ici_s.at[3 * t + k], ici_r.at[3 * t + k], (cx, cy, c)).wait_recv()
                cp = _rcopy(landed, landed, d2d_s.at[3 * t + k], d2d_r.at[3 * t + k], sib)
                cp.start()
                passed.append(cp)
        for k, (cx, cy) in enumerate(chips):
            them = 2 * cx + cy
            for t in range(n):
                got = _half(dst[t], 1 - c, (them,))
                _rcopy(got, got, d2d_s.at[3 * t + k], d2d_r.at[3 * t + k], sib).wait_recv()
        for cp in first + passed:
            cp.wait_send()
        for cp in local:
            cp.wait()

    return pl.pallas_call(
        body, name="gather_packs",
        in_specs=[ANY] * n, out_specs=[ANY] * n,
        out_shape=[_sds((NSH,) + a.shape, a.dtype) for a in srcs],
        scratch_shapes=[pltpu.SemaphoreType.DMA((3 * n,))] * 4 + [pltpu.SemaphoreType.DMA((n,))],
    )(*srcs)


def _sibling_swap(items):
    n = len(items)

    def body(*refs):
        src = refs[:n]
        dst = refs[n:2 * n]
        ssem, rsem = refs[2 * n:]
        x, y, c = _pos()
        sib = (x, y, 1 - c)
        cps = []
        for t in range(n):
            lead = (slice(None),) * (len(src[t].shape) - 2)
            cp = _rcopy(_half(src[t], 1 - c, lead), dst[t], ssem.at[t], rsem.at[t], sib)
            cp.start()
            cps.append(cp)
        for cp in cps:
            cp.wait()

    return pl.pallas_call(
        body, name="rs_sibling_swap",
        in_specs=[ANY] * n, out_specs=[ANY] * n,
        out_shape=[_sds(a.shape[:-2] + (a.shape[-2] // 2, a.shape[-1]), a.dtype) for a in items],
        scratch_shapes=[pltpu.SemaphoreType.DMA((n,)), pltpu.SemaphoreType.DMA((n,))],
    )(*items)


def _chip_exchange(big, small):
    items = tuple(big) + tuple(small)
    n = len(items)
    nb = len(big)

    def body(*refs):
        src = refs[:n]
        dst = refs[n:2 * n]
        ssem, rsem = refs[2 * n:]
        x, y, c = _pos()
        cps = []
        for k, (cx, cy) in enumerate(_other_chips(x, y)):
            for t in range(n):
                s = src[t].at[2 * cx + cy] if t < nb else src[t]
                cp = _rcopy(s, dst[t].at[k], ssem.at[3 * t + k], rsem.at[3 * t + k], (cx, cy, c))
                cp.start()
                cps.append(cp)
        for cp in cps:
            cp.wait()

    return pl.pallas_call(
        body, name="rs_chip_exchange",
        in_specs=[ANY] * n, out_specs=[ANY] * n,
        out_shape=[_sds((3,) + a.shape[-2:], a.dtype) for a in items],
        scratch_shapes=[pltpu.SemaphoreType.DMA((3 * n,)), pltpu.SemaphoreType.DMA((3 * n,))],
    )(*items)


def _sibling_join(items):
    n = len(items)

    def body(*refs):
        src = refs[:n]
        dst = refs[n:2 * n]
        ssem, rsem, loc = refs[2 * n:]
        x, y, c = _pos()
        sib = (x, y, 1 - c)
        cps = []
        for t in range(n):
            mine = _half(dst[t], c)
            lc = pltpu.make_async_copy(src[t], mine, loc.at[t])
            lc.start()
            cp = _rcopy(src[t], mine, ssem.at[t], rsem.at[t], sib)
            cp.start()
            cps.append((lc, cp))
        for t, (lc, cp) in enumerate(cps):
            theirs = _half(dst[t], 1 - c)
            _rcopy(theirs, theirs, ssem.at[t], rsem.at[t], sib).wait_recv()
            cp.wait_send()
            lc.wait()

    return pl.pallas_call(
        body, name="rs_sibling_join",
        in_specs=[ANY] * n, out_specs=[ANY] * n,
        out_shape=[_sds((2 * a.shape[0], a.shape[1]), a.dtype) for a in items],
        scratch_shapes=[pltpu.SemaphoreType.DMA((n,))] * 3,
    )(*items)


def _blocks_of(rows, cols, itemsize):
    rb = rows
    while rb * cols * itemsize > (2 << 20) and rb % 2 == 0 and (rb // 2) % 16 == 0:
        rb //= 2
    return rb


def _add_half(full, recv, c_idx, name):
    p, r, cols = full.shape
    rb = _blocks_of(r // 2, cols, 4)
    nb = (r // 2) // rb

    def body(c_ref, a_ref, b_ref, o_ref):
        o_ref[...] = (a_ref[...].astype(F32) + b_ref[...].astype(F32)).astype(o_ref.dtype)

    grid_spec = pltpu.PrefetchScalarGridSpec(
        num_scalar_prefetch=1, grid=(p, nb),
        in_specs=[pl.BlockSpec((1, rb, cols), lambda j, i, c_ref: (j, c_ref[0] * nb + i, 0)),
                  pl.BlockSpec((1, rb, cols), lambda j, i, c_ref: (j, i, 0))],
        out_specs=pl.BlockSpec((1, rb, cols), lambda j, i, c_ref: (j, i, 0)))
    return pl.pallas_call(
        body, name=name, grid_spec=grid_spec, out_shape=_sds(recv.shape, full.dtype),
        compiler_params=_params(2),
    )(c_idx, full, recv)


def _sum_chips(own, got, plane_idx, name):
    _, r, cols = own.shape
    rb = _blocks_of(r, cols, 4)

    def body(s_ref, a_ref, b_ref, o_ref):
        f = lambda v: v.astype(F32)
        o_ref[...] = (f(a_ref[0]) + f(b_ref[1])) + (f(b_ref[0]) + f(b_ref[2]))

    grid_spec = pltpu.PrefetchScalarGridSpec(
        num_scalar_prefetch=1, grid=(r // rb,),
        in_specs=[pl.BlockSpec((1, rb, cols), lambda i, s_ref: (s_ref[0], i, 0)),
                  pl.BlockSpec((3, rb, cols), lambda i, s_ref: (0, i, 0))],
        out_specs=pl.BlockSpec((rb, cols), lambda i, s_ref: (i, 0)))
    return pl.pallas_call(
        body, name=name, grid_spec=grid_spec, out_shape=_sds((r, cols), F32),
        compiler_params=_params(1),
    )(plane_idx, own, got)


def _adamw_math(w, g, m, v):
    m = ADAM_B1 * m + (1.0 - ADAM_B1) * g
    v = ADAM_B2 * v + (1.0 - ADAM_B2) * (g * g)
    m_hat = m / (1.0 - ADAM_B1 ** ADAM_STEP)
    v_hat = v / (1.0 - ADAM_B2 ** ADAM_STEP)
    delta = -ADAM_LR * (m_hat / (jnp.sqrt(v_hat) + ADAM_EPS) + ADAM_WD * w)
    return delta, m, v


def _adamw_big(w, m, v, gsrc, gblock, gidx, name, rb):
    nl, r, cols = w.shape

    def body(w_ref, m_ref, v_ref, g_ref, go_ref, d_ref, mo_ref, vo_ref):
        g = g_ref[:, 0:cols]
        delta, mn, vn = _adamw_math(w_ref[0], g, m_ref[0], v_ref[0])
        go_ref[0] = g
        d_ref[0] = delta
        mo_ref[0] = mn
        vo_ref[0] = vn

    wspec = pl.BlockSpec((1, rb, cols), lambda l, i: (l, i, 0))
    return pl.pallas_call(
        body, name=name, grid=(nl, r // rb),
        in_specs=[wspec, wspec, wspec, pl.BlockSpec(gblock, gidx)],
        out_specs=[wspec] * 4, out_shape=[_sds(w.shape, F32)] * 4,
        compiler_params=_params(2),
    )(w, m, v, gsrc)


_SMALL = (
    ("g_mix", (2, D), "a", (0, 2), 0, False),
    ("g_xattn", (2, D), "a", (2, 4), 0, False),
    ("g_mem", (2, D), "a", (4, 6), 0, False),
    ("g_ffn", (2, D), "a", (6, 8), 0, False),
    ("g_final", (1, D), "a", (8, 9), 0, False),
    ("ev_a_conv_w", (KA, 128), "a", (16, 16 + KA), 0, True),
    ("ev_a_conv_b", (1, DA), "a", (11, 12), 0, False),
    ("ev_a_ln_g", (1, DA), "a", (11, 12), DA, False),
    ("ev_a_ln_b", (1, DA), "a", (12, 13), 0, False),
    ("ev_b_conv_w", (KB, 128), "a", (48, 48 + KB), 0, True),
    ("ev_b_conv_b", (1, DA), "a", (12, 13), DA, False),
    ("od_c_ln_g", (1, 256), "a", (9, 10), 0, True),
    ("od_c_ln_b", (1, 256), "a", (10, 11), 0, True),
    ("od_w_s", (GROUPS * CHUNK, CHUNK), "b", (0, GROUPS * CHUNK), 0, False),
    ("od_b_s", (GROUPS, CHUNK), "a", (56, 64), 0, False),
)


def _adamw_small(ga, gb, wmv):
    ns = len(_SMALL)

    def body(*refs):
        ga_ref, gb_ref = refs[0], refs[1]
        ins = refs[2:2 + 3 * ns]
        outs = refs[2 + 3 * ns:]
        x, y, _ = _pos()
        chip = 2 * x + y
        for j, (_, shape, pack, (r0, r1), col, sharded) in enumerate(_SMALL):
            src = ga_ref if pack == "a" else gb_ref
            width = shape[1]
            if sharded:
                g = jnp.zeros(shape, F32)
                for k in range(NSH):
                    blk = src[r0:r1, col + width * k:col + width * (k + 1)]
                    g = g + jnp.where(chip == k, blk, 0.0)
            else:
                g = src[r0:r1, col:col + width]
            w_ref, m_ref, v_ref = ins[3 * j:3 * j + 3]
            delta, mn, vn = _adamw_math(w_ref[...], g, m_ref[...], v_ref[...])
            go_ref, d_ref, mo_ref, vo_ref = outs[4 * j:4 * j + 4]
            go_ref[...] = g
            d_ref[...] = delta
            mo_ref[...] = mn
            vo_ref[...] = vn

    vm = pl.BlockSpec(memory_space=pltpu.VMEM)
    flat = [a for trio in wmv for a in trio]
    out_shape = []
    for (_, shape, *_rest) in _SMALL:
        out_shape += [_sds(shape, F32)] * 4
    outs = pl.pallas_call(
        body, name="adamw_small",
        in_specs=[vm] * (2 + 3 * ns), out_specs=[vm] * (4 * ns), out_shape=out_shape,
    )(ga, gb, *flat)
    return [tuple(outs[4 * j:4 * j + 4]) for j in range(ns)]


def _local_fwd_bwd(x, mem, tgt, sm, gr, cols, gpr, gpc, tm_f, tm_b):
    wg = (cols["wg0"], cols["wg1"])
    wu = (cols["wu0"], cols["wu1"])
    row = lambda a, i: a[i:i + 1]

    z = _ev_in_fwd(x, row(sm["g_mix"], 0), cols["wev"], tm_f)
    h1, a2, c2, cat = _ev_mix_fwd(x, z, sm["wa"], sm["ba"], sm["lng_a"], sm["lnb_a"], sm["wb"], sm["bb"], gr, tm_f)
    hs = [x, h1]
    kvs, qs, gts, ups = [], [], [], []
    zpre = None
    for layer in range(2):
        if layer == 1:
            h, zpre = _sgu_fwd(hs[-1], row(sm["g_mix"], 1), sm["lng_c"], sm["lnb_c"], sm["ws"], sm["bsb"],
                               cols["wod"], gr, tm_f)
            hs.append(h)
        k, v = _kv_proj(mem, row(sm["g_mem"], layer), gr, layer)
        h, q = _xattn_fwd(hs[-1], row(sm["g_xattn"], layer), k, v, gr, layer, tm_f)
        hs.append(h)
        h, gt, up = _ffn_fwd(hs[-1], row(sm["g_ffn"], layer), wg[layer], wu[layer], gr, layer, tm_b)
        hs.append(h)
        kvs.append((k, v))
        qs.append(q)
        gts.append(gt)
        ups.append(up)
    dh, loss, dg_final = _final_loss(hs[6], tgt, sm["g_final"], tm_f)

    g = {"g_final": dg_final}
    dg_ffn, dg_xattn, dg_mem = [None, None], [None, None], [None, None]
    for layer in (1, 0):
        h_ffn, h_xa = hs[3 * layer + 2], hs[3 * layer + 1]
        dh_out = dh
        dh, dg_ffn[layer], dgt, dup = _ffn_bwd_dx(h_ffn, dh_out, gts[layer], ups[layer], row(sm["g_ffn"], layer),
                                                  wg[layer], wu[layer], gr, layer, tm_b)
        gpc = _ffn_bwd_dwgu(h_ffn, row(sm["g_ffn"], layer), dgt, dup, gpc, layer, tm_b)
        gpr = _ffn_bwd_dwd(gts[layer], ups[layer], dh_out, gpr, layer, tm_b)
        k, v = kvs[layer]
        dh, dk, dv, dg_xattn[layer], gpr = _xattn_bwd(h_xa, dh, qs[layer], row(sm["g_xattn"], layer), k, v, gr,
                                                      gpr, layer, tm_b)
        dg_mem[layer], gpr = _kv_proj_bwd(mem, row(sm["g_mem"], layer), dk, dv, gr, gpr, layer)
        if layer == 1:
            dh, g["od_w_s"], g["od_b_s"], g["od_c_ln_g"], g["od_c_ln_b"], dg_mix1, gpc, gpr = _sgu_bwd(
                hs[3], dh, zpre, row(sm["g_mix"], 1), sm["lng_c"], sm["lnb_c"], sm["ws"], sm["wst"], sm["bsb"],
                cols["wod"], gr, gpc, gpr, tm_b)
    da2, dc2, dgb, g["ev_a_ln_g"], g["ev_a_ln_b"], gpr = _ev_mix_bwd(dh, cat, a2, c2, z, sm["lng_a"], sm["lnb_a"],
                                                                     gr, gpr, tm_b)
    dx, g["ev_a_conv_w"], g["ev_a_conv_b"], g["ev_b_conv_w"], g["ev_b_conv_b"], dg_mix0, gpc = _ev_in_bwd(
        x, dh, z, da2, dc2, dgb, row(sm["g_mix"], 0), sm["wa"], sm["wb"], cols["wev"], gpc, tm_b)
    g["g_mix"] = jnp.concatenate([dg_mix0, dg_mix1], axis=0)
    g["g_xattn"] = jnp.concatenate(dg_xattn, axis=0)
    g["g_mem"] = jnp.concatenate(dg_mem, axis=0)
    g["g_ffn"] = jnp.concatenate(dg_ffn, axis=0)
    return loss, dx, g, gpr, gpc


def _small_grad_packs(g):
    z = lambda r, c: jnp.zeros((r, c), F32)
    wide = lambda a: jnp.concatenate([a, z(a.shape[0], D - a.shape[1])], axis=1)
    rows = [g["g_mix"], g["g_xattn"], g["g_mem"], g["g_ffn"], g["g_final"], g["od_c_ln_g"], g["od_c_ln_b"],
            jnp.concatenate([g["ev_a_conv_b"], g["ev_a_ln_g"]], axis=1),
            jnp.concatenate([g["ev_a_ln_b"], g["ev_b_conv_b"]], axis=1),
            z(3, D), wide(g["ev_a_conv_w"]), wide(g["ev_b_conv_w"]), wide(g["od_b_s"])]
    a = jnp.concatenate(rows, axis=0)
    assert a.shape == (SA_ROWS, D), a.shape
    return a, g["od_w_s"].reshape(GROUPS * CHUNK, CHUNK)


def kernel(x, mem, g_mix, g_xattn, g_mem, g_ffn, g_final, ev_w_in, ev_a_conv_w, ev_a_conv_b, ev_a_ln_g, ev_a_ln_b, ev_b_conv_w, ev_b_conv_b, ev_w_out, od_w_in, od_c_ln_g, od_c_ln_b, od_w_s, od_b_s, od_w_out, xa_w_q, xa_w_k, xa_w_v, xa_w_o, ffn_w_gate, ffn_w_up, ffn_w_down, loss_target, m_g_mix, m_g_xattn, m_g_mem, m_g_ffn, m_g_final, m_ev_w_in, m_ev_a_conv_w, m_ev_a_conv_b, m_ev_a_ln_g, m_ev_a_ln_b, m_ev_b_conv_w, m_ev_b_conv_b, m_ev_w_out, m_od_w_in, m_od_c_ln_g, m_od_c_ln_b, m_od_w_s, m_od_b_s, m_od_w_out, m_xa_w_q, m_xa_w_k, m_xa_w_v, m_xa_w_o, m_ffn_w_gate, m_ffn_w_up, m_ffn_w_down, v_g_mix, v_g_xattn, v_g_mem, v_g_ffn, v_g_final, v_ev_w_in, v_ev_a_conv_w, v_ev_a_conv_b, v_ev_a_ln_g, v_ev_a_ln_b, v_ev_b_conv_w, v_ev_b_conv_b, v_ev_w_out, v_od_w_in, v_od_c_ln_g, v_od_c_ln_b, v_od_w_s, v_od_b_s, v_od_w_out, v_xa_w_q, v_xa_w_k, v_xa_w_v, v_xa_w_o, v_ffn_w_gate, v_ffn_w_up, v_ffn_w_down):
    W = dict(g_mix=g_mix, g_xattn=g_xattn, g_mem=g_mem, g_ffn=g_ffn, g_final=g_final, ev_w_in=ev_w_in,
             ev_a_conv_w=ev_a_conv_w, ev_a_conv_b=ev_a_conv_b, ev_a_ln_g=ev_a_ln_g, ev_a_ln_b=ev_a_ln_b,
             ev_b_conv_w=ev_b_conv_w, ev_b_conv_b=ev_b_conv_b, ev_w_out=ev_w_out, od_w_in=od_w_in,
             od_c_ln_g=od_c_ln_g, od_c_ln_b=od_c_ln_b, od_w_s=od_w_s, od_b_s=od_b_s, od_w_out=od_w_out,
             xa_w_q=xa_w_q, xa_w_k=xa_w_k, xa_w_v=xa_w_v, xa_w_o=xa_w_o, ffn_w_gate=ffn_w_gate,
             ffn_w_up=ffn_w_up, ffn_w_down=ffn_w_down)
    M = dict(g_mix=m_g_mix, g_xattn=m_g_xattn, g_mem=m_g_mem, g_ffn=m_g_ffn, g_final=m_g_final, ev_w_in=m_ev_w_in,
             ev_a_conv_w=m_ev_a_conv_w, ev_a_conv_b=m_ev_a_conv_b, ev_a_ln_g=m_ev_a_ln_g, ev_a_ln_b=m_ev_a_ln_b,
             ev_b_conv_w=m_ev_b_conv_w, ev_b_conv_b=m_ev_b_conv_b, ev_w_out=m_ev_w_out, od_w_in=m_od_w_in,
             od_c_ln_g=m_od_c_ln_g, od_c_ln_b=m_od_c_ln_b, od_w_s=m_od_w_s, od_b_s=m_od_b_s, od_w_out=m_od_w_out,
             xa_w_q=m_xa_w_q, xa_w_k=m_xa_w_k, xa_w_v=m_xa_w_v, xa_w_o=m_xa_w_o, ffn_w_gate=m_ffn_w_gate,
             ffn_w_up=m_ffn_w_up, ffn_w_down=m_ffn_w_down)
    V = dict(g_mix=v_g_mix, g_xattn=v_g_xattn, g_mem=v_g_mem, g_ffn=v_g_ffn, g_final=v_g_final, ev_w_in=v_ev_w_in,
             ev_a_conv_w=v_ev_a_conv_w, ev_a_conv_b=v_ev_a_conv_b, ev_a_ln_g=v_ev_a_ln_g, ev_a_ln_b=v_ev_a_ln_b,
             ev_b_conv_w=v_ev_b_conv_w, ev_b_conv_b=v_ev_b_conv_b, ev_w_out=v_ev_w_out, od_w_in=v_od_w_in,
             od_c_ln_g=v_od_c_ln_g, od_c_ln_b=v_od_c_ln_b, od_w_s=v_od_w_s, od_b_s=v_od_b_s, od_w_out=v_od_w_out,
             xa_w_q=v_xa_w_q, xa_w_k=v_xa_w_k, xa_w_v=v_xa_w_v, xa_w_o=v_xa_w_o, ffn_w_gate=v_ffn_w_gate,
             ffn_w_up=v_ffn_w_up, ffn_w_down=v_ffn_w_down)
    order = list(W)
    cx, cy, cc = _pos()
    chip = (2 * cx + cy).astype(jnp.int32)

    pr = _pack_rows(ffn_w_down, xa_w_q, xa_w_o, xa_w_k, xa_w_v, ev_w_out, od_w_out)
    pc = _pack_cols(ffn_w_gate, ffn_w_up, od_w_in, ev_w_in)
    sp = jnp.zeros((SP_ROWS, 256), F32)
    sp = sp.at[0:KA, 0:128].set(ev_a_conv_w[0]).at[32:32 + KB, 0:128].set(ev_b_conv_w[0])
    sp = sp.at[40, :].set(od_c_ln_g[0]).at[41, :].set(od_c_ln_b[0])
    gr, gc, gs = _gather_packs(pr, pc, sp)
    wg0, wu0, wg1, wu1, wod, wev = _assemble_cols(gc)
    cols = dict(wg0=wg0, wu0=wu0, wg1=wg1, wu1=wu1, wod=wod, wev=wev)
    unshard = lambda a: jnp.transpose(a, (1, 0, 2)).reshape(a.shape[1], NSH * a.shape[2])
    sm = dict(
        g_mix=g_mix, g_xattn=g_xattn, g_mem=g_mem, g_ffn=g_ffn, g_final=g_final.reshape(1, D),
        wa=unshard(gs[:, 0:32, 0:128]), ba=ev_a_conv_b, lng_a=ev_a_ln_g, lnb_a=ev_a_ln_b,
        wb=unshard(gs[:, 32:40, 0:128]), bb=ev_b_conv_b,
        lng_c=unshard(gs[:, 40:41, :]), lnb_c=unshard(gs[:, 41:42, :]),
        ws=od_w_s[0].astype(BF), wst=jnp.swapaxes(od_w_s[0], 1, 2).astype(BF),
        bsb=jnp.broadcast_to(od_b_s[0][:, :, None], (GROUPS, CHUNK, CHUNK)),
    )

    gpr = jnp.zeros((NSH, ROW_TOTAL, D), BF)
    gpc = jnp.zeros((NSH, D, COL_TOTAL), BF)
    loss, dx, g, gpr, gpc = _local_fwd_bwd(x[0], mem[0], loss_target[0], sm, gr, cols, gpr, gpc, TM_FWD, TM_BWD)
    loss = lax.psum(loss[0, 0], ("x", "y", "c"))

    ga, gb = _small_grad_packs(g)
    c_idx = jnp.reshape(cc, (1,)).astype(jnp.int32)
    zero_idx = jnp.zeros((1,), jnp.int32)
    chip_idx = jnp.reshape(chip, (1,))
    fulls = (gpr, gpc, ga[None], gb[None])
    recvs = _sibling_swap(fulls)
    part = [_add_half(f, r, c_idx, f"rs_add_half{t}") for t, (f, r) in enumerate(zip(fulls, recvs))]
    got = _chip_exchange(part[:2], (part[2][0], part[3][0]))
    halves = [_sum_chips(part[t], got[t], chip_idx if t < 2 else zero_idx, f"rs_sum_chips{t}") for t in range(4)]
    rr, rc, ra, rb_ = _sibling_join(halves)

    res = {}
    big = {
        "ffn_w_down": (rr, (FSH, D), lambda l, i: (l, 0), FSH),
        "xa_w_q": (rr, (256, D), lambda l, i: (_row_item("xa_w_q0") + 2 * l, 0), 256),
        "xa_w_o": (rr, (256, D), lambda l, i: (_row_item("xa_w_o0") + 2 * l, 0), 256),
        "xa_w_k": (rr, (256, D), lambda l, i: (_row_item("xa_w_k0") + 2 * l, 0), 256),
        "xa_w_v": (rr, (256, D), lambda l, i: (_row_item("xa_w_v0") + 2 * l, 0), 256),
        "ev_w_out": (rr, (256, D), lambda l, i: (_row_item("ev_w_out"), 0), 256),
        "od_w_out": (rr, (256, D), lambda l, i: (_row_item("od_w_out"), 0), 256),
        "ffn_w_gate": (rc, (256, FPAD), lambda l, i: (i, 2 * l), 256),
        "ffn_w_up": (rc, (256, FPAD), lambda l, i: (i, 2 * l + 1), 256),
        "od_w_in": (rc, (256, 512), lambda l, i: (i, COL_OD // 512), 256),
        "ev_w_in": (rc, (256, 640), lambda l, i: (i, COL_EV // 640), 256),
    }
    for name, (src, gblock, gidx, rb) in big.items():
        res[name] = _adamw_big(W[name], M[name], V[name], src, gblock, gidx, f"adamw_{name}", rb)
    shape2 = {name: shape for name, shape, *_ in _SMALL}
    wmv = [tuple(d[name].reshape(shape2[name]) for d in (W, M, V)) for name, *_ in _SMALL]
    for (name, *_), outs in zip(_SMALL, _adamw_small(ra, rb_, wmv)):
        res[name] = tuple(o.reshape(W[name].shape) for o in outs)

    grad_x = dx[None]
    return (loss, grad_x, *[res[n][0] for n in order], *[res[n][1] for n in order],
            *[res[n][2] for n in order], *[res[n][3] for n in order])
```

```python
import jax
import jax.numpy as jnp
from jax import lax
from jax.experimental import pallas as pl
from jax.experimental.pallas import tpu as pltpu

BF = jnp.bfloat16
F32 = jnp.float32

D = 1024
DA = 512
DZ = 2560
DFF = 2816
NSH = 4
FSH = DFF // NSH
FH = DFF // 2
FPAD = 768
NMEM = 256
HEADS = 4
HD = D // HEADS
CHUNK = 128
GROUPS = 8
KA = 31
KB = 3
HALO = 16
RC = 32
RMS_EPS = 1e-6
LN_EPS = 1e-5
ATT_SCALE = HD ** -0.5
TM_FWD = 512
TM_BWD = 256

ADAM_LR = 0.001
ADAM_B1 = 0.9
ADAM_B2 = 0.999
ADAM_EPS = 1e-08
ADAM_WD = 0.01
ADAM_STEP = 10

ROW_ITEM0 = 1536
ROW_ITEMS = ("xa_w_q0", "xa_w_o0", "xa_w_q1", "xa_w_o1", "xa_w_k0", "xa_w_v0", "xa_w_k1", "xa_w_v1",
             "ev_w_out", "od_w_out")
ROW_TOTAL = ROW_ITEM0 + 256 * len(ROW_ITEMS)
COL_OD = 4 * FPAD
COL_EV = 3840
COL_TOTAL = COL_EV + 640
SP_ROWS = 48
SA_ROWS = 64

MESH = pl.DeviceIdType.MESH
ANY = pl.BlockSpec(memory_space=pl.ANY)


def _row_item(name):
    return ROW_ITEM0 // 256 + ROW_ITEMS.index(name)


def _row_pair(name):
    k = _row_item(name)
    assert k % 2 == 0
    return k // 2


def _dot(a, b):
    return jnp.dot(a, b, preferred_element_type=F32)


def _dot_nt(a, b):
    return lax.dot_general(a, b, (((1,), (1,)), ((), ())), preferred_element_type=F32)


def _dot_tn(a, b):
    return lax.dot_general(a, b, (((0,), (0,)), ((), ())), preferred_element_type=F32)


def _rms(h):
    r = lax.rsqrt(jnp.mean(h * h, axis=-1, keepdims=True) + RMS_EPS)
    return h * r, r


def _rms_bwd(dn, hhat, r, g):
    dhh = dn * g
    return r * (dhh - hhat * jnp.mean(dhh * hhat, axis=-1, keepdims=True))


def _ln(x):
    mu = jnp.mean(x, axis=-1, keepdims=True)
    xc = x - mu
    rstd = lax.rsqrt(jnp.mean(xc * xc, axis=-1, keepdims=True) + LN_EPS)
    return xc * rstd, rstd


def _ln_bwd(dy, y, rstd):
    return rstd * (dy - jnp.mean(dy, axis=-1, keepdims=True) - y * jnp.mean(dy * y, axis=-1, keepdims=True))


def _colsum(x):
    return jnp.sum(x, axis=0, keepdims=True)


def _sigmoid(x):
    return 1.0 / (1.0 + jnp.exp(-x))


_GELU_C = 0.7978845608028654
_GELU_A = 0.044715


def _gelu(x):
    t = jnp.tanh(_GELU_C * (x + _GELU_A * x * x * x))
    return 0.5 * x * (1.0 + t)


def _gelu_grad(x):
    t = jnp.tanh(_GELU_C * (x + _GELU_A * x * x * x))
    return 0.5 * (1.0 + t) + 0.5 * x * (1.0 - t * t) * _GELU_C * (1.0 + 3.0 * _GELU_A * x * x)


def _first(i):
    return i == 0


def _last(i):
    return i == pl.num_programs(0) - 1


def _const(shape, idx):
    return pl.BlockSpec(shape, lambda *_, _idx=tuple(idx): _idx, pipeline_mode=pl.Buffered(1))


def _rows(tm, width, col=0):
    return pl.BlockSpec((tm, width), lambda i, _c=col: (i, _c))


def _params(ndim=1, vmem_mb=56):
    return pltpu.CompilerParams(dimension_semantics=("arbitrary",) * ndim, vmem_limit_bytes=vmem_mb << 20)


def _sds(shape, dtype):
    return jax.ShapeDtypeStruct(shape, dtype)


def _ev_in_fwd(h, g, wev, tm):
    s = h.shape[0]

    def body(h_ref, g_ref, w_ref, z_ref):
        hhat, _ = _rms(h_ref[...])
        n = (hhat * g_ref[...]).astype(BF)
        z_ref[...] = _dot(n, w_ref[...]).astype(BF)

    return pl.pallas_call(
        body, name="ev_in_fwd", grid=(s // tm,),
        in_specs=[_rows(tm, D), _const((1, D), (0, 0)), _const((D, DZ), (0, 0))],
        out_specs=_rows(tm, DZ),
        out_shape=_sds((s, DZ), BF),
        compiler_params=_params(),
    )(h, g, wev)


def _halo_specs(tm, width, s, col=0):
    per = tm // HALO
    nh = s // HALO
    prev = pl.BlockSpec((HALO, width), lambda i, _c=col: (jnp.maximum(i * per - 1, 0), _c))
    nxt = pl.BlockSpec((HALO, width), lambda i, _c=col: (jnp.minimum((i + 1) * per, nh - 1), _c))
    return prev, nxt


def _fill_pad(pad_ref, prev, main, nxt, i, tm):
    pad_ref[pl.ds(0, HALO), :] = jnp.where(_first(i), 0.0, prev)
    pad_ref[pl.ds(HALO, tm), :] = main
    pad_ref[pl.ds(HALO + tm, HALO), :] = jnp.where(_last(i), 0.0, nxt)


def _glu_a(z):
    return z[:, :DA] * _sigmoid(z[:, DA:2 * DA])


def _gate_c(z):
    return z[:, 2048:] * z[:, 1024:1536]


def _ev_mix_fwd(h, z, wa, ba, lng, lnb, wb, bb, gr, tm):
    s = h.shape[0]

    def body(h_ref, z_ref, zp_ref, zn_ref, wa_ref, ba_ref, lng_ref, lnb_ref, wb_ref, bb_ref, wo_ref,
             h1_ref, a2_ref, c2_ref, cat_ref, apad, cpad, c2f):
        i = pl.program_id(0)
        zm = z_ref[...].astype(F32)
        zp = zp_ref[...].astype(F32)
        zn = zn_ref[...].astype(F32)
        _fill_pad(apad, _glu_a(zp), _glu_a(zm), _glu_a(zn), i, tm)
        _fill_pad(cpad, _gate_c(zp), _gate_c(zm), _gate_c(zn), i, tm)
        for r in range(0, tm, RC):
            acc = jnp.zeros((RC, DA), F32) + ba_ref[...]
            for k in range(KA):
                acc = acc + apad[pl.ds(r + k + 1, RC), :] * wa_ref[pl.ds(k, 1), :]
            a2_ref[pl.ds(r, RC), :] = acc
            acc = jnp.zeros((RC, DA), F32) + bb_ref[...]
            for k in range(KB):
                acc = acc + cpad[pl.ds(r + HALO - 1 + k, RC), :] * wb_ref[pl.ds(k, 1), :]
            c2f[pl.ds(r, RC), :] = acc
        y, _ = _ln(a2_ref[...])
        a3 = y * lng_ref[...] + lnb_ref[...]
        cat_ref[:, :DA] = (a3 * _sigmoid(a3)).astype(BF)
        c2 = c2f[...]
        c2_ref[...] = c2.astype(BF)
        cat_ref[:, DA:] = (zm[:, 1536:2048] * c2).astype(BF)
        h1_ref[...] = h_ref[...] + _dot(cat_ref[...], wo_ref[...].reshape(D, D))

    zp_spec, zn_spec = _halo_specs(tm, DZ, s)
    small = lambda rows: _const((rows, DA), (0, 0))
    return pl.pallas_call(
        body, name="ev_mix_fwd", grid=(s // tm,),
        in_specs=[_rows(tm, D), _rows(tm, DZ), zp_spec, zn_spec, small(32), small(1), small(1), small(1),
                  small(8), small(1), _const((NSH, 256, D), (0, _row_item("ev_w_out"), 0))],
        out_specs=[_rows(tm, D), _rows(tm, DA), _rows(tm, DA), _rows(tm, D)],
        out_shape=[_sds((s, D), F32), _sds((s, DA), F32), _sds((s, DA), BF), _sds((s, D), BF)],
        scratch_shapes=[pltpu.VMEM((tm + 2 * HALO, DA), F32), pltpu.VMEM((tm + 2 * HALO, DA), F32),
                        pltpu.VMEM((tm, DA), F32)],
        compiler_params=_params(),
    )(h, z, z, z, wa, ba, lng, lnb, wb, bb, gr)


def _ev_mix_bwd(dh, cat, a2, c2, z, lng, lnb, gr, gpr, tm):
    s = dh.shape[0]

    def body(dh_ref, cat_ref, a2_ref, c2_ref, zgb_ref, lng_ref, lnb_ref, wo_ref, gpr_in,
             da2_ref, dc2_ref, dgb_ref, dlng_ref, dlnb_ref, gpr_ref, wacc):
        i = pl.program_id(0)

        @pl.when(_first(i))
        def _():
            wacc[...] = jnp.zeros_like(wacc)
            dlng_ref[...] = jnp.zeros_like(dlng_ref)
            dlnb_ref[...] = jnp.zeros_like(dlnb_ref)

        dhb = dh_ref[...].astype(BF)
        dcat = _dot_nt(dhb, wo_ref[...].reshape(D, D))
        wacc[...] += _dot_tn(cat_ref[...], dhb)
        y, rstd = _ln(a2_ref[...])
        a3 = y * lng_ref[...] + lnb_ref[...]
        sg = _sigmoid(a3)
        da3 = dcat[:, :DA] * (sg * (1.0 + a3 * (1.0 - sg)))
        dlng_ref[...] += _colsum(da3 * y)
        dlnb_ref[...] += _colsum(da3)
        da2_ref[...] = _ln_bwd(da3 * lng_ref[...], y, rstd).astype(BF)
        db = dcat[:, DA:]
        dc2_ref[...] = (db * zgb_ref[...].astype(F32)).astype(BF)
        dgb_ref[...] = (db * c2_ref[...].astype(F32)).astype(BF)

        @pl.when(_last(i))
        def _():
            gpr_ref[...] = wacc[...].astype(BF).reshape(NSH, 256, D)

    small = _const((1, DA), (0, 0))
    item = _row_item("ev_w_out")
    return pl.pallas_call(
        body, name="ev_mix_bwd", grid=(s // tm,),
        in_specs=[_rows(tm, D), _rows(tm, D), _rows(tm, DA), _rows(tm, DA), _rows(tm, DA, 3), small, small,
                  _const((NSH, 256, D), (0, item, 0)), ANY],
        out_specs=[_rows(tm, DA), _rows(tm, DA), _rows(tm, DA), small, small,
                   _const((NSH, 256, D), (0, item, 0))],
        out_shape=[_sds((s, DA), BF), _sds((s, DA), BF), _sds((s, DA), BF), _sds((1, DA), F32),
                   _sds((1, DA), F32), _sds(gpr.shape, BF)],
        scratch_shapes=[pltpu.VMEM((D, D), F32)],
        input_output_aliases={8: 5},
        compiler_params=_params(),
    )(dh, cat, a2, c2, z, lng, lnb, gr, gpr)


def _ev_in_bwd(h, dh, z, da2, dc2, dgb, g, wa, wb, wev, gpc, tm):
    s = h.shape[0]

    def body(h_ref, dh_ref, z_ref, zp_ref, zn_ref, da_ref, dap_ref, dan_ref, dc_ref, dcp_ref, dcn_ref,
             dgb_ref, g_ref, wa_ref, wb_ref, w_ref, gpc_in,
             dh0_ref, dwa_ref, dba_ref, dwb_ref, dbb_ref, dg_ref, gpc_ref,
             apad, cpad, dapad, dcpad, dz, wacc, dwa_acc, dwb_acc):
        i = pl.program_id(0)

        @pl.when(_first(i))
        def _():
            wacc[...] = jnp.zeros_like(wacc)
            dwa_acc[...] = jnp.zeros_like(dwa_acc)
            dwb_acc[...] = jnp.zeros_like(dwb_acc)
            dba_ref[...] = jnp.zeros_like(dba_ref)
            dbb_ref[...] = jnp.zeros_like(dbb_ref)
            dg_ref[...] = jnp.zeros_like(dg_ref)

        zm = z_ref[...].astype(F32)
        zp = zp_ref[...].astype(F32)
        zn = zn_ref[...].astype(F32)
        _fill_pad(apad, _glu_a(zp), _glu_a(zm), _glu_a(zn), i, tm)
        _fill_pad(cpad, _gate_c(zp), _gate_c(zm), _gate_c(zn), i, tm)
        da2 = da_ref[...].astype(F32)
        dc2 = dc_ref[...].astype(F32)
        _fill_pad(dapad, dap_ref[...].astype(F32), da2, dan_ref[...].astype(F32), i, tm)
        _fill_pad(dcpad, dcp_ref[...].astype(F32), dc2, dcn_ref[...].astype(F32), i, tm)
        dba_ref[...] += _colsum(da2)
        dbb_ref[...] += _colsum(dc2)

        sg = _sigmoid(zm[:, DA:2 * DA])
        aval = zm[:, :DA]
        for r in range(0, tm, RC):
            acc = jnp.zeros((RC, DA), F32)
            for k in range(KA):
                acc = acc + dapad[pl.ds(r + 2 * HALO - 1 - k, RC), :] * wa_ref[pl.ds(k, 1), :]
            sgr = sg[r:r + RC]
            dz[pl.ds(r, RC), 0:DA] = (acc * sgr).astype(BF)
            dz[pl.ds(r, RC), DA:2 * DA] = (acc * aval[r:r + RC] * sgr * (1.0 - sgr)).astype(BF)
            acc = jnp.zeros((RC, DA), F32)
            for k in range(KB):
                acc = acc + dcpad[pl.ds(r + HALO + 1 - k, RC), :] * wb_ref[pl.ds(k, 1), :]
            dz[pl.ds(r, RC), 1024:1536] = (acc * zm[r:r + RC, 2048:]).astype(BF)
            dz[pl.ds(r, RC), 2048:2560] = (acc * zm[r:r + RC, 1024:1536]).astype(BF)
        dz[:, 1536:2048] = dgb_ref[...]

        for k in range(KA):
            acc = jnp.zeros((8, DA), F32)
            for r in range(0, tm, RC):
                p = apad[pl.ds(r + k + 1, RC), :] * dapad[pl.ds(r + HALO, RC), :]
                acc = acc + p.reshape(RC // 8, 8, DA).sum(axis=0)
            dwa_acc[k] += acc
        for k in range(KB):
            acc = jnp.zeros((8, DA), F32)
            for r in range(0, tm, RC):
                p = cpad[pl.ds(r + HALO - 1 + k, RC), :] * dcpad[pl.ds(r + HALO, RC), :]
                acc = acc + p.reshape(RC // 8, 8, DA).sum(axis=0)
            dwb_acc[k] += acc

        gg = g_ref[...]
        hhat, rr = _rms(h_ref[...])
        n = (hhat * gg).astype(BF)
        dzb = dz[...]
        wacc[...] += _dot_tn(n, dzb)
        dn = _dot_nt(dzb, w_ref[...])
        dg_ref[...] += _colsum(dn * hhat)
        dh0_ref[...] = dh_ref[...] + _rms_bwd(dn, hhat, rr, gg)

        @pl.when(_last(i))
        def _():
            for k in range(NSH):
                gpc_ref[k] = wacc[:, 640 * k:640 * (k + 1)].astype(BF)
            dwa_ref[...] = jnp.sum(dwa_acc[...], axis=1)
            dwb_ref[...] = jnp.sum(dwb_acc[...], axis=1)

    zp_spec, zn_spec = _halo_specs(tm, DZ, s)
    hp_spec, hn_spec = _halo_specs(tm, DA, s)
    small = lambda rows: _const((rows, DA), (0, 0))
    wspec = _const((NSH, D, 640), (0, 0, COL_EV // 640))
    return pl.pallas_call(
        body, name="ev_in_bwd", grid=(s // tm,),
        in_specs=[_rows(tm, D), _rows(tm, D), _rows(tm, DZ), zp_spec, zn_spec,
                  _rows(tm, DA), hp_spec, hn_spec, _rows(tm, DA), hp_spec, hn_spec, _rows(tm, DA),
                  _const((1, D), (0, 0)), small(32), small(8), _const((D, DZ), (0, 0)), ANY],
        out_specs=[_rows(tm, D), small(32), small(1), small(8), small(1), _const((1, D), (0, 0)), wspec],
        out_shape=[_sds((s, D), F32), _sds((32, DA), F32), _sds((1, DA), F32), _sds((8, DA), F32),
                   _sds((1, DA), F32), _sds((1, D), F32), _sds(gpc.shape, BF)],
        scratch_shapes=[pltpu.VMEM((tm + 2 * HALO, DA), F32)] * 4 + [
            pltpu.VMEM((tm, DZ), BF), pltpu.VMEM((D, DZ), F32),
            pltpu.VMEM((32, 8, DA), F32), pltpu.VMEM((8, 8, DA), F32)],
        input_output_aliases={16: 6},
        compiler_params=_params(),
    )(h, dh, z, z, z, da2, da2, da2, dc2, dc2, dc2, dgb, g, wa, wb, wev, gpc)


def _full(shape):
    return pl.BlockSpec(shape, lambda i: (0,) * len(shape))


def _kv_proj(mem, g, gr, layer):
    def body(mem_ref, g_ref, wk_ref, wv_ref, k_ref, v_ref):
        mhat, _ = _rms(mem_ref[...])
        mn = (mhat * g_ref[...]).astype(BF)
        k_ref[...] = _dot(mn, wk_ref[...].reshape(D, D)).astype(BF)
        v_ref[...] = _dot(mn, wv_ref[...].reshape(D, D)).astype(BF)

    return pl.pallas_call(
        body, name=f"kv_proj{layer}", grid=(1,),
        in_specs=[_full((NMEM, D)), _full((1, D)),
                  _const((NSH, 256, D), (0, _row_item(f"xa_w_k{layer}"), 0)),
                  _const((NSH, 256, D), (0, _row_item(f"xa_w_v{layer}"), 0))],
        out_specs=[_full((NMEM, D)), _full((NMEM, D))],
        out_shape=[_sds((NMEM, D), BF), _sds((NMEM, D), BF)],
        compiler_params=_params(),
    )(mem, g, gr, gr)


def _softmax_rows(sc):
    m = jnp.max(sc, axis=-1, keepdims=True)
    e = jnp.exp(sc - m)
    return e / jnp.sum(e, axis=-1, keepdims=True)


def _xattn_fwd(h, g, k, v, gr, layer, tm):
    s = h.shape[0]

    def body(h_ref, g_ref, k_ref, v_ref, wq_ref, wo_ref, ho_ref, q_ref, o_scr):
        hh = h_ref[...]
        hhat, _ = _rms(hh)
        n = (hhat * g_ref[...]).astype(BF)
        q = _dot(n, wq_ref[...].reshape(D, D)).astype(BF)
        q_ref[...] = q
        for hd in range(HEADS):
            cs = slice(HD * hd, HD * (hd + 1))
            p = _softmax_rows(_dot_nt(q[:, cs], k_ref[:, cs]) * ATT_SCALE)
            o_scr[:, cs] = _dot(p.astype(BF), v_ref[:, cs]).astype(BF)
        ho_ref[...] = hh + _dot(o_scr[...], wo_ref[...].reshape(D, D))

    return pl.pallas_call(
        body, name=f"xattn_fwd{layer}", grid=(s // tm,),
        in_specs=[_rows(tm, D), _const((1, D), (0, 0)), _const((NMEM, D), (0, 0)), _const((NMEM, D), (0, 0)),
                  _const((NSH, 256, D), (0, _row_item(f"xa_w_q{layer}"), 0)),
                  _const((NSH, 256, D), (0, _row_item(f"xa_w_o{layer}"), 0))],
        out_specs=[_rows(tm, D), _rows(tm, D)],
        out_shape=[_sds((s, D), F32), _sds((s, D), BF)],
        scratch_shapes=[pltpu.VMEM((tm, D), BF)],
        compiler_params=_params(),
    )(h, g, k, v, gr, gr)


def _xattn_bwd(h, dh, q, g, k, v, gr, gpr, layer, tm):
    s = h.shape[0]

    def body(h_ref, dh_ref, q_ref, g_ref, k_ref, v_ref, wq_ref, wo_ref, gpr_in,
             dhi_ref, dk_ref, dv_ref, dg_ref, gqo_ref, o_scr, dq_scr, qacc, oacc):
        i = pl.program_id(0)

        @pl.when(_first(i))
        def _():
            qacc[...] = jnp.zeros_like(qacc)
            oacc[...] = jnp.zeros_like(oacc)
            dk_ref[...] = jnp.zeros_like(dk_ref)
            dv_ref[...] = jnp.zeros_like(dv_ref)
            dg_ref[...] = jnp.zeros_like(dg_ref)

        gg = g_ref[...]
        dho = dh_ref[...]
        dhb = dho.astype(BF)
        q = q_ref[...]
        do = _dot_nt(dhb, wo_ref[...].reshape(D, D)).astype(BF)
        for hd in range(HEADS):
            cs = slice(HD * hd, HD * (hd + 1))
            kh = k_ref[:, cs]
            vh = v_ref[:, cs]
            p = _softmax_rows(_dot_nt(q[:, cs], kh) * ATT_SCALE)
            pb = p.astype(BF)
            o_scr[:, cs] = _dot(pb, vh).astype(BF)
            doh = do[:, cs]
            dp = _dot_nt(doh, vh)
            dv_ref[:, cs] += _dot_tn(pb, doh)
            ds = (p * (dp - jnp.sum(dp * p, axis=-1, keepdims=True)) * ATT_SCALE).astype(BF)
            dq_scr[:, cs] = _dot(ds, kh).astype(BF)
            dk_ref[:, cs] += _dot_tn(ds, q[:, cs])
        oacc[...] += _dot_tn(o_scr[...], dhb)
        hhat, rr = _rms(h_ref[...])
        n = (hhat * gg).astype(BF)
        dq = dq_scr[...]
        qacc[...] += _dot_tn(n, dq)
        dn = _dot_nt(dq, wq_ref[...].reshape(D, D))
        dg_ref[...] += _colsum(dn * hhat)
        dhi_ref[...] = dho + _rms_bwd(dn, hhat, rr, gg)

        @pl.when(_last(i))
        def _():
            gqo_ref[:, 0:256, :] = qacc[...].astype(BF).reshape(NSH, 256, D)
            gqo_ref[:, 256:512, :] = oacc[...].astype(BF).reshape(NSH, 256, D)

    qs = _const((NSH, 256, D), (0, _row_item(f"xa_w_q{layer}"), 0))
    os_ = _const((NSH, 256, D), (0, _row_item(f"xa_w_o{layer}"), 0))
    kvs = _const((NMEM, D), (0, 0))
    return pl.pallas_call(
        body, name=f"xattn_bwd{layer}", grid=(s // tm,),
        in_specs=[_rows(tm, D), _rows(tm, D), _rows(tm, D), _const((1, D), (0, 0)), kvs, kvs, qs, os_, ANY],
        out_specs=[_rows(tm, D), kvs, kvs, _const((1, D), (0, 0)),
                   _const((NSH, 512, D), (0, _row_pair(f"xa_w_q{layer}"), 0))],
        out_shape=[_sds((s, D), F32), _sds((NMEM, D), F32), _sds((NMEM, D), F32), _sds((1, D), F32),
                   _sds(gpr.shape, BF)],
        scratch_shapes=[pltpu.VMEM((tm, D), BF), pltpu.VMEM((tm, D), BF), pltpu.VMEM((D, D), F32),
                        pltpu.VMEM((D, D), F32)],
        input_output_aliases={8: 4},
        compiler_params=_params(),
    )(h, dh, q, g, k, v, gr, gr, gpr)


def _kv_proj_bwd(mem, g, dk, dv, gr, gpr, layer):
    def body(mem_ref, g_ref, dk_ref, dv_ref, wk_ref, wv_ref, gpr_in, dg_ref, gkv_ref):
        mhat, _ = _rms(mem_ref[...])
        mn = (mhat * g_ref[...]).astype(BF)
        dkb = dk_ref[...].astype(BF)
        dvb = dv_ref[...].astype(BF)
        gkv_ref[:, 0:256, :] = _dot_tn(mn, dkb).astype(BF).reshape(NSH, 256, D)
        gkv_ref[:, 256:512, :] = _dot_tn(mn, dvb).astype(BF).reshape(NSH, 256, D)
        dmn = _dot_nt(dkb, wk_ref[...].reshape(D, D)) + _dot_nt(dvb, wv_ref[...].reshape(D, D))
        dg_ref[...] = _colsum(dmn * mhat)

    return pl.pallas_call(
        body, name=f"kv_proj_bwd{layer}", grid=(1,),
        in_specs=[_full((NMEM, D)), _full((1, D)), _full((NMEM, D)), _full((NMEM, D)),
                  _const((NSH, 256, D), (0, _row_item(f"xa_w_k{layer}"), 0)),
                  _const((NSH, 256, D), (0, _row_item(f"xa_w_v{layer}"), 0)), ANY],
        out_specs=[_full((1, D)), _const((NSH, 512, D), (0, _row_pair(f"xa_w_k{layer}"), 0))],
        out_shape=[_sds((1, D), F32), _sds(gpr.shape, BF)],
        input_output_aliases={6: 1},
        compiler_params=_params(),
    )(mem, g, dk, dv, gr, gr, gpr)


def _ffn_fwd(h, g, wg, wu, gr, layer, tm):
    s = h.shape[0]

    def body(h_ref, g_ref, wg_ref, wu_ref, wd_ref, ho_ref, gt_ref, up_ref):
        hh = h_ref[...]
        hhat, _ = _rms(hh)
        n = (hhat * g_ref[...]).astype(BF)
        gt = _dot(n, wg_ref[...])
        up = _dot(n, wu_ref[...])
        gt_ref[...] = gt.astype(BF)
        up_ref[...] = up.astype(BF)
        act = (gt * _sigmoid(gt) * up).astype(BF)
        ho_ref[...] = hh + _dot(act, wd_ref[...].reshape(DFF, D))

    return pl.pallas_call(
        body, name=f"ffn_fwd{layer}", grid=(s // tm,),
        in_specs=[_rows(tm, D), _const((1, D), (0, 0)), _const((D, DFF), (0, 0)), _const((D, DFF), (0, 0)),
                  _const((NSH, FSH, D), (0, layer, 0))],
        out_specs=[_rows(tm, D), _rows(tm, DFF), _rows(tm, DFF)],
        out_shape=[_sds((s, D), F32), _sds((s, DFF), BF), _sds((s, DFF), BF)],
        compiler_params=_params(),
    )(h, g, wg, wu, gr)


def _ffn_bwd_dx(h, dh, gt, up, g, wg, wu, gr, layer, tm):
    s = h.shape[0]

    def body(h_ref, dh_ref, gt_ref, up_ref, g_ref, wg_ref, wu_ref, wd_ref, dhi_ref, dg_ref, dgt_ref, dup_ref):
        i = pl.program_id(0)

        @pl.when(_first(i))
        def _():
            dg_ref[...] = jnp.zeros_like(dg_ref)

        gg = g_ref[...]
        dho = dh_ref[...]
        dhb = dho.astype(BF)
        for half in range(2):
            cs = slice(FH * half, FH * (half + 1))
            gtv = gt_ref[:, cs].astype(F32)
            upv = up_ref[:, cs].astype(F32)
            sg = _sigmoid(gtv)
            dact = _dot_nt(dhb, wd_ref[2 * half:2 * half + 2].reshape(FH, D))
            dgt_ref[:, cs] = (dact * upv * (sg * (1.0 + gtv * (1.0 - sg)))).astype(BF)
            dup_ref[:, cs] = (dact * gtv * sg).astype(BF)
        dn = _dot_nt(dgt_ref[...], wg_ref[...]) + _dot_nt(dup_ref[...], wu_ref[...])
        hhat, rr = _rms(h_ref[...])
        dg_ref[...] += _colsum(dn * hhat)
        dhi_ref[...] = dho + _rms_bwd(dn, hhat, rr, gg)

    return pl.pallas_call(
        body, name=f"ffn_bwd_dx{layer}", grid=(s // tm,),
        in_specs=[_rows(tm, D), _rows(tm, D), _rows(tm, DFF), _rows(tm, DFF), _const((1, D), (0, 0)),
                  _const((D, DFF), (0, 0)), _const((D, DFF), (0, 0)), _const((NSH, FSH, D), (0, layer, 0))],
        out_specs=[_rows(tm, D), _const((1, D), (0, 0)), _rows(tm, DFF), _rows(tm, DFF)],
        out_shape=[_sds((s, D), F32), _sds((1, D), F32), _sds((s, DFF), BF), _sds((s, DFF), BF)],
        compiler_params=_params(),
    )(h, dh, gt, up, g, wg, wu, gr)


def _ffn_bwd_dwgu(h, g, dgt, dup, gpc, layer, tm):
    s = h.shape[0]

    def body(h_ref, g_ref, dgt_ref, dup_ref, gpc_in, ggu_ref, gacc, uacc):
        i = pl.program_id(0)

        @pl.when(_first(i))
        def _():
            gacc[...] = jnp.zeros_like(gacc)
            uacc[...] = jnp.zeros_like(uacc)

        hhat, _ = _rms(h_ref[...])
        n = (hhat * g_ref[...]).astype(BF)
        gacc[...] += _dot_tn(n, dgt_ref[...])
        uacc[...] += _dot_tn(n, dup_ref[...])

        @pl.when(_last(i))
        def _():
            pad = jnp.zeros((D, FPAD - FSH), BF)
            for k in range(NSH):
                ggu_ref[k, :, 0:FSH] = gacc[:, FSH * k:FSH * (k + 1)].astype(BF)
                ggu_ref[k, :, FSH:FPAD] = pad
                ggu_ref[k, :, FPAD:FPAD + FSH] = uacc[:, FSH * k:FSH * (k + 1)].astype(BF)
                ggu_ref[k, :, FPAD + FSH:2 * FPAD] = pad

    return pl.pallas_call(
        body, name=f"ffn_bwd_dwgu{layer}", grid=(s // tm,),
        in_specs=[_rows(tm, D), _const((1, D), (0, 0)), _rows(tm, DFF), _rows(tm, DFF), ANY],
        out_specs=_const((NSH, D, 2 * FPAD), (0, 0, layer)),
        out_shape=_sds(gpc.shape, BF),
        scratch_shapes=[pltpu.VMEM((D, DFF), F32), pltpu.VMEM((D, DFF), F32)],
        input_output_aliases={4: 0},
        compiler_params=_params(),
    )(h, g, dgt, dup, gpc)


def _ffn_bwd_dwd(gt, up, dh, gpr, layer, tm):
    s = dh.shape[0]

    def body(gt_ref, up_ref, dh_ref, gpr_in, gd_ref, dacc):
        i = pl.program_id(0)

        @pl.when(_first(i))
        def _():
            dacc[...] = jnp.zeros_like(dacc)

        gtv = gt_ref[...].astype(F32)
        act = (gtv * _sigmoid(gtv) * up_ref[...].astype(F32)).astype(BF)
        dacc[...] += _dot_tn(act, dh_ref[...].astype(BF))

        @pl.when(_last(i))
        def _():
            gd_ref[...] = dacc[...].astype(BF).reshape(NSH, FSH, D)

    return pl.pallas_call(
        body, name=f"ffn_bwd_dwd{layer}", grid=(s // tm,),
        in_specs=[_rows(tm, DFF), _rows(tm, DFF), _rows(tm, D), ANY],
        out_specs=_const((NSH, FSH, D), (0, layer, 0)),
        out_shape=_sds(gpr.shape, BF),
        scratch_shapes=[pltpu.VMEM((DFF, D), F32)],
        input_output_aliases={3: 0},
        compiler_params=_params(),
    )(gt, up, dh, gpr)


def _sgu_fwd(h, g, lng, lnb, ws, bsb, wod, gr, tm):
    s = h.shape[0]

    def body(h_ref, g_ref, lng_ref, lnb_ref, ws_ref, bsb_ref, wi_ref, wo_ref, ho_ref, zp_ref, y_scr):
        hh = h_ref[...]
        hhat, _ = _rms(hh)
        n = (hhat * g_ref[...]).astype(BF)
        zpre = _dot(n, wi_ref[...])
        zp_ref[...] = zpre.astype(BF)
        z = _gelu(zpre)
        u = z[:, :D]
        y, _ = _ln(z[:, D:])
        vn = (y * lng_ref[...] + lnb_ref[...]).astype(BF)
        for c in range(tm // CHUNK):
            rs = slice(CHUNK * c, CHUNK * (c + 1))
            for gi in range(GROUPS):
                cs = slice(CHUNK * gi, CHUNK * (gi + 1))
                sv = _dot(ws_ref[gi], vn[rs, cs]) + bsb_ref[gi]
                y_scr[rs, cs] = (u[rs, cs] * sv).astype(BF)
        ho_ref[...] = hh + _dot(y_scr[...], wo_ref[...].reshape(D, D))

    vec = _const((1, D), (0, 0))
    sq = _const((GROUPS, CHUNK, CHUNK), (0, 0, 0))
    return pl.pallas_call(
        body, name="sgu_fwd", grid=(s // tm,),
        in_specs=[_rows(tm, D), vec, vec, vec, sq, sq, _const((D, 2 * D), (0, 0)),
                  _const((NSH, 256, D), (0, _row_item("od_w_out"), 0))],
        out_specs=[_rows(tm, D), _rows(tm, 2 * D)],
        out_shape=[_sds((s, D), F32), _sds((s, 2 * D), BF)],
        scratch_shapes=[pltpu.VMEM((tm, D), BF)],
        compiler_params=_params(),
    )(h, g, lng, lnb, ws, bsb, wod, gr)


def _sgu_bwd(h, dh, zpre, g, lng, lnb, ws, wst, bsb, wod, gr, gpc, gpr, tm):
    s = h.shape[0]

    def body(h_ref, dh_ref, zp_ref, g_ref, lng_ref, lnb_ref, ws_ref, wst_ref, bsb_ref, wi_ref, wo_ref,
             gpc_in, gpr_in,
             dhi_ref, dws_ref, dbs_ref, dlng_ref, dlnb_ref, dg_ref, gwi_ref, gwo_ref,
             y_scr, dz_scr, dvn_scr, iacc, oacc, bacc):
        i = pl.program_id(0)

        @pl.when(_first(i))
        def _():
            iacc[...] = jnp.zeros_like(iacc)
            oacc[...] = jnp.zeros_like(oacc)
            bacc[...] = jnp.zeros_like(bacc)
            dws_ref[...] = jnp.zeros_like(dws_ref)
            dlng_ref[...] = jnp.zeros_like(dlng_ref)
            dlnb_ref[...] = jnp.zeros_like(dlnb_ref)
            dg_ref[...] = jnp.zeros_like(dg_ref)

        gg = g_ref[...]
        dho = dh_ref[...]
        dhb = dho.astype(BF)
        zpre_v = zp_ref[...].astype(F32)
        z = _gelu(zpre_v)
        u = z[:, :D]
        yl, rstd = _ln(z[:, D:])
        vn = (yl * lng_ref[...] + lnb_ref[...]).astype(BF)
        dy = _dot_nt(dhb, wo_ref[...].reshape(D, D))
        for c in range(tm // CHUNK):
            rs = slice(CHUNK * c, CHUNK * (c + 1))
            for gi in range(GROUPS):
                cs = slice(CHUNK * gi, CHUNK * (gi + 1))
                vb = vn[rs, cs]
                sv = _dot(ws_ref[gi], vb) + bsb_ref[gi]
                ub = u[rs, cs]
                dyb = dy[rs, cs]
                y_scr[rs, cs] = (ub * sv).astype(BF)
                dz_scr[rs, cs] = dyb * sv
                dsv = dyb * ub
                bacc[gi] += dsv
                dsvb = dsv.astype(BF)
                dws_ref[gi] += _dot_nt(dsvb, vb)
                dvn_scr[rs, cs] = _dot(wst_ref[gi], dsvb)
        oacc[...] += _dot_tn(y_scr[...], dhb)
        dvn = dvn_scr[...]
        dlng_ref[...] += _colsum(dvn * yl)
        dlnb_ref[...] += _colsum(dvn)
        dz_scr[:, D:] = _ln_bwd(dvn * lng_ref[...], yl, rstd)
        dzb = (dz_scr[...] * _gelu_grad(zpre_v)).astype(BF)
        hhat, rr = _rms(h_ref[...])
        n = (hhat * gg).astype(BF)
        iacc[...] += _dot_tn(n, dzb)
        dn = _dot_nt(dzb, wi_ref[...])
        dg_ref[...] += _colsum(dn * hhat)
        dhi_ref[...] = dho + _rms_bwd(dn, hhat, rr, gg)

        @pl.when(_last(i))
        def _():
            for k in range(NSH):
                gwi_ref[k] = iacc[:, 512 * k:512 * (k + 1)].astype(BF)
            gwo_ref[...] = oacc[...].astype(BF).reshape(NSH, 256, D)
            dbs_ref[...] = jnp.sum(bacc[...], axis=-1)

    vec = _const((1, D), (0, 0))
    sq = _const((GROUPS, CHUNK, CHUNK), (0, 0, 0))
    wi_spec = _const((NSH, D, 512), (0, 0, COL_OD // 512))
    wo_spec = _const((NSH, 256, D), (0, _row_item("od_w_out"), 0))
    return pl.pallas_call(
        body, name="sgu_bwd", grid=(s // tm,),
        in_specs=[_rows(tm, D), _rows(tm, D), _rows(tm, 2 * D), vec, vec, vec, sq, sq, sq,
                  _const((D, 2 * D), (0, 0)), wo_spec, ANY, ANY],
        out_specs=[_rows(tm, D), sq, _const((GROUPS, CHUNK), (0, 0)), vec, vec, vec, wi_spec, wo_spec],
        out_shape=[_sds((s, D), F32), _sds((GROUPS, CHUNK, CHUNK), F32), _sds((GROUPS, CHUNK), F32),
                   _sds((1, D), F32), _sds((1, D), F32), _sds((1, D), F32), _sds(gpc.shape, BF),
                   _sds(gpr.shape, BF)],
        scratch_shapes=[pltpu.VMEM((tm, D), BF), pltpu.VMEM((tm, 2 * D), F32),
                        pltpu.VMEM((tm, D), F32), pltpu.VMEM((D, 2 * D), F32), pltpu.VMEM((D, D), F32),
                        pltpu.VMEM((GROUPS, CHUNK, CHUNK), F32)],
        input_output_aliases={11: 6, 12: 7},
        compiler_params=_params(),
    )(h, dh, zpre, g, lng, lnb, ws, wst, bsb, wod, gr, gpc, gpr)


def _final_loss(h, tgt, g, tm):
    s = h.shape[0]

    def body(h_ref, t_ref, g_ref, dh_ref, loss_ref, dg_ref):
        i = pl.program_id(0)

        @pl.when(_first(i))
        def _():
            loss_ref[...] = jnp.zeros_like(loss_ref)
            dg_ref[...] = jnp.zeros_like(dg_ref)

        gg = g_ref[...]
        hhat, rr = _rms(h_ref[...])
        diff = hhat * gg - t_ref[...]
        loss_ref[...] += jnp.sum(diff * diff) * (0.5 / D)
        dout = diff * (1.0 / D)
        dg_ref[...] += _colsum(dout * hhat)
        dh_ref[...] = _rms_bwd(dout, hhat, rr, gg)

    return pl.pallas_call(
        body, name="final_loss", grid=(s // tm,),
        in_specs=[_rows(tm, D), _rows(tm, D), _const((1, D), (0, 0))],
        out_specs=[_rows(tm, D), _const((1, 128), (0, 0)), _const((1, D), (0, 0))],
        out_shape=[_sds((s, D), F32), _sds((1, 128), F32), _sds((1, D), F32)],
        compiler_params=_params(),
    )(h, tgt, g)


def _whole(a):
    return _const(a.shape, (0,) * a.ndim)


def _own_plane(shape):
    return pl.BlockSpec((1,) + shape, lambda i, chip_ref: (chip_ref[0], 0, 0), pipeline_mode=pl.Buffered(1))


def _pack_rows(chip_idx, w_down, xq, xo, xk, xv, ev_out, od_out):
    def body(chip_ref, wd_ref, xq_ref, xo_ref, xk_ref, xv_ref, ev_ref, od_ref, out_ref):
        out_ref[0, 0:FSH, :] = wd_ref[0].astype(BF)
        out_ref[0, FSH:2 * FSH, :] = wd_ref[1].astype(BF)
        out_ref[0, 2 * FSH:ROW_ITEM0, :] = jnp.zeros((ROW_ITEM0 - 2 * FSH, D), BF)
        srcs = {"xa_w_q": xq_ref, "xa_w_o": xo_ref, "xa_w_k": xk_ref, "xa_w_v": xv_ref}
        for j, name in enumerate(ROW_ITEMS):
            r0 = ROW_ITEM0 + 256 * j
            if name == "ev_w_out":
                val = ev_ref[0]
            elif name == "od_w_out":
                val = od_ref[0]
            else:
                val = srcs[name[:-1]][int(name[-1])]
            out_ref[0, r0:r0 + 256, :] = val.astype(BF)

    ins = (w_down, xq, xo, xk, xv, ev_out, od_out)
    grid_spec = pltpu.PrefetchScalarGridSpec(
        num_scalar_prefetch=1, grid=(1,), in_specs=[_whole(a) for a in ins], out_specs=_own_plane((ROW_TOTAL, D)))
    return pl.pallas_call(
        body, name="pack_rows", grid_spec=grid_spec, out_shape=_sds((NSH, ROW_TOTAL, D), BF),
        compiler_params=_params(),
    )(chip_idx, *ins)


def _pack_cols(chip_idx, w_gate, w_up, od_in, ev_in, conv_a, conv_b, ln_g, ln_b):
    def body(chip_ref, wg_ref, wu_ref, od_ref, ev_ref, ca_ref, cb_ref, lg_ref, lb_ref, out_ref, sp_ref):
        pad = jnp.zeros((D, FPAD - FSH), BF)
        for layer in range(2):
            for j, ref in enumerate((wg_ref, wu_ref)):
                c0 = (2 * layer + j) * FPAD
                out_ref[0, :, c0:c0 + FSH] = ref[layer].astype(BF)
                out_ref[0, :, c0 + FSH:c0 + FPAD] = pad
        out_ref[0, :, COL_OD:COL_OD + 512] = od_ref[0].astype(BF)
        out_ref[0, :, COL_OD + 512:COL_EV] = jnp.zeros((D, COL_EV - COL_OD - 512), BF)
        out_ref[0, :, COL_EV:COL_TOTAL] = ev_ref[0].astype(BF)
        sp_ref[...] = jnp.zeros_like(sp_ref)
        sp_ref[0, 0:KA, 0:128] = ca_ref[0]
        sp_ref[0, 32:32 + KB, 0:128] = cb_ref[0]
        sp_ref[0, 40:41, :] = lg_ref[...]
        sp_ref[0, 41:42, :] = lb_ref[...]

    ins = (w_gate, w_up, od_in, ev_in, conv_a, conv_b, ln_g, ln_b)
    grid_spec = pltpu.PrefetchScalarGridSpec(
        num_scalar_prefetch=1, grid=(1,), in_specs=[_whole(a) for a in ins],
        out_specs=[_own_plane((D, COL_TOTAL)), _own_plane((SP_ROWS, 256))])
    return pl.pallas_call(
        body, name="pack_cols", grid_spec=grid_spec,
        out_shape=[_sds((NSH, D, COL_TOTAL), BF), _sds((NSH, SP_ROWS, 256), F32)],
        compiler_params=_params(),
    )(chip_idx, *ins)


def _assemble_cols(gc):
    tr = 256

    def body(g_ref, wg0, wu0, wg1, wu1, wod, wev):
        for k in range(NSH):
            for j, ref in enumerate((wg0, wu0, wg1, wu1)):
                ref[:, FSH * k:FSH * (k + 1)] = g_ref[k, :, j * FPAD:j * FPAD + FSH]
            wod[:, 512 * k:512 * (k + 1)] = g_ref[k, :, COL_OD:COL_OD + 512]
            wev[:, 640 * k:640 * (k + 1)] = g_ref[k, :, COL_EV:COL_TOTAL]

    ffn = pl.BlockSpec((tr, DFF), lambda i: (i, 0))
    return pl.pallas_call(
        body, name="assemble_cols", grid=(D // tr,),
        in_specs=[pl.BlockSpec((NSH, tr, COL_TOTAL), lambda i: (0, i, 0))],
        out_specs=[ffn, ffn, ffn, ffn, pl.BlockSpec((tr, 2 * D), lambda i: (i, 0)),
                   pl.BlockSpec((tr, DZ), lambda i: (i, 0))],
        out_shape=[_sds((D, DFF), BF)] * 4 + [_sds((D, 2 * D), BF), _sds((D, DZ), BF)],
        compiler_params=_params(),
    )(gc)


def _pos():
    return lax.axis_index("x"), lax.axis_index("y"), lax.axis_index("c")


def _other_chips(x, y):
    return [(1 - x, y), (x, 1 - y), (1 - x, 1 - y)]


def _half(ref, hc, lead=()):
    r = ref.shape[-2] // 2
    return ref.at[(*lead, pl.ds(pl.multiple_of(hc * r, 8), r), slice(None))]


def _rcopy(src, dst, ssem, rsem, to):
    return pltpu.make_async_remote_copy(src_ref=src, dst_ref=dst, send_sem=ssem, recv_sem=rsem,
                                        device_id=to, device_id_type=MESH)


def _gather_packs(bufs):
    n = len(bufs)

    def body(*refs):
        dst = refs[n:2 * n]
        ici_s, ici_r, d2d_s, d2d_r = refs[2 * n:]
        x, y, c = _pos()
        me = 2 * x + y
        sib = (x, y, 1 - c)
        chips = _other_chips(x, y)
        first = []
        for k, (cx, cy) in enumerate(chips):
            for t in range(n):
                mine = _half(dst[t], c, (me,))
                cp = _rcopy(mine, mine, ici_s.at[3 * t + k], ici_r.at[3 * t + k], (cx, cy, c))
                cp.start()
                first.append(cp)
        passed = []
        for k, (cx, cy) in enumerate(chips):
            them = 2 * cx + cy
            for t in range(n):
                landed = _half(dst[t], c, (them,))
                _rcopy(landed, landed, ici_s.at[3 * t + k], ici_r.at[3 * t + k], (cx, cy, c)).wait_recv()
                cp = _rcopy(landed, landed, d2d_s.at[3 * t + k], d2d_r.at[3 * t + k], sib)
                cp.start()
                passed.append(cp)
        for k, (cx, cy) in enumerate(chips):
            them = 2 * cx + cy
            for t in range(n):
                got = _half(dst[t], 1 - c, (them,))
                _rcopy(got, got, d2d_s.at[3 * t + k], d2d_r.at[3 * t + k], sib).wait_recv()
        for cp in first + passed:
            cp.wait_send()

    return pl.pallas_call(
        body, name="gather_packs",
        in_specs=[ANY] * n, out_specs=[ANY] * n,
        out_shape=[_sds(a.shape, a.dtype) for a in bufs],
        input_output_aliases={t: t for t in range(n)},
        scratch_shapes=[pltpu.SemaphoreType.DMA((3 * n,))] * 4,
    )(*bufs)


def _sibling_swap(items):
    n = len(items)

    def body(*refs):
        src = refs[:n]
        dst = refs[n:2 * n]
        ssem, rsem = refs[2 * n:]
        x, y, c = _pos()
        sib = (x, y, 1 - c)
        cps = []
        for t in range(n):
            lead = (slice(None),) * (len(src[t].shape) - 2)
            cp = _rcopy(_half(src[t], 1 - c, lead), dst[t], ssem.at[t], rsem.at[t], sib)
            cp.start()
            cps.append(cp)
        for cp in cps:
            cp.wait()

    return pl.pallas_call(
        body, name="rs_sibling_swap",
        in_specs=[ANY] * n, out_specs=[ANY] * n,
        out_shape=[_sds(a.shape[:-2] + (a.shape[-2] // 2, a.shape[-1]), a.dtype) for a in items],
        scratch_shapes=[pltpu.SemaphoreType.DMA((n,)), pltpu.SemaphoreType.DMA((n,))],
    )(*items)


def _chip_exchange(big, small):
    items = tuple(big) + tuple(small)
    n = len(items)
    nb = len(big)

    def body(*refs):
        src = refs[:n]
        dst = refs[n:2 * n]
        ssem, rsem = refs[2 * n:]
        x, y, c = _pos()
        cps = []
        for k, (cx, cy) in enumerate(_other_chips(x, y)):
            for t in range(n):
                s = src[t].at[2 * cx + cy] if t < nb else src[t]
                cp = _rcopy(s, dst[t].at[k], ssem.at[3 * t + k], rsem.at[3 * t + k], (cx, cy, c))
                cp.start()
                cps.append(cp)
        for cp in cps:
            cp.wait()

    return pl.pallas_call(
        body, name="rs_chip_exchange",
        in_specs=[ANY] * n, out_specs=[ANY] * n,
        out_shape=[_sds((3,) + a.shape[-2:], a.dtype) for a in items],
        scratch_shapes=[pltpu.SemaphoreType.DMA((3 * n,)), pltpu.SemaphoreType.DMA((3 * n,))],
    )(*items)


def _sibling_join(items):
    n = len(items)

    def body(*refs):
        dst = refs[n:2 * n]
        ssem, rsem = refs[2 * n:]
        x, y, c = _pos()
        sib = (x, y, 1 - c)
        cps = []
        for t in range(n):
            mine = _half(dst[t], c)
            cp = _rcopy(mine, mine, ssem.at[t], rsem.at[t], sib)
            cp.start()
            cps.append(cp)
        for t, cp in enumerate(cps):
            theirs = _half(dst[t], 1 - c)
            _rcopy(theirs, theirs, ssem.at[t], rsem.at[t], sib).wait_recv()
            cp.wait_send()

    return pl.pallas_call(
        body, name="rs_sibling_join",
        in_specs=[ANY] * n, out_specs=[ANY] * n,
        out_shape=[_sds(a.shape, a.dtype) for a in items],
        input_output_aliases={t: t for t in range(n)},
        scratch_shapes=[pltpu.SemaphoreType.DMA((n,))] * 2,
    )(*items)


def _blocks_of(rows, cols, itemsize):
    rb = rows
    while rb * cols * itemsize > (2 << 20) and rb % 2 == 0 and (rb // 2) % 16 == 0:
        rb //= 2
    return rb


def _add_half(full, recv, c_idx, name):
    p, r, cols = full.shape
    rb = _blocks_of(r // 2, cols, 4)
    nb = (r // 2) // rb

    def body(c_ref, a_ref, b_ref, o_ref):
        o_ref[...] = (a_ref[...].astype(F32) + b_ref[...].astype(F32)).astype(o_ref.dtype)

    grid_spec = pltpu.PrefetchScalarGridSpec(
        num_scalar_prefetch=1, grid=(p, nb),
        in_specs=[pl.BlockSpec((1, rb, cols), lambda j, i, c_ref: (j, c_ref[0] * nb + i, 0)),
                  pl.BlockSpec((1, rb, cols), lambda j, i, c_ref: (j, i, 0))],
        out_specs=pl.BlockSpec((1, rb, cols), lambda j, i, c_ref: (j, i, 0)))
    return pl.pallas_call(
        body, name=name, grid_spec=grid_spec, out_shape=_sds(recv.shape, full.dtype),
        compiler_params=_params(2),
    )(c_idx, full, recv)


def _sum_chips(own, got, plane_c, name):
    _, r, cols = own.shape
    rb = _blocks_of(r, cols, 4)
    nb = r // rb

    def body(s_ref, a_ref, b_ref, o_ref):
        f = lambda v: v.astype(F32)
        o_ref[...] = (f(a_ref[0]) + f(b_ref[1])) + (f(b_ref[0]) + f(b_ref[2]))

    grid_spec = pltpu.PrefetchScalarGridSpec(
        num_scalar_prefetch=1, grid=(nb,),
        in_specs=[pl.BlockSpec((1, rb, cols), lambda i, s_ref: (s_ref[0], i, 0)),
                  pl.BlockSpec((3, rb, cols), lambda i, s_ref: (0, i, 0))],
        out_specs=pl.BlockSpec((rb, cols), lambda i, s_ref: (s_ref[1] * nb + i, 0)))
    return pl.pallas_call(
        body, name=name, grid_spec=grid_spec, out_shape=_sds((2 * r, cols), F32),
        compiler_params=_params(1),
    )(plane_c, own, got)


def _adamw_math(w, g, m, v):
    m = ADAM_B1 * m + (1.0 - ADAM_B1) * g
    v = ADAM_B2 * v + (1.0 - ADAM_B2) * (g * g)
    m_hat = m / (1.0 - ADAM_B1 ** ADAM_STEP)
    v_hat = v / (1.0 - ADAM_B2 ** ADAM_STEP)
    delta = -ADAM_LR * (m_hat / (jnp.sqrt(v_hat) + ADAM_EPS) + ADAM_WD * w)
    return delta, m, v


def _adamw_big(w, m, v, gsrc, gblock, gidx, name, rb):
    nl, r, cols = w.shape

    def body(w_ref, m_ref, v_ref, g_ref, go_ref, d_ref, mo_ref, vo_ref):
        g = g_ref[:, 0:cols]
        delta, mn, vn = _adamw_math(w_ref[0], g, m_ref[0], v_ref[0])
        go_ref[0] = g
        d_ref[0] = delta
        mo_ref[0] = mn
        vo_ref[0] = vn

    wspec = pl.BlockSpec((1, rb, cols), lambda l, i: (l, i, 0))
    return pl.pallas_call(
        body, name=name, grid=(nl, r // rb),
        in_specs=[wspec, wspec, wspec, pl.BlockSpec(gblock, gidx)],
        out_specs=[wspec] * 4, out_shape=[_sds(w.shape, F32)] * 4,
        compiler_params=_params(2),
    )(w, m, v, gsrc)


_SMALL = (
    ("g_mix", (2, D), "a", (0, 2), 0, False),
    ("g_xattn", (2, D), "a", (2, 4), 0, False),
    ("g_mem", (2, D), "a", (4, 6), 0, False),
    ("g_ffn", (2, D), "a", (6, 8), 0, False),
    ("g_final", (1, D), "a", (8, 9), 0, False),
    ("ev_a_conv_w", (KA, 128), "a", (16, 16 + KA), 0, True),
    ("ev_a_conv_b", (1, DA), "a", (11, 12), 0, False),
    ("ev_a_ln_g", (1, DA), "a", (11, 12), DA, False),
    ("ev_a_ln_b", (1, DA), "a", (12, 13), 0, False),
    ("ev_b_conv_w", (KB, 128), "a", (48, 48 + KB), 0, True),
    ("ev_b_conv_b", (1, DA), "a", (12, 13), DA, False),
    ("od_c_ln_g", (1, 256), "a", (9, 10), 0, True),
    ("od_c_ln_b", (1, 256), "a", (10, 11), 0, True),
    ("od_w_s", (GROUPS * CHUNK, CHUNK), "b", (0, GROUPS * CHUNK), 0, False),
    ("od_b_s", (GROUPS, CHUNK), "a", (56, 64), 0, False),
)


def _adamw_small(ga, gb, wmv):
    ns = len(_SMALL)

    def body(*refs):
        ga_ref, gb_ref = refs[0], refs[1]
        ins = refs[2:2 + 3 * ns]
        outs = refs[2 + 3 * ns:]
        x, y, _ = _pos()
        chip = 2 * x + y
        for j, (_, shape, pack, (r0, r1), col, sharded) in enumerate(_SMALL):
            src = ga_ref if pack == "a" else gb_ref
            width = shape[1]
            if sharded:
                g = jnp.zeros(shape, F32)
                for k in range(NSH):
                    blk = src[r0:r1, col + width * k:col + width * (k + 1)]
                    g = g + jnp.where(chip == k, blk, 0.0)
            else:
                g = src[r0:r1, col:col + width]
            w_ref, m_ref, v_ref = ins[3 * j:3 * j + 3]
            delta, mn, vn = _adamw_math(w_ref[...], g, m_ref[...], v_ref[...])
            go_ref, d_ref, mo_ref, vo_ref = outs[4 * j:4 * j + 4]
            go_ref[...] = g
            d_ref[...] = delta
            mo_ref[...] = mn
            vo_ref[...] = vn

    vm = pl.BlockSpec(memory_space=pltpu.VMEM)
    flat = [a for trio in wmv for a in trio]
    out_shape = []
    for (_, shape, *_rest) in _SMALL:
        out_shape += [_sds(shape, F32)] * 4
    outs = pl.pallas_call(
        body, name="adamw_small",
        in_specs=[vm] * (2 + 3 * ns), out_specs=[vm] * (4 * ns), out_shape=out_shape,
    )(ga, gb, *flat)
    return [tuple(outs[4 * j:4 * j + 4]) for j in range(ns)]


def _local_fwd_bwd(x, mem, tgt, sm, gr, cols, gpr, gpc, tm_f, tm_b):
    wg = (cols["wg0"], cols["wg1"])
    wu = (cols["wu0"], cols["wu1"])
    row = lambda a, i: a[i:i + 1]

    z = _ev_in_fwd(x, row(sm["g_mix"], 0), cols["wev"], tm_f)
    h1, a2, c2, cat = _ev_mix_fwd(x, z, sm["wa"], sm["ba"], sm["lng_a"], sm["lnb_a"], sm["wb"], sm["bb"], gr, tm_f)
    hs = [x, h1]
    kvs, qs, gts, ups = [], [], [], []
    zpre = None
    for layer in range(2):
        if layer == 1:
            h, zpre = _sgu_fwd(hs[-1], row(sm["g_mix"], 1), sm["lng_c"], sm["lnb_c"], sm["ws"], sm["bsb"],
                               cols["wod"], gr, tm_f)
            hs.append(h)
        k, v = _kv_proj(mem, row(sm["g_mem"], layer), gr, layer)
        h, q = _xattn_fwd(hs[-1], row(sm["g_xattn"], layer), k, v, gr, layer, tm_f)
        hs.append(h)
        h, gt, up = _ffn_fwd(hs[-1], row(sm["g_ffn"], layer), wg[layer], wu[layer], gr, layer, tm_b)
        hs.append(h)
        kvs.append((k, v))
        qs.append(q)
        gts.append(gt)
        ups.append(up)
    dh, loss, dg_final = _final_loss(hs[6], tgt, sm["g_final"], tm_f)

    g = {"g_final": dg_final}
    dg_ffn, dg_xattn, dg_mem = [None, None], [None, None], [None, None]
    for layer in (1, 0):
        h_ffn, h_xa = hs[3 * layer + 2], hs[3 * layer + 1]
        dh_out = dh
        dh, dg_ffn[layer], dgt, dup = _ffn_bwd_dx(h_ffn, dh_out, gts[layer], ups[layer], row(sm["g_ffn"], layer),
                                                  wg[layer], wu[layer], gr, layer, tm_b)
        gpc = _ffn_bwd_dwgu(h_ffn, row(sm["g_ffn"], layer), dgt, dup, gpc, layer, tm_b)
        gpr = _ffn_bwd_dwd(gts[layer], ups[layer], dh_out, gpr, layer, tm_b)
        k, v = kvs[layer]
        dh, dk, dv, dg_xattn[layer], gpr = _xattn_bwd(h_xa, dh, qs[layer], row(sm["g_xattn"], layer), k, v, gr,
                                                      gpr, layer, tm_b)
        dg_mem[layer], gpr = _kv_proj_bwd(mem, row(sm["g_mem"], layer), dk, dv, gr, gpr, layer)
        if layer == 1:
            dh, g["od_w_s"], g["od_b_s"], g["od_c_ln_g"], g["od_c_ln_b"], dg_mix1, gpc, gpr = _sgu_bwd(
                hs[3], dh, zpre, row(sm["g_mix"], 1), sm["lng_c"], sm["lnb_c"], sm["ws"], sm["wst"], sm["bsb"],
                cols["wod"], gr, gpc, gpr, tm_b)
    da2, dc2, dgb, g["ev_a_ln_g"], g["ev_a_ln_b"], gpr = _ev_mix_bwd(dh, cat, a2, c2, z, sm["lng_a"], sm["lnb_a"],
                                                                     gr, gpr, tm_b)
    dx, g["ev_a_conv_w"], g["ev_a_conv_b"], g["ev_b_conv_w"], g["ev_b_conv_b"], dg_mix0, gpc = _ev_in_bwd(
        x, dh, z, da2, dc2, dgb, row(sm["g_mix"], 0), sm["wa"], sm["wb"], cols["wev"], gpc, tm_b)
    g["g_mix"] = jnp.concatenate([dg_mix0, dg_mix1], axis=0)
    g["g_xattn"] = jnp.concatenate(dg_xattn, axis=0)
    g["g_mem"] = jnp.concatenate(dg_mem, axis=0)
    g["g_ffn"] = jnp.concatenate(dg_ffn, axis=0)
    return loss, dx, g, gpr, gpc


def _small_grad_packs(g):
    z = lambda r, c: jnp.zeros((r, c), F32)
    wide = lambda a: jnp.concatenate([a, z(a.shape[0], D - a.shape[1])], axis=1)
    rows = [g["g_mix"], g["g_xattn"], g["g_mem"], g["g_ffn"], g["g_final"], g["od_c_ln_g"], g["od_c_ln_b"],
            jnp.concatenate([g["ev_a_conv_b"], g["ev_a_ln_g"]], axis=1),
            jnp.concatenate([g["ev_a_ln_b"], g["ev_b_conv_b"]], axis=1),
            z(3, D), wide(g["ev_a_conv_w"]), wide(g["ev_b_conv_w"]), wide(g["od_b_s"])]
    a = jnp.concatenate(rows, axis=0)
    assert a.shape == (SA_ROWS, D), a.shape
    return a, g["od_w_s"].reshape(GROUPS * CHUNK, CHUNK)


def kernel(x, mem, g_mix, g_xattn, g_mem, g_ffn, g_final, ev_w_in, ev_a_conv_w, ev_a_conv_b, ev_a_ln_g, ev_a_ln_b, ev_b_conv_w, ev_b_conv_b, ev_w_out, od_w_in, od_c_ln_g, od_c_ln_b, od_w_s, od_b_s, od_w_out, xa_w_q, xa_w_k, xa_w_v, xa_w_o, ffn_w_gate, ffn_w_up, ffn_w_down, loss_target, m_g_mix, m_g_xattn, m_g_mem, m_g_ffn, m_g_final, m_ev_w_in, m_ev_a_conv_w, m_ev_a_conv_b, m_ev_a_ln_g, m_ev_a_ln_b, m_ev_b_conv_w, m_ev_b_conv_b, m_ev_w_out, m_od_w_in, m_od_c_ln_g, m_od_c_ln_b, m_od_w_s, m_od_b_s, m_od_w_out, m_xa_w_q, m_xa_w_k, m_xa_w_v, m_xa_w_o, m_ffn_w_gate, m_ffn_w_up, m_ffn_w_down, v_g_mix, v_g_xattn, v_g_mem, v_g_ffn, v_g_final, v_ev_w_in, v_ev_a_conv_w, v_ev_a_conv_b, v_ev_a_ln_g, v_ev_a_ln_b, v_ev_b_conv_w, v_ev_b_conv_b, v_ev_w_out, v_od_w_in, v_od_c_ln_g, v_od_c_ln_b, v_od_w_s, v_od_b_s, v_od_w_out, v_xa_w_q, v_xa_w_k, v_xa_w_v, v_xa_w_o, v_ffn_w_gate, v_ffn_w_up, v_ffn_w_down):
    W = dict(g_mix=g_mix, g_xattn=g_xattn, g_mem=g_mem, g_ffn=g_ffn, g_final=g_final, ev_w_in=ev_w_in,
             ev_a_conv_w=ev_a_conv_w, ev_a_conv_b=ev_a_conv_b, ev_a_ln_g=ev_a_ln_g, ev_a_ln_b=ev_a_ln_b,
             ev_b_conv_w=ev_b_conv_w, ev_b_conv_b=ev_b_conv_b, ev_w_out=ev_w_out, od_w_in=od_w_in,
             od_c_ln_g=od_c_ln_g, od_c_ln_b=od_c_ln_b, od_w_s=od_w_s, od_b_s=od_b_s, od_w_out=od_w_out,
             xa_w_q=xa_w_q, xa_w_k=xa_w_k, xa_w_v=xa_w_v, xa_w_o=xa_w_o, ffn_w_gate=ffn_w_gate,
             ffn_w_up=ffn_w_up, ffn_w_down=ffn_w_down)
    M = dict(g_mix=m_g_mix, g_xattn=m_g_xattn, g_mem=m_g_mem, g_ffn=m_g_ffn, g_final=m_g_final, ev_w_in=m_ev_w_in,
             ev_a_conv_w=m_ev_a_conv_w, ev_a_conv_b=m_ev_a_conv_b, ev_a_ln_g=m_ev_a_ln_g, ev_a_ln_b=m_ev_a_ln_b,
             ev_b_conv_w=m_ev_b_conv_w, ev_b_conv_b=m_ev_b_conv_b, ev_w_out=m_ev_w_out, od_w_in=m_od_w_in,
             od_c_ln_g=m_od_c_ln_g, od_c_ln_b=m_od_c_ln_b, od_w_s=m_od_w_s, od_b_s=m_od_b_s, od_w_out=m_od_w_out,
             xa_w_q=m_xa_w_q, xa_w_k=m_xa_w_k, xa_w_v=m_xa_w_v, xa_w_o=m_xa_w_o, ffn_w_gate=m_ffn_w_gate,
             ffn_w_up=m_ffn_w_up, ffn_w_down=m_ffn_w_down)
    V = dict(g_mix=v_g_mix, g_xattn=v_g_xattn, g_mem=v_g_mem, g_ffn=v_g_ffn, g_final=v_g_final, ev_w_in=v_ev_w_in,
             ev_a_conv_w=v_ev_a_conv_w, ev_a_conv_b=v_ev_a_conv_b, ev_a_ln_g=v_ev_a_ln_g, ev_a_ln_b=v_ev_a_ln_b,
             ev_b_conv_w=v_ev_b_conv_w, ev_b_conv_b=v_ev_b_conv_b, ev_w_out=v_ev_w_out, od_w_in=v_od_w_in,
             od_c_ln_g=v_od_c_ln_g, od_c_ln_b=v_od_c_ln_b, od_w_s=v_od_w_s, od_b_s=v_od_b_s, od_w_out=v_od_w_out,
             xa_w_q=v_xa_w_q, xa_w_k=v_xa_w_k, xa_w_v=v_xa_w_v, xa_w_o=v_xa_w_o, ffn_w_gate=v_ffn_w_gate,
             ffn_w_up=v_ffn_w_up, ffn_w_down=v_ffn_w_down)
    order = list(W)
    cx, cy, cc = _pos()
    chip = (2 * cx + cy).astype(jnp.int32)

    chip_idx = jnp.reshape(chip, (1,))
    gr = _pack_rows(chip_idx, ffn_w_down, xa_w_q, xa_w_o, xa_w_k, xa_w_v, ev_w_out, od_w_out)
    gc, gs = _pack_cols(chip_idx, ffn_w_gate, ffn_w_up, od_w_in, ev_w_in, ev_a_conv_w, ev_b_conv_w, od_c_ln_g,
                        od_c_ln_b)
    gr, gc, gs = _gather_packs((gr, gc, gs))
    wg0, wu0, wg1, wu1, wod, wev = _assemble_cols(gc)
    cols = dict(wg0=wg0, wu0=wu0, wg1=wg1, wu1=wu1, wod=wod, wev=wev)
    unshard = lambda a: jnp.transpose(a, (1, 0, 2)).reshape(a.shape[1], NSH * a.shape[2])
    sm = dict(
        g_mix=g_mix, g_xattn=g_xattn, g_mem=g_mem, g_ffn=g_ffn, g_final=g_final.reshape(1, D),
        wa=unshard(gs[:, 0:32, 0:128]), ba=ev_a_conv_b, lng_a=ev_a_ln_g, lnb_a=ev_a_ln_b,
        wb=unshard(gs[:, 32:40, 0:128]), bb=ev_b_conv_b,
        lng_c=unshard(gs[:, 40:41, :]), lnb_c=unshard(gs[:, 41:42, :]),
        ws=od_w_s[0].astype(BF), wst=jnp.swapaxes(od_w_s[0], 1, 2).astype(BF),
        bsb=jnp.broadcast_to(od_b_s[0][:, :, None], (GROUPS, CHUNK, CHUNK)),
    )

    gpr = jnp.zeros((NSH, ROW_TOTAL, D), BF)
    gpc = jnp.zeros((NSH, D, COL_TOTAL), BF)
    loss, dx, g, gpr, gpc = _local_fwd_bwd(x[0], mem[0], loss_target[0], sm, gr, cols, gpr, gpc, TM_FWD, TM_BWD)
    loss = lax.psum(loss[0, 0], ("x", "y", "c"))

    ga, gb = _small_grad_packs(g)
    c_idx = jnp.reshape(cc, (1,)).astype(jnp.int32)
    chip_c = jnp.stack([chip, cc.astype(jnp.int32)])
    zero_c = jnp.stack([jnp.zeros((), jnp.int32), cc.astype(jnp.int32)])
    fulls = (gpr, gpc, ga[None], gb[None])
    recvs = _sibling_swap(fulls)
    part = [_add_half(f, r, c_idx, f"rs_add_half{t}") for t, (f, r) in enumerate(zip(fulls, recvs))]
    got = _chip_exchange(part[:2], (part[2][0], part[3][0]))
    halves = [_sum_chips(part[t], got[t], chip_c if t < 2 else zero_c, f"rs_sum_chips{t}") for t in range(4)]
    rr, rc, ra, rb_ = _sibling_join(halves)

    res = {}
    big = {
        "ffn_w_down": (rr, (FSH, D), lambda l, i: (l, 0), FSH),
        "xa_w_q": (rr, (256, D), lambda l, i: (_row_item("xa_w_q0") + 2 * l, 0), 256),
        "xa_w_o": (rr, (256, D), lambda l, i: (_row_item("xa_w_o0") + 2 * l, 0), 256),
        "xa_w_k": (rr, (256, D), lambda l, i: (_row_item("xa_w_k0") + 2 * l, 0), 256),
        "xa_w_v": (rr, (256, D), lambda l, i: (_row_item("xa_w_v0") + 2 * l, 0), 256),
        "ev_w_out": (rr, (256, D), lambda l, i: (_row_item("ev_w_out"), 0), 256),
        "od_w_out": (rr, (256, D), lambda l, i: (_row_item("od_w_out"), 0), 256),
        "ffn_w_gate": (rc, (256, FPAD), lambda l, i: (i, 2 * l), 256),
        "ffn_w_up": (rc, (256, FPAD), lambda l, i: (i, 2 * l + 1), 256),
        "od_w_in": (rc, (256, 512), lambda l, i: (i, COL_OD // 512), 256),
        "ev_w_in": (rc, (256, 640), lambda l, i: (i, COL_EV // 640), 256),
    }
    for name, (src, gblock, gidx, rb) in big.items():
        res[name] = _adamw_big(W[name], M[name], V[name], src, gblock, gidx, f"adamw_{name}", rb)
    shape2 = {name: shape for name, shape, *_ in _SMALL}
    wmv = [tuple(d[name].reshape(shape2[name]) for d in (W, M, V)) for name, *_ in _SMALL]
    for (name, *_), outs in zip(_SMALL, _adamw_small(ra, rb_, wmv)):
        res[name] = tuple(o.reshape(W[name].shape) for o in outs)

    grad_x = dx[None]
    return (loss, grad_x, *[res[n][0] for n in order], *[res[n][1] for n in order],
            *[res[n][2] for n in order], *[res[n][3] for n in order])
```

```python
import jax
import jax.numpy as jnp
from jax import lax
from jax.experimental import pallas as pl
from jax.experimental.pallas import tpu as pltpu

BF = jnp.bfloat16
F32 = jnp.float32

D = 1024
DA = 512
DZ = 2560
DFF = 2816
NSH = 4
FSH = DFF // NSH
FH = DFF // 2
FPAD = 768
NMEM = 256
HEADS = 4
HD = D // HEADS
CHUNK = 128
GROUPS = 8
KA = 31
KB = 3
HALO = 16
RC = 32
RMS_EPS = 1e-6
LN_EPS = 1e-5
ATT_SCALE = HD ** -0.5
TM_FWD = 512
TM_BWD = 256

ADAM_LR = 0.001
ADAM_B1 = 0.9
ADAM_B2 = 0.999
ADAM_EPS = 1e-08
ADAM_WD = 0.01
ADAM_STEP = 10

XQ, XO, XK, XV = 0, 1, 2, 3
SP_ROWS = 48
SA_ROWS = 64

MESH = pl.DeviceIdType.MESH
ANY = pl.BlockSpec(memory_space=pl.ANY)


def _dot(a, b):
    return jnp.dot(a, b, preferred_element_type=F32)


def _dot_nt(a, b):
    return lax.dot_general(a, b, (((1,), (1,)), ((), ())), preferred_element_type=F32)


def _dot_tn(a, b):
    return lax.dot_general(a, b, (((0,), (0,)), ((), ())), preferred_element_type=F32)


def _rms(h):
    r = lax.rsqrt(jnp.mean(h * h, axis=-1, keepdims=True) + RMS_EPS)
    return h * r, r


def _rms_bwd(dn, hhat, r, g):
    dhh = dn * g
    return r * (dhh - hhat * jnp.mean(dhh * hhat, axis=-1, keepdims=True))


def _ln(x):
    mu = jnp.mean(x, axis=-1, keepdims=True)
    xc = x - mu
    rstd = lax.rsqrt(jnp.mean(xc * xc, axis=-1, keepdims=True) + LN_EPS)
    return xc * rstd, rstd


def _ln_bwd(dy, y, rstd):
    return rstd * (dy - jnp.mean(dy, axis=-1, keepdims=True) - y * jnp.mean(dy * y, axis=-1, keepdims=True))


def _colsum(x):
    return jnp.sum(x, axis=0, keepdims=True)


def _sigmoid(x):
    return 1.0 / (1.0 + jnp.exp(-x))


_GELU_C = 0.7978845608028654
_GELU_A = 0.044715


def _gelu(x):
    t = jnp.tanh(_GELU_C * (x + _GELU_A * x * x * x))
    return 0.5 * x * (1.0 + t)


def _gelu_grad(x):
    t = jnp.tanh(_GELU_C * (x + _GELU_A * x * x * x))
    return 0.5 * (1.0 + t) + 0.5 * x * (1.0 - t * t) * _GELU_C * (1.0 + 3.0 * _GELU_A * x * x)


def _first(i):
    return i == 0


def _last(i):
    return i == pl.num_programs(0) - 1


def _const(shape, idx):
    return pl.BlockSpec(shape, lambda *_, _idx=tuple(idx): _idx, pipeline_mode=pl.Buffered(1))


def _whole(a):
    return _const(a.shape, (0,) * a.ndim)


def _rows(tm, width, col=0):
    return pl.BlockSpec((tm, width), lambda i, _c=col: (i, _c))


def _params(ndim=1, vmem_mb=56):
    return pltpu.CompilerParams(dimension_semantics=("arbitrary",) * ndim, vmem_limit_bytes=vmem_mb << 20)


def _sds(shape, dtype):
    return jax.ShapeDtypeStruct(shape, dtype)


def _pos():
    return lax.axis_index("x"), lax.axis_index("y"), lax.axis_index("c")


def _other_chips(x, y):
    return [(1 - x, y), (x, 1 - y), (1 - x, 1 - y)]


def _half(ref, hc, lead=()):
    r = ref.shape[-2] // 2
    return ref.at[(*lead, pl.ds(pl.multiple_of(hc * r, 8), r), slice(None))]


def _rcopy(src, dst, ssem, rsem, to):
    return pltpu.make_async_remote_copy(src_ref=src, dst_ref=dst, send_sem=ssem, recv_sem=rsem,
                                        device_id=to, device_id_type=MESH)


def _gather_start(bufs, sems):
    ici_s, ici_r, _, _ = sems
    x, y, c = _pos()
    me = 2 * x + y
    for k, (cx, cy) in enumerate(_other_chips(x, y)):
        for t, buf in enumerate(bufs):
            mine = _half(buf, c, (me,))
            _rcopy(mine, mine, ici_s.at[3 * t + k], ici_r.at[3 * t + k], (cx, cy, c)).start()


def _gather_finish(bufs, sems):
    ici_s, ici_r, d2d_s, d2d_r = sems
    x, y, c = _pos()
    me = 2 * x + y
    sib = (x, y, 1 - c)
    chips = _other_chips(x, y)
    for k, (cx, cy) in enumerate(chips):
        for t, buf in enumerate(bufs):
            landed = _half(buf, c, (2 * cx + cy,))
            _rcopy(landed, landed, ici_s.at[3 * t + k], ici_r.at[3 * t + k], (cx, cy, c)).wait_recv()
            _rcopy(landed, landed, d2d_s.at[3 * t + k], d2d_r.at[3 * t + k], sib).start()
    for k, (cx, cy) in enumerate(chips):
        for t, buf in enumerate(bufs):
            got = _half(buf, 1 - c, (2 * cx + cy,))
            _rcopy(got, got, d2d_s.at[3 * t + k], d2d_r.at[3 * t + k], sib).wait_recv()
    for k, (cx, cy) in enumerate(chips):
        for t, buf in enumerate(bufs):
            mine = _half(buf, c, (me,))
            _rcopy(mine, mine, ici_s.at[3 * t + k], ici_r.at[3 * t + k], (cx, cy, c)).wait_send()
            landed = _half(buf, c, (2 * cx + cy,))
            _rcopy(landed, landed, d2d_s.at[3 * t + k], d2d_r.at[3 * t + k], sib).wait_send()


def _xchg_copies(srcs, dsts, nbig, sems):
    ssem, rsem = sems
    x, y, c = _pos()
    cps = []
    for k, (cx, cy) in enumerate(_other_chips(x, y)):
        for t, (src, dst) in enumerate(zip(srcs, dsts)):
            s = src.at[2 * cx + cy] if t < nbig else src
            cps.append(_rcopy(s, dst.at[k], ssem.at[3 * t + k], rsem.at[3 * t + k], (cx, cy, c)))
    return cps


def _gather_sems(n):
    return [pltpu.SemaphoreType.DMA((3 * n,))] * 4


def _xchg_sems(n):
    return [pltpu.SemaphoreType.DMA((3 * n,))] * 2


def _xchg_out_shapes(items):
    return [_sds((3,) + a.shape[-2:], a.dtype) for a in items]


def _call(body, *, name, grid, in_specs, out_specs, out_shape, args, scratch_shapes=(), aliases=None, carry=None):
    in_specs, out_specs, out_shape = list(in_specs), list(out_specs), list(out_shape)
    args = list(args)
    aliases = dict(aliases or {})
    n_in, n_out, n_scr = len(args), len(out_shape), len(scratch_shapes)
    kind = carry[0] if carry else None
    if kind == "gather":
        cin = list(carry[1])
        cout_shape = [_sds(a.shape, a.dtype) for a in cin]
        sems = _gather_sems(len(cin))
        for t in range(len(cin)):
            aliases[n_in + t] = n_out + t
    elif kind == "xchg":
        cin = list(carry[1]) + list(carry[2])
        nbig = len(carry[1])
        cout_shape = _xchg_out_shapes(cin)
        sems = _xchg_sems(len(cin))
    else:
        cin, cout_shape, sems = [], [], []
    nc = len(cin)

    def wrapped(*refs):
        ins = refs[:n_in]
        c_in = refs[n_in:n_in + nc]
        outs = refs[n_in + nc:n_in + nc + n_out]
        c_out = refs[n_in + nc + n_out:n_in + 2 * nc + n_out]
        scr = refs[n_in + 2 * nc + n_out:n_in + 2 * nc + n_out + n_scr]
        sem = refs[n_in + 2 * nc + n_out + n_scr:]
        i = pl.program_id(0)
        if kind:
            @pl.when(_first(i))
            def _():
                if kind == "gather":
                    _gather_start(c_out, sem)
                else:
                    for cp in _xchg_copies(c_in, c_out, nbig, sem):
                        cp.start()
        body(*ins, *outs, *scr)
        if kind:
            @pl.when(_last(i))
            def _():
                if kind == "gather":
                    _gather_finish(c_out, sem)
                else:
                    for cp in _xchg_copies(c_in, c_out, nbig, sem):
                        cp.wait()

    outs = pl.pallas_call(
        wrapped, name=name, grid=grid,
        in_specs=in_specs + [ANY] * nc, out_specs=out_specs + [ANY] * nc,
        out_shape=out_shape + cout_shape,
        scratch_shapes=list(scratch_shapes) + sems,
        input_output_aliases=aliases,
        compiler_params=_params(len(grid)),
    )(*args, *cin)
    return list(outs)


def _gather_now(bufs):
    n = len(bufs)

    def body(*refs):
        dst = refs[n:2 * n]
        sems = refs[2 * n:]
        _gather_start(dst, sems)
        _gather_finish(dst, sems)

    return pl.pallas_call(
        body, name="gather_first", in_specs=[ANY] * n, out_specs=[ANY] * n,
        out_shape=[_sds(a.shape, a.dtype) for a in bufs],
        input_output_aliases={t: t for t in range(n)},
        scratch_shapes=_gather_sems(n),
    )(*bufs)


def _xchg_now(big, small):
    items = list(big) + list(small)
    n = len(items)

    def body(*refs):
        cps = _xchg_copies(refs[:n], refs[n:2 * n], len(big), refs[2 * n:])
        for cp in cps:
            cp.start()
        for cp in cps:
            cp.wait()

    return pl.pallas_call(
        body, name="rs_xchg_last", in_specs=[ANY] * n, out_specs=[ANY] * n,
        out_shape=_xchg_out_shapes(items), scratch_shapes=_xchg_sems(n),
    )(*items)


def _sibling_swap(items, name):
    n = len(items)

    def body(*refs):
        src = refs[:n]
        dst = refs[n:2 * n]
        ssem, rsem = refs[2 * n:]
        x, y, c = _pos()
        sib = (x, y, 1 - c)
        cps = []
        for t in range(n):
            lead = (slice(None),) * (len(src[t].shape) - 2)
            cp = _rcopy(_half(src[t], 1 - c, lead), dst[t], ssem.at[t], rsem.at[t], sib)
            cp.start()
            cps.append(cp)
        for cp in cps:
            cp.wait()

    return pl.pallas_call(
        body, name=name, in_specs=[ANY] * n, out_specs=[ANY] * n,
        out_shape=[_sds(a.shape[:-2] + (a.shape[-2] // 2, a.shape[-1]), a.dtype) for a in items],
        scratch_shapes=[pltpu.SemaphoreType.DMA((n,)), pltpu.SemaphoreType.DMA((n,))],
    )(*items)


def _sibling_join(items):
    n = len(items)

    def body(*refs):
        dst = refs[n:2 * n]
        ssem, rsem = refs[2 * n:]
        x, y, c = _pos()
        sib = (x, y, 1 - c)
        cps = []
        for t in range(n):
            mine = _half(dst[t], c)
            cp = _rcopy(mine, mine, ssem.at[t], rsem.at[t], sib)
            cp.start()
            cps.append(cp)
        for t, cp in enumerate(cps):
            theirs = _half(dst[t], 1 - c)
            _rcopy(theirs, theirs, ssem.at[t], rsem.at[t], sib).wait_recv()
            cp.wait_send()

    return pl.pallas_call(
        body, name="rs_sibling_join", in_specs=[ANY] * n, out_specs=[ANY] * n,
        out_shape=[_sds(a.shape, a.dtype) for a in items],
        input_output_aliases={t: t for t in range(n)},
        scratch_shapes=[pltpu.SemaphoreType.DMA((n,))] * 2,
    )(*items)


def _ev_in_fwd(h, g, wev, tm, carry=None):
    s = h.shape[0]

    def body(h_ref, g_ref, w_ref, z_ref):
        hhat, _ = _rms(h_ref[...])
        n = (hhat * g_ref[...]).astype(BF)
        z_ref[...] = _dot(n, w_ref[...]).astype(BF)

    return _call(
        body, name="ev_in_fwd", grid=(s // tm,),
        in_specs=[_rows(tm, D), _const((1, D), (0, 0)), _const((D, DZ), (0, 0))],
        out_specs=[_rows(tm, DZ)], out_shape=[_sds((s, DZ), BF)],
        args=(h, g, wev), carry=carry)


def _halo_specs(tm, width, s, col=0):
    per = tm // HALO
    nh = s // HALO
    prev = pl.BlockSpec((HALO, width), lambda i, _c=col: (jnp.maximum(i * per - 1, 0), _c))
    nxt = pl.BlockSpec((HALO, width), lambda i, _c=col: (jnp.minimum((i + 1) * per, nh - 1), _c))
    return prev, nxt


def _fill_pad(pad_ref, prev, main, nxt, i, tm):
    pad_ref[pl.ds(0, HALO), :] = jnp.where(_first(i), 0.0, prev)
    pad_ref[pl.ds(HALO, tm), :] = main
    pad_ref[pl.ds(HALO + tm, HALO), :] = jnp.where(_last(i), 0.0, nxt)


def _glu_a(z):
    return z[:, :DA] * _sigmoid(z[:, DA:2 * DA])


def _gate_c(z):
    return z[:, 2048:] * z[:, 1024:1536]


def _ev_mix_fwd(h, z, wa, ba, lng, lnb, wb, bb, evo, tm, carry=None):
    s = h.shape[0]

    def body(h_ref, z_ref, zp_ref, zn_ref, wa_ref, ba_ref, lng_ref, lnb_ref, wb_ref, bb_ref, wo_ref,
             h1_ref, a2_ref, c2_ref, cat_ref, apad, cpad, c2f):
        i = pl.program_id(0)
        zm = z_ref[...].astype(F32)
        zp = zp_ref[...].astype(F32)
        zn = zn_ref[...].astype(F32)
        _fill_pad(apad, _glu_a(zp), _glu_a(zm), _glu_a(zn), i, tm)
        _fill_pad(cpad, _gate_c(zp), _gate_c(zm), _gate_c(zn), i, tm)
        for r in range(0, tm, RC):
            acc = jnp.zeros((RC, DA), F32) + ba_ref[...]
            for k in range(KA):
                acc = acc + apad[pl.ds(r + k + 1, RC), :] * wa_ref[pl.ds(k, 1), :]
            a2_ref[pl.ds(r, RC), :] = acc
            acc = jnp.zeros((RC, DA), F32) + bb_ref[...]
            for k in range(KB):
                acc = acc + cpad[pl.ds(r + HALO - 1 + k, RC), :] * wb_ref[pl.ds(k, 1), :]
            c2f[pl.ds(r, RC), :] = acc
        y, _ = _ln(a2_ref[...])
        a3 = y * lng_ref[...] + lnb_ref[...]
        cat_ref[:, :DA] = (a3 * _sigmoid(a3)).astype(BF)
        c2 = c2f[...]
        c2_ref[...] = c2.astype(BF)
        cat_ref[:, DA:] = (zm[:, 1536:2048] * c2).astype(BF)
        h1_ref[...] = h_ref[...] + _dot(cat_ref[...], wo_ref[...].reshape(D, D))

    zp_spec, zn_spec = _halo_specs(tm, DZ, s)
    small = lambda rows: _const((rows, DA), (0, 0))
    return _call(
        body, name="ev_mix_fwd", grid=(s // tm,),
        in_specs=[_rows(tm, D), _rows(tm, DZ), zp_spec, zn_spec, small(32), small(1), small(1), small(1),
                  small(8), small(1), _whole(evo)],
        out_specs=[_rows(tm, D), _rows(tm, DA), _rows(tm, DA), _rows(tm, D)],
        out_shape=[_sds((s, D), F32), _sds((s, DA), F32), _sds((s, DA), BF), _sds((s, D), BF)],
        scratch_shapes=[pltpu.VMEM((tm + 2 * HALO, DA), F32), pltpu.VMEM((tm + 2 * HALO, DA), F32),
                        pltpu.VMEM((tm, DA), F32)],
        args=(h, z, z, z, wa, ba, lng, lnb, wb, bb, evo), carry=carry)


def _ev_mix_bwd(dh, cat, a2, c2, z, lng, lnb, evo, tm, carry=None):
    s = dh.shape[0]

    def body(dh_ref, cat_ref, a2_ref, c2_ref, zgb_ref, lng_ref, lnb_ref, wo_ref,
             da2_ref, dc2_ref, dgb_ref, dlng_ref, dlnb_ref, gwo_ref, wacc):
        i = pl.program_id(0)

        @pl.when(_first(i))
        def _():
            wacc[...] = jnp.zeros_like(wacc)
            dlng_ref[...] = jnp.zeros_like(dlng_ref)
            dlnb_ref[...] = jnp.zeros_like(dlnb_ref)

        dhb = dh_ref[...].astype(BF)
        dcat = _dot_nt(dhb, wo_ref[...].reshape(D, D))
        wacc[...] += _dot_tn(cat_ref[...], dhb)
        y, rstd = _ln(a2_ref[...])
        a3 = y * lng_ref[...] + lnb_ref[...]
        sg = _sigmoid(a3)
        da3 = dcat[:, :DA] * (sg * (1.0 + a3 * (1.0 - sg)))
        dlng_ref[...] += _colsum(da3 * y)
        dlnb_ref[...] += _colsum(da3)
        da2_ref[...] = _ln_bwd(da3 * lng_ref[...], y, rstd).astype(BF)
        db = dcat[:, DA:]
        dc2_ref[...] = (db * zgb_ref[...].astype(F32)).astype(BF)
        dgb_ref[...] = (db * c2_ref[...].astype(F32)).astype(BF)

        @pl.when(_last(i))
        def _():
            gwo_ref[...] = wacc[...].astype(BF).reshape(NSH, 256, D)

    small = _const((1, DA), (0, 0))
    return _call(
        body, name="ev_mix_bwd", grid=(s // tm,),
        in_specs=[_rows(tm, D), _rows(tm, D), _rows(tm, DA), _rows(tm, DA), _rows(tm, DA, 3), small, small,
                  _whole(evo)],
        out_specs=[_rows(tm, DA), _rows(tm, DA), _rows(tm, DA), small, small, _whole(evo)],
        out_shape=[_sds((s, DA), BF), _sds((s, DA), BF), _sds((s, DA), BF), _sds((1, DA), F32),
                   _sds((1, DA), F32), _sds(evo.shape, BF)],
        scratch_shapes=[pltpu.VMEM((D, D), F32)],
        args=(dh, cat, a2, c2, z, lng, lnb, evo), carry=carry)


def _ev_in_bwd(h, dh, z, da2, dc2, dgb, g, wa, wb, wev, tm):
    s = h.shape[0]

    def body(h_ref, dh_ref, z_ref, zp_ref, zn_ref, da_ref, dap_ref, dan_ref, dc_ref, dcp_ref, dcn_ref,
             dgb_ref, g_ref, wa_ref, wb_ref, w_ref,
             dh0_ref, dwa_ref, dba_ref, dwb_ref, dbb_ref, dg_ref, gw_ref,
             apad, cpad, dapad, dcpad, dz, wacc, dwa_acc, dwb_acc):
        i = pl.program_id(0)

        @pl.when(_first(i))
        def _():
            wacc[...] = jnp.zeros_like(wacc)
            dwa_acc[...] = jnp.zeros_like(dwa_acc)
            dwb_acc[...] = jnp.zeros_like(dwb_acc)
            dba_ref[...] = jnp.zeros_like(dba_ref)
            dbb_ref[...] = jnp.zeros_like(dbb_ref)
            dg_ref[...] = jnp.zeros_like(dg_ref)

        zm = z_ref[...].astype(F32)
        zp = zp_ref[...].astype(F32)
        zn = zn_ref[...].astype(F32)
        _fill_pad(apad, _glu_a(zp), _glu_a(zm), _glu_a(zn), i, tm)
        _fill_pad(cpad, _gate_c(zp), _gate_c(zm), _gate_c(zn), i, tm)
        da2 = da_ref[...].astype(F32)
        dc2 = dc_ref[...].astype(F32)
        _fill_pad(dapad, dap_ref[...].astype(F32), da2, dan_ref[...].astype(F32), i, tm)
        _fill_pad(dcpad, dcp_ref[...].astype(F32), dc2, dcn_ref[...].astype(F32), i, tm)
        dba_ref[...] += _colsum(da2)
        dbb_ref[...] += _colsum(dc2)

        sg = _sigmoid(zm[:, DA:2 * DA])
        aval = zm[:, :DA]
        for r in range(0, tm, RC):
            acc = jnp.zeros((RC, DA), F32)
            for k in range(KA):
                acc = acc + dapad[pl.ds(r + 2 * HALO - 1 - k, RC), :] * wa_ref[pl.ds(k, 1), :]
            sgr = sg[r:r + RC]
            dz[pl.ds(r, RC), 0:DA] = (acc * sgr).astype(BF)
            dz[pl.ds(r, RC), DA:2 * DA] = (acc * aval[r:r + RC] * sgr * (1.0 - sgr)).astype(BF)
            acc = jnp.zeros((RC, DA), F32)
            for k in range(KB):
                acc = acc + dcpad[pl.ds(r + HALO + 1 - k, RC), :] * wb_ref[pl.ds(k, 1), :]
            dz[pl.ds(r, RC), 1024:1536] = (acc * zm[r:r + RC, 2048:]).astype(BF)
            dz[pl.ds(r, RC), 2048:2560] = (acc * zm[r:r + RC, 1024:1536]).astype(BF)
        dz[:, 1536:2048] = dgb_ref[...]

        for k in range(KA):
            acc = jnp.zeros((8, DA), F32)
            for r in range(0, tm, RC):
                p = apad[pl.ds(r + k + 1, RC), :] * dapad[pl.ds(r + HALO, RC), :]
                acc = acc + p.reshape(RC // 8, 8, DA).sum(axis=0)
            dwa_acc[k] += acc
        for k in range(KB):
            acc = jnp.zeros((8, DA), F32)
            for r in range(0, tm, RC):
                p = cpad[pl.ds(r + HALO - 1 + k, RC), :] * dcpad[pl.ds(r + HALO, RC), :]
                acc = acc + p.reshape(RC // 8, 8, DA).sum(axis=0)
            dwb_acc[k] += acc

        gg = g_ref[...]
        hhat, rr = _rms(h_ref[...])
        n = (hhat * gg).astype(BF)
        dzb = dz[...]
        wacc[...] += _dot_tn(n, dzb)
        dn = _dot_nt(dzb, w_ref[...])
        dg_ref[...] += _colsum(dn * hhat)
        dh0_ref[...] = dh_ref[...] + _rms_bwd(dn, hhat, rr, gg)

        @pl.when(_last(i))
        def _():
            for k in range(NSH):
                gw_ref[k] = wacc[:, 640 * k:640 * (k + 1)].astype(BF)
            dwa_ref[...] = jnp.sum(dwa_acc[...], axis=1)
            dwb_ref[...] = jnp.sum(dwb_acc[...], axis=1)

    zp_spec, zn_spec = _halo_specs(tm, DZ, s)
    hp_spec, hn_spec = _halo_specs(tm, DA, s)
    small = lambda rows: _const((rows, DA), (0, 0))
    return _call(
        body, name="ev_in_bwd", grid=(s // tm,),
        in_specs=[_rows(tm, D), _rows(tm, D), _rows(tm, DZ), zp_spec, zn_spec,
                  _rows(tm, DA), hp_spec, hn_spec, _rows(tm, DA), hp_spec, hn_spec, _rows(tm, DA),
                  _const((1, D), (0, 0)), small(32), small(8), _const((D, DZ), (0, 0))],
        out_specs=[_rows(tm, D), small(32), small(1), small(8), small(1), _const((1, D), (0, 0)),
                   _const((NSH, D, 640), (0, 0, 0))],
        out_shape=[_sds((s, D), F32), _sds((32, DA), F32), _sds((1, DA), F32), _sds((8, DA), F32),
                   _sds((1, DA), F32), _sds((1, D), F32), _sds((NSH, D, 640), BF)],
        scratch_shapes=[pltpu.VMEM((tm + 2 * HALO, DA), F32)] * 4 + [
            pltpu.VMEM((tm, DZ), BF), pltpu.VMEM((D, DZ), F32),
            pltpu.VMEM((32, 8, DA), F32), pltpu.VMEM((8, 8, DA), F32)],
        args=(h, dh, z, z, z, da2, da2, da2, dc2, dc2, dc2, dgb, g, wa, wb, wev))


def _xa_spec(item):
    return _const((NSH, 256, D), (0, item, 0))


def _kv_proj(mem, g, xa, layer):
    def body(mem_ref, g_ref, wk_ref, wv_ref, k_ref, v_ref):
        mhat, _ = _rms(mem_ref[...])
        mn = (mhat * g_ref[...]).astype(BF)
        k_ref[...] = _dot(mn, wk_ref[...].reshape(D, D)).astype(BF)
        v_ref[...] = _dot(mn, wv_ref[...].reshape(D, D)).astype(BF)

    full = _const((NMEM, D), (0, 0))
    return _call(
        body, name=f"kv_proj{layer}", grid=(1,),
        in_specs=[full, _const((1, D), (0, 0)), _xa_spec(XK), _xa_spec(XV)],
        out_specs=[full, full], out_shape=[_sds((NMEM, D), BF), _sds((NMEM, D), BF)],
        args=(mem, g, xa, xa))


def _softmax_rows(sc):
    m = jnp.max(sc, axis=-1, keepdims=True)
    e = jnp.exp(sc - m)
    return e / jnp.sum(e, axis=-1, keepdims=True)


def _xattn_fwd(h, g, k, v, xa, layer, tm, carry=None):
    s = h.shape[0]

    def body(h_ref, g_ref, k_ref, v_ref, wq_ref, wo_ref, ho_ref, q_ref, o_scr):
        hh = h_ref[...]
        hhat, _ = _rms(hh)
        n = (hhat * g_ref[...]).astype(BF)
        q = _dot(n, wq_ref[...].reshape(D, D)).astype(BF)
        q_ref[...] = q
        for hd in range(HEADS):
            cs = slice(HD * hd, HD * (hd + 1))
            p = _softmax_rows(_dot_nt(q[:, cs], k_ref[:, cs]) * ATT_SCALE)
            o_scr[:, cs] = _dot(p.astype(BF), v_ref[:, cs]).astype(BF)
        ho_ref[...] = hh + _dot(o_scr[...], wo_ref[...].reshape(D, D))

    kvs = _const((NMEM, D), (0, 0))
    return _call(
        body, name=f"xattn_fwd{layer}", grid=(s // tm,),
        in_specs=[_rows(tm, D), _const((1, D), (0, 0)), kvs, kvs, _xa_spec(XQ), _xa_spec(XO)],
        out_specs=[_rows(tm, D), _rows(tm, D)],
        out_shape=[_sds((s, D), F32), _sds((s, D), BF)],
        scratch_shapes=[pltpu.VMEM((tm, D), BF)],
        args=(h, g, k, v, xa, xa), carry=carry)


def _xattn_bwd(h, dh, q, g, k, v, xa, layer, tm, carry=None):
    s = h.shape[0]

    def body(h_ref, dh_ref, q_ref, g_ref, k_ref, v_ref, wq_ref, wo_ref,
             dhi_ref, dk_ref, dv_ref, dg_ref, gqo_ref, o_scr, dq_scr, qacc, oacc):
        i = pl.program_id(0)

        @pl.when(_first(i))
        def _():
            qacc[...] = jnp.zeros_like(qacc)
            oacc[...] = jnp.zeros_like(oacc)
            dk_ref[...] = jnp.zeros_like(dk_ref)
            dv_ref[...] = jnp.zeros_like(dv_ref)
            dg_ref[...] = jnp.zeros_like(dg_ref)

        gg = g_ref[...]
        dho = dh_ref[...]
        dhb = dho.astype(BF)
        q = q_ref[...]
        do = _dot_nt(dhb, wo_ref[...].reshape(D, D)).astype(BF)
        for hd in range(HEADS):
            cs = slice(HD * hd, HD * (hd + 1))
            kh = k_ref[:, cs]
            vh = v_ref[:, cs]
            p = _softmax_rows(_dot_nt(q[:, cs], kh) * ATT_SCALE)
            pb = p.astype(BF)
            o_scr[:, cs] = _dot(pb, vh).astype(BF)
            doh = do[:, cs]
            dp = _dot_nt(doh, vh)
            dv_ref[:, cs] += _dot_tn(pb, doh)
            ds = (p * (dp - jnp.sum(dp * p, axis=-1, keepdims=True)) * ATT_SCALE).astype(BF)
            dq_scr[:, cs] = _dot(ds, kh).astype(BF)
            dk_ref[:, cs] += _dot_tn(ds, q[:, cs])
        oacc[...] += _dot_tn(o_scr[...], dhb)
        hhat, rr = _rms(h_ref[...])
        n = (hhat * gg).astype(BF)
        dq = dq_scr[...]
        qacc[...] += _dot_tn(n, dq)
        dn = _dot_nt(dq, wq_ref[...].reshape(D, D))
        dg_ref[...] += _colsum(dn * hhat)
        dhi_ref[...] = dho + _rms_bwd(dn, hhat, rr, gg)

        @pl.when(_last(i))
        def _():
            gqo_ref[:, 0:256, :] = qacc[...].astype(BF).reshape(NSH, 256, D)
            gqo_ref[:, 256:512, :] = oacc[...].astype(BF).reshape(NSH, 256, D)

    kvs = _const((NMEM, D), (0, 0))
    return _call(
        body, name=f"xattn_bwd{layer}", grid=(s // tm,),
        in_specs=[_rows(tm, D), _rows(tm, D), _rows(tm, D), _const((1, D), (0, 0)), kvs, kvs,
                  _xa_spec(XQ), _xa_spec(XO)],
        out_specs=[_rows(tm, D), kvs, kvs, _const((1, D), (0, 0)), _const((NSH, 512, D), (0, 0, 0))],
        out_shape=[_sds((s, D), F32), _sds((NMEM, D), F32), _sds((NMEM, D), F32), _sds((1, D), F32),
                   _sds(xa.shape, BF)],
        scratch_shapes=[pltpu.VMEM((tm, D), BF), pltpu.VMEM((tm, D), BF), pltpu.VMEM((D, D), F32),
                        pltpu.VMEM((D, D), F32)],
        args=(h, dh, q, g, k, v, xa, xa), carry=carry)


def _kv_proj_bwd(mem, g, dk, dv, xa, gxa, layer):
    def body(mem_ref, g_ref, dk_ref, dv_ref, wk_ref, wv_ref, gxa_in, dg_ref, gkv_ref):
        mhat, _ = _rms(mem_ref[...])
        mn = (mhat * g_ref[...]).astype(BF)
        dkb = dk_ref[...].astype(BF)
        dvb = dv_ref[...].astype(BF)
        gkv_ref[:, 0:256, :] = _dot_tn(mn, dkb).astype(BF).reshape(NSH, 256, D)
        gkv_ref[:, 256:512, :] = _dot_tn(mn, dvb).astype(BF).reshape(NSH, 256, D)
        dmn = _dot_nt(dkb, wk_ref[...].reshape(D, D)) + _dot_nt(dvb, wv_ref[...].reshape(D, D))
        dg_ref[...] = _colsum(dmn * mhat)

    full = _const((NMEM, D), (0, 0))
    return _call(
        body, name=f"kv_proj_bwd{layer}", grid=(1,),
        in_specs=[full, _const((1, D), (0, 0)), full, full, _xa_spec(XK), _xa_spec(XV), ANY],
        out_specs=[_const((1, D), (0, 0)), _const((NSH, 512, D), (0, 1, 0))],
        out_shape=[_sds((1, D), F32), _sds(gxa.shape, BF)],
        aliases={6: 1},
        args=(mem, g, dk, dv, xa, xa, gxa))


def _ffn_fwd(h, g, wg, wu, dn, layer, tm, carry=None):
    s = h.shape[0]

    def body(h_ref, g_ref, wg_ref, wu_ref, wd_ref, ho_ref, gt_ref, up_ref):
        hh = h_ref[...]
        hhat, _ = _rms(hh)
        n = (hhat * g_ref[...]).astype(BF)
        gt = _dot(n, wg_ref[...])
        up = _dot(n, wu_ref[...])
        gt_ref[...] = gt.astype(BF)
        up_ref[...] = up.astype(BF)
        act = (gt * _sigmoid(gt) * up).astype(BF)
        ho_ref[...] = hh + _dot(act, wd_ref[...].reshape(DFF, D))

    return _call(
        body, name=f"ffn_fwd{layer}", grid=(s // tm,),
        in_specs=[_rows(tm, D), _const((1, D), (0, 0)), _const((D, DFF), (0, 0)), _const((D, DFF), (0, 0)),
                  _whole(dn)],
        out_specs=[_rows(tm, D), _rows(tm, DFF), _rows(tm, DFF)],
        out_shape=[_sds((s, D), F32), _sds((s, DFF), BF), _sds((s, DFF), BF)],
        args=(h, g, wg, wu, dn), carry=carry)


def _ffn_bwd_dx(h, dh, gt, up, g, wg, wu, dn, layer, tm, carry=None):
    s = h.shape[0]

    def body(h_ref, dh_ref, gt_ref, up_ref, g_ref, wg_ref, wu_ref, wd_ref, dhi_ref, dg_ref, dgt_ref, dup_ref):
        i = pl.program_id(0)

        @pl.when(_first(i))
        def _():
            dg_ref[...] = jnp.zeros_like(dg_ref)

        gg = g_ref[...]
        dho = dh_ref[...]
        dhb = dho.astype(BF)
        for half in range(2):
            cs = slice(FH * half, FH * (half + 1))
            gtv = gt_ref[:, cs].astype(F32)
            upv = up_ref[:, cs].astype(F32)
            sg = _sigmoid(gtv)
            dact = _dot_nt(dhb, wd_ref[2 * half:2 * half + 2].reshape(FH, D))
            dgt_ref[:, cs] = (dact * upv * (sg * (1.0 + gtv * (1.0 - sg)))).astype(BF)
            dup_ref[:, cs] = (dact * gtv * sg).astype(BF)
        dn_ = _dot_nt(dgt_ref[...], wg_ref[...]) + _dot_nt(dup_ref[...], wu_ref[...])
        hhat, rr = _rms(h_ref[...])
        dg_ref[...] += _colsum(dn_ * hhat)
        dhi_ref[...] = dho + _rms_bwd(dn_, hhat, rr, gg)

    return _call(
        body, name=f"ffn_bwd_dx{layer}", grid=(s // tm,),
        in_specs=[_rows(tm, D), _rows(tm, D), _rows(tm, DFF), _rows(tm, DFF), _const((1, D), (0, 0)),
                  _const((D, DFF), (0, 0)), _const((D, DFF), (0, 0)), _whole(dn)],
        out_specs=[_rows(tm, D), _const((1, D), (0, 0)), _rows(tm, DFF), _rows(tm, DFF)],
        out_shape=[_sds((s, D), F32), _sds((1, D), F32), _sds((s, DFF), BF), _sds((s, DFF), BF)],
        args=(h, dh, gt, up, g, wg, wu, dn), carry=carry)


def _ffn_bwd_dwgu(h, g, dgt, dup, layer, tm):
    s = h.shape[0]

    def body(h_ref, g_ref, dgt_ref, dup_ref, ggu_ref, gacc, uacc):
        i = pl.program_id(0)

        @pl.when(_first(i))
        def _():
            gacc[...] = jnp.zeros_like(gacc)
            uacc[...] = jnp.zeros_like(uacc)

        hhat, _ = _rms(h_ref[...])
        n = (hhat * g_ref[...]).astype(BF)
        gacc[...] += _dot_tn(n, dgt_ref[...])
        uacc[...] += _dot_tn(n, dup_ref[...])

        @pl.when(_last(i))
        def _():
            pad = jnp.zeros((D, FPAD - FSH), BF)
            for k in range(NSH):
                ggu_ref[k, :, 0:FSH] = gacc[:, FSH * k:FSH * (k + 1)].astype(BF)
                ggu_ref[k, :, FSH:FPAD] = pad
                ggu_ref[k, :, FPAD:FPAD + FSH] = uacc[:, FSH * k:FSH * (k + 1)].astype(BF)
                ggu_ref[k, :, FPAD + FSH:2 * FPAD] = pad

    return _call(
        body, name=f"ffn_bwd_dwgu{layer}", grid=(s // tm,),
        in_specs=[_rows(tm, D), _const((1, D), (0, 0)), _rows(tm, DFF), _rows(tm, DFF)],
        out_specs=[_const((NSH, D, 2 * FPAD), (0, 0, 0))],
        out_shape=[_sds((NSH, D, 2 * FPAD), BF)],
        scratch_shapes=[pltpu.VMEM((D, DFF), F32), pltpu.VMEM((D, DFF), F32)],
        args=(h, g, dgt, dup))[0]


def _ffn_bwd_dwd(gt, up, dh, layer, tm):
    s = dh.shape[0]

    def body(gt_ref, up_ref, dh_ref, gd_ref, dacc):
        i = pl.program_id(0)

        @pl.when(_first(i))
        def _():
            dacc[...] = jnp.zeros_like(dacc)

        gtv = gt_ref[...].astype(F32)
        act = (gtv * _sigmoid(gtv) * up_ref[...].astype(F32)).astype(BF)
        dacc[...] += _dot_tn(act, dh_ref[...].astype(BF))

        @pl.when(_last(i))
        def _():
            gd_ref[...] = dacc[...].astype(BF).reshape(NSH, FSH, D)

    return _call(
        body, name=f"ffn_bwd_dwd{layer}", grid=(s // tm,),
        in_specs=[_rows(tm, DFF), _rows(tm, DFF), _rows(tm, D)],
        out_specs=[_const((NSH, FSH, D), (0, 0, 0))],
        out_shape=[_sds((NSH, FSH, D), BF)],
        scratch_shapes=[pltpu.VMEM((DFF, D), F32)],
        args=(gt, up, dh))[0]


def _sgu_fwd(h, g, lng, lnb, ws, bsb, wod, odo, tm, carry=None):
    s = h.shape[0]

    def body(h_ref, g_ref, lng_ref, lnb_ref, ws_ref, bsb_ref, wi_ref, wo_ref, ho_ref, zp_ref, y_scr):
        hh = h_ref[...]
        hhat, _ = _rms(hh)
        n = (hhat * g_ref[...]).astype(BF)
        zpre = _dot(n, wi_ref[...])
        zp_ref[...] = zpre.astype(BF)
        z = _gelu(zpre)
        u = z[:, :D]
        y, _ = _ln(z[:, D:])
        vn = (y * lng_ref[...] + lnb_ref[...]).astype(BF)
        for c in range(tm // CHUNK):
            rs = slice(CHUNK * c, CHUNK * (c + 1))
            for gi in range(GROUPS):
                cs = slice(CHUNK * gi, CHUNK * (gi + 1))
                sv = _dot(ws_ref[gi], vn[rs, cs]) + bsb_ref[gi]
                y_scr[rs, cs] = (u[rs, cs] * sv).astype(BF)
        ho_ref[...] = hh + _dot(y_scr[...], wo_ref[...].reshape(D, D))

    vec = _const((1, D), (0, 0))
    sq = _const((GROUPS, CHUNK, CHUNK), (0, 0, 0))
    return _call(
        body, name="sgu_fwd", grid=(s // tm,),
        in_specs=[_rows(tm, D), vec, vec, vec, sq, sq, _const((D, 2 * D), (0, 0)), _whole(odo)],
        out_specs=[_rows(tm, D), _rows(tm, 2 * D)],
        out_shape=[_sds((s, D), F32), _sds((s, 2 * D), BF)],
        scratch_shapes=[pltpu.VMEM((tm, D), BF)],
        args=(h, g, lng, lnb, ws, bsb, wod, odo), carry=carry)


def _sgu_bwd(h, dh, zpre, g, lng, lnb, ws, wst, bsb, wod, odo, tm):
    s = h.shape[0]

    def body(h_ref, dh_ref, zp_ref, g_ref, lng_ref, lnb_ref, ws_ref, wst_ref, bsb_ref, wi_ref, wo_ref,
             dhi_ref, dws_ref, dbs_ref, dlng_ref, dlnb_ref, dg_ref, gwi_ref, gwo_ref,
             y_scr, dz_scr, dvn_scr, iacc, oacc, bacc):
        i = pl.program_id(0)

        @pl.when(_first(i))
        def _():
            iacc[...] = jnp.zeros_like(iacc)
            oacc[...] = jnp.zeros_like(oacc)
            bacc[...] = jnp.zeros_like(bacc)
            dws_ref[...] = jnp.zeros_like(dws_ref)
            dlng_ref[...] = jnp.zeros_like(dlng_ref)
            dlnb_ref[...] = jnp.zeros_like(dlnb_ref)
            dg_ref[...] = jnp.zeros_like(dg_ref)

        gg = g_ref[...]
        dho = dh_ref[...]
        dhb = dho.astype(BF)
        zpre_v = zp_ref[...].astype(F32)
        z = _gelu(zpre_v)
        u = z[:, :D]
        yl, rstd = _ln(z[:, D:])
        vn = (yl * lng_ref[...] + lnb_ref[...]).astype(BF)
        dy = _dot_nt(dhb, wo_ref[...].reshape(D, D))
        for c in range(tm // CHUNK):
            rs = slice(CHUNK * c, CHUNK * (c + 1))
            for gi in range(GROUPS):
                cs = slice(CHUNK * gi, CHUNK * (gi + 1))
                vb = vn[rs, cs]
                sv = _dot(ws_ref[gi], vb) + bsb_ref[gi]
                ub = u[rs, cs]
                dyb = dy[rs, cs]
                y_scr[rs, cs] = (ub * sv).astype(BF)
                dz_scr[rs, cs] = dyb * sv
                dsv = dyb * ub
                bacc[gi] += dsv
                dsvb = dsv.astype(BF)
                dws_ref[gi] += _dot_nt(dsvb, vb)
                dvn_scr[rs, cs] = _dot(wst_ref[gi], dsvb)
        oacc[...] += _dot_tn(y_scr[...], dhb)
        dvn = dvn_scr[...]
        dlng_ref[...] += _colsum(dvn * yl)
        dlnb_ref[...] += _colsum(dvn)
        dz_scr[:, D:] = _ln_bwd(dvn * lng_ref[...], yl, rstd)
        dzb = (dz_scr[...] * _gelu_grad(zpre_v)).astype(BF)
        hhat, rr = _rms(h_ref[...])
        n = (hhat * gg).astype(BF)
        iacc[...] += _dot_tn(n, dzb)
        dn = _dot_nt(dzb, wi_ref[...])
        dg_ref[...] += _colsum(dn * hhat)
        dhi_ref[...] = dho + _rms_bwd(dn, hhat, rr, gg)

        @pl.when(_last(i))
        def _():
            for k in range(NSH):
                gwi_ref[k] = iacc[:, 512 * k:512 * (k + 1)].astype(BF)
            gwo_ref[...] = oacc[...].astype(BF).reshape(NSH, 256, D)
            dbs_ref[...] = jnp.sum(bacc[...], axis=-1)

    vec = _const((1, D), (0, 0))
    sq = _const((GROUPS, CHUNK, CHUNK), (0, 0, 0))
    return _call(
        body, name="sgu_bwd", grid=(s // tm,),
        in_specs=[_rows(tm, D), _rows(tm, D), _rows(tm, 2 * D), vec, vec, vec, sq, sq, sq,
                  _const((D, 2 * D), (0, 0)), _whole(odo)],
        out_specs=[_rows(tm, D), sq, _const((GROUPS, CHUNK), (0, 0)), vec, vec, vec,
                   _const((NSH, D, 512), (0, 0, 0)), _whole(odo)],
        out_shape=[_sds((s, D), F32), _sds((GROUPS, CHUNK, CHUNK), F32), _sds((GROUPS, CHUNK), F32),
                   _sds((1, D), F32), _sds((1, D), F32), _sds((1, D), F32), _sds((NSH, D, 512), BF),
                   _sds(odo.shape, BF)],
        scratch_shapes=[pltpu.VMEM((tm, D), BF), pltpu.VMEM((tm, 2 * D), F32),
                        pltpu.VMEM((tm, D), F32), pltpu.VMEM((D, 2 * D), F32), pltpu.VMEM((D, D), F32),
                        pltpu.VMEM((GROUPS, CHUNK, CHUNK), F32)],
        args=(h, dh, zpre, g, lng, lnb, ws, wst, bsb, wod, odo))


def _final_loss(h, tgt, g, tm):
    s = h.shape[0]

    def body(h_ref, t_ref, g_ref, dh_ref, loss_ref, dg_ref):
        i = pl.program_id(0)

        @pl.when(_first(i))
        def _():
            loss_ref[...] = jnp.zeros_like(loss_ref)
            dg_ref[...] = jnp.zeros_like(dg_ref)

        gg = g_ref[...]
        hhat, rr = _rms(h_ref[...])
        diff = hhat * gg - t_ref[...]
        loss_ref[...] += jnp.sum(diff * diff) * (0.5 / D)
        dout = diff * (1.0 / D)
        dg_ref[...] += _colsum(dout * hhat)
        dh_ref[...] = _rms_bwd(dout, hhat, rr, gg)

    return _call(
        body, name="final_loss", grid=(s // tm,),
        in_specs=[_rows(tm, D), _rows(tm, D), _const((1, D), (0, 0))],
        out_specs=[_rows(tm, D), _const((1, 128), (0, 0)), _const((1, D), (0, 0))],
        out_shape=[_sds((s, D), F32), _sds((1, 128), F32), _sds((1, D), F32)],
        args=(h, tgt, g))


def _own_plane(shape):
    return pl.BlockSpec((1,) + shape, lambda i, chip_ref: (chip_ref[0], 0, 0), pipeline_mode=pl.Buffered(1))


def _pack_rows(chip_idx, w_down, xq, xo, xk, xv, ev_out, od_out):
    def body(chip_ref, wd_ref, xq_ref, xo_ref, xk_ref, xv_ref, ev_ref, od_ref,
             evo_ref, odo_ref, dn0_ref, dn1_ref, xa0_ref, xa1_ref):
        evo_ref[0] = ev_ref[0].astype(BF)
        odo_ref[0] = od_ref[0].astype(BF)
        dn0_ref[0] = wd_ref[0].astype(BF)
        dn1_ref[0] = wd_ref[1].astype(BF)
        for layer, xa_ref in enumerate((xa0_ref, xa1_ref)):
            for item, src in ((XQ, xq_ref), (XO, xo_ref), (XK, xk_ref), (XV, xv_ref)):
                xa_ref[0, 256 * item:256 * (item + 1), :] = src[layer].astype(BF)

    ins = (w_down, xq, xo, xk, xv, ev_out, od_out)
    shapes = [(256, D), (256, D), (FSH, D), (FSH, D), (4 * 256, D), (4 * 256, D)]
    grid_spec = pltpu.PrefetchScalarGridSpec(
        num_scalar_prefetch=1, grid=(1,), in_specs=[_whole(a) for a in ins],
        out_specs=[_own_plane(sh) for sh in shapes])
    return pl.pallas_call(
        body, name="pack_rows", grid_spec=grid_spec, out_shape=[_sds((NSH,) + sh, BF) for sh in shapes],
        compiler_params=_params(),
    )(chip_idx, *ins)


def _pack_cols(chip_idx, w_gate, w_up, od_in, ev_in, conv_a, conv_b, ln_g, ln_b):
    def body(chip_ref, wg_ref, wu_ref, od_ref, ev_ref, ca_ref, cb_ref, lg_ref, lb_ref,
             cev_ref, cod_ref, cf0_ref, cf1_ref, sp_ref):
        cev_ref[0] = ev_ref[0].astype(BF)
        cod_ref[0] = od_ref[0].astype(BF)
        pad = jnp.zeros((D, FPAD - FSH), BF)
        for layer, cf_ref in enumerate((cf0_ref, cf1_ref)):
            for j, ref in enumerate((wg_ref, wu_ref)):
                cf_ref[0, :, j * FPAD:j * FPAD + FSH] = ref[layer].astype(BF)
                cf_ref[0, :, j * FPAD + FSH:(j + 1) * FPAD] = pad
        sp_ref[...] = jnp.zeros_like(sp_ref)
        sp_ref[0, 0:KA, 0:128] = ca_ref[0]
        sp_ref[0, 32:32 + KB, 0:128] = cb_ref[0]
        sp_ref[0, 40:41, :] = lg_ref[...]
        sp_ref[0, 41:42, :] = lb_ref[...]

    ins = (w_gate, w_up, od_in, ev_in, conv_a, conv_b, ln_g, ln_b)
    shapes = [(D, 640), (D, 512), (D, 2 * FPAD), (D, 2 * FPAD)]
    grid_spec = pltpu.PrefetchScalarGridSpec(
        num_scalar_prefetch=1, grid=(1,), in_specs=[_whole(a) for a in ins],
        out_specs=[_own_plane(sh) for sh in shapes] + [_own_plane((SP_ROWS, 256))])
    return pl.pallas_call(
        body, name="pack_cols", grid_spec=grid_spec,
        out_shape=[_sds((NSH,) + sh, BF) for sh in shapes] + [_sds((NSH, SP_ROWS, 256), F32)],
        compiler_params=_params(),
    )(chip_idx, *ins)


def _assemble(piece, parts, name):
    tr = 256
    cols = piece.shape[2]

    def body(p_ref, *outs):
        for k in range(NSH):
            for (off, width), o_ref in zip(parts, outs):
                o_ref[:, width * k:width * (k + 1)] = p_ref[k, :, off:off + width]

    return _call(
        body, name=name, grid=(D // tr,),
        in_specs=[pl.BlockSpec((NSH, tr, cols), lambda i: (0, i, 0))],
        out_specs=[pl.BlockSpec((tr, NSH * width), lambda i: (i, 0)) for _, width in parts],
        out_shape=[_sds((D, NSH * width), BF) for _, width in parts],
        args=(piece,))


def _blocks_of(rows, cols, itemsize):
    rb = rows
    while rb * cols * itemsize > (2 << 20) and rb % 2 == 0 and (rb // 2) % 16 == 0:
        rb //= 2
    return rb


def _add_half(full, recv, c_idx, name):
    p, r, cols = full.shape
    rb = _blocks_of(r // 2, cols, 4)
    nb = (r // 2) // rb

    def body(c_ref, a_ref, b_ref, o_ref):
        o_ref[...] = (a_ref[...].astype(F32) + b_ref[...].astype(F32)).astype(o_ref.dtype)

    grid_spec = pltpu.PrefetchScalarGridSpec(
        num_scalar_prefetch=1, grid=(p, nb),
        in_specs=[pl.BlockSpec((1, rb, cols), lambda j, i, c_ref: (j, c_ref[0] * nb + i, 0)),
                  pl.BlockSpec((1, rb, cols), lambda j, i, c_ref: (j, i, 0))],
        out_specs=pl.BlockSpec((1, rb, cols), lambda j, i, c_ref: (j, i, 0)))
    return pl.pallas_call(
        body, name=name, grid_spec=grid_spec, out_shape=_sds(recv.shape, full.dtype),
        compiler_params=_params(2),
    )(c_idx, full, recv)


def _sum_chips(own, got, plane_c, name):
    _, r, cols = own.shape
    rb = _blocks_of(r, cols, 4)
    nb = r // rb

    def body(s_ref, a_ref, b_ref, o_ref):
        f = lambda v: v.astype(F32)
        o_ref[...] = (f(a_ref[0]) + f(b_ref[1])) + (f(b_ref[0]) + f(b_ref[2]))

    grid_spec = pltpu.PrefetchScalarGridSpec(
        num_scalar_prefetch=1, grid=(nb,),
        in_specs=[pl.BlockSpec((1, rb, cols), lambda i, s_ref: (s_ref[0], i, 0)),
                  pl.BlockSpec((3, rb, cols), lambda i, s_ref: (0, i, 0))],
        out_specs=pl.BlockSpec((rb, cols), lambda i, s_ref: (s_ref[1] * nb + i, 0)))
    return pl.pallas_call(
        body, name=name, grid_spec=grid_spec, out_shape=_sds((2 * r, cols), F32),
        compiler_params=_params(1),
    )(plane_c, own, got)


def _adamw_math(w, g, m, v):
    m = ADAM_B1 * m + (1.0 - ADAM_B1) * g
    v = ADAM_B2 * v + (1.0 - ADAM_B2) * (g * g)
    m_hat = m / (1.0 - ADAM_B1 ** ADAM_STEP)
    v_hat = v / (1.0 - ADAM_B2 ** ADAM_STEP)
    delta = -ADAM_LR * (m_hat / (jnp.sqrt(v_hat) + ADAM_EPS) + ADAM_WD * w)
    return delta, m, v


def _adamw_big(w, m, v, gsrcs, gblock, gidx, name, rb):
    nl, r, cols = w.shape
    assert len(gsrcs) == nl

    def body(w_ref, m_ref, v_ref, *rest):
        g_refs, (go_ref, d_ref, mo_ref, vo_ref) = rest[:nl], rest[nl:]
        g = g_refs[0][:, 0:cols]
        if nl == 2:
            g = jnp.where(pl.program_id(0) == 0, g, g_refs[1][:, 0:cols])
        delta, mn, vn = _adamw_math(w_ref[0], g, m_ref[0], v_ref[0])
        go_ref[0] = g
        d_ref[0] = delta
        mo_ref[0] = mn
        vo_ref[0] = vn

    wspec = pl.BlockSpec((1, rb, cols), lambda l, i: (l, i, 0))
    gspec = pl.BlockSpec(gblock, lambda l, i: gidx(i))
    return pl.pallas_call(
        body, name=name, grid=(nl, r // rb),
        in_specs=[wspec, wspec, wspec] + [gspec] * nl,
        out_specs=[wspec] * 4, out_shape=[_sds(w.shape, F32)] * 4,
        compiler_params=_params(2),
    )(w, m, v, *gsrcs)


_SMALL = (
    ("g_mix", (2, D), "a", (0, 2), 0, False),
    ("g_xattn", (2, D), "a", (2, 4), 0, False),
    ("g_mem", (2, D), "a", (4, 6), 0, False),
    ("g_ffn", (2, D), "a", (6, 8), 0, False),
    ("g_final", (1, D), "a", (8, 9), 0, False),
    ("ev_a_conv_w", (KA, 128), "a", (16, 16 + KA), 0, True),
    ("ev_a_conv_b", (1, DA), "a", (11, 12), 0, False),
    ("ev_a_ln_g", (1, DA), "a", (11, 12), DA, False),
    ("ev_a_ln_b", (1, DA), "a", (12, 13), 0, False),
    ("ev_b_conv_w", (KB, 128), "a", (48, 48 + KB), 0, True),
    ("ev_b_conv_b", (1, DA), "a", (12, 13), DA, False),
    ("od_c_ln_g", (1, 256), "a", (9, 10), 0, True),
    ("od_c_ln_b", (1, 256), "a", (10, 11), 0, True),
    ("od_w_s", (GROUPS * CHUNK, CHUNK), "b", (0, GROUPS * CHUNK), 0, False),
    ("od_b_s", (GROUPS, CHUNK), "a", (56, 64), 0, False),
)


def _adamw_small(ga, gb, wmv):
    ns = len(_SMALL)

    def body(*refs):
        ga_ref, gb_ref = refs[0], refs[1]
        ins = refs[2:2 + 3 * ns]
        outs = refs[2 + 3 * ns:]
        x, y, _ = _pos()
        chip = 2 * x + y
        for j, (_, shape, pack, (r0, r1), col, sharded) in enumerate(_SMALL):
            src = ga_ref if pack == "a" else gb_ref
            width = shape[1]
            if sharded:
                g = jnp.zeros(shape, F32)
                for k in range(NSH):
                    blk = src[r0:r1, col + width * k:col + width * (k + 1)]
                    g = g + jnp.where(chip == k, blk, 0.0)
            else:
                g = src[r0:r1, col:col + width]
            w_ref, m_ref, v_ref = ins[3 * j:3 * j + 3]
            delta, mn, vn = _adamw_math(w_ref[...], g, m_ref[...], v_ref[...])
            go_ref, d_ref, mo_ref, vo_ref = outs[4 * j:4 * j + 4]
            go_ref[...] = g
            d_ref[...] = delta
            mo_ref[...] = mn
            vo_ref[...] = vn

    vm = pl.BlockSpec(memory_space=pltpu.VMEM)
    flat = [a for trio in wmv for a in trio]
    out_shape = []
    for (_, shape, *_rest) in _SMALL:
        out_shape += [_sds(shape, F32)] * 4
    outs = pl.pallas_call(
        body, name="adamw_small",
        in_specs=[vm] * (2 + 3 * ns), out_specs=[vm] * (4 * ns), out_shape=out_shape,
    )(ga, gb, *flat)
    return [tuple(outs[4 * j:4 * j + 4]) for j in range(ns)]


def _small_params(gs, W):
    unshard = lambda a: jnp.transpose(a, (1, 0, 2)).reshape(a.shape[1], NSH * a.shape[2])
    od_w_s, od_b_s = W["od_w_s"], W["od_b_s"]
    return dict(
        g_mix=W["g_mix"], g_xattn=W["g_xattn"], g_mem=W["g_mem"], g_ffn=W["g_ffn"],
        g_final=W["g_final"].reshape(1, D),
        wa=unshard(gs[:, 0:32, 0:128]), ba=W["ev_a_conv_b"], lng_a=W["ev_a_ln_g"], lnb_a=W["ev_a_ln_b"],
        wb=unshard(gs[:, 32:40, 0:128]), bb=W["ev_b_conv_b"],
        lng_c=unshard(gs[:, 40:41, :]), lnb_c=unshard(gs[:, 41:42, :]),
        ws=od_w_s[0].astype(BF), wst=jnp.swapaxes(od_w_s[0], 1, 2).astype(BF),
        bsb=jnp.broadcast_to(od_b_s[0][:, :, None], (GROUPS, CHUNK, CHUNK)),
    )


def _small_grad_packs(g):
    z = lambda r, c: jnp.zeros((r, c), F32)
    wide = lambda a: jnp.concatenate([a, z(a.shape[0], D - a.shape[1])], axis=1)
    rows = [g["g_mix"], g["g_xattn"], g["g_mem"], g["g_ffn"], g["g_final"], g["od_c_ln_g"], g["od_c_ln_b"],
            jnp.concatenate([g["ev_a_conv_b"], g["ev_a_ln_g"]], axis=1),
            jnp.concatenate([g["ev_a_ln_b"], g["ev_b_conv_b"]], axis=1),
            z(3, D), wide(g["ev_a_conv_w"]), wide(g["ev_b_conv_w"]), wide(g["od_b_s"])]
    a = jnp.concatenate(rows, axis=0)
    assert a.shape == (SA_ROWS, D), a.shape
    return a, g["od_w_s"].reshape(GROUPS * CHUNK, CHUNK)


def _step(x, mem, tgt, W, P, c_idx, tm_f, tm_b):
    row = lambda a, i: a[i:i + 1]

    def pre_sum(names, arrays, tag):
        recv = _sibling_swap(arrays, f"rs_swap_{tag}")
        return [_add_half(a, r, c_idx, f"rs_add_{n}") for n, a, r in zip(names, arrays, recv)]

    cev, gs = _gather_now([P["cev"], P["gs"]])
    sm = _small_params(gs, W)
    (wev,) = _assemble(cev, [(0, 640)], "assemble_ev")
    z, evo = _ev_in_fwd(x, row(sm["g_mix"], 0), wev, tm_f, carry=("gather", [P["evo"]]))
    h1, a2, c2, cat, xa0, cf0 = _ev_mix_fwd(x, z, sm["wa"], sm["ba"], sm["lng_a"], sm["lnb_a"], sm["wb"],
                                            sm["bb"], evo, tm_f, carry=("gather", [P["xa0"], P["cf0"]]))
    wg0, wu0 = _assemble(cf0, [(0, FSH), (FPAD, FSH)], "assemble_ffn0")
    k0, v0 = _kv_proj(mem, row(sm["g_mem"], 0), xa0, 0)
    h2, q0, dn0, cod = _xattn_fwd(h1, row(sm["g_xattn"], 0), k0, v0, xa0, 0, tm_f,
                                  carry=("gather", [P["dn0"], P["cod"]]))
    (wod,) = _assemble(cod, [(0, 512)], "assemble_od")
    h3, gt0, up0, odo, xa1, cf1 = _ffn_fwd(h2, row(sm["g_ffn"], 0), wg0, wu0, dn0, 0, tm_b,
                                           carry=("gather", [P["odo"], P["xa1"], P["cf1"]]))
    wg1, wu1 = _assemble(cf1, [(0, FSH), (FPAD, FSH)], "assemble_ffn1")
    h4, zpre, dn1 = _sgu_fwd(h3, row(sm["g_mix"], 1), sm["lng_c"], sm["lnb_c"], sm["ws"], sm["bsb"], wod, odo,
                             tm_f, carry=("gather", [P["dn1"]]))
    k1, v1 = _kv_proj(mem, row(sm["g_mem"], 1), xa1, 1)
    h5, q1 = _xattn_fwd(h4, row(sm["g_xattn"], 1), k1, v1, xa1, 1, tm_f)
    h6, gt1, up1 = _ffn_fwd(h5, row(sm["g_ffn"], 1), wg1, wu1, dn1, 1, tm_b)
    dh6, loss, dg_final = _final_loss(h6, tgt, sm["g_final"], tm_f)

    g = {"g_final": dg_final}
    parts, gots = {}, {}
    dh5, dg_ffn1, dgt, dup = _ffn_bwd_dx(h5, dh6, gt1, up1, row(sm["g_ffn"], 1), wg1, wu1, dn1, 1, tm_b)
    g_cf1 = _ffn_bwd_dwgu(h5, row(sm["g_ffn"], 1), dgt, dup, 1, tm_b)
    g_dn1 = _ffn_bwd_dwd(gt1, up1, dh6, 1, tm_b)
    parts["cf1"], parts["dn1"] = pre_sum(("cf1", "dn1"), (g_cf1, g_dn1), "a")
    dh4, dk, dv, dg_xa1, g_xa1, gots["cf1"], gots["dn1"] = _xattn_bwd(
        h4, dh5, q1, row(sm["g_xattn"], 1), k1, v1, xa1, 1, tm_b, carry=("xchg", [parts["cf1"], parts["dn1"]], []))
    dg_mem1, g_xa1 = _kv_proj_bwd(mem, row(sm["g_mem"], 1), dk, dv, xa1, g_xa1, 1)
    dh3, g["od_w_s"], g["od_b_s"], g["od_c_ln_g"], g["od_c_ln_b"], dg_mix1, g_cod, g_odo = _sgu_bwd(
        h3, dh4, zpre, row(sm["g_mix"], 1), sm["lng_c"], sm["lnb_c"], sm["ws"], sm["wst"], sm["bsb"], wod, odo, tm_b)
    parts["xa1"], parts["cod"], parts["odo"] = pre_sum(("xa1", "cod", "odo"), (g_xa1, g_cod, g_odo), "b")
    dh2, dg_ffn0, dgt, dup, gots["xa1"], gots["cod"], gots["odo"] = _ffn_bwd_dx(
        h2, dh3, gt0, up0, row(sm["g_ffn"], 0), wg0, wu0, dn0, 0, tm_b,
        carry=("xchg", [parts["xa1"], parts["cod"], parts["odo"]], []))
    g_cf0 = _ffn_bwd_dwgu(h2, row(sm["g_ffn"], 0), dgt, dup, 0, tm_b)
    g_dn0 = _ffn_bwd_dwd(gt0, up0, dh3, 0, tm_b)
    parts["cf0"], parts["dn0"] = pre_sum(("cf0", "dn0"), (g_cf0, g_dn0), "c")
    dh1, dk, dv, dg_xa0, g_xa0, gots["cf0"], gots["dn0"] = _xattn_bwd(
        h1, dh2, q0, row(sm["g_xattn"], 0), k0, v0, xa0, 0, tm_b, carry=("xchg", [parts["cf0"], parts["dn0"]], []))
    dg_mem0, g_xa0 = _kv_proj_bwd(mem, row(sm["g_mem"], 0), dk, dv, xa0, g_xa0, 0)
    (parts["xa0"],) = pre_sum(("xa0",), (g_xa0,), "d")
    da2, dc2, dgb, g["ev_a_ln_g"], g["ev_a_ln_b"], g_evo, gots["xa0"] = _ev_mix_bwd(
        dh1, cat, a2, c2, z, sm["lng_a"], sm["lnb_a"], evo, tm_b, carry=("xchg", [parts["xa0"]], []))
    dx, g["ev_a_conv_w"], g["ev_a_conv_b"], g["ev_b_conv_w"], g["ev_b_conv_b"], dg_mix0, g_cev = _ev_in_bwd(
        x, dh1, z, da2, dc2, dgb, row(sm["g_mix"], 0), sm["wa"], sm["wb"], wev, tm_b)
    g["g_mix"] = jnp.concatenate([dg_mix0, dg_mix1], axis=0)
    g["g_xattn"] = jnp.concatenate([dg_xa0, dg_xa1], axis=0)
    g["g_mem"] = jnp.concatenate([dg_mem0, dg_mem1], axis=0)
    g["g_ffn"] = jnp.concatenate([dg_ffn0, dg_ffn1], axis=0)
    ga, gb = _small_grad_packs(g)
    parts["evo"], parts["cev"], parts["ga"], parts["gb"] = pre_sum(
        ("evo", "cev", "ga", "gb"), (g_evo, g_cev, ga[None], gb[None]), "e")
    gots["evo"], gots["cev"], gots["ga"], gots["gb"] = _xchg_now(
        [parts["evo"], parts["cev"]], [parts["ga"][0], parts["gb"][0]])
    return loss, dx, parts, gots


def kernel(x, mem, g_mix, g_xattn, g_mem, g_ffn, g_final, ev_w_in, ev_a_conv_w, ev_a_conv_b, ev_a_ln_g, ev_a_ln_b, ev_b_conv_w, ev_b_conv_b, ev_w_out, od_w_in, od_c_ln_g, od_c_ln_b, od_w_s, od_b_s, od_w_out, xa_w_q, xa_w_k, xa_w_v, xa_w_o, ffn_w_gate, ffn_w_up, ffn_w_down, loss_target, m_g_mix, m_g_xattn, m_g_mem, m_g_ffn, m_g_final, m_ev_w_in, m_ev_a_conv_w, m_ev_a_conv_b, m_ev_a_ln_g, m_ev_a_ln_b, m_ev_b_conv_w, m_ev_b_conv_b, m_ev_w_out, m_od_w_in, m_od_c_ln_g, m_od_c_ln_b, m_od_w_s, m_od_b_s, m_od_w_out, m_xa_w_q, m_xa_w_k, m_xa_w_v, m_xa_w_o, m_ffn_w_gate, m_ffn_w_up, m_ffn_w_down, v_g_mix, v_g_xattn, v_g_mem, v_g_ffn, v_g_final, v_ev_w_in, v_ev_a_conv_w, v_ev_a_conv_b, v_ev_a_ln_g, v_ev_a_ln_b, v_ev_b_conv_w, v_ev_b_conv_b, v_ev_w_out, v_od_w_in, v_od_c_ln_g, v_od_c_ln_b, v_od_w_s, v_od_b_s, v_od_w_out, v_xa_w_q, v_xa_w_k, v_xa_w_v, v_xa_w_o, v_ffn_w_gate, v_ffn_w_up, v_ffn_w_down):
    W = dict(g_mix=g_mix, g_xattn=g_xattn, g_mem=g_mem, g_ffn=g_ffn, g_final=g_final, ev_w_in=ev_w_in,
             ev_a_conv_w=ev_a_conv_w, ev_a_conv_b=ev_a_conv_b, ev_a_ln_g=ev_a_ln_g, ev_a_ln_b=ev_a_ln_b,
             ev_b_conv_w=ev_b_conv_w, ev_b_conv_b=ev_b_conv_b, ev_w_out=ev_w_out, od_w_in=od_w_in,
             od_c_ln_g=od_c_ln_g, od_c_ln_b=od_c_ln_b, od_w_s=od_w_s, od_b_s=od_b_s, od_w_out=od_w_out,
             xa_w_q=xa_w_q, xa_w_k=xa_w_k, xa_w_v=xa_w_v, xa_w_o=xa_w_o, ffn_w_gate=ffn_w_gate,
             ffn_w_up=ffn_w_up, ffn_w_down=ffn_w_down)
    M = dict(g_mix=m_g_mix, g_xattn=m_g_xattn, g_mem=m_g_mem, g_ffn=m_g_ffn, g_final=m_g_final, ev_w_in=m_ev_w_in,
             ev_a_conv_w=m_ev_a_conv_w, ev_a_conv_b=m_ev_a_conv_b, ev_a_ln_g=m_ev_a_ln_g, ev_a_ln_b=m_ev_a_ln_b,
             ev_b_conv_w=m_ev_b_conv_w, ev_b_conv_b=m_ev_b_conv_b, ev_w_out=m_ev_w_out, od_w_in=m_od_w_in,
             od_c_ln_g=m_od_c_ln_g, od_c_ln_b=m_od_c_ln_b, od_w_s=m_od_w_s, od_b_s=m_od_b_s, od_w_out=m_od_w_out,
             xa_w_q=m_xa_w_q, xa_w_k=m_xa_w_k, xa_w_v=m_xa_w_v, xa_w_o=m_xa_w_o, ffn_w_gate=m_ffn_w_gate,
             ffn_w_up=m_ffn_w_up, ffn_w_down=m_ffn_w_down)
    V = dict(g_mix=v_g_mix, g_xattn=v_g_xattn, g_mem=v_g_mem, g_ffn=v_g_ffn, g_final=v_g_final, ev_w_in=v_ev_w_in,
             ev_a_conv_w=v_ev_a_conv_w, ev_a_conv_b=v_ev_a_conv_b, ev_a_ln_g=v_ev_a_ln_g, ev_a_ln_b=v_ev_a_ln_b,
             ev_b_conv_w=v_ev_b_conv_w, ev_b_conv_b=v_ev_b_conv_b, ev_w_out=v_ev_w_out, od_w_in=v_od_w_in,
             od_c_ln_g=v_od_c_ln_g, od_c_ln_b=v_od_c_ln_b, od_w_s=v_od_w_s, od_b_s=v_od_b_s, od_w_out=v_od_w_out,
             xa_w_q=v_xa_w_q, xa_w_k=v_xa_w_k, xa_w_v=v_xa_w_v, xa_w_o=v_xa_w_o, ffn_w_gate=v_ffn_w_gate,
             ffn_w_up=v_ffn_w_up, ffn_w_down=v_ffn_w_down)
    order = list(W)
    cx, cy, cc = _pos()
    chip = (2 * cx + cy).astype(jnp.int32)
    cc = cc.astype(jnp.int32)
    chip_idx = jnp.reshape(chip, (1,))
    c_idx = jnp.reshape(cc, (1,))

    P = {}
    P["evo"], P["odo"], P["dn0"], P["dn1"], P["xa0"], P["xa1"] = _pack_rows(
        chip_idx, ffn_w_down, xa_w_q, xa_w_o, xa_w_k, xa_w_v, ev_w_out, od_w_out)
    P["cev"], P["cod"], P["cf0"], P["cf1"], P["gs"] = _pack_cols(
        chip_idx, ffn_w_gate, ffn_w_up, od_w_in, ev_w_in, ev_a_conv_w, ev_b_conv_w, od_c_ln_g, od_c_ln_b)

    loss, dx, parts, gots = _step(x[0], mem[0], loss_target[0], W, P, c_idx, TM_FWD, TM_BWD)
    loss = lax.psum(loss[0, 0], ("x", "y", "c"))

    chip_c = jnp.stack([chip, cc])
    zero_c = jnp.stack([jnp.zeros((), jnp.int32), cc])
    names = list(parts)
    halves = [_sum_chips(parts[n], gots[n], zero_c if n in ("ga", "gb") else chip_c, f"rs_sum_{n}") for n in names]
    red = dict(zip(names, _sibling_join(halves)))

    res = {}
    xa = [red["xa0"], red["xa1"]]
    cf = [red["cf0"], red["cf1"]]
    big = {
        "ffn_w_down": ([red["dn0"], red["dn1"]], (FSH, D), lambda i: (0, 0), FSH),
        "xa_w_q": (xa, (256, D), lambda i: (XQ, 0), 256),
        "xa_w_o": (xa, (256, D), lambda i: (XO, 0), 256),
        "xa_w_k": (xa, (256, D), lambda i: (XK, 0), 256),
        "xa_w_v": (xa, (256, D), lambda i: (XV, 0), 256),
        "ev_w_out": ([red["evo"]], (256, D), lambda i: (0, 0), 256),
        "od_w_out": ([red["odo"]], (256, D), lambda i: (0, 0), 256),
        "ffn_w_gate": (cf, (256, FPAD), lambda i: (i, 0), 256),
        "ffn_w_up": (cf, (256, FPAD), lambda i: (i, 1), 256),
        "od_w_in": ([red["cod"]], (256, 512), lambda i: (i, 0), 256),
        "ev_w_in": ([red["cev"]], (256, 640), lambda i: (i, 0), 256),
    }
    for name, (srcs, gblock, gidx, rb) in big.items():
        res[name] = _adamw_big(W[name], M[name], V[name], srcs, gblock, gidx, f"adamw_{name}", rb)
    shape2 = {name: shape for name, shape, *_ in _SMALL}
    wmv = [tuple(d[name].reshape(shape2[name]) for d in (W, M, V)) for name, *_ in _SMALL]
    for (name, *_), outs in zip(_SMALL, _adamw_small(red["ga"], red["gb"], wmv)):
        res[name] = tuple(o.reshape(W[name].shape) for o in outs)

    grad_x = dx[None]
    return (loss, grad_x, *[res[n][0] for n in order], *[res[n][1] for n in order],
            *[res[n][2] for n in order], *[res[n][3] for n in order])
```

```python
import jax
import jax.numpy as jnp
from jax import lax
from jax.experimental import pallas as pl
from jax.experimental.pallas import tpu as pltpu

BF = jnp.bfloat16
F32 = jnp.float32

D = 1024
DA = 512
DZ = 2560
DFF = 2816
NSH = 4
FSH = DFF // NSH
FH = DFF // 2
NMEM = 256
HEADS = 4
HD = D // HEADS
CHUNK = 128
GROUPS = 8
KA = 31
KB = 3
HALO = 16
RC = 32
RMS_EPS = 1e-6
LN_EPS = 1e-5
ATT_SCALE = HD ** -0.5
TM_FWD = 512
TM_BWD = 256
TM_WGRAD = 1024

ADAM_LR = 0.001
ADAM_B1 = 0.9
ADAM_B2 = 0.999
ADAM_EPS = 1e-08
ADAM_WD = 0.01
ADAM_STEP = 10

XQ, XO, XK, XV = 0, 1, 2, 3
SP_ROWS = 48
SA_ROWS = 64

MESH = pl.DeviceIdType.MESH
ANY = pl.BlockSpec(memory_space=pl.ANY)


def _dot(a, b):
    return jnp.dot(a, b, preferred_element_type=F32)


def _dot_nt(a, b):
    return lax.dot_general(a, b, (((1,), (1,)), ((), ())), preferred_element_type=F32)


def _dot_tn(a, b):
    return lax.dot_general(a, b, (((0,), (0,)), ((), ())), preferred_element_type=F32)


def _rms(h):
    r = lax.rsqrt(jnp.mean(h * h, axis=-1, keepdims=True) + RMS_EPS)
    return h * r, r


def _rms_bwd(dn, hhat, r, g):
    dhh = dn * g
    return r * (dhh - hhat * jnp.mean(dhh * hhat, axis=-1, keepdims=True))


def _ln(x):
    mu = jnp.mean(x, axis=-1, keepdims=True)
    xc = x - mu
    rstd = lax.rsqrt(jnp.mean(xc * xc, axis=-1, keepdims=True) + LN_EPS)
    return xc * rstd, rstd


def _ln_bwd(dy, y, rstd):
    return rstd * (dy - jnp.mean(dy, axis=-1, keepdims=True) - y * jnp.mean(dy * y, axis=-1, keepdims=True))


def _colsum(x):
    return jnp.sum(x, axis=0, keepdims=True)


def _sigmoid(x):
    return 1.0 / (1.0 + jnp.exp(-x))


_GELU_C = 0.7978845608028654
_GELU_A = 0.044715


def _gelu(x):
    t = jnp.tanh(_GELU_C * (x + _GELU_A * x * x * x))
    return 0.5 * x * (1.0 + t)


def _gelu_grad(x):
    t = jnp.tanh(_GELU_C * (x + _GELU_A * x * x * x))
    return 0.5 * (1.0 + t) + 0.5 * x * (1.0 - t * t) * _GELU_C * (1.0 + 3.0 * _GELU_A * x * x)


def _first(i):
    return i == 0


def _last(i):
    return i == pl.num_programs(0) - 1


def _const(shape, idx):
    return pl.BlockSpec(shape, lambda *_, _idx=tuple(idx): _idx, pipeline_mode=pl.Buffered(1))


def _whole(a):
    return _const(a.shape, (0,) * a.ndim)


def _rows(tm, width, col=0):
    return pl.BlockSpec((tm, width), lambda i, _c=col: (i, _c))


def _params(ndim=1, vmem_mb=56):
    return pltpu.CompilerParams(dimension_semantics=("arbitrary",) * ndim, vmem_limit_bytes=vmem_mb << 20)


def _sds(shape, dtype):
    return jax.ShapeDtypeStruct(shape, dtype)


def _pos():
    return lax.axis_index("x"), lax.axis_index("y"), lax.axis_index("c")


def _other_chips(x, y):
    return [(1 - x, y), (x, 1 - y), (1 - x, 1 - y)]


def _half(ref, hc, lead=()):
    r = ref.shape[-2] // 2
    return ref.at[(*lead, pl.ds(pl.multiple_of(hc * r, 8), r), slice(None))]


def _rcopy(src, dst, ssem, rsem, to):
    return pltpu.make_async_remote_copy(src_ref=src, dst_ref=dst, send_sem=ssem, recv_sem=rsem,
                                        device_id=to, device_id_type=MESH)


def _gather_start(bufs, sems):
    ici_s, ici_r, _, _ = sems
    x, y, c = _pos()
    me = 2 * x + y
    for k, (cx, cy) in enumerate(_other_chips(x, y)):
        for t, buf in enumerate(bufs):
            mine = _half(buf, c, (me,))
            _rcopy(mine, mine, ici_s.at[3 * t + k], ici_r.at[3 * t + k], (cx, cy, c)).start()


def _gather_finish(bufs, sems):
    ici_s, ici_r, d2d_s, d2d_r = sems
    x, y, c = _pos()
    me = 2 * x + y
    sib = (x, y, 1 - c)
    chips = _other_chips(x, y)
    for k, (cx, cy) in enumerate(chips):
        for t, buf in enumerate(bufs):
            landed = _half(buf, c, (2 * cx + cy,))
            _rcopy(landed, landed, ici_s.at[3 * t + k], ici_r.at[3 * t + k], (cx, cy, c)).wait_recv()
            _rcopy(landed, landed, d2d_s.at[3 * t + k], d2d_r.at[3 * t + k], sib).start()
    for k, (cx, cy) in enumerate(chips):
        for t, buf in enumerate(bufs):
            got = _half(buf, 1 - c, (2 * cx + cy,))
            _rcopy(got, got, d2d_s.at[3 * t + k], d2d_r.at[3 * t + k], sib).wait_recv()
    for k, (cx, cy) in enumerate(chips):
        for t, buf in enumerate(bufs):
            mine = _half(buf, c, (me,))
            _rcopy(mine, mine, ici_s.at[3 * t + k], ici_r.at[3 * t + k], (cx, cy, c)).wait_send()
            landed = _half(buf, c, (2 * cx + cy,))
            _rcopy(landed, landed, d2d_s.at[3 * t + k], d2d_r.at[3 * t + k], sib).wait_send()


def _xchg_copies(srcs, dsts, nbig, sems):
    ssem, rsem = sems
    x, y, c = _pos()
    cps = []
    for k, (cx, cy) in enumerate(_other_chips(x, y)):
        for t, (src, dst) in enumerate(zip(srcs, dsts)):
            s = src.at[2 * cx + cy] if t < nbig else src
            cps.append(_rcopy(s, dst.at[k], ssem.at[3 * t + k], rsem.at[3 * t + k], (cx, cy, c)))
    return cps


def _gather_sems(n):
    return [pltpu.SemaphoreType.DMA((3 * n,))] * 4


def _xchg_sems(n):
    return [pltpu.SemaphoreType.DMA((3 * n,))] * 2


def _xchg_out_shapes(items):
    return [_sds((3,) + a.shape[-2:], a.dtype) for a in items]


def _call(body, *, name, grid, in_specs, out_specs, out_shape, args, scratch_shapes=(), aliases=None, carry=None):
    in_specs, out_specs, out_shape = list(in_specs), list(out_specs), list(out_shape)
    args = list(args)
    aliases = dict(aliases or {})
    n_in, n_out, n_scr = len(args), len(out_shape), len(scratch_shapes)
    kind = carry[0] if carry else None
    if kind == "gather":
        cin = list(carry[1])
        cout_shape = [_sds(a.shape, a.dtype) for a in cin]
        sems = _gather_sems(len(cin))
        for t in range(len(cin)):
            aliases[n_in + t] = n_out + t
    elif kind == "xchg":
        cin = list(carry[1]) + list(carry[2])
        nbig = len(carry[1])
        cout_shape = _xchg_out_shapes(cin)
        sems = _xchg_sems(len(cin))
    else:
        cin, cout_shape, sems = [], [], []
    nc = len(cin)

    def wrapped(*refs):
        ins = refs[:n_in]
        c_in = refs[n_in:n_in + nc]
        outs = refs[n_in + nc:n_in + nc + n_out]
        c_out = refs[n_in + nc + n_out:n_in + 2 * nc + n_out]
        scr = refs[n_in + 2 * nc + n_out:n_in + 2 * nc + n_out + n_scr]
        sem = refs[n_in + 2 * nc + n_out + n_scr:]
        i = pl.program_id(0)
        if kind:
            @pl.when(_first(i))
            def _():
                if kind == "gather":
                    _gather_start(c_out, sem)
                else:
                    for cp in _xchg_copies(c_in, c_out, nbig, sem):
                        cp.start()
        body(*ins, *outs, *scr)
        if kind:
            @pl.when(_last(i))
            def _():
                if kind == "gather":
                    _gather_finish(c_out, sem)
                else:
                    for cp in _xchg_copies(c_in, c_out, nbig, sem):
                        cp.wait()

    outs = pl.pallas_call(
        wrapped, name=name, grid=grid,
        in_specs=in_specs + [ANY] * nc, out_specs=out_specs + [ANY] * nc,
        out_shape=out_shape + cout_shape,
        scratch_shapes=list(scratch_shapes) + sems,
        input_output_aliases=aliases,
        compiler_params=_params(len(grid)),
    )(*args, *cin)
    return list(outs)


def _gather_now(bufs):
    n = len(bufs)

    def body(*refs):
        dst = refs[n:2 * n]
        sems = refs[2 * n:]
        _gather_start(dst, sems)
        _gather_finish(dst, sems)

    return pl.pallas_call(
        body, name="gather_first", in_specs=[ANY] * n, out_specs=[ANY] * n,
        out_shape=[_sds(a.shape, a.dtype) for a in bufs],
        input_output_aliases={t: t for t in range(n)},
        scratch_shapes=_gather_sems(n),
    )(*bufs)


def _xchg_now(big, small):
    items = list(big) + list(small)
    n = len(items)

    def body(*refs):
        cps = _xchg_copies(refs[:n], refs[n:2 * n], len(big), refs[2 * n:])
        for cp in cps:
            cp.start()
        for cp in cps:
            cp.wait()

    return pl.pallas_call(
        body, name="rs_xchg_last", in_specs=[ANY] * n, out_specs=[ANY] * n,
        out_shape=_xchg_out_shapes(items), scratch_shapes=_xchg_sems(n),
    )(*items)


def _sibling_swap(items, name):
    n = len(items)

    def body(*refs):
        src = refs[:n]
        dst = refs[n:2 * n]
        ssem, rsem = refs[2 * n:]
        x, y, c = _pos()
        sib = (x, y, 1 - c)
        cps = []
        for t in range(n):
            lead = (slice(None),) * (len(src[t].shape) - 2)
            cp = _rcopy(_half(src[t], 1 - c, lead), dst[t], ssem.at[t], rsem.at[t], sib)
            cp.start()
            cps.append(cp)
        for cp in cps:
            cp.wait()

    return pl.pallas_call(
        body, name=name, in_specs=[ANY] * n, out_specs=[ANY] * n,
        out_shape=[_sds(a.shape[:-2] + (a.shape[-2] // 2, a.shape[-1]), a.dtype) for a in items],
        scratch_shapes=[pltpu.SemaphoreType.DMA((n,)), pltpu.SemaphoreType.DMA((n,))],
    )(*items)


def _sibling_join(items):
    n = len(items)

    def body(*refs):
        dst = refs[n:2 * n]
        ssem, rsem = refs[2 * n:]
        x, y, c = _pos()
        sib = (x, y, 1 - c)
        cps = []
        for t in range(n):
            mine = _half(dst[t], c)
            cp = _rcopy(mine, mine, ssem.at[t], rsem.at[t], sib)
            cp.start()
            cps.append(cp)
        for t, cp in enumerate(cps):
            theirs = _half(dst[t], 1 - c)
            _rcopy(theirs, theirs, ssem.at[t], rsem.at[t], sib).wait_recv()
            cp.wait_send()

    return pl.pallas_call(
        body, name="rs_sibling_join", in_specs=[ANY] * n, out_specs=[ANY] * n,
        out_shape=[_sds(a.shape, a.dtype) for a in items],
        input_output_aliases={t: t for t in range(n)},
        scratch_shapes=[pltpu.SemaphoreType.DMA((n,))] * 2,
    )(*items)


def _ev_in_fwd(h, g, wev, tm, carry=None):
    s = h.shape[0]

    def body(h_ref, g_ref, w_ref, z_ref):
        hhat, _ = _rms(h_ref[...])
        n = (hhat * g_ref[...]).astype(BF)
        z_ref[...] = _dot(n, w_ref[...]).astype(BF)

    return _call(
        body, name="ev_in_fwd", grid=(s // tm,),
        in_specs=[_rows(tm, D), _const((1, D), (0, 0)), _const((D, DZ), (0, 0))],
        out_specs=[_rows(tm, DZ)], out_shape=[_sds((s, DZ), BF)],
        args=(h, g, wev), carry=carry)


def _halo_specs(tm, width, s, col=0):
    per = tm // HALO
    nh = s // HALO
    prev = pl.BlockSpec((HALO, width), lambda i, _c=col: (jnp.maximum(i * per - 1, 0), _c))
    nxt = pl.BlockSpec((HALO, width), lambda i, _c=col: (jnp.minimum((i + 1) * per, nh - 1), _c))
    return prev, nxt


def _fill_pad(pad_ref, prev, main, nxt, i, tm):
    pad_ref[pl.ds(0, HALO), :] = jnp.where(_first(i), 0.0, prev)
    pad_ref[pl.ds(HALO, tm), :] = main
    pad_ref[pl.ds(HALO + tm, HALO), :] = jnp.where(_last(i), 0.0, nxt)


def _shift8(xs_ref, pad_ref, tm):
    for j in range(8):
        xs_ref[j] = pad_ref[pl.ds(j, tm + 2 * HALO - 8), :]


def _tap(xs_ref, r, off):
    return xs_ref[off % 8, pl.ds(r + 8 * (off // 8), RC), :]


def _glu_a(z):
    return z[:, :DA] * _sigmoid(z[:, DA:2 * DA])


def _gate_c(z):
    return z[:, 2048:] * z[:, 1024:1536]


def _ev_mix_fwd(h, z, wa, ba, lng, lnb, wb, bb, evo, tm, carry=None):
    s = h.shape[0]

    def body(h_ref, z_ref, zp_ref, zn_ref, wa_ref, ba_ref, lng_ref, lnb_ref, wb_ref, bb_ref, wo_ref,
             h1_ref, a2_ref, c2_ref, cat_ref, apad, cpad, c2f, xs):
        i = pl.program_id(0)
        zm = z_ref[...].astype(F32)
        zp = zp_ref[...].astype(F32)
        zn = zn_ref[...].astype(F32)
        _fill_pad(apad, _glu_a(zp), _glu_a(zm), _glu_a(zn), i, tm)
        _fill_pad(cpad, _gate_c(zp), _gate_c(zm), _gate_c(zn), i, tm)
        _shift8(xs, apad, tm)
        for r in range(0, tm, RC):
            acc = jnp.zeros((RC, DA), F32) + ba_ref[...]
            for k in range(KA):
                acc = acc + _tap(xs, r, k + 1) * wa_ref[pl.ds(k, 1), :]
            a2_ref[pl.ds(r, RC), :] = acc
            acc = jnp.zeros((RC, DA), F32) + bb_ref[...]
            for k in range(KB):
                acc = acc + cpad[pl.ds(r + HALO - 1 + k, RC), :] * wb_ref[pl.ds(k, 1), :]
            c2f[pl.ds(r, RC), :] = acc
        y, _ = _ln(a2_ref[...])
        a3 = y * lng_ref[...] + lnb_ref[...]
        cat_ref[:, :DA] = (a3 * _sigmoid(a3)).astype(BF)
        c2 = c2f[...]
        c2_ref[...] = c2.astype(BF)
        cat_ref[:, DA:] = (zm[:, 1536:2048] * c2).astype(BF)
        h1_ref[...] = h_ref[...] + _dot(cat_ref[...], wo_ref[...].reshape(D, D))

    zp_spec, zn_spec = _halo_specs(tm, DZ, s)
    small = lambda rows: _const((rows, DA), (0, 0))
    return _call(
        body, name="ev_mix_fwd", grid=(s // tm,),
        in_specs=[_rows(tm, D), _rows(tm, DZ), zp_spec, zn_spec, small(32), small(1), small(1), small(1),
                  small(8), small(1), _whole(evo)],
        out_specs=[_rows(tm, D), _rows(tm, DA), _rows(tm, DA), _rows(tm, D)],
        out_shape=[_sds((s, D), F32), _sds((s, DA), F32), _sds((s, DA), BF), _sds((s, D), BF)],
        scratch_shapes=[pltpu.VMEM((tm + 2 * HALO, DA), F32), pltpu.VMEM((tm + 2 * HALO, DA), F32),
                        pltpu.VMEM((tm, DA), F32), pltpu.VMEM((8, tm + 2 * HALO - 8, DA), F32)],
        args=(h, z, z, z, wa, ba, lng, lnb, wb, bb, evo), carry=carry)


def _ev_mix_bwd(dh, cat, a2, c2, z, lng, lnb, evo, tm, carry=None):
    s = dh.shape[0]

    def body(dh_ref, cat_ref, a2_ref, c2_ref, zgb_ref, lng_ref, lnb_ref, wo_ref,
             da2_ref, dc2_ref, dgb_ref, dlng_ref, dlnb_ref, gwo_ref, wacc):
        i = pl.program_id(0)

        @pl.when(_first(i))
        def _():
            wacc[...] = jnp.zeros_like(wacc)
            dlng_ref[...] = jnp.zeros_like(dlng_ref)
            dlnb_ref[...] = jnp.zeros_like(dlnb_ref)

        dhb = dh_ref[...].astype(BF)
        dcat = _dot_nt(dhb, wo_ref[...].reshape(D, D))
        wacc[...] += _dot_tn(cat_ref[...], dhb)
        y, rstd = _ln(a2_ref[...])
        a3 = y * lng_ref[...] + lnb_ref[...]
        sg = _sigmoid(a3)
        da3 = dcat[:, :DA] * (sg * (1.0 + a3 * (1.0 - sg)))
        dlng_ref[...] += _colsum(da3 * y)
        dlnb_ref[...] += _colsum(da3)
        da2_ref[...] = _ln_bwd(da3 * lng_ref[...], y, rstd).astype(BF)
        db = dcat[:, DA:]
        dc2_ref[...] = (db * zgb_ref[...].astype(F32)).astype(BF)
        dgb_ref[...] = (db * c2_ref[...].astype(F32)).astype(BF)

        @pl.when(_last(i))
        def _():
            gwo_ref[...] = wacc[...].astype(BF).reshape(NSH, 256, D)

    small = _const((1, DA), (0, 0))
    return _call(
        body, name="ev_mix_bwd", grid=(s // tm,),
        in_specs=[_rows(tm, D), _rows(tm, D), _rows(tm, DA), _rows(tm, DA), _rows(tm, DA, 3), small, small,
                  _whole(evo)],
        out_specs=[_rows(tm, DA), _rows(tm, DA), _rows(tm, DA), small, small, _whole(evo)],
        out_shape=[_sds((s, DA), BF), _sds((s, DA), BF), _sds((s, DA), BF), _sds((1, DA), F32),
                   _sds((1, DA), F32), _sds(evo.shape, BF)],
        scratch_shapes=[pltpu.VMEM((D, D), F32)],
        args=(dh, cat, a2, c2, z, lng, lnb, evo), carry=carry)


def _ev_in_bwd(h, dh, z, da2, dc2, dgb, g, wa, wb, wev, tm):
    s = h.shape[0]

    def body(h_ref, dh_ref, z_ref, zp_ref, zn_ref, da_ref, dap_ref, dan_ref, dc_ref, dcp_ref, dcn_ref,
             dgb_ref, g_ref, wa_ref, wb_ref, w_ref,
             dh0_ref, dwa_ref, dba_ref, dwb_ref, dbb_ref, dg_ref, gw_ref,
             apad, cpad, dapad, dcpad, dz, wacc, dwa_acc, dwb_acc, xs):
        i = pl.program_id(0)

        @pl.when(_first(i))
        def _():
            wacc[...] = jnp.zeros_like(wacc)
            dwa_acc[...] = jnp.zeros_like(dwa_acc)
            dwb_acc[...] = jnp.zeros_like(dwb_acc)
            dba_ref[...] = jnp.zeros_like(dba_ref)
            dbb_ref[...] = jnp.zeros_like(dbb_ref)
            dg_ref[...] = jnp.zeros_like(dg_ref)

        zm = z_ref[...].astype(F32)
        zp = zp_ref[...].astype(F32)
        zn = zn_ref[...].astype(F32)
        _fill_pad(apad, _glu_a(zp), _glu_a(zm), _glu_a(zn), i, tm)
        _fill_pad(cpad, _gate_c(zp), _gate_c(zm), _gate_c(zn), i, tm)
        da2 = da_ref[...].astype(F32)
        dc2 = dc_ref[...].astype(F32)
        _fill_pad(dapad, dap_ref[...].astype(F32), da2, dan_ref[...].astype(F32), i, tm)
        _fill_pad(dcpad, dcp_ref[...].astype(F32), dc2, dcn_ref[...].astype(F32), i, tm)
        dba_ref[...] += _colsum(da2)
        dbb_ref[...] += _colsum(dc2)

        sg = _sigmoid(zm[:, DA:2 * DA])
        aval = zm[:, :DA]
        _shift8(xs, dapad, tm)
        for r in range(0, tm, RC):
            acc = jnp.zeros((RC, DA), F32)
            for k in range(KA):
                acc = acc + _tap(xs, r, 2 * HALO - 1 - k) * wa_ref[pl.ds(k, 1), :]
            sgr = sg[r:r + RC]
            dz[pl.ds(r, RC), 0:DA] = (acc * sgr).astype(BF)
            dz[pl.ds(r, RC), DA:2 * DA] = (acc * aval[r:r + RC] * sgr * (1.0 - sgr)).astype(BF)
            acc = jnp.zeros((RC, DA), F32)
            for k in range(KB):
                acc = acc + dcpad[pl.ds(r + HALO + 1 - k, RC), :] * wb_ref[pl.ds(k, 1), :]
            dz[pl.ds(r, RC), 1024:1536] = (acc * zm[r:r + RC, 2048:]).astype(BF)
            dz[pl.ds(r, RC), 2048:2560] = (acc * zm[r:r + RC, 1024:1536]).astype(BF)
        dz[:, 1536:2048] = dgb_ref[...]

        _shift8(xs, apad, tm)
        for k in range(KA):
            acc = jnp.zeros((8, DA), F32)
            for r in range(0, tm, RC):
                p = _tap(xs, r, k + 1) * dapad[pl.ds(r + HALO, RC), :]
                acc = acc + p.reshape(RC // 8, 8, DA).sum(axis=0)
            dwa_acc[k] += acc
        for k in range(KB):
            acc = jnp.zeros((8, DA), F32)
            for r in range(0, tm, RC):
                p = cpad[pl.ds(r + HALO - 1 + k, RC), :] * dcpad[pl.ds(r + HALO, RC), :]
                acc = acc + p.reshape(RC // 8, 8, DA).sum(axis=0)
            dwb_acc[k] += acc

        gg = g_ref[...]
        hhat, rr = _rms(h_ref[...])
        n = (hhat * gg).astype(BF)
        dzb = dz[...]
        wacc[...] += _dot_tn(n, dzb)
        dn = _dot_nt(dzb, w_ref[...])
        dg_ref[...] += _colsum(dn * hhat)
        dh0_ref[...] = dh_ref[...] + _rms_bwd(dn, hhat, rr, gg)

        @pl.when(_last(i))
        def _():
            for k in range(NSH):
                gw_ref[k] = wacc[:, 640 * k:640 * (k + 1)].astype(BF)
            dwa_ref[...] = jnp.sum(dwa_acc[...], axis=1)
            dwb_ref[...] = jnp.sum(dwb_acc[...], axis=1)

    zp_spec, zn_spec = _halo_specs(tm, DZ, s)
    hp_spec, hn_spec = _halo_specs(tm, DA, s)
    small = lambda rows: _const((rows, DA), (0, 0))
    return _call(
        body, name="ev_in_bwd", grid=(s // tm,),
        in_specs=[_rows(tm, D), _rows(tm, D), _rows(tm, DZ), zp_spec, zn_spec,
                  _rows(tm, DA), hp_spec, hn_spec, _rows(tm, DA), hp_spec, hn_spec, _rows(tm, DA),
                  _const((1, D), (0, 0)), small(32), small(8), _const((D, DZ), (0, 0))],
        out_specs=[_rows(tm, D), small(32), small(1), small(8), small(1), _const((1, D), (0, 0)),
                   _const((NSH, D, 640), (0, 0, 0))],
        out_shape=[_sds((s, D), F32), _sds((32, DA), F32), _sds((1, DA), F32), _sds((8, DA), F32),
                   _sds((1, DA), F32), _sds((1, D), F32), _sds((NSH, D, 640), BF)],
        scratch_shapes=[pltpu.VMEM((tm + 2 * HALO, DA), F32)] * 4 + [
            pltpu.VMEM((tm, DZ), BF), pltpu.VMEM((D, DZ), F32),
            pltpu.VMEM((32, 8, DA), F32), pltpu.VMEM((8, 8, DA), F32),
            pltpu.VMEM((8, tm + 2 * HALO - 8, DA), F32)],
        args=(h, dh, z, z, z, da2, da2, da2, dc2, dc2, dc2, dgb, g, wa, wb, wev))


def _xa_spec(item):
    return _const((NSH, 256, D), (0, item, 0))


def _kv_proj(mem, g, xa, layer):
    def body(mem_ref, g_ref, wk_ref, wv_ref, k_ref, v_ref):
        mhat, _ = _rms(mem_ref[...])
        mn = (mhat * g_ref[...]).astype(BF)
        k_ref[...] = _dot(mn, wk_ref[...].reshape(D, D)).astype(BF)
        v_ref[...] = _dot(mn, wv_ref[...].reshape(D, D)).astype(BF)

    full = _const((NMEM, D), (0, 0))
    return _call(
        body, name=f"kv_proj{layer}", grid=(1,),
        in_specs=[full, _const((1, D), (0, 0)), _xa_spec(XK), _xa_spec(XV)],
        out_specs=[full, full], out_shape=[_sds((NMEM, D), BF), _sds((NMEM, D), BF)],
        args=(mem, g, xa, xa))


def _softmax_rows(sc):
    m = jnp.max(sc, axis=-1, keepdims=True)
    e = jnp.exp(sc - m)
    return e / jnp.sum(e, axis=-1, keepdims=True)


def _xattn_fwd(h, g, k, v, xa, layer, tm, carry=None):
    s = h.shape[0]

    def body(h_ref, g_ref, k_ref, v_ref, wq_ref, wo_ref, ho_ref, q_ref, o_scr):
        hh = h_ref[...]
        hhat, _ = _rms(hh)
        n = (hhat * g_ref[...]).astype(BF)
        q = _dot(n, wq_ref[...].reshape(D, D)).astype(BF)
        q_ref[...] = q
        for hd in range(HEADS):
            cs = slice(HD * hd, HD * (hd + 1))
            p = _softmax_rows(_dot_nt(q[:, cs], k_ref[:, cs]) * ATT_SCALE)
            o_scr[:, cs] = _dot(p.astype(BF), v_ref[:, cs]).astype(BF)
        ho_ref[...] = hh + _dot(o_scr[...], wo_ref[...].reshape(D, D))

    kvs = _const((NMEM, D), (0, 0))
    return _call(
        body, name=f"xattn_fwd{layer}", grid=(s // tm,),
        in_specs=[_rows(tm, D), _const((1, D), (0, 0)), kvs, kvs, _xa_spec(XQ), _xa_spec(XO)],
        out_specs=[_rows(tm, D), _rows(tm, D)],
        out_shape=[_sds((s, D), F32), _sds((s, D), BF)],
        scratch_shapes=[pltpu.VMEM((tm, D), BF)],
        args=(h, g, k, v, xa, xa), carry=carry)


def _xattn_bwd(h, dh, q, g, k, v, xa, layer, tm, carry=None):
    s = h.shape[0]

    def body(h_ref, dh_ref, q_ref, g_ref, k_ref, v_ref, wq_ref, wo_ref,
             dhi_ref, dk_ref, dv_ref, dg_ref, gqo_ref, o_scr, dq_scr, qacc, oacc):
        i = pl.program_id(0)

        @pl.when(_first(i))
        def _():
            qacc[...] = jnp.zeros_like(qacc)
            oacc[...] = jnp.zeros_like(oacc)
            dk_ref[...] = jnp.zeros_like(dk_ref)
            dv_ref[...] = jnp.zeros_like(dv_ref)
            dg_ref[...] = jnp.zeros_like(dg_ref)

        gg = g_ref[...]
        dho = dh_ref[...]
        dhb = dho.astype(BF)
        q = q_ref[...]
        do = _dot_nt(dhb, wo_ref[...].reshape(D, D)).astype(BF)
        for hd in range(HEADS):
            cs = slice(HD * hd, HD * (hd + 1))
            kh = k_ref[:, cs]
            vh = v_ref[:, cs]
            p = _softmax_rows(_dot_nt(q[:, cs], kh) * ATT_SCALE)
            pb = p.astype(BF)
            o_scr[:, cs] = _dot(pb, vh).astype(BF)
            doh = do[:, cs]
            dp = _dot_nt(doh, vh)
            dv_ref[:, cs] += _dot_tn(pb, doh)
            ds = (p * (dp - jnp.sum(dp * p, axis=-1, keepdims=True)) * ATT_SCALE).astype(BF)
            dq_scr[:, cs] = _dot(ds, kh).astype(BF)
            dk_ref[:, cs] += _dot_tn(ds, q[:, cs])
        oacc[...] += _dot_tn(o_scr[...], dhb)
        hhat, rr = _rms(h_ref[...])
        n = (hhat * gg).astype(BF)
        dq = dq_scr[...]
        qacc[...] += _dot_tn(n, dq)
        dn = _dot_nt(dq, wq_ref[...].reshape(D, D))
        dg_ref[...] += _colsum(dn * hhat)
        dhi_ref[...] = dho + _rms_bwd(dn, hhat, rr, gg)

        @pl.when(_last(i))
        def _():
            gqo_ref[:, 0:256, :] = qacc[...].astype(BF).reshape(NSH, 256, D)
            gqo_ref[:, 256:512, :] = oacc[...].astype(BF).reshape(NSH, 256, D)

    kvs = _const((NMEM, D), (0, 0))
    return _call(
        body, name=f"xattn_bwd{layer}", grid=(s // tm,),
        in_specs=[_rows(tm, D), _rows(tm, D), _rows(tm, D), _const((1, D), (0, 0)), kvs, kvs,
                  _xa_spec(XQ), _xa_spec(XO)],
        out_specs=[_rows(tm, D), kvs, kvs, _const((1, D), (0, 0)), _const((NSH, 512, D), (0, 0, 0))],
        out_shape=[_sds((s, D), F32), _sds((NMEM, D), F32), _sds((NMEM, D), F32), _sds((1, D), F32),
                   _sds(xa.shape, BF)],
        scratch_shapes=[pltpu.VMEM((tm, D), BF), pltpu.VMEM((tm, D), BF), pltpu.VMEM((D, D), F32),
                        pltpu.VMEM((D, D), F32)],
        args=(h, dh, q, g, k, v, xa, xa), carry=carry)


def _kv_proj_bwd(mem, g, dk, dv, xa, gxa, layer):
    def body(mem_ref, g_ref, dk_ref, dv_ref, wk_ref, wv_ref, gxa_in, dg_ref, gkv_ref):
        mhat, _ = _rms(mem_ref[...])
        mn = (mhat * g_ref[...]).astype(BF)
        dkb = dk_ref[...].astype(BF)
        dvb = dv_ref[...].astype(BF)
        gkv_ref[:, 0:256, :] = _dot_tn(mn, dkb).astype(BF).reshape(NSH, 256, D)
        gkv_ref[:, 256:512, :] = _dot_tn(mn, dvb).astype(BF).reshape(NSH, 256, D)
        dmn = _dot_nt(dkb, wk_ref[...].reshape(D, D)) + _dot_nt(dvb, wv_ref[...].reshape(D, D))
        dg_ref[...] = _colsum(dmn * mhat)

    full = _const((NMEM, D), (0, 0))
    return _call(
        body, name=f"kv_proj_bwd{layer}", grid=(1,),
        in_specs=[full, _const((1, D), (0, 0)), full, full, _xa_spec(XK), _xa_spec(XV), ANY],
        out_specs=[_const((1, D), (0, 0)), _const((NSH, 512, D), (0, 1, 0))],
        out_shape=[_sds((1, D), F32), _sds(gxa.shape, BF)],
        aliases={6: 1},
        args=(mem, g, dk, dv, xa, xa, gxa))


def _ffn_fwd(h, g, wg, wu, dn, layer, tm, carry=None):
    s = h.shape[0]

    def body(h_ref, g_ref, wg_ref, wu_ref, wd_ref, ho_ref, gt_ref, up_ref):
        hh = h_ref[...]
        hhat, _ = _rms(hh)
        n = (hhat * g_ref[...]).astype(BF)
        gt = _dot_nt(n, wg_ref[...].reshape(DFF, D))
        up = _dot_nt(n, wu_ref[...].reshape(DFF, D))
        gt_ref[...] = gt.astype(BF)
        up_ref[...] = up.astype(BF)
        act = (gt * _sigmoid(gt) * up).astype(BF)
        ho_ref[...] = hh + _dot(act, wd_ref[...].reshape(DFF, D))

    return _call(
        body, name=f"ffn_fwd{layer}", grid=(s // tm,),
        in_specs=[_rows(tm, D), _const((1, D), (0, 0)), _whole(wg), _whole(wu), _whole(dn)],
        out_specs=[_rows(tm, D), _rows(tm, DFF), _rows(tm, DFF)],
        out_shape=[_sds((s, D), F32), _sds((s, DFF), BF), _sds((s, DFF), BF)],
        args=(h, g, wg, wu, dn), carry=carry)


def _ffn_bwd_dx(h, dh, gt, up, g, wg, wu, dn, layer, tm, carry=None):
    s = h.shape[0]

    def body(h_ref, dh_ref, gt_ref, up_ref, g_ref, wg_ref, wu_ref, wd_ref,
             dhi_ref, dg_ref, dgt_ref, dup_ref, act_ref, n_ref):
        i = pl.program_id(0)

        @pl.when(_first(i))
        def _():
            dg_ref[...] = jnp.zeros_like(dg_ref)

        gg = g_ref[...]
        dho = dh_ref[...]
        dhb = dho.astype(BF)
        for half in range(2):
            cs = slice(FH * half, FH * (half + 1))
            gtv = gt_ref[:, cs].astype(F32)
            upv = up_ref[:, cs].astype(F32)
            sg = _sigmoid(gtv)
            silu = gtv * sg
            dact = _dot_nt(dhb, wd_ref[2 * half:2 * half + 2].reshape(FH, D))
            dgt_ref[:, cs] = (dact * upv * (sg * (1.0 + gtv * (1.0 - sg)))).astype(BF)
            dup_ref[:, cs] = (dact * silu).astype(BF)
            act_ref[:, cs] = (silu * upv).astype(BF)
        dn_ = (_dot(dgt_ref[...], wg_ref[...].reshape(DFF, D))
               + _dot(dup_ref[...], wu_ref[...].reshape(DFF, D)))
        hhat, rr = _rms(h_ref[...])
        n_ref[...] = (hhat * gg).astype(BF)
        dg_ref[...] += _colsum(dn_ * hhat)
        dhi_ref[...] = dho + _rms_bwd(dn_, hhat, rr, gg)

    wide = _rows(tm, DFF)
    return _call(
        body, name=f"ffn_bwd_dx{layer}", grid=(s // tm,),
        in_specs=[_rows(tm, D), _rows(tm, D), wide, wide, _const((1, D), (0, 0)),
                  _whole(wg), _whole(wu), _whole(dn)],
        out_specs=[_rows(tm, D), _const((1, D), (0, 0)), wide, wide, wide, _rows(tm, D)],
        out_shape=[_sds((s, D), F32), _sds((1, D), F32), _sds((s, DFF), BF), _sds((s, DFF), BF),
                   _sds((s, DFF), BF), _sds((s, D), BF)],
        args=(h, dh, gt, up, g, wg, wu, dn), carry=carry)


def _ffn_wgrad(a, b, name, tm):
    s = a.shape[0]
    tm = min(tm, s)

    def body(a_ref, b_ref, o_ref, acc):
        i = pl.program_id(0)

        @pl.when(_first(i))
        def _():
            acc[...] = jnp.zeros_like(acc)

        acc[...] += _dot_tn(a_ref[...], b_ref[...].astype(BF))

        @pl.when(_last(i))
        def _():
            o_ref[...] = acc[...].astype(BF).reshape(NSH, FSH, D)

    return _call(
        body, name=name, grid=(s // tm,),
        in_specs=[_rows(tm, DFF), _rows(tm, D)],
        out_specs=[_const((NSH, FSH, D), (0, 0, 0))],
        out_shape=[_sds((NSH, FSH, D), BF)],
        scratch_shapes=[pltpu.VMEM((DFF, D), F32)],
        args=(a, b))[0]


def _sgu_fwd(h, g, lng, lnb, ws, bsb, wod, odo, tm, carry=None):
    s = h.shape[0]

    def body(h_ref, g_ref, lng_ref, lnb_ref, ws_ref, bsb_ref, wi_ref, wo_ref, ho_ref, zp_ref, y_scr):
        hh = h_ref[...]
        hhat, _ = _rms(hh)
        n = (hhat * g_ref[...]).astype(BF)
        zpre = _dot(n, wi_ref[...])
        zp_ref[...] = zpre.astype(BF)
        z = _gelu(zpre)
        u = z[:, :D]
        y, _ = _ln(z[:, D:])
        vn = (y * lng_ref[...] + lnb_ref[...]).astype(BF)
        for c in range(tm // CHUNK):
            rs = slice(CHUNK * c, CHUNK * (c + 1))
            for gi in range(GROUPS):
                cs = slice(CHUNK * gi, CHUNK * (gi + 1))
                sv = _dot(ws_ref[gi], vn[rs, cs]) + bsb_ref[gi]
                y_scr[rs, cs] = (u[rs, cs] * sv).astype(BF)
        ho_ref[...] = hh + _dot(y_scr[...], wo_ref[...].reshape(D, D))

    vec = _const((1, D), (0, 0))
    sq = _const((GROUPS, CHUNK, CHUNK), (0, 0, 0))
    return _call(
        body, name="sgu_fwd", grid=(s // tm,),
        in_specs=[_rows(tm, D), vec, vec, vec, sq, sq, _const((D, 2 * D), (0, 0)), _whole(odo)],
        out_specs=[_rows(tm, D), _rows(tm, 2 * D)],
        out_shape=[_sds((s, D), F32), _sds((s, 2 * D), BF)],
        scratch_shapes=[pltpu.VMEM((tm, D), BF)],
        args=(h, g, lng, lnb, ws, bsb, wod, odo), carry=carry)


def _sgu_bwd(h, dh, zpre, g, lng, lnb, ws, wst, bsb, wod, odo, tm):
    s = h.shape[0]

    def body(h_ref, dh_ref, zp_ref, g_ref, lng_ref, lnb_ref, ws_ref, wst_ref, bsb_ref, wi_ref, wo_ref,
             dhi_ref, dws_ref, dbs_ref, dlng_ref, dlnb_ref, dg_ref, gwi_ref, gwo_ref,
             y_scr, dz_scr, dvn_scr, iacc, oacc, bacc):
        i = pl.program_id(0)

        @pl.when(_first(i))
        def _():
            iacc[...] = jnp.zeros_like(iacc)
            oacc[...] = jnp.zeros_like(oacc)
            bacc[...] = jnp.zeros_like(bacc)
            dws_ref[...] = jnp.zeros_like(dws_ref)
            dlng_ref[...] = jnp.zeros_like(dlng_ref)
            dlnb_ref[...] = jnp.zeros_like(dlnb_ref)
            dg_ref[...] = jnp.zeros_like(dg_ref)

        gg = g_ref[...]
        dho = dh_ref[...]
        dhb = dho.astype(BF)
        zpre_v = zp_ref[...].astype(F32)
        z = _gelu(zpre_v)
        u = z[:, :D]
        yl, rstd = _ln(z[:, D:])
        vn = (yl * lng_ref[...] + lnb_ref[...]).astype(BF)
        dy = _dot_nt(dhb, wo_ref[...].reshape(D, D))
        for c in range(tm // CHUNK):
            rs = slice(CHUNK * c, CHUNK * (c + 1))
            for gi in range(GROUPS):
                cs = slice(CHUNK * gi, CHUNK * (gi + 1))
                vb = vn[rs, cs]
                sv = _dot(ws_ref[gi], vb) + bsb_ref[gi]
                ub = u[rs, cs]
                dyb = dy[rs, cs]
                y_scr[rs, cs] = (ub * sv).astype(BF)
                dz_scr[rs, cs] = dyb * sv
                dsv = dyb * ub
                bacc[gi] += dsv
                dsvb = dsv.astype(BF)
                dws_ref[gi] += _dot_nt(dsvb, vb)
                dvn_scr[rs, cs] = _dot(wst_ref[gi], dsvb)
        oacc[...] += _dot_tn(y_scr[...], dhb)
        dvn = dvn_scr[...]
        dlng_ref[...] += _colsum(dvn * yl)
        dlnb_ref[...] += _colsum(dvn)
        dz_scr[:, D:] = _ln_bwd(dvn * lng_ref[...], yl, rstd)
        dzb = (dz_scr[...] * _gelu_grad(zpre_v)).astype(BF)
        hhat, rr = _rms(h_ref[...])
        n = (hhat * gg).astype(BF)
        iacc[...] += _dot_tn(n, dzb)
        dn = _dot_nt(dzb, wi_ref[...])
        dg_ref[...] += _colsum(dn * hhat)
        dhi_ref[...] = dho + _rms_bwd(dn, hhat, rr, gg)

        @pl.when(_last(i))
        def _():
            for k in range(NSH):
                gwi_ref[k] = iacc[:, 512 * k:512 * (k + 1)].astype(BF)
            gwo_ref[...] = oacc[...].astype(BF).reshape(NSH, 256, D)
            dbs_ref[...] = jnp.sum(bacc[...], axis=-1)

    vec = _const((1, D), (0, 0))
    sq = _const((GROUPS, CHUNK, CHUNK), (0, 0, 0))
    return _call(
        body, name="sgu_bwd", grid=(s // tm,),
        in_specs=[_rows(tm, D), _rows(tm, D), _rows(tm, 2 * D), vec, vec, vec, sq, sq, sq,
                  _const((D, 2 * D), (0, 0)), _whole(odo)],
        out_specs=[_rows(tm, D), sq, _const((GROUPS, CHUNK), (0, 0)), vec, vec, vec,
                   _const((NSH, D, 512), (0, 0, 0)), _whole(odo)],
        out_shape=[_sds((s, D), F32), _sds((GROUPS, CHUNK, CHUNK), F32), _sds((GROUPS, CHUNK), F32),
                   _sds((1, D), F32), _sds((1, D), F32), _sds((1, D), F32), _sds((NSH, D, 512), BF),
                   _sds(odo.shape, BF)],
        scratch_shapes=[pltpu.VMEM((tm, D), BF), pltpu.VMEM((tm, 2 * D), F32),
                        pltpu.VMEM((tm, D), F32), pltpu.VMEM((D, 2 * D), F32), pltpu.VMEM((D, D), F32),
                        pltpu.VMEM((GROUPS, CHUNK, CHUNK), F32)],
        args=(h, dh, zpre, g, lng, lnb, ws, wst, bsb, wod, odo))


def _final_loss(h, tgt, g, tm):
    s = h.shape[0]

    def body(h_ref, t_ref, g_ref, dh_ref, loss_ref, dg_ref):
        i = pl.program_id(0)

        @pl.when(_first(i))
        def _():
            loss_ref[...] = jnp.zeros_like(loss_ref)
            dg_ref[...] = jnp.zeros_like(dg_ref)

        gg = g_ref[...]
        hhat, rr = _rms(h_ref[...])
        diff = hhat * gg - t_ref[...]
        loss_ref[...] += jnp.sum(diff * diff) * (0.5 / D)
        dout = diff * (1.0 / D)
        dg_ref[...] += _colsum(dout * hhat)
        dh_ref[...] = _rms_bwd(dout, hhat, rr, gg)

    return _call(
        body, name="final_loss", grid=(s // tm,),
        in_specs=[_rows(tm, D), _rows(tm, D), _const((1, D), (0, 0))],
        out_specs=[_rows(tm, D), _const((1, 128), (0, 0)), _const((1, D), (0, 0))],
        out_shape=[_sds((s, D), F32), _sds((1, 128), F32), _sds((1, D), F32)],
        args=(h, tgt, g))


def _own_plane(shape):
    return pl.BlockSpec((1,) + shape, lambda i, chip_ref: (chip_ref[0], 0, 0), pipeline_mode=pl.Buffered(1))


def _pack_rows(chip_idx, w_down, w_gate_t, w_up_t, xq, xo, xk, xv, ev_out, od_out):
    def body(chip_ref, wd_ref, wg_ref, wu_ref, xq_ref, xo_ref, xk_ref, xv_ref, ev_ref, od_ref,
             evo_ref, odo_ref, dn0_ref, dn1_ref, wg0_ref, wg1_ref, wu0_ref, wu1_ref, xa0_ref, xa1_ref):
        evo_ref[0] = ev_ref[0].astype(BF)
        odo_ref[0] = od_ref[0].astype(BF)
        for src, outs in ((wd_ref, (dn0_ref, dn1_ref)), (wg_ref, (wg0_ref, wg1_ref)), (wu_ref, (wu0_ref, wu1_ref))):
            for layer, o_ref in enumerate(outs):
                o_ref[0] = src[layer].astype(BF)
        for layer, xa_ref in enumerate((xa0_ref, xa1_ref)):
            for item, src in ((XQ, xq_ref), (XO, xo_ref), (XK, xk_ref), (XV, xv_ref)):
                xa_ref[0, 256 * item:256 * (item + 1), :] = src[layer].astype(BF)

    ins = (w_down, w_gate_t, w_up_t, xq, xo, xk, xv, ev_out, od_out)
    shapes = [(256, D), (256, D)] + [(FSH, D)] * 6 + [(4 * 256, D), (4 * 256, D)]
    grid_spec = pltpu.PrefetchScalarGridSpec(
        num_scalar_prefetch=1, grid=(1,), in_specs=[_whole(a) for a in ins],
        out_specs=[_own_plane(sh) for sh in shapes])
    return pl.pallas_call(
        body, name="pack_rows", grid_spec=grid_spec, out_shape=[_sds((NSH,) + sh, BF) for sh in shapes],
        compiler_params=_params(),
    )(chip_idx, *ins)


def _pack_cols(chip_idx, od_in, ev_in, conv_a, conv_b, ln_g, ln_b):
    def body(chip_ref, od_ref, ev_ref, ca_ref, cb_ref, lg_ref, lb_ref, cev_ref, cod_ref, sp_ref):
        cev_ref[0] = ev_ref[0].astype(BF)
        cod_ref[0] = od_ref[0].astype(BF)
        sp_ref[...] = jnp.zeros_like(sp_ref)
        sp_ref[0, 0:KA, 0:128] = ca_ref[0]
        sp_ref[0, 32:32 + KB, 0:128] = cb_ref[0]
        sp_ref[0, 40:41, :] = lg_ref[...]
        sp_ref[0, 41:42, :] = lb_ref[...]

    ins = (od_in, ev_in, conv_a, conv_b, ln_g, ln_b)
    shapes = [(D, 640), (D, 512)]
    grid_spec = pltpu.PrefetchScalarGridSpec(
        num_scalar_prefetch=1, grid=(1,), in_specs=[_whole(a) for a in ins],
        out_specs=[_own_plane(sh) for sh in shapes] + [_own_plane((SP_ROWS, 256))])
    return pl.pallas_call(
        body, name="pack_cols", grid_spec=grid_spec,
        out_shape=[_sds((NSH,) + sh, BF) for sh in shapes] + [_sds((NSH, SP_ROWS, 256), F32)],
        compiler_params=_params(),
    )(chip_idx, *ins)


def _assemble(piece, parts, name):
    tr = 256
    cols = piece.shape[2]

    def body(p_ref, *outs):
        for k in range(NSH):
            for (off, width), o_ref in zip(parts, outs):
                o_ref[:, width * k:width * (k + 1)] = p_ref[k, :, off:off + width]

    return _call(
        body, name=name, grid=(D // tr,),
        in_specs=[pl.BlockSpec((NSH, tr, cols), lambda i: (0, i, 0))],
        out_specs=[pl.BlockSpec((tr, NSH * width), lambda i: (i, 0)) for _, width in parts],
        out_shape=[_sds((D, NSH * width), BF) for _, width in parts],
        args=(piece,))


def _blocks_of(rows, cols, itemsize):
    rb = rows
    while rb * cols * itemsize > (2 << 20) and rb % 2 == 0 and (rb // 2) % 16 == 0:
        rb //= 2
    return rb


def _add_half(full, recv, c_idx, name):
    p, r, cols = full.shape
    rb = _blocks_of(r // 2, cols, 4)
    nb = (r // 2) // rb

    def body(c_ref, a_ref, b_ref, o_ref):
        o_ref[...] = (a_ref[...].astype(F32) + b_ref[...].astype(F32)).astype(o_ref.dtype)

    grid_spec = pltpu.PrefetchScalarGridSpec(
        num_scalar_prefetch=1, grid=(p, nb),
        in_specs=[pl.BlockSpec((1, rb, cols), lambda j, i, c_ref: (j, c_ref[0] * nb + i, 0)),
                  pl.BlockSpec((1, rb, cols), lambda j, i, c_ref: (j, i, 0))],
        out_specs=pl.BlockSpec((1, rb, cols), lambda j, i, c_ref: (j, i, 0)))
    return pl.pallas_call(
        body, name=name, grid_spec=grid_spec, out_shape=_sds(recv.shape, full.dtype),
        compiler_params=_params(2),
    )(c_idx, full, recv)


def _sum_chips(own, got, plane_c, name):
    _, r, cols = own.shape
    rb = _blocks_of(r, cols, 4)
    nb = r // rb

    def body(s_ref, a_ref, b_ref, o_ref):
        f = lambda v: v.astype(F32)
        o_ref[...] = (f(a_ref[0]) + f(b_ref[1])) + (f(b_ref[0]) + f(b_ref[2]))

    grid_spec = pltpu.PrefetchScalarGridSpec(
        num_scalar_prefetch=1, grid=(nb,),
        in_specs=[pl.BlockSpec((1, rb, cols), lambda i, s_ref: (s_ref[0], i, 0)),
                  pl.BlockSpec((3, rb, cols), lambda i, s_ref: (0, i, 0))],
        out_specs=pl.BlockSpec((rb, cols), lambda i, s_ref: (s_ref[1] * nb + i, 0)))
    return pl.pallas_call(
        body, name=name, grid_spec=grid_spec, out_shape=_sds((2 * r, cols), F32),
        compiler_params=_params(1),
    )(plane_c, own, got)


def _adamw_math(w, g, m, v):
    m = ADAM_B1 * m + (1.0 - ADAM_B1) * g
    v = ADAM_B2 * v + (1.0 - ADAM_B2) * (g * g)
    m_hat = m / (1.0 - ADAM_B1 ** ADAM_STEP)
    v_hat = v / (1.0 - ADAM_B2 ** ADAM_STEP)
    delta = -ADAM_LR * (m_hat / (jnp.sqrt(v_hat) + ADAM_EPS) + ADAM_WD * w)
    return delta, m, v


def _adamw_big(w, m, v, gsrcs, gblock, gidx, name, rb):
    nl, r, cols = w.shape
    assert len(gsrcs) == nl

    def body(w_ref, m_ref, v_ref, *rest):
        g_refs, (go_ref, d_ref, mo_ref, vo_ref) = rest[:nl], rest[nl:]
        g = g_refs[0][:, 0:cols]
        if nl == 2:
            g = jnp.where(pl.program_id(0) == 0, g, g_refs[1][:, 0:cols])
        delta, mn, vn = _adamw_math(w_ref[0], g, m_ref[0], v_ref[0])
        go_ref[0] = g
        d_ref[0] = delta
        mo_ref[0] = mn
        vo_ref[0] = vn

    wspec = pl.BlockSpec((1, rb, cols), lambda l, i: (l, i, 0))
    gspec = pl.BlockSpec(gblock, lambda l, i: gidx(i))
    return pl.pallas_call(
        body, name=name, grid=(nl, r // rb),
        in_specs=[wspec, wspec, wspec] + [gspec] * nl,
        out_specs=[wspec] * 4, out_shape=[_sds(w.shape, F32)] * 4,
        compiler_params=_params(2),
    )(w, m, v, *gsrcs)


_SMALL = (
    ("g_mix", (2, D), "a", (0, 2), 0, False),
    ("g_xattn", (2, D), "a", (2, 4), 0, False),
    ("g_mem", (2, D), "a", (4, 6), 0, False),
    ("g_ffn", (2, D), "a", (6, 8), 0, False),
    ("g_final", (1, D), "a", (8, 9), 0, False),
    ("ev_a_conv_w", (KA, 128), "a", (16, 16 + KA), 0, True),
    ("ev_a_conv_b", (1, DA), "a", (11, 12), 0, False),
    ("ev_a_ln_g", (1, DA), "a", (11, 12), DA, False),
    ("ev_a_ln_b", (1, DA), "a", (12, 13), 0, False),
    ("ev_b_conv_w", (KB, 128), "a", (48, 48 + KB), 0, True),
    ("ev_b_conv_b", (1, DA), "a", (12, 13), DA, False),
    ("od_c_ln_g", (1, 256), "a", (9, 10), 0, True),
    ("od_c_ln_b", (1, 256), "a", (10, 11), 0, True),
    ("od_w_s", (GROUPS * CHUNK, CHUNK), "b", (0, GROUPS * CHUNK), 0, False),
    ("od_b_s", (GROUPS, CHUNK), "a", (56, 64), 0, False),
)


def _adamw_small(ga, gb, wmv):
    ns = len(_SMALL)

    def body(*refs):
        ga_ref, gb_ref = refs[0], refs[1]
        ins = refs[2:2 + 3 * ns]
        outs = refs[2 + 3 * ns:]
        x, y, _ = _pos()
        chip = 2 * x + y
        for j, (_, shape, pack, (r0, r1), col, sharded) in enumerate(_SMALL):
            src = ga_ref if pack == "a" else gb_ref
            width = shape[1]
            if sharded:
                g = jnp.zeros(shape, F32)
                for k in range(NSH):
                    blk = src[r0:r1, col + width * k:col + width * (k + 1)]
                    g = g + jnp.where(chip == k, blk, 0.0)
            else:
                g = src[r0:r1, col:col + width]
            w_ref, m_ref, v_ref = ins[3 * j:3 * j + 3]
            delta, mn, vn = _adamw_math(w_ref[...], g, m_ref[...], v_ref[...])
            go_ref, d_ref, mo_ref, vo_ref = outs[4 * j:4 * j + 4]
            go_ref[...] = g
            d_ref[...] = delta
            mo_ref[...] = mn
            vo_ref[...] = vn

    vm = pl.BlockSpec(memory_space=pltpu.VMEM)
    flat = [a for trio in wmv for a in trio]
    out_shape = []
    for (_, shape, *_rest) in _SMALL:
        out_shape += [_sds(shape, F32)] * 4
    outs = pl.pallas_call(
        body, name="adamw_small",
        in_specs=[vm] * (2 + 3 * ns), out_specs=[vm] * (4 * ns), out_shape=out_shape,
    )(ga, gb, *flat)
    return [tuple(outs[4 * j:4 * j + 4]) for j in range(ns)]


def _small_params(gs, W):
    unshard = lambda a: jnp.transpose(a, (1, 0, 2)).reshape(a.shape[1], NSH * a.shape[2])
    od_w_s, od_b_s = W["od_w_s"], W["od_b_s"]
    return dict(
        g_mix=W["g_mix"], g_xattn=W["g_xattn"], g_mem=W["g_mem"], g_ffn=W["g_ffn"],
        g_final=W["g_final"].reshape(1, D),
        wa=unshard(gs[:, 0:32, 0:128]), ba=W["ev_a_conv_b"], lng_a=W["ev_a_ln_g"], lnb_a=W["ev_a_ln_b"],
        wb=unshard(gs[:, 32:40, 0:128]), bb=W["ev_b_conv_b"],
        lng_c=unshard(gs[:, 40:41, :]), lnb_c=unshard(gs[:, 41:42, :]),
        ws=od_w_s[0].astype(BF), wst=jnp.swapaxes(od_w_s[0], 1, 2).astype(BF),
        bsb=jnp.broadcast_to(od_b_s[0][:, :, None], (GROUPS, CHUNK, CHUNK)),
    )


def _small_grad_packs(g):
    z = lambda r, c: jnp.zeros((r, c), F32)
    wide = lambda a: jnp.concatenate([a, z(a.shape[0], D - a.shape[1])], axis=1)
    rows = [g["g_mix"], g["g_xattn"], g["g_mem"], g["g_ffn"], g["g_final"], g["od_c_ln_g"], g["od_c_ln_b"],
            jnp.concatenate([g["ev_a_conv_b"], g["ev_a_ln_g"]], axis=1),
            jnp.concatenate([g["ev_a_ln_b"], g["ev_b_conv_b"]], axis=1),
            z(3, D), wide(g["ev_a_conv_w"]), wide(g["ev_b_conv_w"]), wide(g["od_b_s"])]
    a = jnp.concatenate(rows, axis=0)
    assert a.shape == (SA_ROWS, D), a.shape
    return a, g["od_w_s"].reshape(GROUPS * CHUNK, CHUNK)


def _step(x, mem, tgt, W, P, c_idx, tm_f, tm_b, tm_w):
    row = lambda a, i: a[i:i + 1]

    def pre_sum(names, arrays, tag):
        recv = _sibling_swap(arrays, f"rs_swap_{tag}")
        return [_add_half(a, r, c_idx, f"rs_add_{n}") for n, a, r in zip(names, arrays, recv)]

    cev, gs = _gather_now([P["cev"], P["gs"]])
    sm = _small_params(gs, W)
    (wev,) = _assemble(cev, [(0, 640)], "assemble_ev")
    z, evo = _ev_in_fwd(x, row(sm["g_mix"], 0), wev, tm_f, carry=("gather", [P["evo"]]))
    h1, a2, c2, cat, xa0, wg0, wu0 = _ev_mix_fwd(
        x, z, sm["wa"], sm["ba"], sm["lng_a"], sm["lnb_a"], sm["wb"], sm["bb"], evo, tm_f,
        carry=("gather", [P["xa0"], P["wg0"], P["wu0"]]))
    k0, v0 = _kv_proj(mem, row(sm["g_mem"], 0), xa0, 0)
    h2, q0, dn0, cod = _xattn_fwd(h1, row(sm["g_xattn"], 0), k0, v0, xa0, 0, tm_f,
                                  carry=("gather", [P["dn0"], P["cod"]]))
    (wod,) = _assemble(cod, [(0, 512)], "assemble_od")
    h3, gt0, up0, odo, xa1, wg1, wu1 = _ffn_fwd(h2, row(sm["g_ffn"], 0), wg0, wu0, dn0, 0, tm_b,
                                                carry=("gather", [P["odo"], P["xa1"], P["wg1"], P["wu1"]]))
    h4, zpre, dn1 = _sgu_fwd(h3, row(sm["g_mix"], 1), sm["lng_c"], sm["lnb_c"], sm["ws"], sm["bsb"], wod, odo,
                             tm_f, carry=("gather", [P["dn1"]]))
    k1, v1 = _kv_proj(mem, row(sm["g_mem"], 1), xa1, 1)
    h5, q1 = _xattn_fwd(h4, row(sm["g_xattn"], 1), k1, v1, xa1, 1, tm_f)
    h6, gt1, up1 = _ffn_fwd(h5, row(sm["g_ffn"], 1), wg1, wu1, dn1, 1, tm_b)
    dh6, loss, dg_final = _final_loss(h6, tgt, sm["g_final"], tm_f)

    g = {"g_final": dg_final}
    parts, gots = {}, {}
    def ffn_wgrads(layer, dgt, dup, act, nb, dh_out):
        names = (f"wg{layer}", f"wu{layer}", f"dn{layer}")
        grads = (_ffn_wgrad(dgt, nb, f"ffn_wgrad_gate{layer}", tm_w),
                 _ffn_wgrad(dup, nb, f"ffn_wgrad_up{layer}", tm_w),
                 _ffn_wgrad(act, dh_out, f"ffn_wgrad_down{layer}", tm_w))
        for n, p in zip(names, pre_sum(names, grads, f"ffn{layer}")):
            parts[n] = p
        return [parts[n] for n in names], names

    dh5, dg_ffn1, dgt, dup, act, nb = _ffn_bwd_dx(h5, dh6, gt1, up1, row(sm["g_ffn"], 1), wg1, wu1, dn1, 1, tm_b)
    grp, names = ffn_wgrads(1, dgt, dup, act, nb, dh6)
    dh4, dk, dv, dg_xa1, g_xa1, *got = _xattn_bwd(
        h4, dh5, q1, row(sm["g_xattn"], 1), k1, v1, xa1, 1, tm_b, carry=("xchg", grp, []))
    gots.update(zip(names, got))
    dg_mem1, g_xa1 = _kv_proj_bwd(mem, row(sm["g_mem"], 1), dk, dv, xa1, g_xa1, 1)
    dh3, g["od_w_s"], g["od_b_s"], g["od_c_ln_g"], g["od_c_ln_b"], dg_mix1, g_cod, g_odo = _sgu_bwd(
        h3, dh4, zpre, row(sm["g_mix"], 1), sm["lng_c"], sm["lnb_c"], sm["ws"], sm["wst"], sm["bsb"], wod, odo, tm_b)
    parts["xa1"], parts["cod"], parts["odo"] = pre_sum(("xa1", "cod", "odo"), (g_xa1, g_cod, g_odo), "b")
    dh2, dg_ffn0, dgt, dup, act, nb, gots["xa1"], gots["cod"], gots["odo"] = _ffn_bwd_dx(
        h2, dh3, gt0, up0, row(sm["g_ffn"], 0), wg0, wu0, dn0, 0, tm_b,
        carry=("xchg", [parts["xa1"], parts["cod"], parts["odo"]], []))
    grp, names = ffn_wgrads(0, dgt, dup, act, nb, dh3)
    dh1, dk, dv, dg_xa0, g_xa0, *got = _xattn_bwd(
        h1, dh2, q0, row(sm["g_xattn"], 0), k0, v0, xa0, 0, tm_b, carry=("xchg", grp, []))
    gots.update(zip(names, got))
    dg_mem0, g_xa0 = _kv_proj_bwd(mem, row(sm["g_mem"], 0), dk, dv, xa0, g_xa0, 0)
    (parts["xa0"],) = pre_sum(("xa0",), (g_xa0,), "d")
    da2, dc2, dgb, g["ev_a_ln_g"], g["ev_a_ln_b"], g_evo, gots["xa0"] = _ev_mix_bwd(
        dh1, cat, a2, c2, z, sm["lng_a"], sm["lnb_a"], evo, tm_b, carry=("xchg", [parts["xa0"]], []))
    dx, g["ev_a_conv_w"], g["ev_a_conv_b"], g["ev_b_conv_w"], g["ev_b_conv_b"], dg_mix0, g_cev = _ev_in_bwd(
        x, dh1, z, da2, dc2, dgb, row(sm["g_mix"], 0), sm["wa"], sm["wb"], wev, tm_b)
    g["g_mix"] = jnp.concatenate([dg_mix0, dg_mix1], axis=0)
    g["g_xattn"] = jnp.concatenate([dg_xa0, dg_xa1], axis=0)
    g["g_mem"] = jnp.concatenate([dg_mem0, dg_mem1], axis=0)
    g["g_ffn"] = jnp.concatenate([dg_ffn0, dg_ffn1], axis=0)
    ga, gb = _small_grad_packs(g)
    parts["evo"], parts["cev"], parts["ga"], parts["gb"] = pre_sum(
        ("evo", "cev", "ga", "gb"), (g_evo, g_cev, ga[None], gb[None]), "e")
    gots["evo"], gots["cev"], gots["ga"], gots["gb"] = _xchg_now(
        [parts["evo"], parts["cev"]], [parts["ga"][0], parts["gb"][0]])
    return loss, dx, parts, gots


def kernel(x, mem, g_mix, g_xattn, g_mem, g_ffn, g_final, ev_w_in, ev_a_conv_w, ev_a_conv_b, ev_a_ln_g, ev_a_ln_b, ev_b_conv_w, ev_b_conv_b, ev_w_out, od_w_in, od_c_ln_g, od_c_ln_b, od_w_s, od_b_s, od_w_out, xa_w_q, xa_w_k, xa_w_v, xa_w_o, ffn_w_gate, ffn_w_up, ffn_w_down, loss_target, m_g_mix, m_g_xattn, m_g_mem, m_g_ffn, m_g_final, m_ev_w_in, m_ev_a_conv_w, m_ev_a_conv_b, m_ev_a_ln_g, m_ev_a_ln_b, m_ev_b_conv_w, m_ev_b_conv_b, m_ev_w_out, m_od_w_in, m_od_c_ln_g, m_od_c_ln_b, m_od_w_s, m_od_b_s, m_od_w_out, m_xa_w_q, m_xa_w_k, m_xa_w_v, m_xa_w_o, m_ffn_w_gate, m_ffn_w_up, m_ffn_w_down, v_g_mix, v_g_xattn, v_g_mem, v_g_ffn, v_g_final, v_ev_w_in, v_ev_a_conv_w, v_ev_a_conv_b, v_ev_a_ln_g, v_ev_a_ln_b, v_ev_b_conv_w, v_ev_b_conv_b, v_ev_w_out, v_od_w_in, v_od_c_ln_g, v_od_c_ln_b, v_od_w_s, v_od_b_s, v_od_w_out, v_xa_w_q, v_xa_w_k, v_xa_w_v, v_xa_w_o, v_ffn_w_gate, v_ffn_w_up, v_ffn_w_down):
    W = dict(g_mix=g_mix, g_xattn=g_xattn, g_mem=g_mem, g_ffn=g_ffn, g_final=g_final, ev_w_in=ev_w_in,
             ev_a_conv_w=ev_a_conv_w, ev_a_conv_b=ev_a_conv_b, ev_a_ln_g=ev_a_ln_g, ev_a_ln_b=ev_a_ln_b,
             ev_b_conv_w=ev_b_conv_w, ev_b_conv_b=ev_b_conv_b, ev_w_out=ev_w_out, od_w_in=od_w_in,
             od_c_ln_g=od_c_ln_g, od_c_ln_b=od_c_ln_b, od_w_s=od_w_s, od_b_s=od_b_s, od_w_out=od_w_out,
             xa_w_q=xa_w_q, xa_w_k=xa_w_k, xa_w_v=xa_w_v, xa_w_o=xa_w_o, ffn_w_gate=ffn_w_gate,
             ffn_w_up=ffn_w_up, ffn_w_down=ffn_w_down)
    M = dict(g_mix=m_g_mix, g_xattn=m_g_xattn, g_mem=m_g_mem, g_ffn=m_g_ffn, g_final=m_g_final, ev_w_in=m_ev_w_in,
             ev_a_conv_w=m_ev_a_conv_w, ev_a_conv_b=m_ev_a_conv_b, ev_a_ln_g=m_ev_a_ln_g, ev_a_ln_b=m_ev_a_ln_b,
             ev_b_conv_w=m_ev_b_conv_w, ev_b_conv_b=m_ev_b_conv_b, ev_w_out=m_ev_w_out, od_w_in=m_od_w_in,
             od_c_ln_g=m_od_c_ln_g, od_c_ln_b=m_od_c_ln_b, od_w_s=m_od_w_s, od_b_s=m_od_b_s, od_w_out=m_od_w_out,
             xa_w_q=m_xa_w_q, xa_w_k=m_xa_w_k, xa_w_v=m_xa_w_v, xa_w_o=m_xa_w_o, ffn_w_gate=m_ffn_w_gate,
             ffn_w_up=m_ffn_w_up, ffn_w_down=m_ffn_w_down)
    V = dict(g_mix=v_g_mix, g_xattn=v_g_xattn, g_mem=v_g_mem, g_ffn=v_g_ffn, g_final=v_g_final, ev_w_in=v_ev_w_in,
             ev_a_conv_w=v_ev_a_conv_w, ev_a_conv_b=v_ev_a_conv_b, ev_a_ln_g=v_ev_a_ln_g, ev_a_ln_b=v_ev_a_ln_b,
             ev_b_conv_w=v_ev_b_conv_w, ev_b_conv_b=v_ev_b_conv_b, ev_w_out=v_ev_w_out, od_w_in=v_od_w_in,
             od_c_ln_g=v_od_c_ln_g, od_c_ln_b=v_od_c_ln_b, od_w_s=v_od_w_s, od_b_s=v_od_b_s, od_w_out=v_od_w_out,
             xa_w_q=v_xa_w_q, xa_w_k=v_xa_w_k, xa_w_v=v_xa_w_v, xa_w_o=v_xa_w_o, ffn_w_gate=v_ffn_w_gate,
             ffn_w_up=v_ffn_w_up, ffn_w_down=v_ffn_w_down)
    order = list(W)
    cx, cy, cc = _pos()
    chip = (2 * cx + cy).astype(jnp.int32)
    cc = cc.astype(jnp.int32)
    chip_idx = jnp.reshape(chip, (1,))
    c_idx = jnp.reshape(cc, (1,))

    tr = lambda a: jnp.swapaxes(a, 1, 2)
    P = {}
    (P["evo"], P["odo"], P["dn0"], P["dn1"], P["wg0"], P["wg1"], P["wu0"], P["wu1"], P["xa0"],
     P["xa1"]) = _pack_rows(chip_idx, ffn_w_down, tr(ffn_w_gate), tr(ffn_w_up), xa_w_q, xa_w_o, xa_w_k, xa_w_v,
                            ev_w_out, od_w_out)
    P["cev"], P["cod"], P["gs"] = _pack_cols(chip_idx, od_w_in, ev_w_in, ev_a_conv_w, ev_b_conv_w, od_c_ln_g,
                                             od_c_ln_b)

    loss, dx, parts, gots = _step(x[0], mem[0], loss_target[0], W, P, c_idx, TM_FWD, TM_BWD, TM_WGRAD)
    loss = lax.psum(loss[0, 0], ("x", "y", "c"))

    chip_c = jnp.stack([chip, cc])
    zero_c = jnp.stack([jnp.zeros((), jnp.int32), cc])
    names = list(parts)
    halves = [_sum_chips(parts[n], gots[n], zero_c if n in ("ga", "gb") else chip_c, f"rs_sum_{n}") for n in names]
    red = dict(zip(names, _sibling_join(halves)))

    res = {}
    xa = [red["xa0"], red["xa1"]]
    big = {
        "ffn_w_down": ([red["dn0"], red["dn1"]], (FSH, D), lambda i: (0, 0), FSH),
        "ffn_w_gate": ([red["wg0"], red["wg1"]], (FSH, D), lambda i: (0, 0), FSH),
        "ffn_w_up": ([red["wu0"], red["wu1"]], (FSH, D), lambda i: (0, 0), FSH),
        "xa_w_q": (xa, (256, D), lambda i: (XQ, 0), 256),
        "xa_w_o": (xa, (256, D), lambda i: (XO, 0), 256),
        "xa_w_k": (xa, (256, D), lambda i: (XK, 0), 256),
        "xa_w_v": (xa, (256, D), lambda i: (XV, 0), 256),
        "ev_w_out": ([red["evo"]], (256, D), lambda i: (0, 0), 256),
        "od_w_out": ([red["odo"]], (256, D), lambda i: (0, 0), 256),
        "od_w_in": ([red["cod"]], (256, 512), lambda i: (i, 0), 256),
        "ev_w_in": ([red["cev"]], (256, 640), lambda i: (i, 0), 256),
    }
    for name, (srcs, gblock, gidx, rb) in big.items():
        if name in ("ffn_w_gate", "ffn_w_up"):
            outs = _adamw_big(tr(W[name]), tr(M[name]), tr(V[name]), srcs, gblock, gidx, f"adamw_{name}", rb)
            res[name] = tuple(tr(o) for o in outs)
        else:
            res[name] = _adamw_big(W[name], M[name], V[name], srcs, gblock, gidx, f"adamw_{name}", rb)
    shape2 = {name: shape for name, shape, *_ in _SMALL}
    wmv = [tuple(d[name].reshape(shape2[name]) for d in (W, M, V)) for name, *_ in _SMALL]
    for (name, *_), outs in zip(_SMALL, _adamw_small(red["ga"], red["gb"], wmv)):
        res[name] = tuple(o.reshape(W[name].shape) for o in outs)

    grad_x = dx[None]
    return (loss, grad_x, *[res[n][0] for n in order], *[res[n][1] for n in order],
            *[res[n][2] for n in order], *[res[n][3] for n in order])
```

```python
import jax
import jax.numpy as jnp
from jax import lax
from jax.experimental import pallas as pl
from jax.experimental.pallas import tpu as pltpu

BF = jnp.bfloat16
F32 = jnp.float32

D = 1024
DA = 512
DZ = 2560
DFF = 2816
NSH = 4
FSH = DFF // NSH
FF_CHUNKS = ((0, 768), (768, 1536), (1536, 2304), (2304, 2816))
NMEM = 256
HEADS = 4
HD = D // HEADS
CHUNK = 128
GROUPS = 8
KA = 31
KB = 3
HALO = 16
RC = 32
RMS_EPS = 1e-6
LN_EPS = 1e-5
ATT_SCALE = HD ** -0.5
TM_FWD = 512
TM_BWD = 256
TM_WGRAD = 1024

ADAM_LR = 0.001
ADAM_B1 = 0.9
ADAM_B2 = 0.999
ADAM_EPS = 1e-08
ADAM_WD = 0.01
ADAM_STEP = 10

XQ, XO, XK, XV = 0, 1, 2, 3
SP_ROWS = 48
SA_ROWS = 64

MESH = pl.DeviceIdType.MESH
ANY = pl.BlockSpec(memory_space=pl.ANY)


def _dot(a, b):
    return jnp.dot(a, b, preferred_element_type=F32)


def _dot_nt(a, b):
    return lax.dot_general(a, b, (((1,), (1,)), ((), ())), preferred_element_type=F32)


def _dot_tn(a, b):
    return lax.dot_general(a, b, (((0,), (0,)), ((), ())), preferred_element_type=F32)


def _rms(h):
    r = lax.rsqrt(jnp.mean(h * h, axis=-1, keepdims=True) + RMS_EPS)
    return h * r, r


def _rms_bwd(dn, hhat, r, g):
    dhh = dn * g
    return r * (dhh - hhat * jnp.mean(dhh * hhat, axis=-1, keepdims=True))


def _ln(x):
    mu = jnp.mean(x, axis=-1, keepdims=True)
    xc = x - mu
    rstd = lax.rsqrt(jnp.mean(xc * xc, axis=-1, keepdims=True) + LN_EPS)
    return xc * rstd, rstd


def _ln_bwd(dy, y, rstd):
    return rstd * (dy - jnp.mean(dy, axis=-1, keepdims=True) - y * jnp.mean(dy * y, axis=-1, keepdims=True))


def _colsum(x):
    return jnp.sum(x, axis=0, keepdims=True)


def _sigmoid(x):
    return 1.0 / (1.0 + jnp.exp(-x))


_GELU_C = 0.7978845608028654
_GELU_A = 0.044715


def _gelu(x):
    t = jnp.tanh(_GELU_C * (x + _GELU_A * x * x * x))
    return 0.5 * x * (1.0 + t)


def _gelu_grad(x):
    t = jnp.tanh(_GELU_C * (x + _GELU_A * x * x * x))
    return 0.5 * (1.0 + t) + 0.5 * x * (1.0 - t * t) * _GELU_C * (1.0 + 3.0 * _GELU_A * x * x)


def _first(i):
    return i == 0


def _last(i):
    return i == pl.num_programs(0) - 1


def _const(shape, idx):
    return pl.BlockSpec(shape, lambda *_, _idx=tuple(idx): _idx, pipeline_mode=pl.Buffered(1))


def _whole(a):
    return _const(a.shape, (0,) * a.ndim)


def _rows(tm, width, col=0):
    return pl.BlockSpec((tm, width), lambda i, _c=col: (i, _c))


def _params(ndim=1, vmem_mb=56):
    return pltpu.CompilerParams(dimension_semantics=("arbitrary",) * ndim, vmem_limit_bytes=vmem_mb << 20)


def _sds(shape, dtype):
    return jax.ShapeDtypeStruct(shape, dtype)


def _pos():
    return lax.axis_index("x"), lax.axis_index("y"), lax.axis_index("c")


def _other_chips(x, y):
    return [(1 - x, y), (x, 1 - y), (1 - x, 1 - y)]


def _half(ref, hc, lead=()):
    r = ref.shape[-2] // 2
    return ref.at[(*lead, pl.ds(pl.multiple_of(hc * r, 8), r), slice(None))]


def _rcopy(src, dst, ssem, rsem, to):
    return pltpu.make_async_remote_copy(src_ref=src, dst_ref=dst, send_sem=ssem, recv_sem=rsem,
                                        device_id=to, device_id_type=MESH)


def _gather_start(bufs, sems):
    ici_s, ici_r, _, _ = sems
    x, y, c = _pos()
    me = 2 * x + y
    for k, (cx, cy) in enumerate(_other_chips(x, y)):
        for t, buf in enumerate(bufs):
            mine = _half(buf, c, (me,))
            _rcopy(mine, mine, ici_s.at[3 * t + k], ici_r.at[3 * t + k], (cx, cy, c)).start()


def _gather_finish(bufs, sems):
    ici_s, ici_r, d2d_s, d2d_r = sems
    x, y, c = _pos()
    me = 2 * x + y
    sib = (x, y, 1 - c)
    chips = _other_chips(x, y)
    for k, (cx, cy) in enumerate(chips):
        for t, buf in enumerate(bufs):
            landed = _half(buf, c, (2 * cx + cy,))
            _rcopy(landed, landed, ici_s.at[3 * t + k], ici_r.at[3 * t + k], (cx, cy, c)).wait_recv()
            _rcopy(landed, landed, d2d_s.at[3 * t + k], d2d_r.at[3 * t + k], sib).start()
    for k, (cx, cy) in enumerate(chips):
        for t, buf in enumerate(bufs):
            got = _half(buf, 1 - c, (2 * cx + cy,))
            _rcopy(got, got, d2d_s.at[3 * t + k], d2d_r.at[3 * t + k], sib).wait_recv()
    for k, (cx, cy) in enumerate(chips):
        for t, buf in enumerate(bufs):
            mine = _half(buf, c, (me,))
            _rcopy(mine, mine, ici_s.at[3 * t + k], ici_r.at[3 * t + k], (cx, cy, c)).wait_send()
            landed = _half(buf, c, (2 * cx + cy,))
            _rcopy(landed, landed, d2d_s.at[3 * t + k], d2d_r.at[3 * t + k], sib).wait_send()


def _xchg_copies(srcs, dsts, nbig, sems):
    ssem, rsem = sems
    x, y, c = _pos()
    cps = []
    for k, (cx, cy) in enumerate(_other_chips(x, y)):
        for t, (src, dst) in enumerate(zip(srcs, dsts)):
            s = src.at[2 * cx + cy] if t < nbig else src
            cps.append(_rcopy(s, dst.at[k], ssem.at[3 * t + k], rsem.at[3 * t + k], (cx, cy, c)))
    return cps


def _gather_sems(n):
    return [pltpu.SemaphoreType.DMA((3 * n,))] * 4


def _xchg_sems(n):
    return [pltpu.SemaphoreType.DMA((3 * n,))] * 2


def _xchg_out_shapes(items):
    return [_sds((3,) + a.shape[-2:], a.dtype) for a in items]


def _call(body, *, name, grid, in_specs, out_specs, out_shape, args, scratch_shapes=(), aliases=None, carry=None):
    in_specs, out_specs, out_shape = list(in_specs), list(out_specs), list(out_shape)
    args = list(args)
    aliases = dict(aliases or {})
    n_in, n_out, n_scr = len(args), len(out_shape), len(scratch_shapes)
    kind = carry[0] if carry else None
    if kind == "gather":
        cin = list(carry[1])
        cout_shape = [_sds(a.shape, a.dtype) for a in cin]
        sems = _gather_sems(len(cin))
        for t in range(len(cin)):
            aliases[n_in + t] = n_out + t
    elif kind == "xchg":
        cin = list(carry[1]) + list(carry[2])
        nbig = len(carry[1])
        cout_shape = _xchg_out_shapes(cin)
        sems = _xchg_sems(len(cin))
    else:
        cin, cout_shape, sems = [], [], []
    nc = len(cin)

    def wrapped(*refs):
        ins = refs[:n_in]
        c_in = refs[n_in:n_in + nc]
        outs = refs[n_in + nc:n_in + nc + n_out]
        c_out = refs[n_in + nc + n_out:n_in + 2 * nc + n_out]
        scr = refs[n_in + 2 * nc + n_out:n_in + 2 * nc + n_out + n_scr]
        sem = refs[n_in + 2 * nc + n_out + n_scr:]
        i = pl.program_id(0)
        if kind:
            @pl.when(_first(i))
            def _():
                if kind == "gather":
                    _gather_start(c_out, sem)
                else:
                    for cp in _xchg_copies(c_in, c_out, nbig, sem):
                        cp.start()
        body(*ins, *outs, *scr)
        if kind:
            @pl.when(_last(i))
            def _():
                if kind == "gather":
                    _gather_finish(c_out, sem)
                else:
                    for cp in _xchg_copies(c_in, c_out, nbig, sem):
                        cp.wait()

    outs = pl.pallas_call(
        wrapped, name=name, grid=grid,
        in_specs=in_specs + [ANY] * nc, out_specs=out_specs + [ANY] * nc,
        out_shape=out_shape + cout_shape,
        scratch_shapes=list(scratch_shapes) + sems,
        input_output_aliases=aliases,
        compiler_params=_params(len(grid)),
    )(*args, *cin)
    return list(outs)


def _xchg_now(big, small):
    items = list(big) + list(small)
    n = len(items)

    def body(*refs):
        cps = _xchg_copies(refs[:n], refs[n:2 * n], len(big), refs[2 * n:])
        for cp in cps:
            cp.start()
        for cp in cps:
            cp.wait()

    return pl.pallas_call(
        body, name="rs_xchg_last", in_specs=[ANY] * n, out_specs=[ANY] * n,
        out_shape=_xchg_out_shapes(items), scratch_shapes=_xchg_sems(n),
    )(*items)


def _sibling_swap(items, name):
    n = len(items)

    def body(*refs):
        src = refs[:n]
        dst = refs[n:2 * n]
        ssem, rsem = refs[2 * n:]
        x, y, c = _pos()
        sib = (x, y, 1 - c)
        cps = []
        for t in range(n):
            lead = (slice(None),) * (len(src[t].shape) - 2)
            cp = _rcopy(_half(src[t], 1 - c, lead), dst[t], ssem.at[t], rsem.at[t], sib)
            cp.start()
            cps.append(cp)
        for cp in cps:
            cp.wait()

    return pl.pallas_call(
        body, name=name, in_specs=[ANY] * n, out_specs=[ANY] * n,
        out_shape=[_sds(a.shape[:-2] + (a.shape[-2] // 2, a.shape[-1]), a.dtype) for a in items],
        scratch_shapes=[pltpu.SemaphoreType.DMA((n,)), pltpu.SemaphoreType.DMA((n,))],
    )(*items)


def _sibling_join(items):
    n = len(items)

    def body(*refs):
        dst = refs[n:2 * n]
        ssem, rsem = refs[2 * n:]
        x, y, c = _pos()
        sib = (x, y, 1 - c)
        cps = []
        for t in range(n):
            mine = _half(dst[t], c)
            cp = _rcopy(mine, mine, ssem.at[t], rsem.at[t], sib)
            cp.start()
            cps.append(cp)
        for t, cp in enumerate(cps):
            theirs = _half(dst[t], 1 - c)
            _rcopy(theirs, theirs, ssem.at[t], rsem.at[t], sib).wait_recv()
            cp.wait_send()

    return pl.pallas_call(
        body, name="rs_sibling_join", in_specs=[ANY] * n, out_specs=[ANY] * n,
        out_shape=[_sds(a.shape, a.dtype) for a in items],
        input_output_aliases={t: t for t in range(n)},
        scratch_shapes=[pltpu.SemaphoreType.DMA((n,))] * 2,
    )(*items)


def _ev_in_fwd(h, g, wev, tm, carry=None):
    s = h.shape[0]

    def body(h_ref, g_ref, w_ref, z_ref):
        hhat, _ = _rms(h_ref[...])
        n = (hhat * g_ref[...]).astype(BF)
        z_ref[...] = _dot(n, w_ref[...]).astype(BF)

    return _call(
        body, name="ev_in_fwd", grid=(s // tm,),
        in_specs=[_rows(tm, D), _const((1, D), (0, 0)), _const((D, DZ), (0, 0))],
        out_specs=[_rows(tm, DZ)], out_shape=[_sds((s, DZ), BF)],
        args=(h, g, wev), carry=carry)


def _halo_specs(tm, width, s, col=0):
    per = tm // HALO
    nh = s // HALO
    prev = pl.BlockSpec((HALO, width), lambda i, _c=col: (jnp.maximum(i * per - 1, 0), _c))
    nxt = pl.BlockSpec((HALO, width), lambda i, _c=col: (jnp.minimum((i + 1) * per, nh - 1), _c))
    return prev, nxt


def _fill_pad(pad_ref, prev, main, nxt, i, tm):
    pad_ref[pl.ds(0, HALO), :] = jnp.where(_first(i), 0.0, prev)
    pad_ref[pl.ds(HALO, tm), :] = main
    pad_ref[pl.ds(HALO + tm, HALO), :] = jnp.where(_last(i), 0.0, nxt)


def _shift8(xs_ref, pad_ref, tm):
    for j in range(8):
        xs_ref[j] = pad_ref[pl.ds(j, tm + 2 * HALO - 8), :]


def _tap(xs_ref, r, off):
    return xs_ref[off % 8, pl.ds(r + 8 * (off // 8), RC), :]


def _glu_a(z):
    return z[:, :DA] * _sigmoid(z[:, DA:2 * DA])


def _gate_c(z):
    return z[:, 2048:] * z[:, 1024:1536]


def _ev_mix_fwd(h, z, wa, ba, lng, lnb, wb, bb, evo, tm, carry=None):
    s = h.shape[0]

    def body(h_ref, z_ref, zp_ref, zn_ref, wa_ref, ba_ref, lng_ref, lnb_ref, wb_ref, bb_ref, wo_ref,
             h1_ref, a2_ref, c2_ref, cat_ref, apad, cpad, c2f, xs):
        i = pl.program_id(0)
        zm = z_ref[...].astype(F32)
        zp = zp_ref[...].astype(F32)
        zn = zn_ref[...].astype(F32)
        _fill_pad(apad, _glu_a(zp), _glu_a(zm), _glu_a(zn), i, tm)
        _fill_pad(cpad, _gate_c(zp), _gate_c(zm), _gate_c(zn), i, tm)
        _shift8(xs, apad, tm)
        for r in range(0, tm, RC):
            acc = jnp.zeros((RC, DA), F32) + ba_ref[...]
            for k in range(KA):
                acc = acc + _tap(xs, r, k + 1) * wa_ref[pl.ds(k, 1), :]
            a2_ref[pl.ds(r, RC), :] = acc
            acc = jnp.zeros((RC, DA), F32) + bb_ref[...]
            for k in range(KB):
                acc = acc + cpad[pl.ds(r + HALO - 1 + k, RC), :] * wb_ref[pl.ds(k, 1), :]
            c2f[pl.ds(r, RC), :] = acc
        y, _ = _ln(a2_ref[...])
        a3 = y * lng_ref[...] + lnb_ref[...]
        cat_ref[:, :DA] = (a3 * _sigmoid(a3)).astype(BF)
        c2 = c2f[...]
        c2_ref[...] = c2.astype(BF)
        cat_ref[:, DA:] = (zm[:, 1536:2048] * c2).astype(BF)
        h1_ref[...] = h_ref[...] + _dot(cat_ref[...], wo_ref[...].reshape(D, D))

    zp_spec, zn_spec = _halo_specs(tm, DZ, s)
    small = lambda rows: _const((rows, DA), (0, 0))
    return _call(
        body, name="ev_mix_fwd", grid=(s // tm,),
        in_specs=[_rows(tm, D), _rows(tm, DZ), zp_spec, zn_spec, small(32), small(1), small(1), small(1),
                  small(8), small(1), _whole(evo)],
        out_specs=[_rows(tm, D), _rows(tm, DA), _rows(tm, DA), _rows(tm, D)],
        out_shape=[_sds((s, D), F32), _sds((s, DA), F32), _sds((s, DA), BF), _sds((s, D), BF)],
        scratch_shapes=[pltpu.VMEM((tm + 2 * HALO, DA), F32), pltpu.VMEM((tm + 2 * HALO, DA), F32),
                        pltpu.VMEM((tm, DA), F32), pltpu.VMEM((8, tm + 2 * HALO - 8, DA), F32)],
        args=(h, z, z, z, wa, ba, lng, lnb, wb, bb, evo), carry=carry)


def _ev_mix_bwd(dh, cat, a2, c2, z, lng, lnb, evo, tm, carry=None):
    s = dh.shape[0]

    def body(dh_ref, cat_ref, a2_ref, c2_ref, zgb_ref, lng_ref, lnb_ref, wo_ref,
             da2_ref, dc2_ref, dgb_ref, dlng_ref, dlnb_ref, gwo_ref, wacc):
        i = pl.program_id(0)

        @pl.when(_first(i))
        def _():
            wacc[...] = jnp.zeros_like(wacc)
            dlng_ref[...] = jnp.zeros_like(dlng_ref)
            dlnb_ref[...] = jnp.zeros_like(dlnb_ref)

        dhb = dh_ref[...].astype(BF)
        dcat = _dot_nt(dhb, wo_ref[...].reshape(D, D))
        wacc[...] += _dot_tn(cat_ref[...], dhb)
        y, rstd = _ln(a2_ref[...])
        a3 = y * lng_ref[...] + lnb_ref[...]
        sg = _sigmoid(a3)
        da3 = dcat[:, :DA] * (sg * (1.0 + a3 * (1.0 - sg)))
        dlng_ref[...] += _colsum(da3 * y)
        dlnb_ref[...] += _colsum(da3)
        da2_ref[...] = _ln_bwd(da3 * lng_ref[...], y, rstd).astype(BF)
        db = dcat[:, DA:]
        dc2_ref[...] = (db * zgb_ref[...].astype(F32)).astype(BF)
        dgb_ref[...] = (db * c2_ref[...].astype(F32)).astype(BF)

        @pl.when(_last(i))
        def _():
            gwo_ref[...] = wacc[...].astype(BF).reshape(NSH, 256, D)

    small = _const((1, DA), (0, 0))
    return _call(
        body, name="ev_mix_bwd", grid=(s // tm,),
        in_specs=[_rows(tm, D), _rows(tm, D), _rows(tm, DA), _rows(tm, DA), _rows(tm, DA, 3), small, small,
                  _whole(evo)],
        out_specs=[_rows(tm, DA), _rows(tm, DA), _rows(tm, DA), small, small, _whole(evo)],
        out_shape=[_sds((s, DA), BF), _sds((s, DA), BF), _sds((s, DA), BF), _sds((1, DA), F32),
                   _sds((1, DA), F32), _sds(evo.shape, BF)],
        scratch_shapes=[pltpu.VMEM((D, D), F32)],
        args=(dh, cat, a2, c2, z, lng, lnb, evo), carry=carry)


def _ev_in_bwd(h, dh, z, da2, dc2, dgb, g, wa, wb, wev, tm):
    s = h.shape[0]

    def body(h_ref, dh_ref, z_ref, zp_ref, zn_ref, da_ref, dap_ref, dan_ref, dc_ref, dcp_ref, dcn_ref,
             dgb_ref, g_ref, wa_ref, wb_ref, w_ref,
             dh0_ref, dwa_ref, dba_ref, dwb_ref, dbb_ref, dg_ref, gw_ref,
             apad, cpad, dapad, dcpad, dz, wacc, dwa_acc, dwb_acc, xs):
        i = pl.program_id(0)

        @pl.when(_first(i))
        def _():
            wacc[...] = jnp.zeros_like(wacc)
            dwa_acc[...] = jnp.zeros_like(dwa_acc)
            dwb_acc[...] = jnp.zeros_like(dwb_acc)
            dba_ref[...] = jnp.zeros_like(dba_ref)
            dbb_ref[...] = jnp.zeros_like(dbb_ref)
            dg_ref[...] = jnp.zeros_like(dg_ref)

        zm = z_ref[...].astype(F32)
        zp = zp_ref[...].astype(F32)
        zn = zn_ref[...].astype(F32)
        _fill_pad(apad, _glu_a(zp), _glu_a(zm), _glu_a(zn), i, tm)
        _fill_pad(cpad, _gate_c(zp), _gate_c(zm), _gate_c(zn), i, tm)
        da2 = da_ref[...].astype(F32)
        dc2 = dc_ref[...].astype(F32)
        _fill_pad(dapad, dap_ref[...].astype(F32), da2, dan_ref[...].astype(F32), i, tm)
        _fill_pad(dcpad, dcp_ref[...].astype(F32), dc2, dcn_ref[...].astype(F32), i, tm)
        dba_ref[...] += _colsum(da2)
        dbb_ref[...] += _colsum(dc2)

        sg = _sigmoid(zm[:, DA:2 * DA])
        aval = zm[:, :DA]
        _shift8(xs, dapad, tm)
        for r in range(0, tm, RC):
            acc = jnp.zeros((RC, DA), F32)
            for k in range(KA):
                acc = acc + _tap(xs, r, 2 * HALO - 1 - k) * wa_ref[pl.ds(k, 1), :]
            sgr = sg[r:r + RC]
            dz[pl.ds(r, RC), 0:DA] = (acc * sgr).astype(BF)
            dz[pl.ds(r, RC), DA:2 * DA] = (acc * aval[r:r + RC] * sgr * (1.0 - sgr)).astype(BF)
            acc = jnp.zeros((RC, DA), F32)
            for k in range(KB):
                acc = acc + dcpad[pl.ds(r + HALO + 1 - k, RC), :] * wb_ref[pl.ds(k, 1), :]
            dz[pl.ds(r, RC), 1024:1536] = (acc * zm[r:r + RC, 2048:]).astype(BF)
            dz[pl.ds(r, RC), 2048:2560] = (acc * zm[r:r + RC, 1024:1536]).astype(BF)
        dz[:, 1536:2048] = dgb_ref[...]

        _shift8(xs, apad, tm)
        for k in range(KA):
            acc = jnp.zeros((8, DA), F32)
            for r in range(0, tm, RC):
                p = _tap(xs, r, k + 1) * dapad[pl.ds(r + HALO, RC), :]
                acc = acc + p.reshape(RC // 8, 8, DA).sum(axis=0)
            dwa_acc[k] += acc
        for k in range(KB):
            acc = jnp.zeros((8, DA), F32)
            for r in range(0, tm, RC):
                p = cpad[pl.ds(r + HALO - 1 + k, RC), :] * dcpad[pl.ds(r + HALO, RC), :]
                acc = acc + p.reshape(RC // 8, 8, DA).sum(axis=0)
            dwb_acc[k] += acc

        gg = g_ref[...]
        hhat, rr = _rms(h_ref[...])
        n = (hhat * gg).astype(BF)
        dzb = dz[...]
        wacc[...] += _dot_tn(n, dzb)
        dn = _dot_nt(dzb, w_ref[...])
        dg_ref[...] += _colsum(dn * hhat)
        dh0_ref[...] = dh_ref[...] + _rms_bwd(dn, hhat, rr, gg)

        @pl.when(_last(i))
        def _():
            for k in range(NSH):
                gw_ref[k] = wacc[:, 640 * k:640 * (k + 1)].astype(BF)
            dwa_ref[...] = jnp.sum(dwa_acc[...], axis=1)
            dwb_ref[...] = jnp.sum(dwb_acc[...], axis=1)

    zp_spec, zn_spec = _halo_specs(tm, DZ, s)
    hp_spec, hn_spec = _halo_specs(tm, DA, s)
    small = lambda rows: _const((rows, DA), (0, 0))
    return _call(
        body, name="ev_in_bwd", grid=(s // tm,),
        in_specs=[_rows(tm, D), _rows(tm, D), _rows(tm, DZ), zp_spec, zn_spec,
                  _rows(tm, DA), hp_spec, hn_spec, _rows(tm, DA), hp_spec, hn_spec, _rows(tm, DA),
                  _const((1, D), (0, 0)), small(32), small(8), _const((D, DZ), (0, 0))],
        out_specs=[_rows(tm, D), small(32), small(1), small(8), small(1), _const((1, D), (0, 0)),
                   _const((NSH, D, 640), (0, 0, 0))],
        out_shape=[_sds((s, D), F32), _sds((32, DA), F32), _sds((1, DA), F32), _sds((8, DA), F32),
                   _sds((1, DA), F32), _sds((1, D), F32), _sds((NSH, D, 640), BF)],
        scratch_shapes=[pltpu.VMEM((tm + 2 * HALO, DA), F32)] * 4 + [
            pltpu.VMEM((tm, DZ), BF), pltpu.VMEM((D, DZ), F32),
            pltpu.VMEM((32, 8, DA), F32), pltpu.VMEM((8, 8, DA), F32),
            pltpu.VMEM((8, tm + 2 * HALO - 8, DA), F32)],
        args=(h, dh, z, z, z, da2, da2, da2, dc2, dc2, dc2, dgb, g, wa, wb, wev))


def _xa_spec(item):
    return _const((NSH, 256, D), (0, item, 0))


def _kv_proj(mem, g, xa, layer):
    def body(mem_ref, g_ref, wk_ref, wv_ref, k_ref, v_ref):
        mhat, _ = _rms(mem_ref[...])
        mn = (mhat * g_ref[...]).astype(BF)
        k_ref[...] = _dot(mn, wk_ref[...].reshape(D, D)).astype(BF)
        v_ref[...] = _dot(mn, wv_ref[...].reshape(D, D)).astype(BF)

    full = _const((NMEM, D), (0, 0))
    return _call(
        body, name=f"kv_proj{layer}", grid=(1,),
        in_specs=[full, _const((1, D), (0, 0)), _xa_spec(XK), _xa_spec(XV)],
        out_specs=[full, full], out_shape=[_sds((NMEM, D), BF), _sds((NMEM, D), BF)],
        args=(mem, g, xa, xa))


def _softmax_rows(sc):
    m = jnp.max(sc, axis=-1, keepdims=True)
    e = jnp.exp(sc - m)
    return e / jnp.sum(e, axis=-1, keepdims=True)


def _xattn_fwd(h, g, k, v, xa, layer, tm, carry=None):
    s = h.shape[0]

    def body(h_ref, g_ref, k_ref, v_ref, wq_ref, wo_ref, ho_ref, q_ref, o_scr):
        hh = h_ref[...]
        hhat, _ = _rms(hh)
        n = (hhat * g_ref[...]).astype(BF)
        q = _dot(n, wq_ref[...].reshape(D, D)).astype(BF)
        q_ref[...] = q
        for hd in range(HEADS):
            cs = slice(HD * hd, HD * (hd + 1))
            p = _softmax_rows(_dot_nt(q[:, cs], k_ref[:, cs]) * ATT_SCALE)
            o_scr[:, cs] = _dot(p.astype(BF), v_ref[:, cs]).astype(BF)
        ho_ref[...] = hh + _dot(o_scr[...], wo_ref[...].reshape(D, D))

    kvs = _const((NMEM, D), (0, 0))
    return _call(
        body, name=f"xattn_fwd{layer}", grid=(s // tm,),
        in_specs=[_rows(tm, D), _const((1, D), (0, 0)), kvs, kvs, _xa_spec(XQ), _xa_spec(XO)],
        out_specs=[_rows(tm, D), _rows(tm, D)],
        out_shape=[_sds((s, D), F32), _sds((s, D), BF)],
        scratch_shapes=[pltpu.VMEM((tm, D), BF)],
        args=(h, g, k, v, xa, xa), carry=carry)


def _xattn_bwd(h, dh, q, g, k, v, xa, layer, tm, carry=None):
    s = h.shape[0]

    def body(h_ref, dh_ref, q_ref, g_ref, k_ref, v_ref, wq_ref, wo_ref,
             dhi_ref, dk_ref, dv_ref, dg_ref, gqo_ref, o_scr, dq_scr, qacc, oacc):
        i = pl.program_id(0)

        @pl.when(_first(i))
        def _():
            qacc[...] = jnp.zeros_like(qacc)
            oacc[...] = jnp.zeros_like(oacc)
            dk_ref[...] = jnp.zeros_like(dk_ref)
            dv_ref[...] = jnp.zeros_like(dv_ref)
            dg_ref[...] = jnp.zeros_like(dg_ref)

        gg = g_ref[...]
        dho = dh_ref[...]
        dhb = dho.astype(BF)
        q = q_ref[...]
        do = _dot_nt(dhb, wo_ref[...].reshape(D, D)).astype(BF)
        for hd in range(HEADS):
            cs = slice(HD * hd, HD * (hd + 1))
            kh = k_ref[:, cs]
            vh = v_ref[:, cs]
            p = _softmax_rows(_dot_nt(q[:, cs], kh) * ATT_SCALE)
            pb = p.astype(BF)
            o_scr[:, cs] = _dot(pb, vh).astype(BF)
            doh = do[:, cs]
            dp = _dot_nt(doh, vh)
            dv_ref[:, cs] += _dot_tn(pb, doh)
            ds = (p * (dp - jnp.sum(dp * p, axis=-1, keepdims=True)) * ATT_SCALE).astype(BF)
            dq_scr[:, cs] = _dot(ds, kh).astype(BF)
            dk_ref[:, cs] += _dot_tn(ds, q[:, cs])
        oacc[...] += _dot_tn(o_scr[...], dhb)
        hhat, rr = _rms(h_ref[...])
        n = (hhat * gg).astype(BF)
        dq = dq_scr[...]
        qacc[...] += _dot_tn(n, dq)
        dn = _dot_nt(dq, wq_ref[...].reshape(D, D))
        dg_ref[...] += _colsum(dn * hhat)
        dhi_ref[...] = dho + _rms_bwd(dn, hhat, rr, gg)

        @pl.when(_last(i))
        def _():
            gqo_ref[:, 0:256, :] = qacc[...].astype(BF).reshape(NSH, 256, D)
            gqo_ref[:, 256:512, :] = oacc[...].astype(BF).reshape(NSH, 256, D)

    kvs = _const((NMEM, D), (0, 0))
    return _call(
        body, name=f"xattn_bwd{layer}", grid=(s // tm,),
        in_specs=[_rows(tm, D), _rows(tm, D), _rows(tm, D), _const((1, D), (0, 0)), kvs, kvs,
                  _xa_spec(XQ), _xa_spec(XO)],
        out_specs=[_rows(tm, D), kvs, kvs, _const((1, D), (0, 0)), _const((NSH, 512, D), (0, 0, 0))],
        out_shape=[_sds((s, D), F32), _sds((NMEM, D), F32), _sds((NMEM, D), F32), _sds((1, D), F32),
                   _sds(xa.shape, BF)],
        scratch_shapes=[pltpu.VMEM((tm, D), BF), pltpu.VMEM((tm, D), BF), pltpu.VMEM((D, D), F32),
                        pltpu.VMEM((D, D), F32)],
        args=(h, dh, q, g, k, v, xa, xa), carry=carry)


def _kv_proj_bwd(mem, g, dk, dv, xa, gxa, layer):
    def body(mem_ref, g_ref, dk_ref, dv_ref, wk_ref, wv_ref, gxa_in, dg_ref, gkv_ref):
        mhat, _ = _rms(mem_ref[...])
        mn = (mhat * g_ref[...]).astype(BF)
        dkb = dk_ref[...].astype(BF)
        dvb = dv_ref[...].astype(BF)
        gkv_ref[:, 0:256, :] = _dot_tn(mn, dkb).astype(BF).reshape(NSH, 256, D)
        gkv_ref[:, 256:512, :] = _dot_tn(mn, dvb).astype(BF).reshape(NSH, 256, D)
        dmn = _dot_nt(dkb, wk_ref[...].reshape(D, D)) + _dot_nt(dvb, wv_ref[...].reshape(D, D))
        dg_ref[...] = _colsum(dmn * mhat)

    full = _const((NMEM, D), (0, 0))
    return _call(
        body, name=f"kv_proj_bwd{layer}", grid=(1,),
        in_specs=[full, _const((1, D), (0, 0)), full, full, _xa_spec(XK), _xa_spec(XV), ANY],
        out_specs=[_const((1, D), (0, 0)), _const((NSH, 512, D), (0, 1, 0))],
        out_shape=[_sds((1, D), F32), _sds(gxa.shape, BF)],
        aliases={6: 1},
        args=(mem, g, dk, dv, xa, xa, gxa))


def _ffn_fwd(h, g, wg, wu, dn, layer, tm, carry=None):
    s = h.shape[0]
    wg, wu, dn = (a.reshape(DFF, D) for a in (wg, wu, dn))

    def body(h_ref, g_ref, wg_ref, wu_ref, wd_ref, ho_ref, gt_ref, up_ref):
        hh = h_ref[...]
        hhat, _ = _rms(hh)
        n = (hhat * g_ref[...]).astype(BF)
        out = hh
        for c0, c1 in FF_CHUNKS:
            gt = _dot_nt(n, wg_ref[c0:c1, :])
            up = _dot_nt(n, wu_ref[c0:c1, :])
            gt_ref[:, c0:c1] = gt.astype(BF)
            up_ref[:, c0:c1] = up.astype(BF)
            act = (gt * _sigmoid(gt) * up).astype(BF)
            out = out + _dot(act, wd_ref[c0:c1, :])
        ho_ref[...] = out

    return _call(
        body, name=f"ffn_fwd{layer}", grid=(s // tm,),
        in_specs=[_rows(tm, D), _const((1, D), (0, 0)), _whole(wg), _whole(wu), _whole(dn)],
        out_specs=[_rows(tm, D), _rows(tm, DFF), _rows(tm, DFF)],
        out_shape=[_sds((s, D), F32), _sds((s, DFF), BF), _sds((s, DFF), BF)],
        args=(h, g, wg, wu, dn), carry=carry)


def _ffn_bwd_dx(h, dh, gt, up, g, wg, wu, dn, layer, tm, carry=None):
    s = h.shape[0]
    wg, wu, dn = (a.reshape(DFF, D) for a in (wg, wu, dn))

    def body(h_ref, dh_ref, gt_ref, up_ref, g_ref, wg_ref, wu_ref, wd_ref,
             dhi_ref, dg_ref, dgt_ref, dup_ref, act_ref, n_ref):
        i = pl.program_id(0)

        @pl.when(_first(i))
        def _():
            dg_ref[...] = jnp.zeros_like(dg_ref)

        gg = g_ref[...]
        dho = dh_ref[...]
        dhb = dho.astype(BF)
        dn_ = jnp.zeros((tm, D), F32)
        for c0, c1 in FF_CHUNKS:
            gtv = gt_ref[:, c0:c1].astype(F32)
            upv = up_ref[:, c0:c1].astype(F32)
            sg = _sigmoid(gtv)
            silu = gtv * sg
            dact = _dot_nt(dhb, wd_ref[c0:c1, :])
            dgt = (dact * upv * (sg * (1.0 + gtv * (1.0 - sg)))).astype(BF)
            dup = (dact * silu).astype(BF)
            dgt_ref[:, c0:c1] = dgt
            dup_ref[:, c0:c1] = dup
            act_ref[:, c0:c1] = (silu * upv).astype(BF)
            dn_ = dn_ + _dot(dgt, wg_ref[c0:c1, :]) + _dot(dup, wu_ref[c0:c1, :])
        hhat, rr = _rms(h_ref[...])
        n_ref[...] = (hhat * gg).astype(BF)
        dg_ref[...] += _colsum(dn_ * hhat)
        dhi_ref[...] = dho + _rms_bwd(dn_, hhat, rr, gg)

    wide = _rows(tm, DFF)
    return _call(
        body, name=f"ffn_bwd_dx{layer}", grid=(s // tm,),
        in_specs=[_rows(tm, D), _rows(tm, D), wide, wide, _const((1, D), (0, 0)),
                  _whole(wg), _whole(wu), _whole(dn)],
        out_specs=[_rows(tm, D), _const((1, D), (0, 0)), wide, wide, wide, _rows(tm, D)],
        out_shape=[_sds((s, D), F32), _sds((1, D), F32), _sds((s, DFF), BF), _sds((s, DFF), BF),
                   _sds((s, DFF), BF), _sds((s, D), BF)],
        args=(h, dh, gt, up, g, wg, wu, dn), carry=carry)


def _ffn_wgrad(a, b, name, tm):
    s = a.shape[0]
    tm = min(tm, s)

    def body(a_ref, b_ref, o_ref, acc):
        i = pl.program_id(0)

        @pl.when(_first(i))
        def _():
            acc[...] = jnp.zeros_like(acc)

        acc[...] += _dot_tn(a_ref[...], b_ref[...].astype(BF))

        @pl.when(_last(i))
        def _():
            o_ref[...] = acc[...].astype(BF).reshape(NSH, FSH, D)

    return _call(
        body, name=name, grid=(s // tm,),
        in_specs=[_rows(tm, DFF), _rows(tm, D)],
        out_specs=[_const((NSH, FSH, D), (0, 0, 0))],
        out_shape=[_sds((NSH, FSH, D), BF)],
        scratch_shapes=[pltpu.VMEM((DFF, D), F32)],
        args=(a, b))[0]


def _sgu_fwd(h, g, lng, lnb, ws, bsb, wod, odo, tm, carry=None):
    s = h.shape[0]

    def body(h_ref, g_ref, lng_ref, lnb_ref, ws_ref, bsb_ref, wi_ref, wo_ref, ho_ref, zp_ref, y_scr):
        hh = h_ref[...]
        hhat, _ = _rms(hh)
        n = (hhat * g_ref[...]).astype(BF)
        zpre = _dot(n, wi_ref[...])
        zp_ref[...] = zpre.astype(BF)
        z = _gelu(zpre)
        u = z[:, :D]
        y, _ = _ln(z[:, D:])
        vn = (y * lng_ref[...] + lnb_ref[...]).astype(BF)
        for c in range(tm // CHUNK):
            rs = slice(CHUNK * c, CHUNK * (c + 1))
            for gi in range(GROUPS):
                cs = slice(CHUNK * gi, CHUNK * (gi + 1))
                sv = _dot(ws_ref[gi], vn[rs, cs]) + bsb_ref[gi]
                y_scr[rs, cs] = (u[rs, cs] * sv).astype(BF)
        ho_ref[...] = hh + _dot(y_scr[...], wo_ref[...].reshape(D, D))

    vec = _const((1, D), (0, 0))
    sq = _const((GROUPS, CHUNK, CHUNK), (0, 0, 0))
    return _call(
        body, name="sgu_fwd", grid=(s // tm,),
        in_specs=[_rows(tm, D), vec, vec, vec, sq, sq, _const((D, 2 * D), (0, 0)), _whole(odo)],
        out_specs=[_rows(tm, D), _rows(tm, 2 * D)],
        out_shape=[_sds((s, D), F32), _sds((s, 2 * D), BF)],
        scratch_shapes=[pltpu.VMEM((tm, D), BF)],
        args=(h, g, lng, lnb, ws, bsb, wod, odo), carry=carry)


def _sgu_bwd(h, dh, zpre, g, lng, lnb, ws, wst, bsb, wod, odo, tm):
    s = h.shape[0]

    def body(h_ref, dh_ref, zp_ref, g_ref, lng_ref, lnb_ref, ws_ref, wst_ref, bsb_ref, wi_ref, wo_ref,
             dhi_ref, dws_ref, dbs_ref, dlng_ref, dlnb_ref, dg_ref, gwi_ref, gwo_ref,
             y_scr, dz_scr, dvn_scr, iacc, oacc, bacc):
        i = pl.program_id(0)

        @pl.when(_first(i))
        def _():
            for ref in (iacc, oacc, bacc, dws_ref, dlng_ref, dlnb_ref, dg_ref):
                ref[...] = jnp.zeros_like(ref)

        gg = g_ref[...]
        dho = dh_ref[...]
        dhb = dho.astype(BF)
        zpre_v = zp_ref[...].astype(F32)
        z = _gelu(zpre_v)
        u = z[:, :D]
        yl, rstd = _ln(z[:, D:])
        vn = (yl * lng_ref[...] + lnb_ref[...]).astype(BF)
        dy = _dot_nt(dhb, wo_ref[...].reshape(D, D))
        for c in range(tm // CHUNK):
            rs = slice(CHUNK * c, CHUNK * (c + 1))
            for gi in range(GROUPS):
                cs = slice(CHUNK * gi, CHUNK * (gi + 1))
                vb = vn[rs, cs]
                sv = _dot(ws_ref[gi], vb) + bsb_ref[gi]
                ub = u[rs, cs]
                dyb = dy[rs, cs]
                y_scr[rs, cs] = (ub * sv).astype(BF)
                dz_scr[rs, cs] = dyb * sv
                dsv = dyb * ub
                bacc[gi] += dsv
                dsvb = dsv.astype(BF)
                dws_ref[gi] += _dot_nt(dsvb, vb)
                dvn_scr[rs, cs] = _dot(wst_ref[gi], dsvb)
        oacc[...] += _dot_tn(y_scr[...], dhb)
        dvn = dvn_scr[...]
        dlng_ref[...] += _colsum(dvn * yl)
        dlnb_ref[...] += _colsum(dvn)
        dz_scr[:, D:] = _ln_bwd(dvn * lng_ref[...], yl, rstd)
        dzb = (dz_scr[...] * _gelu_grad(zpre_v)).astype(BF)
        hhat, rr = _rms(h_ref[...])
        n = (hhat * gg).astype(BF)
        iacc[...] += _dot_tn(n, dzb)
        dn = _dot_nt(dzb, wi_ref[...])
        dg_ref[...] += _colsum(dn * hhat)
        dhi_ref[...] = dho + _rms_bwd(dn, hhat, rr, gg)

        @pl.when(_last(i))
        def _():
            for k in range(NSH):
                gwi_ref[k] = iacc[:, 512 * k:512 * (k + 1)].astype(BF)
            gwo_ref[...] = oacc[...].astype(BF).reshape(NSH, 256, D)
            dbs_ref[...] = jnp.sum(bacc[...], axis=-1)

    vec = _const((1, D), (0, 0))
    sq = _const((GROUPS, CHUNK, CHUNK), (0, 0, 0))
    return _call(
        body, name="sgu_bwd", grid=(s // tm,),
        in_specs=[_rows(tm, D), _rows(tm, D), _rows(tm, 2 * D), vec, vec, vec, sq, sq, sq,
                  _const((D, 2 * D), (0, 0)), _whole(odo)],
        out_specs=[_rows(tm, D), sq, _const((GROUPS, CHUNK), (0, 0)), vec, vec, vec,
                   _const((NSH, D, 512), (0, 0, 0)), _whole(odo)],
        out_shape=[_sds((s, D), F32), _sds((GROUPS, CHUNK, CHUNK), F32), _sds((GROUPS, CHUNK), F32),
                   _sds((1, D), F32), _sds((1, D), F32), _sds((1, D), F32), _sds((NSH, D, 512), BF),
                   _sds(odo.shape, BF)],
        scratch_shapes=[pltpu.VMEM((tm, D), BF), pltpu.VMEM((tm, 2 * D), F32),
                        pltpu.VMEM((tm, D), F32), pltpu.VMEM((D, 2 * D), F32), pltpu.VMEM((D, D), F32),
                        pltpu.VMEM((GROUPS, CHUNK, CHUNK), F32)],
        args=(h, dh, zpre, g, lng, lnb, ws, wst, bsb, wod, odo))


def _final_loss(h, tgt, g, tm):
    s = h.shape[0]

    def body(h_ref, t_ref, g_ref, dh_ref, loss_ref, dg_ref):
        i = pl.program_id(0)

        @pl.when(_first(i))
        def _():
            loss_ref[...] = jnp.zeros_like(loss_ref)
            dg_ref[...] = jnp.zeros_like(dg_ref)

        gg = g_ref[...]
        hhat, rr = _rms(h_ref[...])
        diff = hhat * gg - t_ref[...]
        loss_ref[...] += jnp.sum(diff * diff) * (0.5 / D)
        dout = diff * (1.0 / D)
        dg_ref[...] += _colsum(dout * hhat)
        dh_ref[...] = _rms_bwd(dout, hhat, rr, gg)

    return _call(
        body, name="final_loss", grid=(s // tm,),
        in_specs=[_rows(tm, D), _rows(tm, D), _const((1, D), (0, 0))],
        out_specs=[_rows(tm, D), _const((1, 128), (0, 0)), _const((1, D), (0, 0))],
        out_shape=[_sds((s, D), F32), _sds((1, 128), F32), _sds((1, D), F32)],
        args=(h, tgt, g))


def _own_plane(shape):
    return pl.BlockSpec((1,) + shape, lambda i, chip_ref: (chip_ref[0], 0, 0), pipeline_mode=pl.Buffered(1))


def _pack_rows(chip_idx, w_down, w_gate_t, w_up_t, xq, xo, xk, xv, ev_out, od_out, first):
    nf = len(first)

    def body(chip_ref, wd_ref, wg_ref, wu_ref, xq_ref, xo_ref, xk_ref, xv_ref, ev_ref, od_ref, *rest):
        (evo_ref, odo_ref, dn0_ref, dn1_ref, wg0_ref, wg1_ref, wu0_ref, wu1_ref, xa0_ref,
         xa1_ref) = rest[nf:nf + 10]
        gathered = rest[nf + 10:2 * nf + 10]
        sems = rest[2 * nf + 10:]
        _gather_start(gathered, sems)
        evo_ref[0] = ev_ref[0].astype(BF)
        odo_ref[0] = od_ref[0].astype(BF)
        for src, outs in ((wd_ref, (dn0_ref, dn1_ref)), (wg_ref, (wg0_ref, wg1_ref)), (wu_ref, (wu0_ref, wu1_ref))):
            for layer, o_ref in enumerate(outs):
                o_ref[0] = src[layer].astype(BF)
        for layer, xa_ref in enumerate((xa0_ref, xa1_ref)):
            for item, src in ((XQ, xq_ref), (XO, xo_ref), (XK, xk_ref), (XV, xv_ref)):
                xa_ref[0, 256 * item:256 * (item + 1), :] = src[layer].astype(BF)
        _gather_finish(gathered, sems)

    ins = (w_down, w_gate_t, w_up_t, xq, xo, xk, xv, ev_out, od_out)
    shapes = [(256, D), (256, D)] + [(FSH, D)] * 6 + [(4 * 256, D), (4 * 256, D)]
    grid_spec = pltpu.PrefetchScalarGridSpec(
        num_scalar_prefetch=1, grid=(1,), in_specs=[_whole(a) for a in ins] + [ANY] * nf,
        out_specs=[_own_plane(sh) for sh in shapes] + [ANY] * nf,
        scratch_shapes=_gather_sems(nf))
    outs = pl.pallas_call(
        body, name="pack_rows", grid_spec=grid_spec,
        out_shape=[_sds((NSH,) + sh, BF) for sh in shapes] + [_sds(a.shape, a.dtype) for a in first],
        input_output_aliases={1 + len(ins) + t: len(shapes) + t for t in range(nf)},
        compiler_params=_params(),
    )(chip_idx, *ins, *first)
    return outs[:len(shapes)], outs[len(shapes):]


def _pack_cols(chip_idx, od_in, ev_in, conv_a, conv_b, ln_g, ln_b):
    def body(chip_ref, od_ref, ev_ref, ca_ref, cb_ref, lg_ref, lb_ref, cev_ref, cod_ref, sp_ref):
        cev_ref[0] = ev_ref[0].astype(BF)
        cod_ref[0] = od_ref[0].astype(BF)
        sp_ref[...] = jnp.zeros_like(sp_ref)
        sp_ref[0, 0:KA, 0:128] = ca_ref[0]
        sp_ref[0, 32:32 + KB, 0:128] = cb_ref[0]
        sp_ref[0, 40:41, :] = lg_ref[...]
        sp_ref[0, 41:42, :] = lb_ref[...]

    ins = (od_in, ev_in, conv_a, conv_b, ln_g, ln_b)
    shapes = [(D, 640), (D, 512)]
    grid_spec = pltpu.PrefetchScalarGridSpec(
        num_scalar_prefetch=1, grid=(1,), in_specs=[_whole(a) for a in ins],
        out_specs=[_own_plane(sh) for sh in shapes] + [_own_plane((SP_ROWS, 256))])
    return pl.pallas_call(
        body, name="pack_cols", grid_spec=grid_spec,
        out_shape=[_sds((NSH,) + sh, BF) for sh in shapes] + [_sds((NSH, SP_ROWS, 256), F32)],
        compiler_params=_params(),
    )(chip_idx, *ins)


def _assemble(piece, parts, name):
    tr = 256
    cols = piece.shape[2]

    def body(p_ref, *outs):
        for k in range(NSH):
            for (off, width), o_ref in zip(parts, outs):
                o_ref[:, width * k:width * (k + 1)] = p_ref[k, :, off:off + width]

    return _call(
        body, name=name, grid=(D // tr,),
        in_specs=[pl.BlockSpec((NSH, tr, cols), lambda i: (0, i, 0))],
        out_specs=[pl.BlockSpec((tr, NSH * width), lambda i: (i, 0)) for _, width in parts],
        out_shape=[_sds((D, NSH * width), BF) for _, width in parts],
        args=(piece,))


def _blocks_of(rows, cols, itemsize):
    rb = rows
    while rb * cols * itemsize > (2 << 20) and rb % 2 == 0 and (rb // 2) % 16 == 0:
        rb //= 2
    return rb


def _add_half(full, recv, c_idx, name):
    p, r, cols = full.shape
    rb = _blocks_of(r // 2, cols, 4)
    nb = (r // 2) // rb

    def body(c_ref, a_ref, b_ref, o_ref):
        o_ref[...] = (a_ref[...].astype(F32) + b_ref[...].astype(F32)).astype(o_ref.dtype)

    grid_spec = pltpu.PrefetchScalarGridSpec(
        num_scalar_prefetch=1, grid=(p, nb),
        in_specs=[pl.BlockSpec((1, rb, cols), lambda j, i, c_ref: (j, c_ref[0] * nb + i, 0)),
                  pl.BlockSpec((1, rb, cols), lambda j, i, c_ref: (j, i, 0))],
        out_specs=pl.BlockSpec((1, rb, cols), lambda j, i, c_ref: (j, i, 0)))
    return pl.pallas_call(
        body, name=name, grid_spec=grid_spec, out_shape=_sds(recv.shape, full.dtype),
        compiler_params=_params(2),
    )(c_idx, full, recv)


def _sum_chips(own, got, plane_c, name):
    _, r, cols = own.shape
    rb = _blocks_of(r, cols, 4)
    nb = r // rb

    def body(s_ref, a_ref, b_ref, o_ref):
        f = lambda v: v.astype(F32)
        o_ref[...] = (f(a_ref[0]) + f(b_ref[1])) + (f(b_ref[0]) + f(b_ref[2]))

    grid_spec = pltpu.PrefetchScalarGridSpec(
        num_scalar_prefetch=1, grid=(nb,),
        in_specs=[pl.BlockSpec((1, rb, cols), lambda i, s_ref: (s_ref[0], i, 0)),
                  pl.BlockSpec((3, rb, cols), lambda i, s_ref: (0, i, 0))],
        out_specs=pl.BlockSpec((rb, cols), lambda i, s_ref: (s_ref[1] * nb + i, 0)))
    return pl.pallas_call(
        body, name=name, grid_spec=grid_spec, out_shape=_sds((2 * r, cols), F32),
        compiler_params=_params(1),
    )(plane_c, own, got)


def _adamw_math(w, g, m, v):
    m = ADAM_B1 * m + (1.0 - ADAM_B1) * g
    v = ADAM_B2 * v + (1.0 - ADAM_B2) * (g * g)
    m_hat = m / (1.0 - ADAM_B1 ** ADAM_STEP)
    v_hat = v / (1.0 - ADAM_B2 ** ADAM_STEP)
    delta = -ADAM_LR * (m_hat / (jnp.sqrt(v_hat) + ADAM_EPS) + ADAM_WD * w)
    return delta, m, v


def _adamw_big(w, m, v, gsrcs, gblock, gidx, name, rb):
    nl, r, cols = w.shape
    assert len(gsrcs) == nl

    def body(w_ref, m_ref, v_ref, *rest):
        g_refs, (go_ref, d_ref, mo_ref, vo_ref) = rest[:nl], rest[nl:]
        g = g_refs[0][:, 0:cols]
        if nl == 2:
            g = jnp.where(pl.program_id(0) == 0, g, g_refs[1][:, 0:cols])
        delta, mn, vn = _adamw_math(w_ref[0], g, m_ref[0], v_ref[0])
        go_ref[0] = g
        d_ref[0] = delta
        mo_ref[0] = mn
        vo_ref[0] = vn

    wspec = pl.BlockSpec((1, rb, cols), lambda l, i: (l, i, 0))
    gspec = pl.BlockSpec(gblock, lambda l, i: gidx(i))
    return pl.pallas_call(
        body, name=name, grid=(nl, r // rb),
        in_specs=[wspec, wspec, wspec] + [gspec] * nl,
        out_specs=[wspec] * 4, out_shape=[_sds(w.shape, F32)] * 4,
        compiler_params=_params(2),
    )(w, m, v, *gsrcs)


_SMALL = (
    ("g_mix", (2, D), "a", (0, 2), 0, False),
    ("g_xattn", (2, D), "a", (2, 4), 0, False),
    ("g_mem", (2, D), "a", (4, 6), 0, False),
    ("g_ffn", (2, D), "a", (6, 8), 0, False),
    ("g_final", (1, D), "a", (8, 9), 0, False),
    ("ev_a_conv_w", (KA, 128), "a", (16, 16 + KA), 0, True),
    ("ev_a_conv_b", (1, DA), "a", (11, 12), 0, False),
    ("ev_a_ln_g", (1, DA), "a", (11, 12), DA, False),
    ("ev_a_ln_b", (1, DA), "a", (12, 13), 0, False),
    ("ev_b_conv_w", (KB, 128), "a", (48, 48 + KB), 0, True),
    ("ev_b_conv_b", (1, DA), "a", (12, 13), DA, False),
    ("od_c_ln_g", (1, 256), "a", (9, 10), 0, True),
    ("od_c_ln_b", (1, 256), "a", (10, 11), 0, True),
    ("od_w_s", (GROUPS * CHUNK, CHUNK), "b", (0, GROUPS * CHUNK), 0, False),
    ("od_b_s", (GROUPS, CHUNK), "a", (56, 64), 0, False),
)


def _adamw_small(ga, gb, wmv):
    ns = len(_SMALL)

    def body(*refs):
        ga_ref, gb_ref = refs[0], refs[1]
        ins = refs[2:2 + 3 * ns]
        outs = refs[2 + 3 * ns:]
        x, y, _ = _pos()
        chip = 2 * x + y
        for j, (_, shape, pack, (r0, r1), col, sharded) in enumerate(_SMALL):
            src = ga_ref if pack == "a" else gb_ref
            width = shape[1]
            if sharded:
                g = jnp.zeros(shape, F32)
                for k in range(NSH):
                    blk = src[r0:r1, col + width * k:col + width * (k + 1)]
                    g = g + jnp.where(chip == k, blk, 0.0)
            else:
                g = src[r0:r1, col:col + width]
            w_ref, m_ref, v_ref = ins[3 * j:3 * j + 3]
            delta, mn, vn = _adamw_math(w_ref[...], g, m_ref[...], v_ref[...])
            go_ref, d_ref, mo_ref, vo_ref = outs[4 * j:4 * j + 4]
            go_ref[...] = g
            d_ref[...] = delta
            mo_ref[...] = mn
            vo_ref[...] = vn

    vm = pl.BlockSpec(memory_space=pltpu.VMEM)
    flat = [a for trio in wmv for a in trio]
    out_shape = []
    for (_, shape, *_rest) in _SMALL:
        out_shape += [_sds(shape, F32)] * 4
    outs = pl.pallas_call(
        body, name="adamw_small",
        in_specs=[vm] * (2 + 3 * ns), out_specs=[vm] * (4 * ns), out_shape=out_shape,
    )(ga, gb, *flat)
    return [tuple(outs[4 * j:4 * j + 4]) for j in range(ns)]


def _small_params(gs, W):
    unshard = lambda a: jnp.transpose(a, (1, 0, 2)).reshape(a.shape[1], NSH * a.shape[2])
    od_w_s, od_b_s = W["od_w_s"], W["od_b_s"]
    return dict(
        g_mix=W["g_mix"], g_xattn=W["g_xattn"], g_mem=W["g_mem"], g_ffn=W["g_ffn"],
        g_final=W["g_final"].reshape(1, D),
        wa=unshard(gs[:, 0:32, 0:128]), ba=W["ev_a_conv_b"], lng_a=W["ev_a_ln_g"], lnb_a=W["ev_a_ln_b"],
        wb=unshard(gs[:, 32:40, 0:128]), bb=W["ev_b_conv_b"],
        lng_c=unshard(gs[:, 40:41, :]), lnb_c=unshard(gs[:, 41:42, :]),
        ws=od_w_s[0].astype(BF), wst=jnp.swapaxes(od_w_s[0], 1, 2).astype(BF),
        bsb=jnp.broadcast_to(od_b_s[0][:, :, None], (GROUPS, CHUNK, CHUNK)),
    )


def _small_grad_packs(g):
    z = lambda r, c: jnp.zeros((r, c), F32)
    wide = lambda a: jnp.concatenate([a, z(a.shape[0], D - a.shape[1])], axis=1)
    rows = [g["g_mix"], g["g_xattn"], g["g_mem"], g["g_ffn"], g["g_final"], g["od_c_ln_g"], g["od_c_ln_b"],
            jnp.concatenate([g["ev_a_conv_b"], g["ev_a_ln_g"]], axis=1),
            jnp.concatenate([g["ev_a_ln_b"], g["ev_b_conv_b"]], axis=1),
            z(3, D), wide(g["ev_a_conv_w"]), wide(g["ev_b_conv_w"]), wide(g["od_b_s"])]
    a = jnp.concatenate(rows, axis=0)
    assert a.shape == (SA_ROWS, D), a.shape
    return a, g["od_w_s"].reshape(GROUPS * CHUNK, CHUNK)


def _step(x, mem, tgt, W, P, c_idx, tm_f, tm_b, tm_w):
    row = lambda a, i: a[i:i + 1]

    def pre_sum(names, arrays, tag):
        recv = _sibling_swap(arrays, f"rs_swap_{tag}")
        return [_add_half(a, r, c_idx, f"rs_add_{n}") for n, a, r in zip(names, arrays, recv)]

    sm = _small_params(P["gs"], W)
    (wev,) = _assemble(P["cev"], [(0, 640)], "assemble_ev")
    z, evo = _ev_in_fwd(x, row(sm["g_mix"], 0), wev, 2 * tm_f, carry=("gather", [P["evo"]]))
    h1, a2, c2, cat, xa0, wg0, wu0 = _ev_mix_fwd(
        x, z, sm["wa"], sm["ba"], sm["lng_a"], sm["lnb_a"], sm["wb"], sm["bb"], evo, tm_f,
        carry=("gather", [P["xa0"], P["wg0"], P["wu0"]]))
    k0, v0 = _kv_proj(mem, row(sm["g_mem"], 0), xa0, 0)
    h2, q0, dn0, cod = _xattn_fwd(h1, row(sm["g_xattn"], 0), k0, v0, xa0, 0, 2 * tm_f,
                                  carry=("gather", [P["dn0"], P["cod"]]))
    (wod,) = _assemble(cod, [(0, 512)], "assemble_od")
    h3, gt0, up0, odo, xa1, wg1, wu1 = _ffn_fwd(h2, row(sm["g_ffn"], 0), wg0, wu0, dn0, 0, tm_f,
                                                carry=("gather", [P["odo"], P["xa1"], P["wg1"], P["wu1"]]))
    h4, zpre, dn1 = _sgu_fwd(h3, row(sm["g_mix"], 1), sm["lng_c"], sm["lnb_c"], sm["ws"], sm["bsb"], wod, odo,
                             tm_f, carry=("gather", [P["dn1"]]))
    k1, v1 = _kv_proj(mem, row(sm["g_mem"], 1), xa1, 1)
    h5, q1 = _xattn_fwd(h4, row(sm["g_xattn"], 1), k1, v1, xa1, 1, 2 * tm_f)
    h6, gt1, up1 = _ffn_fwd(h5, row(sm["g_ffn"], 1), wg1, wu1, dn1, 1, tm_f)
    dh6, loss, dg_final = _final_loss(h6, tgt, sm["g_final"], 2 * tm_f)

    g = {"g_final": dg_final}
    parts, gots = {}, {}
    def ffn_wgrads(layer, dgt, dup, act, nb, dh_out):
        names = (f"wg{layer}", f"wu{layer}", f"dn{layer}")
        grads = (_ffn_wgrad(dgt, nb, f"ffn_wgrad_gate{layer}", tm_w),
                 _ffn_wgrad(dup, nb, f"ffn_wgrad_up{layer}", tm_w),
                 _ffn_wgrad(act, dh_out, f"ffn_wgrad_down{layer}", tm_w))
        for n, p in zip(names, pre_sum(names, grads, f"ffn{layer}")):
            parts[n] = p
        return [parts[n] for n in names], names

    dh5, dg_ffn1, dgt, dup, act, nb = _ffn_bwd_dx(h5, dh6, gt1, up1, row(sm["g_ffn"], 1), wg1, wu1, dn1, 1, tm_b)
    grp, names = ffn_wgrads(1, dgt, dup, act, nb, dh6)
    dh4, dk, dv, dg_xa1, g_xa1, *got = _xattn_bwd(
        h4, dh5, q1, row(sm["g_xattn"], 1), k1, v1, xa1, 1, tm_f, carry=("xchg", grp, []))
    gots.update(zip(names, got))
    dg_mem1, g_xa1 = _kv_proj_bwd(mem, row(sm["g_mem"], 1), dk, dv, xa1, g_xa1, 1)
    dh3, g["od_w_s"], g["od_b_s"], g["od_c_ln_g"], g["od_c_ln_b"], dg_mix1, g_cod, g_odo = _sgu_bwd(
        h3, dh4, zpre, row(sm["g_mix"], 1), sm["lng_c"], sm["lnb_c"], sm["ws"], sm["wst"], sm["bsb"], wod, odo, tm_b)
    parts["xa1"], parts["cod"], parts["odo"] = pre_sum(("xa1", "cod", "odo"), (g_xa1, g_cod, g_odo), "b")
    dh2, dg_ffn0, dgt, dup, act, nb, gots["xa1"], gots["cod"], gots["odo"] = _ffn_bwd_dx(
        h2, dh3, gt0, up0, row(sm["g_ffn"], 0), wg0, wu0, dn0, 0, tm_b,
        carry=("xchg", [parts["xa1"], parts["cod"], parts["odo"]], []))
    grp, names = ffn_wgrads(0, dgt, dup, act, nb, dh3)
    dh1, dk, dv, dg_xa0, g_xa0, *got = _xattn_bwd(
        h1, dh2, q0, row(sm["g_xattn"], 0), k0, v0, xa0, 0, tm_f, carry=("xchg", grp, []))
    gots.update(zip(names, got))
    dg_mem0, g_xa0 = _kv_proj_bwd(mem, row(sm["g_mem"], 0), dk, dv, xa0, g_xa0, 0)
    (parts["xa0"],) = pre_sum(("xa0",), (g_xa0,), "d")
    da2, dc2, dgb, g["ev_a_ln_g"], g["ev_a_ln_b"], g_evo, gots["xa0"] = _ev_mix_bwd(
        dh1, cat, a2, c2, z, sm["lng_a"], sm["lnb_a"], evo, tm_f, carry=("xchg", [parts["xa0"]], []))
    dx, g["ev_a_conv_w"], g["ev_a_conv_b"], g["ev_b_conv_w"], g["ev_b_conv_b"], dg_mix0, g_cev = _ev_in_bwd(
        x, dh1, z, da2, dc2, dgb, row(sm["g_mix"], 0), sm["wa"], sm["wb"], wev, tm_b)
    g["g_mix"] = jnp.concatenate([dg_mix0, dg_mix1], axis=0)
    g["g_xattn"] = jnp.concatenate([dg_xa0, dg_xa1], axis=0)
    g["g_mem"] = jnp.concatenate([dg_mem0, dg_mem1], axis=0)
    g["g_ffn"] = jnp.concatenate([dg_ffn0, dg_ffn1], axis=0)
    ga, gb = _small_grad_packs(g)
    parts["evo"], parts["cev"], parts["ga"], parts["gb"] = pre_sum(
        ("evo", "cev", "ga", "gb"), (g_evo, g_cev, ga[None], gb[None]), "e")
    gots["evo"], gots["cev"], gots["ga"], gots["gb"] = _xchg_now(
        [parts["evo"], parts["cev"]], [parts["ga"][0], parts["gb"][0]])
    return loss, dx, parts, gots


def kernel(x, mem, g_mix, g_xattn, g_mem, g_ffn, g_final, ev_w_in, ev_a_conv_w, ev_a_conv_b, ev_a_ln_g, ev_a_ln_b, ev_b_conv_w, ev_b_conv_b, ev_w_out, od_w_in, od_c_ln_g, od_c_ln_b, od_w_s, od_b_s, od_w_out, xa_w_q, xa_w_k, xa_w_v, xa_w_o, ffn_w_gate, ffn_w_up, ffn_w_down, loss_target, m_g_mix, m_g_xattn, m_g_mem, m_g_ffn, m_g_final, m_ev_w_in, m_ev_a_conv_w, m_ev_a_conv_b, m_ev_a_ln_g, m_ev_a_ln_b, m_ev_b_conv_w, m_ev_b_conv_b, m_ev_w_out, m_od_w_in, m_od_c_ln_g, m_od_c_ln_b, m_od_w_s, m_od_b_s, m_od_w_out, m_xa_w_q, m_xa_w_k, m_xa_w_v, m_xa_w_o, m_ffn_w_gate, m_ffn_w_up, m_ffn_w_down, v_g_mix, v_g_xattn, v_g_mem, v_g_ffn, v_g_final, v_ev_w_in, v_ev_a_conv_w, v_ev_a_conv_b, v_ev_a_ln_g, v_ev_a_ln_b, v_ev_b_conv_w, v_ev_b_conv_b, v_ev_w_out, v_od_w_in, v_od_c_ln_g, v_od_c_ln_b, v_od_w_s, v_od_b_s, v_od_w_out, v_xa_w_q, v_xa_w_k, v_xa_w_v, v_xa_w_o, v_ffn_w_gate, v_ffn_w_up, v_ffn_w_down):
    W = dict(g_mix=g_mix, g_xattn=g_xattn, g_mem=g_mem, g_ffn=g_ffn, g_final=g_final, ev_w_in=ev_w_in,
             ev_a_conv_w=ev_a_conv_w, ev_a_conv_b=ev_a_conv_b, ev_a_ln_g=ev_a_ln_g, ev_a_ln_b=ev_a_ln_b,
             ev_b_conv_w=ev_b_conv_w, ev_b_conv_b=ev_b_conv_b, ev_w_out=ev_w_out, od_w_in=od_w_in,
             od_c_ln_g=od_c_ln_g, od_c_ln_b=od_c_ln_b, od_w_s=od_w_s, od_b_s=od_b_s, od_w_out=od_w_out,
             xa_w_q=xa_w_q, xa_w_k=xa_w_k, xa_w_v=xa_w_v, xa_w_o=xa_w_o, ffn_w_gate=ffn_w_gate,
             ffn_w_up=ffn_w_up, ffn_w_down=ffn_w_down)
    M = dict(g_mix=m_g_mix, g_xattn=m_g_xattn, g_mem=m_g_mem, g_ffn=m_g_ffn, g_final=m_g_final, ev_w_in=m_ev_w_in,
             ev_a_conv_w=m_ev_a_conv_w, ev_a_conv_b=m_ev_a_conv_b, ev_a_ln_g=m_ev_a_ln_g, ev_a_ln_b=m_ev_a_ln_b,
             ev_b_conv_w=m_ev_b_conv_w, ev_b_conv_b=m_ev_b_conv_b, ev_w_out=m_ev_w_out, od_w_in=m_od_w_in,
             od_c_ln_g=m_od_c_ln_g, od_c_ln_b=m_od_c_ln_b, od_w_s=m_od_w_s, od_b_s=m_od_b_s, od_w_out=m_od_w_out,
             xa_w_q=m_xa_w_q, xa_w_k=m_xa_w_k, xa_w_v=m_xa_w_v, xa_w_o=m_xa_w_o, ffn_w_gate=m_ffn_w_gate,
             ffn_w_up=m_ffn_w_up, ffn_w_down=m_ffn_w_down)
    V = dict(g_mix=v_g_mix, g_xattn=v_g_xattn, g_mem=v_g_mem, g_ffn=v_g_ffn, g_final=v_g_final, ev_w_in=v_ev_w_in,
             ev_a_conv_w=v_ev_a_conv_w, ev_a_conv_b=v_ev_a_conv_b, ev_a_ln_g=v_ev_a_ln_g, ev_a_ln_b=v_ev_a_ln_b,
             ev_b_conv_w=v_ev_b_conv_w, ev_b_conv_b=v_ev_b_conv_b, ev_w_out=v_ev_w_out, od_w_in=v_od_w_in,
             od_c_ln_g=v_od_c_ln_g, od_c_ln_b=v_od_c_ln_b, od_w_s=v_od_w_s, od_b_s=v_od_b_s, od_w_out=v_od_w_out,
             xa_w_q=v_xa_w_q, xa_w_k=v_xa_w_k, xa_w_v=v_xa_w_v, xa_w_o=v_xa_w_o, ffn_w_gate=v_ffn_w_gate,
             ffn_w_up=v_ffn_w_up, ffn_w_down=v_ffn_w_down)
    order = list(W)
    cx, cy, cc = _pos()
    chip = (2 * cx + cy).astype(jnp.int32)
    cc = cc.astype(jnp.int32)
    chip_idx = jnp.reshape(chip, (1,))
    c_idx = jnp.reshape(cc, (1,))

    tr = lambda a: jnp.swapaxes(a, 1, 2)
    P = {}
    cev, P["cod"], gs = _pack_cols(chip_idx, od_w_in, ev_w_in, ev_a_conv_w, ev_b_conv_w, od_c_ln_g, od_c_ln_b)
    ((P["evo"], P["odo"], P["dn0"], P["dn1"], P["wg0"], P["wg1"], P["wu0"], P["wu1"], P["xa0"], P["xa1"]),
     (P["cev"], P["gs"])) = _pack_rows(chip_idx, ffn_w_down, tr(ffn_w_gate), tr(ffn_w_up), xa_w_q, xa_w_o,
                                       xa_w_k, xa_w_v, ev_w_out, od_w_out, first=[cev, gs])

    loss, dx, parts, gots = _step(x[0], mem[0], loss_target[0], W, P, c_idx, TM_FWD, TM_BWD, TM_WGRAD)
    loss = lax.psum(loss[0, 0], ("x", "y", "c"))

    chip_c = jnp.stack([chip, cc])
    zero_c = jnp.stack([jnp.zeros((), jnp.int32), cc])
    names = list(parts)
    halves = [_sum_chips(parts[n], gots[n], zero_c if n in ("ga", "gb") else chip_c, f"rs_sum_{n}") for n in names]
    red = dict(zip(names, _sibling_join(halves)))

    res = {}
    xa = [red["xa0"], red["xa1"]]
    big = {
        "ffn_w_down": ([red["dn0"], red["dn1"]], (FSH, D), lambda i: (0, 0), FSH),
        "ffn_w_gate": ([red["wg0"], red["wg1"]], (FSH, D), lambda i: (0, 0), FSH),
        "ffn_w_up": ([red["wu0"], red["wu1"]], (FSH, D), lambda i: (0, 0), FSH),
        "xa_w_q": (xa, (256, D), lambda i: (XQ, 0), 256),
        "xa_w_o": (xa, (256, D), lambda i: (XO, 0), 256),
        "xa_w_k": (xa, (256, D), lambda i: (XK, 0), 256),
        "xa_w_v": (xa, (256, D), lambda i: (XV, 0), 256),
        "ev_w_out": ([red["evo"]], (256, D), lambda i: (0, 0), 256),
        "od_w_out": ([red["odo"]], (256, D), lambda i: (0, 0), 256),
        "od_w_in": ([red["cod"]], (256, 512), lambda i: (i, 0), 256),
        "ev_w_in": ([red["cev"]], (256, 640), lambda i: (i, 0), 256),
    }
    for name, (srcs, gblock, gidx, rb) in big.items():
        if name in ("ffn_w_gate", "ffn_w_up"):
            outs = _adamw_big(tr(W[name]), tr(M[name]), tr(V[name]), srcs, gblock, gidx, f"adamw_{name}", rb)
            res[name] = tuple(tr(o) for o in outs)
        else:
            res[name] = _adamw_big(W[name], M[name], V[name], srcs, gblock, gidx, f"adamw_{name}", rb)
    shape2 = {name: shape for name, shape, *_ in _SMALL}
    wmv = [tuple(d[name].reshape(shape2[name]) for d in (W, M, V)) for name, *_ in _SMALL]
    for (name, *_), outs in zip(_SMALL, _adamw_small(red["ga"], red["gb"], wmv)):
        res[name] = tuple(o.reshape(W[name].shape) for o in outs)

    grad_x = dx[None]
    return (loss, grad_x, *[res[n][0] for n in order], *[res[n][1] for n in order],
            *[res[n][2] for n in order], *[res[n][3] for n in order])
```

```python
import jax
import jax.numpy as jnp
from jax import lax
from jax.experimental import pallas as pl
from jax.experimental.pallas import tpu as pltpu

BF = jnp.bfloat16
F32 = jnp.float32

D = 1024
DA = 512
DZ = 2560
DFF = 2816
NSH = 4
FSH = DFF // NSH
FF_CHUNKS = ((0, 768), (768, 1536), (1536, 2304), (2304, 2816))
NMEM = 256
HEADS = 4
HD = D // HEADS
CHUNK = 128
GROUPS = 8
KA = 31
KB = 3
HALO = 16
RC = 32
RMS_EPS = 1e-6
LN_EPS = 1e-5
ATT_SCALE = HD ** -0.5
TM_FWD = 512
TM_BWD = 256
TM_WGRAD = 1024

ADAM_LR = 0.001
ADAM_B1 = 0.9
ADAM_B2 = 0.999
ADAM_EPS = 1e-08
ADAM_WD = 0.01
ADAM_STEP = 10

XQ, XO, XK, XV = 0, 1, 2, 3
SP_ROWS = 48
SA_ROWS = 64

MESH = pl.DeviceIdType.MESH
ANY = pl.BlockSpec(memory_space=pl.ANY)


def _dot(a, b):
    return jnp.dot(a, b, preferred_element_type=F32)


def _dot_nt(a, b):
    return lax.dot_general(a, b, (((1,), (1,)), ((), ())), preferred_element_type=F32)


def _dot_tn(a, b):
    return lax.dot_general(a, b, (((0,), (0,)), ((), ())), preferred_element_type=F32)


def _rms(h):
    r = lax.rsqrt(jnp.mean(h * h, axis=-1, keepdims=True) + RMS_EPS)
    return h * r, r


def _rms_bwd(dn, hhat, r, g):
    dhh = dn * g
    return r * (dhh - hhat * jnp.mean(dhh * hhat, axis=-1, keepdims=True))


def _ln(x):
    mu = jnp.mean(x, axis=-1, keepdims=True)
    xc = x - mu
    rstd = lax.rsqrt(jnp.mean(xc * xc, axis=-1, keepdims=True) + LN_EPS)
    return xc * rstd, rstd


def _ln_bwd(dy, y, rstd):
    return rstd * (dy - jnp.mean(dy, axis=-1, keepdims=True) - y * jnp.mean(dy * y, axis=-1, keepdims=True))


def _colsum(x):
    return jnp.sum(x, axis=0, keepdims=True)


def _sigmoid(x):
    return 1.0 / (1.0 + jnp.exp(-x))


_GELU_C = 0.7978845608028654
_GELU_A = 0.044715


def _gelu(x):
    t = jnp.tanh(_GELU_C * (x + _GELU_A * x * x * x))
    return 0.5 * x * (1.0 + t)


def _gelu_grad(x):
    t = jnp.tanh(_GELU_C * (x + _GELU_A * x * x * x))
    return 0.5 * (1.0 + t) + 0.5 * x * (1.0 - t * t) * _GELU_C * (1.0 + 3.0 * _GELU_A * x * x)


def _first(i):
    return i == 0


def _last(i):
    return i == pl.num_programs(0) - 1


def _const(shape, idx):
    return pl.BlockSpec(shape, lambda *_, _idx=tuple(idx): _idx, pipeline_mode=pl.Buffered(1))


def _whole(a):
    return _const(a.shape, (0,) * a.ndim)


def _rows(tm, width, col=0):
    return pl.BlockSpec((tm, width), lambda i, _c=col: (i, _c))


def _params(ndim=1, vmem_mb=56):
    return pltpu.CompilerParams(dimension_semantics=("arbitrary",) * ndim, vmem_limit_bytes=vmem_mb << 20)


def _sds(shape, dtype):
    return jax.ShapeDtypeStruct(shape, dtype)


def _pos():
    return lax.axis_index("x"), lax.axis_index("y"), lax.axis_index("c")


def _other_chips(x, y):
    return [(1 - x, y), (x, 1 - y), (1 - x, 1 - y)]


def _half(ref, hc, lead=()):
    r = ref.shape[-2] // 2
    return ref.at[(*lead, pl.ds(pl.multiple_of(hc * r, 8), r), slice(None))]


def _rcopy(src, dst, ssem, rsem, to):
    return pltpu.make_async_remote_copy(src_ref=src, dst_ref=dst, send_sem=ssem, recv_sem=rsem,
                                        device_id=to, device_id_type=MESH)


def _gather_start(bufs, sems):
    ici_s, ici_r, _, _ = sems
    x, y, c = _pos()
    me = 2 * x + y
    for k, (cx, cy) in enumerate(_other_chips(x, y)):
        for t, buf in enumerate(bufs):
            mine = _half(buf, c, (me,))
            _rcopy(mine, mine, ici_s.at[3 * t + k], ici_r.at[3 * t + k], (cx, cy, c)).start()


def _gather_forward(bufs, sems):
    ici_s, ici_r, d2d_s, d2d_r = sems
    x, y, c = _pos()
    sib = (x, y, 1 - c)
    for k, (cx, cy) in enumerate(_other_chips(x, y)):
        for t, buf in enumerate(bufs):
            landed = _half(buf, c, (2 * cx + cy,))
            _rcopy(landed, landed, ici_s.at[3 * t + k], ici_r.at[3 * t + k], (cx, cy, c)).wait_recv()
            _rcopy(landed, landed, d2d_s.at[3 * t + k], d2d_r.at[3 * t + k], sib).start()


def _gather_drain(bufs, sems):
    ici_s, ici_r, d2d_s, d2d_r = sems
    x, y, c = _pos()
    me = 2 * x + y
    sib = (x, y, 1 - c)
    chips = _other_chips(x, y)
    for k, (cx, cy) in enumerate(chips):
        for t, buf in enumerate(bufs):
            got = _half(buf, 1 - c, (2 * cx + cy,))
            _rcopy(got, got, d2d_s.at[3 * t + k], d2d_r.at[3 * t + k], sib).wait_recv()
    for k, (cx, cy) in enumerate(chips):
        for t, buf in enumerate(bufs):
            mine = _half(buf, c, (me,))
            _rcopy(mine, mine, ici_s.at[3 * t + k], ici_r.at[3 * t + k], (cx, cy, c)).wait_send()
            landed = _half(buf, c, (2 * cx + cy,))
            _rcopy(landed, landed, d2d_s.at[3 * t + k], d2d_r.at[3 * t + k], sib).wait_send()


def _xchg_copies(srcs, dsts, nbig, sems):
    ssem, rsem = sems
    x, y, c = _pos()
    cps = []
    for k, (cx, cy) in enumerate(_other_chips(x, y)):
        for t, (src, dst) in enumerate(zip(srcs, dsts)):
            s = src.at[2 * cx + cy] if t < nbig else src
            cps.append(_rcopy(s, dst.at[k], ssem.at[3 * t + k], rsem.at[3 * t + k], (cx, cy, c)))
    return cps


def _gather_sems(n):
    return [pltpu.SemaphoreType.DMA((3 * n,))] * 4


def _xchg_sems(n):
    return [pltpu.SemaphoreType.DMA((3 * n,))] * 2


def _xchg_out_shapes(items):
    return [_sds((3,) + a.shape[-2:], a.dtype) for a in items]


def _call(body, *, name, grid, in_specs, out_specs, out_shape, args, scratch_shapes=(), aliases=None, carry=None):
    in_specs, out_specs, out_shape = list(in_specs), list(out_specs), list(out_shape)
    args = list(args)
    aliases = dict(aliases or {})
    n_in, n_out, n_scr = len(args), len(out_shape), len(scratch_shapes)
    kind = carry[0] if carry else None
    if kind == "gather":
        cin = list(carry[1])
        cout_shape = [_sds(a.shape, a.dtype) for a in cin]
        sems = _gather_sems(len(cin))
        for t in range(len(cin)):
            aliases[n_in + t] = n_out + t
    elif kind == "xchg":
        cin = list(carry[1]) + list(carry[2])
        nbig = len(carry[1])
        cout_shape = _xchg_out_shapes(cin)
        sems = _xchg_sems(len(cin))
    else:
        cin, cout_shape, sems = [], [], []
    nc = len(cin)

    def wrapped(*refs):
        ins = refs[:n_in]
        c_in = refs[n_in:n_in + nc]
        outs = refs[n_in + nc:n_in + nc + n_out]
        c_out = refs[n_in + nc + n_out:n_in + 2 * nc + n_out]
        scr = refs[n_in + 2 * nc + n_out:n_in + 2 * nc + n_out + n_scr]
        sem = refs[n_in + 2 * nc + n_out + n_scr:]
        i = pl.program_id(0)
        if kind:
            @pl.when(_first(i))
            def _():
                if kind == "gather":
                    _gather_start(c_out, sem)
                else:
                    for cp in _xchg_copies(c_in, c_out, nbig, sem):
                        cp.start()
        body(*ins, *outs, *scr)
        if kind == "gather":
            @pl.when(i == max(grid[0] - 3, 0))
            def _():
                _gather_forward(c_out, sem)

            @pl.when(_last(i))
            def _():
                _gather_drain(c_out, sem)
        elif kind == "xchg":
            @pl.when(_last(i))
            def _():
                for cp in _xchg_copies(c_in, c_out, nbig, sem):
                    cp.wait()

    outs = pl.pallas_call(
        wrapped, name=name, grid=grid,
        in_specs=in_specs + [ANY] * nc, out_specs=out_specs + [ANY] * nc,
        out_shape=out_shape + cout_shape,
        scratch_shapes=list(scratch_shapes) + sems,
        input_output_aliases=aliases,
        compiler_params=_params(len(grid)),
    )(*args, *cin)
    return list(outs)


def _sibling_swap(items, name):
    n = len(items)

    def body(*refs):
        src = refs[:n]
        dst = refs[n:2 * n]
        ssem, rsem = refs[2 * n:]
        x, y, c = _pos()
        sib = (x, y, 1 - c)
        cps = []
        for t in range(n):
            lead = (slice(None),) * (len(src[t].shape) - 2)
            cp = _rcopy(_half(src[t], 1 - c, lead), dst[t], ssem.at[t], rsem.at[t], sib)
            cp.start()
            cps.append(cp)
        for cp in cps:
            cp.wait()

    return pl.pallas_call(
        body, name=name, in_specs=[ANY] * n, out_specs=[ANY] * n,
        out_shape=[_sds(a.shape[:-2] + (a.shape[-2] // 2, a.shape[-1]), a.dtype) for a in items],
        scratch_shapes=[pltpu.SemaphoreType.DMA((n,)), pltpu.SemaphoreType.DMA((n,))],
    )(*items)


def _sibling_join(items, name, xchg=((), ())):
    n = len(items)
    xs = list(xchg[0]) + list(xchg[1])
    nx = len(xs)

    def body(*refs):
        x_in = refs[n:n + nx]
        dst = refs[n + nx:2 * n + nx]
        x_out = refs[2 * n + nx:2 * n + 2 * nx]
        ssem, rsem = refs[2 * n + 2 * nx:2 * n + 2 * nx + 2]
        x, y, c = _pos()
        sib = (x, y, 1 - c)
        far = _xchg_copies(x_in, x_out, len(xchg[0]), refs[2 * n + 2 * nx + 2:]) if nx else []
        for cp in far:
            cp.start()
        cps = []
        for t in range(n):
            mine = _half(dst[t], c)
            cp = _rcopy(mine, mine, ssem.at[t], rsem.at[t], sib)
            cp.start()
            cps.append(cp)
        for t, cp in enumerate(cps):
            theirs = _half(dst[t], 1 - c)
            _rcopy(theirs, theirs, ssem.at[t], rsem.at[t], sib).wait_recv()
            cp.wait_send()
        for cp in far:
            cp.wait()

    outs = pl.pallas_call(
        body, name=name, in_specs=[ANY] * (n + nx), out_specs=[ANY] * (n + nx),
        out_shape=[_sds(a.shape, a.dtype) for a in items] + _xchg_out_shapes(xs),
        input_output_aliases={t: t for t in range(n)},
        scratch_shapes=[pltpu.SemaphoreType.DMA((n,))] * 2 + (_xchg_sems(nx) if nx else []),
    )(*items, *xs)
    return list(outs[:n]), list(outs[n:])


def _ev_in_fwd(h, g, wev, tm, carry=None):
    s = h.shape[0]

    def body(h_ref, g_ref, w_ref, z_ref):
        hhat, _ = _rms(h_ref[...])
        n = (hhat * g_ref[...]).astype(BF)
        z_ref[...] = _dot(n, w_ref[...]).astype(BF)

    return _call(
        body, name="ev_in_fwd", grid=(s // tm,),
        in_specs=[_rows(tm, D), _const((1, D), (0, 0)), _const((D, DZ), (0, 0))],
        out_specs=[_rows(tm, DZ)], out_shape=[_sds((s, DZ), BF)],
        args=(h, g, wev), carry=carry)


def _halo_specs(tm, width, s, col=0):
    per = tm // HALO
    nh = s // HALO
    prev = pl.BlockSpec((HALO, width), lambda i, _c=col: (jnp.maximum(i * per - 1, 0), _c))
    nxt = pl.BlockSpec((HALO, width), lambda i, _c=col: (jnp.minimum((i + 1) * per, nh - 1), _c))
    return prev, nxt


def _fill_pad(pad_ref, prev, main, nxt, i, tm):
    pad_ref[pl.ds(0, HALO), :] = jnp.where(_first(i), 0.0, prev)
    pad_ref[pl.ds(HALO, tm), :] = main
    pad_ref[pl.ds(HALO + tm, HALO), :] = jnp.where(_last(i), 0.0, nxt)


def _shift8(xs_ref, pad_ref, tm):
    for j in range(8):
        xs_ref[j] = pad_ref[pl.ds(j, tm + 2 * HALO - 8), :]


def _tap(xs_ref, r, off):
    return xs_ref[off % 8, pl.ds(r + 8 * (off // 8), RC), :]


def _glu_a(z):
    return z[:, :DA] * _sigmoid(z[:, DA:2 * DA])


def _gate_c(z):
    return z[:, 2048:] * z[:, 1024:1536]


def _ev_mix_fwd(h, z, wa, ba, lng, lnb, wb, bb, evo, tm, carry=None):
    s = h.shape[0]

    def body(h_ref, z_ref, zp_ref, zn_ref, wa_ref, ba_ref, lng_ref, lnb_ref, wb_ref, bb_ref, wo_ref,
             h1_ref, a2_ref, c2_ref, cat_ref, apad, cpad, c2f, xs):
        i = pl.program_id(0)
        zm = z_ref[...].astype(F32)
        zp = zp_ref[...].astype(F32)
        zn = zn_ref[...].astype(F32)
        _fill_pad(apad, _glu_a(zp), _glu_a(zm), _glu_a(zn), i, tm)
        _fill_pad(cpad, _gate_c(zp), _gate_c(zm), _gate_c(zn), i, tm)
        _shift8(xs, apad, tm)
        for r in range(0, tm, RC):
            acc = jnp.zeros((RC, DA), F32) + ba_ref[...]
            for k in range(KA):
                acc = acc + _tap(xs, r, k + 1) * wa_ref[pl.ds(k, 1), :]
            a2_ref[pl.ds(r, RC), :] = acc
            acc = jnp.zeros((RC, DA), F32) + bb_ref[...]
            for k in range(KB):
                acc = acc + cpad[pl.ds(r + HALO - 1 + k, RC), :] * wb_ref[pl.ds(k, 1), :]
            c2f[pl.ds(r, RC), :] = acc
        y, _ = _ln(a2_ref[...])
        a3 = y * lng_ref[...] + lnb_ref[...]
        cat_ref[:, :DA] = (a3 * _sigmoid(a3)).astype(BF)
        c2 = c2f[...]
        c2_ref[...] = c2.astype(BF)
        cat_ref[:, DA:] = (zm[:, 1536:2048] * c2).astype(BF)
        h1_ref[...] = h_ref[...] + _dot(cat_ref[...], wo_ref[...].reshape(D, D))

    zp_spec, zn_spec = _halo_specs(tm, DZ, s)
    small = lambda rows: _const((rows, DA), (0, 0))
    return _call(
        body, name="ev_mix_fwd", grid=(s // tm,),
        in_specs=[_rows(tm, D), _rows(tm, DZ), zp_spec, zn_spec, small(32), small(1), small(1), small(1),
                  small(8), small(1), _whole(evo)],
        out_specs=[_rows(tm, D), _rows(tm, DA), _rows(tm, DA), _rows(tm, D)],
        out_shape=[_sds((s, D), F32), _sds((s, DA), F32), _sds((s, DA), BF), _sds((s, D), BF)],
        scratch_shapes=[pltpu.VMEM((tm + 2 * HALO, DA), F32), pltpu.VMEM((tm + 2 * HALO, DA), F32),
                        pltpu.VMEM((tm, DA), F32), pltpu.VMEM((8, tm + 2 * HALO - 8, DA), F32)],
        args=(h, z, z, z, wa, ba, lng, lnb, wb, bb, evo), carry=carry)


def _ev_mix_bwd(dh, cat, a2, c2, z, lng, lnb, evo, tm, carry=None):
    s = dh.shape[0]

    def body(dh_ref, cat_ref, a2_ref, c2_ref, zgb_ref, lng_ref, lnb_ref, wo_ref,
             da2_ref, dc2_ref, dgb_ref, dlng_ref, dlnb_ref, gwo_ref, wacc):
        i = pl.program_id(0)

        @pl.when(_first(i))
        def _():
            wacc[...] = jnp.zeros_like(wacc)
            dlng_ref[...] = jnp.zeros_like(dlng_ref)
            dlnb_ref[...] = jnp.zeros_like(dlnb_ref)

        dhb = dh_ref[...].astype(BF)
        dcat = _dot_nt(dhb, wo_ref[...].reshape(D, D))
        wacc[...] += _dot_tn(cat_ref[...], dhb)
        y, rstd = _ln(a2_ref[...])
        a3 = y * lng_ref[...] + lnb_ref[...]
        sg = _sigmoid(a3)
        da3 = dcat[:, :DA] * (sg * (1.0 + a3 * (1.0 - sg)))
        dlng_ref[...] += _colsum(da3 * y)
        dlnb_ref[...] += _colsum(da3)
        da2_ref[...] = _ln_bwd(da3 * lng_ref[...], y, rstd).astype(BF)
        db = dcat[:, DA:]
        dc2_ref[...] = (db * zgb_ref[...].astype(F32)).astype(BF)
        dgb_ref[...] = (db * c2_ref[...].astype(F32)).astype(BF)

        @pl.when(_last(i))
        def _():
            gwo_ref[...] = wacc[...].astype(BF).reshape(NSH, 256, D)

    small = _const((1, DA), (0, 0))
    return _call(
        body, name="ev_mix_bwd", grid=(s // tm,),
        in_specs=[_rows(tm, D), _rows(tm, D), _rows(tm, DA), _rows(tm, DA), _rows(tm, DA, 3), small, small,
                  _whole(evo)],
        out_specs=[_rows(tm, DA), _rows(tm, DA), _rows(tm, DA), small, small, _whole(evo)],
        out_shape=[_sds((s, DA), BF), _sds((s, DA), BF), _sds((s, DA), BF), _sds((1, DA), F32),
                   _sds((1, DA), F32), _sds(evo.shape, BF)],
        scratch_shapes=[pltpu.VMEM((D, D), F32)],
        args=(dh, cat, a2, c2, z, lng, lnb, evo), carry=carry)


def _ev_in_bwd(h, dh, z, da2, dc2, dgb, g, wa, wb, wev, tm):
    s = h.shape[0]

    def body(h_ref, dh_ref, z_ref, zp_ref, zn_ref, da_ref, dap_ref, dan_ref, dc_ref, dcp_ref, dcn_ref,
             dgb_ref, g_ref, wa_ref, wb_ref, w_ref,
             dh0_ref, dwa_ref, dba_ref, dwb_ref, dbb_ref, dg_ref, gw_ref,
             apad, cpad, dapad, dcpad, dz, wacc, dwa_acc, dwb_acc, xs):
        i = pl.program_id(0)

        @pl.when(_first(i))
        def _():
            wacc[...] = jnp.zeros_like(wacc)
            dwa_acc[...] = jnp.zeros_like(dwa_acc)
            dwb_acc[...] = jnp.zeros_like(dwb_acc)
            dba_ref[...] = jnp.zeros_like(dba_ref)
            dbb_ref[...] = jnp.zeros_like(dbb_ref)
            dg_ref[...] = jnp.zeros_like(dg_ref)

        zm = z_ref[...].astype(F32)
        zp = zp_ref[...].astype(F32)
        zn = zn_ref[...].astype(F32)
        _fill_pad(apad, _glu_a(zp), _glu_a(zm), _glu_a(zn), i, tm)
        _fill_pad(cpad, _gate_c(zp), _gate_c(zm), _gate_c(zn), i, tm)
        da2 = da_ref[...].astype(F32)
        dc2 = dc_ref[...].astype(F32)
        _fill_pad(dapad, dap_ref[...].astype(F32), da2, dan_ref[...].astype(F32), i, tm)
        _fill_pad(dcpad, dcp_ref[...].astype(F32), dc2, dcn_ref[...].astype(F32), i, tm)
        dba_ref[...] += _colsum(da2)
        dbb_ref[...] += _colsum(dc2)

        sg = _sigmoid(zm[:, DA:2 * DA])
        aval = zm[:, :DA]
        _shift8(xs, dapad, tm)
        for r in range(0, tm, RC):
            acc = jnp.zeros((RC, DA), F32)
            for k in range(KA):
                acc = acc + _tap(xs, r, 2 * HALO - 1 - k) * wa_ref[pl.ds(k, 1), :]
            sgr = sg[r:r + RC]
            dz[pl.ds(r, RC), 0:DA] = (acc * sgr).astype(BF)
            dz[pl.ds(r, RC), DA:2 * DA] = (acc * aval[r:r + RC] * sgr * (1.0 - sgr)).astype(BF)
            acc = jnp.zeros((RC, DA), F32)
            for k in range(KB):
                acc = acc + dcpad[pl.ds(r + HALO + 1 - k, RC), :] * wb_ref[pl.ds(k, 1), :]
            dz[pl.ds(r, RC), 1024:1536] = (acc * zm[r:r + RC, 2048:]).astype(BF)
            dz[pl.ds(r, RC), 2048:2560] = (acc * zm[r:r + RC, 1024:1536]).astype(BF)
        dz[:, 1536:2048] = dgb_ref[...]

        _shift8(xs, apad, tm)
        for k in range(KA):
            acc = jnp.zeros((8, DA), F32)
            for r in range(0, tm, RC):
                p = _tap(xs, r, k + 1) * dapad[pl.ds(r + HALO, RC), :]
                acc = acc + p.reshape(RC // 8, 8, DA).sum(axis=0)
            dwa_acc[k] += acc
        for k in range(KB):
            acc = jnp.zeros((8, DA), F32)
            for r in range(0, tm, RC):
                p = cpad[pl.ds(r + HALO - 1 + k, RC), :] * dcpad[pl.ds(r + HALO, RC), :]
                acc = acc + p.reshape(RC // 8, 8, DA).sum(axis=0)
            dwb_acc[k] += acc

        gg = g_ref[...]
        hhat, rr = _rms(h_ref[...])
        n = (hhat * gg).astype(BF)
        dzb = dz[...]
        wacc[...] += _dot_tn(n, dzb)
        dn = _dot_nt(dzb, w_ref[...])
        dg_ref[...] += _colsum(dn * hhat)
        dh0_ref[...] = dh_ref[...] + _rms_bwd(dn, hhat, rr, gg)

        @pl.when(_last(i))
        def _():
            for k in range(NSH):
                gw_ref[k] = wacc[:, 640 * k:640 * (k + 1)].astype(BF)
            dwa_ref[...] = jnp.sum(dwa_acc[...], axis=1)
            dwb_ref[...] = jnp.sum(dwb_acc[...], axis=1)

    zp_spec, zn_spec = _halo_specs(tm, DZ, s)
    hp_spec, hn_spec = _halo_specs(tm, DA, s)
    small = lambda rows: _const((rows, DA), (0, 0))
    return _call(
        body, name="ev_in_bwd", grid=(s // tm,),
        in_specs=[_rows(tm, D), _rows(tm, D), _rows(tm, DZ), zp_spec, zn_spec,
                  _rows(tm, DA), hp_spec, hn_spec, _rows(tm, DA), hp_spec, hn_spec, _rows(tm, DA),
                  _const((1, D), (0, 0)), small(32), small(8), _const((D, DZ), (0, 0))],
        out_specs=[_rows(tm, D), small(32), small(1), small(8), small(1), _const((1, D), (0, 0)),
                   _const((NSH, D, 640), (0, 0, 0))],
        out_shape=[_sds((s, D), F32), _sds((32, DA), F32), _sds((1, DA), F32), _sds((8, DA), F32),
                   _sds((1, DA), F32), _sds((1, D), F32), _sds((NSH, D, 640), BF)],
        scratch_shapes=[pltpu.VMEM((tm + 2 * HALO, DA), F32)] * 4 + [
            pltpu.VMEM((tm, DZ), BF), pltpu.VMEM((D, DZ), F32),
            pltpu.VMEM((32, 8, DA), F32), pltpu.VMEM((8, 8, DA), F32),
            pltpu.VMEM((8, tm + 2 * HALO - 8, DA), F32)],
        args=(h, dh, z, z, z, da2, da2, da2, dc2, dc2, dc2, dgb, g, wa, wb, wev))


def _xa_spec(item):
    return _const((NSH, 256, D), (0, item, 0))


def _kv_proj(mem, g, xa, layer):
    def body(mem_ref, g_ref, wk_ref, wv_ref, k_ref, v_ref):
        mhat, _ = _rms(mem_ref[...])
        mn = (mhat * g_ref[...]).astype(BF)
        k_ref[...] = _dot(mn, wk_ref[...].reshape(D, D)).astype(BF)
        v_ref[...] = _dot(mn, wv_ref[...].reshape(D, D)).astype(BF)

    full = _const((NMEM, D), (0, 0))
    return _call(
        body, name=f"kv_proj{layer}", grid=(1,),
        in_specs=[full, _const((1, D), (0, 0)), _xa_spec(XK), _xa_spec(XV)],
        out_specs=[full, full], out_shape=[_sds((NMEM, D), BF), _sds((NMEM, D), BF)],
        args=(mem, g, xa, xa))


def _softmax_rows(sc):
    m = jnp.max(sc, axis=-1, keepdims=True)
    e = jnp.exp(sc - m)
    return e / jnp.sum(e, axis=-1, keepdims=True)


def _xattn_fwd(h, g, k, v, xa, layer, tm, carry=None):
    s = h.shape[0]

    def body(h_ref, g_ref, k_ref, v_ref, wq_ref, wo_ref, ho_ref, q_ref, o_scr):
        hh = h_ref[...]
        hhat, _ = _rms(hh)
        n = (hhat * g_ref[...]).astype(BF)
        q = _dot(n, wq_ref[...].reshape(D, D)).astype(BF)
        q_ref[...] = q
        for hd in range(HEADS):
            cs = slice(HD * hd, HD * (hd + 1))
            p = _softmax_rows(_dot_nt(q[:, cs], k_ref[:, cs]) * ATT_SCALE)
            o_scr[:, cs] = _dot(p.astype(BF), v_ref[:, cs]).astype(BF)
        ho_ref[...] = hh + _dot(o_scr[...], wo_ref[...].reshape(D, D))

    kvs = _const((NMEM, D), (0, 0))
    return _call(
        body, name=f"xattn_fwd{layer}", grid=(s // tm,),
        in_specs=[_rows(tm, D), _const((1, D), (0, 0)), kvs, kvs, _xa_spec(XQ), _xa_spec(XO)],
        out_specs=[_rows(tm, D), _rows(tm, D)],
        out_shape=[_sds((s, D), F32), _sds((s, D), BF)],
        scratch_shapes=[pltpu.VMEM((tm, D), BF)],
        args=(h, g, k, v, xa, xa), carry=carry)


def _xattn_bwd(h, dh, q, g, k, v, xa, layer, tm, carry=None):
    s = h.shape[0]

    def body(h_ref, dh_ref, q_ref, g_ref, k_ref, v_ref, wq_ref, wo_ref,
             dhi_ref, dk_ref, dv_ref, dg_ref, gqo_ref, o_scr, dq_scr, qacc, oacc):
        i = pl.program_id(0)

        @pl.when(_first(i))
        def _():
            qacc[...] = jnp.zeros_like(qacc)
            oacc[...] = jnp.zeros_like(oacc)
            dk_ref[...] = jnp.zeros_like(dk_ref)
            dv_ref[...] = jnp.zeros_like(dv_ref)
            dg_ref[...] = jnp.zeros_like(dg_ref)

        gg = g_ref[...]
        dho = dh_ref[...]
        dhb = dho.astype(BF)
        q = q_ref[...]
        do = _dot_nt(dhb, wo_ref[...].reshape(D, D)).astype(BF)
        for hd in range(HEADS):
            cs = slice(HD * hd, HD * (hd + 1))
            kh = k_ref[:, cs]
            vh = v_ref[:, cs]
            p = _softmax_rows(_dot_nt(q[:, cs], kh) * ATT_SCALE)
            pb = p.astype(BF)
            o_scr[:, cs] = _dot(pb, vh).astype(BF)
            doh = do[:, cs]
            dp = _dot_nt(doh, vh)
            dv_ref[:, cs] += _dot_tn(pb, doh)
            ds = (p * (dp - jnp.sum(dp * p, axis=-1, keepdims=True)) * ATT_SCALE).astype(BF)
            dq_scr[:, cs] = _dot(ds, kh).astype(BF)
            dk_ref[:, cs] += _dot_tn(ds, q[:, cs])
        oacc[...] += _dot_tn(o_scr[...], dhb)
        hhat, rr = _rms(h_ref[...])
        n = (hhat * gg).astype(BF)
        dq = dq_scr[...]
        qacc[...] += _dot_tn(n, dq)
        dn = _dot_nt(dq, wq_ref[...].reshape(D, D))
        dg_ref[...] += _colsum(dn * hhat)
        dhi_ref[...] = dho + _rms_bwd(dn, hhat, rr, gg)

        @pl.when(_last(i))
        def _():
            gqo_ref[:, 0:256, :] = qacc[...].astype(BF).reshape(NSH, 256, D)
            gqo_ref[:, 256:512, :] = oacc[...].astype(BF).reshape(NSH, 256, D)

    kvs = _const((NMEM, D), (0, 0))
    return _call(
        body, name=f"xattn_bwd{layer}", grid=(s // tm,),
        in_specs=[_rows(tm, D), _rows(tm, D), _rows(tm, D), _const((1, D), (0, 0)), kvs, kvs,
                  _xa_spec(XQ), _xa_spec(XO)],
        out_specs=[_rows(tm, D), kvs, kvs, _const((1, D), (0, 0)), _const((NSH, 512, D), (0, 0, 0))],
        out_shape=[_sds((s, D), F32), _sds((NMEM, D), F32), _sds((NMEM, D), F32), _sds((1, D), F32),
                   _sds(xa.shape, BF)],
        scratch_shapes=[pltpu.VMEM((tm, D), BF), pltpu.VMEM((tm, D), BF), pltpu.VMEM((D, D), F32),
                        pltpu.VMEM((D, D), F32)],
        args=(h, dh, q, g, k, v, xa, xa), carry=carry)


def _kv_proj_bwd(mem, g, dk, dv, xa, gxa, layer):
    def body(mem_ref, g_ref, dk_ref, dv_ref, wk_ref, wv_ref, gxa_in, dg_ref, gkv_ref):
        mhat, _ = _rms(mem_ref[...])
        mn = (mhat * g_ref[...]).astype(BF)
        dkb = dk_ref[...].astype(BF)
        dvb = dv_ref[...].astype(BF)
        gkv_ref[:, 0:256, :] = _dot_tn(mn, dkb).astype(BF).reshape(NSH, 256, D)
        gkv_ref[:, 256:512, :] = _dot_tn(mn, dvb).astype(BF).reshape(NSH, 256, D)
        dmn = _dot_nt(dkb, wk_ref[...].reshape(D, D)) + _dot_nt(dvb, wv_ref[...].reshape(D, D))
        dg_ref[...] = _colsum(dmn * mhat)

    full = _const((NMEM, D), (0, 0))
    return _call(
        body, name=f"kv_proj_bwd{layer}", grid=(1,),
        in_specs=[full, _const((1, D), (0, 0)), full, full, _xa_spec(XK), _xa_spec(XV), ANY],
        out_specs=[_const((1, D), (0, 0)), _const((NSH, 512, D), (0, 1, 0))],
        out_shape=[_sds((1, D), F32), _sds(gxa.shape, BF)],
        aliases={6: 1},
        args=(mem, g, dk, dv, xa, xa, gxa))


def _ffn_fwd(h, g, wg, wu, dn, layer, tm, carry=None):
    s = h.shape[0]
    wg, wu, dn = (a.reshape(DFF, D) for a in (wg, wu, dn))

    def body(h_ref, g_ref, wg_ref, wu_ref, wd_ref, ho_ref, gt_ref, up_ref):
        hh = h_ref[...]
        hhat, _ = _rms(hh)
        n = (hhat * g_ref[...]).astype(BF)
        out = hh
        for c0, c1 in FF_CHUNKS:
            gt = _dot_nt(n, wg_ref[c0:c1, :])
            up = _dot_nt(n, wu_ref[c0:c1, :])
            gt_ref[:, c0:c1] = gt.astype(BF)
            up_ref[:, c0:c1] = up.astype(BF)
            act = (gt * _sigmoid(gt) * up).astype(BF)
            out = out + _dot(act, wd_ref[c0:c1, :])
        ho_ref[...] = out

    return _call(
        body, name=f"ffn_fwd{layer}", grid=(s // tm,),
        in_specs=[_rows(tm, D), _const((1, D), (0, 0)), _whole(wg), _whole(wu), _whole(dn)],
        out_specs=[_rows(tm, D), _rows(tm, DFF), _rows(tm, DFF)],
        out_shape=[_sds((s, D), F32), _sds((s, DFF), BF), _sds((s, DFF), BF)],
        args=(h, g, wg, wu, dn), carry=carry)


def _ffn_bwd_dx(h, dh, gt, up, g, wg, wu, dn, layer, tm, carry=None):
    s = h.shape[0]
    wg, wu, dn = (a.reshape(DFF, D) for a in (wg, wu, dn))

    def body(h_ref, dh_ref, gt_ref, up_ref, g_ref, wg_ref, wu_ref, wd_ref,
             dhi_ref, dg_ref, dgt_ref, dup_ref, act_ref, n_ref):
        i = pl.program_id(0)

        @pl.when(_first(i))
        def _():
            dg_ref[...] = jnp.zeros_like(dg_ref)

        gg = g_ref[...]
        dho = dh_ref[...]
        dhb = dho.astype(BF)
        dn_ = jnp.zeros((tm, D), F32)
        for c0, c1 in FF_CHUNKS:
            gtv = gt_ref[:, c0:c1].astype(F32)
            upv = up_ref[:, c0:c1].astype(F32)
            sg = _sigmoid(gtv)
            silu = gtv * sg
            dact = _dot_nt(dhb, wd_ref[c0:c1, :])
            dgt = (dact * upv * (sg * (1.0 + gtv * (1.0 - sg)))).astype(BF)
            dup = (dact * silu).astype(BF)
            dgt_ref[:, c0:c1] = dgt
            dup_ref[:, c0:c1] = dup
            act_ref[:, c0:c1] = (silu * upv).astype(BF)
            dn_ = dn_ + _dot(dgt, wg_ref[c0:c1, :]) + _dot(dup, wu_ref[c0:c1, :])
        hhat, rr = _rms(h_ref[...])
        n_ref[...] = (hhat * gg).astype(BF)
        dg_ref[...] += _colsum(dn_ * hhat)
        dhi_ref[...] = dho + _rms_bwd(dn_, hhat, rr, gg)

    wide = _rows(tm, DFF)
    return _call(
        body, name=f"ffn_bwd_dx{layer}", grid=(s // tm,),
        in_specs=[_rows(tm, D), _rows(tm, D), wide, wide, _const((1, D), (0, 0)),
                  _whole(wg), _whole(wu), _whole(dn)],
        out_specs=[_rows(tm, D), _const((1, D), (0, 0)), wide, wide, wide, _rows(tm, D)],
        out_shape=[_sds((s, D), F32), _sds((1, D), F32), _sds((s, DFF), BF), _sds((s, DFF), BF),
                   _sds((s, DFF), BF), _sds((s, D), BF)],
        args=(h, dh, gt, up, g, wg, wu, dn), carry=carry)


def _ffn_wgrad(a, b, name, tm, slot, into=None):
    s = a.shape[0]
    tm = min(tm, s)

    def body(a_ref, b_ref, *rest):
        o_ref, acc = rest[-2:]
        i = pl.program_id(0)

        @pl.when(_first(i))
        def _():
            acc[...] = jnp.zeros_like(acc)

        acc[...] += _dot_tn(a_ref[...], b_ref[...].astype(BF))

        @pl.when(_last(i))
        def _():
            o_ref[...] = acc[...].astype(BF).reshape(NSH, FSH, D)

    chained = into is not None
    return _call(
        body, name=name, grid=(s // tm,),
        in_specs=[_rows(tm, DFF), _rows(tm, D)] + [ANY] * chained,
        out_specs=[_const((NSH, FSH, D), (0, slot, 0))],
        out_shape=[_sds((NSH, 3 * FSH, D), BF)],
        scratch_shapes=[pltpu.VMEM((DFF, D), F32)],
        aliases={2: 0} if chained else None,
        args=(a, b) + ((into,) if chained else ()))[0]


def _sgu_fwd(h, g, lng, lnb, ws, bsb, wod, odo, tm, carry=None):
    s = h.shape[0]

    def body(h_ref, g_ref, lng_ref, lnb_ref, ws_ref, bsb_ref, wi_ref, wo_ref, ho_ref, zp_ref, y_scr):
        hh = h_ref[...]
        hhat, _ = _rms(hh)
        n = (hhat * g_ref[...]).astype(BF)
        zpre = _dot(n, wi_ref[...])
        zp_ref[...] = zpre.astype(BF)
        z = _gelu(zpre)
        u = z[:, :D]
        y, _ = _ln(z[:, D:])
        vn = (y * lng_ref[...] + lnb_ref[...]).astype(BF)
        for c in range(tm // CHUNK):
            rs = slice(CHUNK * c, CHUNK * (c + 1))
            for gi in range(GROUPS):
                cs = slice(CHUNK * gi, CHUNK * (gi + 1))
                sv = _dot(ws_ref[gi], vn[rs, cs]) + bsb_ref[gi]
                y_scr[rs, cs] = (u[rs, cs] * sv).astype(BF)
        ho_ref[...] = hh + _dot(y_scr[...], wo_ref[...].reshape(D, D))

    vec = _const((1, D), (0, 0))
    sq = _const((GROUPS, CHUNK, CHUNK), (0, 0, 0))
    return _call(
        body, name="sgu_fwd", grid=(s // tm,),
        in_specs=[_rows(tm, D), vec, vec, vec, sq, sq, _const((D, 2 * D), (0, 0)), _whole(odo)],
        out_specs=[_rows(tm, D), _rows(tm, 2 * D)],
        out_shape=[_sds((s, D), F32), _sds((s, 2 * D), BF)],
        scratch_shapes=[pltpu.VMEM((tm, D), BF)],
        args=(h, g, lng, lnb, ws, bsb, wod, odo), carry=carry)


def _sgu_bwd(h, dh, zpre, g, lng, lnb, ws, wst, bsb, wod, odo, tm):
    s = h.shape[0]

    def body(h_ref, dh_ref, zp_ref, g_ref, lng_ref, lnb_ref, ws_ref, wst_ref, bsb_ref, wi_ref, wo_ref,
             dhi_ref, dws_ref, dbs_ref, dlng_ref, dlnb_ref, dg_ref, gwi_ref, gwo_ref,
             y_scr, dz_scr, dvn_scr, iacc, oacc, bacc):
        i = pl.program_id(0)

        @pl.when(_first(i))
        def _():
            for ref in (iacc, oacc, bacc, dws_ref, dlng_ref, dlnb_ref, dg_ref):
                ref[...] = jnp.zeros_like(ref)

        gg = g_ref[...]
        dho = dh_ref[...]
        dhb = dho.astype(BF)
        zpre_v = zp_ref[...].astype(F32)
        z = _gelu(zpre_v)
        u = z[:, :D]
        yl, rstd = _ln(z[:, D:])
        vn = (yl * lng_ref[...] + lnb_ref[...]).astype(BF)
        dy = _dot_nt(dhb, wo_ref[...].reshape(D, D))
        for c in range(tm // CHUNK):
            rs = slice(CHUNK * c, CHUNK * (c + 1))
            for gi in range(GROUPS):
                cs = slice(CHUNK * gi, CHUNK * (gi + 1))
                vb = vn[rs, cs]
                sv = _dot(ws_ref[gi], vb) + bsb_ref[gi]
                ub = u[rs, cs]
                dyb = dy[rs, cs]
                y_scr[rs, cs] = (ub * sv).astype(BF)
                dz_scr[rs, cs] = dyb * sv
                dsv = dyb * ub
                bacc[gi] += dsv
                dsvb = dsv.astype(BF)
                dws_ref[gi] += _dot_nt(dsvb, vb)
                dvn_scr[rs, cs] = _dot(wst_ref[gi], dsvb)
        oacc[...] += _dot_tn(y_scr[...], dhb)
        dvn = dvn_scr[...]
        dlng_ref[...] += _colsum(dvn * yl)
        dlnb_ref[...] += _colsum(dvn)
        dz_scr[:, D:] = _ln_bwd(dvn * lng_ref[...], yl, rstd)
        dzb = (dz_scr[...] * _gelu_grad(zpre_v)).astype(BF)
        hhat, rr = _rms(h_ref[...])
        n = (hhat * gg).astype(BF)
        iacc[...] += _dot_tn(n, dzb)
        dn = _dot_nt(dzb, wi_ref[...])
        dg_ref[...] += _colsum(dn * hhat)
        dhi_ref[...] = dho + _rms_bwd(dn, hhat, rr, gg)

        @pl.when(_last(i))
        def _():
            for k in range(NSH):
                gwi_ref[k] = iacc[:, 512 * k:512 * (k + 1)].astype(BF)
            gwo_ref[...] = oacc[...].astype(BF).reshape(NSH, 256, D)
            dbs_ref[...] = jnp.sum(bacc[...], axis=-1)

    vec = _const((1, D), (0, 0))
    sq = _const((GROUPS, CHUNK, CHUNK), (0, 0, 0))
    return _call(
        body, name="sgu_bwd", grid=(s // tm,),
        in_specs=[_rows(tm, D), _rows(tm, D), _rows(tm, 2 * D), vec, vec, vec, sq, sq, sq,
                  _const((D, 2 * D), (0, 0)), _whole(odo)],
        out_specs=[_rows(tm, D), sq, _const((GROUPS, CHUNK), (0, 0)), vec, vec, vec,
                   _const((NSH, D, 512), (0, 0, 0)), _whole(odo)],
        out_shape=[_sds((s, D), F32), _sds((GROUPS, CHUNK, CHUNK), F32), _sds((GROUPS, CHUNK), F32),
                   _sds((1, D), F32), _sds((1, D), F32), _sds((1, D), F32), _sds((NSH, D, 512), BF),
                   _sds(odo.shape, BF)],
        scratch_shapes=[pltpu.VMEM((tm, D), BF), pltpu.VMEM((tm, 2 * D), F32),
                        pltpu.VMEM((tm, D), F32), pltpu.VMEM((D, 2 * D), F32), pltpu.VMEM((D, D), F32),
                        pltpu.VMEM((GROUPS, CHUNK, CHUNK), F32)],
        args=(h, dh, zpre, g, lng, lnb, ws, wst, bsb, wod, odo))


def _final_loss(h, tgt, g, tm):
    s = h.shape[0]

    def body(h_ref, t_ref, g_ref, dh_ref, loss_ref, dg_ref):
        i = pl.program_id(0)

        @pl.when(_first(i))
        def _():
            loss_ref[...] = jnp.zeros_like(loss_ref)
            dg_ref[...] = jnp.zeros_like(dg_ref)

        gg = g_ref[...]
        hhat, rr = _rms(h_ref[...])
        diff = hhat * gg - t_ref[...]
        loss_ref[...] += jnp.sum(diff * diff) * (0.5 / D)
        dout = diff * (1.0 / D)
        dg_ref[...] += _colsum(dout * hhat)
        dh_ref[...] = _rms_bwd(dout, hhat, rr, gg)

    return _call(
        body, name="final_loss", grid=(s // tm,),
        in_specs=[_rows(tm, D), _rows(tm, D), _const((1, D), (0, 0))],
        out_specs=[_rows(tm, D), _const((1, 128), (0, 0)), _const((1, D), (0, 0))],
        out_shape=[_sds((s, D), F32), _sds((1, 128), F32), _sds((1, D), F32)],
        args=(h, tgt, g))


def _own_plane(shape):
    return pl.BlockSpec((1,) + shape, lambda i, chip_ref: (chip_ref[0], 0, 0), pipeline_mode=pl.Buffered(1))


def _pack_rows(chip_idx, w_down, w_gate_t, w_up_t, xq, xo, xk, xv, ev_out, od_out, first):
    nf = len(first)

    def body(chip_ref, wd_ref, wg_ref, wu_ref, xq_ref, xo_ref, xk_ref, xv_ref, ev_ref, od_ref, *rest):
        (evo_ref, odo_ref, dn0_ref, dn1_ref, wg0_ref, wg1_ref, wu0_ref, wu1_ref, xa0_ref,
         xa1_ref) = rest[nf:nf + 10]
        gathered = rest[nf + 10:2 * nf + 10]
        sems = rest[2 * nf + 10:]
        _gather_start(gathered, sems)
        evo_ref[0] = ev_ref[0].astype(BF)
        odo_ref[0] = od_ref[0].astype(BF)
        for src, outs in ((wd_ref, (dn0_ref, dn1_ref)), (wg_ref, (wg0_ref, wg1_ref)), (wu_ref, (wu0_ref, wu1_ref))):
            for layer, o_ref in enumerate(outs):
                o_ref[0] = src[layer].astype(BF)
        for layer, xa_ref in enumerate((xa0_ref, xa1_ref)):
            for item, src in ((XQ, xq_ref), (XO, xo_ref), (XK, xk_ref), (XV, xv_ref)):
                xa_ref[0, 256 * item:256 * (item + 1), :] = src[layer].astype(BF)
        _gather_forward(gathered, sems)
        _gather_drain(gathered, sems)

    ins = (w_down, w_gate_t, w_up_t, xq, xo, xk, xv, ev_out, od_out)
    shapes = [(256, D), (256, D)] + [(FSH, D)] * 6 + [(4 * 256, D), (4 * 256, D)]
    grid_spec = pltpu.PrefetchScalarGridSpec(
        num_scalar_prefetch=1, grid=(1,), in_specs=[_whole(a) for a in ins] + [ANY] * nf,
        out_specs=[_own_plane(sh) for sh in shapes] + [ANY] * nf,
        scratch_shapes=_gather_sems(nf))
    outs = pl.pallas_call(
        body, name="pack_rows", grid_spec=grid_spec,
        out_shape=[_sds((NSH,) + sh, BF) for sh in shapes] + [_sds(a.shape, a.dtype) for a in first],
        input_output_aliases={1 + len(ins) + t: len(shapes) + t for t in range(nf)},
        compiler_params=_params(),
    )(chip_idx, *ins, *first)
    return outs[:len(shapes)], outs[len(shapes):]


def _pack_cols(chip_idx, od_in, ev_in, conv_a, conv_b, ln_g, ln_b):
    def body(chip_ref, od_ref, ev_ref, ca_ref, cb_ref, lg_ref, lb_ref, cev_ref, cod_ref, sp_ref):
        cev_ref[0] = ev_ref[0].astype(BF)
        cod_ref[0] = od_ref[0].astype(BF)
        sp_ref[...] = jnp.zeros_like(sp_ref)
        sp_ref[0, 0:KA, 0:128] = ca_ref[0]
        sp_ref[0, 32:32 + KB, 0:128] = cb_ref[0]
        sp_ref[0, 40:41, :] = lg_ref[...]
        sp_ref[0, 41:42, :] = lb_ref[...]

    ins = (od_in, ev_in, conv_a, conv_b, ln_g, ln_b)
    shapes = [(D, 640), (D, 512)]
    grid_spec = pltpu.PrefetchScalarGridSpec(
        num_scalar_prefetch=1, grid=(1,), in_specs=[_whole(a) for a in ins],
        out_specs=[_own_plane(sh) for sh in shapes] + [_own_plane((SP_ROWS, 256))])
    return pl.pallas_call(
        body, name="pack_cols", grid_spec=grid_spec,
        out_shape=[_sds((NSH,) + sh, BF) for sh in shapes] + [_sds((NSH, SP_ROWS, 256), F32)],
        compiler_params=_params(),
    )(chip_idx, *ins)


def _assemble(piece, parts, name):
    tr = 256
    cols = piece.shape[2]

    def body(p_ref, *outs):
        for k in range(NSH):
            for (off, width), o_ref in zip(parts, outs):
                o_ref[:, width * k:width * (k + 1)] = p_ref[k, :, off:off + width]

    return _call(
        body, name=name, grid=(D // tr,),
        in_specs=[pl.BlockSpec((NSH, tr, cols), lambda i: (0, i, 0))],
        out_specs=[pl.BlockSpec((tr, NSH * width), lambda i: (i, 0)) for _, width in parts],
        out_shape=[_sds((D, NSH * width), BF) for _, width in parts],
        args=(piece,))


def _blocks_of(rows, cols, itemsize):
    rb = rows
    while rb * cols * itemsize > (2 << 20) and rb % 2 == 0 and (rb // 2) % 16 == 0:
        rb //= 2
    return rb


def _add_half(full, recv, c_idx, name):
    p, r, cols = full.shape
    rb = _blocks_of(r // 2, cols, 4)
    nb = (r // 2) // rb

    def body(c_ref, a_ref, b_ref, o_ref):
        o_ref[...] = (a_ref[...].astype(F32) + b_ref[...].astype(F32)).astype(o_ref.dtype)

    grid_spec = pltpu.PrefetchScalarGridSpec(
        num_scalar_prefetch=1, grid=(p, nb),
        in_specs=[pl.BlockSpec((1, rb, cols), lambda j, i, c_ref: (j, c_ref[0] * nb + i, 0)),
                  pl.BlockSpec((1, rb, cols), lambda j, i, c_ref: (j, i, 0))],
        out_specs=pl.BlockSpec((1, rb, cols), lambda j, i, c_ref: (j, i, 0)))
    return pl.pallas_call(
        body, name=name, grid_spec=grid_spec, out_shape=_sds(recv.shape, full.dtype),
        compiler_params=_params(2),
    )(c_idx, full, recv)


def _sum_chips(own, got, plane_c, name):
    _, r, cols = own.shape
    rb = _blocks_of(r, cols, 4)
    nb = r // rb

    def body(s_ref, a_ref, b_ref, o_ref):
        f = lambda v: v.astype(F32)
        o_ref[...] = (f(a_ref[0]) + f(b_ref[1])) + (f(b_ref[0]) + f(b_ref[2]))

    grid_spec = pltpu.PrefetchScalarGridSpec(
        num_scalar_prefetch=1, grid=(nb,),
        in_specs=[pl.BlockSpec((1, rb, cols), lambda i, s_ref: (s_ref[0], i, 0)),
                  pl.BlockSpec((3, rb, cols), lambda i, s_ref: (0, i, 0))],
        out_specs=pl.BlockSpec((rb, cols), lambda i, s_ref: (s_ref[1] * nb + i, 0)))
    return pl.pallas_call(
        body, name=name, grid_spec=grid_spec, out_shape=_sds((2 * r, cols), F32),
        compiler_params=_params(1),
    )(plane_c, own, got)


def _adamw_math(w, g, m, v):
    m = ADAM_B1 * m + (1.0 - ADAM_B1) * g
    v = ADAM_B2 * v + (1.0 - ADAM_B2) * (g * g)
    m_hat = m / (1.0 - ADAM_B1 ** ADAM_STEP)
    v_hat = v / (1.0 - ADAM_B2 ** ADAM_STEP)
    delta = -ADAM_LR * (m_hat / (jnp.sqrt(v_hat) + ADAM_EPS) + ADAM_WD * w)
    return delta, m, v


def _adamw_big(w, m, v, gsrcs, gblock, gidx, name, rb):
    nl, r, cols = w.shape
    assert len(gsrcs) == nl

    def body(w_ref, m_ref, v_ref, *rest):
        g_refs, (go_ref, d_ref, mo_ref, vo_ref) = rest[:nl], rest[nl:]
        g = g_refs[0][:, 0:cols]
        if nl == 2:
            g = jnp.where(pl.program_id(0) == 0, g, g_refs[1][:, 0:cols])
        delta, mn, vn = _adamw_math(w_ref[0], g, m_ref[0], v_ref[0])
        go_ref[0] = g
        d_ref[0] = delta
        mo_ref[0] = mn
        vo_ref[0] = vn

    wspec = pl.BlockSpec((1, rb, cols), lambda l, i: (l, i, 0))
    gspec = pl.BlockSpec(gblock, lambda l, i: gidx(i))
    return pl.pallas_call(
        body, name=name, grid=(nl, r // rb),
        in_specs=[wspec, wspec, wspec] + [gspec] * nl,
        out_specs=[wspec] * 4, out_shape=[_sds(w.shape, F32)] * 4,
        compiler_params=_params(2),
    )(w, m, v, *gsrcs)


_SMALL = (
    ("g_mix", (2, D), "a", (0, 2), 0, False),
    ("g_xattn", (2, D), "a", (2, 4), 0, False),
    ("g_mem", (2, D), "a", (4, 6), 0, False),
    ("g_ffn", (2, D), "a", (6, 8), 0, False),
    ("g_final", (1, D), "a", (8, 9), 0, False),
    ("ev_a_conv_w", (KA, 128), "a", (16, 16 + KA), 0, True),
    ("ev_a_conv_b", (1, DA), "a", (11, 12), 0, False),
    ("ev_a_ln_g", (1, DA), "a", (11, 12), DA, False),
    ("ev_a_ln_b", (1, DA), "a", (12, 13), 0, False),
    ("ev_b_conv_w", (KB, 128), "a", (48, 48 + KB), 0, True),
    ("ev_b_conv_b", (1, DA), "a", (12, 13), DA, False),
    ("od_c_ln_g", (1, 256), "a", (9, 10), 0, True),
    ("od_c_ln_b", (1, 256), "a", (10, 11), 0, True),
    ("od_w_s", (GROUPS * CHUNK, CHUNK), "b", (0, GROUPS * CHUNK), 0, False),
    ("od_b_s", (GROUPS, CHUNK), "a", (56, 64), 0, False),
)


def _adamw_small(ga, gb, wmv):
    ns = len(_SMALL)

    def body(*refs):
        ga_ref, gb_ref = refs[0], refs[1]
        ins = refs[2:2 + 3 * ns]
        outs = refs[2 + 3 * ns:]
        x, y, _ = _pos()
        chip = 2 * x + y
        for j, (_, shape, pack, (r0, r1), col, sharded) in enumerate(_SMALL):
            src = ga_ref if pack == "a" else gb_ref
            width = shape[1]
            if sharded:
                g = jnp.zeros(shape, F32)
                for k in range(NSH):
                    blk = src[r0:r1, col + width * k:col + width * (k + 1)]
                    g = g + jnp.where(chip == k, blk, 0.0)
            else:
                g = src[r0:r1, col:col + width]
            w_ref, m_ref, v_ref = ins[3 * j:3 * j + 3]
            delta, mn, vn = _adamw_math(w_ref[...], g, m_ref[...], v_ref[...])
            go_ref, d_ref, mo_ref, vo_ref = outs[4 * j:4 * j + 4]
            go_ref[...] = g
            d_ref[...] = delta
            mo_ref[...] = mn
            vo_ref[...] = vn

    vm = pl.BlockSpec(memory_space=pltpu.VMEM)
    flat = [a for trio in wmv for a in trio]
    out_shape = []
    for (_, shape, *_rest) in _SMALL:
        out_shape += [_sds(shape, F32)] * 4
    outs = pl.pallas_call(
        body, name="adamw_small",
        in_specs=[vm] * (2 + 3 * ns), out_specs=[vm] * (4 * ns), out_shape=out_shape,
    )(ga, gb, *flat)
    return [tuple(outs[4 * j:4 * j + 4]) for j in range(ns)]


def _small_params(gs, W):
    unshard = lambda a: jnp.transpose(a, (1, 0, 2)).reshape(a.shape[1], NSH * a.shape[2])
    od_w_s, od_b_s = W["od_w_s"], W["od_b_s"]
    return dict(
        g_mix=W["g_mix"], g_xattn=W["g_xattn"], g_mem=W["g_mem"], g_ffn=W["g_ffn"],
        g_final=W["g_final"].reshape(1, D),
        wa=unshard(gs[:, 0:32, 0:128]), ba=W["ev_a_conv_b"], lng_a=W["ev_a_ln_g"], lnb_a=W["ev_a_ln_b"],
        wb=unshard(gs[:, 32:40, 0:128]), bb=W["ev_b_conv_b"],
        lng_c=unshard(gs[:, 40:41, :]), lnb_c=unshard(gs[:, 41:42, :]),
        ws=od_w_s[0].astype(BF), wst=jnp.swapaxes(od_w_s[0], 1, 2).astype(BF),
        bsb=jnp.broadcast_to(od_b_s[0][:, :, None], (GROUPS, CHUNK, CHUNK)),
    )


def _small_grad_packs(g):
    z = lambda r, c: jnp.zeros((r, c), F32)
    wide = lambda a: jnp.concatenate([a, z(a.shape[0], D - a.shape[1])], axis=1)
    rows = [g["g_mix"], g["g_xattn"], g["g_mem"], g["g_ffn"], g["g_final"], g["od_c_ln_g"], g["od_c_ln_b"],
            jnp.concatenate([g["ev_a_conv_b"], g["ev_a_ln_g"]], axis=1),
            jnp.concatenate([g["ev_a_ln_b"], g["ev_b_conv_b"]], axis=1),
            z(3, D), wide(g["ev_a_conv_w"]), wide(g["ev_b_conv_w"]), wide(g["od_b_s"])]
    a = jnp.concatenate(rows, axis=0)
    assert a.shape == (SA_ROWS, D), a.shape
    return a, g["od_w_s"].reshape(GROUPS * CHUNK, CHUNK)


def _step(x, mem, tgt, W, P, c_idx, tm_f, tm_b, tm_w):
    row = lambda a, i: a[i:i + 1]

    def pre_sum(names, arrays, tag):
        recv = _sibling_swap(arrays, f"rs_swap_{tag}")
        return [_add_half(a, r, c_idx, f"rs_add_{n}") for n, a, r in zip(names, arrays, recv)]

    sm = _small_params(P["gs"], W)
    (wev,) = _assemble(P["cev"], [(0, 640)], "assemble_ev")
    z, evo = _ev_in_fwd(x, row(sm["g_mix"], 0), wev, 2 * tm_f, carry=("gather", [P["evo"]]))
    h1, a2, c2, cat, xa0, wg0, wu0 = _ev_mix_fwd(
        x, z, sm["wa"], sm["ba"], sm["lng_a"], sm["lnb_a"], sm["wb"], sm["bb"], evo, tm_f,
        carry=("gather", [P["xa0"], P["wg0"], P["wu0"]]))
    k0, v0 = _kv_proj(mem, row(sm["g_mem"], 0), xa0, 0)
    h2, q0, dn0, cod = _xattn_fwd(h1, row(sm["g_xattn"], 0), k0, v0, xa0, 0, 2 * tm_f,
                                  carry=("gather", [P["dn0"], P["cod"]]))
    (wod,) = _assemble(cod, [(0, 512)], "assemble_od")
    h3, gt0, up0, odo, xa1, wg1 = _ffn_fwd(h2, row(sm["g_ffn"], 0), wg0, wu0, dn0, 0, tm_f,
                                           carry=("gather", [P["odo"], P["xa1"], P["wg1"]]))
    h4, zpre, wu1, dn1 = _sgu_fwd(h3, row(sm["g_mix"], 1), sm["lng_c"], sm["lnb_c"], sm["ws"], sm["bsb"], wod,
                                  odo, tm_f, carry=("gather", [P["wu1"], P["dn1"]]))
    k1, v1 = _kv_proj(mem, row(sm["g_mem"], 1), xa1, 1)
    h5, q1 = _xattn_fwd(h4, row(sm["g_xattn"], 1), k1, v1, xa1, 1, 2 * tm_f)
    h6, gt1, up1 = _ffn_fwd(h5, row(sm["g_ffn"], 1), wg1, wu1, dn1, 1, tm_f)
    dh6, loss, dg_final = _final_loss(h6, tgt, sm["g_final"], 2 * tm_f)

    g = {"g_final": dg_final}
    parts, gots = {}, {}
    def ffn_wgrads(layer, dgt, dup, act, nb, dh_out):
        g_ffn = _ffn_wgrad(dgt, nb, f"ffn_wgrad_gate{layer}", tm_w, 0)
        g_ffn = _ffn_wgrad(dup, nb, f"ffn_wgrad_up{layer}", tm_w, 1, into=g_ffn)
        g_ffn = _ffn_wgrad(act, dh_out, f"ffn_wgrad_down{layer}", tm_w, 2, into=g_ffn)
        (parts[f"ffn{layer}"],) = pre_sum((f"ffn{layer}",), (g_ffn,), f"ffn{layer}")
        return parts[f"ffn{layer}"]

    dh5, dg_ffn1, dgt, dup, act, nb = _ffn_bwd_dx(h5, dh6, gt1, up1, row(sm["g_ffn"], 1), wg1, wu1, dn1, 1, tm_b)
    grp = ffn_wgrads(1, dgt, dup, act, nb, dh6)
    dh4, dk, dv, dg_xa1, g_xa1, gots["ffn1"] = _xattn_bwd(
        h4, dh5, q1, row(sm["g_xattn"], 1), k1, v1, xa1, 1, tm_f, carry=("xchg", [grp], []))
    dg_mem1, g_xa1 = _kv_proj_bwd(mem, row(sm["g_mem"], 1), dk, dv, xa1, g_xa1, 1)
    dh3, g["od_w_s"], g["od_b_s"], g["od_c_ln_g"], g["od_c_ln_b"], dg_mix1, g_cod, g_odo = _sgu_bwd(
        h3, dh4, zpre, row(sm["g_mix"], 1), sm["lng_c"], sm["lnb_c"], sm["ws"], sm["wst"], sm["bsb"], wod, odo, tm_b)
    parts["xa1"], parts["cod"], parts["odo"] = pre_sum(("xa1", "cod", "odo"), (g_xa1, g_cod, g_odo), "b")
    dh2, dg_ffn0, dgt, dup, act, nb, gots["xa1"], gots["cod"], gots["odo"] = _ffn_bwd_dx(
        h2, dh3, gt0, up0, row(sm["g_ffn"], 0), wg0, wu0, dn0, 0, tm_b,
        carry=("xchg", [parts["xa1"], parts["cod"], parts["odo"]], []))
    grp = ffn_wgrads(0, dgt, dup, act, nb, dh3)
    dh1, dk, dv, dg_xa0, g_xa0, gots["ffn0"] = _xattn_bwd(
        h1, dh2, q0, row(sm["g_xattn"], 0), k0, v0, xa0, 0, tm_f, carry=("xchg", [grp], []))
    dg_mem0, g_xa0 = _kv_proj_bwd(mem, row(sm["g_mem"], 0), dk, dv, xa0, g_xa0, 0)
    (parts["xa0"],) = pre_sum(("xa0",), (g_xa0,), "d")
    da2, dc2, dgb, g["ev_a_ln_g"], g["ev_a_ln_b"], g_evo, gots["xa0"] = _ev_mix_bwd(
        dh1, cat, a2, c2, z, sm["lng_a"], sm["lnb_a"], evo, tm_f, carry=("xchg", [parts["xa0"]], []))
    dx, g["ev_a_conv_w"], g["ev_a_conv_b"], g["ev_b_conv_w"], g["ev_b_conv_b"], dg_mix0, g_cev = _ev_in_bwd(
        x, dh1, z, da2, dc2, dgb, row(sm["g_mix"], 0), sm["wa"], sm["wb"], wev, tm_b)
    g["g_mix"] = jnp.concatenate([dg_mix0, dg_mix1], axis=0)
    g["g_xattn"] = jnp.concatenate([dg_xa0, dg_xa1], axis=0)
    g["g_mem"] = jnp.concatenate([dg_mem0, dg_mem1], axis=0)
    g["g_ffn"] = jnp.concatenate([dg_ffn0, dg_ffn1], axis=0)
    ga, gb = _small_grad_packs(g)
    parts["evo"], parts["cev"], parts["ga"], parts["gb"] = pre_sum(
        ("evo", "cev", "ga", "gb"), (g_evo, g_cev, ga[None], gb[None]), "e")
    return loss, dx, parts, gots


def kernel(x, mem, g_mix, g_xattn, g_mem, g_ffn, g_final, ev_w_in, ev_a_conv_w, ev_a_conv_b, ev_a_ln_g, ev_a_ln_b, ev_b_conv_w, ev_b_conv_b, ev_w_out, od_w_in, od_c_ln_g, od_c_ln_b, od_w_s, od_b_s, od_w_out, xa_w_q, xa_w_k, xa_w_v, xa_w_o, ffn_w_gate, ffn_w_up, ffn_w_down, loss_target, m_g_mix, m_g_xattn, m_g_mem, m_g_ffn, m_g_final, m_ev_w_in, m_ev_a_conv_w, m_ev_a_conv_b, m_ev_a_ln_g, m_ev_a_ln_b, m_ev_b_conv_w, m_ev_b_conv_b, m_ev_w_out, m_od_w_in, m_od_c_ln_g, m_od_c_ln_b, m_od_w_s, m_od_b_s, m_od_w_out, m_xa_w_q, m_xa_w_k, m_xa_w_v, m_xa_w_o, m_ffn_w_gate, m_ffn_w_up, m_ffn_w_down, v_g_mix, v_g_xattn, v_g_mem, v_g_ffn, v_g_final, v_ev_w_in, v_ev_a_conv_w, v_ev_a_conv_b, v_ev_a_ln_g, v_ev_a_ln_b, v_ev_b_conv_w, v_ev_b_conv_b, v_ev_w_out, v_od_w_in, v_od_c_ln_g, v_od_c_ln_b, v_od_w_s, v_od_b_s, v_od_w_out, v_xa_w_q, v_xa_w_k, v_xa_w_v, v_xa_w_o, v_ffn_w_gate, v_ffn_w_up, v_ffn_w_down):
    W = dict(g_mix=g_mix, g_xattn=g_xattn, g_mem=g_mem, g_ffn=g_ffn, g_final=g_final, ev_w_in=ev_w_in,
             ev_a_conv_w=ev_a_conv_w, ev_a_conv_b=ev_a_conv_b, ev_a_ln_g=ev_a_ln_g, ev_a_ln_b=ev_a_ln_b,
             ev_b_conv_w=ev_b_conv_w, ev_b_conv_b=ev_b_conv_b, ev_w_out=ev_w_out, od_w_in=od_w_in,
             od_c_ln_g=od_c_ln_g, od_c_ln_b=od_c_ln_b, od_w_s=od_w_s, od_b_s=od_b_s, od_w_out=od_w_out,
             xa_w_q=xa_w_q, xa_w_k=xa_w_k, xa_w_v=xa_w_v, xa_w_o=xa_w_o, ffn_w_gate=ffn_w_gate,
             ffn_w_up=ffn_w_up, ffn_w_down=ffn_w_down)
    M = dict(g_mix=m_g_mix, g_xattn=m_g_xattn, g_mem=m_g_mem, g_ffn=m_g_ffn, g_final=m_g_final, ev_w_in=m_ev_w_in,
             ev_a_conv_w=m_ev_a_conv_w, ev_a_conv_b=m_ev_a_conv_b, ev_a_ln_g=m_ev_a_ln_g, ev_a_ln_b=m_ev_a_ln_b,
             ev_b_conv_w=m_ev_b_conv_w, ev_b_conv_b=m_ev_b_conv_b, ev_w_out=m_ev_w_out, od_w_in=m_od_w_in,
             od_c_ln_g=m_od_c_ln_g, od_c_ln_b=m_od_c_ln_b, od_w_s=m_od_w_s, od_b_s=m_od_b_s, od_w_out=m_od_w_out,
             xa_w_q=m_xa_w_q, xa_w_k=m_xa_w_k, xa_w_v=m_xa_w_v, xa_w_o=m_xa_w_o, ffn_w_gate=m_ffn_w_gate,
             ffn_w_up=m_ffn_w_up, ffn_w_down=m_ffn_w_down)
    V = dict(g_mix=v_g_mix, g_xattn=v_g_xattn, g_mem=v_g_mem, g_ffn=v_g_ffn, g_final=v_g_final, ev_w_in=v_ev_w_in,
             ev_a_conv_w=v_ev_a_conv_w, ev_a_conv_b=v_ev_a_conv_b, ev_a_ln_g=v_ev_a_ln_g, ev_a_ln_b=v_ev_a_ln_b,
             ev_b_conv_w=v_ev_b_conv_w, ev_b_conv_b=v_ev_b_conv_b, ev_w_out=v_ev_w_out, od_w_in=v_od_w_in,
             od_c_ln_g=v_od_c_ln_g, od_c_ln_b=v_od_c_ln_b, od_w_s=v_od_w_s, od_b_s=v_od_b_s, od_w_out=v_od_w_out,
             xa_w_q=v_xa_w_q, xa_w_k=v_xa_w_k, xa_w_v=v_xa_w_v, xa_w_o=v_xa_w_o, ffn_w_gate=v_ffn_w_gate,
             ffn_w_up=v_ffn_w_up, ffn_w_down=v_ffn_w_down)
    order = list(W)
    cx, cy, cc = _pos()
    chip = (2 * cx + cy).astype(jnp.int32)
    cc = cc.astype(jnp.int32)
    chip_idx = jnp.reshape(chip, (1,))
    c_idx = jnp.reshape(cc, (1,))

    tr = lambda a: jnp.swapaxes(a, 1, 2)
    P = {}
    cev, P["cod"], gs = _pack_cols(chip_idx, od_w_in, ev_w_in, ev_a_conv_w, ev_b_conv_w, od_c_ln_g, od_c_ln_b)
    ((P["evo"], P["odo"], P["dn0"], P["dn1"], P["wg0"], P["wg1"], P["wu0"], P["wu1"], P["xa0"], P["xa1"]),
     (P["cev"], P["gs"])) = _pack_rows(chip_idx, ffn_w_down, tr(ffn_w_gate), tr(ffn_w_up), xa_w_q, xa_w_o,
                                       xa_w_k, xa_w_v, ev_w_out, od_w_out, first=[cev, gs])

    loss, dx, parts, gots = _step(x[0], mem[0], loss_target[0], W, P, c_idx, TM_FWD, TM_BWD, TM_WGRAD)
    loss = lax.psum(loss[0, 0], ("x", "y", "c"))

    chip_c = jnp.stack([chip, cc])
    zero_c = jnp.stack([jnp.zeros((), jnp.int32), cc])
    def sums(names):
        return [_sum_chips(parts[n], gots[n], zero_c if n in ("ga", "gb") else chip_c, f"rs_sum_{n}")
                for n in names]

    last = ("evo", "cev", "ga", "gb")
    early = [n for n in parts if n not in last]
    joined, got_last = _sibling_join(sums(early), "rs_join_xchg", xchg=(
        [parts["evo"], parts["cev"]], [parts["ga"][0], parts["gb"][0]]))
    red = dict(zip(early, joined))
    gots.update(zip(last, got_last))
    joined, _ = _sibling_join(sums(last), "rs_join_last")
    red.update(zip(last, joined))

    res = {}
    xa = [red["xa0"], red["xa1"]]
    ffn = [red["ffn0"], red["ffn1"]]
    big = {
        "ffn_w_gate": (ffn, (FSH, D), lambda i: (0, 0), FSH),
        "ffn_w_up": (ffn, (FSH, D), lambda i: (1, 0), FSH),
        "ffn_w_down": (ffn, (FSH, D), lambda i: (2, 0), FSH),
        "xa_w_q": (xa, (256, D), lambda i: (XQ, 0), 256),
        "xa_w_o": (xa, (256, D), lambda i: (XO, 0), 256),
        "xa_w_k": (xa, (256, D), lambda i: (XK, 0), 256),
        "xa_w_v": (xa, (256, D), lambda i: (XV, 0), 256),
        "ev_w_out": ([red["evo"]], (256, D), lambda i: (0, 0), 256),
        "od_w_out": ([red["odo"]], (256, D), lambda i: (0, 0), 256),
        "od_w_in": ([red["cod"]], (256, 512), lambda i: (i, 0), 256),
        "ev_w_in": ([red["cev"]], (256, 640), lambda i: (i, 0), 256),
    }
    for name, (srcs, gblock, gidx, rb) in big.items():
        t = tr if name in ("ffn_w_gate", "ffn_w_up") else (lambda a: a)
        outs = _adamw_big(t(W[name]), t(M[name]), t(V[name]), srcs, gblock, gidx, f"adamw_{name}", rb)
        res[name] = tuple(t(o) for o in outs)
    shape2 = {name: shape for name, shape, *_ in _SMALL}
    wmv = [tuple(d[name].reshape(shape2[name]) for d in (W, M, V)) for name, *_ in _SMALL]
    for (name, *_), outs in zip(_SMALL, _adamw_small(red["ga"], red["gb"], wmv)):
        res[name] = tuple(o.reshape(W[name].shape) for o in outs)

    grad_x = dx[None]
    return (loss, grad_x, *[res[n][0] for n in order], *[res[n][1] for n in order],
            *[res[n][2] for n in order], *[res[n][3] for n in order])
```

```python
import jax
import jax.numpy as jnp
from jax import lax
from jax.experimental import pallas as pl
from jax.experimental.pallas import tpu as pltpu

BF = jnp.bfloat16
F32 = jnp.float32

D = 1024
DA = 512
DZ = 2560
DFF = 2816
NSH = 4
FSH = DFF // NSH
FF_CHUNKS = ((0, 768), (768, 1536), (1536, 2304), (2304, 2816))
NMEM = 256
HEADS = 4
HD = D // HEADS
CHUNK = 128
GROUPS = 8
KA = 31
KB = 3
HALO = 16
RC = 32
RMS_EPS = 1e-6
LN_EPS = 1e-5
ATT_SCALE = HD ** -0.5
TM_FWD = 512
TM_BWD = 256
TM_WGRAD = 1024

ADAM_LR = 0.001
ADAM_B1 = 0.9
ADAM_B2 = 0.999
ADAM_EPS = 1e-08
ADAM_WD = 0.01
ADAM_STEP = 10

XQ, XO, XK, XV = 0, 1, 2, 3
SP_ROWS = 48
SA_ROWS = 64

MESH = pl.DeviceIdType.MESH
ANY = pl.BlockSpec(memory_space=pl.ANY)


def _dot(a, b):
    return jnp.dot(a, b, preferred_element_type=F32)


def _dot_nt(a, b):
    return lax.dot_general(a, b, (((1,), (1,)), ((), ())), preferred_element_type=F32)


def _dot_tn(a, b):
    return lax.dot_general(a, b, (((0,), (0,)), ((), ())), preferred_element_type=F32)


def _rms(h):
    r = lax.rsqrt(jnp.mean(h * h, axis=-1, keepdims=True) + RMS_EPS)
    return h * r, r


def _rms_bwd(dn, hhat, r, g):
    dhh = dn * g
    return r * (dhh - hhat * jnp.mean(dhh * hhat, axis=-1, keepdims=True))


def _ln(x):
    mu = jnp.mean(x, axis=-1, keepdims=True)
    xc = x - mu
    rstd = lax.rsqrt(jnp.mean(xc * xc, axis=-1, keepdims=True) + LN_EPS)
    return xc * rstd, rstd


def _ln_bwd(dy, y, rstd):
    return rstd * (dy - jnp.mean(dy, axis=-1, keepdims=True) - y * jnp.mean(dy * y, axis=-1, keepdims=True))


def _colsum(x):
    return jnp.sum(x, axis=0, keepdims=True)


def _sigmoid(x):
    return 1.0 / (1.0 + jnp.exp(-x))


_GELU_C = 0.7978845608028654
_GELU_A = 0.044715


def _gelu(x):
    t = jnp.tanh(_GELU_C * (x + _GELU_A * x * x * x))
    return 0.5 * x * (1.0 + t)


def _gelu_grad(x):
    t = jnp.tanh(_GELU_C * (x + _GELU_A * x * x * x))
    return 0.5 * (1.0 + t) + 0.5 * x * (1.0 - t * t) * _GELU_C * (1.0 + 3.0 * _GELU_A * x * x)


def _first(i):
    return i == 0


def _last(i):
    return i == pl.num_programs(0) - 1


def _const(shape, idx):
    return pl.BlockSpec(shape, lambda *_, _idx=tuple(idx): _idx, pipeline_mode=pl.Buffered(1))


def _whole(a):
    return _const(a.shape, (0,) * a.ndim)


def _rows(tm, width, col=0):
    return pl.BlockSpec((tm, width), lambda i, _c=col: (i, _c))


def _params(ndim=1, vmem_mb=56):
    return pltpu.CompilerParams(dimension_semantics=("arbitrary",) * ndim, vmem_limit_bytes=vmem_mb << 20)


def _sds(shape, dtype):
    return jax.ShapeDtypeStruct(shape, dtype)


def _pos():
    return lax.axis_index("x"), lax.axis_index("y"), lax.axis_index("c")


def _other_chips(x, y):
    return [(1 - x, y), (x, 1 - y), (1 - x, 1 - y)]


def _half(ref, hc, lead=()):
    r = ref.shape[-2] // 2
    return ref.at[(*lead, pl.ds(pl.multiple_of(hc * r, 8), r), slice(None))]


def _rcopy(src, dst, ssem, rsem, to):
    return pltpu.make_async_remote_copy(src_ref=src, dst_ref=dst, send_sem=ssem, recv_sem=rsem,
                                        device_id=to, device_id_type=MESH)


def _gather_start(bufs, sems):
    ici_s, ici_r, _, _ = sems
    x, y, c = _pos()
    me = 2 * x + y
    for k, (cx, cy) in enumerate(_other_chips(x, y)):
        for t, buf in enumerate(bufs):
            mine = _half(buf, c, (me,))
            _rcopy(mine, mine, ici_s.at[3 * t + k], ici_r.at[3 * t + k], (cx, cy, c)).start()


def _gather_forward(bufs, sems):
    ici_s, ici_r, d2d_s, d2d_r = sems
    x, y, c = _pos()
    sib = (x, y, 1 - c)
    for k, (cx, cy) in enumerate(_other_chips(x, y)):
        for t, buf in enumerate(bufs):
            landed = _half(buf, c, (2 * cx + cy,))
            _rcopy(landed, landed, ici_s.at[3 * t + k], ici_r.at[3 * t + k], (cx, cy, c)).wait_recv()
            _rcopy(landed, landed, d2d_s.at[3 * t + k], d2d_r.at[3 * t + k], sib).start()


def _gather_drain(bufs, sems):
    ici_s, ici_r, d2d_s, d2d_r = sems
    x, y, c = _pos()
    me = 2 * x + y
    sib = (x, y, 1 - c)
    chips = _other_chips(x, y)
    for k, (cx, cy) in enumerate(chips):
        for t, buf in enumerate(bufs):
            got = _half(buf, 1 - c, (2 * cx + cy,))
            _rcopy(got, got, d2d_s.at[3 * t + k], d2d_r.at[3 * t + k], sib).wait_recv()
    for k, (cx, cy) in enumerate(chips):
        for t, buf in enumerate(bufs):
            mine = _half(buf, c, (me,))
            _rcopy(mine, mine, ici_s.at[3 * t + k], ici_r.at[3 * t + k], (cx, cy, c)).wait_send()
            landed = _half(buf, c, (2 * cx + cy,))
            _rcopy(landed, landed, d2d_s.at[3 * t + k], d2d_r.at[3 * t + k], sib).wait_send()


def _xchg_copies(srcs, dsts, nbig, sems):
    ssem, rsem = sems
    x, y, c = _pos()
    cps = []
    for k, (cx, cy) in enumerate(_other_chips(x, y)):
        for t, (src, dst) in enumerate(zip(srcs, dsts)):
            s = src.at[2 * cx + cy] if t < nbig else src
            cps.append(_rcopy(s, dst.at[k], ssem.at[3 * t + k], rsem.at[3 * t + k], (cx, cy, c)))
    return cps


def _gather_sems(n):
    return [pltpu.SemaphoreType.DMA((3 * n,))] * 4


def _xchg_sems(n):
    return [pltpu.SemaphoreType.DMA((3 * n,))] * 2


def _xchg_out_shapes(items):
    return [_sds((3,) + a.shape[-2:], a.dtype) for a in items]


def _swap_copies(srcs, dsts, sems):
    ssem, rsem = sems
    x, y, c = _pos()
    cps = []
    for t, (src, dst) in enumerate(zip(srcs, dsts)):
        lead = (slice(None),) * (len(src.shape) - 2)
        cps.append(_rcopy(_half(src, 1 - c, lead), dst, ssem.at[t], rsem.at[t], (x, y, 1 - c)))
    return cps


def _swap_out_shapes(items):
    return [_sds(a.shape[:-2] + (a.shape[-2] // 2, a.shape[-1]), a.dtype) for a in items]


def _call(body, *, name, grid, in_specs, out_specs, out_shape, args, scratch_shapes=(), aliases=None, carry=None):
    in_specs, out_specs, out_shape = list(in_specs), list(out_specs), list(out_shape)
    args = list(args)
    aliases = dict(aliases or {})
    n_in, n_out, n_scr = len(args), len(out_shape), len(scratch_shapes)
    carries = [] if not carry else ([carry] if isinstance(carry, tuple) else list(carry))
    plans = []
    cin, cout_shape, sems = [], [], []
    for c in carries:
        kind = c[0]
        items = list(c[1]) + (list(c[2]) if kind == "xchg" else [])
        if kind == "gather":
            shapes, sm = [_sds(a.shape, a.dtype) for a in items], _gather_sems(len(items))
            for t in range(len(items)):
                aliases[n_in + len(cin) + t] = n_out + len(cin) + t
        elif kind == "xchg":
            shapes, sm = _xchg_out_shapes(items), _xchg_sems(len(items))
        else:
            shapes, sm = _swap_out_shapes(items), [pltpu.SemaphoreType.DMA((len(items),))] * 2
        plans.append((kind, len(cin), len(items), len(sems), len(sm), len(c[1])))
        cin += items
        cout_shape += shapes
        sems += sm
    nc = len(cin)

    def wrapped(*refs):
        ins = refs[:n_in]
        c_in = refs[n_in:n_in + nc]
        outs = refs[n_in + nc:n_in + nc + n_out]
        c_out = refs[n_in + nc + n_out:n_in + 2 * nc + n_out]
        scr = refs[n_in + 2 * nc + n_out:n_in + 2 * nc + n_out + n_scr]
        sem = refs[n_in + 2 * nc + n_out + n_scr:]
        i = pl.program_id(0)

        def copies(plan):
            kind, o0, no, s0, ns, nbig = plan
            if kind == "xchg":
                return _xchg_copies(c_in[o0:o0 + no], c_out[o0:o0 + no], nbig, sem[s0:s0 + ns])
            return _swap_copies(c_in[o0:o0 + no], c_out[o0:o0 + no], sem[s0:s0 + ns])

        if plans:
            @pl.when(_first(i))
            def _():
                for plan in plans:
                    kind, o0, no, s0, ns, _ = plan
                    if kind == "gather":
                        _gather_start(c_out[o0:o0 + no], sem[s0:s0 + ns])
                    else:
                        for cp in copies(plan):
                            cp.start()
        body(*ins, *outs, *scr)
        for plan in plans:
            kind, o0, no, s0, ns, _ = plan
            if kind == "gather":
                @pl.when(i == max(grid[0] - 3, 0))
                def _(o0=o0, no=no, s0=s0, ns=ns):
                    _gather_forward(c_out[o0:o0 + no], sem[s0:s0 + ns])
        if plans:
            @pl.when(_last(i))
            def _():
                for plan in plans:
                    kind, o0, no, s0, ns, _ = plan
                    if kind == "gather":
                        _gather_drain(c_out[o0:o0 + no], sem[s0:s0 + ns])
                    else:
                        for cp in copies(plan):
                            cp.wait()

    outs = pl.pallas_call(
        wrapped, name=name, grid=grid,
        in_specs=in_specs + [ANY] * nc, out_specs=out_specs + [ANY] * nc,
        out_shape=out_shape + cout_shape,
        scratch_shapes=list(scratch_shapes) + sems,
        input_output_aliases=aliases,
        compiler_params=_params(len(grid)),
    )(*args, *cin)
    return list(outs)


def _sibling_swap(items, name):
    n = len(items)

    def body(*refs):
        src = refs[:n]
        dst = refs[n:2 * n]
        ssem, rsem = refs[2 * n:]
        x, y, c = _pos()
        sib = (x, y, 1 - c)
        cps = []
        for t in range(n):
            lead = (slice(None),) * (len(src[t].shape) - 2)
            cp = _rcopy(_half(src[t], 1 - c, lead), dst[t], ssem.at[t], rsem.at[t], sib)
            cp.start()
            cps.append(cp)
        for cp in cps:
            cp.wait()

    return pl.pallas_call(
        body, name=name, in_specs=[ANY] * n, out_specs=[ANY] * n,
        out_shape=[_sds(a.shape[:-2] + (a.shape[-2] // 2, a.shape[-1]), a.dtype) for a in items],
        scratch_shapes=[pltpu.SemaphoreType.DMA((n,)), pltpu.SemaphoreType.DMA((n,))],
    )(*items)


def _sibling_join(items, name, xchg=((), ())):
    n = len(items)
    xs = list(xchg[0]) + list(xchg[1])
    nx = len(xs)

    def body(*refs):
        x_in = refs[n:n + nx]
        dst = refs[n + nx:2 * n + nx]
        x_out = refs[2 * n + nx:2 * n + 2 * nx]
        ssem, rsem = refs[2 * n + 2 * nx:2 * n + 2 * nx + 2]
        x, y, c = _pos()
        sib = (x, y, 1 - c)
        far = _xchg_copies(x_in, x_out, len(xchg[0]), refs[2 * n + 2 * nx + 2:]) if nx else []
        for cp in far:
            cp.start()
        cps = []
        for t in range(n):
            mine = _half(dst[t], c)
            cp = _rcopy(mine, mine, ssem.at[t], rsem.at[t], sib)
            cp.start()
            cps.append(cp)
        for t, cp in enumerate(cps):
            theirs = _half(dst[t], 1 - c)
            _rcopy(theirs, theirs, ssem.at[t], rsem.at[t], sib).wait_recv()
            cp.wait_send()
        for cp in far:
            cp.wait()

    outs = pl.pallas_call(
        body, name=name, in_specs=[ANY] * (n + nx), out_specs=[ANY] * (n + nx),
        out_shape=[_sds(a.shape, a.dtype) for a in items] + _xchg_out_shapes(xs),
        input_output_aliases={t: t for t in range(n)},
        scratch_shapes=[pltpu.SemaphoreType.DMA((n,))] * 2 + (_xchg_sems(nx) if nx else []),
    )(*items, *xs)
    return list(outs[:n]), list(outs[n:])


def _ev_in_fwd(h, g, wev, tm, carry=None):
    s = h.shape[0]

    def body(h_ref, g_ref, w_ref, z_ref):
        hhat, _ = _rms(h_ref[...])
        n = (hhat * g_ref[...]).astype(BF)
        z_ref[...] = _dot(n, w_ref[...]).astype(BF)

    return _call(
        body, name="ev_in_fwd", grid=(s // tm,),
        in_specs=[_rows(tm, D), _const((1, D), (0, 0)), _const((D, DZ), (0, 0))],
        out_specs=[_rows(tm, DZ)], out_shape=[_sds((s, DZ), BF)],
        args=(h, g, wev), carry=carry)


def _halo_specs(tm, width, s, col=0):
    per = tm // HALO
    nh = s // HALO
    prev = pl.BlockSpec((HALO, width), lambda i, _c=col: (jnp.maximum(i * per - 1, 0), _c))
    nxt = pl.BlockSpec((HALO, width), lambda i, _c=col: (jnp.minimum((i + 1) * per, nh - 1), _c))
    return prev, nxt


def _fill_pad(pad_ref, prev, main, nxt, i, tm):
    pad_ref[pl.ds(0, HALO), :] = jnp.where(_first(i), 0.0, prev)
    pad_ref[pl.ds(HALO, tm), :] = main
    pad_ref[pl.ds(HALO + tm, HALO), :] = jnp.where(_last(i), 0.0, nxt)


def _shift8(xs_ref, pad_ref, tm):
    for j in range(8):
        xs_ref[j] = pad_ref[pl.ds(j, tm + 2 * HALO - 8), :]


def _tap(xs_ref, r, off):
    return xs_ref[off % 8, pl.ds(r + 8 * (off // 8), RC), :]


def _glu_a(z):
    return z[:, :DA] * _sigmoid(z[:, DA:2 * DA])


def _gate_c(z):
    return z[:, 2048:] * z[:, 1024:1536]


def _ev_mix_fwd(h, z, wa, ba, lng, lnb, wb, bb, evo, tm, carry=None):
    s = h.shape[0]

    def body(h_ref, z_ref, zp_ref, zn_ref, wa_ref, ba_ref, lng_ref, lnb_ref, wb_ref, bb_ref, wo_ref,
             h1_ref, a2_ref, c2_ref, cat_ref, apad, cpad, c2f, xs):
        i = pl.program_id(0)
        zm = z_ref[...].astype(F32)
        zp = zp_ref[...].astype(F32)
        zn = zn_ref[...].astype(F32)
        _fill_pad(apad, _glu_a(zp), _glu_a(zm), _glu_a(zn), i, tm)
        _fill_pad(cpad, _gate_c(zp), _gate_c(zm), _gate_c(zn), i, tm)
        _shift8(xs, apad, tm)
        for r in range(0, tm, RC):
            acc = jnp.zeros((RC, DA), F32) + ba_ref[...]
            for k in range(KA):
                acc = acc + _tap(xs, r, k + 1) * wa_ref[pl.ds(k, 1), :]
            a2_ref[pl.ds(r, RC), :] = acc
            acc = jnp.zeros((RC, DA), F32) + bb_ref[...]
            for k in range(KB):
                acc = acc + cpad[pl.ds(r + HALO - 1 + k, RC), :] * wb_ref[pl.ds(k, 1), :]
            c2f[pl.ds(r, RC), :] = acc
        y, _ = _ln(a2_ref[...])
        a3 = y * lng_ref[...] + lnb_ref[...]
        cat_ref[:, :DA] = (a3 * _sigmoid(a3)).astype(BF)
        c2 = c2f[...]
        c2_ref[...] = c2.astype(BF)
        cat_ref[:, DA:] = (zm[:, 1536:2048] * c2).astype(BF)
        h1_ref[...] = h_ref[...] + _dot(cat_ref[...], wo_ref[...].reshape(D, D))

    zp_spec, zn_spec = _halo_specs(tm, DZ, s)
    small = lambda rows: _const((rows, DA), (0, 0))
    return _call(
        body, name="ev_mix_fwd", grid=(s // tm,),
        in_specs=[_rows(tm, D), _rows(tm, DZ), zp_spec, zn_spec, small(32), small(1), small(1), small(1),
                  small(8), small(1), _whole(evo)],
        out_specs=[_rows(tm, D), _rows(tm, DA), _rows(tm, DA), _rows(tm, D)],
        out_shape=[_sds((s, D), F32), _sds((s, DA), F32), _sds((s, DA), BF), _sds((s, D), BF)],
        scratch_shapes=[pltpu.VMEM((tm + 2 * HALO, DA), F32), pltpu.VMEM((tm + 2 * HALO, DA), F32),
                        pltpu.VMEM((tm, DA), F32), pltpu.VMEM((8, tm + 2 * HALO - 8, DA), F32)],
        args=(h, z, z, z, wa, ba, lng, lnb, wb, bb, evo), carry=carry)


def _ev_mix_bwd(dh, cat, a2, c2, z, lng, lnb, evo, tm, carry=None):
    s = dh.shape[0]

    def body(dh_ref, cat_ref, a2_ref, c2_ref, zgb_ref, lng_ref, lnb_ref, wo_ref,
             da2_ref, dc2_ref, dgb_ref, dlng_ref, dlnb_ref, gwo_ref, wacc):
        i = pl.program_id(0)

        @pl.when(_first(i))
        def _():
            wacc[...] = jnp.zeros_like(wacc)
            dlng_ref[...] = jnp.zeros_like(dlng_ref)
            dlnb_ref[...] = jnp.zeros_like(dlnb_ref)

        dhb = dh_ref[...].astype(BF)
        dcat = _dot_nt(dhb, wo_ref[...].reshape(D, D))
        wacc[...] += _dot_tn(cat_ref[...], dhb)
        y, rstd = _ln(a2_ref[...])
        a3 = y * lng_ref[...] + lnb_ref[...]
        sg = _sigmoid(a3)
        da3 = dcat[:, :DA] * (sg * (1.0 + a3 * (1.0 - sg)))
        dlng_ref[...] += _colsum(da3 * y)
        dlnb_ref[...] += _colsum(da3)
        da2_ref[...] = _ln_bwd(da3 * lng_ref[...], y, rstd).astype(BF)
        db = dcat[:, DA:]
        dc2_ref[...] = (db * zgb_ref[...].astype(F32)).astype(BF)
        dgb_ref[...] = (db * c2_ref[...].astype(F32)).astype(BF)

        @pl.when(_last(i))
        def _():
            gwo_ref[...] = wacc[...].astype(BF).reshape(NSH, 256, D)

    small = _const((1, DA), (0, 0))
    return _call(
        body, name="ev_mix_bwd", grid=(s // tm,),
        in_specs=[_rows(tm, D), _rows(tm, D), _rows(tm, DA), _rows(tm, DA), _rows(tm, DA, 3), small, small,
                  _whole(evo)],
        out_specs=[_rows(tm, DA), _rows(tm, DA), _rows(tm, DA), small, small, _whole(evo)],
        out_shape=[_sds((s, DA), BF), _sds((s, DA), BF), _sds((s, DA), BF), _sds((1, DA), F32),
                   _sds((1, DA), F32), _sds(evo.shape, BF)],
        scratch_shapes=[pltpu.VMEM((D, D), F32)],
        args=(dh, cat, a2, c2, z, lng, lnb, evo), carry=carry)


def _ev_in_bwd(h, dh, z, da2, dc2, dgb, g, wa, wb, wev, tm, carry=None):
    s = h.shape[0]

    def body(h_ref, dh_ref, z_ref, zp_ref, zn_ref, da_ref, dap_ref, dan_ref, dc_ref, dcp_ref, dcn_ref,
             dgb_ref, g_ref, wa_ref, wb_ref, w_ref,
             dh0_ref, dwa_ref, dba_ref, dwb_ref, dbb_ref, dg_ref, gw_ref,
             apad, cpad, dapad, dcpad, dz, wacc, dwa_acc, dwb_acc, xs):
        i = pl.program_id(0)

        @pl.when(_first(i))
        def _():
            wacc[...] = jnp.zeros_like(wacc)
            dwa_acc[...] = jnp.zeros_like(dwa_acc)
            dwb_acc[...] = jnp.zeros_like(dwb_acc)
            dba_ref[...] = jnp.zeros_like(dba_ref)
            dbb_ref[...] = jnp.zeros_like(dbb_ref)
            dg_ref[...] = jnp.zeros_like(dg_ref)

        zm = z_ref[...].astype(F32)
        zp = zp_ref[...].astype(F32)
        zn = zn_ref[...].astype(F32)
        _fill_pad(apad, _glu_a(zp), _glu_a(zm), _glu_a(zn), i, tm)
        _fill_pad(cpad, _gate_c(zp), _gate_c(zm), _gate_c(zn), i, tm)
        da2 = da_ref[...].astype(F32)
        dc2 = dc_ref[...].astype(F32)
        _fill_pad(dapad, dap_ref[...].astype(F32), da2, dan_ref[...].astype(F32), i, tm)
        _fill_pad(dcpad, dcp_ref[...].astype(F32), dc2, dcn_ref[...].astype(F32), i, tm)
        dba_ref[...] += _colsum(da2)
        dbb_ref[...] += _colsum(dc2)

        sg = _sigmoid(zm[:, DA:2 * DA])
        aval = zm[:, :DA]
        _shift8(xs, dapad, tm)
        for r in range(0, tm, RC):
            acc = jnp.zeros((RC, DA), F32)
            for k in range(KA):
                acc = acc + _tap(xs, r, 2 * HALO - 1 - k) * wa_ref[pl.ds(k, 1), :]
            sgr = sg[r:r + RC]
            dz[pl.ds(r, RC), 0:DA] = (acc * sgr).astype(BF)
            dz[pl.ds(r, RC), DA:2 * DA] = (acc * aval[r:r + RC] * sgr * (1.0 - sgr)).astype(BF)
            acc = jnp.zeros((RC, DA), F32)
            for k in range(KB):
                acc = acc + dcpad[pl.ds(r + HALO + 1 - k, RC), :] * wb_ref[pl.ds(k, 1), :]
            dz[pl.ds(r, RC), 1024:1536] = (acc * zm[r:r + RC, 2048:]).astype(BF)
            dz[pl.ds(r, RC), 2048:2560] = (acc * zm[r:r + RC, 1024:1536]).astype(BF)
        dz[:, 1536:2048] = dgb_ref[...]

        _shift8(xs, apad, tm)
        for k in range(KA):
            acc = jnp.zeros((8, DA), F32)
            for r in range(0, tm, RC):
                p = _tap(xs, r, k + 1) * dapad[pl.ds(r + HALO, RC), :]
                acc = acc + p.reshape(RC // 8, 8, DA).sum(axis=0)
            dwa_acc[k] += acc
        for k in range(KB):
            acc = jnp.zeros((8, DA), F32)
            for r in range(0, tm, RC):
                p = cpad[pl.ds(r + HALO - 1 + k, RC), :] * dcpad[pl.ds(r + HALO, RC), :]
                acc = acc + p.reshape(RC // 8, 8, DA).sum(axis=0)
            dwb_acc[k] += acc

        gg = g_ref[...]
        hhat, rr = _rms(h_ref[...])
        n = (hhat * gg).astype(BF)
        dzb = dz[...]
        wacc[...] += _dot_tn(n, dzb)
        dn = _dot_nt(dzb, w_ref[...])
        dg_ref[...] += _colsum(dn * hhat)
        dh0_ref[...] = dh_ref[...] + _rms_bwd(dn, hhat, rr, gg)

        @pl.when(_last(i))
        def _():
            for k in range(NSH):
                gw_ref[k] = wacc[:, 640 * k:640 * (k + 1)].astype(BF)
            dwa_ref[...] = jnp.sum(dwa_acc[...], axis=1)
            dwb_ref[...] = jnp.sum(dwb_acc[...], axis=1)

    zp_spec, zn_spec = _halo_specs(tm, DZ, s)
    hp_spec, hn_spec = _halo_specs(tm, DA, s)
    small = lambda rows: _const((rows, DA), (0, 0))
    return _call(
        body, name="ev_in_bwd", grid=(s // tm,),
        in_specs=[_rows(tm, D), _rows(tm, D), _rows(tm, DZ), zp_spec, zn_spec,
                  _rows(tm, DA), hp_spec, hn_spec, _rows(tm, DA), hp_spec, hn_spec, _rows(tm, DA),
                  _const((1, D), (0, 0)), small(32), small(8), _const((D, DZ), (0, 0))],
        out_specs=[_rows(tm, D), small(32), small(1), small(8), small(1), _const((1, D), (0, 0)),
                   _const((NSH, D, 640), (0, 0, 0))],
        out_shape=[_sds((s, D), F32), _sds((32, DA), F32), _sds((1, DA), F32), _sds((8, DA), F32),
                   _sds((1, DA), F32), _sds((1, D), F32), _sds((NSH, D, 640), BF)],
        scratch_shapes=[pltpu.VMEM((tm + 2 * HALO, DA), F32)] * 4 + [
            pltpu.VMEM((tm, DZ), BF), pltpu.VMEM((D, DZ), F32),
            pltpu.VMEM((32, 8, DA), F32), pltpu.VMEM((8, 8, DA), F32),
            pltpu.VMEM((8, tm + 2 * HALO - 8, DA), F32)],
        args=(h, dh, z, z, z, da2, da2, da2, dc2, dc2, dc2, dgb, g, wa, wb, wev), carry=carry)


def _xa_spec(item):
    return _const((NSH, 256, D), (0, item, 0))


def _kv_proj(mem, g, xa, layer):
    def body(mem_ref, g_ref, wk_ref, wv_ref, k_ref, v_ref):
        mhat, _ = _rms(mem_ref[...])
        mn = (mhat * g_ref[...]).astype(BF)
        k_ref[...] = _dot(mn, wk_ref[...].reshape(D, D)).astype(BF)
        v_ref[...] = _dot(mn, wv_ref[...].reshape(D, D)).astype(BF)

    full = _const((NMEM, D), (0, 0))
    return _call(
        body, name=f"kv_proj{layer}", grid=(1,),
        in_specs=[full, _const((1, D), (0, 0)), _xa_spec(XK), _xa_spec(XV)],
        out_specs=[full, full], out_shape=[_sds((NMEM, D), BF), _sds((NMEM, D), BF)],
        args=(mem, g, xa, xa))


def _softmax_rows(sc):
    m = jnp.max(sc, axis=-1, keepdims=True)
    e = jnp.exp(sc - m)
    return e / jnp.sum(e, axis=-1, keepdims=True)


def _xattn_fwd(h, g, k, v, xa, layer, tm, carry=None):
    s = h.shape[0]

    def body(h_ref, g_ref, k_ref, v_ref, wq_ref, wo_ref, ho_ref, q_ref, o_scr):
        hh = h_ref[...]
        hhat, _ = _rms(hh)
        n = (hhat * g_ref[...]).astype(BF)
        q = _dot(n, wq_ref[...].reshape(D, D)).astype(BF)
        q_ref[...] = q
        for hd in range(HEADS):
            cs = slice(HD * hd, HD * (hd + 1))
            p = _softmax_rows(_dot_nt(q[:, cs], k_ref[:, cs]) * ATT_SCALE)
            o_scr[:, cs] = _dot(p.astype(BF), v_ref[:, cs]).astype(BF)
        ho_ref[...] = hh + _dot(o_scr[...], wo_ref[...].reshape(D, D))

    kvs = _const((NMEM, D), (0, 0))
    return _call(
        body, name=f"xattn_fwd{layer}", grid=(s // tm,),
        in_specs=[_rows(tm, D), _const((1, D), (0, 0)), kvs, kvs, _xa_spec(XQ), _xa_spec(XO)],
        out_specs=[_rows(tm, D), _rows(tm, D)],
        out_shape=[_sds((s, D), F32), _sds((s, D), BF)],
        scratch_shapes=[pltpu.VMEM((tm, D), BF)],
        args=(h, g, k, v, xa, xa), carry=carry)


def _xattn_bwd(h, dh, q, g, k, v, xa, layer, tm, carry=None):
    s = h.shape[0]

    def body(h_ref, dh_ref, q_ref, g_ref, k_ref, v_ref, wq_ref, wo_ref,
             dhi_ref, dk_ref, dv_ref, dg_ref, gqo_ref, o_scr, dq_scr, qacc, oacc):
        i = pl.program_id(0)

        @pl.when(_first(i))
        def _():
            qacc[...] = jnp.zeros_like(qacc)
            oacc[...] = jnp.zeros_like(oacc)
            dk_ref[...] = jnp.zeros_like(dk_ref)
            dv_ref[...] = jnp.zeros_like(dv_ref)
            dg_ref[...] = jnp.zeros_like(dg_ref)

        gg = g_ref[...]
        dho = dh_ref[...]
        dhb = dho.astype(BF)
        q = q_ref[...]
        do = _dot_nt(dhb, wo_ref[...].reshape(D, D)).astype(BF)
        for hd in range(HEADS):
            cs = slice(HD * hd, HD * (hd + 1))
            kh = k_ref[:, cs]
            vh = v_ref[:, cs]
            p = _softmax_rows(_dot_nt(q[:, cs], kh) * ATT_SCALE)
            pb = p.astype(BF)
            o_scr[:, cs] = _dot(pb, vh).astype(BF)
            doh = do[:, cs]
            dp = _dot_nt(doh, vh)
            dv_ref[:, cs] += _dot_tn(pb, doh)
            ds = (p * (dp - jnp.sum(dp * p, axis=-1, keepdims=True)) * ATT_SCALE).astype(BF)
            dq_scr[:, cs] = _dot(ds, kh).astype(BF)
            dk_ref[:, cs] += _dot_tn(ds, q[:, cs])
        oacc[...] += _dot_tn(o_scr[...], dhb)
        hhat, rr = _rms(h_ref[...])
        n = (hhat * gg).astype(BF)
        dq = dq_scr[...]
        qacc[...] += _dot_tn(n, dq)
        dn = _dot_nt(dq, wq_ref[...].reshape(D, D))
        dg_ref[...] += _colsum(dn * hhat)
        dhi_ref[...] = dho + _rms_bwd(dn, hhat, rr, gg)

        @pl.when(_last(i))
        def _():
            gqo_ref[:, 0:256, :] = qacc[...].astype(BF).reshape(NSH, 256, D)
            gqo_ref[:, 256:512, :] = oacc[...].astype(BF).reshape(NSH, 256, D)

    kvs = _const((NMEM, D), (0, 0))
    return _call(
        body, name=f"xattn_bwd{layer}", grid=(s // tm,),
        in_specs=[_rows(tm, D), _rows(tm, D), _rows(tm, D), _const((1, D), (0, 0)), kvs, kvs,
                  _xa_spec(XQ), _xa_spec(XO)],
        out_specs=[_rows(tm, D), kvs, kvs, _const((1, D), (0, 0)), _const((NSH, 512, D), (0, 0, 0))],
        out_shape=[_sds((s, D), F32), _sds((NMEM, D), F32), _sds((NMEM, D), F32), _sds((1, D), F32),
                   _sds(xa.shape, BF)],
        scratch_shapes=[pltpu.VMEM((tm, D), BF), pltpu.VMEM((tm, D), BF), pltpu.VMEM((D, D), F32),
                        pltpu.VMEM((D, D), F32)],
        args=(h, dh, q, g, k, v, xa, xa), carry=carry)


def _kv_proj_bwd(mem, g, dk, dv, xa, gxa, layer):
    def body(mem_ref, g_ref, dk_ref, dv_ref, wk_ref, wv_ref, gxa_in, dg_ref, gkv_ref):
        mhat, _ = _rms(mem_ref[...])
        mn = (mhat * g_ref[...]).astype(BF)
        dkb = dk_ref[...].astype(BF)
        dvb = dv_ref[...].astype(BF)
        gkv_ref[:, 0:256, :] = _dot_tn(mn, dkb).astype(BF).reshape(NSH, 256, D)
        gkv_ref[:, 256:512, :] = _dot_tn(mn, dvb).astype(BF).reshape(NSH, 256, D)
        dmn = _dot_nt(dkb, wk_ref[...].reshape(D, D)) + _dot_nt(dvb, wv_ref[...].reshape(D, D))
        dg_ref[...] = _colsum(dmn * mhat)

    full = _const((NMEM, D), (0, 0))
    return _call(
        body, name=f"kv_proj_bwd{layer}", grid=(1,),
        in_specs=[full, _const((1, D), (0, 0)), full, full, _xa_spec(XK), _xa_spec(XV), ANY],
        out_specs=[_const((1, D), (0, 0)), _const((NSH, 512, D), (0, 1, 0))],
        out_shape=[_sds((1, D), F32), _sds(gxa.shape, BF)],
        aliases={6: 1},
        args=(mem, g, dk, dv, xa, xa, gxa))


def _ffn_block(h_ref, g_ref, wg_ref, wu_ref, wd_ref, gt_ref, up_ref):
    hh = h_ref[...]
    hhat, _ = _rms(hh)
    n = (hhat * g_ref[...]).astype(BF)
    out = hh
    for c0, c1 in FF_CHUNKS:
        gt = _dot_nt(n, wg_ref[c0:c1, :])
        up = _dot_nt(n, wu_ref[c0:c1, :])
        gt_ref[:, c0:c1] = gt.astype(BF)
        up_ref[:, c0:c1] = up.astype(BF)
        act = (gt * _sigmoid(gt) * up).astype(BF)
        out = out + _dot(act, wd_ref[c0:c1, :])
    return out


def _ffn_fwd(h, g, wg, wu, dn, layer, tm, carry=None):
    s = h.shape[0]
    wg, wu, dn = (a.reshape(DFF, D) for a in (wg, wu, dn))

    def body(h_ref, g_ref, wg_ref, wu_ref, wd_ref, ho_ref, gt_ref, up_ref):
        ho_ref[...] = _ffn_block(h_ref, g_ref, wg_ref, wu_ref, wd_ref, gt_ref, up_ref)

    return _call(
        body, name=f"ffn_fwd{layer}", grid=(s // tm,),
        in_specs=[_rows(tm, D), _const((1, D), (0, 0)), _whole(wg), _whole(wu), _whole(dn)],
        out_specs=[_rows(tm, D), _rows(tm, DFF), _rows(tm, DFF)],
        out_shape=[_sds((s, D), F32), _sds((s, DFF), BF), _sds((s, DFF), BF)],
        args=(h, g, wg, wu, dn), carry=carry)


def _ffn_fwd_loss(h, g, wg, wu, dn, tgt, g_final, layer, tm):
    s = h.shape[0]
    wg, wu, dn = (a.reshape(DFF, D) for a in (wg, wu, dn))

    def body(h_ref, g_ref, wg_ref, wu_ref, wd_ref, t_ref, gf_ref, dh_ref, loss_ref, dg_ref, gt_ref, up_ref):
        i = pl.program_id(0)

        @pl.when(_first(i))
        def _():
            loss_ref[...] = jnp.zeros_like(loss_ref)
            dg_ref[...] = jnp.zeros_like(dg_ref)

        out = _ffn_block(h_ref, g_ref, wg_ref, wu_ref, wd_ref, gt_ref, up_ref)
        gg = gf_ref[...]
        hhat, rr = _rms(out)
        diff = hhat * gg - t_ref[...]
        loss_ref[...] += jnp.sum(diff * diff) * (0.5 / D)
        dout = diff * (1.0 / D)
        dg_ref[...] += _colsum(dout * hhat)
        dh_ref[...] = _rms_bwd(dout, hhat, rr, gg)

    vec = _const((1, D), (0, 0))
    return _call(
        body, name=f"ffn_fwd_loss{layer}", grid=(s // tm,),
        in_specs=[_rows(tm, D), vec, _whole(wg), _whole(wu), _whole(dn), _rows(tm, D), vec],
        out_specs=[_rows(tm, D), _const((1, 128), (0, 0)), vec, _rows(tm, DFF), _rows(tm, DFF)],
        out_shape=[_sds((s, D), F32), _sds((1, 128), F32), _sds((1, D), F32), _sds((s, DFF), BF),
                   _sds((s, DFF), BF)],
        args=(h, g, wg, wu, dn, tgt, g_final))


def _ffn_bwd_dx(h, dh, gt, up, g, wg, wu, dn, layer, tm, carry=None):
    s = h.shape[0]
    wg, wu, dn = (a.reshape(DFF, D) for a in (wg, wu, dn))

    def body(h_ref, dh_ref, gt_ref, up_ref, g_ref, wg_ref, wu_ref, wd_ref,
             dhi_ref, dg_ref, dgt_ref, dup_ref, act_ref, n_ref):
        i = pl.program_id(0)

        @pl.when(_first(i))
        def _():
            dg_ref[...] = jnp.zeros_like(dg_ref)

        gg = g_ref[...]
        dho = dh_ref[...]
        dhb = dho.astype(BF)
        dn_ = jnp.zeros((tm, D), F32)
        for c0, c1 in FF_CHUNKS:
            gtv = gt_ref[:, c0:c1].astype(F32)
            upv = up_ref[:, c0:c1].astype(F32)
            sg = _sigmoid(gtv)
            silu = gtv * sg
            dact = _dot_nt(dhb, wd_ref[c0:c1, :])
            dgt = (dact * upv * (sg * (1.0 + gtv * (1.0 - sg)))).astype(BF)
            dup = (dact * silu).astype(BF)
            dgt_ref[:, c0:c1] = dgt
            dup_ref[:, c0:c1] = dup
            act_ref[:, c0:c1] = (silu * upv).astype(BF)
            dn_ = dn_ + _dot(dgt, wg_ref[c0:c1, :]) + _dot(dup, wu_ref[c0:c1, :])
        hhat, rr = _rms(h_ref[...])
        n_ref[...] = (hhat * gg).astype(BF)
        dg_ref[...] += _colsum(dn_ * hhat)
        dhi_ref[...] = dho + _rms_bwd(dn_, hhat, rr, gg)

    wide = _rows(tm, DFF)
    return _call(
        body, name=f"ffn_bwd_dx{layer}", grid=(s // tm,),
        in_specs=[_rows(tm, D), _rows(tm, D), wide, wide, _const((1, D), (0, 0)),
                  _whole(wg), _whole(wu), _whole(dn)],
        out_specs=[_rows(tm, D), _const((1, D), (0, 0)), wide, wide, wide, _rows(tm, D)],
        out_shape=[_sds((s, D), F32), _sds((1, D), F32), _sds((s, DFF), BF), _sds((s, DFF), BF),
                   _sds((s, DFF), BF), _sds((s, D), BF)],
        args=(h, dh, gt, up, g, wg, wu, dn), carry=carry)


def _ffn_wgrad(a, b, name, tm, slot, into=None, carry=None):
    s = a.shape[0]
    tm = min(tm, s)

    def body(a_ref, b_ref, *rest):
        o_ref, acc = rest[-2:]
        i = pl.program_id(0)

        @pl.when(_first(i))
        def _():
            acc[...] = jnp.zeros_like(acc)

        acc[...] += _dot_tn(a_ref[...], b_ref[...].astype(BF))

        @pl.when(_last(i))
        def _():
            o_ref[...] = acc[...].astype(BF).reshape(NSH, FSH, D)

    chained = into is not None
    return _call(
        body, name=name, grid=(s // tm,),
        in_specs=[_rows(tm, DFF), _rows(tm, D)] + [ANY] * chained,
        out_specs=[_const((NSH, FSH, D), (0, slot, 0))],
        out_shape=[_sds((NSH, 3 * FSH, D), BF)],
        scratch_shapes=[pltpu.VMEM((DFF, D), F32)],
        aliases={2: 0} if chained else None,
        args=(a, b) + ((into,) if chained else ()), carry=carry)


def _sgu_fwd(h, g, lng, lnb, ws, bsb, wod, odo, tm, carry=None):
    s = h.shape[0]

    def body(h_ref, g_ref, lng_ref, lnb_ref, ws_ref, bsb_ref, wi_ref, wo_ref, ho_ref, zp_ref, y_scr):
        hh = h_ref[...]
        hhat, _ = _rms(hh)
        n = (hhat * g_ref[...]).astype(BF)
        zpre = _dot(n, wi_ref[...])
        zp_ref[...] = zpre.astype(BF)
        z = _gelu(zpre)
        u = z[:, :D]
        y, _ = _ln(z[:, D:])
        vn = (y * lng_ref[...] + lnb_ref[...]).astype(BF)
        for c in range(tm // CHUNK):
            rs = slice(CHUNK * c, CHUNK * (c + 1))
            for gi in range(GROUPS):
                cs = slice(CHUNK * gi, CHUNK * (gi + 1))
                sv = _dot(ws_ref[gi], vn[rs, cs]) + bsb_ref[gi]
                y_scr[rs, cs] = (u[rs, cs] * sv).astype(BF)
        ho_ref[...] = hh + _dot(y_scr[...], wo_ref[...].reshape(D, D))

    vec = _const((1, D), (0, 0))
    sq = _const((GROUPS, CHUNK, CHUNK), (0, 0, 0))
    return _call(
        body, name="sgu_fwd", grid=(s // tm,),
        in_specs=[_rows(tm, D), vec, vec, vec, sq, sq, _const((D, 2 * D), (0, 0)), _whole(odo)],
        out_specs=[_rows(tm, D), _rows(tm, 2 * D)],
        out_shape=[_sds((s, D), F32), _sds((s, 2 * D), BF)],
        scratch_shapes=[pltpu.VMEM((tm, D), BF)],
        args=(h, g, lng, lnb, ws, bsb, wod, odo), carry=carry)


def _sgu_bwd(h, dh, zpre, g, lng, lnb, ws, wst, bsb, wod, odo, tm, carry=None):
    s = h.shape[0]

    def body(h_ref, dh_ref, zp_ref, g_ref, lng_ref, lnb_ref, ws_ref, wst_ref, bsb_ref, wi_ref, wo_ref,
             dhi_ref, dws_ref, dbs_ref, dlng_ref, dlnb_ref, dg_ref, gwi_ref, gwo_ref,
             y_scr, dz_scr, dvn_scr, iacc, oacc, bacc):
        i = pl.program_id(0)

        @pl.when(_first(i))
        def _():
            for ref in (iacc, oacc, bacc, dws_ref, dlng_ref, dlnb_ref, dg_ref):
                ref[...] = jnp.zeros_like(ref)

        gg = g_ref[...]
        dho = dh_ref[...]
        dhb = dho.astype(BF)
        zpre_v = zp_ref[...].astype(F32)
        z = _gelu(zpre_v)
        u = z[:, :D]
        yl, rstd = _ln(z[:, D:])
        vn = (yl * lng_ref[...] + lnb_ref[...]).astype(BF)
        dy = _dot_nt(dhb, wo_ref[...].reshape(D, D))
        for c in range(tm // CHUNK):
            rs = slice(CHUNK * c, CHUNK * (c + 1))
            for gi in range(GROUPS):
                cs = slice(CHUNK * gi, CHUNK * (gi + 1))
                vb = vn[rs, cs]
                sv = _dot(ws_ref[gi], vb) + bsb_ref[gi]
                ub = u[rs, cs]
                dyb = dy[rs, cs]
                y_scr[rs, cs] = (ub * sv).astype(BF)
                dz_scr[rs, cs] = dyb * sv
                dsv = dyb * ub
                bacc[gi] += dsv
                dsvb = dsv.astype(BF)
                dws_ref[gi] += _dot_nt(dsvb, vb)
                dvn_scr[rs, cs] = _dot(wst_ref[gi], dsvb)
        oacc[...] += _dot_tn(y_scr[...], dhb)
        dvn = dvn_scr[...]
        dlng_ref[...] += _colsum(dvn * yl)
        dlnb_ref[...] += _colsum(dvn)
        dz_scr[:, D:] = _ln_bwd(dvn * lng_ref[...], yl, rstd)
        dzb = (dz_scr[...] * _gelu_grad(zpre_v)).astype(BF)
        hhat, rr = _rms(h_ref[...])
        n = (hhat * gg).astype(BF)
        iacc[...] += _dot_tn(n, dzb)
        dn = _dot_nt(dzb, wi_ref[...])
        dg_ref[...] += _colsum(dn * hhat)
        dhi_ref[...] = dho + _rms_bwd(dn, hhat, rr, gg)

        @pl.when(_last(i))
        def _():
            for k in range(NSH):
                gwi_ref[k] = iacc[:, 512 * k:512 * (k + 1)].astype(BF)
            gwo_ref[...] = oacc[...].astype(BF).reshape(NSH, 256, D)
            dbs_ref[...] = jnp.sum(bacc[...], axis=-1)

    vec = _const((1, D), (0, 0))
    sq = _const((GROUPS, CHUNK, CHUNK), (0, 0, 0))
    return _call(
        body, name="sgu_bwd", grid=(s // tm,),
        in_specs=[_rows(tm, D), _rows(tm, D), _rows(tm, 2 * D), vec, vec, vec, sq, sq, sq,
                  _const((D, 2 * D), (0, 0)), _whole(odo)],
        out_specs=[_rows(tm, D), sq, _const((GROUPS, CHUNK), (0, 0)), vec, vec, vec,
                   _const((NSH, D, 512), (0, 0, 0)), _whole(odo)],
        out_shape=[_sds((s, D), F32), _sds((GROUPS, CHUNK, CHUNK), F32), _sds((GROUPS, CHUNK), F32),
                   _sds((1, D), F32), _sds((1, D), F32), _sds((1, D), F32), _sds((NSH, D, 512), BF),
                   _sds(odo.shape, BF)],
        scratch_shapes=[pltpu.VMEM((tm, D), BF), pltpu.VMEM((tm, 2 * D), F32),
                        pltpu.VMEM((tm, D), F32), pltpu.VMEM((D, 2 * D), F32), pltpu.VMEM((D, D), F32),
                        pltpu.VMEM((GROUPS, CHUNK, CHUNK), F32)],
        args=(h, dh, zpre, g, lng, lnb, ws, wst, bsb, wod, odo), carry=carry)


def _own_plane(shape):
    return pl.BlockSpec((1,) + shape, lambda i, chip_ref: (chip_ref[0], 0, 0), pipeline_mode=pl.Buffered(1))


def _pack_rows(chip_idx, w_down, w_gate_t, w_up_t, xq, xo, xk, xv, ev_out, od_out, first):
    nf = len(first)

    def body(chip_ref, wd_ref, wg_ref, wu_ref, xq_ref, xo_ref, xk_ref, xv_ref, ev_ref, od_ref, *rest):
        (evo_ref, odo_ref, dn0_ref, dn1_ref, wg0_ref, wg1_ref, wu0_ref, wu1_ref, xa0_ref,
         xa1_ref) = rest[nf:nf + 10]
        gathered = rest[nf + 10:2 * nf + 10]
        sems = rest[2 * nf + 10:]
        _gather_start(gathered, sems)
        evo_ref[0] = ev_ref[0].astype(BF)
        odo_ref[0] = od_ref[0].astype(BF)
        for src, outs in ((wd_ref, (dn0_ref, dn1_ref)), (wg_ref, (wg0_ref, wg1_ref)), (wu_ref, (wu0_ref, wu1_ref))):
            for layer, o_ref in enumerate(outs):
                o_ref[0] = src[layer].astype(BF)
        for layer, xa_ref in enumerate((xa0_ref, xa1_ref)):
            for item, src in ((XQ, xq_ref), (XO, xo_ref), (XK, xk_ref), (XV, xv_ref)):
                xa_ref[0, 256 * item:256 * (item + 1), :] = src[layer].astype(BF)
        _gather_forward(gathered, sems)
        _gather_drain(gathered, sems)

    ins = (w_down, w_gate_t, w_up_t, xq, xo, xk, xv, ev_out, od_out)
    shapes = [(256, D), (256, D)] + [(FSH, D)] * 6 + [(4 * 256, D), (4 * 256, D)]
    grid_spec = pltpu.PrefetchScalarGridSpec(
        num_scalar_prefetch=1, grid=(1,), in_specs=[_whole(a) for a in ins] + [ANY] * nf,
        out_specs=[_own_plane(sh) for sh in shapes] + [ANY] * nf,
        scratch_shapes=_gather_sems(nf))
    outs = pl.pallas_call(
        body, name="pack_rows", grid_spec=grid_spec,
        out_shape=[_sds((NSH,) + sh, BF) for sh in shapes] + [_sds(a.shape, a.dtype) for a in first],
        input_output_aliases={1 + len(ins) + t: len(shapes) + t for t in range(nf)},
        compiler_params=_params(),
    )(chip_idx, *ins, *first)
    return outs[:len(shapes)], outs[len(shapes):]


def _pack_cols(chip_idx, od_in, ev_in, conv_a, conv_b, ln_g, ln_b):
    def body(chip_ref, od_ref, ev_ref, ca_ref, cb_ref, lg_ref, lb_ref, cev_ref, cod_ref, sp_ref):
        cev_ref[0] = ev_ref[0].astype(BF)
        cod_ref[0] = od_ref[0].astype(BF)
        sp_ref[...] = jnp.zeros_like(sp_ref)
        sp_ref[0, 0:KA, 0:128] = ca_ref[0]
        sp_ref[0, 32:32 + KB, 0:128] = cb_ref[0]
        sp_ref[0, 40:41, :] = lg_ref[...]
        sp_ref[0, 41:42, :] = lb_ref[...]

    ins = (od_in, ev_in, conv_a, conv_b, ln_g, ln_b)
    shapes = [(D, 640), (D, 512)]
    grid_spec = pltpu.PrefetchScalarGridSpec(
        num_scalar_prefetch=1, grid=(1,), in_specs=[_whole(a) for a in ins],
        out_specs=[_own_plane(sh) for sh in shapes] + [_own_plane((SP_ROWS, 256))])
    return pl.pallas_call(
        body, name="pack_cols", grid_spec=grid_spec,
        out_shape=[_sds((NSH,) + sh, BF) for sh in shapes] + [_sds((NSH, SP_ROWS, 256), F32)],
        compiler_params=_params(),
    )(chip_idx, *ins)


def _assemble(piece, parts, name):
    tr = 256
    cols = piece.shape[2]

    def body(p_ref, *outs):
        for k in range(NSH):
            for (off, width), o_ref in zip(parts, outs):
                o_ref[:, width * k:width * (k + 1)] = p_ref[k, :, off:off + width]

    return _call(
        body, name=name, grid=(D // tr,),
        in_specs=[pl.BlockSpec((NSH, tr, cols), lambda i: (0, i, 0))],
        out_specs=[pl.BlockSpec((tr, NSH * width), lambda i: (i, 0)) for _, width in parts],
        out_shape=[_sds((D, NSH * width), BF) for _, width in parts],
        args=(piece,))


def _blocks_of(rows, cols, itemsize):
    rb = rows
    while rb * cols * itemsize > (2 << 20) and rb % 2 == 0 and (rb // 2) % 16 == 0:
        rb //= 2
    return rb


def _add_half(full, recv, c_idx, name):
    p, r, cols = full.shape
    rb = _blocks_of(r // 2, cols, 4)
    nb = (r // 2) // rb

    def body(c_ref, a_ref, b_ref, o_ref):
        o_ref[...] = (a_ref[...].astype(F32) + b_ref[...].astype(F32)).astype(o_ref.dtype)

    grid_spec = pltpu.PrefetchScalarGridSpec(
        num_scalar_prefetch=1, grid=(p, nb),
        in_specs=[pl.BlockSpec((1, rb, cols), lambda j, i, c_ref: (j, c_ref[0] * nb + i, 0)),
                  pl.BlockSpec((1, rb, cols), lambda j, i, c_ref: (j, i, 0))],
        out_specs=pl.BlockSpec((1, rb, cols), lambda j, i, c_ref: (j, i, 0)))
    return pl.pallas_call(
        body, name=name, grid_spec=grid_spec, out_shape=_sds(recv.shape, full.dtype),
        compiler_params=_params(2),
    )(c_idx, full, recv)


def _sum_chips(own, got, plane_c, name):
    _, r, cols = own.shape
    rb = _blocks_of(r, cols, 4)
    nb = r // rb

    def body(s_ref, a_ref, b_ref, o_ref):
        f = lambda v: v.astype(F32)
        o_ref[...] = (f(a_ref[0]) + f(b_ref[1])) + (f(b_ref[0]) + f(b_ref[2]))

    grid_spec = pltpu.PrefetchScalarGridSpec(
        num_scalar_prefetch=1, grid=(nb,),
        in_specs=[pl.BlockSpec((1, rb, cols), lambda i, s_ref: (s_ref[0], i, 0)),
                  pl.BlockSpec((3, rb, cols), lambda i, s_ref: (0, i, 0))],
        out_specs=pl.BlockSpec((rb, cols), lambda i, s_ref: (s_ref[1] * nb + i, 0)))
    return pl.pallas_call(
        body, name=name, grid_spec=grid_spec, out_shape=_sds((2 * r, cols), F32),
        compiler_params=_params(1),
    )(plane_c, own, got)


def _adamw_math(w, g, m, v):
    m = ADAM_B1 * m + (1.0 - ADAM_B1) * g
    v = ADAM_B2 * v + (1.0 - ADAM_B2) * (g * g)
    m_hat = m / (1.0 - ADAM_B1 ** ADAM_STEP)
    v_hat = v / (1.0 - ADAM_B2 ** ADAM_STEP)
    delta = -ADAM_LR * (m_hat / (jnp.sqrt(v_hat) + ADAM_EPS) + ADAM_WD * w)
    return delta, m, v


def _adamw_big(w, m, v, gsrcs, gblock, gidx, name, rb):
    nl, r, cols = w.shape
    assert len(gsrcs) == nl

    def body(w_ref, m_ref, v_ref, *rest):
        g_refs, (go_ref, d_ref, mo_ref, vo_ref) = rest[:nl], rest[nl:]
        g = g_refs[0][:, 0:cols]
        if nl == 2:
            g = jnp.where(pl.program_id(0) == 0, g, g_refs[1][:, 0:cols])
        delta, mn, vn = _adamw_math(w_ref[0], g, m_ref[0], v_ref[0])
        go_ref[0] = g
        d_ref[0] = delta
        mo_ref[0] = mn
        vo_ref[0] = vn

    wspec = pl.BlockSpec((1, rb, cols), lambda l, i: (l, i, 0))
    gspec = pl.BlockSpec(gblock, lambda l, i: gidx(i))
    return pl.pallas_call(
        body, name=name, grid=(nl, r // rb),
        in_specs=[wspec, wspec, wspec] + [gspec] * nl,
        out_specs=[wspec] * 4, out_shape=[_sds(w.shape, F32)] * 4,
        compiler_params=_params(2),
    )(w, m, v, *gsrcs)


_SMALL = (
    ("g_mix", (2, D), "a", (0, 2), 0, False),
    ("g_xattn", (2, D), "a", (2, 4), 0, False),
    ("g_mem", (2, D), "a", (4, 6), 0, False),
    ("g_ffn", (2, D), "a", (6, 8), 0, False),
    ("g_final", (1, D), "a", (8, 9), 0, False),
    ("ev_a_conv_w", (KA, 128), "a", (16, 16 + KA), 0, True),
    ("ev_a_conv_b", (1, DA), "a", (11, 12), 0, False),
    ("ev_a_ln_g", (1, DA), "a", (11, 12), DA, False),
    ("ev_a_ln_b", (1, DA), "a", (12, 13), 0, False),
    ("ev_b_conv_w", (KB, 128), "a", (48, 48 + KB), 0, True),
    ("ev_b_conv_b", (1, DA), "a", (12, 13), DA, False),
    ("od_c_ln_g", (1, 256), "a", (9, 10), 0, True),
    ("od_c_ln_b", (1, 256), "a", (10, 11), 0, True),
    ("od_w_s", (GROUPS * CHUNK, CHUNK), "b", (0, GROUPS * CHUNK), 0, False),
    ("od_b_s", (GROUPS, CHUNK), "a", (56, 64), 0, False),
)


def _adamw_small(ga, gb, wmv):
    ns = len(_SMALL)

    def body(*refs):
        ga_ref, gb_ref = refs[0], refs[1]
        ins = refs[2:2 + 3 * ns]
        outs = refs[2 + 3 * ns:]
        x, y, _ = _pos()
        chip = 2 * x + y
        for j, (_, shape, pack, (r0, r1), col, sharded) in enumerate(_SMALL):
            src = ga_ref if pack == "a" else gb_ref
            width = shape[1]
            if sharded:
                g = jnp.zeros(shape, F32)
                for k in range(NSH):
                    blk = src[r0:r1, col + width * k:col + width * (k + 1)]
                    g = g + jnp.where(chip == k, blk, 0.0)
            else:
                g = src[r0:r1, col:col + width]
            w_ref, m_ref, v_ref = ins[3 * j:3 * j + 3]
            delta, mn, vn = _adamw_math(w_ref[...], g, m_ref[...], v_ref[...])
            go_ref, d_ref, mo_ref, vo_ref = outs[4 * j:4 * j + 4]
            go_ref[...] = g
            d_ref[...] = delta
            mo_ref[...] = mn
            vo_ref[...] = vn

    vm = pl.BlockSpec(memory_space=pltpu.VMEM)
    flat = [a for trio in wmv for a in trio]
    out_shape = []
    for (_, shape, *_rest) in _SMALL:
        out_shape += [_sds(shape, F32)] * 4
    outs = pl.pallas_call(
        body, name="adamw_small",
        in_specs=[vm] * (2 + 3 * ns), out_specs=[vm] * (4 * ns), out_shape=out_shape,
    )(ga, gb, *flat)
    return [tuple(outs[4 * j:4 * j + 4]) for j in range(ns)]


def _small_params(gs, W):
    unshard = lambda a: jnp.transpose(a, (1, 0, 2)).reshape(a.shape[1], NSH * a.shape[2])
    od_w_s, od_b_s = W["od_w_s"], W["od_b_s"]
    return dict(
        g_mix=W["g_mix"], g_xattn=W["g_xattn"], g_mem=W["g_mem"], g_ffn=W["g_ffn"],
        g_final=W["g_final"].reshape(1, D),
        wa=unshard(gs[:, 0:32, 0:128]), ba=W["ev_a_conv_b"], lng_a=W["ev_a_ln_g"], lnb_a=W["ev_a_ln_b"],
        wb=unshard(gs[:, 32:40, 0:128]), bb=W["ev_b_conv_b"],
        lng_c=unshard(gs[:, 40:41, :]), lnb_c=unshard(gs[:, 41:42, :]),
        ws=od_w_s[0].astype(BF), wst=jnp.swapaxes(od_w_s[0], 1, 2).astype(BF),
        bsb=jnp.broadcast_to(od_b_s[0][:, :, None], (GROUPS, CHUNK, CHUNK)),
    )


def _small_grad_packs(g):
    z = lambda r, c: jnp.zeros((r, c), F32)
    wide = lambda a: jnp.concatenate([a, z(a.shape[0], D - a.shape[1])], axis=1)
    rows = [g["g_mix"], g["g_xattn"], g["g_mem"], g["g_ffn"], g["g_final"], g["od_c_ln_g"], g["od_c_ln_b"],
            jnp.concatenate([g["ev_a_conv_b"], g["ev_a_ln_g"]], axis=1),
            jnp.concatenate([g["ev_a_ln_b"], g["ev_b_conv_b"]], axis=1),
            z(3, D), wide(g["ev_a_conv_w"]), wide(g["ev_b_conv_w"]), wide(g["od_b_s"])]
    a = jnp.concatenate(rows, axis=0)
    assert a.shape == (SA_ROWS, D), a.shape
    return a, g["od_w_s"].reshape(GROUPS * CHUNK, CHUNK)


def _step(x, mem, tgt, W, P, c_idx, tm_f, tm_b, tm_w):
    row = lambda a, i: a[i:i + 1]

    sm = _small_params(P["gs"], W)
    (wev,) = _assemble(P["cev"], [(0, 640)], "assemble_ev")
    z, evo, xa0 = _ev_in_fwd(x, row(sm["g_mix"], 0), wev, 2 * tm_f, carry=("gather", [P["evo"], P["xa0"]]))
    h1, a2, c2, cat, wg0, wu0 = _ev_mix_fwd(
        x, z, sm["wa"], sm["ba"], sm["lng_a"], sm["lnb_a"], sm["wb"], sm["bb"], evo, tm_f,
        carry=("gather", [P["wg0"], P["wu0"]]))
    k0, v0 = _kv_proj(mem, row(sm["g_mem"], 0), xa0, 0)
    h2, q0, dn0, cod = _xattn_fwd(h1, row(sm["g_xattn"], 0), k0, v0, xa0, 0, 2 * tm_f,
                                  carry=("gather", [P["dn0"], P["cod"]]))
    (wod,) = _assemble(cod, [(0, 512)], "assemble_od")
    h3, gt0, up0, odo, xa1, wg1 = _ffn_fwd(h2, row(sm["g_ffn"], 0), wg0, wu0, dn0, 0, tm_f,
                                           carry=("gather", [P["odo"], P["xa1"], P["wg1"]]))
    h4, zpre, wu1, dn1 = _sgu_fwd(h3, row(sm["g_mix"], 1), sm["lng_c"], sm["lnb_c"], sm["ws"], sm["bsb"], wod,
                                  odo, tm_f, carry=("gather", [P["wu1"], P["dn1"]]))
    k1, v1 = _kv_proj(mem, row(sm["g_mem"], 1), xa1, 1)
    h5, q1 = _xattn_fwd(h4, row(sm["g_xattn"], 1), k1, v1, xa1, 1, 2 * tm_f)
    dh6, loss, dg_final, gt1, up1 = _ffn_fwd_loss(h5, row(sm["g_ffn"], 1), wg1, wu1, dn1, tgt, sm["g_final"], 1,
                                                  tm_f)

    g = {"g_final": dg_final}
    parts, gots = {}, {}
    def add(name, piece, recv):
        parts[name] = _add_half(piece, recv, c_idx, f"rs_add_{name}")
        return parts[name]

    def ffn_wgrads(layer, dgt, dup, act, nb, dh_out, carry=None):
        g_ffn, *carried = _ffn_wgrad(dgt, nb, f"ffn_wgrad_gate{layer}", tm_w, 0, carry=carry)
        (g_ffn,) = _ffn_wgrad(dup, nb, f"ffn_wgrad_up{layer}", tm_w, 1, into=g_ffn)
        (g_ffn,) = _ffn_wgrad(act, dh_out, f"ffn_wgrad_down{layer}", tm_w, 2, into=g_ffn)
        return g_ffn, carried

    dh5, dg_ffn1, dgt, dup, act, nb = _ffn_bwd_dx(h5, dh6, gt1, up1, row(sm["g_ffn"], 1), wg1, wu1, dn1, 1, tm_b)
    g_ffn1, _ = ffn_wgrads(1, dgt, dup, act, nb, dh6)
    dh4, dk, dv, dg_xa1, g_xa1, recv = _xattn_bwd(
        h4, dh5, q1, row(sm["g_xattn"], 1), k1, v1, xa1, 1, tm_f, carry=("swap", [g_ffn1]))
    add("ffn1", g_ffn1, recv)
    dg_mem1, g_xa1 = _kv_proj_bwd(mem, row(sm["g_mem"], 1), dk, dv, xa1, g_xa1, 1)
    (dh3, g["od_w_s"], g["od_b_s"], g["od_c_ln_g"], g["od_c_ln_b"], dg_mix1, g_cod, g_odo, gots["ffn1"],
     recv) = _sgu_bwd(
        h3, dh4, zpre, row(sm["g_mix"], 1), sm["lng_c"], sm["lnb_c"], sm["ws"], sm["wst"], sm["bsb"], wod, odo, tm_b,
        carry=[("xchg", [parts["ffn1"]], []), ("swap", [g_xa1])])
    add("xa1", g_xa1, recv)
    dh2, dg_ffn0, dgt, dup, act, nb, gots["xa1"], recv_cod, recv_odo = _ffn_bwd_dx(
        h2, dh3, gt0, up0, row(sm["g_ffn"], 0), wg0, wu0, dn0, 0, tm_b,
        carry=[("xchg", [parts["xa1"]], []), ("swap", [g_cod, g_odo])])
    add("cod", g_cod, recv_cod)
    add("odo", g_odo, recv_odo)
    g_ffn0, (gots["cod"], gots["odo"]) = ffn_wgrads(0, dgt, dup, act, nb, dh3,
                                                    carry=("xchg", [parts["cod"], parts["odo"]], []))
    dh1, dk, dv, dg_xa0, g_xa0, recv = _xattn_bwd(
        h1, dh2, q0, row(sm["g_xattn"], 0), k0, v0, xa0, 0, tm_f, carry=("swap", [g_ffn0]))
    add("ffn0", g_ffn0, recv)
    dg_mem0, g_xa0 = _kv_proj_bwd(mem, row(sm["g_mem"], 0), dk, dv, xa0, g_xa0, 0)
    da2, dc2, dgb, g["ev_a_ln_g"], g["ev_a_ln_b"], g_evo, recv = _ev_mix_bwd(
        dh1, cat, a2, c2, z, sm["lng_a"], sm["lnb_a"], evo, tm_f, carry=("swap", [g_xa0]))
    add("xa0", g_xa0, recv)
    (dx, g["ev_a_conv_w"], g["ev_a_conv_b"], g["ev_b_conv_w"], g["ev_b_conv_b"], dg_mix0, g_cev, gots["ffn0"],
     gots["xa0"], recv) = _ev_in_bwd(
        x, dh1, z, da2, dc2, dgb, row(sm["g_mix"], 0), sm["wa"], sm["wb"], wev, tm_b,
        carry=[("xchg", [parts["ffn0"], parts["xa0"]], []), ("swap", [g_evo])])
    add("evo", g_evo, recv)
    g["g_mix"] = jnp.concatenate([dg_mix0, dg_mix1], axis=0)
    g["g_xattn"] = jnp.concatenate([dg_xa0, dg_xa1], axis=0)
    g["g_mem"] = jnp.concatenate([dg_mem0, dg_mem1], axis=0)
    g["g_ffn"] = jnp.concatenate([dg_ffn0, dg_ffn1], axis=0)
    ga, gb = _small_grad_packs(g)
    last = (g_cev, ga[None], gb[None])
    for name, piece, recv in zip(("cev", "ga", "gb"), last, _sibling_swap(last, "rs_swap_last")):
        add(name, piece, recv)
    return loss, dx, parts, gots


def kernel(x, mem, g_mix, g_xattn, g_mem, g_ffn, g_final, ev_w_in, ev_a_conv_w, ev_a_conv_b, ev_a_ln_g, ev_a_ln_b, ev_b_conv_w, ev_b_conv_b, ev_w_out, od_w_in, od_c_ln_g, od_c_ln_b, od_w_s, od_b_s, od_w_out, xa_w_q, xa_w_k, xa_w_v, xa_w_o, ffn_w_gate, ffn_w_up, ffn_w_down, loss_target, m_g_mix, m_g_xattn, m_g_mem, m_g_ffn, m_g_final, m_ev_w_in, m_ev_a_conv_w, m_ev_a_conv_b, m_ev_a_ln_g, m_ev_a_ln_b, m_ev_b_conv_w, m_ev_b_conv_b, m_ev_w_out, m_od_w_in, m_od_c_ln_g, m_od_c_ln_b, m_od_w_s, m_od_b_s, m_od_w_out, m_xa_w_q, m_xa_w_k, m_xa_w_v, m_xa_w_o, m_ffn_w_gate, m_ffn_w_up, m_ffn_w_down, v_g_mix, v_g_xattn, v_g_mem, v_g_ffn, v_g_final, v_ev_w_in, v_ev_a_conv_w, v_ev_a_conv_b, v_ev_a_ln_g, v_ev_a_ln_b, v_ev_b_conv_w, v_ev_b_conv_b, v_ev_w_out, v_od_w_in, v_od_c_ln_g, v_od_c_ln_b, v_od_w_s, v_od_b_s, v_od_w_out, v_xa_w_q, v_xa_w_k, v_xa_w_v, v_xa_w_o, v_ffn_w_gate, v_ffn_w_up, v_ffn_w_down):
    W = dict(g_mix=g_mix, g_xattn=g_xattn, g_mem=g_mem, g_ffn=g_ffn, g_final=g_final, ev_w_in=ev_w_in,
             ev_a_conv_w=ev_a_conv_w, ev_a_conv_b=ev_a_conv_b, ev_a_ln_g=ev_a_ln_g, ev_a_ln_b=ev_a_ln_b,
             ev_b_conv_w=ev_b_conv_w, ev_b_conv_b=ev_b_conv_b, ev_w_out=ev_w_out, od_w_in=od_w_in,
             od_c_ln_g=od_c_ln_g, od_c_ln_b=od_c_ln_b, od_w_s=od_w_s, od_b_s=od_b_s, od_w_out=od_w_out,
             xa_w_q=xa_w_q, xa_w_k=xa_w_k, xa_w_v=xa_w_v, xa_w_o=xa_w_o, ffn_w_gate=ffn_w_gate,
             ffn_w_up=ffn_w_up, ffn_w_down=ffn_w_down)
    M = dict(g_mix=m_g_mix, g_xattn=m_g_xattn, g_mem=m_g_mem, g_ffn=m_g_ffn, g_final=m_g_final, ev_w_in=m_ev_w_in,
             ev_a_conv_w=m_ev_a_conv_w, ev_a_conv_b=m_ev_a_conv_b, ev_a_ln_g=m_ev_a_ln_g, ev_a_ln_b=m_ev_a_ln_b,
             ev_b_conv_w=m_ev_b_conv_w, ev_b_conv_b=m_ev_b_conv_b, ev_w_out=m_ev_w_out, od_w_in=m_od_w_in,
             od_c_ln_g=m_od_c_ln_g, od_c_ln_b=m_od_c_ln_b, od_w_s=m_od_w_s, od_b_s=m_od_b_s, od_w_out=m_od_w_out,
             xa_w_q=m_xa_w_q, xa_w_k=m_xa_w_k, xa_w_v=m_xa_w_v, xa_w_o=m_xa_w_o, ffn_w_gate=m_ffn_w_gate,
             ffn_w_up=m_ffn_w_up, ffn_w_down=m_ffn_w_down)
    V = dict(g_mix=v_g_mix, g_xattn=v_g_xattn, g_mem=v_g_mem, g_ffn=v_g_ffn, g_final=v_g_final, ev_w_in=v_ev_w_in,
             ev_a_conv_w=v_ev_a_conv_w, ev_a_conv_b=v_ev_a_conv_b, ev_a_ln_g=v_ev_a_ln_g, ev_a_ln_b=v_ev_a_ln_b,
             ev_b_conv_w=v_ev_b_conv_w, ev_b_conv_b=v_ev_b_conv_b, ev_w_out=v_ev_w_out, od_w_in=v_od_w_in,
             od_c_ln_g=v_od_c_ln_g, od_c_ln_b=v_od_c_ln_b, od_w_s=v_od_w_s, od_b_s=v_od_b_s, od_w_out=v_od_w_out,
             xa_w_q=v_xa_w_q, xa_w_k=v_xa_w_k, xa_w_v=v_xa_w_v, xa_w_o=v_xa_w_o, ffn_w_gate=v_ffn_w_gate,
             ffn_w_up=v_ffn_w_up, ffn_w_down=v_ffn_w_down)
    order = list(W)
    cx, cy, cc = _pos()
    chip = (2 * cx + cy).astype(jnp.int32)
    cc = cc.astype(jnp.int32)
    chip_idx = jnp.reshape(chip, (1,))
    c_idx = jnp.reshape(cc, (1,))

    tr = lambda a: jnp.swapaxes(a, 1, 2)
    P = {}
    cev, P["cod"], gs = _pack_cols(chip_idx, od_w_in, ev_w_in, ev_a_conv_w, ev_b_conv_w, od_c_ln_g, od_c_ln_b)
    ((P["evo"], P["odo"], P["dn0"], P["dn1"], P["wg0"], P["wg1"], P["wu0"], P["wu1"], P["xa0"], P["xa1"]),
     (P["cev"], P["gs"])) = _pack_rows(chip_idx, ffn_w_down, tr(ffn_w_gate), tr(ffn_w_up), xa_w_q, xa_w_o,
                                       xa_w_k, xa_w_v, ev_w_out, od_w_out, first=[cev, gs])

    loss, dx, parts, gots = _step(x[0], mem[0], loss_target[0], W, P, c_idx, TM_FWD, TM_BWD, TM_WGRAD)
    loss = lax.psum(loss[0, 0], ("x", "y", "c"))

    chip_c = jnp.stack([chip, cc])
    zero_c = jnp.stack([jnp.zeros((), jnp.int32), cc])
    def sums(names):
        return [_sum_chips(parts[n], gots[n], zero_c if n in ("ga", "gb") else chip_c, f"rs_sum_{n}")
                for n in names]

    last = ("evo", "cev", "ga", "gb")
    early = [n for n in parts if n not in last]
    joined, got_last = _sibling_join(sums(early), "rs_join_xchg", xchg=(
        [parts["evo"], parts["cev"]], [parts["ga"][0], parts["gb"][0]]))
    red = dict(zip(early, joined))
    gots.update(zip(last, got_last))
    joined, _ = _sibling_join(sums(last), "rs_join_last")
    red.update(zip(last, joined))

    res = {}
    xa = [red["xa0"], red["xa1"]]
    ffn = [red["ffn0"], red["ffn1"]]
    big = {
        "ffn_w_gate": (ffn, (FSH, D), lambda i: (0, 0), FSH),
        "ffn_w_up": (ffn, (FSH, D), lambda i: (1, 0), FSH),
        "ffn_w_down": (ffn, (FSH, D), lambda i: (2, 0), FSH),
        "xa_w_q": (xa, (256, D), lambda i: (XQ, 0), 256),
        "xa_w_o": (xa, (256, D), lambda i: (XO, 0), 256),
        "xa_w_k": (xa, (256, D), lambda i: (XK, 0), 256),
        "xa_w_v": (xa, (256, D), lambda i: (XV, 0), 256),
        "ev_w_out": ([red["evo"]], (256, D), lambda i: (0, 0), 256),
        "od_w_out": ([red["odo"]], (256, D), lambda i: (0, 0), 256),
        "od_w_in": ([red["cod"]], (256, 512), lambda i: (i, 0), 256),
        "ev_w_in": ([red["cev"]], (256, 640), lambda i: (i, 0), 256),
    }
    for name, (srcs, gblock, gidx, rb) in big.items():
        t = tr if name in ("ffn_w_gate", "ffn_w_up") else (lambda a: a)
        outs = _adamw_big(t(W[name]), t(M[name]), t(V[name]), srcs, gblock, gidx, f"adamw_{name}", rb)
        res[name] = tuple(t(o) for o in outs)
    shape2 = {name: shape for name, shape, *_ in _SMALL}
    wmv = [tuple(d[name].reshape(shape2[name]) for d in (W, M, V)) for name, *_ in _SMALL]
    for (name, *_), outs in zip(_SMALL, _adamw_small(red["ga"], red["gb"], wmv)):
        res[name] = tuple(o.reshape(W[name].shape) for o in outs)

    grad_x = dx[None]
    return (loss, grad_x, *[res[n][0] for n in order], *[res[n][1] for n in order],
            *[res[n][2] for n in order], *[res[n][3] for n in order])
```

```python
import jax
import jax.numpy as jnp
from jax import lax
from jax.experimental import pallas as pl
from jax.experimental.pallas import tpu as pltpu

BF = jnp.bfloat16
F32 = jnp.float32

D = 1024
DA = 512
DZ = 2560
DFF = 2816
NSH = 4
FSH = DFF // NSH
FF_CHUNKS = ((0, 768), (768, 1536), (1536, 2304), (2304, 2816))
NMEM = 256
HEADS = 4
HD = D // HEADS
CHUNK = 128
GROUPS = 8
KA = 31
KB = 3
HALO = 16
RC = 32
RMS_EPS = 1e-6
LN_EPS = 1e-5
ATT_SCALE = HD ** -0.5
TM_FWD = 512
TM_BWD = 256
TM_WGRAD = 1024

ADAM_LR = 0.001
ADAM_B1 = 0.9
ADAM_B2 = 0.999
ADAM_EPS = 1e-08
ADAM_WD = 0.01
ADAM_STEP = 10

XQ, XO, XK, XV = 0, 1, 2, 3
SP_ROWS = 48
SA_ROWS = 64

MESH = pl.DeviceIdType.MESH
ANY = pl.BlockSpec(memory_space=pl.ANY)


def _dot(a, b):
    return jnp.dot(a, b, preferred_element_type=F32)


def _dot_nt(a, b):
    return lax.dot_general(a, b, (((1,), (1,)), ((), ())), preferred_element_type=F32)


def _dot_tn(a, b):
    return lax.dot_general(a, b, (((0,), (0,)), ((), ())), preferred_element_type=F32)


def _rms(h):
    r = lax.rsqrt(jnp.mean(h * h, axis=-1, keepdims=True) + RMS_EPS)
    return h * r, r


def _rms_bwd(dn, hhat, r, g):
    dhh = dn * g
    return r * (dhh - hhat * jnp.mean(dhh * hhat, axis=-1, keepdims=True))


def _ln(x):
    mu = jnp.mean(x, axis=-1, keepdims=True)
    xc = x - mu
    rstd = lax.rsqrt(jnp.mean(xc * xc, axis=-1, keepdims=True) + LN_EPS)
    return xc * rstd, rstd


def _ln_bwd(dy, y, rstd):
    return rstd * (dy - jnp.mean(dy, axis=-1, keepdims=True) - y * jnp.mean(dy * y, axis=-1, keepdims=True))


def _colsum(x):
    return jnp.sum(x, axis=0, keepdims=True)


def _sigmoid(x):
    return 1.0 / (1.0 + jnp.exp(-x))


_GELU_C = 0.7978845608028654
_GELU_A = 0.044715


def _gelu(x):
    t = jnp.tanh(_GELU_C * (x + _GELU_A * x * x * x))
    return 0.5 * x * (1.0 + t)


def _gelu_grad(x):
    t = jnp.tanh(_GELU_C * (x + _GELU_A * x * x * x))
    return 0.5 * (1.0 + t) + 0.5 * x * (1.0 - t * t) * _GELU_C * (1.0 + 3.0 * _GELU_A * x * x)


def _first(i):
    return i == 0


def _last(i):
    return i == pl.num_programs(0) - 1


def _const(shape, idx):
    return pl.BlockSpec(shape, lambda *_, _idx=tuple(idx): _idx, pipeline_mode=pl.Buffered(1))


def _whole(a):
    return _const(a.shape, (0,) * a.ndim)


def _rows(tm, width, col=0):
    return pl.BlockSpec((tm, width), lambda i, _c=col: (i, _c))


def _params(ndim=1, vmem_mb=56):
    return pltpu.CompilerParams(dimension_semantics=("arbitrary",) * ndim, vmem_limit_bytes=vmem_mb << 20)


def _sds(shape, dtype):
    return jax.ShapeDtypeStruct(shape, dtype)


def _pos():
    return lax.axis_index("x"), lax.axis_index("y"), lax.axis_index("c")


def _other_chips(x, y):
    return [(1 - x, y), (x, 1 - y), (1 - x, 1 - y)]


def _half(ref, hc, lead=()):
    r = ref.shape[-2] // 2
    return ref.at[(*lead, pl.ds(pl.multiple_of(hc * r, 8), r), slice(None))]


def _rcopy(src, dst, ssem, rsem, to):
    return pltpu.make_async_remote_copy(src_ref=src, dst_ref=dst, send_sem=ssem, recv_sem=rsem,
                                        device_id=to, device_id_type=MESH)


def _gather_start(bufs, sems):
    ici_s, ici_r, _, _ = sems
    x, y, c = _pos()
    me = 2 * x + y
    for k, (cx, cy) in enumerate(_other_chips(x, y)):
        for t, buf in enumerate(bufs):
            mine = _half(buf, c, (me,))
            _rcopy(mine, mine, ici_s.at[3 * t + k], ici_r.at[3 * t + k], (cx, cy, c)).start()


def _gather_forward(bufs, sems):
    ici_s, ici_r, d2d_s, d2d_r = sems
    x, y, c = _pos()
    sib = (x, y, 1 - c)
    for k, (cx, cy) in enumerate(_other_chips(x, y)):
        for t, buf in enumerate(bufs):
            landed = _half(buf, c, (2 * cx + cy,))
            _rcopy(landed, landed, ici_s.at[3 * t + k], ici_r.at[3 * t + k], (cx, cy, c)).wait_recv()
            _rcopy(landed, landed, d2d_s.at[3 * t + k], d2d_r.at[3 * t + k], sib).start()


def _gather_drain(bufs, sems):
    ici_s, ici_r, d2d_s, d2d_r = sems
    x, y, c = _pos()
    me = 2 * x + y
    sib = (x, y, 1 - c)
    chips = _other_chips(x, y)
    for k, (cx, cy) in enumerate(chips):
        for t, buf in enumerate(bufs):
            got = _half(buf, 1 - c, (2 * cx + cy,))
            _rcopy(got, got, d2d_s.at[3 * t + k], d2d_r.at[3 * t + k], sib).wait_recv()
    for k, (cx, cy) in enumerate(chips):
        for t, buf in enumerate(bufs):
            mine = _half(buf, c, (me,))
            _rcopy(mine, mine, ici_s.at[3 * t + k], ici_r.at[3 * t + k], (cx, cy, c)).wait_send()
            landed = _half(buf, c, (2 * cx + cy,))
            _rcopy(landed, landed, d2d_s.at[3 * t + k], d2d_r.at[3 * t + k], sib).wait_send()


def _xchg_copies(srcs, dsts, nbig, sems):
    ssem, rsem = sems
    x, y, c = _pos()
    cps = []
    for k, (cx, cy) in enumerate(_other_chips(x, y)):
        for t, (src, dst) in enumerate(zip(srcs, dsts)):
            s = src.at[2 * cx + cy] if t < nbig else src
            cps.append(_rcopy(s, dst.at[k], ssem.at[3 * t + k], rsem.at[3 * t + k], (cx, cy, c)))
    return cps


def _gather_sems(n):
    return [pltpu.SemaphoreType.DMA((3 * n,))] * 4


def _xchg_sems(n):
    return [pltpu.SemaphoreType.DMA((3 * n,))] * 2


def _xchg_out_shapes(items):
    return [_sds((3,) + a.shape[-2:], a.dtype) for a in items]


def _swap_copies(srcs, dsts, sems):
    ssem, rsem = sems
    x, y, c = _pos()
    cps = []
    for t, (src, dst) in enumerate(zip(srcs, dsts)):
        lead = (slice(None),) * (len(src.shape) - 2)
        cps.append(_rcopy(_half(src, 1 - c, lead), dst, ssem.at[t], rsem.at[t], (x, y, 1 - c)))
    return cps


def _swap_out_shapes(items):
    return [_sds(a.shape[:-2] + (a.shape[-2] // 2, a.shape[-1]), a.dtype) for a in items]


def _call(body, *, name, grid, in_specs, out_specs, out_shape, args, scratch_shapes=(), aliases=None, carry=None):
    in_specs, out_specs, out_shape = list(in_specs), list(out_specs), list(out_shape)
    args = list(args)
    aliases = dict(aliases or {})
    n_in, n_out, n_scr = len(args), len(out_shape), len(scratch_shapes)
    carries = [] if not carry else ([carry] if isinstance(carry, tuple) else list(carry))
    plans = []
    cin, cout_shape, sems = [], [], []
    for c in carries:
        kind = c[0]
        items = list(c[1]) + (list(c[2]) if kind == "xchg" else [])
        if kind == "gather":
            shapes, sm = [_sds(a.shape, a.dtype) for a in items], _gather_sems(len(items))
            for t in range(len(items)):
                aliases[n_in + len(cin) + t] = n_out + len(cin) + t
        elif kind == "xchg":
            shapes, sm = _xchg_out_shapes(items), _xchg_sems(len(items))
        else:
            shapes, sm = _swap_out_shapes(items), [pltpu.SemaphoreType.DMA((len(items),))] * 2
        plans.append((kind, len(cin), len(items), len(sems), len(sm), len(c[1])))
        cin += items
        cout_shape += shapes
        sems += sm
    nc = len(cin)

    def wrapped(*refs):
        ins = refs[:n_in]
        c_in = refs[n_in:n_in + nc]
        outs = refs[n_in + nc:n_in + nc + n_out]
        c_out = refs[n_in + nc + n_out:n_in + 2 * nc + n_out]
        scr = refs[n_in + 2 * nc + n_out:n_in + 2 * nc + n_out + n_scr]
        sem = refs[n_in + 2 * nc + n_out + n_scr:]
        i = pl.program_id(0)

        def copies(plan):
            kind, o0, no, s0, ns, nbig = plan
            if kind == "xchg":
                return _xchg_copies(c_in[o0:o0 + no], c_out[o0:o0 + no], nbig, sem[s0:s0 + ns])
            return _swap_copies(c_in[o0:o0 + no], c_out[o0:o0 + no], sem[s0:s0 + ns])

        if plans:
            @pl.when(_first(i))
            def _():
                for plan in plans:
                    kind, o0, no, s0, ns, _ = plan
                    if kind == "gather":
                        _gather_start(c_out[o0:o0 + no], sem[s0:s0 + ns])
                    else:
                        for cp in copies(plan):
                            cp.start()
        body(*ins, *outs, *scr)
        for plan in plans:
            kind, o0, no, s0, ns, _ = plan
            if kind == "gather":
                @pl.when(i == max(grid[0] - 3, 0))
                def _(o0=o0, no=no, s0=s0, ns=ns):
                    _gather_forward(c_out[o0:o0 + no], sem[s0:s0 + ns])
        if plans:
            @pl.when(_last(i))
            def _():
                for plan in plans:
                    kind, o0, no, s0, ns, _ = plan
                    if kind == "gather":
                        _gather_drain(c_out[o0:o0 + no], sem[s0:s0 + ns])
                    else:
                        for cp in copies(plan):
                            cp.wait()

    outs = pl.pallas_call(
        wrapped, name=name, grid=grid,
        in_specs=in_specs + [ANY] * nc, out_specs=out_specs + [ANY] * nc,
        out_shape=out_shape + cout_shape,
        scratch_shapes=list(scratch_shapes) + sems,
        input_output_aliases=aliases,
        compiler_params=_params(len(grid)),
    )(*args, *cin)
    return list(outs)


def _sibling_swap(items, name):
    n = len(items)

    def body(*refs):
        src = refs[:n]
        dst = refs[n:2 * n]
        ssem, rsem = refs[2 * n:]
        x, y, c = _pos()
        sib = (x, y, 1 - c)
        cps = []
        for t in range(n):
            lead = (slice(None),) * (len(src[t].shape) - 2)
            cp = _rcopy(_half(src[t], 1 - c, lead), dst[t], ssem.at[t], rsem.at[t], sib)
            cp.start()
            cps.append(cp)
        for cp in cps:
            cp.wait()

    return pl.pallas_call(
        body, name=name, in_specs=[ANY] * n, out_specs=[ANY] * n,
        out_shape=[_sds(a.shape[:-2] + (a.shape[-2] // 2, a.shape[-1]), a.dtype) for a in items],
        scratch_shapes=[pltpu.SemaphoreType.DMA((n,)), pltpu.SemaphoreType.DMA((n,))],
    )(*items)


_HBM = pl.BlockSpec(memory_space=pltpu.HBM)
_SEM = pl.BlockSpec(memory_space=pltpu.SEMAPHORE)
_EFFECT = pltpu.SideEffectType.DATAFLOW_SIDE_EFFECTING


def _xchg_start(big, small):
    items = list(big) + list(small)
    n = len(items)

    def body(*refs):
        src, land = refs[:n], refs[n:2 * n]
        send_sem, recv_sem = refs[2 * n:2 * n + 2]
        token = refs[-1]
        for cp in _xchg_copies(src, land, len(big), (send_sem, recv_sem)):
            cp.start()
        token[...] = jnp.zeros_like(token)

    lands = _xchg_out_shapes(items)
    hbm = lambda a: pltpu.with_memory_space_constraint(a, pltpu.HBM)
    outs = pl.pallas_call(
        body, name="rs_xchg_start",
        out_shape=[pltpu.SemaphoreType.DMA((3 * n,)), pltpu.SemaphoreType.DMA((3 * n,))]
        + [pltpu.HBM(a.shape, a.dtype) for a in items] + [pltpu.HBM(a.shape, a.dtype) for a in lands]
        + [_sds((8, 128), F32)],
        in_specs=[_HBM] * (2 * n), out_specs=[_SEM, _SEM] + [_HBM] * (2 * n) + [pl.BlockSpec(memory_space=pltpu.VMEM)],
        input_output_aliases={t: 2 + t for t in range(2 * n)},
        compiler_params=pltpu.CompilerParams(has_side_effects=_EFFECT),
    )(*[hbm(a) for a in items], *[hbm(lax.empty(a.shape, a.dtype)) for a in lands])
    return outs[0], outs[1], list(outs[2:2 + n]), list(outs[2 + n:2 + 2 * n]), outs[-1]


def _xchg_wait(send_sem, recv_sem, srcs, lands, nbig, after):
    n = len(srcs)

    def body(*refs):
        src, land = refs[:n], refs[n:2 * n]
        ssem, rsem = refs[2 * n:2 * n + 2]
        for cp in _xchg_copies(src, land, nbig, (ssem, rsem)):
            cp.wait_send()
            cp.wait_recv()

    outs = pl.pallas_call(
        body, name="rs_xchg_wait",
        out_shape=[pltpu.HBM(a.shape, a.dtype) for a in srcs] + [pltpu.HBM(a.shape, a.dtype) for a in lands],
        in_specs=[_HBM] * (2 * n) + [_SEM, _SEM, ANY], out_specs=[_HBM] * (2 * n),
        input_output_aliases={t: t for t in range(2 * n)},
        compiler_params=pltpu.CompilerParams(has_side_effects=_EFFECT),
    )(*srcs, *lands, send_sem, recv_sem, after)
    return list(outs[:n]), list(outs[n:])


def _sibling_join(items, name, after=()):
    n = len(items)

    def body(*refs):
        dst = refs[n + len(after):2 * n + len(after)]
        ssem, rsem = refs[2 * n + len(after):]
        x, y, c = _pos()
        sib = (x, y, 1 - c)
        cps = []
        for t in range(n):
            mine = _half(dst[t], c)
            cp = _rcopy(mine, mine, ssem.at[t], rsem.at[t], sib)
            cp.start()
            cps.append(cp)
        for t, cp in enumerate(cps):
            theirs = _half(dst[t], 1 - c)
            _rcopy(theirs, theirs, ssem.at[t], rsem.at[t], sib).wait_recv()
            cp.wait_send()

    outs = pl.pallas_call(
        body, name=name, in_specs=[ANY] * (n + len(after)), out_specs=[ANY] * n,
        out_shape=[_sds(a.shape, a.dtype) for a in items],
        input_output_aliases={t: t for t in range(n)},
        scratch_shapes=[pltpu.SemaphoreType.DMA((n,))] * 2,
    )(*items, *after)
    return list(outs)


def _ev_in_fwd(h, g, wev, tm, carry=None):
    s = h.shape[0]

    def body(h_ref, g_ref, w_ref, z_ref):
        hhat, _ = _rms(h_ref[...])
        n = (hhat * g_ref[...]).astype(BF)
        z_ref[...] = _dot(n, w_ref[...]).astype(BF)

    return _call(
        body, name="ev_in_fwd", grid=(s // tm,),
        in_specs=[_rows(tm, D), _const((1, D), (0, 0)), _const((D, DZ), (0, 0))],
        out_specs=[_rows(tm, DZ)], out_shape=[_sds((s, DZ), BF)],
        args=(h, g, wev), carry=carry)


def _halo_specs(tm, width, s, col=0):
    per = tm // HALO
    nh = s // HALO
    prev = pl.BlockSpec((HALO, width), lambda i, _c=col: (jnp.maximum(i * per - 1, 0), _c))
    nxt = pl.BlockSpec((HALO, width), lambda i, _c=col: (jnp.minimum((i + 1) * per, nh - 1), _c))
    return prev, nxt


def _fill_pad(pad_ref, prev, main, nxt, i, tm):
    pad_ref[pl.ds(0, HALO), :] = jnp.where(_first(i), 0.0, prev)
    pad_ref[pl.ds(HALO, tm), :] = main
    pad_ref[pl.ds(HALO + tm, HALO), :] = jnp.where(_last(i), 0.0, nxt)


def _shift8(xs_ref, pad_ref, tm):
    for j in range(8):
        xs_ref[j] = pad_ref[pl.ds(j, tm + 2 * HALO - 8), :]


def _tap(xs_ref, r, off):
    return xs_ref[off % 8, pl.ds(r + 8 * (off // 8), RC), :]


def _glu_a(z):
    return z[:, :DA] * _sigmoid(z[:, DA:2 * DA])


def _gate_c(z):
    return z[:, 2048:] * z[:, 1024:1536]


def _ev_mix_fwd(h, z, wa, ba, lng, lnb, wb, bb, evo, tm, carry=None):
    s = h.shape[0]

    def body(h_ref, z_ref, zp_ref, zn_ref, wa_ref, ba_ref, lng_ref, lnb_ref, wb_ref, bb_ref, wo_ref,
             h1_ref, a2_ref, c2_ref, cat_ref, apad, cpad, c2f, xs):
        i = pl.program_id(0)
        zm = z_ref[...].astype(F32)
        zp = zp_ref[...].astype(F32)
        zn = zn_ref[...].astype(F32)
        _fill_pad(apad, _glu_a(zp), _glu_a(zm), _glu_a(zn), i, tm)
        _fill_pad(cpad, _gate_c(zp), _gate_c(zm), _gate_c(zn), i, tm)
        _shift8(xs, apad, tm)
        for r in range(0, tm, RC):
            acc = jnp.zeros((RC, DA), F32) + ba_ref[...]
            for k in range(KA):
                acc = acc + _tap(xs, r, k + 1) * wa_ref[pl.ds(k, 1), :]
            a2_ref[pl.ds(r, RC), :] = acc
            acc = jnp.zeros((RC, DA), F32) + bb_ref[...]
            for k in range(KB):
                acc = acc + cpad[pl.ds(r + HALO - 1 + k, RC), :] * wb_ref[pl.ds(k, 1), :]
            c2f[pl.ds(r, RC), :] = acc
        y, _ = _ln(a2_ref[...])
        a3 = y * lng_ref[...] + lnb_ref[...]
        cat_ref[:, :DA] = (a3 * _sigmoid(a3)).astype(BF)
        c2 = c2f[...]
        c2_ref[...] = c2.astype(BF)
        cat_ref[:, DA:] = (zm[:, 1536:2048] * c2).astype(BF)
        h1_ref[...] = h_ref[...] + _dot(cat_ref[...], wo_ref[...].reshape(D, D))

    zp_spec, zn_spec = _halo_specs(tm, DZ, s)
    small = lambda rows: _const((rows, DA), (0, 0))
    return _call(
        body, name="ev_mix_fwd", grid=(s // tm,),
        in_specs=[_rows(tm, D), _rows(tm, DZ), zp_spec, zn_spec, small(32), small(1), small(1), small(1),
                  small(8), small(1), _whole(evo)],
        out_specs=[_rows(tm, D), _rows(tm, DA), _rows(tm, DA), _rows(tm, D)],
        out_shape=[_sds((s, D), F32), _sds((s, DA), F32), _sds((s, DA), BF), _sds((s, D), BF)],
        scratch_shapes=[pltpu.VMEM((tm + 2 * HALO, DA), F32), pltpu.VMEM((tm + 2 * HALO, DA), F32),
                        pltpu.VMEM((tm, DA), F32), pltpu.VMEM((8, tm + 2 * HALO - 8, DA), F32)],
        args=(h, z, z, z, wa, ba, lng, lnb, wb, bb, evo), carry=carry)


def _ev_mix_bwd(dh, cat, a2, c2, z, lng, lnb, evo, tm, carry=None):
    s = dh.shape[0]

    def body(dh_ref, cat_ref, a2_ref, c2_ref, zgb_ref, lng_ref, lnb_ref, wo_ref,
             da2_ref, dc2_ref, dgb_ref, dlng_ref, dlnb_ref, gwo_ref, wacc):
        i = pl.program_id(0)

        @pl.when(_first(i))
        def _():
            wacc[...] = jnp.zeros_like(wacc)
            dlng_ref[...] = jnp.zeros_like(dlng_ref)
            dlnb_ref[...] = jnp.zeros_like(dlnb_ref)

        dhb = dh_ref[...].astype(BF)
        dcat = _dot_nt(dhb, wo_ref[...].reshape(D, D))
        wacc[...] += _dot_tn(cat_ref[...], dhb)
        y, rstd = _ln(a2_ref[...])
        a3 = y * lng_ref[...] + lnb_ref[...]
        sg = _sigmoid(a3)
        da3 = dcat[:, :DA] * (sg * (1.0 + a3 * (1.0 - sg)))
        dlng_ref[...] += _colsum(da3 * y)
        dlnb_ref[...] += _colsum(da3)
        da2_ref[...] = _ln_bwd(da3 * lng_ref[...], y, rstd).astype(BF)
        db = dcat[:, DA:]
        dc2_ref[...] = (db * zgb_ref[...].astype(F32)).astype(BF)
        dgb_ref[...] = (db * c2_ref[...].astype(F32)).astype(BF)

        @pl.when(_last(i))
        def _():
            gwo_ref[...] = wacc[...].astype(BF).reshape(NSH, 256, D)

    small = _const((1, DA), (0, 0))
    return _call(
        body, name="ev_mix_bwd", grid=(s // tm,),
        in_specs=[_rows(tm, D), _rows(tm, D), _rows(tm, DA), _rows(tm, DA), _rows(tm, DA, 3), small, small,
                  _whole(evo)],
        out_specs=[_rows(tm, DA), _rows(tm, DA), _rows(tm, DA), small, small, _whole(evo)],
        out_shape=[_sds((s, DA), BF), _sds((s, DA), BF), _sds((s, DA), BF), _sds((1, DA), F32),
                   _sds((1, DA), F32), _sds(evo.shape, BF)],
        scratch_shapes=[pltpu.VMEM((D, D), F32)],
        args=(dh, cat, a2, c2, z, lng, lnb, evo), carry=carry)


def _ev_in_bwd(h, dh, z, da2, dc2, dgb, g, wa, wb, wev, tm, carry=None):
    s = h.shape[0]

    def body(h_ref, dh_ref, z_ref, zp_ref, zn_ref, da_ref, dap_ref, dan_ref, dc_ref, dcp_ref, dcn_ref,
             dgb_ref, g_ref, wa_ref, wb_ref, w_ref,
             dh0_ref, dwa_ref, dba_ref, dwb_ref, dbb_ref, dg_ref, gw_ref,
             apad, cpad, dapad, dcpad, dz, wacc, dwa_acc, dwb_acc, xs):
        i = pl.program_id(0)

        @pl.when(_first(i))
        def _():
            wacc[...] = jnp.zeros_like(wacc)
            dwa_acc[...] = jnp.zeros_like(dwa_acc)
            dwb_acc[...] = jnp.zeros_like(dwb_acc)
            dba_ref[...] = jnp.zeros_like(dba_ref)
            dbb_ref[...] = jnp.zeros_like(dbb_ref)
            dg_ref[...] = jnp.zeros_like(dg_ref)

        zm = z_ref[...].astype(F32)
        zp = zp_ref[...].astype(F32)
        zn = zn_ref[...].astype(F32)
        _fill_pad(apad, _glu_a(zp), _glu_a(zm), _glu_a(zn), i, tm)
        _fill_pad(cpad, _gate_c(zp), _gate_c(zm), _gate_c(zn), i, tm)
        da2 = da_ref[...].astype(F32)
        dc2 = dc_ref[...].astype(F32)
        _fill_pad(dapad, dap_ref[...].astype(F32), da2, dan_ref[...].astype(F32), i, tm)
        _fill_pad(dcpad, dcp_ref[...].astype(F32), dc2, dcn_ref[...].astype(F32), i, tm)
        dba_ref[...] += _colsum(da2)
        dbb_ref[...] += _colsum(dc2)

        sg = _sigmoid(zm[:, DA:2 * DA])
        aval = zm[:, :DA]
        _shift8(xs, dapad, tm)
        for r in range(0, tm, RC):
            acc = jnp.zeros((RC, DA), F32)
            for k in range(KA):
                acc = acc + _tap(xs, r, 2 * HALO - 1 - k) * wa_ref[pl.ds(k, 1), :]
            sgr = sg[r:r + RC]
            dz[pl.ds(r, RC), 0:DA] = (acc * sgr).astype(BF)
            dz[pl.ds(r, RC), DA:2 * DA] = (acc * aval[r:r + RC] * sgr * (1.0 - sgr)).astype(BF)
            acc = jnp.zeros((RC, DA), F32)
            for k in range(KB):
                acc = acc + dcpad[pl.ds(r + HALO + 1 - k, RC), :] * wb_ref[pl.ds(k, 1), :]
            dz[pl.ds(r, RC), 1024:1536] = (acc * zm[r:r + RC, 2048:]).astype(BF)
            dz[pl.ds(r, RC), 2048:2560] = (acc * zm[r:r + RC, 1024:1536]).astype(BF)
        dz[:, 1536:2048] = dgb_ref[...]

        _shift8(xs, apad, tm)
        for k in range(KA):
            acc = jnp.zeros((8, DA), F32)
            for r in range(0, tm, RC):
                p = _tap(xs, r, k + 1) * dapad[pl.ds(r + HALO, RC), :]
                acc = acc + p.reshape(RC // 8, 8, DA).sum(axis=0)
            dwa_acc[k] += acc
        for k in range(KB):
            acc = jnp.zeros((8, DA), F32)
            for r in range(0, tm, RC):
                p = cpad[pl.ds(r + HALO - 1 + k, RC), :] * dcpad[pl.ds(r + HALO, RC), :]
                acc = acc + p.reshape(RC // 8, 8, DA).sum(axis=0)
            dwb_acc[k] += acc

        gg = g_ref[...]
        hhat, rr = _rms(h_ref[...])
        n = (hhat * gg).astype(BF)
        dzb = dz[...]
        wacc[...] += _dot_tn(n, dzb)
        dn = _dot_nt(dzb, w_ref[...])
        dg_ref[...] += _colsum(dn * hhat)
        dh0_ref[...] = dh_ref[...] + _rms_bwd(dn, hhat, rr, gg)

        @pl.when(_last(i))
        def _():
            for k in range(NSH):
                gw_ref[k] = wacc[:, 640 * k:640 * (k + 1)].astype(BF)
            dwa_ref[...] = jnp.sum(dwa_acc[...], axis=1)
            dwb_ref[...] = jnp.sum(dwb_acc[...], axis=1)

    zp_spec, zn_spec = _halo_specs(tm, DZ, s)
    hp_spec, hn_spec = _halo_specs(tm, DA, s)
    small = lambda rows: _const((rows, DA), (0, 0))
    return _call(
        body, name="ev_in_bwd", grid=(s // tm,),
        in_specs=[_rows(tm, D), _rows(tm, D), _rows(tm, DZ), zp_spec, zn_spec,
                  _rows(tm, DA), hp_spec, hn_spec, _rows(tm, DA), hp_spec, hn_spec, _rows(tm, DA),
                  _const((1, D), (0, 0)), small(32), small(8), _const((D, DZ), (0, 0))],
        out_specs=[_rows(tm, D), small(32), small(1), small(8), small(1), _const((1, D), (0, 0)),
                   _const((NSH, D, 640), (0, 0, 0))],
        out_shape=[_sds((s, D), F32), _sds((32, DA), F32), _sds((1, DA), F32), _sds((8, DA), F32),
                   _sds((1, DA), F32), _sds((1, D), F32), _sds((NSH, D, 640), BF)],
        scratch_shapes=[pltpu.VMEM((tm + 2 * HALO, DA), F32)] * 4 + [
            pltpu.VMEM((tm, DZ), BF), pltpu.VMEM((D, DZ), F32),
            pltpu.VMEM((32, 8, DA), F32), pltpu.VMEM((8, 8, DA), F32),
            pltpu.VMEM((8, tm + 2 * HALO - 8, DA), F32)],
        args=(h, dh, z, z, z, da2, da2, da2, dc2, dc2, dc2, dgb, g, wa, wb, wev), carry=carry)


def _xa_spec(item):
    return _const((NSH, 256, D), (0, item, 0))


def _kv_proj(mem, g, xa, layer):
    def body(mem_ref, g_ref, wk_ref, wv_ref, k_ref, v_ref):
        mhat, _ = _rms(mem_ref[...])
        mn = (mhat * g_ref[...]).astype(BF)
        k_ref[...] = _dot(mn, wk_ref[...].reshape(D, D)).astype(BF)
        v_ref[...] = _dot(mn, wv_ref[...].reshape(D, D)).astype(BF)

    full = _const((NMEM, D), (0, 0))
    return _call(
        body, name=f"kv_proj{layer}", grid=(1,),
        in_specs=[full, _const((1, D), (0, 0)), _xa_spec(XK), _xa_spec(XV)],
        out_specs=[full, full], out_shape=[_sds((NMEM, D), BF), _sds((NMEM, D), BF)],
        args=(mem, g, xa, xa))


def _softmax_rows(sc):
    m = jnp.max(sc, axis=-1, keepdims=True)
    e = jnp.exp(sc - m)
    return e / jnp.sum(e, axis=-1, keepdims=True)


def _xattn_fwd(h, g, k, v, xa, layer, tm, carry=None):
    s = h.shape[0]

    def body(h_ref, g_ref, k_ref, v_ref, wq_ref, wo_ref, ho_ref, q_ref, o_scr):
        hh = h_ref[...]
        hhat, _ = _rms(hh)
        n = (hhat * g_ref[...]).astype(BF)
        q = _dot(n, wq_ref[...].reshape(D, D)).astype(BF)
        q_ref[...] = q
        for hd in range(HEADS):
            cs = slice(HD * hd, HD * (hd + 1))
            p = _softmax_rows(_dot_nt(q[:, cs], k_ref[:, cs]) * ATT_SCALE)
            o_scr[:, cs] = _dot(p.astype(BF), v_ref[:, cs]).astype(BF)
        ho_ref[...] = hh + _dot(o_scr[...], wo_ref[...].reshape(D, D))

    kvs = _const((NMEM, D), (0, 0))
    return _call(
        body, name=f"xattn_fwd{layer}", grid=(s // tm,),
        in_specs=[_rows(tm, D), _const((1, D), (0, 0)), kvs, kvs, _xa_spec(XQ), _xa_spec(XO)],
        out_specs=[_rows(tm, D), _rows(tm, D)],
        out_shape=[_sds((s, D), F32), _sds((s, D), BF)],
        scratch_shapes=[pltpu.VMEM((tm, D), BF)],
        args=(h, g, k, v, xa, xa), carry=carry)


def _xattn_bwd(h, dh, q, g, k, v, xa, layer, tm, carry=None):
    s = h.shape[0]

    def body(h_ref, dh_ref, q_ref, g_ref, k_ref, v_ref, wq_ref, wo_ref,
             dhi_ref, dk_ref, dv_ref, dg_ref, gqo_ref, o_scr, dq_scr, qacc, oacc):
        i = pl.program_id(0)

        @pl.when(_first(i))
        def _():
            qacc[...] = jnp.zeros_like(qacc)
            oacc[...] = jnp.zeros_like(oacc)
            dk_ref[...] = jnp.zeros_like(dk_ref)
            dv_ref[...] = jnp.zeros_like(dv_ref)
            dg_ref[...] = jnp.zeros_like(dg_ref)

        gg = g_ref[...]
        dho = dh_ref[...]
        dhb = dho.astype(BF)
        q = q_ref[...]
        do = _dot_nt(dhb, wo_ref[...].reshape(D, D)).astype(BF)
        for hd in range(HEADS):
            cs = slice(HD * hd, HD * (hd + 1))
            kh = k_ref[:, cs]
            vh = v_ref[:, cs]
            p = _softmax_rows(_dot_nt(q[:, cs], kh) * ATT_SCALE)
            pb = p.astype(BF)
            o_scr[:, cs] = _dot(pb, vh).astype(BF)
            doh = do[:, cs]
            dp = _dot_nt(doh, vh)
            dv_ref[:, cs] += _dot_tn(pb, doh)
            ds = (p * (dp - jnp.sum(dp * p, axis=-1, keepdims=True)) * ATT_SCALE).astype(BF)
            dq_scr[:, cs] = _dot(ds, kh).astype(BF)
            dk_ref[:, cs] += _dot_tn(ds, q[:, cs])
        oacc[...] += _dot_tn(o_scr[...], dhb)
        hhat, rr = _rms(h_ref[...])
        n = (hhat * gg).astype(BF)
        dq = dq_scr[...]
        qacc[...] += _dot_tn(n, dq)
        dn = _dot_nt(dq, wq_ref[...].reshape(D, D))
        dg_ref[...] += _colsum(dn * hhat)
        dhi_ref[...] = dho + _rms_bwd(dn, hhat, rr, gg)

        @pl.when(_last(i))
        def _():
            gqo_ref[:, 0:256, :] = qacc[...].astype(BF).reshape(NSH, 256, D)
            gqo_ref[:, 256:512, :] = oacc[...].astype(BF).reshape(NSH, 256, D)

    kvs = _const((NMEM, D), (0, 0))
    return _call(
        body, name=f"xattn_bwd{layer}", grid=(s // tm,),
        in_specs=[_rows(tm, D), _rows(tm, D), _rows(tm, D), _const((1, D), (0, 0)), kvs, kvs,
                  _xa_spec(XQ), _xa_spec(XO)],
        out_specs=[_rows(tm, D), kvs, kvs, _const((1, D), (0, 0)), _const((NSH, 512, D), (0, 0, 0))],
        out_shape=[_sds((s, D), F32), _sds((NMEM, D), F32), _sds((NMEM, D), F32), _sds((1, D), F32),
                   _sds(xa.shape, BF)],
        scratch_shapes=[pltpu.VMEM((tm, D), BF), pltpu.VMEM((tm, D), BF), pltpu.VMEM((D, D), F32),
                        pltpu.VMEM((D, D), F32)],
        args=(h, dh, q, g, k, v, xa, xa), carry=carry)


def _kv_proj_bwd(mem, g, dk, dv, xa, gxa, layer):
    def body(mem_ref, g_ref, dk_ref, dv_ref, wk_ref, wv_ref, gxa_in, dg_ref, gkv_ref):
        mhat, _ = _rms(mem_ref[...])
        mn = (mhat * g_ref[...]).astype(BF)
        dkb = dk_ref[...].astype(BF)
        dvb = dv_ref[...].astype(BF)
        gkv_ref[:, 0:256, :] = _dot_tn(mn, dkb).astype(BF).reshape(NSH, 256, D)
        gkv_ref[:, 256:512, :] = _dot_tn(mn, dvb).astype(BF).reshape(NSH, 256, D)
        dmn = _dot_nt(dkb, wk_ref[...].reshape(D, D)) + _dot_nt(dvb, wv_ref[...].reshape(D, D))
        dg_ref[...] = _colsum(dmn * mhat)

    full = _const((NMEM, D), (0, 0))
    return _call(
        body, name=f"kv_proj_bwd{layer}", grid=(1,),
        in_specs=[full, _const((1, D), (0, 0)), full, full, _xa_spec(XK), _xa_spec(XV), ANY],
        out_specs=[_const((1, D), (0, 0)), _const((NSH, 512, D), (0, 1, 0))],
        out_shape=[_sds((1, D), F32), _sds(gxa.shape, BF)],
        aliases={6: 1},
        args=(mem, g, dk, dv, xa, xa, gxa))


def _ffn_block(h_ref, g_ref, wg_ref, wu_ref, wd_ref, gt_ref, up_ref):
    hh = h_ref[...]
    hhat, _ = _rms(hh)
    n = (hhat * g_ref[...]).astype(BF)
    out = hh
    for c0, c1 in FF_CHUNKS:
        gt = _dot_nt(n, wg_ref[c0:c1, :])
        up = _dot_nt(n, wu_ref[c0:c1, :])
        gt_ref[:, c0:c1] = gt.astype(BF)
        up_ref[:, c0:c1] = up.astype(BF)
        act = (gt * _sigmoid(gt) * up).astype(BF)
        out = out + _dot(act, wd_ref[c0:c1, :])
    return out


def _ffn_fwd(h, g, wg, wu, dn, layer, tm, carry=None):
    s = h.shape[0]
    wg, wu, dn = (a.reshape(DFF, D) for a in (wg, wu, dn))

    def body(h_ref, g_ref, wg_ref, wu_ref, wd_ref, ho_ref, gt_ref, up_ref):
        ho_ref[...] = _ffn_block(h_ref, g_ref, wg_ref, wu_ref, wd_ref, gt_ref, up_ref)

    return _call(
        body, name=f"ffn_fwd{layer}", grid=(s // tm,),
        in_specs=[_rows(tm, D), _const((1, D), (0, 0)), _whole(wg), _whole(wu), _whole(dn)],
        out_specs=[_rows(tm, D), _rows(tm, DFF), _rows(tm, DFF)],
        out_shape=[_sds((s, D), F32), _sds((s, DFF), BF), _sds((s, DFF), BF)],
        args=(h, g, wg, wu, dn), carry=carry)


def _ffn_fwd_loss(h, g, wg, wu, dn, tgt, g_final, layer, tm):
    s = h.shape[0]
    wg, wu, dn = (a.reshape(DFF, D) for a in (wg, wu, dn))

    def body(h_ref, g_ref, wg_ref, wu_ref, wd_ref, t_ref, gf_ref, dh_ref, loss_ref, dg_ref, gt_ref, up_ref):
        i = pl.program_id(0)

        @pl.when(_first(i))
        def _():
            loss_ref[...] = jnp.zeros_like(loss_ref)
            dg_ref[...] = jnp.zeros_like(dg_ref)

        out = _ffn_block(h_ref, g_ref, wg_ref, wu_ref, wd_ref, gt_ref, up_ref)
        gg = gf_ref[...]
        hhat, rr = _rms(out)
        diff = hhat * gg - t_ref[...]
        loss_ref[...] += jnp.sum(diff * diff) * (0.5 / D)
        dout = diff * (1.0 / D)
        dg_ref[...] += _colsum(dout * hhat)
        dh_ref[...] = _rms_bwd(dout, hhat, rr, gg)

    vec = _const((1, D), (0, 0))
    return _call(
        body, name=f"ffn_fwd_loss{layer}", grid=(s // tm,),
        in_specs=[_rows(tm, D), vec, _whole(wg), _whole(wu), _whole(dn), _rows(tm, D), vec],
        out_specs=[_rows(tm, D), _const((1, 128), (0, 0)), vec, _rows(tm, DFF), _rows(tm, DFF)],
        out_shape=[_sds((s, D), F32), _sds((1, 128), F32), _sds((1, D), F32), _sds((s, DFF), BF),
                   _sds((s, DFF), BF)],
        args=(h, g, wg, wu, dn, tgt, g_final))


def _ffn_bwd_dx(h, dh, gt, up, g, wg, wu, dn, layer, tm, carry=None):
    s = h.shape[0]
    wg, wu, dn = (a.reshape(DFF, D) for a in (wg, wu, dn))

    def body(h_ref, dh_ref, gt_ref, up_ref, g_ref, wg_ref, wu_ref, wd_ref,
             dhi_ref, dg_ref, dgt_ref, dup_ref, act_ref, n_ref):
        i = pl.program_id(0)

        @pl.when(_first(i))
        def _():
            dg_ref[...] = jnp.zeros_like(dg_ref)

        gg = g_ref[...]
        dho = dh_ref[...]
        dhb = dho.astype(BF)
        dn_ = jnp.zeros((tm, D), F32)
        for c0, c1 in FF_CHUNKS:
            gtv = gt_ref[:, c0:c1].astype(F32)
            upv = up_ref[:, c0:c1].astype(F32)
            sg = _sigmoid(gtv)
            silu = gtv * sg
            dact = _dot_nt(dhb, wd_ref[c0:c1, :])
            dgt = (dact * upv * (sg * (1.0 + gtv * (1.0 - sg)))).astype(BF)
            dup = (dact * silu).astype(BF)
            dgt_ref[:, c0:c1] = dgt
            dup_ref[:, c0:c1] = dup
            act_ref[:, c0:c1] = (silu * upv).astype(BF)
            dn_ = dn_ + _dot(dgt, wg_ref[c0:c1, :]) + _dot(dup, wu_ref[c0:c1, :])
        hhat, rr = _rms(h_ref[...])
        n_ref[...] = (hhat * gg).astype(BF)
        dg_ref[...] += _colsum(dn_ * hhat)
        dhi_ref[...] = dho + _rms_bwd(dn_, hhat, rr, gg)

    wide = _rows(tm, DFF)
    return _call(
        body, name=f"ffn_bwd_dx{layer}", grid=(s // tm,),
        in_specs=[_rows(tm, D), _rows(tm, D), wide, wide, _const((1, D), (0, 0)),
                  _whole(wg), _whole(wu), _whole(dn)],
        out_specs=[_rows(tm, D), _const((1, D), (0, 0)), wide, wide, wide, _rows(tm, D)],
        out_shape=[_sds((s, D), F32), _sds((1, D), F32), _sds((s, DFF), BF), _sds((s, DFF), BF),
                   _sds((s, DFF), BF), _sds((s, D), BF)],
        args=(h, dh, gt, up, g, wg, wu, dn), carry=carry)


def _ffn_wgrad(a, b, name, tm, slot, into=None, carry=None):
    s = a.shape[0]
    tm = min(tm, s)

    def body(a_ref, b_ref, *rest):
        o_ref, acc = rest[-2:]
        i = pl.program_id(0)

        @pl.when(_first(i))
        def _():
            acc[...] = jnp.zeros_like(acc)

        acc[...] += _dot_tn(a_ref[...], b_ref[...].astype(BF))

        @pl.when(_last(i))
        def _():
            o_ref[...] = acc[...].astype(BF).reshape(NSH, FSH, D)

    chained = into is not None
    return _call(
        body, name=name, grid=(s // tm,),
        in_specs=[_rows(tm, DFF), _rows(tm, D)] + [ANY] * chained,
        out_specs=[_const((NSH, FSH, D), (0, slot, 0))],
        out_shape=[_sds((NSH, 3 * FSH, D), BF)],
        scratch_shapes=[pltpu.VMEM((DFF, D), F32)],
        aliases={2: 0} if chained else None,
        args=(a, b) + ((into,) if chained else ()), carry=carry)


def _sgu_fwd(h, g, lng, lnb, ws, bsb, wod, odo, tm, carry=None):
    s = h.shape[0]

    def body(h_ref, g_ref, lng_ref, lnb_ref, ws_ref, bsb_ref, wi_ref, wo_ref, ho_ref, zp_ref, y_scr):
        hh = h_ref[...]
        hhat, _ = _rms(hh)
        n = (hhat * g_ref[...]).astype(BF)
        zpre = _dot(n, wi_ref[...])
        zp_ref[...] = zpre.astype(BF)
        z = _gelu(zpre)
        u = z[:, :D]
        y, _ = _ln(z[:, D:])
        vn = (y * lng_ref[...] + lnb_ref[...]).astype(BF)
        for c in range(tm // CHUNK):
            rs = slice(CHUNK * c, CHUNK * (c + 1))
            for gi in range(GROUPS):
                cs = slice(CHUNK * gi, CHUNK * (gi + 1))
                sv = _dot(ws_ref[gi], vn[rs, cs]) + bsb_ref[gi]
                y_scr[rs, cs] = (u[rs, cs] * sv).astype(BF)
        ho_ref[...] = hh + _dot(y_scr[...], wo_ref[...].reshape(D, D))

    vec = _const((1, D), (0, 0))
    sq = _const((GROUPS, CHUNK, CHUNK), (0, 0, 0))
    return _call(
        body, name="sgu_fwd", grid=(s // tm,),
        in_specs=[_rows(tm, D), vec, vec, vec, sq, sq, _const((D, 2 * D), (0, 0)), _whole(odo)],
        out_specs=[_rows(tm, D), _rows(tm, 2 * D)],
        out_shape=[_sds((s, D), F32), _sds((s, 2 * D), BF)],
        scratch_shapes=[pltpu.VMEM((tm, D), BF)],
        args=(h, g, lng, lnb, ws, bsb, wod, odo), carry=carry)


def _sgu_bwd(h, dh, zpre, g, lng, lnb, ws, wst, bsb, wod, odo, tm, carry=None):
    s = h.shape[0]

    def body(h_ref, dh_ref, zp_ref, g_ref, lng_ref, lnb_ref, ws_ref, wst_ref, bsb_ref, wi_ref, wo_ref,
             dhi_ref, dws_ref, dbs_ref, dlng_ref, dlnb_ref, dg_ref, gwi_ref, gwo_ref,
             y_scr, dz_scr, dvn_scr, iacc, oacc, bacc):
        i = pl.program_id(0)

        @pl.when(_first(i))
        def _():
            for ref in (iacc, oacc, bacc, dws_ref, dlng_ref, dlnb_ref, dg_ref):
                ref[...] = jnp.zeros_like(ref)

        gg = g_ref[...]
        dho = dh_ref[...]
        dhb = dho.astype(BF)
        zpre_v = zp_ref[...].astype(F32)
        z = _gelu(zpre_v)
        u = z[:, :D]
        yl, rstd = _ln(z[:, D:])
        vn = (yl * lng_ref[...] + lnb_ref[...]).astype(BF)
        dy = _dot_nt(dhb, wo_ref[...].reshape(D, D))
        for c in range(tm // CHUNK):
            rs = slice(CHUNK * c, CHUNK * (c + 1))
            for gi in range(GROUPS):
                cs = slice(CHUNK * gi, CHUNK * (gi + 1))
                vb = vn[rs, cs]
                sv = _dot(ws_ref[gi], vb) + bsb_ref[gi]
                ub = u[rs, cs]
                dyb = dy[rs, cs]
                y_scr[rs, cs] = (ub * sv).astype(BF)
                dz_scr[rs, cs] = dyb * sv
                dsv = dyb * ub
                bacc[gi] += dsv
                dsvb = dsv.astype(BF)
                dws_ref[gi] += _dot_nt(dsvb, vb)
                dvn_scr[rs, cs] = _dot(wst_ref[gi], dsvb)
        oacc[...] += _dot_tn(y_scr[...], dhb)
        dvn = dvn_scr[...]
        dlng_ref[...] += _colsum(dvn * yl)
        dlnb_ref[...] += _colsum(dvn)
        dz_scr[:, D:] = _ln_bwd(dvn * lng_ref[...], yl, rstd)
        dzb = (dz_scr[...] * _gelu_grad(zpre_v)).astype(BF)
        hhat, rr = _rms(h_ref[...])
        n = (hhat * gg).astype(BF)
        iacc[...] += _dot_tn(n, dzb)
        dn = _dot_nt(dzb, wi_ref[...])
        dg_ref[...] += _colsum(dn * hhat)
        dhi_ref[...] = dho + _rms_bwd(dn, hhat, rr, gg)

        @pl.when(_last(i))
        def _():
            for k in range(NSH):
                gwi_ref[k] = iacc[:, 512 * k:512 * (k + 1)].astype(BF)
            gwo_ref[...] = oacc[...].astype(BF).reshape(NSH, 256, D)
            dbs_ref[...] = jnp.sum(bacc[...], axis=-1)

    vec = _const((1, D), (0, 0))
    sq = _const((GROUPS, CHUNK, CHUNK), (0, 0, 0))
    return _call(
        body, name="sgu_bwd", grid=(s // tm,),
        in_specs=[_rows(tm, D), _rows(tm, D), _rows(tm, 2 * D), vec, vec, vec, sq, sq, sq,
                  _const((D, 2 * D), (0, 0)), _whole(odo)],
        out_specs=[_rows(tm, D), sq, _const((GROUPS, CHUNK), (0, 0)), vec, vec, vec,
                   _const((NSH, D, 512), (0, 0, 0)), _whole(odo)],
        out_shape=[_sds((s, D), F32), _sds((GROUPS, CHUNK, CHUNK), F32), _sds((GROUPS, CHUNK), F32),
                   _sds((1, D), F32), _sds((1, D), F32), _sds((1, D), F32), _sds((NSH, D, 512), BF),
                   _sds(odo.shape, BF)],
        scratch_shapes=[pltpu.VMEM((tm, D), BF), pltpu.VMEM((tm, 2 * D), F32),
                        pltpu.VMEM((tm, D), F32), pltpu.VMEM((D, 2 * D), F32), pltpu.VMEM((D, D), F32),
                        pltpu.VMEM((GROUPS, CHUNK, CHUNK), F32)],
        args=(h, dh, zpre, g, lng, lnb, ws, wst, bsb, wod, odo), carry=carry)


def _own_plane(shape):
    return pl.BlockSpec((1,) + shape, lambda i, chip_ref: (chip_ref[0], 0, 0), pipeline_mode=pl.Buffered(1))


def _pack_rows(chip_idx, w_down, w_gate_t, w_up_t, xq, xo, xk, xv, ev_out, od_out, first):
    nf = len(first)

    def body(chip_ref, wd_ref, wg_ref, wu_ref, xq_ref, xo_ref, xk_ref, xv_ref, ev_ref, od_ref, *rest):
        (evo_ref, odo_ref, dn0_ref, dn1_ref, wg0_ref, wg1_ref, wu0_ref, wu1_ref, xa0_ref,
         xa1_ref) = rest[nf:nf + 10]
        gathered = rest[nf + 10:2 * nf + 10]
        sems = rest[2 * nf + 10:]
        _gather_start(gathered, sems)
        evo_ref[0] = ev_ref[0].astype(BF)
        odo_ref[0] = od_ref[0].astype(BF)
        for src, outs in ((wd_ref, (dn0_ref, dn1_ref)), (wg_ref, (wg0_ref, wg1_ref)), (wu_ref, (wu0_ref, wu1_ref))):
            for layer, o_ref in enumerate(outs):
                o_ref[0] = src[layer].astype(BF)
        for layer, xa_ref in enumerate((xa0_ref, xa1_ref)):
            for item, src in ((XQ, xq_ref), (XO, xo_ref), (XK, xk_ref), (XV, xv_ref)):
                xa_ref[0, 256 * item:256 * (item + 1), :] = src[layer].astype(BF)
        _gather_forward(gathered, sems)
        _gather_drain(gathered, sems)

    ins = (w_down, w_gate_t, w_up_t, xq, xo, xk, xv, ev_out, od_out)
    shapes = [(256, D), (256, D)] + [(FSH, D)] * 6 + [(4 * 256, D), (4 * 256, D)]
    grid_spec = pltpu.PrefetchScalarGridSpec(
        num_scalar_prefetch=1, grid=(1,), in_specs=[_whole(a) for a in ins] + [ANY] * nf,
        out_specs=[_own_plane(sh) for sh in shapes] + [ANY] * nf,
        scratch_shapes=_gather_sems(nf))
    outs = pl.pallas_call(
        body, name="pack_rows", grid_spec=grid_spec,
        out_shape=[_sds((NSH,) + sh, BF) for sh in shapes] + [_sds(a.shape, a.dtype) for a in first],
        input_output_aliases={1 + len(ins) + t: len(shapes) + t for t in range(nf)},
        compiler_params=_params(),
    )(chip_idx, *ins, *first)
    return outs[:len(shapes)], outs[len(shapes):]


def _pack_cols(chip_idx, od_in, ev_in, conv_a, conv_b, ln_g, ln_b):
    def body(chip_ref, od_ref, ev_ref, ca_ref, cb_ref, lg_ref, lb_ref, cev_ref, cod_ref, sp_ref):
        cev_ref[0] = ev_ref[0].astype(BF)
        cod_ref[0] = od_ref[0].astype(BF)
        sp_ref[...] = jnp.zeros_like(sp_ref)
        sp_ref[0, 0:KA, 0:128] = ca_ref[0]
        sp_ref[0, 32:32 + KB, 0:128] = cb_ref[0]
        sp_ref[0, 40:41, :] = lg_ref[...]
        sp_ref[0, 41:42, :] = lb_ref[...]

    ins = (od_in, ev_in, conv_a, conv_b, ln_g, ln_b)
    shapes = [(D, 640), (D, 512)]
    grid_spec = pltpu.PrefetchScalarGridSpec(
        num_scalar_prefetch=1, grid=(1,), in_specs=[_whole(a) for a in ins],
        out_specs=[_own_plane(sh) for sh in shapes] + [_own_plane((SP_ROWS, 256))])
    return pl.pallas_call(
        body, name="pack_cols", grid_spec=grid_spec,
        out_shape=[_sds((NSH,) + sh, BF) for sh in shapes] + [_sds((NSH, SP_ROWS, 256), F32)],
        compiler_params=_params(),
    )(chip_idx, *ins)


def _assemble(piece, parts, name):
    tr = 256
    cols = piece.shape[2]

    def body(p_ref, *outs):
        for k in range(NSH):
            for (off, width), o_ref in zip(parts, outs):
                o_ref[:, width * k:width * (k + 1)] = p_ref[k, :, off:off + width]

    return _call(
        body, name=name, grid=(D // tr,),
        in_specs=[pl.BlockSpec((NSH, tr, cols), lambda i: (0, i, 0))],
        out_specs=[pl.BlockSpec((tr, NSH * width), lambda i: (i, 0)) for _, width in parts],
        out_shape=[_sds((D, NSH * width), BF) for _, width in parts],
        args=(piece,))


def _blocks_of(rows, cols, itemsize):
    rb = rows
    while rb * cols * itemsize > (2 << 20) and rb % 2 == 0 and (rb // 2) % 16 == 0:
        rb //= 2
    return rb


def _add_half(full, recv, c_idx, name):
    p, r, cols = full.shape
    rb = _blocks_of(r // 2, cols, 4)
    nb = (r // 2) // rb

    def body(c_ref, a_ref, b_ref, o_ref):
        o_ref[...] = (a_ref[...].astype(F32) + b_ref[...].astype(F32)).astype(o_ref.dtype)

    grid_spec = pltpu.PrefetchScalarGridSpec(
        num_scalar_prefetch=1, grid=(p, nb),
        in_specs=[pl.BlockSpec((1, rb, cols), lambda j, i, c_ref: (j, c_ref[0] * nb + i, 0)),
                  pl.BlockSpec((1, rb, cols), lambda j, i, c_ref: (j, i, 0))],
        out_specs=pl.BlockSpec((1, rb, cols), lambda j, i, c_ref: (j, i, 0)))
    return pl.pallas_call(
        body, name=name, grid_spec=grid_spec, out_shape=_sds(recv.shape, full.dtype),
        compiler_params=_params(2),
    )(c_idx, full, recv)


def _sum_chips(own, got, plane_c, name):
    _, r, cols = own.shape
    rb = _blocks_of(r, cols, 4)
    nb = r // rb

    def body(s_ref, a_ref, b_ref, o_ref):
        f = lambda v: v.astype(F32)
        o_ref[...] = (f(a_ref[0]) + f(b_ref[1])) + (f(b_ref[0]) + f(b_ref[2]))

    grid_spec = pltpu.PrefetchScalarGridSpec(
        num_scalar_prefetch=1, grid=(nb,),
        in_specs=[pl.BlockSpec((1, rb, cols), lambda i, s_ref: (s_ref[0], i, 0)),
                  pl.BlockSpec((3, rb, cols), lambda i, s_ref: (0, i, 0))],
        out_specs=pl.BlockSpec((rb, cols), lambda i, s_ref: (s_ref[1] * nb + i, 0)))
    return pl.pallas_call(
        body, name=name, grid_spec=grid_spec, out_shape=_sds((2 * r, cols), F32),
        compiler_params=_params(1),
    )(plane_c, own, got)


def _adamw_math(w, g, m, v):
    m = ADAM_B1 * m + (1.0 - ADAM_B1) * g
    v = ADAM_B2 * v + (1.0 - ADAM_B2) * (g * g)
    m_hat = m / (1.0 - ADAM_B1 ** ADAM_STEP)
    v_hat = v / (1.0 - ADAM_B2 ** ADAM_STEP)
    delta = -ADAM_LR * (m_hat / (jnp.sqrt(v_hat) + ADAM_EPS) + ADAM_WD * w)
    return delta, m, v


def _adamw_big(w, m, v, gsrcs, gblock, gidx, name, rb):
    nl, r, cols = w.shape
    assert len(gsrcs) == nl

    def body(w_ref, m_ref, v_ref, *rest):
        g_refs, (go_ref, d_ref, mo_ref, vo_ref) = rest[:nl], rest[nl:]
        g = g_refs[0][:, 0:cols]
        if nl == 2:
            g = jnp.where(pl.program_id(0) == 0, g, g_refs[1][:, 0:cols])
        delta, mn, vn = _adamw_math(w_ref[0], g, m_ref[0], v_ref[0])
        go_ref[0] = g
        d_ref[0] = delta
        mo_ref[0] = mn
        vo_ref[0] = vn

    wspec = pl.BlockSpec((1, rb, cols), lambda l, i: (l, i, 0))
    gspec = pl.BlockSpec(gblock, lambda l, i: gidx(i))
    return pl.pallas_call(
        body, name=name, grid=(nl, r // rb),
        in_specs=[wspec, wspec, wspec] + [gspec] * nl,
        out_specs=[wspec] * 4, out_shape=[_sds(w.shape, F32)] * 4,
        compiler_params=_params(2),
    )(w, m, v, *gsrcs)


_SMALL = (
    ("g_mix", (2, D), "a", (0, 2), 0, False),
    ("g_xattn", (2, D), "a", (2, 4), 0, False),
    ("g_mem", (2, D), "a", (4, 6), 0, False),
    ("g_ffn", (2, D), "a", (6, 8), 0, False),
    ("g_final", (1, D), "a", (8, 9), 0, False),
    ("ev_a_conv_w", (KA, 128), "a", (16, 16 + KA), 0, True),
    ("ev_a_conv_b", (1, DA), "a", (11, 12), 0, False),
    ("ev_a_ln_g", (1, DA), "a", (11, 12), DA, False),
    ("ev_a_ln_b", (1, DA), "a", (12, 13), 0, False),
    ("ev_b_conv_w", (KB, 128), "a", (48, 48 + KB), 0, True),
    ("ev_b_conv_b", (1, DA), "a", (12, 13), DA, False),
    ("od_c_ln_g", (1, 256), "a", (9, 10), 0, True),
    ("od_c_ln_b", (1, 256), "a", (10, 11), 0, True),
    ("od_w_s", (GROUPS * CHUNK, CHUNK), "b", (0, GROUPS * CHUNK), 0, False),
    ("od_b_s", (GROUPS, CHUNK), "a", (56, 64), 0, False),
)


def _adamw_small(ga, gb, wmv):
    ns = len(_SMALL)

    def body(*refs):
        ga_ref, gb_ref = refs[0], refs[1]
        ins = refs[2:2 + 3 * ns]
        outs = refs[2 + 3 * ns:]
        x, y, _ = _pos()
        chip = 2 * x + y
        for j, (_, shape, pack, (r0, r1), col, sharded) in enumerate(_SMALL):
            src = ga_ref if pack == "a" else gb_ref
            width = shape[1]
            if sharded:
                g = jnp.zeros(shape, F32)
                for k in range(NSH):
                    blk = src[r0:r1, col + width * k:col + width * (k + 1)]
                    g = g + jnp.where(chip == k, blk, 0.0)
            else:
                g = src[r0:r1, col:col + width]
            w_ref, m_ref, v_ref = ins[3 * j:3 * j + 3]
            delta, mn, vn = _adamw_math(w_ref[...], g, m_ref[...], v_ref[...])
            go_ref, d_ref, mo_ref, vo_ref = outs[4 * j:4 * j + 4]
            go_ref[...] = g
            d_ref[...] = delta
            mo_ref[...] = mn
            vo_ref[...] = vn

    vm = pl.BlockSpec(memory_space=pltpu.VMEM)
    flat = [a for trio in wmv for a in trio]
    out_shape = []
    for (_, shape, *_rest) in _SMALL:
        out_shape += [_sds(shape, F32)] * 4
    outs = pl.pallas_call(
        body, name="adamw_small",
        in_specs=[vm] * (2 + 3 * ns), out_specs=[vm] * (4 * ns), out_shape=out_shape,
    )(ga, gb, *flat)
    return [tuple(outs[4 * j:4 * j + 4]) for j in range(ns)]


def _small_params(gs, W):
    unshard = lambda a: jnp.transpose(a, (1, 0, 2)).reshape(a.shape[1], NSH * a.shape[2])
    od_w_s, od_b_s = W["od_w_s"], W["od_b_s"]
    return dict(
        g_mix=W["g_mix"], g_xattn=W["g_xattn"], g_mem=W["g_mem"], g_ffn=W["g_ffn"],
        g_final=W["g_final"].reshape(1, D),
        wa=unshard(gs[:, 0:32, 0:128]), ba=W["ev_a_conv_b"], lng_a=W["ev_a_ln_g"], lnb_a=W["ev_a_ln_b"],
        wb=unshard(gs[:, 32:40, 0:128]), bb=W["ev_b_conv_b"],
        lng_c=unshard(gs[:, 40:41, :]), lnb_c=unshard(gs[:, 41:42, :]),
        ws=od_w_s[0].astype(BF), wst=jnp.swapaxes(od_w_s[0], 1, 2).astype(BF),
        bsb=jnp.broadcast_to(od_b_s[0][:, :, None], (GROUPS, CHUNK, CHUNK)),
    )


def _small_grad_packs(g):
    z = lambda r, c: jnp.zeros((r, c), F32)
    wide = lambda a: jnp.concatenate([a, z(a.shape[0], D - a.shape[1])], axis=1)
    rows = [g["g_mix"], g["g_xattn"], g["g_mem"], g["g_ffn"], g["g_final"], g["od_c_ln_g"], g["od_c_ln_b"],
            jnp.concatenate([g["ev_a_conv_b"], g["ev_a_ln_g"]], axis=1),
            jnp.concatenate([g["ev_a_ln_b"], g["ev_b_conv_b"]], axis=1),
            z(3, D), wide(g["ev_a_conv_w"]), wide(g["ev_b_conv_w"]), wide(g["od_b_s"])]
    a = jnp.concatenate(rows, axis=0)
    assert a.shape == (SA_ROWS, D), a.shape
    return a, g["od_w_s"].reshape(GROUPS * CHUNK, CHUNK)


def _step(x, mem, tgt, W, P, c_idx, tm_f, tm_b, tm_w):
    row = lambda a, i: a[i:i + 1]

    sm = _small_params(P["gs"], W)
    (wev,) = _assemble(P["cev"], [(0, 640)], "assemble_ev")
    z, evo, xa0 = _ev_in_fwd(x, row(sm["g_mix"], 0), wev, 2 * tm_f, carry=("gather", [P["evo"], P["xa0"]]))
    h1, a2, c2, cat, wg0, wu0 = _ev_mix_fwd(
        x, z, sm["wa"], sm["ba"], sm["lng_a"], sm["lnb_a"], sm["wb"], sm["bb"], evo, tm_f,
        carry=("gather", [P["wg0"], P["wu0"]]))
    k0, v0 = _kv_proj(mem, row(sm["g_mem"], 0), xa0, 0)
    h2, q0, dn0, cod = _xattn_fwd(h1, row(sm["g_xattn"], 0), k0, v0, xa0, 0, 2 * tm_f,
                                  carry=("gather", [P["dn0"], P["cod"]]))
    (wod,) = _assemble(cod, [(0, 512)], "assemble_od")
    h3, gt0, up0, odo, xa1, wg1 = _ffn_fwd(h2, row(sm["g_ffn"], 0), wg0, wu0, dn0, 0, tm_f,
                                           carry=("gather", [P["odo"], P["xa1"], P["wg1"]]))
    h4, zpre, wu1, dn1 = _sgu_fwd(h3, row(sm["g_mix"], 1), sm["lng_c"], sm["lnb_c"], sm["ws"], sm["bsb"], wod,
                                  odo, tm_f, carry=("gather", [P["wu1"], P["dn1"]]))
    k1, v1 = _kv_proj(mem, row(sm["g_mem"], 1), xa1, 1)
    h5, q1 = _xattn_fwd(h4, row(sm["g_xattn"], 1), k1, v1, xa1, 1, 2 * tm_f)
    dh6, loss, dg_final, gt1, up1 = _ffn_fwd_loss(h5, row(sm["g_ffn"], 1), wg1, wu1, dn1, tgt, sm["g_final"], 1,
                                                  tm_f)

    g = {"g_final": dg_final}
    parts, gots = {}, {}
    def add(name, piece, recv):
        parts[name] = _add_half(piece, recv, c_idx, f"rs_add_{name}")
        return parts[name]

    def ffn_wgrads(layer, dgt, dup, act, nb, dh_out, carry=None):
        g_ffn, *carried = _ffn_wgrad(dgt, nb, f"ffn_wgrad_gate{layer}", tm_w, 0, carry=carry)
        (g_ffn,) = _ffn_wgrad(dup, nb, f"ffn_wgrad_up{layer}", tm_w, 1, into=g_ffn)
        (g_ffn,) = _ffn_wgrad(act, dh_out, f"ffn_wgrad_down{layer}", tm_w, 2, into=g_ffn)
        return g_ffn, carried

    dh5, dg_ffn1, dgt, dup, act, nb = _ffn_bwd_dx(h5, dh6, gt1, up1, row(sm["g_ffn"], 1), wg1, wu1, dn1, 1, tm_b)
    g_ffn1, _ = ffn_wgrads(1, dgt, dup, act, nb, dh6)
    dh4, dk, dv, dg_xa1, g_xa1, recv = _xattn_bwd(
        h4, dh5, q1, row(sm["g_xattn"], 1), k1, v1, xa1, 1, tm_f, carry=("swap", [g_ffn1]))
    add("ffn1", g_ffn1, recv)
    dg_mem1, g_xa1 = _kv_proj_bwd(mem, row(sm["g_mem"], 1), dk, dv, xa1, g_xa1, 1)
    (dh3, g["od_w_s"], g["od_b_s"], g["od_c_ln_g"], g["od_c_ln_b"], dg_mix1, g_cod, g_odo, gots["ffn1"],
     recv) = _sgu_bwd(
        h3, dh4, zpre, row(sm["g_mix"], 1), sm["lng_c"], sm["lnb_c"], sm["ws"], sm["wst"], sm["bsb"], wod, odo, tm_b,
        carry=[("xchg", [parts["ffn1"]], []), ("swap", [g_xa1])])
    add("xa1", g_xa1, recv)
    dh2, dg_ffn0, dgt, dup, act, nb, gots["xa1"], recv_cod, recv_odo = _ffn_bwd_dx(
        h2, dh3, gt0, up0, row(sm["g_ffn"], 0), wg0, wu0, dn0, 0, tm_b,
        carry=[("xchg", [parts["xa1"]], []), ("swap", [g_cod, g_odo])])
    add("cod", g_cod, recv_cod)
    add("odo", g_odo, recv_odo)
    g_ffn0, (gots["cod"], gots["odo"]) = ffn_wgrads(0, dgt, dup, act, nb, dh3,
                                                    carry=("xchg", [parts["cod"], parts["odo"]], []))
    dh1, dk, dv, dg_xa0, g_xa0, recv = _xattn_bwd(
        h1, dh2, q0, row(sm["g_xattn"], 0), k0, v0, xa0, 0, tm_f, carry=("swap", [g_ffn0]))
    add("ffn0", g_ffn0, recv)
    dg_mem0, g_xa0 = _kv_proj_bwd(mem, row(sm["g_mem"], 0), dk, dv, xa0, g_xa0, 0)
    da2, dc2, dgb, g["ev_a_ln_g"], g["ev_a_ln_b"], g_evo, recv = _ev_mix_bwd(
        dh1, cat, a2, c2, z, sm["lng_a"], sm["lnb_a"], evo, tm_f, carry=("swap", [g_xa0]))
    add("xa0", g_xa0, recv)
    (dx, g["ev_a_conv_w"], g["ev_a_conv_b"], g["ev_b_conv_w"], g["ev_b_conv_b"], dg_mix0, g_cev, gots["ffn0"],
     gots["xa0"], recv) = _ev_in_bwd(
        x, dh1, z, da2, dc2, dgb, row(sm["g_mix"], 0), sm["wa"], sm["wb"], wev, tm_b,
        carry=[("xchg", [parts["ffn0"], parts["xa0"]], []), ("swap", [g_evo])])
    add("evo", g_evo, recv)
    g["g_mix"] = jnp.concatenate([dg_mix0, dg_mix1], axis=0)
    g["g_xattn"] = jnp.concatenate([dg_xa0, dg_xa1], axis=0)
    g["g_mem"] = jnp.concatenate([dg_mem0, dg_mem1], axis=0)
    g["g_ffn"] = jnp.concatenate([dg_ffn0, dg_ffn1], axis=0)
    ga, gb = _small_grad_packs(g)
    last = (g_cev, ga[None], gb[None])
    for name, piece, recv in zip(("cev", "ga", "gb"), last, _sibling_swap(last, "rs_swap_last")):
        add(name, piece, recv)
    return loss, dx, parts, gots


def kernel(x, mem, g_mix, g_xattn, g_mem, g_ffn, g_final, ev_w_in, ev_a_conv_w, ev_a_conv_b, ev_a_ln_g, ev_a_ln_b, ev_b_conv_w, ev_b_conv_b, ev_w_out, od_w_in, od_c_ln_g, od_c_ln_b, od_w_s, od_b_s, od_w_out, xa_w_q, xa_w_k, xa_w_v, xa_w_o, ffn_w_gate, ffn_w_up, ffn_w_down, loss_target, m_g_mix, m_g_xattn, m_g_mem, m_g_ffn, m_g_final, m_ev_w_in, m_ev_a_conv_w, m_ev_a_conv_b, m_ev_a_ln_g, m_ev_a_ln_b, m_ev_b_conv_w, m_ev_b_conv_b, m_ev_w_out, m_od_w_in, m_od_c_ln_g, m_od_c_ln_b, m_od_w_s, m_od_b_s, m_od_w_out, m_xa_w_q, m_xa_w_k, m_xa_w_v, m_xa_w_o, m_ffn_w_gate, m_ffn_w_up, m_ffn_w_down, v_g_mix, v_g_xattn, v_g_mem, v_g_ffn, v_g_final, v_ev_w_in, v_ev_a_conv_w, v_ev_a_conv_b, v_ev_a_ln_g, v_ev_a_ln_b, v_ev_b_conv_w, v_ev_b_conv_b, v_ev_w_out, v_od_w_in, v_od_c_ln_g, v_od_c_ln_b, v_od_w_s, v_od_b_s, v_od_w_out, v_xa_w_q, v_xa_w_k, v_xa_w_v, v_xa_w_o, v_ffn_w_gate, v_ffn_w_up, v_ffn_w_down):
    W = dict(g_mix=g_mix, g_xattn=g_xattn, g_mem=g_mem, g_ffn=g_ffn, g_final=g_final, ev_w_in=ev_w_in,
             ev_a_conv_w=ev_a_conv_w, ev_a_conv_b=ev_a_conv_b, ev_a_ln_g=ev_a_ln_g, ev_a_ln_b=ev_a_ln_b,
             ev_b_conv_w=ev_b_conv_w, ev_b_conv_b=ev_b_conv_b, ev_w_out=ev_w_out, od_w_in=od_w_in,
             od_c_ln_g=od_c_ln_g, od_c_ln_b=od_c_ln_b, od_w_s=od_w_s, od_b_s=od_b_s, od_w_out=od_w_out,
             xa_w_q=xa_w_q, xa_w_k=xa_w_k, xa_w_v=xa_w_v, xa_w_o=xa_w_o, ffn_w_gate=ffn_w_gate,
             ffn_w_up=ffn_w_up, ffn_w_down=ffn_w_down)
    M = dict(g_mix=m_g_mix, g_xattn=m_g_xattn, g_mem=m_g_mem, g_ffn=m_g_ffn, g_final=m_g_final, ev_w_in=m_ev_w_in,
             ev_a_conv_w=m_ev_a_conv_w, ev_a_conv_b=m_ev_a_conv_b, ev_a_ln_g=m_ev_a_ln_g, ev_a_ln_b=m_ev_a_ln_b,
             ev_b_conv_w=m_ev_b_conv_w, ev_b_conv_b=m_ev_b_conv_b, ev_w_out=m_ev_w_out, od_w_in=m_od_w_in,
             od_c_ln_g=m_od_c_ln_g, od_c_ln_b=m_od_c_ln_b, od_w_s=m_od_w_s, od_b_s=m_od_b_s, od_w_out=m_od_w_out,
             xa_w_q=m_xa_w_q, xa_w_k=m_xa_w_k, xa_w_v=m_xa_w_v, xa_w_o=m_xa_w_o, ffn_w_gate=m_ffn_w_gate,
             ffn_w_up=m_ffn_w_up, ffn_w_down=m_ffn_w_down)
    V = dict(g_mix=v_g_mix, g_xattn=v_g_xattn, g_mem=v_g_mem, g_ffn=v_g_ffn, g_final=v_g_final, ev_w_in=v_ev_w_in,
             ev_a_conv_w=v_ev_a_conv_w, ev_a_conv_b=v_ev_a_conv_b, ev_a_ln_g=v_ev_a_ln_g, ev_a_ln_b=v_ev_a_ln_b,
             ev_b_conv_w=v_ev_b_conv_w, ev_b_conv_b=v_ev_b_conv_b, ev_w_out=v_ev_w_out, od_w_in=v_od_w_in,
             od_c_ln_g=v_od_c_ln_g, od_c_ln_b=v_od_c_ln_b, od_w_s=v_od_w_s, od_b_s=v_od_b_s, od_w_out=v_od_w_out,
             xa_w_q=v_xa_w_q, xa_w_k=v_xa_w_k, xa_w_v=v_xa_w_v, xa_w_o=v_xa_w_o, ffn_w_gate=v_ffn_w_gate,
             ffn_w_up=v_ffn_w_up, ffn_w_down=v_ffn_w_down)
    order = list(W)
    cx, cy, cc = _pos()
    chip = (2 * cx + cy).astype(jnp.int32)
    cc = cc.astype(jnp.int32)
    chip_idx = jnp.reshape(chip, (1,))
    c_idx = jnp.reshape(cc, (1,))

    tr = lambda a: jnp.swapaxes(a, 1, 2)
    P = {}
    cev, P["cod"], gs = _pack_cols(chip_idx, od_w_in, ev_w_in, ev_a_conv_w, ev_b_conv_w, od_c_ln_g, od_c_ln_b)
    ((P["evo"], P["odo"], P["dn0"], P["dn1"], P["wg0"], P["wg1"], P["wu0"], P["wu1"], P["xa0"], P["xa1"]),
     (P["cev"], P["gs"])) = _pack_rows(chip_idx, ffn_w_down, tr(ffn_w_gate), tr(ffn_w_up), xa_w_q, xa_w_o,
                                       xa_w_k, xa_w_v, ev_w_out, od_w_out, first=[cev, gs])

    loss, dx, parts, gots = _step(x[0], mem[0], loss_target[0], W, P, c_idx, TM_FWD, TM_BWD, TM_WGRAD)
    loss = lax.psum(loss[0, 0], ("x", "y", "c"))

    chip_c = jnp.stack([chip, cc])
    zero_c = jnp.stack([jnp.zeros((), jnp.int32), cc])
    def sums(names):
        return [_sum_chips(parts[n], gots[n], zero_c if n in ("ga", "gb") else chip_c, f"rs_sum_{n}")
                for n in names]

    last = ("evo", "cev", "ga", "gb")
    early = [n for n in parts if n not in last]
    send_sem, recv_sem, srcs_thru, lands, token = _xchg_start(
        [parts["evo"], parts["cev"]], [parts["ga"][0], parts["gb"][0]])
    red = dict(zip(early, _sibling_join(sums(early), "rs_join_early", after=(token,))))

    res = {}

    def adamw(name, srcs, gblock, gidx, rb):
        t = tr if name in ("ffn_w_gate", "ffn_w_up") else (lambda a: a)
        outs = _adamw_big(t(W[name]), t(M[name]), t(V[name]), srcs, gblock, gidx, f"adamw_{name}", rb)
        res[name] = tuple(t(o) for o in outs)

    xa = [red["xa0"], red["xa1"]]
    ffn = [red["ffn0"], red["ffn1"]]
    adamw("ffn_w_gate", ffn, (FSH, D), lambda i: (0, 0), FSH)
    adamw("ffn_w_up", ffn, (FSH, D), lambda i: (1, 0), FSH)
    adamw("ffn_w_down", ffn, (FSH, D), lambda i: (2, 0), FSH)
    adamw("xa_w_q", xa, (256, D), lambda i: (XQ, 0), 256)
    adamw("xa_w_o", xa, (256, D), lambda i: (XO, 0), 256)
    adamw("xa_w_k", xa, (256, D), lambda i: (XK, 0), 256)
    adamw("xa_w_v", xa, (256, D), lambda i: (XV, 0), 256)
    adamw("od_w_out", [red["odo"]], (256, D), lambda i: (0, 0), 256)
    adamw("od_w_in", [red["cod"]], (256, 512), lambda i: (i, 0), 256)

    srcs_done, got_last = _xchg_wait(send_sem, recv_sem, srcs_thru, lands, 2, res["od_w_in"][1])
    parts.update(zip(last, srcs_done[:2] + [a[None] for a in srcs_done[2:]]))
    gots.update(zip(last, got_last))
    red.update(zip(last, _sibling_join(sums(last), "rs_join_last")))
    adamw("ev_w_out", [red["evo"]], (256, D), lambda i: (0, 0), 256)
    adamw("ev_w_in", [red["cev"]], (256, 640), lambda i: (i, 0), 256)
    shape2 = {name: shape for name, shape, *_ in _SMALL}
    wmv = [tuple(d[name].reshape(shape2[name]) for d in (W, M, V)) for name, *_ in _SMALL]
    for (name, *_), outs in zip(_SMALL, _adamw_small(red["ga"], red["gb"], wmv)):
        res[name] = tuple(o.reshape(W[name].shape) for o in outs)

    grad_x = dx[None]
    return (loss, grad_x, *[res[n][0] for n in order], *[res[n][1] for n in order],
            *[res[n][2] for n in order], *[res[n][3] for n in order])
```

```python
import jax
import jax.numpy as jnp
from jax import lax
from jax.experimental import pallas as pl
from jax.experimental.pallas import tpu as pltpu

BF = jnp.bfloat16
F32 = jnp.float32

D = 1024
DA = 512
DZ = 2560
DFF = 2816
NSH = 4
FSH = DFF // NSH
FF_CHUNKS = ((0, 768), (768, 1536), (1536, 2304), (2304, 2816))
NMEM = 256
HEADS = 4
HD = D // HEADS
CHUNK = 128
GROUPS = 8
KA = 31
KB = 3
HALO = 16
RC = 32
RMS_EPS = 1e-6
LN_EPS = 1e-5
ATT_SCALE = HD ** -0.5
TM_FWD = 512
TM_BWD = 256
TM_WGRAD = 1024

ADAM_LR = 0.001
ADAM_B1 = 0.9
ADAM_B2 = 0.999
ADAM_EPS = 1e-08
ADAM_WD = 0.01
ADAM_STEP = 10

XQ, XO, XK, XV = 0, 1, 2, 3
SP_ROWS = 48
SA_ROWS = 64

MESH = pl.DeviceIdType.MESH
ANY = pl.BlockSpec(memory_space=pl.ANY)


def _dot(a, b):
    return jnp.dot(a, b, preferred_element_type=F32)


def _dot_nt(a, b):
    return lax.dot_general(a, b, (((1,), (1,)), ((), ())), preferred_element_type=F32)


def _dot_tn(a, b):
    return lax.dot_general(a, b, (((0,), (0,)), ((), ())), preferred_element_type=F32)


def _rms(h):
    r = lax.rsqrt(jnp.mean(h * h, axis=-1, keepdims=True) + RMS_EPS)
    return h * r, r


def _rms_bwd(dn, hhat, r, g):
    dhh = dn * g
    return r * (dhh - hhat * jnp.mean(dhh * hhat, axis=-1, keepdims=True))


def _ln(x):
    mu = jnp.mean(x, axis=-1, keepdims=True)
    xc = x - mu
    rstd = lax.rsqrt(jnp.mean(xc * xc, axis=-1, keepdims=True) + LN_EPS)
    return xc * rstd, rstd


def _ln_bwd(dy, y, rstd):
    return rstd * (dy - jnp.mean(dy, axis=-1, keepdims=True) - y * jnp.mean(dy * y, axis=-1, keepdims=True))


def _colsum(x):
    return jnp.sum(x, axis=0, keepdims=True)


def _sigmoid(x):
    return 1.0 / (1.0 + jnp.exp(-x))


_GELU_C = 0.7978845608028654
_GELU_A = 0.044715


def _gelu(x):
    t = jnp.tanh(_GELU_C * (x + _GELU_A * x * x * x))
    return 0.5 * x * (1.0 + t)


def _gelu_grad(x):
    t = jnp.tanh(_GELU_C * (x + _GELU_A * x * x * x))
    return 0.5 * (1.0 + t) + 0.5 * x * (1.0 - t * t) * _GELU_C * (1.0 + 3.0 * _GELU_A * x * x)


def _first(i):
    return i == 0


def _last(i):
    return i == pl.num_programs(0) - 1


def _const(shape, idx):
    return pl.BlockSpec(shape, lambda *_, _idx=tuple(idx): _idx, pipeline_mode=pl.Buffered(1))


def _whole(a):
    return _const(a.shape, (0,) * a.ndim)


def _rows(tm, width, col=0):
    return pl.BlockSpec((tm, width), lambda i, _c=col: (i, _c))


def _params(ndim=1, vmem_mb=56):
    return pltpu.CompilerParams(dimension_semantics=("arbitrary",) * ndim, vmem_limit_bytes=vmem_mb << 20)


def _sds(shape, dtype):
    return jax.ShapeDtypeStruct(shape, dtype)


def _pos():
    return lax.axis_index("x"), lax.axis_index("y"), lax.axis_index("c")


def _other_chips(x, y):
    return [(1 - x, y), (x, 1 - y), (1 - x, 1 - y)]


def _half(ref, hc, lead=()):
    r = ref.shape[-2] // 2
    return ref.at[(*lead, pl.ds(pl.multiple_of(hc * r, 8), r), slice(None))]


def _rcopy(src, dst, ssem, rsem, to):
    return pltpu.make_async_remote_copy(src_ref=src, dst_ref=dst, send_sem=ssem, recv_sem=rsem,
                                        device_id=to, device_id_type=MESH)


def _gather_start(bufs, sems):
    ici_s, ici_r, _, _ = sems
    x, y, c = _pos()
    me = 2 * x + y
    for k, (cx, cy) in enumerate(_other_chips(x, y)):
        for t, buf in enumerate(bufs):
            mine = _half(buf, c, (me,))
            _rcopy(mine, mine, ici_s.at[3 * t + k], ici_r.at[3 * t + k], (cx, cy, c)).start()


def _gather_forward(bufs, sems):
    ici_s, ici_r, d2d_s, d2d_r = sems
    x, y, c = _pos()
    sib = (x, y, 1 - c)
    for k, (cx, cy) in enumerate(_other_chips(x, y)):
        for t, buf in enumerate(bufs):
            landed = _half(buf, c, (2 * cx + cy,))
            _rcopy(landed, landed, ici_s.at[3 * t + k], ici_r.at[3 * t + k], (cx, cy, c)).wait_recv()
            _rcopy(landed, landed, d2d_s.at[3 * t + k], d2d_r.at[3 * t + k], sib).start()


def _gather_drain(bufs, sems):
    ici_s, ici_r, d2d_s, d2d_r = sems
    x, y, c = _pos()
    me = 2 * x + y
    sib = (x, y, 1 - c)
    chips = _other_chips(x, y)
    for k, (cx, cy) in enumerate(chips):
        for t, buf in enumerate(bufs):
            got = _half(buf, 1 - c, (2 * cx + cy,))
            _rcopy(got, got, d2d_s.at[3 * t + k], d2d_r.at[3 * t + k], sib).wait_recv()
    for k, (cx, cy) in enumerate(chips):
        for t, buf in enumerate(bufs):
            mine = _half(buf, c, (me,))
            _rcopy(mine, mine, ici_s.at[3 * t + k], ici_r.at[3 * t + k], (cx, cy, c)).wait_send()
            landed = _half(buf, c, (2 * cx + cy,))
            _rcopy(landed, landed, d2d_s.at[3 * t + k], d2d_r.at[3 * t + k], sib).wait_send()


def _xchg_copies(srcs, dsts, nbig, sems):
    ssem, rsem = sems
    x, y, c = _pos()
    cps = []
    for k, (cx, cy) in enumerate(_other_chips(x, y)):
        for t, (src, dst) in enumerate(zip(srcs, dsts)):
            s = src.at[2 * cx + cy] if t < nbig else src
            cps.append(_rcopy(s, dst.at[k], ssem.at[3 * t + k], rsem.at[3 * t + k], (cx, cy, c)))
    return cps


def _gather_sems(n):
    return [pltpu.SemaphoreType.DMA((3 * n,))] * 4


def _xchg_sems(n):
    return [pltpu.SemaphoreType.DMA((3 * n,))] * 2


def _xchg_out_shapes(items):
    return [_sds((3,) + a.shape[-2:], a.dtype) for a in items]


def _swap_copies(srcs, dsts, sems):
    ssem, rsem = sems
    x, y, c = _pos()
    cps = []
    for t, (src, dst) in enumerate(zip(srcs, dsts)):
        lead = (slice(None),) * (len(src.shape) - 2)
        cps.append(_rcopy(_half(src, 1 - c, lead), dst, ssem.at[t], rsem.at[t], (x, y, 1 - c)))
    return cps


def _swap_out_shapes(items):
    return [_sds(a.shape[:-2] + (a.shape[-2] // 2, a.shape[-1]), a.dtype) for a in items]


def _call(body, *, name, grid, in_specs, out_specs, out_shape, args, scratch_shapes=(), aliases=None, carry=None):
    in_specs, out_specs, out_shape = list(in_specs), list(out_specs), list(out_shape)
    args = list(args)
    aliases = dict(aliases or {})
    n_in, n_out, n_scr = len(args), len(out_shape), len(scratch_shapes)
    carries = [] if not carry else ([carry] if isinstance(carry, tuple) else list(carry))
    plans = []
    cin, cout_shape, sems = [], [], []
    for c in carries:
        kind = c[0]
        items = list(c[1]) + (list(c[2]) if kind == "xchg" else [])
        if kind == "gather":
            shapes, sm = [_sds(a.shape, a.dtype) for a in items], _gather_sems(len(items))
            for t in range(len(items)):
                aliases[n_in + len(cin) + t] = n_out + len(cin) + t
        elif kind == "xchg":
            shapes, sm = _xchg_out_shapes(items), _xchg_sems(len(items))
        else:
            shapes, sm = _swap_out_shapes(items), [pltpu.SemaphoreType.DMA((len(items),))] * 2
        plans.append((kind, len(cin), len(items), len(sems), len(sm), len(c[1])))
        cin += items
        cout_shape += shapes
        sems += sm
    nc = len(cin)

    def wrapped(*refs):
        ins = refs[:n_in]
        c_in = refs[n_in:n_in + nc]
        outs = refs[n_in + nc:n_in + nc + n_out]
        c_out = refs[n_in + nc + n_out:n_in + 2 * nc + n_out]
        scr = refs[n_in + 2 * nc + n_out:n_in + 2 * nc + n_out + n_scr]
        sem = refs[n_in + 2 * nc + n_out + n_scr:]
        i = pl.program_id(0)

        def copies(plan):
            kind, o0, no, s0, ns, nbig = plan
            if kind == "xchg":
                return _xchg_copies(c_in[o0:o0 + no], c_out[o0:o0 + no], nbig, sem[s0:s0 + ns])
            return _swap_copies(c_in[o0:o0 + no], c_out[o0:o0 + no], sem[s0:s0 + ns])

        if plans:
            @pl.when(_first(i))
            def _():
                for plan in plans:
                    kind, o0, no, s0, ns, _ = plan
                    if kind == "gather":
                        _gather_start(c_out[o0:o0 + no], sem[s0:s0 + ns])
                    else:
                        for cp in copies(plan):
                            cp.start()
        body(*ins, *outs, *scr)
        for plan in plans:
            kind, o0, no, s0, ns, _ = plan
            if kind == "gather":
                @pl.when(i == max(grid[0] - 3, 0))
                def _(o0=o0, no=no, s0=s0, ns=ns):
                    _gather_forward(c_out[o0:o0 + no], sem[s0:s0 + ns])
        if plans:
            @pl.when(_last(i))
            def _():
                for plan in plans:
                    kind, o0, no, s0, ns, _ = plan
                    if kind == "gather":
                        _gather_drain(c_out[o0:o0 + no], sem[s0:s0 + ns])
                    else:
                        for cp in copies(plan):
                            cp.wait()

    outs = pl.pallas_call(
        wrapped, name=name, grid=grid,
        in_specs=in_specs + [ANY] * nc, out_specs=out_specs + [ANY] * nc,
        out_shape=out_shape + cout_shape,
        scratch_shapes=list(scratch_shapes) + sems,
        input_output_aliases=aliases,
        compiler_params=_params(len(grid)),
    )(*args, *cin)
    return list(outs)


def _sibling_swap(items, name):
    n = len(items)

    def body(*refs):
        src = refs[:n]
        dst = refs[n:2 * n]
        ssem, rsem = refs[2 * n:]
        x, y, c = _pos()
        sib = (x, y, 1 - c)
        cps = []
        for t in range(n):
            lead = (slice(None),) * (len(src[t].shape) - 2)
            cp = _rcopy(_half(src[t], 1 - c, lead), dst[t], ssem.at[t], rsem.at[t], sib)
            cp.start()
            cps.append(cp)
        for cp in cps:
            cp.wait()

    return pl.pallas_call(
        body, name=name, in_specs=[ANY] * n, out_specs=[ANY] * n,
        out_shape=[_sds(a.shape[:-2] + (a.shape[-2] // 2, a.shape[-1]), a.dtype) for a in items],
        scratch_shapes=[pltpu.SemaphoreType.DMA((n,)), pltpu.SemaphoreType.DMA((n,))],
    )(*items)


_HBM = pl.BlockSpec(memory_space=pltpu.HBM)
_SEM = pl.BlockSpec(memory_space=pltpu.SEMAPHORE)
_EFFECT = pltpu.SideEffectType.DATAFLOW_SIDE_EFFECTING


def _split_copies(src, land, nbig, nx, sems):
    return (_xchg_copies(src[:nx], land[:nx], nbig, sems[0:2]) if nx else []) + \
        (_swap_copies(src[nx:], land[nx:], sems[2:4]) if len(src) > nx else [])


def _comm_start(big, small, swaps, tag):
    xs = list(big) + list(small)
    items = xs + list(swaps)
    n, nx = len(items), len(xs)

    def body(*refs):
        for cp in _split_copies(refs[:n], refs[n:2 * n], len(big), nx, refs[2 * n:2 * n + 4]):
            cp.start()
        refs[-1][...] = jnp.zeros_like(refs[-1])

    lands = _xchg_out_shapes(xs) + _swap_out_shapes(swaps)
    sem_shapes = [pltpu.SemaphoreType.DMA((max(3 * nx, 1),))] * 2 + [pltpu.SemaphoreType.DMA((max(n - nx, 1),))] * 2
    hbm = lambda a: pltpu.with_memory_space_constraint(a, pltpu.HBM)
    outs = pl.pallas_call(
        body, name=f"rs_start_{tag}",
        out_shape=sem_shapes + [pltpu.HBM(a.shape, a.dtype) for a in items]
        + [pltpu.HBM(a.shape, a.dtype) for a in lands] + [_sds((8, 128), F32)],
        in_specs=[_HBM] * (2 * n), out_specs=[_SEM] * 4 + [_HBM] * (2 * n) + [pl.BlockSpec(memory_space=pltpu.VMEM)],
        input_output_aliases={t: 4 + t for t in range(2 * n)},
        compiler_params=pltpu.CompilerParams(has_side_effects=_EFFECT),
    )(*[hbm(a) for a in items], *[hbm(lax.empty(a.shape, a.dtype)) for a in lands])
    handle = (list(outs[:4]), list(outs[4:4 + n]), list(outs[4 + n:4 + 2 * n]), len(big), nx, tag)
    return handle, outs[-1]


def _comm_wait(handle, after):
    sems, srcs, lands, nbig, nx, tag = handle
    n = len(srcs)

    def body(*refs):
        for cp in _split_copies(refs[:n], refs[n:2 * n], nbig, nx, refs[2 * n:2 * n + 4]):
            cp.wait_send()
            cp.wait_recv()

    outs = pl.pallas_call(
        body, name=f"rs_wait_{tag}",
        out_shape=[pltpu.HBM(a.shape, a.dtype) for a in srcs] + [pltpu.HBM(a.shape, a.dtype) for a in lands],
        in_specs=[_HBM] * (2 * n) + [_SEM] * 4 + [ANY], out_specs=[_HBM] * (2 * n),
        input_output_aliases={t: t for t in range(2 * n)},
        compiler_params=pltpu.CompilerParams(has_side_effects=_EFFECT),
    )(*srcs, *lands, *sems, after)
    return list(outs[:n]), list(outs[n:])


def _sibling_join(items, name, after=()):
    n = len(items)

    def body(*refs):
        dst = refs[n + len(after):2 * n + len(after)]
        ssem, rsem = refs[2 * n + len(after):]
        x, y, c = _pos()
        sib = (x, y, 1 - c)
        cps = []
        for t in range(n):
            mine = _half(dst[t], c)
            cp = _rcopy(mine, mine, ssem.at[t], rsem.at[t], sib)
            cp.start()
            cps.append(cp)
        for t, cp in enumerate(cps):
            theirs = _half(dst[t], 1 - c)
            _rcopy(theirs, theirs, ssem.at[t], rsem.at[t], sib).wait_recv()
            cp.wait_send()

    outs = pl.pallas_call(
        body, name=name, in_specs=[ANY] * (n + len(after)), out_specs=[ANY] * n,
        out_shape=[_sds(a.shape, a.dtype) for a in items],
        input_output_aliases={t: t for t in range(n)},
        scratch_shapes=[pltpu.SemaphoreType.DMA((n,))] * 2,
    )(*items, *after)
    return list(outs)


def _ev_in_fwd(h, g, wev, tm, carry=None):
    s = h.shape[0]

    def body(h_ref, g_ref, w_ref, z_ref):
        hhat, _ = _rms(h_ref[...])
        n = (hhat * g_ref[...]).astype(BF)
        z_ref[...] = _dot(n, w_ref[...]).astype(BF)

    return _call(
        body, name="ev_in_fwd", grid=(s // tm,),
        in_specs=[_rows(tm, D), _const((1, D), (0, 0)), _const((D, DZ), (0, 0))],
        out_specs=[_rows(tm, DZ)], out_shape=[_sds((s, DZ), BF)],
        args=(h, g, wev), carry=carry)


def _halo_specs(tm, width, s, col=0):
    per = tm // HALO
    nh = s // HALO
    prev = pl.BlockSpec((HALO, width), lambda i, _c=col: (jnp.maximum(i * per - 1, 0), _c))
    nxt = pl.BlockSpec((HALO, width), lambda i, _c=col: (jnp.minimum((i + 1) * per, nh - 1), _c))
    return prev, nxt


def _fill_pad(pad_ref, prev, main, nxt, i, tm):
    pad_ref[pl.ds(0, HALO), :] = jnp.where(_first(i), 0.0, prev)
    pad_ref[pl.ds(HALO, tm), :] = main
    pad_ref[pl.ds(HALO + tm, HALO), :] = jnp.where(_last(i), 0.0, nxt)


def _shift8(xs_ref, pad_ref, tm):
    for j in range(8):
        xs_ref[j] = pad_ref[pl.ds(j, tm + 2 * HALO - 8), :]


def _tap(xs_ref, r, off):
    return xs_ref[off % 8, pl.ds(r + 8 * (off // 8), RC), :]


def _glu_a(z):
    return z[:, :DA] * _sigmoid(z[:, DA:2 * DA])


def _gate_c(z):
    return z[:, 2048:] * z[:, 1024:1536]


def _ev_mix_fwd(h, z, wa, ba, lng, lnb, wb, bb, evo, tm, carry=None):
    s = h.shape[0]

    def body(h_ref, z_ref, zp_ref, zn_ref, wa_ref, ba_ref, lng_ref, lnb_ref, wb_ref, bb_ref, wo_ref,
             h1_ref, a2_ref, c2_ref, cat_ref, apad, cpad, c2f, xs):
        i = pl.program_id(0)
        zm = z_ref[...].astype(F32)
        zp = zp_ref[...].astype(F32)
        zn = zn_ref[...].astype(F32)
        _fill_pad(apad, _glu_a(zp), _glu_a(zm), _glu_a(zn), i, tm)
        _fill_pad(cpad, _gate_c(zp), _gate_c(zm), _gate_c(zn), i, tm)
        _shift8(xs, apad, tm)
        for r in range(0, tm, RC):
            acc = jnp.zeros((RC, DA), F32) + ba_ref[...]
            for k in range(KA):
                acc = acc + _tap(xs, r, k + 1) * wa_ref[pl.ds(k, 1), :]
            a2_ref[pl.ds(r, RC), :] = acc
            acc = jnp.zeros((RC, DA), F32) + bb_ref[...]
            for k in range(KB):
                acc = acc + cpad[pl.ds(r + HALO - 1 + k, RC), :] * wb_ref[pl.ds(k, 1), :]
            c2f[pl.ds(r, RC), :] = acc
        y, _ = _ln(a2_ref[...])
        a3 = y * lng_ref[...] + lnb_ref[...]
        cat_ref[:, :DA] = (a3 * _sigmoid(a3)).astype(BF)
        c2 = c2f[...]
        c2_ref[...] = c2.astype(BF)
        cat_ref[:, DA:] = (zm[:, 1536:2048] * c2).astype(BF)
        h1_ref[...] = h_ref[...] + _dot(cat_ref[...], wo_ref[...].reshape(D, D))

    zp_spec, zn_spec = _halo_specs(tm, DZ, s)
    small = lambda rows: _const((rows, DA), (0, 0))
    return _call(
        body, name="ev_mix_fwd", grid=(s // tm,),
        in_specs=[_rows(tm, D), _rows(tm, DZ), zp_spec, zn_spec, small(32), small(1), small(1), small(1),
                  small(8), small(1), _whole(evo)],
        out_specs=[_rows(tm, D), _rows(tm, DA), _rows(tm, DA), _rows(tm, D)],
        out_shape=[_sds((s, D), F32), _sds((s, DA), F32), _sds((s, DA), BF), _sds((s, D), BF)],
        scratch_shapes=[pltpu.VMEM((tm + 2 * HALO, DA), F32), pltpu.VMEM((tm + 2 * HALO, DA), F32),
                        pltpu.VMEM((tm, DA), F32), pltpu.VMEM((8, tm + 2 * HALO - 8, DA), F32)],
        args=(h, z, z, z, wa, ba, lng, lnb, wb, bb, evo), carry=carry)


def _ev_mix_bwd(dh, cat, a2, c2, z, lng, lnb, evo, tm, carry=None):
    s = dh.shape[0]

    def body(dh_ref, cat_ref, a2_ref, c2_ref, zgb_ref, lng_ref, lnb_ref, wo_ref,
             da2_ref, dc2_ref, dgb_ref, dlng_ref, dlnb_ref, gwo_ref, wacc):
        i = pl.program_id(0)

        @pl.when(_first(i))
        def _():
            wacc[...] = jnp.zeros_like(wacc)
            dlng_ref[...] = jnp.zeros_like(dlng_ref)
            dlnb_ref[...] = jnp.zeros_like(dlnb_ref)

        dhb = dh_ref[...].astype(BF)
        dcat = _dot_nt(dhb, wo_ref[...].reshape(D, D))
        wacc[...] += _dot_tn(cat_ref[...], dhb)
        y, rstd = _ln(a2_ref[...])
        a3 = y * lng_ref[...] + lnb_ref[...]
        sg = _sigmoid(a3)
        da3 = dcat[:, :DA] * (sg * (1.0 + a3 * (1.0 - sg)))
        dlng_ref[...] += _colsum(da3 * y)
        dlnb_ref[...] += _colsum(da3)
        da2_ref[...] = _ln_bwd(da3 * lng_ref[...], y, rstd).astype(BF)
        db = dcat[:, DA:]
        dc2_ref[...] = (db * zgb_ref[...].astype(F32)).astype(BF)
        dgb_ref[...] = (db * c2_ref[...].astype(F32)).astype(BF)

        @pl.when(_last(i))
        def _():
            gwo_ref[...] = wacc[...].astype(BF).reshape(NSH, 256, D)

    small = _const((1, DA), (0, 0))
    return _call(
        body, name="ev_mix_bwd", grid=(s // tm,),
        in_specs=[_rows(tm, D), _rows(tm, D), _rows(tm, DA), _rows(tm, DA), _rows(tm, DA, 3), small, small,
                  _whole(evo)],
        out_specs=[_rows(tm, DA), _rows(tm, DA), _rows(tm, DA), small, small, _whole(evo)],
        out_shape=[_sds((s, DA), BF), _sds((s, DA), BF), _sds((s, DA), BF), _sds((1, DA), F32),
                   _sds((1, DA), F32), _sds(evo.shape, BF)],
        scratch_shapes=[pltpu.VMEM((D, D), F32)],
        args=(dh, cat, a2, c2, z, lng, lnb, evo), carry=carry)


def _ev_in_bwd(h, dh, z, da2, dc2, dgb, g, wa, wb, wev, tm, after):
    s = h.shape[0]

    def body(h_ref, dh_ref, z_ref, zp_ref, zn_ref, da_ref, dap_ref, dan_ref, dc_ref, dcp_ref, dcn_ref,
             dgb_ref, g_ref, wa_ref, wb_ref, w_ref, after_ref,
             dh0_ref, dwa_ref, dba_ref, dwb_ref, dbb_ref, dg_ref, gw_ref,
             apad, cpad, dapad, dcpad, dz, wacc, dwa_acc, dwb_acc, xs):
        i = pl.program_id(0)

        @pl.when(_first(i))
        def _():
            wacc[...] = jnp.zeros_like(wacc)
            dwa_acc[...] = jnp.zeros_like(dwa_acc)
            dwb_acc[...] = jnp.zeros_like(dwb_acc)
            dba_ref[...] = jnp.zeros_like(dba_ref)
            dbb_ref[...] = jnp.zeros_like(dbb_ref)
            dg_ref[...] = jnp.zeros_like(dg_ref)

        zm = z_ref[...].astype(F32)
        zp = zp_ref[...].astype(F32)
        zn = zn_ref[...].astype(F32)
        _fill_pad(apad, _glu_a(zp), _glu_a(zm), _glu_a(zn), i, tm)
        _fill_pad(cpad, _gate_c(zp), _gate_c(zm), _gate_c(zn), i, tm)
        da2 = da_ref[...].astype(F32)
        dc2 = dc_ref[...].astype(F32)
        _fill_pad(dapad, dap_ref[...].astype(F32), da2, dan_ref[...].astype(F32), i, tm)
        _fill_pad(dcpad, dcp_ref[...].astype(F32), dc2, dcn_ref[...].astype(F32), i, tm)
        dba_ref[...] += _colsum(da2)
        dbb_ref[...] += _colsum(dc2)

        sg = _sigmoid(zm[:, DA:2 * DA])
        aval = zm[:, :DA]
        _shift8(xs, dapad, tm)
        for r in range(0, tm, RC):
            acc = jnp.zeros((RC, DA), F32)
            for k in range(KA):
                acc = acc + _tap(xs, r, 2 * HALO - 1 - k) * wa_ref[pl.ds(k, 1), :]
            sgr = sg[r:r + RC]
            dz[pl.ds(r, RC), 0:DA] = (acc * sgr).astype(BF)
            dz[pl.ds(r, RC), DA:2 * DA] = (acc * aval[r:r + RC] * sgr * (1.0 - sgr)).astype(BF)
            acc = jnp.zeros((RC, DA), F32)
            for k in range(KB):
                acc = acc + dcpad[pl.ds(r + HALO + 1 - k, RC), :] * wb_ref[pl.ds(k, 1), :]
            dz[pl.ds(r, RC), 1024:1536] = (acc * zm[r:r + RC, 2048:]).astype(BF)
            dz[pl.ds(r, RC), 2048:2560] = (acc * zm[r:r + RC, 1024:1536]).astype(BF)
        dz[:, 1536:2048] = dgb_ref[...]

        _shift8(xs, apad, tm)
        for k in range(KA):
            acc = jnp.zeros((8, DA), F32)
            for r in range(0, tm, RC):
                p = _tap(xs, r, k + 1) * dapad[pl.ds(r + HALO, RC), :]
                acc = acc + p.reshape(RC // 8, 8, DA).sum(axis=0)
            dwa_acc[k] += acc
        for k in range(KB):
            acc = jnp.zeros((8, DA), F32)
            for r in range(0, tm, RC):
                p = cpad[pl.ds(r + HALO - 1 + k, RC), :] * dcpad[pl.ds(r + HALO, RC), :]
                acc = acc + p.reshape(RC // 8, 8, DA).sum(axis=0)
            dwb_acc[k] += acc

        gg = g_ref[...]
        hhat, rr = _rms(h_ref[...])
        n = (hhat * gg).astype(BF)
        dzb = dz[...]
        wacc[...] += _dot_tn(n, dzb)
        dn = _dot_nt(dzb, w_ref[...])
        dg_ref[...] += _colsum(dn * hhat)
        dh0_ref[...] = dh_ref[...] + _rms_bwd(dn, hhat, rr, gg)

        @pl.when(_last(i))
        def _():
            for k in range(NSH):
                gw_ref[k] = wacc[:, 640 * k:640 * (k + 1)].astype(BF)
            dwa_ref[...] = jnp.sum(dwa_acc[...], axis=1)
            dwb_ref[...] = jnp.sum(dwb_acc[...], axis=1)

    zp_spec, zn_spec = _halo_specs(tm, DZ, s)
    hp_spec, hn_spec = _halo_specs(tm, DA, s)
    small = lambda rows: _const((rows, DA), (0, 0))
    return _call(
        body, name="ev_in_bwd", grid=(s // tm,),
        in_specs=[_rows(tm, D), _rows(tm, D), _rows(tm, DZ), zp_spec, zn_spec,
                  _rows(tm, DA), hp_spec, hn_spec, _rows(tm, DA), hp_spec, hn_spec, _rows(tm, DA),
                  _const((1, D), (0, 0)), small(32), small(8), _const((D, DZ), (0, 0)), ANY],
        out_specs=[_rows(tm, D), small(32), small(1), small(8), small(1), _const((1, D), (0, 0)),
                   _const((NSH, D, 640), (0, 0, 0))],
        out_shape=[_sds((s, D), F32), _sds((32, DA), F32), _sds((1, DA), F32), _sds((8, DA), F32),
                   _sds((1, DA), F32), _sds((1, D), F32), _sds((NSH, D, 640), BF)],
        scratch_shapes=[pltpu.VMEM((tm + 2 * HALO, DA), F32)] * 4 + [
            pltpu.VMEM((tm, DZ), BF), pltpu.VMEM((D, DZ), F32),
            pltpu.VMEM((32, 8, DA), F32), pltpu.VMEM((8, 8, DA), F32),
            pltpu.VMEM((8, tm + 2 * HALO - 8, DA), F32)],
        args=(h, dh, z, z, z, da2, da2, da2, dc2, dc2, dc2, dgb, g, wa, wb, wev, after))


def _xa_spec(item):
    return _const((NSH, 256, D), (0, item, 0))


def _kv_proj(mem, g, xa, layer):
    def body(mem_ref, g_ref, wk_ref, wv_ref, k_ref, v_ref):
        mhat, _ = _rms(mem_ref[...])
        mn = (mhat * g_ref[...]).astype(BF)
        k_ref[...] = _dot(mn, wk_ref[...].reshape(D, D)).astype(BF)
        v_ref[...] = _dot(mn, wv_ref[...].reshape(D, D)).astype(BF)

    full = _const((NMEM, D), (0, 0))
    return _call(
        body, name=f"kv_proj{layer}", grid=(1,),
        in_specs=[full, _const((1, D), (0, 0)), _xa_spec(XK), _xa_spec(XV)],
        out_specs=[full, full], out_shape=[_sds((NMEM, D), BF), _sds((NMEM, D), BF)],
        args=(mem, g, xa, xa))


def _softmax_rows(sc):
    m = jnp.max(sc, axis=-1, keepdims=True)
    e = jnp.exp(sc - m)
    return e / jnp.sum(e, axis=-1, keepdims=True)


def _xattn_fwd(h, g, k, v, xa, layer, tm, carry=None):
    s = h.shape[0]

    def body(h_ref, g_ref, k_ref, v_ref, wq_ref, wo_ref, ho_ref, q_ref, o_scr):
        hh = h_ref[...]
        hhat, _ = _rms(hh)
        n = (hhat * g_ref[...]).astype(BF)
        q = _dot(n, wq_ref[...].reshape(D, D)).astype(BF)
        q_ref[...] = q
        for hd in range(HEADS):
            cs = slice(HD * hd, HD * (hd + 1))
            p = _softmax_rows(_dot_nt(q[:, cs], k_ref[:, cs]) * ATT_SCALE)
            o_scr[:, cs] = _dot(p.astype(BF), v_ref[:, cs]).astype(BF)
        ho_ref[...] = hh + _dot(o_scr[...], wo_ref[...].reshape(D, D))

    kvs = _const((NMEM, D), (0, 0))
    return _call(
        body, name=f"xattn_fwd{layer}", grid=(s // tm,),
        in_specs=[_rows(tm, D), _const((1, D), (0, 0)), kvs, kvs, _xa_spec(XQ), _xa_spec(XO)],
        out_specs=[_rows(tm, D), _rows(tm, D)],
        out_shape=[_sds((s, D), F32), _sds((s, D), BF)],
        scratch_shapes=[pltpu.VMEM((tm, D), BF)],
        args=(h, g, k, v, xa, xa), carry=carry)


def _xattn_bwd(h, dh, q, g, k, v, xa, layer, tm, carry=None):
    s = h.shape[0]

    def body(h_ref, dh_ref, q_ref, g_ref, k_ref, v_ref, wq_ref, wo_ref,
             dhi_ref, dk_ref, dv_ref, dg_ref, gqo_ref, o_scr, dq_scr, qacc, oacc):
        i = pl.program_id(0)

        @pl.when(_first(i))
        def _():
            qacc[...] = jnp.zeros_like(qacc)
            oacc[...] = jnp.zeros_like(oacc)
            dk_ref[...] = jnp.zeros_like(dk_ref)
            dv_ref[...] = jnp.zeros_like(dv_ref)
            dg_ref[...] = jnp.zeros_like(dg_ref)

        gg = g_ref[...]
        dho = dh_ref[...]
        dhb = dho.astype(BF)
        q = q_ref[...]
        do = _dot_nt(dhb, wo_ref[...].reshape(D, D)).astype(BF)
        for hd in range(HEADS):
            cs = slice(HD * hd, HD * (hd + 1))
            kh = k_ref[:, cs]
            vh = v_ref[:, cs]
            p = _softmax_rows(_dot_nt(q[:, cs], kh) * ATT_SCALE)
            pb = p.astype(BF)
            o_scr[:, cs] = _dot(pb, vh).astype(BF)
            doh = do[:, cs]
            dp = _dot_nt(doh, vh)
            dv_ref[:, cs] += _dot_tn(pb, doh)
            ds = (p * (dp - jnp.sum(dp * p, axis=-1, keepdims=True)) * ATT_SCALE).astype(BF)
            dq_scr[:, cs] = _dot(ds, kh).astype(BF)
            dk_ref[:, cs] += _dot_tn(ds, q[:, cs])
        oacc[...] += _dot_tn(o_scr[...], dhb)
        hhat, rr = _rms(h_ref[...])
        n = (hhat * gg).astype(BF)
        dq = dq_scr[...]
        qacc[...] += _dot_tn(n, dq)
        dn = _dot_nt(dq, wq_ref[...].reshape(D, D))
        dg_ref[...] += _colsum(dn * hhat)
        dhi_ref[...] = dho + _rms_bwd(dn, hhat, rr, gg)

        @pl.when(_last(i))
        def _():
            gqo_ref[:, 0:256, :] = qacc[...].astype(BF).reshape(NSH, 256, D)
            gqo_ref[:, 256:512, :] = oacc[...].astype(BF).reshape(NSH, 256, D)

    kvs = _const((NMEM, D), (0, 0))
    return _call(
        body, name=f"xattn_bwd{layer}", grid=(s // tm,),
        in_specs=[_rows(tm, D), _rows(tm, D), _rows(tm, D), _const((1, D), (0, 0)), kvs, kvs,
                  _xa_spec(XQ), _xa_spec(XO)],
        out_specs=[_rows(tm, D), kvs, kvs, _const((1, D), (0, 0)), _const((NSH, 512, D), (0, 0, 0))],
        out_shape=[_sds((s, D), F32), _sds((NMEM, D), F32), _sds((NMEM, D), F32), _sds((1, D), F32),
                   _sds(xa.shape, BF)],
        scratch_shapes=[pltpu.VMEM((tm, D), BF), pltpu.VMEM((tm, D), BF), pltpu.VMEM((D, D), F32),
                        pltpu.VMEM((D, D), F32)],
        args=(h, dh, q, g, k, v, xa, xa), carry=carry)


def _kv_proj_bwd(mem, g, dk, dv, xa, gxa, layer):
    def body(mem_ref, g_ref, dk_ref, dv_ref, wk_ref, wv_ref, gxa_in, dg_ref, gkv_ref):
        mhat, _ = _rms(mem_ref[...])
        mn = (mhat * g_ref[...]).astype(BF)
        dkb = dk_ref[...].astype(BF)
        dvb = dv_ref[...].astype(BF)
        gkv_ref[:, 0:256, :] = _dot_tn(mn, dkb).astype(BF).reshape(NSH, 256, D)
        gkv_ref[:, 256:512, :] = _dot_tn(mn, dvb).astype(BF).reshape(NSH, 256, D)
        dmn = _dot_nt(dkb, wk_ref[...].reshape(D, D)) + _dot_nt(dvb, wv_ref[...].reshape(D, D))
        dg_ref[...] = _colsum(dmn * mhat)

    full = _const((NMEM, D), (0, 0))
    return _call(
        body, name=f"kv_proj_bwd{layer}", grid=(1,),
        in_specs=[full, _const((1, D), (0, 0)), full, full, _xa_spec(XK), _xa_spec(XV), ANY],
        out_specs=[_const((1, D), (0, 0)), _const((NSH, 512, D), (0, 1, 0))],
        out_shape=[_sds((1, D), F32), _sds(gxa.shape, BF)],
        aliases={6: 1},
        args=(mem, g, dk, dv, xa, xa, gxa))


def _ffn_block(h_ref, g_ref, wg_ref, wu_ref, wd_ref, gt_ref, up_ref):
    hh = h_ref[...]
    hhat, _ = _rms(hh)
    n = (hhat * g_ref[...]).astype(BF)
    out = hh
    for c0, c1 in FF_CHUNKS:
        gt = _dot_nt(n, wg_ref[c0:c1, :])
        up = _dot_nt(n, wu_ref[c0:c1, :])
        gt_ref[:, c0:c1] = gt.astype(BF)
        up_ref[:, c0:c1] = up.astype(BF)
        act = (gt * _sigmoid(gt) * up).astype(BF)
        out = out + _dot(act, wd_ref[c0:c1, :])
    return out


def _ffn_fwd(h, g, wg, wu, dn, layer, tm, carry=None):
    s = h.shape[0]
    wg, wu, dn = (a.reshape(DFF, D) for a in (wg, wu, dn))

    def body(h_ref, g_ref, wg_ref, wu_ref, wd_ref, ho_ref, gt_ref, up_ref):
        ho_ref[...] = _ffn_block(h_ref, g_ref, wg_ref, wu_ref, wd_ref, gt_ref, up_ref)

    return _call(
        body, name=f"ffn_fwd{layer}", grid=(s // tm,),
        in_specs=[_rows(tm, D), _const((1, D), (0, 0)), _whole(wg), _whole(wu), _whole(dn)],
        out_specs=[_rows(tm, D), _rows(tm, DFF), _rows(tm, DFF)],
        out_shape=[_sds((s, D), F32), _sds((s, DFF), BF), _sds((s, DFF), BF)],
        args=(h, g, wg, wu, dn), carry=carry)


def _ffn_fwd_loss(h, g, wg, wu, dn, tgt, g_final, layer, tm):
    s = h.shape[0]
    wg, wu, dn = (a.reshape(DFF, D) for a in (wg, wu, dn))

    def body(h_ref, g_ref, wg_ref, wu_ref, wd_ref, t_ref, gf_ref, dh_ref, loss_ref, dg_ref, gt_ref, up_ref):
        i = pl.program_id(0)

        @pl.when(_first(i))
        def _():
            loss_ref[...] = jnp.zeros_like(loss_ref)
            dg_ref[...] = jnp.zeros_like(dg_ref)

        out = _ffn_block(h_ref, g_ref, wg_ref, wu_ref, wd_ref, gt_ref, up_ref)
        gg = gf_ref[...]
        hhat, rr = _rms(out)
        diff = hhat * gg - t_ref[...]
        loss_ref[...] += jnp.sum(diff * diff) * (0.5 / D)
        dout = diff * (1.0 / D)
        dg_ref[...] += _colsum(dout * hhat)
        dh_ref[...] = _rms_bwd(dout, hhat, rr, gg)

    vec = _const((1, D), (0, 0))
    return _call(
        body, name=f"ffn_fwd_loss{layer}", grid=(s // tm,),
        in_specs=[_rows(tm, D), vec, _whole(wg), _whole(wu), _whole(dn), _rows(tm, D), vec],
        out_specs=[_rows(tm, D), _const((1, 128), (0, 0)), vec, _rows(tm, DFF), _rows(tm, DFF)],
        out_shape=[_sds((s, D), F32), _sds((1, 128), F32), _sds((1, D), F32), _sds((s, DFF), BF),
                   _sds((s, DFF), BF)],
        args=(h, g, wg, wu, dn, tgt, g_final))


def _ffn_bwd_dx(h, dh, gt, up, g, wg, wu, dn, layer, tm, carry=None):
    s = h.shape[0]
    wg, wu, dn = (a.reshape(DFF, D) for a in (wg, wu, dn))

    def body(h_ref, dh_ref, gt_ref, up_ref, g_ref, wg_ref, wu_ref, wd_ref,
             dhi_ref, dg_ref, dgt_ref, dup_ref, act_ref, n_ref):
        i = pl.program_id(0)

        @pl.when(_first(i))
        def _():
            dg_ref[...] = jnp.zeros_like(dg_ref)

        gg = g_ref[...]
        dho = dh_ref[...]
        dhb = dho.astype(BF)
        dn_ = jnp.zeros((tm, D), F32)
        for c0, c1 in FF_CHUNKS:
            gtv = gt_ref[:, c0:c1].astype(F32)
            upv = up_ref[:, c0:c1].astype(F32)
            sg = _sigmoid(gtv)
            silu = gtv * sg
            dact = _dot_nt(dhb, wd_ref[c0:c1, :])
            dgt = (dact * upv * (sg * (1.0 + gtv * (1.0 - sg)))).astype(BF)
            dup = (dact * silu).astype(BF)
            dgt_ref[:, c0:c1] = dgt
            dup_ref[:, c0:c1] = dup
            act_ref[:, c0:c1] = (silu * upv).astype(BF)
            dn_ = dn_ + _dot(dgt, wg_ref[c0:c1, :]) + _dot(dup, wu_ref[c0:c1, :])
        hhat, rr = _rms(h_ref[...])
        n_ref[...] = (hhat * gg).astype(BF)
        dg_ref[...] += _colsum(dn_ * hhat)
        dhi_ref[...] = dho + _rms_bwd(dn_, hhat, rr, gg)

    wide = _rows(tm, DFF)
    return _call(
        body, name=f"ffn_bwd_dx{layer}", grid=(s // tm,),
        in_specs=[_rows(tm, D), _rows(tm, D), wide, wide, _const((1, D), (0, 0)),
                  _whole(wg), _whole(wu), _whole(dn)],
        out_specs=[_rows(tm, D), _const((1, D), (0, 0)), wide, wide, wide, _rows(tm, D)],
        out_shape=[_sds((s, D), F32), _sds((1, D), F32), _sds((s, DFF), BF), _sds((s, DFF), BF),
                   _sds((s, DFF), BF), _sds((s, D), BF)],
        args=(h, dh, gt, up, g, wg, wu, dn), carry=carry)


def _ffn_wgrad(a, b, name, tm, slot, into=None, carry=None):
    s = a.shape[0]
    tm = min(tm, s)

    def body(a_ref, b_ref, *rest):
        o_ref, acc = rest[-2:]
        i = pl.program_id(0)

        @pl.when(_first(i))
        def _():
            acc[...] = jnp.zeros_like(acc)

        acc[...] += _dot_tn(a_ref[...], b_ref[...].astype(BF))

        @pl.when(_last(i))
        def _():
            o_ref[...] = acc[...].astype(BF).reshape(NSH, FSH, D)

    chained = into is not None
    return _call(
        body, name=name, grid=(s // tm,),
        in_specs=[_rows(tm, DFF), _rows(tm, D)] + [ANY] * chained,
        out_specs=[_const((NSH, FSH, D), (0, slot, 0))],
        out_shape=[_sds((NSH, 3 * FSH, D), BF)],
        scratch_shapes=[pltpu.VMEM((DFF, D), F32)],
        aliases={2: 0} if chained else None,
        args=(a, b) + ((into,) if chained else ()), carry=carry)


def _sgu_fwd(h, g, lng, lnb, ws, bsb, wod, odo, tm, carry=None):
    s = h.shape[0]

    def body(h_ref, g_ref, lng_ref, lnb_ref, ws_ref, bsb_ref, wi_ref, wo_ref, ho_ref, zp_ref, y_scr):
        hh = h_ref[...]
        hhat, _ = _rms(hh)
        n = (hhat * g_ref[...]).astype(BF)
        zpre = _dot(n, wi_ref[...])
        zp_ref[...] = zpre.astype(BF)
        z = _gelu(zpre)
        u = z[:, :D]
        y, _ = _ln(z[:, D:])
        vn = (y * lng_ref[...] + lnb_ref[...]).astype(BF)
        for c in range(tm // CHUNK):
            rs = slice(CHUNK * c, CHUNK * (c + 1))
            for gi in range(GROUPS):
                cs = slice(CHUNK * gi, CHUNK * (gi + 1))
                sv = _dot(ws_ref[gi], vn[rs, cs]) + bsb_ref[gi]
                y_scr[rs, cs] = (u[rs, cs] * sv).astype(BF)
        ho_ref[...] = hh + _dot(y_scr[...], wo_ref[...].reshape(D, D))

    vec = _const((1, D), (0, 0))
    sq = _const((GROUPS, CHUNK, CHUNK), (0, 0, 0))
    return _call(
        body, name="sgu_fwd", grid=(s // tm,),
        in_specs=[_rows(tm, D), vec, vec, vec, sq, sq, _const((D, 2 * D), (0, 0)), _whole(odo)],
        out_specs=[_rows(tm, D), _rows(tm, 2 * D)],
        out_shape=[_sds((s, D), F32), _sds((s, 2 * D), BF)],
        scratch_shapes=[pltpu.VMEM((tm, D), BF)],
        args=(h, g, lng, lnb, ws, bsb, wod, odo), carry=carry)


def _sgu_bwd(h, dh, zpre, g, lng, lnb, ws, wst, bsb, wod, odo, tm, carry=None):
    s = h.shape[0]

    def body(h_ref, dh_ref, zp_ref, g_ref, lng_ref, lnb_ref, ws_ref, wst_ref, bsb_ref, wi_ref, wo_ref,
             dhi_ref, dws_ref, dbs_ref, dlng_ref, dlnb_ref, dg_ref, gwi_ref, gwo_ref,
             y_scr, dz_scr, dvn_scr, iacc, oacc, bacc):
        i = pl.program_id(0)

        @pl.when(_first(i))
        def _():
            for ref in (iacc, oacc, bacc, dws_ref, dlng_ref, dlnb_ref, dg_ref):
                ref[...] = jnp.zeros_like(ref)

        gg = g_ref[...]
        dho = dh_ref[...]
        dhb = dho.astype(BF)
        zpre_v = zp_ref[...].astype(F32)
        z = _gelu(zpre_v)
        u = z[:, :D]
        yl, rstd = _ln(z[:, D:])
        vn = (yl * lng_ref[...] + lnb_ref[...]).astype(BF)
        dy = _dot_nt(dhb, wo_ref[...].reshape(D, D))
        for c in range(tm // CHUNK):
            rs = slice(CHUNK * c, CHUNK * (c + 1))
            for gi in range(GROUPS):
                cs = slice(CHUNK * gi, CHUNK * (gi + 1))
                vb = vn[rs, cs]
                sv = _dot(ws_ref[gi], vb) + bsb_ref[gi]
                ub = u[rs, cs]
                dyb = dy[rs, cs]
                y_scr[rs, cs] = (ub * sv).astype(BF)
                dz_scr[rs, cs] = dyb * sv
                dsv = dyb * ub
                bacc[gi] += dsv
                dsvb = dsv.astype(BF)
                dws_ref[gi] += _dot_nt(dsvb, vb)
                dvn_scr[rs, cs] = _dot(wst_ref[gi], dsvb)
        oacc[...] += _dot_tn(y_scr[...], dhb)
        dvn = dvn_scr[...]
        dlng_ref[...] += _colsum(dvn * yl)
        dlnb_ref[...] += _colsum(dvn)
        dz_scr[:, D:] = _ln_bwd(dvn * lng_ref[...], yl, rstd)
        dzb = (dz_scr[...] * _gelu_grad(zpre_v)).astype(BF)
        hhat, rr = _rms(h_ref[...])
        n = (hhat * gg).astype(BF)
        iacc[...] += _dot_tn(n, dzb)
        dn = _dot_nt(dzb, wi_ref[...])
        dg_ref[...] += _colsum(dn * hhat)
        dhi_ref[...] = dho + _rms_bwd(dn, hhat, rr, gg)

        @pl.when(_last(i))
        def _():
            for k in range(NSH):
                gwi_ref[k] = iacc[:, 512 * k:512 * (k + 1)].astype(BF)
            gwo_ref[...] = oacc[...].astype(BF).reshape(NSH, 256, D)
            dbs_ref[...] = jnp.sum(bacc[...], axis=-1)

    vec = _const((1, D), (0, 0))
    sq = _const((GROUPS, CHUNK, CHUNK), (0, 0, 0))
    return _call(
        body, name="sgu_bwd", grid=(s // tm,),
        in_specs=[_rows(tm, D), _rows(tm, D), _rows(tm, 2 * D), vec, vec, vec, sq, sq, sq,
                  _const((D, 2 * D), (0, 0)), _whole(odo)],
        out_specs=[_rows(tm, D), sq, _const((GROUPS, CHUNK), (0, 0)), vec, vec, vec,
                   _const((NSH, D, 512), (0, 0, 0)), _whole(odo)],
        out_shape=[_sds((s, D), F32), _sds((GROUPS, CHUNK, CHUNK), F32), _sds((GROUPS, CHUNK), F32),
                   _sds((1, D), F32), _sds((1, D), F32), _sds((1, D), F32), _sds((NSH, D, 512), BF),
                   _sds(odo.shape, BF)],
        scratch_shapes=[pltpu.VMEM((tm, D), BF), pltpu.VMEM((tm, 2 * D), F32),
                        pltpu.VMEM((tm, D), F32), pltpu.VMEM((D, 2 * D), F32), pltpu.VMEM((D, D), F32),
                        pltpu.VMEM((GROUPS, CHUNK, CHUNK), F32)],
        args=(h, dh, zpre, g, lng, lnb, ws, wst, bsb, wod, odo), carry=carry)


def _own_plane(shape):
    return pl.BlockSpec((1,) + shape, lambda i, chip_ref: (chip_ref[0], 0, 0), pipeline_mode=pl.Buffered(1))


def _pack_rows(chip_idx, w_down, w_gate_t, w_up_t, xq, xo, xk, xv, ev_out, od_out, first):
    nf = len(first)

    def body(chip_ref, wd_ref, wg_ref, wu_ref, xq_ref, xo_ref, xk_ref, xv_ref, ev_ref, od_ref, *rest):
        (evo_ref, odo_ref, dn0_ref, dn1_ref, wg0_ref, wg1_ref, wu0_ref, wu1_ref, xa0_ref,
         xa1_ref) = rest[nf:nf + 10]
        gathered = rest[nf + 10:2 * nf + 10]
        sems = rest[2 * nf + 10:]
        _gather_start(gathered, sems)
        evo_ref[0] = ev_ref[0].astype(BF)
        odo_ref[0] = od_ref[0].astype(BF)
        for src, outs in ((wd_ref, (dn0_ref, dn1_ref)), (wg_ref, (wg0_ref, wg1_ref)), (wu_ref, (wu0_ref, wu1_ref))):
            for layer, o_ref in enumerate(outs):
                o_ref[0] = src[layer].astype(BF)
        for layer, xa_ref in enumerate((xa0_ref, xa1_ref)):
            for item, src in ((XQ, xq_ref), (XO, xo_ref), (XK, xk_ref), (XV, xv_ref)):
                xa_ref[0, 256 * item:256 * (item + 1), :] = src[layer].astype(BF)
        _gather_forward(gathered, sems)
        _gather_drain(gathered, sems)

    ins = (w_down, w_gate_t, w_up_t, xq, xo, xk, xv, ev_out, od_out)
    shapes = [(256, D), (256, D)] + [(FSH, D)] * 6 + [(4 * 256, D), (4 * 256, D)]
    grid_spec = pltpu.PrefetchScalarGridSpec(
        num_scalar_prefetch=1, grid=(1,), in_specs=[_whole(a) for a in ins] + [ANY] * nf,
        out_specs=[_own_plane(sh) for sh in shapes] + [ANY] * nf,
        scratch_shapes=_gather_sems(nf))
    outs = pl.pallas_call(
        body, name="pack_rows", grid_spec=grid_spec,
        out_shape=[_sds((NSH,) + sh, BF) for sh in shapes] + [_sds(a.shape, a.dtype) for a in first],
        input_output_aliases={1 + len(ins) + t: len(shapes) + t for t in range(nf)},
        compiler_params=_params(),
    )(chip_idx, *ins, *first)
    return outs[:len(shapes)], outs[len(shapes):]


def _pack_cols(chip_idx, od_in, ev_in, conv_a, conv_b, ln_g, ln_b):
    def body(chip_ref, od_ref, ev_ref, ca_ref, cb_ref, lg_ref, lb_ref, cev_ref, cod_ref, sp_ref):
        cev_ref[0] = ev_ref[0].astype(BF)
        cod_ref[0] = od_ref[0].astype(BF)
        sp_ref[...] = jnp.zeros_like(sp_ref)
        sp_ref[0, 0:KA, 0:128] = ca_ref[0]
        sp_ref[0, 32:32 + KB, 0:128] = cb_ref[0]
        sp_ref[0, 40:41, :] = lg_ref[...]
        sp_ref[0, 41:42, :] = lb_ref[...]

    ins = (od_in, ev_in, conv_a, conv_b, ln_g, ln_b)
    shapes = [(D, 640), (D, 512)]
    grid_spec = pltpu.PrefetchScalarGridSpec(
        num_scalar_prefetch=1, grid=(1,), in_specs=[_whole(a) for a in ins],
        out_specs=[_own_plane(sh) for sh in shapes] + [_own_plane((SP_ROWS, 256))])
    return pl.pallas_call(
        body, name="pack_cols", grid_spec=grid_spec,
        out_shape=[_sds((NSH,) + sh, BF) for sh in shapes] + [_sds((NSH, SP_ROWS, 256), F32)],
        compiler_params=_params(),
    )(chip_idx, *ins)


def _assemble(piece, parts, name):
    tr = 256
    cols = piece.shape[2]

    def body(p_ref, *outs):
        for k in range(NSH):
            for (off, width), o_ref in zip(parts, outs):
                o_ref[:, width * k:width * (k + 1)] = p_ref[k, :, off:off + width]

    return _call(
        body, name=name, grid=(D // tr,),
        in_specs=[pl.BlockSpec((NSH, tr, cols), lambda i: (0, i, 0))],
        out_specs=[pl.BlockSpec((tr, NSH * width), lambda i: (i, 0)) for _, width in parts],
        out_shape=[_sds((D, NSH * width), BF) for _, width in parts],
        args=(piece,))


def _blocks_of(rows, cols, itemsize):
    rb = rows
    while rb * cols * itemsize > (2 << 20) and rb % 2 == 0 and (rb // 2) % 16 == 0:
        rb //= 2
    return rb


def _add_half(full, recv, c_idx, name):
    p, r, cols = full.shape
    rb = _blocks_of(r // 2, cols, 4)
    nb = (r // 2) // rb

    def body(c_ref, a_ref, b_ref, o_ref):
        o_ref[...] = (a_ref[...].astype(F32) + b_ref[...].astype(F32)).astype(o_ref.dtype)

    grid_spec = pltpu.PrefetchScalarGridSpec(
        num_scalar_prefetch=1, grid=(p, nb),
        in_specs=[pl.BlockSpec((1, rb, cols), lambda j, i, c_ref: (j, c_ref[0] * nb + i, 0)),
                  pl.BlockSpec((1, rb, cols), lambda j, i, c_ref: (j, i, 0))],
        out_specs=pl.BlockSpec((1, rb, cols), lambda j, i, c_ref: (j, i, 0)))
    return pl.pallas_call(
        body, name=name, grid_spec=grid_spec, out_shape=_sds(recv.shape, full.dtype),
        compiler_params=_params(2),
    )(c_idx, full, recv)


def _sum_chips(own, got, plane_c, name):
    _, r, cols = own.shape
    rb = _blocks_of(r, cols, 4)
    nb = r // rb

    def body(s_ref, a_ref, b_ref, o_ref):
        f = lambda v: v.astype(F32)
        o_ref[...] = (f(a_ref[0]) + f(b_ref[1])) + (f(b_ref[0]) + f(b_ref[2]))

    grid_spec = pltpu.PrefetchScalarGridSpec(
        num_scalar_prefetch=1, grid=(nb,),
        in_specs=[pl.BlockSpec((1, rb, cols), lambda i, s_ref: (s_ref[0], i, 0)),
                  pl.BlockSpec((3, rb, cols), lambda i, s_ref: (0, i, 0))],
        out_specs=pl.BlockSpec((rb, cols), lambda i, s_ref: (s_ref[1] * nb + i, 0)))
    return pl.pallas_call(
        body, name=name, grid_spec=grid_spec, out_shape=_sds((2 * r, cols), F32),
        compiler_params=_params(1),
    )(plane_c, own, got)


def _adamw_math(w, g, m, v):
    m = ADAM_B1 * m + (1.0 - ADAM_B1) * g
    v = ADAM_B2 * v + (1.0 - ADAM_B2) * (g * g)
    m_hat = m / (1.0 - ADAM_B1 ** ADAM_STEP)
    v_hat = v / (1.0 - ADAM_B2 ** ADAM_STEP)
    delta = -ADAM_LR * (m_hat / (jnp.sqrt(v_hat) + ADAM_EPS) + ADAM_WD * w)
    return delta, m, v


def _adamw_big(w, m, v, gsrcs, gblock, gidx, name, rb):
    nl, r, cols = w.shape
    assert len(gsrcs) == nl

    def body(w_ref, m_ref, v_ref, *rest):
        g_refs, (go_ref, d_ref, mo_ref, vo_ref) = rest[:nl], rest[nl:]
        g = g_refs[0][:, 0:cols]
        if nl == 2:
            g = jnp.where(pl.program_id(0) == 0, g, g_refs[1][:, 0:cols])
        delta, mn, vn = _adamw_math(w_ref[0], g, m_ref[0], v_ref[0])
        go_ref[0] = g
        d_ref[0] = delta
        mo_ref[0] = mn
        vo_ref[0] = vn

    wspec = pl.BlockSpec((1, rb, cols), lambda l, i: (l, i, 0))
    gspec = pl.BlockSpec(gblock, lambda l, i: gidx(i))
    return pl.pallas_call(
        body, name=name, grid=(nl, r // rb),
        in_specs=[wspec, wspec, wspec] + [gspec] * nl,
        out_specs=[wspec] * 4, out_shape=[_sds(w.shape, F32)] * 4,
        compiler_params=_params(2),
    )(w, m, v, *gsrcs)


_SMALL = (
    ("g_mix", (2, D), "a", (0, 2), 0, False),
    ("g_xattn", (2, D), "a", (2, 4), 0, False),
    ("g_mem", (2, D), "a", (4, 6), 0, False),
    ("g_ffn", (2, D), "a", (6, 8), 0, False),
    ("g_final", (1, D), "a", (8, 9), 0, False),
    ("ev_a_conv_w", (KA, 128), "a", (16, 16 + KA), 0, True),
    ("ev_a_conv_b", (1, DA), "a", (11, 12), 0, False),
    ("ev_a_ln_g", (1, DA), "a", (11, 12), DA, False),
    ("ev_a_ln_b", (1, DA), "a", (12, 13), 0, False),
    ("ev_b_conv_w", (KB, 128), "a", (48, 48 + KB), 0, True),
    ("ev_b_conv_b", (1, DA), "a", (12, 13), DA, False),
    ("od_c_ln_g", (1, 256), "a", (9, 10), 0, True),
    ("od_c_ln_b", (1, 256), "a", (10, 11), 0, True),
    ("od_w_s", (GROUPS * CHUNK, CHUNK), "b", (0, GROUPS * CHUNK), 0, False),
    ("od_b_s", (GROUPS, CHUNK), "a", (56, 64), 0, False),
)


def _adamw_small(ga, gb, wmv):
    ns = len(_SMALL)

    def body(*refs):
        ga_ref, gb_ref = refs[0], refs[1]
        ins = refs[2:2 + 3 * ns]
        outs = refs[2 + 3 * ns:]
        x, y, _ = _pos()
        chip = 2 * x + y
        for j, (_, shape, pack, (r0, r1), col, sharded) in enumerate(_SMALL):
            src = ga_ref if pack == "a" else gb_ref
            width = shape[1]
            if sharded:
                g = jnp.zeros(shape, F32)
                for k in range(NSH):
                    blk = src[r0:r1, col + width * k:col + width * (k + 1)]
                    g = g + jnp.where(chip == k, blk, 0.0)
            else:
                g = src[r0:r1, col:col + width]
            w_ref, m_ref, v_ref = ins[3 * j:3 * j + 3]
            delta, mn, vn = _adamw_math(w_ref[...], g, m_ref[...], v_ref[...])
            go_ref, d_ref, mo_ref, vo_ref = outs[4 * j:4 * j + 4]
            go_ref[...] = g
            d_ref[...] = delta
            mo_ref[...] = mn
            vo_ref[...] = vn

    vm = pl.BlockSpec(memory_space=pltpu.VMEM)
    flat = [a for trio in wmv for a in trio]
    out_shape = []
    for (_, shape, *_rest) in _SMALL:
        out_shape += [_sds(shape, F32)] * 4
    outs = pl.pallas_call(
        body, name="adamw_small",
        in_specs=[vm] * (2 + 3 * ns), out_specs=[vm] * (4 * ns), out_shape=out_shape,
    )(ga, gb, *flat)
    return [tuple(outs[4 * j:4 * j + 4]) for j in range(ns)]


def _small_params(gs, W):
    unshard = lambda a: jnp.transpose(a, (1, 0, 2)).reshape(a.shape[1], NSH * a.shape[2])
    od_w_s, od_b_s = W["od_w_s"], W["od_b_s"]
    return dict(
        g_mix=W["g_mix"], g_xattn=W["g_xattn"], g_mem=W["g_mem"], g_ffn=W["g_ffn"],
        g_final=W["g_final"].reshape(1, D),
        wa=unshard(gs[:, 0:32, 0:128]), ba=W["ev_a_conv_b"], lng_a=W["ev_a_ln_g"], lnb_a=W["ev_a_ln_b"],
        wb=unshard(gs[:, 32:40, 0:128]), bb=W["ev_b_conv_b"],
        lng_c=unshard(gs[:, 40:41, :]), lnb_c=unshard(gs[:, 41:42, :]),
        ws=od_w_s[0].astype(BF), wst=jnp.swapaxes(od_w_s[0], 1, 2).astype(BF),
        bsb=jnp.broadcast_to(od_b_s[0][:, :, None], (GROUPS, CHUNK, CHUNK)),
    )


def _small_grad_packs(g):
    z = lambda r, c: jnp.zeros((r, c), F32)
    wide = lambda a: jnp.concatenate([a, z(a.shape[0], D - a.shape[1])], axis=1)
    rows = [g["g_mix"], g["g_xattn"], g["g_mem"], g["g_ffn"], g["g_final"], g["od_c_ln_g"], g["od_c_ln_b"],
            jnp.concatenate([g["ev_a_conv_b"], g["ev_a_ln_g"]], axis=1),
            jnp.concatenate([g["ev_a_ln_b"], g["ev_b_conv_b"]], axis=1),
            z(3, D), wide(g["ev_a_conv_w"]), wide(g["ev_b_conv_w"]), wide(g["od_b_s"])]
    a = jnp.concatenate(rows, axis=0)
    assert a.shape == (SA_ROWS, D), a.shape
    return a, g["od_w_s"].reshape(GROUPS * CHUNK, CHUNK)


def _step(x, mem, tgt, W, P, c_idx, tm_f, tm_b, tm_w):
    row = lambda a, i: a[i:i + 1]

    sm = _small_params(P["gs"], W)
    (wev,) = _assemble(P["cev"], [(0, 640)], "assemble_ev")
    z, evo, xa0 = _ev_in_fwd(x, row(sm["g_mix"], 0), wev, 2 * tm_f, carry=("gather", [P["evo"], P["xa0"]]))
    h1, a2, c2, cat, wg0, wu0 = _ev_mix_fwd(
        x, z, sm["wa"], sm["ba"], sm["lng_a"], sm["lnb_a"], sm["wb"], sm["bb"], evo, tm_f,
        carry=("gather", [P["wg0"], P["wu0"]]))
    k0, v0 = _kv_proj(mem, row(sm["g_mem"], 0), xa0, 0)
    h2, q0, dn0, cod = _xattn_fwd(h1, row(sm["g_xattn"], 0), k0, v0, xa0, 0, 2 * tm_f,
                                  carry=("gather", [P["dn0"], P["cod"]]))
    (wod,) = _assemble(cod, [(0, 512)], "assemble_od")
    h3, gt0, up0, odo, xa1, wg1 = _ffn_fwd(h2, row(sm["g_ffn"], 0), wg0, wu0, dn0, 0, tm_f,
                                           carry=("gather", [P["odo"], P["xa1"], P["wg1"]]))
    h4, zpre, wu1, dn1 = _sgu_fwd(h3, row(sm["g_mix"], 1), sm["lng_c"], sm["lnb_c"], sm["ws"], sm["bsb"], wod,
                                  odo, tm_f, carry=("gather", [P["wu1"], P["dn1"]]))
    k1, v1 = _kv_proj(mem, row(sm["g_mem"], 1), xa1, 1)
    h5, q1 = _xattn_fwd(h4, row(sm["g_xattn"], 1), k1, v1, xa1, 1, 2 * tm_f)
    dh6, loss, dg_final, gt1, up1 = _ffn_fwd_loss(h5, row(sm["g_ffn"], 1), wg1, wu1, dn1, tgt, sm["g_final"], 1,
                                                  tm_f)

    g = {"g_final": dg_final}
    parts, gots = {}, {}
    def add(name, piece, recv):
        parts[name] = _add_half(piece, recv, c_idx, f"rs_add_{name}")
        return parts[name]

    def ffn_wgrads(layer, dgt, dup, act, nb, dh_out, carry=None):
        g_ffn, *carried = _ffn_wgrad(dgt, nb, f"ffn_wgrad_gate{layer}", tm_w, 0, carry=carry)
        (g_ffn,) = _ffn_wgrad(dup, nb, f"ffn_wgrad_up{layer}", tm_w, 1, into=g_ffn)
        (g_ffn,) = _ffn_wgrad(act, dh_out, f"ffn_wgrad_down{layer}", tm_w, 2, into=g_ffn)
        return g_ffn, carried

    dh5, dg_ffn1, dgt, dup, act, nb = _ffn_bwd_dx(h5, dh6, gt1, up1, row(sm["g_ffn"], 1), wg1, wu1, dn1, 1, tm_b)
    g_ffn1, _ = ffn_wgrads(1, dgt, dup, act, nb, dh6)
    dh4, dk, dv, dg_xa1, g_xa1, recv = _xattn_bwd(
        h4, dh5, q1, row(sm["g_xattn"], 1), k1, v1, xa1, 1, tm_f, carry=("swap", [g_ffn1]))
    add("ffn1", g_ffn1, recv)
    dg_mem1, g_xa1 = _kv_proj_bwd(mem, row(sm["g_mem"], 1), dk, dv, xa1, g_xa1, 1)
    (dh3, g["od_w_s"], g["od_b_s"], g["od_c_ln_g"], g["od_c_ln_b"], dg_mix1, g_cod, g_odo, gots["ffn1"],
     recv) = _sgu_bwd(
        h3, dh4, zpre, row(sm["g_mix"], 1), sm["lng_c"], sm["lnb_c"], sm["ws"], sm["wst"], sm["bsb"], wod, odo, tm_b,
        carry=[("xchg", [parts["ffn1"]], []), ("swap", [g_xa1])])
    add("xa1", g_xa1, recv)
    dh2, dg_ffn0, dgt, dup, act, nb, gots["xa1"], recv_cod, recv_odo = _ffn_bwd_dx(
        h2, dh3, gt0, up0, row(sm["g_ffn"], 0), wg0, wu0, dn0, 0, tm_b,
        carry=[("xchg", [parts["xa1"]], []), ("swap", [g_cod, g_odo])])
    add("cod", g_cod, recv_cod)
    add("odo", g_odo, recv_odo)
    g_ffn0, (gots["cod"], gots["odo"]) = ffn_wgrads(0, dgt, dup, act, nb, dh3,
                                                    carry=("xchg", [parts["cod"], parts["odo"]], []))
    dh1, dk, dv, dg_xa0, g_xa0, recv = _xattn_bwd(
        h1, dh2, q0, row(sm["g_xattn"], 0), k0, v0, xa0, 0, tm_f, carry=("swap", [g_ffn0]))
    add("ffn0", g_ffn0, recv)
    dg_mem0, g_xa0 = _kv_proj_bwd(mem, row(sm["g_mem"], 0), dk, dv, xa0, g_xa0, 0)
    da2, dc2, dgb, g["ev_a_ln_g"], g["ev_a_ln_b"], g_evo, recv = _ev_mix_bwd(
        dh1, cat, a2, c2, z, sm["lng_a"], sm["lnb_a"], evo, tm_f, carry=("swap", [g_xa0]))
    add("xa0", g_xa0, recv)
    handle, token = _comm_start([parts["ffn0"], parts["xa0"]], [], [g_evo], "ev")
    dx, g["ev_a_conv_w"], g["ev_a_conv_b"], g["ev_b_conv_w"], g["ev_b_conv_b"], dg_mix0, g_cev = _ev_in_bwd(
        x, dh1, z, da2, dc2, dgb, row(sm["g_mix"], 0), sm["wa"], sm["wb"], wev, tm_b, token)
    (parts["ffn0"], parts["xa0"], g_evo), (gots["ffn0"], gots["xa0"], recv) = _comm_wait(handle, g_cev)
    add("evo", g_evo, recv)
    g["g_mix"] = jnp.concatenate([dg_mix0, dg_mix1], axis=0)
    g["g_xattn"] = jnp.concatenate([dg_xa0, dg_xa1], axis=0)
    g["g_mem"] = jnp.concatenate([dg_mem0, dg_mem1], axis=0)
    g["g_ffn"] = jnp.concatenate([dg_ffn0, dg_ffn1], axis=0)
    ga, gb = _small_grad_packs(g)
    last = (g_cev, ga[None], gb[None])
    for name, piece, recv in zip(("cev", "ga", "gb"), last, _sibling_swap(last, "rs_swap_last")):
        add(name, piece, recv)
    return loss, dx, parts, gots


def kernel(x, mem, g_mix, g_xattn, g_mem, g_ffn, g_final, ev_w_in, ev_a_conv_w, ev_a_conv_b, ev_a_ln_g, ev_a_ln_b, ev_b_conv_w, ev_b_conv_b, ev_w_out, od_w_in, od_c_ln_g, od_c_ln_b, od_w_s, od_b_s, od_w_out, xa_w_q, xa_w_k, xa_w_v, xa_w_o, ffn_w_gate, ffn_w_up, ffn_w_down, loss_target, m_g_mix, m_g_xattn, m_g_mem, m_g_ffn, m_g_final, m_ev_w_in, m_ev_a_conv_w, m_ev_a_conv_b, m_ev_a_ln_g, m_ev_a_ln_b, m_ev_b_conv_w, m_ev_b_conv_b, m_ev_w_out, m_od_w_in, m_od_c_ln_g, m_od_c_ln_b, m_od_w_s, m_od_b_s, m_od_w_out, m_xa_w_q, m_xa_w_k, m_xa_w_v, m_xa_w_o, m_ffn_w_gate, m_ffn_w_up, m_ffn_w_down, v_g_mix, v_g_xattn, v_g_mem, v_g_ffn, v_g_final, v_ev_w_in, v_ev_a_conv_w, v_ev_a_conv_b, v_ev_a_ln_g, v_ev_a_ln_b, v_ev_b_conv_w, v_ev_b_conv_b, v_ev_w_out, v_od_w_in, v_od_c_ln_g, v_od_c_ln_b, v_od_w_s, v_od_b_s, v_od_w_out, v_xa_w_q, v_xa_w_k, v_xa_w_v, v_xa_w_o, v_ffn_w_gate, v_ffn_w_up, v_ffn_w_down):
    W = dict(g_mix=g_mix, g_xattn=g_xattn, g_mem=g_mem, g_ffn=g_ffn, g_final=g_final, ev_w_in=ev_w_in,
             ev_a_conv_w=ev_a_conv_w, ev_a_conv_b=ev_a_conv_b, ev_a_ln_g=ev_a_ln_g, ev_a_ln_b=ev_a_ln_b,
             ev_b_conv_w=ev_b_conv_w, ev_b_conv_b=ev_b_conv_b, ev_w_out=ev_w_out, od_w_in=od_w_in,
             od_c_ln_g=od_c_ln_g, od_c_ln_b=od_c_ln_b, od_w_s=od_w_s, od_b_s=od_b_s, od_w_out=od_w_out,
             xa_w_q=xa_w_q, xa_w_k=xa_w_k, xa_w_v=xa_w_v, xa_w_o=xa_w_o, ffn_w_gate=ffn_w_gate,
             ffn_w_up=ffn_w_up, ffn_w_down=ffn_w_down)
    M = dict(g_mix=m_g_mix, g_xattn=m_g_xattn, g_mem=m_g_mem, g_ffn=m_g_ffn, g_final=m_g_final, ev_w_in=m_ev_w_in,
             ev_a_conv_w=m_ev_a_conv_w, ev_a_conv_b=m_ev_a_conv_b, ev_a_ln_g=m_ev_a_ln_g, ev_a_ln_b=m_ev_a_ln_b,
             ev_b_conv_w=m_ev_b_conv_w, ev_b_conv_b=m_ev_b_conv_b, ev_w_out=m_ev_w_out, od_w_in=m_od_w_in,
             od_c_ln_g=m_od_c_ln_g, od_c_ln_b=m_od_c_ln_b, od_w_s=m_od_w_s, od_b_s=m_od_b_s, od_w_out=m_od_w_out,
             xa_w_q=m_xa_w_q, xa_w_k=m_xa_w_k, xa_w_v=m_xa_w_v, xa_w_o=m_xa_w_o, ffn_w_gate=m_ffn_w_gate,
             ffn_w_up=m_ffn_w_up, ffn_w_down=m_ffn_w_down)
    V = dict(g_mix=v_g_mix, g_xattn=v_g_xattn, g_mem=v_g_mem, g_ffn=v_g_ffn, g_final=v_g_final, ev_w_in=v_ev_w_in,
             ev_a_conv_w=v_ev_a_conv_w, ev_a_conv_b=v_ev_a_conv_b, ev_a_ln_g=v_ev_a_ln_g, ev_a_ln_b=v_ev_a_ln_b,
             ev_b_conv_w=v_ev_b_conv_w, ev_b_conv_b=v_ev_b_conv_b, ev_w_out=v_ev_w_out, od_w_in=v_od_w_in,
             od_c_ln_g=v_od_c_ln_g, od_c_ln_b=v_od_c_ln_b, od_w_s=v_od_w_s, od_b_s=v_od_b_s, od_w_out=v_od_w_out,
             xa_w_q=v_xa_w_q, xa_w_k=v_xa_w_k, xa_w_v=v_xa_w_v, xa_w_o=v_xa_w_o, ffn_w_gate=v_ffn_w_gate,
             ffn_w_up=v_ffn_w_up, ffn_w_down=v_ffn_w_down)
    order = list(W)
    cx, cy, cc = _pos()
    chip = (2 * cx + cy).astype(jnp.int32)
    cc = cc.astype(jnp.int32)
    chip_idx = jnp.reshape(chip, (1,))
    c_idx = jnp.reshape(cc, (1,))

    tr = lambda a: jnp.swapaxes(a, 1, 2)
    P = {}
    cev, P["cod"], gs = _pack_cols(chip_idx, od_w_in, ev_w_in, ev_a_conv_w, ev_b_conv_w, od_c_ln_g, od_c_ln_b)
    ((P["evo"], P["odo"], P["dn0"], P["dn1"], P["wg0"], P["wg1"], P["wu0"], P["wu1"], P["xa0"], P["xa1"]),
     (P["cev"], P["gs"])) = _pack_rows(chip_idx, ffn_w_down, tr(ffn_w_gate), tr(ffn_w_up), xa_w_q, xa_w_o,
                                       xa_w_k, xa_w_v, ev_w_out, od_w_out, first=[cev, gs])

    loss, dx, parts, gots = _step(x[0], mem[0], loss_target[0], W, P, c_idx, TM_FWD, TM_BWD, TM_WGRAD)
    loss = lax.psum(loss[0, 0], ("x", "y", "c"))

    chip_c = jnp.stack([chip, cc])
    zero_c = jnp.stack([jnp.zeros((), jnp.int32), cc])
    def sums(names):
        return [_sum_chips(parts[n], gots[n], zero_c if n in ("ga", "gb") else chip_c, f"rs_sum_{n}")
                for n in names]

    last = ("evo", "cev", "ga", "gb")
    early = [n for n in parts if n not in last]
    handle, token = _comm_start([parts["evo"], parts["cev"]], [parts["ga"][0], parts["gb"][0]], [], "last")
    red = dict(zip(early, _sibling_join(sums(early), "rs_join_early", after=(token,))))

    res = {}

    def adamw(name, srcs, gblock, gidx, rb):
        t = tr if name in ("ffn_w_gate", "ffn_w_up") else (lambda a: a)
        outs = _adamw_big(t(W[name]), t(M[name]), t(V[name]), srcs, gblock, gidx, f"adamw_{name}", rb)
        res[name] = tuple(t(o) for o in outs)

    xa = [red["xa0"], red["xa1"]]
    ffn = [red["ffn0"], red["ffn1"]]
    adamw("ffn_w_gate", ffn, (FSH, D), lambda i: (0, 0), FSH)
    adamw("ffn_w_up", ffn, (FSH, D), lambda i: (1, 0), FSH)
    adamw("ffn_w_down", ffn, (FSH, D), lambda i: (2, 0), FSH)
    adamw("xa_w_q", xa, (256, D), lambda i: (XQ, 0), 256)
    adamw("xa_w_o", xa, (256, D), lambda i: (XO, 0), 256)
    adamw("xa_w_k", xa, (256, D), lambda i: (XK, 0), 256)
    adamw("xa_w_v", xa, (256, D), lambda i: (XV, 0), 256)
    adamw("od_w_out", [red["odo"]], (256, D), lambda i: (0, 0), 256)
    adamw("od_w_in", [red["cod"]], (256, 512), lambda i: (i, 0), 256)

    srcs_done, got_last = _comm_wait(handle, res["od_w_in"][1])
    parts.update(zip(last, srcs_done[:2] + [a[None] for a in srcs_done[2:]]))
    gots.update(zip(last, got_last))
    red.update(zip(last, _sibling_join(sums(last), "rs_join_last")))
    adamw("ev_w_out", [red["evo"]], (256, D), lambda i: (0, 0), 256)
    adamw("ev_w_in", [red["cev"]], (256, 640), lambda i: (i, 0), 256)
    shape2 = {name: shape for name, shape, *_ in _SMALL}
    wmv = [tuple(d[name].reshape(shape2[name]) for d in (W, M, V)) for name, *_ in _SMALL]
    for (name, *_), outs in zip(_SMALL, _adamw_small(red["ga"], red["gb"], wmv)):
        res[name] = tuple(o.reshape(W[name].shape) for o in outs)

    grad_x = dx[None]
    return (loss, grad_x, *[res[n][0] for n in order], *[res[n][1] for n in order],
            *[res[n][2] for n in order], *[res[n][3] for n in order])
```

```python
import jax
import jax.numpy as jnp
from jax import lax
from jax.experimental import pallas as pl
from jax.experimental.pallas import tpu as pltpu

BF = jnp.bfloat16
F32 = jnp.float32

D = 1024
DA = 512
DZ = 2560
DFF = 2816
NSH = 4
FSH = DFF // NSH
FF_CHUNKS = ((0, 768), (768, 1536), (1536, 2304), (2304, 2816))
NMEM = 256
HEADS = 4
HD = D // HEADS
CHUNK = 128
GROUPS = 8
KA = 31
KB = 3
HALO = 16
RC = 32
RMS_EPS = 1e-6
LN_EPS = 1e-5
ATT_SCALE = HD ** -0.5
TM_FWD = 512
TM_BWD = 256
TM_WGRAD = 1024

ADAM_LR = 0.001
ADAM_B1 = 0.9
ADAM_B2 = 0.999
ADAM_EPS = 1e-08
ADAM_WD = 0.01
ADAM_STEP = 10

XQ, XO, XK, XV = 0, 1, 2, 3
SP_ROWS = 48
SA_ROWS = 64

MESH = pl.DeviceIdType.MESH
ANY = pl.BlockSpec(memory_space=pl.ANY)


def _dot(a, b):
    return jnp.dot(a, b, preferred_element_type=F32)


def _dot_nt(a, b):
    return lax.dot_general(a, b, (((1,), (1,)), ((), ())), preferred_element_type=F32)


def _dot_tn(a, b):
    return lax.dot_general(a, b, (((0,), (0,)), ((), ())), preferred_element_type=F32)


def _rms(h):
    r = lax.rsqrt(jnp.mean(h * h, axis=-1, keepdims=True) + RMS_EPS)
    return h * r, r


def _rms_bwd(dn, hhat, r, g):
    dhh = dn * g
    return r * (dhh - hhat * jnp.mean(dhh * hhat, axis=-1, keepdims=True))


def _ln(x):
    mu = jnp.mean(x, axis=-1, keepdims=True)
    xc = x - mu
    rstd = lax.rsqrt(jnp.mean(xc * xc, axis=-1, keepdims=True) + LN_EPS)
    return xc * rstd, rstd


def _ln_bwd(dy, y, rstd):
    return rstd * (dy - jnp.mean(dy, axis=-1, keepdims=True) - y * jnp.mean(dy * y, axis=-1, keepdims=True))


def _colsum(x):
    return jnp.sum(x, axis=0, keepdims=True)


def _sigmoid(x):
    return 1.0 / (1.0 + jnp.exp(-x))


_GELU_C = 0.7978845608028654
_GELU_A = 0.044715


def _gelu(x):
    t = jnp.tanh(_GELU_C * (x + _GELU_A * x * x * x))
    return 0.5 * x * (1.0 + t)


def _gelu_grad(x):
    t = jnp.tanh(_GELU_C * (x + _GELU_A * x * x * x))
    return 0.5 * (1.0 + t) + 0.5 * x * (1.0 - t * t) * _GELU_C * (1.0 + 3.0 * _GELU_A * x * x)


def _first(i):
    return i == 0


def _last(i):
    return i == pl.num_programs(0) - 1


def _const(shape, idx):
    return pl.BlockSpec(shape, lambda *_, _idx=tuple(idx): _idx, pipeline_mode=pl.Buffered(1))


def _whole(a):
    return _const(a.shape, (0,) * a.ndim)


def _rows(tm, width, col=0):
    return pl.BlockSpec((tm, width), lambda i, _c=col: (i, _c))


def _params(ndim=1, vmem_mb=56):
    return pltpu.CompilerParams(dimension_semantics=("arbitrary",) * ndim, vmem_limit_bytes=vmem_mb << 20)


def _sds(shape, dtype):
    return jax.ShapeDtypeStruct(shape, dtype)


def _pos():
    return lax.axis_index("x"), lax.axis_index("y"), lax.axis_index("c")


def _other_chips(x, y):
    return [(1 - x, y), (x, 1 - y), (1 - x, 1 - y)]


def _half(ref, hc, lead=()):
    r = ref.shape[-2] // 2
    return ref.at[(*lead, pl.ds(pl.multiple_of(hc * r, 8), r), slice(None))]


def _rcopy(src, dst, ssem, rsem, to):
    return pltpu.make_async_remote_copy(src_ref=src, dst_ref=dst, send_sem=ssem, recv_sem=rsem,
                                        device_id=to, device_id_type=MESH)


def _gather_start(bufs, sems):
    ici_s, ici_r, _, _ = sems
    x, y, c = _pos()
    me = 2 * x + y
    for k, (cx, cy) in enumerate(_other_chips(x, y)):
        for t, buf in enumerate(bufs):
            mine = _half(buf, c, (me,))
            _rcopy(mine, mine, ici_s.at[3 * t + k], ici_r.at[3 * t + k], (cx, cy, c)).start()


def _gather_forward(bufs, sems):
    ici_s, ici_r, d2d_s, d2d_r = sems
    x, y, c = _pos()
    sib = (x, y, 1 - c)
    for k, (cx, cy) in enumerate(_other_chips(x, y)):
        for t, buf in enumerate(bufs):
            landed = _half(buf, c, (2 * cx + cy,))
            _rcopy(landed, landed, ici_s.at[3 * t + k], ici_r.at[3 * t + k], (cx, cy, c)).wait_recv()
            _rcopy(landed, landed, d2d_s.at[3 * t + k], d2d_r.at[3 * t + k], sib).start()


def _gather_drain(bufs, sems):
    ici_s, ici_r, d2d_s, d2d_r = sems
    x, y, c = _pos()
    me = 2 * x + y
    sib = (x, y, 1 - c)
    chips = _other_chips(x, y)
    for k, (cx, cy) in enumerate(chips):
        for t, buf in enumerate(bufs):
            got = _half(buf, 1 - c, (2 * cx + cy,))
            _rcopy(got, got, d2d_s.at[3 * t + k], d2d_r.at[3 * t + k], sib).wait_recv()
    for k, (cx, cy) in enumerate(chips):
        for t, buf in enumerate(bufs):
            mine = _half(buf, c, (me,))
            _rcopy(mine, mine, ici_s.at[3 * t + k], ici_r.at[3 * t + k], (cx, cy, c)).wait_send()
            landed = _half(buf, c, (2 * cx + cy,))
            _rcopy(landed, landed, d2d_s.at[3 * t + k], d2d_r.at[3 * t + k], sib).wait_send()


def _xchg_copies(srcs, dsts, nbig, sems):
    ssem, rsem = sems
    x, y, c = _pos()
    cps = []
    for k, (cx, cy) in enumerate(_other_chips(x, y)):
        for t, (src, dst) in enumerate(zip(srcs, dsts)):
            s = src.at[2 * cx + cy] if t < nbig else src
            cps.append(_rcopy(s, dst.at[k], ssem.at[3 * t + k], rsem.at[3 * t + k], (cx, cy, c)))
    return cps


def _gather_sems(n):
    return [pltpu.SemaphoreType.DMA((3 * n,))] * 4


def _xchg_sems(n):
    return [pltpu.SemaphoreType.DMA((3 * n,))] * 2


def _xchg_out_shapes(items):
    return [_sds((3,) + a.shape[-2:], a.dtype) for a in items]


def _swap_copies(srcs, dsts, sems):
    ssem, rsem = sems
    x, y, c = _pos()
    cps = []
    for t, (src, dst) in enumerate(zip(srcs, dsts)):
        lead = (slice(None),) * (len(src.shape) - 2)
        cps.append(_rcopy(_half(src, 1 - c, lead), dst, ssem.at[t], rsem.at[t], (x, y, 1 - c)))
    return cps


def _swap_out_shapes(items):
    return [_sds(a.shape[:-2] + (a.shape[-2] // 2, a.shape[-1]), a.dtype) for a in items]


def _call(body, *, name, grid, in_specs, out_specs, out_shape, args, scratch_shapes=(), aliases=None, carry=None):
    in_specs, out_specs, out_shape = list(in_specs), list(out_specs), list(out_shape)
    args = list(args)
    aliases = dict(aliases or {})
    n_in, n_out, n_scr = len(args), len(out_shape), len(scratch_shapes)
    carries = [] if not carry else ([carry] if isinstance(carry, tuple) else list(carry))
    plans = []
    cin, cout_shape, sems = [], [], []
    for c in carries:
        kind = c[0]
        items = list(c[1]) + (list(c[2]) if kind == "xchg" else [])
        if kind == "gather":
            shapes, sm = [_sds(a.shape, a.dtype) for a in items], _gather_sems(len(items))
            for t in range(len(items)):
                aliases[n_in + len(cin) + t] = n_out + len(cin) + t
        elif kind == "xchg":
            shapes, sm = _xchg_out_shapes(items), _xchg_sems(len(items))
        else:
            shapes, sm = _swap_out_shapes(items), [pltpu.SemaphoreType.DMA((len(items),))] * 2
        plans.append((kind, len(cin), len(items), len(sems), len(sm), len(c[1])))
        cin += items
        cout_shape += shapes
        sems += sm
    nc = len(cin)

    def wrapped(*refs):
        ins = refs[:n_in]
        c_in = refs[n_in:n_in + nc]
        outs = refs[n_in + nc:n_in + nc + n_out]
        c_out = refs[n_in + nc + n_out:n_in + 2 * nc + n_out]
        scr = refs[n_in + 2 * nc + n_out:n_in + 2 * nc + n_out + n_scr]
        sem = refs[n_in + 2 * nc + n_out + n_scr:]
        i = pl.program_id(0)

        def copies(plan):
            kind, o0, no, s0, ns, nbig = plan
            if kind == "xchg":
                return _xchg_copies(c_in[o0:o0 + no], c_out[o0:o0 + no], nbig, sem[s0:s0 + ns])
            return _swap_copies(c_in[o0:o0 + no], c_out[o0:o0 + no], sem[s0:s0 + ns])

        if plans:
            @pl.when(_first(i))
            def _():
                for plan in plans:
                    kind, o0, no, s0, ns, _ = plan
                    if kind == "gather":
                        _gather_start(c_out[o0:o0 + no], sem[s0:s0 + ns])
                    else:
                        for cp in copies(plan):
                            cp.start()
        body(*ins, *outs, *scr)
        for plan in plans:
            kind, o0, no, s0, ns, _ = plan
            if kind == "gather":
                @pl.when(i == max(grid[0] - 3, 0))
                def _(o0=o0, no=no, s0=s0, ns=ns):
                    _gather_forward(c_out[o0:o0 + no], sem[s0:s0 + ns])
        if plans:
            @pl.when(_last(i))
            def _():
                for plan in plans:
                    kind, o0, no, s0, ns, _ = plan
                    if kind == "gather":
                        _gather_drain(c_out[o0:o0 + no], sem[s0:s0 + ns])
                    else:
                        for cp in copies(plan):
                            cp.wait()

    outs = pl.pallas_call(
        wrapped, name=name, grid=grid,
        in_specs=in_specs + [ANY] * nc, out_specs=out_specs + [ANY] * nc,
        out_shape=out_shape + cout_shape,
        scratch_shapes=list(scratch_shapes) + sems,
        input_output_aliases=aliases,
        compiler_params=_params(len(grid)),
    )(*args, *cin)
    return list(outs)


def _sibling_swap(items, name):
    n = len(items)

    def body(*refs):
        src = refs[:n]
        dst = refs[n:2 * n]
        ssem, rsem = refs[2 * n:]
        x, y, c = _pos()
        sib = (x, y, 1 - c)
        cps = []
        for t in range(n):
            lead = (slice(None),) * (len(src[t].shape) - 2)
            cp = _rcopy(_half(src[t], 1 - c, lead), dst[t], ssem.at[t], rsem.at[t], sib)
            cp.start()
            cps.append(cp)
        for cp in cps:
            cp.wait()

    return pl.pallas_call(
        body, name=name, in_specs=[ANY] * n, out_specs=[ANY] * n,
        out_shape=[_sds(a.shape[:-2] + (a.shape[-2] // 2, a.shape[-1]), a.dtype) for a in items],
        scratch_shapes=[pltpu.SemaphoreType.DMA((n,)), pltpu.SemaphoreType.DMA((n,))],
    )(*items)


_HBM = pl.BlockSpec(memory_space=pltpu.HBM)
_SEM = pl.BlockSpec(memory_space=pltpu.SEMAPHORE)
_EFFECT = pltpu.SideEffectType.DATAFLOW_SIDE_EFFECTING


def _split_copies(src, land, nbig, nx, sems):
    return (_xchg_copies(src[:nx], land[:nx], nbig, sems[0:2]) if nx else []) + \
        (_swap_copies(src[nx:], land[nx:], sems[2:4]) if len(src) > nx else [])


def _comm_start(big, small, swaps, tag):
    xs = list(big) + list(small)
    items = xs + list(swaps)
    n, nx = len(items), len(xs)

    def body(*refs):
        for cp in _split_copies(refs[:n], refs[n:2 * n], len(big), nx, refs[2 * n:2 * n + 4]):
            cp.start()
        refs[-1][...] = jnp.zeros_like(refs[-1])

    lands = _xchg_out_shapes(xs) + _swap_out_shapes(swaps)
    sem_shapes = [pltpu.SemaphoreType.DMA((max(3 * nx, 1),))] * 2 + [pltpu.SemaphoreType.DMA((max(n - nx, 1),))] * 2
    hbm = lambda a: pltpu.with_memory_space_constraint(a, pltpu.HBM)
    outs = pl.pallas_call(
        body, name=f"rs_start_{tag}",
        out_shape=sem_shapes + [pltpu.HBM(a.shape, a.dtype) for a in items]
        + [pltpu.HBM(a.shape, a.dtype) for a in lands] + [_sds((8, 128), F32)],
        in_specs=[_HBM] * (2 * n), out_specs=[_SEM] * 4 + [_HBM] * (2 * n) + [pl.BlockSpec(memory_space=pltpu.VMEM)],
        input_output_aliases={t: 4 + t for t in range(2 * n)},
        compiler_params=pltpu.CompilerParams(has_side_effects=_EFFECT),
    )(*[hbm(a) for a in items], *[hbm(lax.empty(a.shape, a.dtype)) for a in lands])
    handle = (list(outs[:4]), list(outs[4:4 + n]), list(outs[4 + n:4 + 2 * n]), len(big), nx, tag)
    return handle, outs[-1]


def _comm_wait(handle, after):
    sems, srcs, lands, nbig, nx, tag = handle
    n = len(srcs)

    def body(*refs):
        for cp in _split_copies(refs[:n], refs[n:2 * n], nbig, nx, refs[2 * n:2 * n + 4]):
            cp.wait_send()
            cp.wait_recv()

    outs = pl.pallas_call(
        body, name=f"rs_wait_{tag}",
        out_shape=[pltpu.HBM(a.shape, a.dtype) for a in srcs] + [pltpu.HBM(a.shape, a.dtype) for a in lands],
        in_specs=[_HBM] * (2 * n) + [_SEM] * 4 + [ANY], out_specs=[_HBM] * (2 * n),
        input_output_aliases={t: t for t in range(2 * n)},
        compiler_params=pltpu.CompilerParams(has_side_effects=_EFFECT),
    )(*srcs, *lands, *sems, after)
    return list(outs[:n]), list(outs[n:])


def _join_copies(items, sems):
    ssem, rsem = sems
    x, y, c = _pos()
    return [_rcopy(_half(a, c), _half(a, c), ssem.at[t], rsem.at[t], (x, y, 1 - c)) for t, a in enumerate(items)]


def _join_start(items, tag, after=()):
    n = len(items)

    def body(*refs):
        for cp in _join_copies(refs[:n], refs[n + len(after):n + len(after) + 2]):
            cp.start()
        refs[-1][...] = jnp.zeros_like(refs[-1])

    hbm = lambda a: pltpu.with_memory_space_constraint(a, pltpu.HBM)
    outs = pl.pallas_call(
        body, name=f"rs_join_start_{tag}",
        out_shape=[pltpu.SemaphoreType.DMA((n,))] * 2 + [pltpu.HBM(a.shape, a.dtype) for a in items]
        + [_sds((8, 128), F32)],
        in_specs=[_HBM] * n + [ANY] * len(after),
        out_specs=[_SEM] * 2 + [_HBM] * n + [pl.BlockSpec(memory_space=pltpu.VMEM)],
        input_output_aliases={t: 2 + t for t in range(n)},
        compiler_params=pltpu.CompilerParams(has_side_effects=_EFFECT),
    )(*[hbm(a) for a in items], *after)
    return (list(outs[:2]), list(outs[2:2 + n]), tag), outs[-1]


def _join_wait(handle, after):
    sems, items, tag = handle
    n = len(items)

    def body(*refs):
        x, y, c = _pos()
        ssem, rsem = refs[n:n + 2]
        for t, cp in enumerate(_join_copies(refs[:n], (ssem, rsem))):
            cp.wait_send()
            theirs = _half(refs[t], 1 - c)
            _rcopy(theirs, theirs, ssem.at[t], rsem.at[t], (x, y, 1 - c)).wait_recv()

    outs = pl.pallas_call(
        body, name=f"rs_join_wait_{tag}",
        out_shape=[pltpu.HBM(a.shape, a.dtype) for a in items],
        in_specs=[_HBM] * n + [_SEM] * 2 + [ANY], out_specs=[_HBM] * n,
        input_output_aliases={t: t for t in range(n)},
        compiler_params=pltpu.CompilerParams(has_side_effects=_EFFECT),
    )(*items, *sems, after)
    return list(outs)


def _sibling_join(items, name, after=()):
    n = len(items)

    def body(*refs):
        dst = refs[n + len(after):2 * n + len(after)]
        ssem, rsem = refs[2 * n + len(after):]
        x, y, c = _pos()
        sib = (x, y, 1 - c)
        cps = []
        for t in range(n):
            mine = _half(dst[t], c)
            cp = _rcopy(mine, mine, ssem.at[t], rsem.at[t], sib)
            cp.start()
            cps.append(cp)
        for t, cp in enumerate(cps):
            theirs = _half(dst[t], 1 - c)
            _rcopy(theirs, theirs, ssem.at[t], rsem.at[t], sib).wait_recv()
            cp.wait_send()

    outs = pl.pallas_call(
        body, name=name, in_specs=[ANY] * (n + len(after)), out_specs=[ANY] * n,
        out_shape=[_sds(a.shape, a.dtype) for a in items],
        input_output_aliases={t: t for t in range(n)},
        scratch_shapes=[pltpu.SemaphoreType.DMA((n,))] * 2,
    )(*items, *after)
    return list(outs)


def _ev_in_fwd(h, g, wev, tm, carry=None):
    s = h.shape[0]

    def body(h_ref, g_ref, w_ref, z_ref):
        hhat, _ = _rms(h_ref[...])
        n = (hhat * g_ref[...]).astype(BF)
        z_ref[...] = _dot(n, w_ref[...]).astype(BF)

    return _call(
        body, name="ev_in_fwd", grid=(s // tm,),
        in_specs=[_rows(tm, D), _const((1, D), (0, 0)), _const((D, DZ), (0, 0))],
        out_specs=[_rows(tm, DZ)], out_shape=[_sds((s, DZ), BF)],
        args=(h, g, wev), carry=carry)


def _halo_specs(tm, width, s, col=0):
    per = tm // HALO
    nh = s // HALO
    prev = pl.BlockSpec((HALO, width), lambda i, _c=col: (jnp.maximum(i * per - 1, 0), _c))
    nxt = pl.BlockSpec((HALO, width), lambda i, _c=col: (jnp.minimum((i + 1) * per, nh - 1), _c))
    return prev, nxt


def _fill_pad(pad_ref, prev, main, nxt, i, tm):
    pad_ref[pl.ds(0, HALO), :] = jnp.where(_first(i), 0.0, prev)
    pad_ref[pl.ds(HALO, tm), :] = main
    pad_ref[pl.ds(HALO + tm, HALO), :] = jnp.where(_last(i), 0.0, nxt)


def _shift8(xs_ref, pad_ref, tm):
    for j in range(8):
        xs_ref[j] = pad_ref[pl.ds(j, tm + 2 * HALO - 8), :]


def _tap(xs_ref, r, off):
    return xs_ref[off % 8, pl.ds(r + 8 * (off // 8), RC), :]


def _glu_a(z):
    return z[:, :DA] * _sigmoid(z[:, DA:2 * DA])


def _gate_c(z):
    return z[:, 2048:] * z[:, 1024:1536]


def _ev_mix_fwd(h, z, wa, ba, lng, lnb, wb, bb, evo, tm, carry=None):
    s = h.shape[0]

    def body(h_ref, z_ref, zp_ref, zn_ref, wa_ref, ba_ref, lng_ref, lnb_ref, wb_ref, bb_ref, wo_ref,
             h1_ref, a2_ref, c2_ref, cat_ref, apad, cpad, c2f, xs):
        i = pl.program_id(0)
        zm = z_ref[...].astype(F32)
        zp = zp_ref[...].astype(F32)
        zn = zn_ref[...].astype(F32)
        _fill_pad(apad, _glu_a(zp), _glu_a(zm), _glu_a(zn), i, tm)
        _fill_pad(cpad, _gate_c(zp), _gate_c(zm), _gate_c(zn), i, tm)
        _shift8(xs, apad, tm)
        for r in range(0, tm, RC):
            acc = jnp.zeros((RC, DA), F32) + ba_ref[...]
            for k in range(KA):
                acc = acc + _tap(xs, r, k + 1) * wa_ref[pl.ds(k, 1), :]
            a2_ref[pl.ds(r, RC), :] = acc
            acc = jnp.zeros((RC, DA), F32) + bb_ref[...]
            for k in range(KB):
                acc = acc + cpad[pl.ds(r + HALO - 1 + k, RC), :] * wb_ref[pl.ds(k, 1), :]
            c2f[pl.ds(r, RC), :] = acc
        y, _ = _ln(a2_ref[...])
        a3 = y * lng_ref[...] + lnb_ref[...]
        cat_ref[:, :DA] = (a3 * _sigmoid(a3)).astype(BF)
        c2 = c2f[...]
        c2_ref[...] = c2.astype(BF)
        cat_ref[:, DA:] = (zm[:, 1536:2048] * c2).astype(BF)
        h1_ref[...] = h_ref[...] + _dot(cat_ref[...], wo_ref[...].reshape(D, D))

    zp_spec, zn_spec = _halo_specs(tm, DZ, s)
    small = lambda rows: _const((rows, DA), (0, 0))
    return _call(
        body, name="ev_mix_fwd", grid=(s // tm,),
        in_specs=[_rows(tm, D), _rows(tm, DZ), zp_spec, zn_spec, small(32), small(1), small(1), small(1),
                  small(8), small(1), _whole(evo)],
        out_specs=[_rows(tm, D), _rows(tm, DA), _rows(tm, DA), _rows(tm, D)],
        out_shape=[_sds((s, D), F32), _sds((s, DA), F32), _sds((s, DA), BF), _sds((s, D), BF)],
        scratch_shapes=[pltpu.VMEM((tm + 2 * HALO, DA), F32), pltpu.VMEM((tm + 2 * HALO, DA), F32),
                        pltpu.VMEM((tm, DA), F32), pltpu.VMEM((8, tm + 2 * HALO - 8, DA), F32)],
        args=(h, z, z, z, wa, ba, lng, lnb, wb, bb, evo), carry=carry)


def _ev_mix_bwd(dh, cat, a2, c2, z, lng, lnb, evo, tm, carry=None):
    s = dh.shape[0]

    def body(dh_ref, cat_ref, a2_ref, c2_ref, zgb_ref, lng_ref, lnb_ref, wo_ref,
             da2_ref, dc2_ref, dgb_ref, dlng_ref, dlnb_ref, gwo_ref, wacc):
        i = pl.program_id(0)

        @pl.when(_first(i))
        def _():
            wacc[...] = jnp.zeros_like(wacc)
            dlng_ref[...] = jnp.zeros_like(dlng_ref)
            dlnb_ref[...] = jnp.zeros_like(dlnb_ref)

        dhb = dh_ref[...].astype(BF)
        dcat = _dot_nt(dhb, wo_ref[...].reshape(D, D))
        wacc[...] += _dot_tn(cat_ref[...], dhb)
        y, rstd = _ln(a2_ref[...])
        a3 = y * lng_ref[...] + lnb_ref[...]
        sg = _sigmoid(a3)
        da3 = dcat[:, :DA] * (sg * (1.0 + a3 * (1.0 - sg)))
        dlng_ref[...] += _colsum(da3 * y)
        dlnb_ref[...] += _colsum(da3)
        da2_ref[...] = _ln_bwd(da3 * lng_ref[...], y, rstd).astype(BF)
        db = dcat[:, DA:]
        dc2_ref[...] = (db * zgb_ref[...].astype(F32)).astype(BF)
        dgb_ref[...] = (db * c2_ref[...].astype(F32)).astype(BF)

        @pl.when(_last(i))
        def _():
            gwo_ref[...] = wacc[...].astype(BF).reshape(NSH, 256, D)

    small = _const((1, DA), (0, 0))
    return _call(
        body, name="ev_mix_bwd", grid=(s // tm,),
        in_specs=[_rows(tm, D), _rows(tm, D), _rows(tm, DA), _rows(tm, DA), _rows(tm, DA, 3), small, small,
                  _whole(evo)],
        out_specs=[_rows(tm, DA), _rows(tm, DA), _rows(tm, DA), small, small, _whole(evo)],
        out_shape=[_sds((s, DA), BF), _sds((s, DA), BF), _sds((s, DA), BF), _sds((1, DA), F32),
                   _sds((1, DA), F32), _sds(evo.shape, BF)],
        scratch_shapes=[pltpu.VMEM((D, D), F32)],
        args=(dh, cat, a2, c2, z, lng, lnb, evo), carry=carry)


def _ev_in_bwd(h, dh, z, da2, dc2, dgb, g, wa, wb, wev, tm, after):
    s = h.shape[0]

    def body(h_ref, dh_ref, z_ref, zp_ref, zn_ref, da_ref, dap_ref, dan_ref, dc_ref, dcp_ref, dcn_ref,
             dgb_ref, g_ref, wa_ref, wb_ref, w_ref, after_ref,
             dh0_ref, dwa_ref, dba_ref, dwb_ref, dbb_ref, dg_ref, gw_ref,
             apad, cpad, dapad, dcpad, dz, wacc, dwa_acc, dwb_acc, xs):
        i = pl.program_id(0)

        @pl.when(_first(i))
        def _():
            wacc[...] = jnp.zeros_like(wacc)
            dwa_acc[...] = jnp.zeros_like(dwa_acc)
            dwb_acc[...] = jnp.zeros_like(dwb_acc)
            dba_ref[...] = jnp.zeros_like(dba_ref)
            dbb_ref[...] = jnp.zeros_like(dbb_ref)
            dg_ref[...] = jnp.zeros_like(dg_ref)

        zm = z_ref[...].astype(F32)
        zp = zp_ref[...].astype(F32)
        zn = zn_ref[...].astype(F32)
        _fill_pad(apad, _glu_a(zp), _glu_a(zm), _glu_a(zn), i, tm)
        _fill_pad(cpad, _gate_c(zp), _gate_c(zm), _gate_c(zn), i, tm)
        da2 = da_ref[...].astype(F32)
        dc2 = dc_ref[...].astype(F32)
        _fill_pad(dapad, dap_ref[...].astype(F32), da2, dan_ref[...].astype(F32), i, tm)
        _fill_pad(dcpad, dcp_ref[...].astype(F32), dc2, dcn_ref[...].astype(F32), i, tm)
        dba_ref[...] += _colsum(da2)
        dbb_ref[...] += _colsum(dc2)

        sg = _sigmoid(zm[:, DA:2 * DA])
        aval = zm[:, :DA]
        _shift8(xs, dapad, tm)
        for r in range(0, tm, RC):
            acc = jnp.zeros((RC, DA), F32)
            for k in range(KA):
                acc = acc + _tap(xs, r, 2 * HALO - 1 - k) * wa_ref[pl.ds(k, 1), :]
            sgr = sg[r:r + RC]
            dz[pl.ds(r, RC), 0:DA] = (acc * sgr).astype(BF)
            dz[pl.ds(r, RC), DA:2 * DA] = (acc * aval[r:r + RC] * sgr * (1.0 - sgr)).astype(BF)
            acc = jnp.zeros((RC, DA), F32)
            for k in range(KB):
                acc = acc + dcpad[pl.ds(r + HALO + 1 - k, RC), :] * wb_ref[pl.ds(k, 1), :]
            dz[pl.ds(r, RC), 1024:1536] = (acc * zm[r:r + RC, 2048:]).astype(BF)
            dz[pl.ds(r, RC), 2048:2560] = (acc * zm[r:r + RC, 1024:1536]).astype(BF)
        dz[:, 1536:2048] = dgb_ref[...]

        _shift8(xs, apad, tm)
        for k in range(KA):
            acc = jnp.zeros((8, DA), F32)
            for r in range(0, tm, RC):
                p = _tap(xs, r, k + 1) * dapad[pl.ds(r + HALO, RC), :]
                acc = acc + p.reshape(RC // 8, 8, DA).sum(axis=0)
            dwa_acc[k] += acc
        for k in range(KB):
            acc = jnp.zeros((8, DA), F32)
            for r in range(0, tm, RC):
                p = cpad[pl.ds(r + HALO - 1 + k, RC), :] * dcpad[pl.ds(r + HALO, RC), :]
                acc = acc + p.reshape(RC // 8, 8, DA).sum(axis=0)
            dwb_acc[k] += acc

        gg = g_ref[...]
        hhat, rr = _rms(h_ref[...])
        n = (hhat * gg).astype(BF)
        dzb = dz[...]
        wacc[...] += _dot_tn(n, dzb)
        dn = _dot_nt(dzb, w_ref[...])
        dg_ref[...] += _colsum(dn * hhat)
        dh0_ref[...] = dh_ref[...] + _rms_bwd(dn, hhat, rr, gg)

        @pl.when(_last(i))
        def _():
            for k in range(NSH):
                gw_ref[k] = wacc[:, 640 * k:640 * (k + 1)].astype(BF)
            dwa_ref[...] = jnp.sum(dwa_acc[...], axis=1)
            dwb_ref[...] = jnp.sum(dwb_acc[...], axis=1)

    zp_spec, zn_spec = _halo_specs(tm, DZ, s)
    hp_spec, hn_spec = _halo_specs(tm, DA, s)
    small = lambda rows: _const((rows, DA), (0, 0))
    return _call(
        body, name="ev_in_bwd", grid=(s // tm,),
        in_specs=[_rows(tm, D), _rows(tm, D), _rows(tm, DZ), zp_spec, zn_spec,
                  _rows(tm, DA), hp_spec, hn_spec, _rows(tm, DA), hp_spec, hn_spec, _rows(tm, DA),
                  _const((1, D), (0, 0)), small(32), small(8), _const((D, DZ), (0, 0)), ANY],
        out_specs=[_rows(tm, D), small(32), small(1), small(8), small(1), _const((1, D), (0, 0)),
                   _const((NSH, D, 640), (0, 0, 0))],
        out_shape=[_sds((s, D), F32), _sds((32, DA), F32), _sds((1, DA), F32), _sds((8, DA), F32),
                   _sds((1, DA), F32), _sds((1, D), F32), _sds((NSH, D, 640), BF)],
        scratch_shapes=[pltpu.VMEM((tm + 2 * HALO, DA), F32)] * 4 + [
            pltpu.VMEM((tm, DZ), BF), pltpu.VMEM((D, DZ), F32),
            pltpu.VMEM((32, 8, DA), F32), pltpu.VMEM((8, 8, DA), F32),
            pltpu.VMEM((8, tm + 2 * HALO - 8, DA), F32)],
        args=(h, dh, z, z, z, da2, da2, da2, dc2, dc2, dc2, dgb, g, wa, wb, wev, after))


def _xa_spec(item):
    return _const((NSH, 256, D), (0, item, 0))


def _kv_proj(mem, g, xa, layer):
    def body(mem_ref, g_ref, wk_ref, wv_ref, k_ref, v_ref):
        mhat, _ = _rms(mem_ref[...])
        mn = (mhat * g_ref[...]).astype(BF)
        k_ref[...] = _dot(mn, wk_ref[...].reshape(D, D)).astype(BF)
        v_ref[...] = _dot(mn, wv_ref[...].reshape(D, D)).astype(BF)

    full = _const((NMEM, D), (0, 0))
    return _call(
        body, name=f"kv_proj{layer}", grid=(1,),
        in_specs=[full, _const((1, D), (0, 0)), _xa_spec(XK), _xa_spec(XV)],
        out_specs=[full, full], out_shape=[_sds((NMEM, D), BF), _sds((NMEM, D), BF)],
        args=(mem, g, xa, xa))


def _softmax_rows(sc):
    m = jnp.max(sc, axis=-1, keepdims=True)
    e = jnp.exp(sc - m)
    return e / jnp.sum(e, axis=-1, keepdims=True)


def _xattn_fwd(h, g, k, v, xa, layer, tm, carry=None):
    s = h.shape[0]

    def body(h_ref, g_ref, k_ref, v_ref, wq_ref, wo_ref, ho_ref, q_ref, o_scr):
        hh = h_ref[...]
        hhat, _ = _rms(hh)
        n = (hhat * g_ref[...]).astype(BF)
        q = _dot(n, wq_ref[...].reshape(D, D)).astype(BF)
        q_ref[...] = q
        for hd in range(HEADS):
            cs = slice(HD * hd, HD * (hd + 1))
            p = _softmax_rows(_dot_nt(q[:, cs], k_ref[:, cs]) * ATT_SCALE)
            o_scr[:, cs] = _dot(p.astype(BF), v_ref[:, cs]).astype(BF)
        ho_ref[...] = hh + _dot(o_scr[...], wo_ref[...].reshape(D, D))

    kvs = _const((NMEM, D), (0, 0))
    return _call(
        body, name=f"xattn_fwd{layer}", grid=(s // tm,),
        in_specs=[_rows(tm, D), _const((1, D), (0, 0)), kvs, kvs, _xa_spec(XQ), _xa_spec(XO)],
        out_specs=[_rows(tm, D), _rows(tm, D)],
        out_shape=[_sds((s, D), F32), _sds((s, D), BF)],
        scratch_shapes=[pltpu.VMEM((tm, D), BF)],
        args=(h, g, k, v, xa, xa), carry=carry)


def _xattn_bwd(h, dh, q, g, k, v, xa, layer, tm, carry=None):
    s = h.shape[0]

    def body(h_ref, dh_ref, q_ref, g_ref, k_ref, v_ref, wq_ref, wo_ref,
             dhi_ref, dk_ref, dv_ref, dg_ref, gqo_ref, o_scr, dq_scr, qacc, oacc):
        i = pl.program_id(0)

        @pl.when(_first(i))
        def _():
            qacc[...] = jnp.zeros_like(qacc)
            oacc[...] = jnp.zeros_like(oacc)
            dk_ref[...] = jnp.zeros_like(dk_ref)
            dv_ref[...] = jnp.zeros_like(dv_ref)
            dg_ref[...] = jnp.zeros_like(dg_ref)

        gg = g_ref[...]
        dho = dh_ref[...]
        dhb = dho.astype(BF)
        q = q_ref[...]
        do = _dot_nt(dhb, wo_ref[...].reshape(D, D)).astype(BF)
        for hd in range(HEADS):
            cs = slice(HD * hd, HD * (hd + 1))
            kh = k_ref[:, cs]
            vh = v_ref[:, cs]
            p = _softmax_rows(_dot_nt(q[:, cs], kh) * ATT_SCALE)
            pb = p.astype(BF)
            o_scr[:, cs] = _dot(pb, vh).astype(BF)
            doh = do[:, cs]
            dp = _dot_nt(doh, vh)
            dv_ref[:, cs] += _dot_tn(pb, doh)
            ds = (p * (dp - jnp.sum(dp * p, axis=-1, keepdims=True)) * ATT_SCALE).astype(BF)
            dq_scr[:, cs] = _dot(ds, kh).astype(BF)
            dk_ref[:, cs] += _dot_tn(ds, q[:, cs])
        oacc[...] += _dot_tn(o_scr[...], dhb)
        hhat, rr = _rms(h_ref[...])
        n = (hhat * gg).astype(BF)
        dq = dq_scr[...]
        qacc[...] += _dot_tn(n, dq)
        dn = _dot_nt(dq, wq_ref[...].reshape(D, D))
        dg_ref[...] += _colsum(dn * hhat)
        dhi_ref[...] = dho + _rms_bwd(dn, hhat, rr, gg)

        @pl.when(_last(i))
        def _():
            gqo_ref[:, 0:256, :] = qacc[...].astype(BF).reshape(NSH, 256, D)
            gqo_ref[:, 256:512, :] = oacc[...].astype(BF).reshape(NSH, 256, D)

    kvs = _const((NMEM, D), (0, 0))
    return _call(
        body, name=f"xattn_bwd{layer}", grid=(s // tm,),
        in_specs=[_rows(tm, D), _rows(tm, D), _rows(tm, D), _const((1, D), (0, 0)), kvs, kvs,
                  _xa_spec(XQ), _xa_spec(XO)],
        out_specs=[_rows(tm, D), kvs, kvs, _const((1, D), (0, 0)), _const((NSH, 512, D), (0, 0, 0))],
        out_shape=[_sds((s, D), F32), _sds((NMEM, D), F32), _sds((NMEM, D), F32), _sds((1, D), F32),
                   _sds(xa.shape, BF)],
        scratch_shapes=[pltpu.VMEM((tm, D), BF), pltpu.VMEM((tm, D), BF), pltpu.VMEM((D, D), F32),
                        pltpu.VMEM((D, D), F32)],
        args=(h, dh, q, g, k, v, xa, xa), carry=carry)


def _kv_proj_bwd(mem, g, dk, dv, xa, gxa, layer):
    def body(mem_ref, g_ref, dk_ref, dv_ref, wk_ref, wv_ref, gxa_in, dg_ref, gkv_ref):
        mhat, _ = _rms(mem_ref[...])
        mn = (mhat * g_ref[...]).astype(BF)
        dkb = dk_ref[...].astype(BF)
        dvb = dv_ref[...].astype(BF)
        gkv_ref[:, 0:256, :] = _dot_tn(mn, dkb).astype(BF).reshape(NSH, 256, D)
        gkv_ref[:, 256:512, :] = _dot_tn(mn, dvb).astype(BF).reshape(NSH, 256, D)
        dmn = _dot_nt(dkb, wk_ref[...].reshape(D, D)) + _dot_nt(dvb, wv_ref[...].reshape(D, D))
        dg_ref[...] = _colsum(dmn * mhat)

    full = _const((NMEM, D), (0, 0))
    return _call(
        body, name=f"kv_proj_bwd{layer}", grid=(1,),
        in_specs=[full, _const((1, D), (0, 0)), full, full, _xa_spec(XK), _xa_spec(XV), ANY],
        out_specs=[_const((1, D), (0, 0)), _const((NSH, 512, D), (0, 1, 0))],
        out_shape=[_sds((1, D), F32), _sds(gxa.shape, BF)],
        aliases={6: 1},
        args=(mem, g, dk, dv, xa, xa, gxa))


def _ffn_block(h_ref, g_ref, wg_ref, wu_ref, wd_ref, gt_ref, up_ref):
    hh = h_ref[...]
    hhat, _ = _rms(hh)
    n = (hhat * g_ref[...]).astype(BF)
    out = hh
    for c0, c1 in FF_CHUNKS:
        gt = _dot_nt(n, wg_ref[c0:c1, :])
        up = _dot_nt(n, wu_ref[c0:c1, :])
        gt_ref[:, c0:c1] = gt.astype(BF)
        up_ref[:, c0:c1] = up.astype(BF)
        act = (gt * _sigmoid(gt) * up).astype(BF)
        out = out + _dot(act, wd_ref[c0:c1, :])
    return out


def _ffn_fwd(h, g, wg, wu, dn, layer, tm, carry=None):
    s = h.shape[0]
    wg, wu, dn = (a.reshape(DFF, D) for a in (wg, wu, dn))

    def body(h_ref, g_ref, wg_ref, wu_ref, wd_ref, ho_ref, gt_ref, up_ref):
        ho_ref[...] = _ffn_block(h_ref, g_ref, wg_ref, wu_ref, wd_ref, gt_ref, up_ref)

    return _call(
        body, name=f"ffn_fwd{layer}", grid=(s // tm,),
        in_specs=[_rows(tm, D), _const((1, D), (0, 0)), _whole(wg), _whole(wu), _whole(dn)],
        out_specs=[_rows(tm, D), _rows(tm, DFF), _rows(tm, DFF)],
        out_shape=[_sds((s, D), F32), _sds((s, DFF), BF), _sds((s, DFF), BF)],
        args=(h, g, wg, wu, dn), carry=carry)


def _ffn_fwd_loss(h, g, wg, wu, dn, tgt, g_final, layer, tm):
    s = h.shape[0]
    wg, wu, dn = (a.reshape(DFF, D) for a in (wg, wu, dn))

    def body(h_ref, g_ref, wg_ref, wu_ref, wd_ref, t_ref, gf_ref, dh_ref, loss_ref, dg_ref, gt_ref, up_ref):
        i = pl.program_id(0)

        @pl.when(_first(i))
        def _():
            loss_ref[...] = jnp.zeros_like(loss_ref)
            dg_ref[...] = jnp.zeros_like(dg_ref)

        out = _ffn_block(h_ref, g_ref, wg_ref, wu_ref, wd_ref, gt_ref, up_ref)
        gg = gf_ref[...]
        hhat, rr = _rms(out)
        diff = hhat * gg - t_ref[...]
        loss_ref[...] += jnp.sum(diff * diff) * (0.5 / D)
        dout = diff * (1.0 / D)
        dg_ref[...] += _colsum(dout * hhat)
        dh_ref[...] = _rms_bwd(dout, hhat, rr, gg)

    vec = _const((1, D), (0, 0))
    return _call(
        body, name=f"ffn_fwd_loss{layer}", grid=(s // tm,),
        in_specs=[_rows(tm, D), vec, _whole(wg), _whole(wu), _whole(dn), _rows(tm, D), vec],
        out_specs=[_rows(tm, D), _const((1, 128), (0, 0)), vec, _rows(tm, DFF), _rows(tm, DFF)],
        out_shape=[_sds((s, D), F32), _sds((1, 128), F32), _sds((1, D), F32), _sds((s, DFF), BF),
                   _sds((s, DFF), BF)],
        args=(h, g, wg, wu, dn, tgt, g_final))


def _ffn_bwd_dx(h, dh, gt, up, g, wg, wu, dn, layer, tm, carry=None):
    s = h.shape[0]
    wg, wu, dn = (a.reshape(DFF, D) for a in (wg, wu, dn))

    def body(h_ref, dh_ref, gt_ref, up_ref, g_ref, wg_ref, wu_ref, wd_ref,
             dhi_ref, dg_ref, dgt_ref, dup_ref, act_ref, n_ref):
        i = pl.program_id(0)

        @pl.when(_first(i))
        def _():
            dg_ref[...] = jnp.zeros_like(dg_ref)

        gg = g_ref[...]
        dho = dh_ref[...]
        dhb = dho.astype(BF)
        dn_ = jnp.zeros((tm, D), F32)
        for c0, c1 in FF_CHUNKS:
            gtv = gt_ref[:, c0:c1].astype(F32)
            upv = up_ref[:, c0:c1].astype(F32)
            sg = _sigmoid(gtv)
            silu = gtv * sg
            dact = _dot_nt(dhb, wd_ref[c0:c1, :])
            dgt = (dact * upv * (sg * (1.0 + gtv * (1.0 - sg)))).astype(BF)
            dup = (dact * silu).astype(BF)
            dgt_ref[:, c0:c1] = dgt
            dup_ref[:, c0:c1] = dup
            act_ref[:, c0:c1] = (silu * upv).astype(BF)
            dn_ = dn_ + _dot(dgt, wg_ref[c0:c1, :]) + _dot(dup, wu_ref[c0:c1, :])
        hhat, rr = _rms(h_ref[...])
        n_ref[...] = (hhat * gg).astype(BF)
        dg_ref[...] += _colsum(dn_ * hhat)
        dhi_ref[...] = dho + _rms_bwd(dn_, hhat, rr, gg)

    wide = _rows(tm, DFF)
    return _call(
        body, name=f"ffn_bwd_dx{layer}", grid=(s // tm,),
        in_specs=[_rows(tm, D), _rows(tm, D), wide, wide, _const((1, D), (0, 0)),
                  _whole(wg), _whole(wu), _whole(dn)],
        out_specs=[_rows(tm, D), _const((1, D), (0, 0)), wide, wide, wide, _rows(tm, D)],
        out_shape=[_sds((s, D), F32), _sds((1, D), F32), _sds((s, DFF), BF), _sds((s, DFF), BF),
                   _sds((s, DFF), BF), _sds((s, D), BF)],
        args=(h, dh, gt, up, g, wg, wu, dn), carry=carry)


def _ffn_wgrad(a, b, name, tm, slot, into=None, carry=None):
    s = a.shape[0]
    tm = min(tm, s)

    def body(a_ref, b_ref, *rest):
        o_ref, acc = rest[-2:]
        i = pl.program_id(0)

        @pl.when(_first(i))
        def _():
            acc[...] = jnp.zeros_like(acc)

        acc[...] += _dot_tn(a_ref[...], b_ref[...].astype(BF))

        @pl.when(_last(i))
        def _():
            o_ref[...] = acc[...].astype(BF).reshape(NSH, FSH, D)

    chained = into is not None
    return _call(
        body, name=name, grid=(s // tm,),
        in_specs=[_rows(tm, DFF), _rows(tm, D)] + [ANY] * chained,
        out_specs=[_const((NSH, FSH, D), (0, slot, 0))],
        out_shape=[_sds((NSH, 3 * FSH, D), BF)],
        scratch_shapes=[pltpu.VMEM((DFF, D), F32)],
        aliases={2: 0} if chained else None,
        args=(a, b) + ((into,) if chained else ()), carry=carry)


def _sgu_fwd(h, g, lng, lnb, ws, bsb, wod, odo, tm, carry=None):
    s = h.shape[0]

    def body(h_ref, g_ref, lng_ref, lnb_ref, ws_ref, bsb_ref, wi_ref, wo_ref, ho_ref, zp_ref, y_scr):
        hh = h_ref[...]
        hhat, _ = _rms(hh)
        n = (hhat * g_ref[...]).astype(BF)
        zpre = _dot(n, wi_ref[...])
        zp_ref[...] = zpre.astype(BF)
        z = _gelu(zpre)
        u = z[:, :D]
        y, _ = _ln(z[:, D:])
        vn = (y * lng_ref[...] + lnb_ref[...]).astype(BF)
        for c in range(tm // CHUNK):
            rs = slice(CHUNK * c, CHUNK * (c + 1))
            for gi in range(GROUPS):
                cs = slice(CHUNK * gi, CHUNK * (gi + 1))
                sv = _dot(ws_ref[gi], vn[rs, cs]) + bsb_ref[gi]
                y_scr[rs, cs] = (u[rs, cs] * sv).astype(BF)
        ho_ref[...] = hh + _dot(y_scr[...], wo_ref[...].reshape(D, D))

    vec = _const((1, D), (0, 0))
    sq = _const((GROUPS, CHUNK, CHUNK), (0, 0, 0))
    return _call(
        body, name="sgu_fwd", grid=(s // tm,),
        in_specs=[_rows(tm, D), vec, vec, vec, sq, sq, _const((D, 2 * D), (0, 0)), _whole(odo)],
        out_specs=[_rows(tm, D), _rows(tm, 2 * D)],
        out_shape=[_sds((s, D), F32), _sds((s, 2 * D), BF)],
        scratch_shapes=[pltpu.VMEM((tm, D), BF)],
        args=(h, g, lng, lnb, ws, bsb, wod, odo), carry=carry)


def _sgu_bwd(h, dh, zpre, g, lng, lnb, ws, wst, bsb, wod, odo, tm, carry=None):
    s = h.shape[0]

    def body(h_ref, dh_ref, zp_ref, g_ref, lng_ref, lnb_ref, ws_ref, wst_ref, bsb_ref, wi_ref, wo_ref,
             dhi_ref, dws_ref, dbs_ref, dlng_ref, dlnb_ref, dg_ref, gwi_ref, gwo_ref,
             y_scr, dz_scr, dvn_scr, iacc, oacc, bacc):
        i = pl.program_id(0)

        @pl.when(_first(i))
        def _():
            for ref in (iacc, oacc, bacc, dws_ref, dlng_ref, dlnb_ref, dg_ref):
                ref[...] = jnp.zeros_like(ref)

        gg = g_ref[...]
        dho = dh_ref[...]
        dhb = dho.astype(BF)
        zpre_v = zp_ref[...].astype(F32)
        z = _gelu(zpre_v)
        u = z[:, :D]
        yl, rstd = _ln(z[:, D:])
        vn = (yl * lng_ref[...] + lnb_ref[...]).astype(BF)
        dy = _dot_nt(dhb, wo_ref[...].reshape(D, D))
        for c in range(tm // CHUNK):
            rs = slice(CHUNK * c, CHUNK * (c + 1))
            for gi in range(GROUPS):
                cs = slice(CHUNK * gi, CHUNK * (gi + 1))
                vb = vn[rs, cs]
                sv = _dot(ws_ref[gi], vb) + bsb_ref[gi]
                ub = u[rs, cs]
                dyb = dy[rs, cs]
                y_scr[rs, cs] = (ub * sv).astype(BF)
                dz_scr[rs, cs] = dyb * sv
                dsv = dyb * ub
                bacc[gi] += dsv
                dsvb = dsv.astype(BF)
                dws_ref[gi] += _dot_nt(dsvb, vb)
                dvn_scr[rs, cs] = _dot(wst_ref[gi], dsvb)
        oacc[...] += _dot_tn(y_scr[...], dhb)
        dvn = dvn_scr[...]
        dlng_ref[...] += _colsum(dvn * yl)
        dlnb_ref[...] += _colsum(dvn)
        dz_scr[:, D:] = _ln_bwd(dvn * lng_ref[...], yl, rstd)
        dzb = (dz_scr[...] * _gelu_grad(zpre_v)).astype(BF)
        hhat, rr = _rms(h_ref[...])
        n = (hhat * gg).astype(BF)
        iacc[...] += _dot_tn(n, dzb)
        dn = _dot_nt(dzb, wi_ref[...])
        dg_ref[...] += _colsum(dn * hhat)
        dhi_ref[...] = dho + _rms_bwd(dn, hhat, rr, gg)

        @pl.when(_last(i))
        def _():
            for k in range(NSH):
                gwi_ref[k] = iacc[:, 512 * k:512 * (k + 1)].astype(BF)
            gwo_ref[...] = oacc[...].astype(BF).reshape(NSH, 256, D)
            dbs_ref[...] = jnp.sum(bacc[...], axis=-1)

    vec = _const((1, D), (0, 0))
    sq = _const((GROUPS, CHUNK, CHUNK), (0, 0, 0))
    return _call(
        body, name="sgu_bwd", grid=(s // tm,),
        in_specs=[_rows(tm, D), _rows(tm, D), _rows(tm, 2 * D), vec, vec, vec, sq, sq, sq,
                  _const((D, 2 * D), (0, 0)), _whole(odo)],
        out_specs=[_rows(tm, D), sq, _const((GROUPS, CHUNK), (0, 0)), vec, vec, vec,
                   _const((NSH, D, 512), (0, 0, 0)), _whole(odo)],
        out_shape=[_sds((s, D), F32), _sds((GROUPS, CHUNK, CHUNK), F32), _sds((GROUPS, CHUNK), F32),
                   _sds((1, D), F32), _sds((1, D), F32), _sds((1, D), F32), _sds((NSH, D, 512), BF),
                   _sds(odo.shape, BF)],
        scratch_shapes=[pltpu.VMEM((tm, D), BF), pltpu.VMEM((tm, 2 * D), F32),
                        pltpu.VMEM((tm, D), F32), pltpu.VMEM((D, 2 * D), F32), pltpu.VMEM((D, D), F32),
                        pltpu.VMEM((GROUPS, CHUNK, CHUNK), F32)],
        args=(h, dh, zpre, g, lng, lnb, ws, wst, bsb, wod, odo), carry=carry)


def _own_plane(shape):
    return pl.BlockSpec((1,) + shape, lambda i, chip_ref: (chip_ref[0], 0, 0), pipeline_mode=pl.Buffered(1))


def _pack_rows(chip_idx, w_down, w_gate_t, w_up_t, xq, xo, xk, xv, ev_out, od_out, first):
    nf = len(first)

    def body(chip_ref, wd_ref, wg_ref, wu_ref, xq_ref, xo_ref, xk_ref, xv_ref, ev_ref, od_ref, *rest):
        (evo_ref, odo_ref, dn0_ref, dn1_ref, wg0_ref, wg1_ref, wu0_ref, wu1_ref, xa0_ref,
         xa1_ref) = rest[nf:nf + 10]
        gathered = rest[nf + 10:2 * nf + 10]
        sems = rest[2 * nf + 10:]
        _gather_start(gathered, sems)
        evo_ref[0] = ev_ref[0].astype(BF)
        odo_ref[0] = od_ref[0].astype(BF)
        for src, outs in ((wd_ref, (dn0_ref, dn1_ref)), (wg_ref, (wg0_ref, wg1_ref)), (wu_ref, (wu0_ref, wu1_ref))):
            for layer, o_ref in enumerate(outs):
                o_ref[0] = src[layer].astype(BF)
        for layer, xa_ref in enumerate((xa0_ref, xa1_ref)):
            for item, src in ((XQ, xq_ref), (XO, xo_ref), (XK, xk_ref), (XV, xv_ref)):
                xa_ref[0, 256 * item:256 * (item + 1), :] = src[layer].astype(BF)
        _gather_forward(gathered, sems)
        _gather_drain(gathered, sems)

    ins = (w_down, w_gate_t, w_up_t, xq, xo, xk, xv, ev_out, od_out)
    shapes = [(256, D), (256, D)] + [(FSH, D)] * 6 + [(4 * 256, D), (4 * 256, D)]
    grid_spec = pltpu.PrefetchScalarGridSpec(
        num_scalar_prefetch=1, grid=(1,), in_specs=[_whole(a) for a in ins] + [ANY] * nf,
        out_specs=[_own_plane(sh) for sh in shapes] + [ANY] * nf,
        scratch_shapes=_gather_sems(nf))
    outs = pl.pallas_call(
        body, name="pack_rows", grid_spec=grid_spec,
        out_shape=[_sds((NSH,) + sh, BF) for sh in shapes] + [_sds(a.shape, a.dtype) for a in first],
        input_output_aliases={1 + len(ins) + t: len(shapes) + t for t in range(nf)},
        compiler_params=_params(),
    )(chip_idx, *ins, *first)
    return outs[:len(shapes)], outs[len(shapes):]


def _pack_cols(chip_idx, od_in, ev_in, conv_a, conv_b, ln_g, ln_b):
    def body(chip_ref, od_ref, ev_ref, ca_ref, cb_ref, lg_ref, lb_ref, cev_ref, cod_ref, sp_ref):
        cev_ref[0] = ev_ref[0].astype(BF)
        cod_ref[0] = od_ref[0].astype(BF)
        sp_ref[...] = jnp.zeros_like(sp_ref)
        sp_ref[0, 0:KA, 0:128] = ca_ref[0]
        sp_ref[0, 32:32 + KB, 0:128] = cb_ref[0]
        sp_ref[0, 40:41, :] = lg_ref[...]
        sp_ref[0, 41:42, :] = lb_ref[...]

    ins = (od_in, ev_in, conv_a, conv_b, ln_g, ln_b)
    shapes = [(D, 640), (D, 512)]
    grid_spec = pltpu.PrefetchScalarGridSpec(
        num_scalar_prefetch=1, grid=(1,), in_specs=[_whole(a) for a in ins],
        out_specs=[_own_plane(sh) for sh in shapes] + [_own_plane((SP_ROWS, 256))])
    return pl.pallas_call(
        body, name="pack_cols", grid_spec=grid_spec,
        out_shape=[_sds((NSH,) + sh, BF) for sh in shapes] + [_sds((NSH, SP_ROWS, 256), F32)],
        compiler_params=_params(),
    )(chip_idx, *ins)


def _assemble(piece, parts, name):
    tr = 256
    cols = piece.shape[2]

    def body(p_ref, *outs):
        for k in range(NSH):
            for (off, width), o_ref in zip(parts, outs):
                o_ref[:, width * k:width * (k + 1)] = p_ref[k, :, off:off + width]

    return _call(
        body, name=name, grid=(D // tr,),
        in_specs=[pl.BlockSpec((NSH, tr, cols), lambda i: (0, i, 0))],
        out_specs=[pl.BlockSpec((tr, NSH * width), lambda i: (i, 0)) for _, width in parts],
        out_shape=[_sds((D, NSH * width), BF) for _, width in parts],
        args=(piece,))


def _blocks_of(rows, cols, itemsize):
    rb = rows
    while rb * cols * itemsize > (2 << 20) and rb % 2 == 0 and (rb // 2) % 16 == 0:
        rb //= 2
    return rb


def _add_half(full, recv, c_idx, name):
    p, r, cols = full.shape
    rb = _blocks_of(r // 2, cols, 4)
    nb = (r // 2) // rb

    def body(c_ref, a_ref, b_ref, o_ref):
        o_ref[...] = (a_ref[...].astype(F32) + b_ref[...].astype(F32)).astype(o_ref.dtype)

    grid_spec = pltpu.PrefetchScalarGridSpec(
        num_scalar_prefetch=1, grid=(p, nb),
        in_specs=[pl.BlockSpec((1, rb, cols), lambda j, i, c_ref: (j, c_ref[0] * nb + i, 0)),
                  pl.BlockSpec((1, rb, cols), lambda j, i, c_ref: (j, i, 0))],
        out_specs=pl.BlockSpec((1, rb, cols), lambda j, i, c_ref: (j, i, 0)))
    return pl.pallas_call(
        body, name=name, grid_spec=grid_spec, out_shape=_sds(recv.shape, full.dtype),
        compiler_params=_params(2),
    )(c_idx, full, recv)


def _sum_chips(own, got, plane_c, name):
    _, r, cols = own.shape
    rb = _blocks_of(r, cols, 4)
    nb = r // rb

    def body(s_ref, a_ref, b_ref, o_ref):
        f = lambda v: v.astype(F32)
        o_ref[...] = (f(a_ref[0]) + f(b_ref[1])) + (f(b_ref[0]) + f(b_ref[2]))

    grid_spec = pltpu.PrefetchScalarGridSpec(
        num_scalar_prefetch=1, grid=(nb,),
        in_specs=[pl.BlockSpec((1, rb, cols), lambda i, s_ref: (s_ref[0], i, 0)),
                  pl.BlockSpec((3, rb, cols), lambda i, s_ref: (0, i, 0))],
        out_specs=pl.BlockSpec((rb, cols), lambda i, s_ref: (s_ref[1] * nb + i, 0)))
    return pl.pallas_call(
        body, name=name, grid_spec=grid_spec, out_shape=_sds((2 * r, cols), F32),
        compiler_params=_params(1),
    )(plane_c, own, got)


def _adamw_math(w, g, m, v):
    m = ADAM_B1 * m + (1.0 - ADAM_B1) * g
    v = ADAM_B2 * v + (1.0 - ADAM_B2) * (g * g)
    m_hat = m / (1.0 - ADAM_B1 ** ADAM_STEP)
    v_hat = v / (1.0 - ADAM_B2 ** ADAM_STEP)
    delta = -ADAM_LR * (m_hat / (jnp.sqrt(v_hat) + ADAM_EPS) + ADAM_WD * w)
    return delta, m, v


def _adamw_big(w, m, v, gsrcs, gblock, gidx, name, rb):
    nl, r, cols = w.shape
    assert len(gsrcs) == nl

    def body(w_ref, m_ref, v_ref, *rest):
        g_refs, (go_ref, d_ref, mo_ref, vo_ref) = rest[:nl], rest[nl:]
        g = g_refs[0][:, 0:cols]
        if nl == 2:
            g = jnp.where(pl.program_id(0) == 0, g, g_refs[1][:, 0:cols])
        delta, mn, vn = _adamw_math(w_ref[0], g, m_ref[0], v_ref[0])
        go_ref[0] = g
        d_ref[0] = delta
        mo_ref[0] = mn
        vo_ref[0] = vn

    wspec = pl.BlockSpec((1, rb, cols), lambda l, i: (l, i, 0))
    gspec = pl.BlockSpec(gblock, lambda l, i: gidx(i))
    return pl.pallas_call(
        body, name=name, grid=(nl, r // rb),
        in_specs=[wspec, wspec, wspec] + [gspec] * nl,
        out_specs=[wspec] * 4, out_shape=[_sds(w.shape, F32)] * 4,
        compiler_params=_params(2),
    )(w, m, v, *gsrcs)


_SMALL = (
    ("g_mix", (2, D), "a", (0, 2), 0, False),
    ("g_xattn", (2, D), "a", (2, 4), 0, False),
    ("g_mem", (2, D), "a", (4, 6), 0, False),
    ("g_ffn", (2, D), "a", (6, 8), 0, False),
    ("g_final", (1, D), "a", (8, 9), 0, False),
    ("ev_a_conv_w", (KA, 128), "a", (16, 16 + KA), 0, True),
    ("ev_a_conv_b", (1, DA), "a", (11, 12), 0, False),
    ("ev_a_ln_g", (1, DA), "a", (11, 12), DA, False),
    ("ev_a_ln_b", (1, DA), "a", (12, 13), 0, False),
    ("ev_b_conv_w", (KB, 128), "a", (48, 48 + KB), 0, True),
    ("ev_b_conv_b", (1, DA), "a", (12, 13), DA, False),
    ("od_c_ln_g", (1, 256), "a", (9, 10), 0, True),
    ("od_c_ln_b", (1, 256), "a", (10, 11), 0, True),
    ("od_w_s", (GROUPS * CHUNK, CHUNK), "b", (0, GROUPS * CHUNK), 0, False),
    ("od_b_s", (GROUPS, CHUNK), "a", (56, 64), 0, False),
)


def _adamw_small(ga, gb, wmv):
    ns = len(_SMALL)

    def body(*refs):
        ga_ref, gb_ref = refs[0], refs[1]
        ins = refs[2:2 + 3 * ns]
        outs = refs[2 + 3 * ns:]
        x, y, _ = _pos()
        chip = 2 * x + y
        for j, (_, shape, pack, (r0, r1), col, sharded) in enumerate(_SMALL):
            src = ga_ref if pack == "a" else gb_ref
            width = shape[1]
            if sharded:
                g = jnp.zeros(shape, F32)
                for k in range(NSH):
                    blk = src[r0:r1, col + width * k:col + width * (k + 1)]
                    g = g + jnp.where(chip == k, blk, 0.0)
            else:
                g = src[r0:r1, col:col + width]
            w_ref, m_ref, v_ref = ins[3 * j:3 * j + 3]
            delta, mn, vn = _adamw_math(w_ref[...], g, m_ref[...], v_ref[...])
            go_ref, d_ref, mo_ref, vo_ref = outs[4 * j:4 * j + 4]
            go_ref[...] = g
            d_ref[...] = delta
            mo_ref[...] = mn
            vo_ref[...] = vn

    vm = pl.BlockSpec(memory_space=pltpu.VMEM)
    flat = [a for trio in wmv for a in trio]
    out_shape = []
    for (_, shape, *_rest) in _SMALL:
        out_shape += [_sds(shape, F32)] * 4
    outs = pl.pallas_call(
        body, name="adamw_small",
        in_specs=[vm] * (2 + 3 * ns), out_specs=[vm] * (4 * ns), out_shape=out_shape,
    )(ga, gb, *flat)
    return [tuple(outs[4 * j:4 * j + 4]) for j in range(ns)]


def _small_params(gs, W):
    unshard = lambda a: jnp.transpose(a, (1, 0, 2)).reshape(a.shape[1], NSH * a.shape[2])
    od_w_s, od_b_s = W["od_w_s"], W["od_b_s"]
    return dict(
        g_mix=W["g_mix"], g_xattn=W["g_xattn"], g_mem=W["g_mem"], g_ffn=W["g_ffn"],
        g_final=W["g_final"].reshape(1, D),
        wa=unshard(gs[:, 0:32, 0:128]), ba=W["ev_a_conv_b"], lng_a=W["ev_a_ln_g"], lnb_a=W["ev_a_ln_b"],
        wb=unshard(gs[:, 32:40, 0:128]), bb=W["ev_b_conv_b"],
        lng_c=unshard(gs[:, 40:41, :]), lnb_c=unshard(gs[:, 41:42, :]),
        ws=od_w_s[0].astype(BF), wst=jnp.swapaxes(od_w_s[0], 1, 2).astype(BF),
        bsb=jnp.broadcast_to(od_b_s[0][:, :, None], (GROUPS, CHUNK, CHUNK)),
    )


def _small_grad_packs(g):
    z = lambda r, c: jnp.zeros((r, c), F32)
    wide = lambda a: jnp.concatenate([a, z(a.shape[0], D - a.shape[1])], axis=1)
    rows = [g["g_mix"], g["g_xattn"], g["g_mem"], g["g_ffn"], g["g_final"], g["od_c_ln_g"], g["od_c_ln_b"],
            jnp.concatenate([g["ev_a_conv_b"], g["ev_a_ln_g"]], axis=1),
            jnp.concatenate([g["ev_a_ln_b"], g["ev_b_conv_b"]], axis=1),
            z(3, D), wide(g["ev_a_conv_w"]), wide(g["ev_b_conv_w"]), wide(g["od_b_s"])]
    a = jnp.concatenate(rows, axis=0)
    assert a.shape == (SA_ROWS, D), a.shape
    return a, g["od_w_s"].reshape(GROUPS * CHUNK, CHUNK)


def _step(x, mem, tgt, W, P, c_idx, tm_f, tm_b, tm_w):
    row = lambda a, i: a[i:i + 1]

    sm = _small_params(P["gs"], W)
    (wev,) = _assemble(P["cev"], [(0, 640)], "assemble_ev")
    z, evo, xa0 = _ev_in_fwd(x, row(sm["g_mix"], 0), wev, 2 * tm_f, carry=("gather", [P["evo"], P["xa0"]]))
    h1, a2, c2, cat, wg0, wu0 = _ev_mix_fwd(
        x, z, sm["wa"], sm["ba"], sm["lng_a"], sm["lnb_a"], sm["wb"], sm["bb"], evo, tm_f,
        carry=("gather", [P["wg0"], P["wu0"]]))
    k0, v0 = _kv_proj(mem, row(sm["g_mem"], 0), xa0, 0)
    h2, q0, dn0, cod = _xattn_fwd(h1, row(sm["g_xattn"], 0), k0, v0, xa0, 0, 2 * tm_f,
                                  carry=("gather", [P["dn0"], P["cod"]]))
    (wod,) = _assemble(cod, [(0, 512)], "assemble_od")
    h3, gt0, up0, odo, xa1, wg1 = _ffn_fwd(h2, row(sm["g_ffn"], 0), wg0, wu0, dn0, 0, tm_f,
                                           carry=("gather", [P["odo"], P["xa1"], P["wg1"]]))
    h4, zpre, wu1, dn1 = _sgu_fwd(h3, row(sm["g_mix"], 1), sm["lng_c"], sm["lnb_c"], sm["ws"], sm["bsb"], wod,
                                  odo, tm_f, carry=("gather", [P["wu1"], P["dn1"]]))
    k1, v1 = _kv_proj(mem, row(sm["g_mem"], 1), xa1, 1)
    h5, q1 = _xattn_fwd(h4, row(sm["g_xattn"], 1), k1, v1, xa1, 1, 2 * tm_f)
    dh6, loss, dg_final, gt1, up1 = _ffn_fwd_loss(h5, row(sm["g_ffn"], 1), wg1, wu1, dn1, tgt, sm["g_final"], 1,
                                                  tm_f)

    g = {"g_final": dg_final}
    parts, gots = {}, {}
    def add(name, piece, recv):
        parts[name] = _add_half(piece, recv, c_idx, f"rs_add_{name}")
        return parts[name]

    def ffn_wgrads(layer, dgt, dup, act, nb, dh_out, carry=None):
        g_ffn, *carried = _ffn_wgrad(dgt, nb, f"ffn_wgrad_gate{layer}", tm_w, 0, carry=carry)
        (g_ffn,) = _ffn_wgrad(dup, nb, f"ffn_wgrad_up{layer}", tm_w, 1, into=g_ffn)
        (g_ffn,) = _ffn_wgrad(act, dh_out, f"ffn_wgrad_down{layer}", tm_w, 2, into=g_ffn)
        return g_ffn, carried

    dh5, dg_ffn1, dgt, dup, act, nb = _ffn_bwd_dx(h5, dh6, gt1, up1, row(sm["g_ffn"], 1), wg1, wu1, dn1, 1, tm_b)
    g_ffn1, _ = ffn_wgrads(1, dgt, dup, act, nb, dh6)
    dh4, dk, dv, dg_xa1, g_xa1, recv = _xattn_bwd(
        h4, dh5, q1, row(sm["g_xattn"], 1), k1, v1, xa1, 1, tm_f, carry=("swap", [g_ffn1]))
    add("ffn1", g_ffn1, recv)
    dg_mem1, g_xa1 = _kv_proj_bwd(mem, row(sm["g_mem"], 1), dk, dv, xa1, g_xa1, 1)
    (dh3, g["od_w_s"], g["od_b_s"], g["od_c_ln_g"], g["od_c_ln_b"], dg_mix1, g_cod, g_odo, gots["ffn1"],
     recv) = _sgu_bwd(
        h3, dh4, zpre, row(sm["g_mix"], 1), sm["lng_c"], sm["lnb_c"], sm["ws"], sm["wst"], sm["bsb"], wod, odo, tm_b,
        carry=[("xchg", [parts["ffn1"]], []), ("swap", [g_xa1])])
    add("xa1", g_xa1, recv)
    dh2, dg_ffn0, dgt, dup, act, nb, gots["xa1"], recv_cod, recv_odo = _ffn_bwd_dx(
        h2, dh3, gt0, up0, row(sm["g_ffn"], 0), wg0, wu0, dn0, 0, tm_b,
        carry=[("xchg", [parts["xa1"]], []), ("swap", [g_cod, g_odo])])
    add("cod", g_cod, recv_cod)
    add("odo", g_odo, recv_odo)
    g_ffn0, (gots["cod"], gots["odo"]) = ffn_wgrads(0, dgt, dup, act, nb, dh3,
                                                    carry=("xchg", [parts["cod"], parts["odo"]], []))
    dh1, dk, dv, dg_xa0, g_xa0, recv = _xattn_bwd(
        h1, dh2, q0, row(sm["g_xattn"], 0), k0, v0, xa0, 0, tm_f, carry=("swap", [g_ffn0]))
    add("ffn0", g_ffn0, recv)
    dg_mem0, g_xa0 = _kv_proj_bwd(mem, row(sm["g_mem"], 0), dk, dv, xa0, g_xa0, 0)
    da2, dc2, dgb, g["ev_a_ln_g"], g["ev_a_ln_b"], g_evo, recv = _ev_mix_bwd(
        dh1, cat, a2, c2, z, sm["lng_a"], sm["lnb_a"], evo, tm_f, carry=("swap", [g_xa0]))
    add("xa0", g_xa0, recv)
    handle, token = _comm_start([parts["ffn0"], parts["xa0"]], [], [g_evo], "ev")
    dx, g["ev_a_conv_w"], g["ev_a_conv_b"], g["ev_b_conv_w"], g["ev_b_conv_b"], dg_mix0, g_cev = _ev_in_bwd(
        x, dh1, z, da2, dc2, dgb, row(sm["g_mix"], 0), sm["wa"], sm["wb"], wev, tm_b, token)
    (parts["ffn0"], parts["xa0"], g_evo), (gots["ffn0"], gots["xa0"], recv) = _comm_wait(handle, g_cev)
    add("evo", g_evo, recv)
    g["g_mix"] = jnp.concatenate([dg_mix0, dg_mix1], axis=0)
    g["g_xattn"] = jnp.concatenate([dg_xa0, dg_xa1], axis=0)
    g["g_mem"] = jnp.concatenate([dg_mem0, dg_mem1], axis=0)
    g["g_ffn"] = jnp.concatenate([dg_ffn0, dg_ffn1], axis=0)
    ga, gb = _small_grad_packs(g)
    last = (g_cev, ga[None], gb[None])
    for name, piece, recv in zip(("cev", "ga", "gb"), last, _sibling_swap(last, "rs_swap_last")):
        add(name, piece, recv)
    return loss, dx, parts, gots


def kernel(x, mem, g_mix, g_xattn, g_mem, g_ffn, g_final, ev_w_in, ev_a_conv_w, ev_a_conv_b, ev_a_ln_g, ev_a_ln_b, ev_b_conv_w, ev_b_conv_b, ev_w_out, od_w_in, od_c_ln_g, od_c_ln_b, od_w_s, od_b_s, od_w_out, xa_w_q, xa_w_k, xa_w_v, xa_w_o, ffn_w_gate, ffn_w_up, ffn_w_down, loss_target, m_g_mix, m_g_xattn, m_g_mem, m_g_ffn, m_g_final, m_ev_w_in, m_ev_a_conv_w, m_ev_a_conv_b, m_ev_a_ln_g, m_ev_a_ln_b, m_ev_b_conv_w, m_ev_b_conv_b, m_ev_w_out, m_od_w_in, m_od_c_ln_g, m_od_c_ln_b, m_od_w_s, m_od_b_s, m_od_w_out, m_xa_w_q, m_xa_w_k, m_xa_w_v, m_xa_w_o, m_ffn_w_gate, m_ffn_w_up, m_ffn_w_down, v_g_mix, v_g_xattn, v_g_mem, v_g_ffn, v_g_final, v_ev_w_in, v_ev_a_conv_w, v_ev_a_conv_b, v_ev_a_ln_g, v_ev_a_ln_b, v_ev_b_conv_w, v_ev_b_conv_b, v_ev_w_out, v_od_w_in, v_od_c_ln_g, v_od_c_ln_b, v_od_w_s, v_od_b_s, v_od_w_out, v_xa_w_q, v_xa_w_k, v_xa_w_v, v_xa_w_o, v_ffn_w_gate, v_ffn_w_up, v_ffn_w_down):
    W = dict(g_mix=g_mix, g_xattn=g_xattn, g_mem=g_mem, g_ffn=g_ffn, g_final=g_final, ev_w_in=ev_w_in,
             ev_a_conv_w=ev_a_conv_w, ev_a_conv_b=ev_a_conv_b, ev_a_ln_g=ev_a_ln_g, ev_a_ln_b=ev_a_ln_b,
             ev_b_conv_w=ev_b_conv_w, ev_b_conv_b=ev_b_conv_b, ev_w_out=ev_w_out, od_w_in=od_w_in,
             od_c_ln_g=od_c_ln_g, od_c_ln_b=od_c_ln_b, od_w_s=od_w_s, od_b_s=od_b_s, od_w_out=od_w_out,
             xa_w_q=xa_w_q, xa_w_k=xa_w_k, xa_w_v=xa_w_v, xa_w_o=xa_w_o, ffn_w_gate=ffn_w_gate,
             ffn_w_up=ffn_w_up, ffn_w_down=ffn_w_down)
    M = dict(g_mix=m_g_mix, g_xattn=m_g_xattn, g_mem=m_g_mem, g_ffn=m_g_ffn, g_final=m_g_final, ev_w_in=m_ev_w_in,
             ev_a_conv_w=m_ev_a_conv_w, ev_a_conv_b=m_ev_a_conv_b, ev_a_ln_g=m_ev_a_ln_g, ev_a_ln_b=m_ev_a_ln_b,
             ev_b_conv_w=m_ev_b_conv_w, ev_b_conv_b=m_ev_b_conv_b, ev_w_out=m_ev_w_out, od_w_in=m_od_w_in,
             od_c_ln_g=m_od_c_ln_g, od_c_ln_b=m_od_c_ln_b, od_w_s=m_od_w_s, od_b_s=m_od_b_s, od_w_out=m_od_w_out,
             xa_w_q=m_xa_w_q, xa_w_k=m_xa_w_k, xa_w_v=m_xa_w_v, xa_w_o=m_xa_w_o, ffn_w_gate=m_ffn_w_gate,
             ffn_w_up=m_ffn_w_up, ffn_w_down=m_ffn_w_down)
    V = dict(g_mix=v_g_mix, g_xattn=v_g_xattn, g_mem=v_g_mem, g_ffn=v_g_ffn, g_final=v_g_final, ev_w_in=v_ev_w_in,
             ev_a_conv_w=v_ev_a_conv_w, ev_a_conv_b=v_ev_a_conv_b, ev_a_ln_g=v_ev_a_ln_g, ev_a_ln_b=v_ev_a_ln_b,
             ev_b_conv_w=v_ev_b_conv_w, ev_b_conv_b=v_ev_b_conv_b, ev_w_out=v_ev_w_out, od_w_in=v_od_w_in,
             od_c_ln_g=v_od_c_ln_g, od_c_ln_b=v_od_c_ln_b, od_w_s=v_od_w_s, od_b_s=v_od_b_s, od_w_out=v_od_w_out,
             xa_w_q=v_xa_w_q, xa_w_k=v_xa_w_k, xa_w_v=v_xa_w_v, xa_w_o=v_xa_w_o, ffn_w_gate=v_ffn_w_gate,
             ffn_w_up=v_ffn_w_up, ffn_w_down=v_ffn_w_down)
    order = list(W)
    cx, cy, cc = _pos()
    chip = (2 * cx + cy).astype(jnp.int32)
    cc = cc.astype(jnp.int32)
    chip_idx = jnp.reshape(chip, (1,))
    c_idx = jnp.reshape(cc, (1,))

    tr = lambda a: jnp.swapaxes(a, 1, 2)
    P = {}
    cev, P["cod"], gs = _pack_cols(chip_idx, od_w_in, ev_w_in, ev_a_conv_w, ev_b_conv_w, od_c_ln_g, od_c_ln_b)
    ((P["evo"], P["odo"], P["dn0"], P["dn1"], P["wg0"], P["wg1"], P["wu0"], P["wu1"], P["xa0"], P["xa1"]),
     (P["cev"], P["gs"])) = _pack_rows(chip_idx, ffn_w_down, tr(ffn_w_gate), tr(ffn_w_up), xa_w_q, xa_w_o,
                                       xa_w_k, xa_w_v, ev_w_out, od_w_out, first=[cev, gs])

    loss, dx, parts, gots = _step(x[0], mem[0], loss_target[0], W, P, c_idx, TM_FWD, TM_BWD, TM_WGRAD)
    loss = lax.psum(loss[0, 0], ("x", "y", "c"))

    chip_c = jnp.stack([chip, cc])
    zero_c = jnp.stack([jnp.zeros((), jnp.int32), cc])
    def sums(names):
        return [_sum_chips(parts[n], gots[n], zero_c if n in ("ga", "gb") else chip_c, f"rs_sum_{n}")
                for n in names]

    last = ("evo", "cev", "ga", "gb")
    handle, token = _comm_start([parts["evo"], parts["cev"]], [parts["ga"][0], parts["gb"][0]], [], "last")
    big_names, rest_names = ["ffn1", "ffn0"], ["xa1", "xa0", "cod", "odo"]
    join_big, token = _join_start(sums(big_names), "ffn", after=(token,))
    join_rest, token = _join_start(sums(rest_names), "rest", after=(token,))
    red = dict(zip(big_names, _join_wait(join_big, token)))

    res = {}

    def adamw(name, srcs, gblock, gidx, rb):
        t = tr if name in ("ffn_w_gate", "ffn_w_up") else (lambda a: a)
        outs = _adamw_big(t(W[name]), t(M[name]), t(V[name]), srcs, gblock, gidx, f"adamw_{name}", rb)
        res[name] = tuple(t(o) for o in outs)

    ffn = [red["ffn0"], red["ffn1"]]
    adamw("ffn_w_gate", ffn, (FSH, D), lambda i: (0, 0), FSH)
    adamw("ffn_w_up", ffn, (FSH, D), lambda i: (1, 0), FSH)
    adamw("ffn_w_down", ffn, (FSH, D), lambda i: (2, 0), FSH)
    red.update(zip(rest_names, _join_wait(join_rest, res["ffn_w_down"][1])))
    xa = [red["xa0"], red["xa1"]]
    adamw("xa_w_q", xa, (256, D), lambda i: (XQ, 0), 256)
    adamw("xa_w_o", xa, (256, D), lambda i: (XO, 0), 256)
    adamw("xa_w_k", xa, (256, D), lambda i: (XK, 0), 256)
    adamw("xa_w_v", xa, (256, D), lambda i: (XV, 0), 256)
    adamw("od_w_out", [red["odo"]], (256, D), lambda i: (0, 0), 256)
    adamw("od_w_in", [red["cod"]], (256, 512), lambda i: (i, 0), 256)

    srcs_done, got_last = _comm_wait(handle, res["od_w_in"][1])
    parts.update(zip(last, srcs_done[:2] + [a[None] for a in srcs_done[2:]]))
    gots.update(zip(last, got_last))
    red.update(zip(last, _sibling_join(sums(last), "rs_join_last")))
    adamw("ev_w_out", [red["evo"]], (256, D), lambda i: (0, 0), 256)
    adamw("ev_w_in", [red["cev"]], (256, 640), lambda i: (i, 0), 256)
    shape2 = {name: shape for name, shape, *_ in _SMALL}
    wmv = [tuple(d[name].reshape(shape2[name]) for d in (W, M, V)) for name, *_ in _SMALL]
    for (name, *_), outs in zip(_SMALL, _adamw_small(red["ga"], red["gb"], wmv)):
        res[name] = tuple(o.reshape(W[name].shape) for o in outs)

    grad_x = dx[None]
    return (loss, grad_x, *[res[n][0] for n in order], *[res[n][1] for n in order],
            *[res[n][2] for n in order], *[res[n][3] for n in order])
```

```python
import jax
import jax.numpy as jnp
from jax import lax
from jax.experimental import pallas as pl
from jax.experimental.pallas import tpu as pltpu

BF = jnp.bfloat16
F32 = jnp.float32

D = 1024
DA = 512
DZ = 2560
DFF = 2816
NSH = 4
FSH = DFF // NSH
FF_CHUNKS = ((0, 768), (768, 1536), (1536, 2304), (2304, 2816))
NMEM = 256
HEADS = 4
HD = D // HEADS
CHUNK = 128
GROUPS = 8
KA = 31
KB = 3
HALO = 16
RC = 32
RMS_EPS = 1e-6
LN_EPS = 1e-5
ATT_SCALE = HD ** -0.5
TM_FWD = 512
TM_BWD = 256
TM_WGRAD = 1024

ADAM_LR = 0.001
ADAM_B1 = 0.9
ADAM_B2 = 0.999
ADAM_EPS = 1e-08
ADAM_WD = 0.01
ADAM_STEP = 10

XQ, XO, XK, XV = 0, 1, 2, 3
SP_ROWS = 48
SA_ROWS = 64

MESH = pl.DeviceIdType.MESH
ANY = pl.BlockSpec(memory_space=pl.ANY)


def _dot(a, b):
    return jnp.dot(a, b, preferred_element_type=F32)


def _dot_nt(a, b):
    return lax.dot_general(a, b, (((1,), (1,)), ((), ())), preferred_element_type=F32)


def _dot_tn(a, b):
    return lax.dot_general(a, b, (((0,), (0,)), ((), ())), preferred_element_type=F32)


def _rms(h):
    r = lax.rsqrt(jnp.mean(h * h, axis=-1, keepdims=True) + RMS_EPS)
    return h * r, r


def _rms_bwd(dn, hhat, r, g):
    dhh = dn * g
    return r * (dhh - hhat * jnp.mean(dhh * hhat, axis=-1, keepdims=True))


def _ln(x):
    mu = jnp.mean(x, axis=-1, keepdims=True)
    xc = x - mu
    rstd = lax.rsqrt(jnp.mean(xc * xc, axis=-1, keepdims=True) + LN_EPS)
    return xc * rstd, rstd


def _ln_bwd(dy, y, rstd):
    return rstd * (dy - jnp.mean(dy, axis=-1, keepdims=True) - y * jnp.mean(dy * y, axis=-1, keepdims=True))


def _colsum(x):
    return jnp.sum(x, axis=0, keepdims=True)


def _sigmoid(x):
    return 1.0 / (1.0 + jnp.exp(-x))


_GELU_C = 0.7978845608028654
_GELU_A = 0.044715


def _gelu(x):
    t = jnp.tanh(_GELU_C * (x + _GELU_A * x * x * x))
    return 0.5 * x * (1.0 + t)


def _gelu_grad(x):
    t = jnp.tanh(_GELU_C * (x + _GELU_A * x * x * x))
    return 0.5 * (1.0 + t) + 0.5 * x * (1.0 - t * t) * _GELU_C * (1.0 + 3.0 * _GELU_A * x * x)


def _first(i):
    return i == 0


def _last(i):
    return i == pl.num_programs(0) - 1


def _const(shape, idx):
    return pl.BlockSpec(shape, lambda *_, _idx=tuple(idx): _idx, pipeline_mode=pl.Buffered(1))


def _whole(a):
    return _const(a.shape, (0,) * a.ndim)


def _rows(tm, width, col=0):
    return pl.BlockSpec((tm, width), lambda i, _c=col: (i, _c))


def _params(ndim=1, vmem_mb=56):
    return pltpu.CompilerParams(dimension_semantics=("arbitrary",) * ndim, vmem_limit_bytes=vmem_mb << 20)


def _sds(shape, dtype):
    return jax.ShapeDtypeStruct(shape, dtype)


def _pos():
    return lax.axis_index("x"), lax.axis_index("y"), lax.axis_index("c")


def _other_chips(x, y):
    return [(1 - x, y), (x, 1 - y), (1 - x, 1 - y)]


def _half(ref, hc, lead=()):
    r = ref.shape[-2] // 2
    return ref.at[(*lead, pl.ds(pl.multiple_of(hc * r, 8), r), slice(None))]


def _rcopy(src, dst, ssem, rsem, to):
    return pltpu.make_async_remote_copy(src_ref=src, dst_ref=dst, send_sem=ssem, recv_sem=rsem,
                                        device_id=to, device_id_type=MESH)


def _gather_start(bufs, sems):
    ici_s, ici_r, _, _ = sems
    x, y, c = _pos()
    me = 2 * x + y
    for k, (cx, cy) in enumerate(_other_chips(x, y)):
        for t, buf in enumerate(bufs):
            mine = _half(buf, c, (me,))
            _rcopy(mine, mine, ici_s.at[3 * t + k], ici_r.at[3 * t + k], (cx, cy, c)).start()


def _gather_forward(bufs, sems):
    ici_s, ici_r, d2d_s, d2d_r = sems
    x, y, c = _pos()
    sib = (x, y, 1 - c)
    for k, (cx, cy) in enumerate(_other_chips(x, y)):
        for t, buf in enumerate(bufs):
            landed = _half(buf, c, (2 * cx + cy,))
            _rcopy(landed, landed, ici_s.at[3 * t + k], ici_r.at[3 * t + k], (cx, cy, c)).wait_recv()
            _rcopy(landed, landed, d2d_s.at[3 * t + k], d2d_r.at[3 * t + k], sib).start()


def _gather_drain(bufs, sems):
    ici_s, ici_r, d2d_s, d2d_r = sems
    x, y, c = _pos()
    me = 2 * x + y
    sib = (x, y, 1 - c)
    chips = _other_chips(x, y)
    for k, (cx, cy) in enumerate(chips):
        for t, buf in enumerate(bufs):
            got = _half(buf, 1 - c, (2 * cx + cy,))
            _rcopy(got, got, d2d_s.at[3 * t + k], d2d_r.at[3 * t + k], sib).wait_recv()
    for k, (cx, cy) in enumerate(chips):
        for t, buf in enumerate(bufs):
            mine = _half(buf, c, (me,))
            _rcopy(mine, mine, ici_s.at[3 * t + k], ici_r.at[3 * t + k], (cx, cy, c)).wait_send()
            landed = _half(buf, c, (2 * cx + cy,))
            _rcopy(landed, landed, d2d_s.at[3 * t + k], d2d_r.at[3 * t + k], sib).wait_send()


def _xchg_copies(srcs, dsts, nbig, sems):
    ssem, rsem = sems
    x, y, c = _pos()
    cps = []
    for k, (cx, cy) in enumerate(_other_chips(x, y)):
        for t, (src, dst) in enumerate(zip(srcs, dsts)):
            s = src.at[2 * cx + cy] if t < nbig else src
            cps.append(_rcopy(s, dst.at[k], ssem.at[3 * t + k], rsem.at[3 * t + k], (cx, cy, c)))
    return cps


def _gather_sems(n):
    return [pltpu.SemaphoreType.DMA((3 * n,))] * 4


def _xchg_sems(n):
    return [pltpu.SemaphoreType.DMA((3 * n,))] * 2


def _xchg_out_shapes(items):
    return [_sds((3,) + a.shape[-2:], a.dtype) for a in items]


def _swap_copies(srcs, dsts, sems):
    ssem, rsem = sems
    x, y, c = _pos()
    cps = []
    for t, (src, dst) in enumerate(zip(srcs, dsts)):
        lead = (slice(None),) * (len(src.shape) - 2)
        cps.append(_rcopy(_half(src, 1 - c, lead), dst, ssem.at[t], rsem.at[t], (x, y, 1 - c)))
    return cps


def _swap_out_shapes(items):
    return [_sds(a.shape[:-2] + (a.shape[-2] // 2, a.shape[-1]), a.dtype) for a in items]


def _call(body, *, name, grid, in_specs, out_specs, out_shape, args, scratch_shapes=(), aliases=None, carry=None):
    in_specs, out_specs, out_shape = list(in_specs), list(out_specs), list(out_shape)
    args = list(args)
    aliases = dict(aliases or {})
    n_in, n_out, n_scr = len(args), len(out_shape), len(scratch_shapes)
    carries = [] if not carry else ([carry] if isinstance(carry, tuple) else list(carry))
    plans = []
    cin, cout_shape, sems = [], [], []
    for c in carries:
        kind = c[0]
        items = list(c[1]) + (list(c[2]) if kind == "xchg" else [])
        if kind == "gather":
            shapes, sm = [_sds(a.shape, a.dtype) for a in items], _gather_sems(len(items))
            for t in range(len(items)):
                aliases[n_in + len(cin) + t] = n_out + len(cin) + t
        elif kind == "xchg":
            shapes, sm = _xchg_out_shapes(items), _xchg_sems(len(items))
        else:
            shapes, sm = _swap_out_shapes(items), [pltpu.SemaphoreType.DMA((len(items),))] * 2
        plans.append((kind, len(cin), len(items), len(sems), len(sm), len(c[1])))
        cin += items
        cout_shape += shapes
        sems += sm
    nc = len(cin)

    def wrapped(*refs):
        ins = refs[:n_in]
        c_in = refs[n_in:n_in + nc]
        outs = refs[n_in + nc:n_in + nc + n_out]
        c_out = refs[n_in + nc + n_out:n_in + 2 * nc + n_out]
        scr = refs[n_in + 2 * nc + n_out:n_in + 2 * nc + n_out + n_scr]
        sem = refs[n_in + 2 * nc + n_out + n_scr:]
        i = pl.program_id(0)

        def copies(plan):
            kind, o0, no, s0, ns, nbig = plan
            if kind == "xchg":
                return _xchg_copies(c_in[o0:o0 + no], c_out[o0:o0 + no], nbig, sem[s0:s0 + ns])
            return _swap_copies(c_in[o0:o0 + no], c_out[o0:o0 + no], sem[s0:s0 + ns])

        if plans:
            @pl.when(_first(i))
            def _():
                for plan in plans:
                    kind, o0, no, s0, ns, _ = plan
                    if kind == "gather":
                        _gather_start(c_out[o0:o0 + no], sem[s0:s0 + ns])
                    else:
                        for cp in copies(plan):
                            cp.start()
        body(*ins, *outs, *scr)
        for plan in plans:
            kind, o0, no, s0, ns, _ = plan
            if kind == "gather":
                @pl.when(i == max(grid[0] - 3, 0))
                def _(o0=o0, no=no, s0=s0, ns=ns):
                    _gather_forward(c_out[o0:o0 + no], sem[s0:s0 + ns])
        if plans:
            @pl.when(_last(i))
            def _():
                for plan in plans:
                    kind, o0, no, s0, ns, _ = plan
                    if kind == "gather":
                        _gather_drain(c_out[o0:o0 + no], sem[s0:s0 + ns])
                    else:
                        for cp in copies(plan):
                            cp.wait()

    outs = pl.pallas_call(
        wrapped, name=name, grid=grid,
        in_specs=in_specs + [ANY] * nc, out_specs=out_specs + [ANY] * nc,
        out_shape=out_shape + cout_shape,
        scratch_shapes=list(scratch_shapes) + sems,
        input_output_aliases=aliases,
        compiler_params=_params(len(grid)),
    )(*args, *cin)
    return list(outs)


def _sibling_swap(items, name):
    n = len(items)

    def body(*refs):
        src = refs[:n]
        dst = refs[n:2 * n]
        ssem, rsem = refs[2 * n:]
        x, y, c = _pos()
        sib = (x, y, 1 - c)
        cps = []
        for t in range(n):
            lead = (slice(None),) * (len(src[t].shape) - 2)
            cp = _rcopy(_half(src[t], 1 - c, lead), dst[t], ssem.at[t], rsem.at[t], sib)
            cp.start()
            cps.append(cp)
        for cp in cps:
            cp.wait()

    return pl.pallas_call(
        body, name=name, in_specs=[ANY] * n, out_specs=[ANY] * n,
        out_shape=[_sds(a.shape[:-2] + (a.shape[-2] // 2, a.shape[-1]), a.dtype) for a in items],
        scratch_shapes=[pltpu.SemaphoreType.DMA((n,)), pltpu.SemaphoreType.DMA((n,))],
    )(*items)


_HBM = pl.BlockSpec(memory_space=pltpu.HBM)
_SEM = pl.BlockSpec(memory_space=pltpu.SEMAPHORE)
_EFFECT = pltpu.SideEffectType.DATAFLOW_SIDE_EFFECTING


def _split_copies(src, land, nbig, nx, sems):
    return (_xchg_copies(src[:nx], land[:nx], nbig, sems[0:2]) if nx else []) + \
        (_swap_copies(src[nx:], land[nx:], sems[2:4]) if len(src) > nx else [])


def _comm_start(big, small, swaps, tag):
    xs = list(big) + list(small)
    items = xs + list(swaps)
    n, nx = len(items), len(xs)

    def body(*refs):
        for cp in _split_copies(refs[:n], refs[n:2 * n], len(big), nx, refs[2 * n:2 * n + 4]):
            cp.start()
        refs[-1][...] = jnp.zeros_like(refs[-1])

    lands = _xchg_out_shapes(xs) + _swap_out_shapes(swaps)
    sem_shapes = [pltpu.SemaphoreType.DMA((max(3 * nx, 1),))] * 2 + [pltpu.SemaphoreType.DMA((max(n - nx, 1),))] * 2
    hbm = lambda a: pltpu.with_memory_space_constraint(a, pltpu.HBM)
    outs = pl.pallas_call(
        body, name=f"rs_start_{tag}",
        out_shape=sem_shapes + [pltpu.HBM(a.shape, a.dtype) for a in items]
        + [pltpu.HBM(a.shape, a.dtype) for a in lands] + [_sds((8, 128), F32)],
        in_specs=[_HBM] * (2 * n), out_specs=[_SEM] * 4 + [_HBM] * (2 * n) + [pl.BlockSpec(memory_space=pltpu.VMEM)],
        input_output_aliases={t: 4 + t for t in range(2 * n)},
        compiler_params=pltpu.CompilerParams(has_side_effects=_EFFECT),
    )(*[hbm(a) for a in items], *[hbm(lax.empty(a.shape, a.dtype)) for a in lands])
    handle = (list(outs[:4]), list(outs[4:4 + n]), list(outs[4 + n:4 + 2 * n]), len(big), nx, tag)
    return handle, outs[-1]


def _comm_wait(handle, after):
    sems, srcs, lands, nbig, nx, tag = handle
    n = len(srcs)

    def body(*refs):
        for cp in _split_copies(refs[:n], refs[n:2 * n], nbig, nx, refs[2 * n:2 * n + 4]):
            cp.wait_send()
            cp.wait_recv()

    outs = pl.pallas_call(
        body, name=f"rs_wait_{tag}",
        out_shape=[pltpu.HBM(a.shape, a.dtype) for a in srcs] + [pltpu.HBM(a.shape, a.dtype) for a in lands],
        in_specs=[_HBM] * (2 * n) + [_SEM] * 4 + [ANY], out_specs=[_HBM] * (2 * n),
        input_output_aliases={t: t for t in range(2 * n)},
        compiler_params=pltpu.CompilerParams(has_side_effects=_EFFECT),
    )(*srcs, *lands, *sems, after)
    return list(outs[:n]), list(outs[n:])


def _join_copies(items, sems):
    ssem, rsem = sems
    x, y, c = _pos()
    return [_rcopy(_half(a, c), _half(a, c), ssem.at[t], rsem.at[t], (x, y, 1 - c)) for t, a in enumerate(items)]


def _join_start(items, tag, after=()):
    n = len(items)

    def body(*refs):
        for cp in _join_copies(refs[:n], refs[n + len(after):n + len(after) + 2]):
            cp.start()
        refs[-1][...] = jnp.zeros_like(refs[-1])

    hbm = lambda a: pltpu.with_memory_space_constraint(a, pltpu.HBM)
    outs = pl.pallas_call(
        body, name=f"rs_join_start_{tag}",
        out_shape=[pltpu.SemaphoreType.DMA((n,))] * 2 + [pltpu.HBM(a.shape, a.dtype) for a in items]
        + [_sds((8, 128), F32)],
        in_specs=[_HBM] * n + [ANY] * len(after),
        out_specs=[_SEM] * 2 + [_HBM] * n + [pl.BlockSpec(memory_space=pltpu.VMEM)],
        input_output_aliases={t: 2 + t for t in range(n)},
        compiler_params=pltpu.CompilerParams(has_side_effects=_EFFECT),
    )(*[hbm(a) for a in items], *after)
    return (list(outs[:2]), list(outs[2:2 + n]), tag), outs[-1]


def _join_wait(handle, after):
    sems, items, tag = handle
    n = len(items)

    def body(*refs):
        x, y, c = _pos()
        ssem, rsem = refs[n:n + 2]
        for t, cp in enumerate(_join_copies(refs[:n], (ssem, rsem))):
            cp.wait_send()
            theirs = _half(refs[t], 1 - c)
            _rcopy(theirs, theirs, ssem.at[t], rsem.at[t], (x, y, 1 - c)).wait_recv()

    outs = pl.pallas_call(
        body, name=f"rs_join_wait_{tag}",
        out_shape=[pltpu.HBM(a.shape, a.dtype) for a in items],
        in_specs=[_HBM] * n + [_SEM] * 2 + [ANY], out_specs=[_HBM] * n,
        input_output_aliases={t: t for t in range(n)},
        compiler_params=pltpu.CompilerParams(has_side_effects=_EFFECT),
    )(*items, *sems, after)
    return list(outs)


def _sibling_join(items, name, after=()):
    n = len(items)

    def body(*refs):
        dst = refs[n + len(after):2 * n + len(after)]
        ssem, rsem = refs[2 * n + len(after):]
        x, y, c = _pos()
        sib = (x, y, 1 - c)
        cps = []
        for t in range(n):
            mine = _half(dst[t], c)
            cp = _rcopy(mine, mine, ssem.at[t], rsem.at[t], sib)
            cp.start()
            cps.append(cp)
        for t, cp in enumerate(cps):
            theirs = _half(dst[t], 1 - c)
            _rcopy(theirs, theirs, ssem.at[t], rsem.at[t], sib).wait_recv()
            cp.wait_send()

    outs = pl.pallas_call(
        body, name=name, in_specs=[ANY] * (n + len(after)), out_specs=[ANY] * n,
        out_shape=[_sds(a.shape, a.dtype) for a in items],
        input_output_aliases={t: t for t in range(n)},
        scratch_shapes=[pltpu.SemaphoreType.DMA((n,))] * 2,
    )(*items, *after)
    return list(outs)


def _ev_in_fwd(h, g, wev, tm, carry=None):
    s = h.shape[0]

    def body(h_ref, g_ref, w_ref, z_ref):
        hhat, _ = _rms(h_ref[...])
        n = (hhat * g_ref[...]).astype(BF)
        z_ref[...] = _dot(n, w_ref[...]).astype(BF)

    return _call(
        body, name="ev_in_fwd", grid=(s // tm,),
        in_specs=[_rows(tm, D), _const((1, D), (0, 0)), _const((D, DZ), (0, 0))],
        out_specs=[_rows(tm, DZ)], out_shape=[_sds((s, DZ), BF)],
        args=(h, g, wev), carry=carry)


def _halo_specs(tm, width, s, col=0):
    per = tm // HALO
    nh = s // HALO
    prev = pl.BlockSpec((HALO, width), lambda i, _c=col: (jnp.maximum(i * per - 1, 0), _c))
    nxt = pl.BlockSpec((HALO, width), lambda i, _c=col: (jnp.minimum((i + 1) * per, nh - 1), _c))
    return prev, nxt


def _fill_pad(pad_ref, prev, main, nxt, i, tm):
    pad_ref[pl.ds(0, HALO), :] = jnp.where(_first(i), 0.0, prev)
    pad_ref[pl.ds(HALO, tm), :] = main
    pad_ref[pl.ds(HALO + tm, HALO), :] = jnp.where(_last(i), 0.0, nxt)


def _shift8(xs_ref, pad_ref, tm):
    for j in range(8):
        xs_ref[j] = pad_ref[pl.ds(j, tm + 2 * HALO - 8), :]


def _tap(xs_ref, r, off):
    return xs_ref[off % 8, pl.ds(r + 8 * (off // 8), RC), :]


def _glu_a(z):
    return z[:, :DA] * _sigmoid(z[:, DA:2 * DA])


def _gate_c(z):
    return z[:, 2048:] * z[:, 1024:1536]


def _ev_mix_fwd(h, z, wa, ba, lng, lnb, wb, bb, evo, tm, carry=None):
    s = h.shape[0]

    def body(h_ref, z_ref, zp_ref, zn_ref, wa_ref, ba_ref, lng_ref, lnb_ref, wb_ref, bb_ref, wo_ref,
             h1_ref, a2_ref, c2_ref, cat_ref, apad, cpad, c2f, xs):
        i = pl.program_id(0)
        zm = z_ref[...].astype(F32)
        zp = zp_ref[...].astype(F32)
        zn = zn_ref[...].astype(F32)
        _fill_pad(apad, _glu_a(zp), _glu_a(zm), _glu_a(zn), i, tm)
        _fill_pad(cpad, _gate_c(zp), _gate_c(zm), _gate_c(zn), i, tm)
        _shift8(xs, apad, tm)
        for r in range(0, tm, RC):
            acc = jnp.zeros((RC, DA), F32) + ba_ref[...]
            for k in range(KA):
                acc = acc + _tap(xs, r, k + 1) * wa_ref[pl.ds(k, 1), :]
            a2_ref[pl.ds(r, RC), :] = acc
            acc = jnp.zeros((RC, DA), F32) + bb_ref[...]
            for k in range(KB):
                acc = acc + cpad[pl.ds(r + HALO - 1 + k, RC), :] * wb_ref[pl.ds(k, 1), :]
            c2f[pl.ds(r, RC), :] = acc
        y, _ = _ln(a2_ref[...])
        a3 = y * lng_ref[...] + lnb_ref[...]
        cat_ref[:, :DA] = (a3 * _sigmoid(a3)).astype(BF)
        c2 = c2f[...]
        c2_ref[...] = c2.astype(BF)
        cat_ref[:, DA:] = (zm[:, 1536:2048] * c2).astype(BF)
        h1_ref[...] = h_ref[...] + _dot(cat_ref[...], wo_ref[...].reshape(D, D))

    zp_spec, zn_spec = _halo_specs(tm, DZ, s)
    small = lambda rows: _const((rows, DA), (0, 0))
    return _call(
        body, name="ev_mix_fwd", grid=(s // tm,),
        in_specs=[_rows(tm, D), _rows(tm, DZ), zp_spec, zn_spec, small(32), small(1), small(1), small(1),
                  small(8), small(1), _whole(evo)],
        out_specs=[_rows(tm, D), _rows(tm, DA), _rows(tm, DA), _rows(tm, D)],
        out_shape=[_sds((s, D), F32), _sds((s, DA), F32), _sds((s, DA), BF), _sds((s, D), BF)],
        scratch_shapes=[pltpu.VMEM((tm + 2 * HALO, DA), F32), pltpu.VMEM((tm + 2 * HALO, DA), F32),
                        pltpu.VMEM((tm, DA), F32), pltpu.VMEM((8, tm + 2 * HALO - 8, DA), F32)],
        args=(h, z, z, z, wa, ba, lng, lnb, wb, bb, evo), carry=carry)


def _ev_mix_bwd(dh, cat, a2, c2, z, lng, lnb, evo, tm, carry=None):
    s = dh.shape[0]

    def body(dh_ref, cat_ref, a2_ref, c2_ref, zgb_ref, lng_ref, lnb_ref, wo_ref,
             da2_ref, dc2_ref, dgb_ref, dlng_ref, dlnb_ref, gwo_ref, wacc):
        i = pl.program_id(0)

        @pl.when(_first(i))
        def _():
            wacc[...] = jnp.zeros_like(wacc)
            dlng_ref[...] = jnp.zeros_like(dlng_ref)
            dlnb_ref[...] = jnp.zeros_like(dlnb_ref)

        dhb = dh_ref[...].astype(BF)
        dcat = _dot_nt(dhb, wo_ref[...].reshape(D, D))
        wacc[...] += _dot_tn(cat_ref[...], dhb)
        y, rstd = _ln(a2_ref[...])
        a3 = y * lng_ref[...] + lnb_ref[...]
        sg = _sigmoid(a3)
        da3 = dcat[:, :DA] * (sg * (1.0 + a3 * (1.0 - sg)))
        dlng_ref[...] += _colsum(da3 * y)
        dlnb_ref[...] += _colsum(da3)
        da2_ref[...] = _ln_bwd(da3 * lng_ref[...], y, rstd).astype(BF)
        db = dcat[:, DA:]
        dc2_ref[...] = (db * zgb_ref[...].astype(F32)).astype(BF)
        dgb_ref[...] = (db * c2_ref[...].astype(F32)).astype(BF)

        @pl.when(_last(i))
        def _():
            gwo_ref[...] = wacc[...].astype(BF).reshape(NSH, 256, D)

    small = _const((1, DA), (0, 0))
    return _call(
        body, name="ev_mix_bwd", grid=(s // tm,),
        in_specs=[_rows(tm, D), _rows(tm, D), _rows(tm, DA), _rows(tm, DA), _rows(tm, DA, 3), small, small,
                  _whole(evo)],
        out_specs=[_rows(tm, DA), _rows(tm, DA), _rows(tm, DA), small, small, _whole(evo)],
        out_shape=[_sds((s, DA), BF), _sds((s, DA), BF), _sds((s, DA), BF), _sds((1, DA), F32),
                   _sds((1, DA), F32), _sds(evo.shape, BF)],
        scratch_shapes=[pltpu.VMEM((D, D), F32)],
        args=(dh, cat, a2, c2, z, lng, lnb, evo), carry=carry)


def _ev_in_bwd(h, dh, z, da2, dc2, dgb, g, wa, wb, wev, tm, after):
    s = h.shape[0]

    def body(h_ref, dh_ref, z_ref, zp_ref, zn_ref, da_ref, dap_ref, dan_ref, dc_ref, dcp_ref, dcn_ref,
             dgb_ref, g_ref, wa_ref, wb_ref, w_ref, after_ref,
             dh0_ref, dwa_ref, dba_ref, dwb_ref, dbb_ref, dg_ref, gw_ref,
             apad, cpad, dapad, dcpad, dz, wacc, dwa_acc, dwb_acc, xs):
        i = pl.program_id(0)

        @pl.when(_first(i))
        def _():
            wacc[...] = jnp.zeros_like(wacc)
            dwa_acc[...] = jnp.zeros_like(dwa_acc)
            dwb_acc[...] = jnp.zeros_like(dwb_acc)
            dba_ref[...] = jnp.zeros_like(dba_ref)
            dbb_ref[...] = jnp.zeros_like(dbb_ref)
            dg_ref[...] = jnp.zeros_like(dg_ref)

        zm = z_ref[...].astype(F32)
        zp = zp_ref[...].astype(F32)
        zn = zn_ref[...].astype(F32)
        _fill_pad(apad, _glu_a(zp), _glu_a(zm), _glu_a(zn), i, tm)
        _fill_pad(cpad, _gate_c(zp), _gate_c(zm), _gate_c(zn), i, tm)
        da2 = da_ref[...].astype(F32)
        dc2 = dc_ref[...].astype(F32)
        _fill_pad(dapad, dap_ref[...].astype(F32), da2, dan_ref[...].astype(F32), i, tm)
        _fill_pad(dcpad, dcp_ref[...].astype(F32), dc2, dcn_ref[...].astype(F32), i, tm)
        dba_ref[...] += _colsum(da2)
        dbb_ref[...] += _colsum(dc2)

        sg = _sigmoid(zm[:, DA:2 * DA])
        aval = zm[:, :DA]
        _shift8(xs, dapad, tm)
        for r in range(0, tm, RC):
            acc = jnp.zeros((RC, DA), F32)
            for k in range(KA):
                acc = acc + _tap(xs, r, 2 * HALO - 1 - k) * wa_ref[pl.ds(k, 1), :]
            sgr = sg[r:r + RC]
            dz[pl.ds(r, RC), 0:DA] = (acc * sgr).astype(BF)
            dz[pl.ds(r, RC), DA:2 * DA] = (acc * aval[r:r + RC] * sgr * (1.0 - sgr)).astype(BF)
            acc = jnp.zeros((RC, DA), F32)
            for k in range(KB):
                acc = acc + dcpad[pl.ds(r + HALO + 1 - k, RC), :] * wb_ref[pl.ds(k, 1), :]
            dz[pl.ds(r, RC), 1024:1536] = (acc * zm[r:r + RC, 2048:]).astype(BF)
            dz[pl.ds(r, RC), 2048:2560] = (acc * zm[r:r + RC, 1024:1536]).astype(BF)
        dz[:, 1536:2048] = dgb_ref[...]

        _shift8(xs, apad, tm)
        for k in range(KA):
            acc = jnp.zeros((8, DA), F32)
            for r in range(0, tm, RC):
                p = _tap(xs, r, k + 1) * dapad[pl.ds(r + HALO, RC), :]
                acc = acc + p.reshape(RC // 8, 8, DA).sum(axis=0)
            dwa_acc[k] += acc
        for k in range(KB):
            acc = jnp.zeros((8, DA), F32)
            for r in range(0, tm, RC):
                p = cpad[pl.ds(r + HALO - 1 + k, RC), :] * dcpad[pl.ds(r + HALO, RC), :]
                acc = acc + p.reshape(RC // 8, 8, DA).sum(axis=0)
            dwb_acc[k] += acc

        gg = g_ref[...]
        hhat, rr = _rms(h_ref[...])
        n = (hhat * gg).astype(BF)
        dzb = dz[...]
        wacc[...] += _dot_tn(n, dzb)
        dn = _dot_nt(dzb, w_ref[...])
        dg_ref[...] += _colsum(dn * hhat)
        dh0_ref[...] = dh_ref[...] + _rms_bwd(dn, hhat, rr, gg)

        @pl.when(_last(i))
        def _():
            for k in range(NSH):
                gw_ref[k] = wacc[:, 640 * k:640 * (k + 1)].astype(BF)
            dwa_ref[...] = jnp.sum(dwa_acc[...], axis=1)
            dwb_ref[...] = jnp.sum(dwb_acc[...], axis=1)

    zp_spec, zn_spec = _halo_specs(tm, DZ, s)
    hp_spec, hn_spec = _halo_specs(tm, DA, s)
    small = lambda rows: _const((rows, DA), (0, 0))
    return _call(
        body, name="ev_in_bwd", grid=(s // tm,),
        in_specs=[_rows(tm, D), _rows(tm, D), _rows(tm, DZ), zp_spec, zn_spec,
                  _rows(tm, DA), hp_spec, hn_spec, _rows(tm, DA), hp_spec, hn_spec, _rows(tm, DA),
                  _const((1, D), (0, 0)), small(32), small(8), _const((D, DZ), (0, 0)), ANY],
        out_specs=[_rows(tm, D), small(32), small(1), small(8), small(1), _const((1, D), (0, 0)),
                   _const((NSH, D, 640), (0, 0, 0))],
        out_shape=[_sds((s, D), F32), _sds((32, DA), F32), _sds((1, DA), F32), _sds((8, DA), F32),
                   _sds((1, DA), F32), _sds((1, D), F32), _sds((NSH, D, 640), BF)],
        scratch_shapes=[pltpu.VMEM((tm + 2 * HALO, DA), F32)] * 4 + [
            pltpu.VMEM((tm, DZ), BF), pltpu.VMEM((D, DZ), F32),
            pltpu.VMEM((32, 8, DA), F32), pltpu.VMEM((8, 8, DA), F32),
            pltpu.VMEM((8, tm + 2 * HALO - 8, DA), F32)],
        args=(h, dh, z, z, z, da2, da2, da2, dc2, dc2, dc2, dgb, g, wa, wb, wev, after))


def _xa_spec(item):
    return _const((NSH, 256, D), (0, item, 0))


def _kv_proj(mem, g, xa, layer):
    def body(mem_ref, g_ref, wk_ref, wv_ref, k_ref, v_ref):
        mhat, _ = _rms(mem_ref[...])
        mn = (mhat * g_ref[...]).astype(BF)
        k_ref[...] = _dot(mn, wk_ref[...].reshape(D, D)).astype(BF)
        v_ref[...] = _dot(mn, wv_ref[...].reshape(D, D)).astype(BF)

    full = _const((NMEM, D), (0, 0))
    return _call(
        body, name=f"kv_proj{layer}", grid=(1,),
        in_specs=[full, _const((1, D), (0, 0)), _xa_spec(XK), _xa_spec(XV)],
        out_specs=[full, full], out_shape=[_sds((NMEM, D), BF), _sds((NMEM, D), BF)],
        args=(mem, g, xa, xa))


def _softmax_rows(sc):
    m = jnp.max(sc, axis=-1, keepdims=True)
    e = jnp.exp(sc - m)
    return e / jnp.sum(e, axis=-1, keepdims=True)


def _xattn_fwd(h, g, k, v, xa, layer, tm, carry=None):
    s = h.shape[0]

    def body(h_ref, g_ref, k_ref, v_ref, wq_ref, wo_ref, ho_ref, q_ref, o_scr):
        hh = h_ref[...]
        hhat, _ = _rms(hh)
        n = (hhat * g_ref[...]).astype(BF)
        q = _dot(n, wq_ref[...].reshape(D, D)).astype(BF)
        q_ref[...] = q
        for hd in range(HEADS):
            cs = slice(HD * hd, HD * (hd + 1))
            p = _softmax_rows(_dot_nt(q[:, cs], k_ref[:, cs]) * ATT_SCALE)
            o_scr[:, cs] = _dot(p.astype(BF), v_ref[:, cs]).astype(BF)
        ho_ref[...] = hh + _dot(o_scr[...], wo_ref[...].reshape(D, D))

    kvs = _const((NMEM, D), (0, 0))
    return _call(
        body, name=f"xattn_fwd{layer}", grid=(s // tm,),
        in_specs=[_rows(tm, D), _const((1, D), (0, 0)), kvs, kvs, _xa_spec(XQ), _xa_spec(XO)],
        out_specs=[_rows(tm, D), _rows(tm, D)],
        out_shape=[_sds((s, D), F32), _sds((s, D), BF)],
        scratch_shapes=[pltpu.VMEM((tm, D), BF)],
        args=(h, g, k, v, xa, xa), carry=carry)


def _xattn_bwd(h, dh, q, g, k, v, xa, layer, tm, carry=None):
    s = h.shape[0]

    def body(h_ref, dh_ref, q_ref, g_ref, k_ref, v_ref, wq_ref, wo_ref,
             dhi_ref, dk_ref, dv_ref, dg_ref, gqo_ref, o_scr, dq_scr, qacc, oacc):
        i = pl.program_id(0)

        @pl.when(_first(i))
        def _():
            qacc[...] = jnp.zeros_like(qacc)
            oacc[...] = jnp.zeros_like(oacc)
            dk_ref[...] = jnp.zeros_like(dk_ref)
            dv_ref[...] = jnp.zeros_like(dv_ref)
            dg_ref[...] = jnp.zeros_like(dg_ref)

        gg = g_ref[...]
        dho = dh_ref[...]
        dhb = dho.astype(BF)
        q = q_ref[...]
        do = _dot_nt(dhb, wo_ref[...].reshape(D, D)).astype(BF)
        for hd in range(HEADS):
            cs = slice(HD * hd, HD * (hd + 1))
            kh = k_ref[:, cs]
            vh = v_ref[:, cs]
            p = _softmax_rows(_dot_nt(q[:, cs], kh) * ATT_SCALE)
            pb = p.astype(BF)
            o_scr[:, cs] = _dot(pb, vh).astype(BF)
            doh = do[:, cs]
            dp = _dot_nt(doh, vh)
            dv_ref[:, cs] += _dot_tn(pb, doh)
            ds = (p * (dp - jnp.sum(dp * p, axis=-1, keepdims=True)) * ATT_SCALE).astype(BF)
            dq_scr[:, cs] = _dot(ds, kh).astype(BF)
            dk_ref[:, cs] += _dot_tn(ds, q[:, cs])
        oacc[...] += _dot_tn(o_scr[...], dhb)
        hhat, rr = _rms(h_ref[...])
        n = (hhat * gg).astype(BF)
        dq = dq_scr[...]
        qacc[...] += _dot_tn(n, dq)
        dn = _dot_nt(dq, wq_ref[...].reshape(D, D))
        dg_ref[...] += _colsum(dn * hhat)
        dhi_ref[...] = dho + _rms_bwd(dn, hhat, rr, gg)

        @pl.when(_last(i))
        def _():
            gqo_ref[:, 0:256, :] = qacc[...].astype(BF).reshape(NSH, 256, D)
            gqo_ref[:, 256:512, :] = oacc[...].astype(BF).reshape(NSH, 256, D)

    kvs = _const((NMEM, D), (0, 0))
    return _call(
        body, name=f"xattn_bwd{layer}", grid=(s // tm,),
        in_specs=[_rows(tm, D), _rows(tm, D), _rows(tm, D), _const((1, D), (0, 0)), kvs, kvs,
                  _xa_spec(XQ), _xa_spec(XO)],
        out_specs=[_rows(tm, D), kvs, kvs, _const((1, D), (0, 0)), _const((NSH, 512, D), (0, 0, 0))],
        out_shape=[_sds((s, D), F32), _sds((NMEM, D), F32), _sds((NMEM, D), F32), _sds((1, D), F32),
                   _sds(xa.shape, BF)],
        scratch_shapes=[pltpu.VMEM((tm, D), BF), pltpu.VMEM((tm, D), BF), pltpu.VMEM((D, D), F32),
                        pltpu.VMEM((D, D), F32)],
        args=(h, dh, q, g, k, v, xa, xa), carry=carry)


def _kv_proj_bwd(mem, g, dk, dv, xa, gxa, layer):
    def body(mem_ref, g_ref, dk_ref, dv_ref, wk_ref, wv_ref, gxa_in, dg_ref, gkv_ref):
        mhat, _ = _rms(mem_ref[...])
        mn = (mhat * g_ref[...]).astype(BF)
        dkb = dk_ref[...].astype(BF)
        dvb = dv_ref[...].astype(BF)
        gkv_ref[:, 0:256, :] = _dot_tn(mn, dkb).astype(BF).reshape(NSH, 256, D)
        gkv_ref[:, 256:512, :] = _dot_tn(mn, dvb).astype(BF).reshape(NSH, 256, D)
        dmn = _dot_nt(dkb, wk_ref[...].reshape(D, D)) + _dot_nt(dvb, wv_ref[...].reshape(D, D))
        dg_ref[...] = _colsum(dmn * mhat)

    full = _const((NMEM, D), (0, 0))
    return _call(
        body, name=f"kv_proj_bwd{layer}", grid=(1,),
        in_specs=[full, _const((1, D), (0, 0)), full, full, _xa_spec(XK), _xa_spec(XV), ANY],
        out_specs=[_const((1, D), (0, 0)), _const((NSH, 512, D), (0, 1, 0))],
        out_shape=[_sds((1, D), F32), _sds(gxa.shape, BF)],
        aliases={6: 1},
        args=(mem, g, dk, dv, xa, xa, gxa))


def _ffn_block(h_ref, g_ref, wg_ref, wu_ref, wd_ref, gt_ref, up_ref):
    hh = h_ref[...]
    hhat, _ = _rms(hh)
    n = (hhat * g_ref[...]).astype(BF)
    out = hh
    for c0, c1 in FF_CHUNKS:
        gt = _dot_nt(n, wg_ref[c0:c1, :])
        up = _dot_nt(n, wu_ref[c0:c1, :])
        gt_ref[:, c0:c1] = gt.astype(BF)
        up_ref[:, c0:c1] = up.astype(BF)
        act = (gt * _sigmoid(gt) * up).astype(BF)
        out = out + _dot(act, wd_ref[c0:c1, :])
    return out


def _ffn_fwd(h, g, wg, wu, dn, layer, tm, carry=None):
    s = h.shape[0]
    wg, wu, dn = (a.reshape(DFF, D) for a in (wg, wu, dn))

    def body(h_ref, g_ref, wg_ref, wu_ref, wd_ref, ho_ref, gt_ref, up_ref):
        ho_ref[...] = _ffn_block(h_ref, g_ref, wg_ref, wu_ref, wd_ref, gt_ref, up_ref)

    return _call(
        body, name=f"ffn_fwd{layer}", grid=(s // tm,),
        in_specs=[_rows(tm, D), _const((1, D), (0, 0)), _whole(wg), _whole(wu), _whole(dn)],
        out_specs=[_rows(tm, D), _rows(tm, DFF), _rows(tm, DFF)],
        out_shape=[_sds((s, D), F32), _sds((s, DFF), BF), _sds((s, DFF), BF)],
        args=(h, g, wg, wu, dn), carry=carry)


def _ffn_fwd_loss(h, g, wg, wu, dn, tgt, g_final, layer, tm):
    s = h.shape[0]
    wg, wu, dn = (a.reshape(DFF, D) for a in (wg, wu, dn))

    def body(h_ref, g_ref, wg_ref, wu_ref, wd_ref, t_ref, gf_ref, dh_ref, loss_ref, dg_ref, gt_ref, up_ref):
        i = pl.program_id(0)

        @pl.when(_first(i))
        def _():
            loss_ref[...] = jnp.zeros_like(loss_ref)
            dg_ref[...] = jnp.zeros_like(dg_ref)

        out = _ffn_block(h_ref, g_ref, wg_ref, wu_ref, wd_ref, gt_ref, up_ref)
        gg = gf_ref[...]
        hhat, rr = _rms(out)
        diff = hhat * gg - t_ref[...]
        loss_ref[...] += jnp.sum(diff * diff) * (0.5 / D)
        dout = diff * (1.0 / D)
        dg_ref[...] += _colsum(dout * hhat)
        dh_ref[...] = _rms_bwd(dout, hhat, rr, gg)

    vec = _const((1, D), (0, 0))
    return _call(
        body, name=f"ffn_fwd_loss{layer}", grid=(s // tm,),
        in_specs=[_rows(tm, D), vec, _whole(wg), _whole(wu), _whole(dn), _rows(tm, D), vec],
        out_specs=[_rows(tm, D), _const((1, 128), (0, 0)), vec, _rows(tm, DFF), _rows(tm, DFF)],
        out_shape=[_sds((s, D), F32), _sds((1, 128), F32), _sds((1, D), F32), _sds((s, DFF), BF),
                   _sds((s, DFF), BF)],
        args=(h, g, wg, wu, dn, tgt, g_final))


def _ffn_bwd_dx(h, dh, gt, up, g, wg, wu, dn, layer, tm, carry=None):
    s = h.shape[0]
    wg, wu, dn = (a.reshape(DFF, D) for a in (wg, wu, dn))

    def body(h_ref, dh_ref, gt_ref, up_ref, g_ref, wg_ref, wu_ref, wd_ref,
             dhi_ref, dg_ref, dgt_ref, dup_ref, act_ref, n_ref):
        i = pl.program_id(0)

        @pl.when(_first(i))
        def _():
            dg_ref[...] = jnp.zeros_like(dg_ref)

        gg = g_ref[...]
        dho = dh_ref[...]
        dhb = dho.astype(BF)
        dn_ = jnp.zeros((tm, D), F32)
        for c0, c1 in FF_CHUNKS:
            gtv = gt_ref[:, c0:c1].astype(F32)
            upv = up_ref[:, c0:c1].astype(F32)
            sg = _sigmoid(gtv)
            silu = gtv * sg
            dact = _dot_nt(dhb, wd_ref[c0:c1, :])
            dgt = (dact * upv * (sg * (1.0 + gtv * (1.0 - sg)))).astype(BF)
            dup = (dact * silu).astype(BF)
            dgt_ref[:, c0:c1] = dgt
            dup_ref[:, c0:c1] = dup
            act_ref[:, c0:c1] = (silu * upv).astype(BF)
            dn_ = dn_ + _dot(dgt, wg_ref[c0:c1, :]) + _dot(dup, wu_ref[c0:c1, :])
        hhat, rr = _rms(h_ref[...])
        n_ref[...] = (hhat * gg).astype(BF)
        dg_ref[...] += _colsum(dn_ * hhat)
        dhi_ref[...] = dho + _rms_bwd(dn_, hhat, rr, gg)

    wide = _rows(tm, DFF)
    return _call(
        body, name=f"ffn_bwd_dx{layer}", grid=(s // tm,),
        in_specs=[_rows(tm, D), _rows(tm, D), wide, wide, _const((1, D), (0, 0)),
                  _whole(wg), _whole(wu), _whole(dn)],
        out_specs=[_rows(tm, D), _const((1, D), (0, 0)), wide, wide, wide, _rows(tm, D)],
        out_shape=[_sds((s, D), F32), _sds((1, D), F32), _sds((s, DFF), BF), _sds((s, DFF), BF),
                   _sds((s, DFF), BF), _sds((s, D), BF)],
        args=(h, dh, gt, up, g, wg, wu, dn), carry=carry)


def _ffn_wgrad(a, b, name, tm, slot, into=None, carry=None):
    s = a.shape[0]
    tm = min(tm, s)

    def body(a_ref, b_ref, *rest):
        o_ref, acc = rest[-2:]
        i = pl.program_id(0)

        @pl.when(_first(i))
        def _():
            acc[...] = jnp.zeros_like(acc)

        acc[...] += _dot_tn(a_ref[...], b_ref[...].astype(BF))

        @pl.when(_last(i))
        def _():
            o_ref[...] = acc[...].astype(BF).reshape(NSH, FSH, D)

    chained = into is not None
    return _call(
        body, name=name, grid=(s // tm,),
        in_specs=[_rows(tm, DFF), _rows(tm, D)] + [ANY] * chained,
        out_specs=[_const((NSH, FSH, D), (0, slot, 0))],
        out_shape=[_sds((NSH, 3 * FSH, D), BF)],
        scratch_shapes=[pltpu.VMEM((DFF, D), F32)],
        aliases={2: 0} if chained else None,
        args=(a, b) + ((into,) if chained else ()), carry=carry)


def _sgu_fwd(h, g, lng, lnb, ws, bsb, wod, odo, tm, carry=None):
    s = h.shape[0]

    def body(h_ref, g_ref, lng_ref, lnb_ref, ws_ref, bsb_ref, wi_ref, wo_ref, ho_ref, zp_ref, y_scr):
        hh = h_ref[...]
        hhat, _ = _rms(hh)
        n = (hhat * g_ref[...]).astype(BF)
        zpre = _dot(n, wi_ref[...])
        zp_ref[...] = zpre.astype(BF)
        z = _gelu(zpre)
        u = z[:, :D]
        y, _ = _ln(z[:, D:])
        vn = (y * lng_ref[...] + lnb_ref[...]).astype(BF)
        for c in range(tm // CHUNK):
            rs = slice(CHUNK * c, CHUNK * (c + 1))
            for gi in range(GROUPS):
                cs = slice(CHUNK * gi, CHUNK * (gi + 1))
                sv = _dot(ws_ref[gi], vn[rs, cs]) + bsb_ref[gi]
                y_scr[rs, cs] = (u[rs, cs] * sv).astype(BF)
        ho_ref[...] = hh + _dot(y_scr[...], wo_ref[...].reshape(D, D))

    vec = _const((1, D), (0, 0))
    sq = _const((GROUPS, CHUNK, CHUNK), (0, 0, 0))
    return _call(
        body, name="sgu_fwd", grid=(s // tm,),
        in_specs=[_rows(tm, D), vec, vec, vec, sq, sq, _const((D, 2 * D), (0, 0)), _whole(odo)],
        out_specs=[_rows(tm, D), _rows(tm, 2 * D)],
        out_shape=[_sds((s, D), F32), _sds((s, 2 * D), BF)],
        scratch_shapes=[pltpu.VMEM((tm, D), BF)],
        args=(h, g, lng, lnb, ws, bsb, wod, odo), carry=carry)


def _sgu_bwd(h, dh, zpre, g, lng, lnb, ws, wst, bsb, wod, odo, tm, carry=None):
    s = h.shape[0]

    def body(h_ref, dh_ref, zp_ref, g_ref, lng_ref, lnb_ref, ws_ref, wst_ref, bsb_ref, wi_ref, wo_ref,
             dhi_ref, dws_ref, dbs_ref, dlng_ref, dlnb_ref, dg_ref, gwi_ref, gwo_ref,
             y_scr, dz_scr, dvn_scr, iacc, oacc, bacc):
        i = pl.program_id(0)

        @pl.when(_first(i))
        def _():
            for ref in (iacc, oacc, bacc, dws_ref, dlng_ref, dlnb_ref, dg_ref):
                ref[...] = jnp.zeros_like(ref)

        gg = g_ref[...]
        dho = dh_ref[...]
        dhb = dho.astype(BF)
        zpre_v = zp_ref[...].astype(F32)
        z = _gelu(zpre_v)
        u = z[:, :D]
        yl, rstd = _ln(z[:, D:])
        vn = (yl * lng_ref[...] + lnb_ref[...]).astype(BF)
        dy = _dot_nt(dhb, wo_ref[...].reshape(D, D))
        for c in range(tm // CHUNK):
            rs = slice(CHUNK * c, CHUNK * (c + 1))
            for gi in range(GROUPS):
                cs = slice(CHUNK * gi, CHUNK * (gi + 1))
                vb = vn[rs, cs]
                sv = _dot(ws_ref[gi], vb) + bsb_ref[gi]
                ub = u[rs, cs]
                dyb = dy[rs, cs]
                y_scr[rs, cs] = (ub * sv).astype(BF)
                dz_scr[rs, cs] = dyb * sv
                dsv = dyb * ub
                bacc[gi] += dsv
                dsvb = dsv.astype(BF)
                dws_ref[gi] += _dot_nt(dsvb, vb)
                dvn_scr[rs, cs] = _dot(wst_ref[gi], dsvb)
        oacc[...] += _dot_tn(y_scr[...], dhb)
        dvn = dvn_scr[...]
        dlng_ref[...] += _colsum(dvn * yl)
        dlnb_ref[...] += _colsum(dvn)
        dz_scr[:, D:] = _ln_bwd(dvn * lng_ref[...], yl, rstd)
        dzb = (dz_scr[...] * _gelu_grad(zpre_v)).astype(BF)
        hhat, rr = _rms(h_ref[...])
        n = (hhat * gg).astype(BF)
        iacc[...] += _dot_tn(n, dzb)
        dn = _dot_nt(dzb, wi_ref[...])
        dg_ref[...] += _colsum(dn * hhat)
        dhi_ref[...] = dho + _rms_bwd(dn, hhat, rr, gg)

        @pl.when(_last(i))
        def _():
            for k in range(NSH):
                gwi_ref[k] = iacc[:, 512 * k:512 * (k + 1)].astype(BF)
            gwo_ref[...] = oacc[...].astype(BF).reshape(NSH, 256, D)
            dbs_ref[...] = jnp.sum(bacc[...], axis=-1)

    vec = _const((1, D), (0, 0))
    sq = _const((GROUPS, CHUNK, CHUNK), (0, 0, 0))
    return _call(
        body, name="sgu_bwd", grid=(s // tm,),
        in_specs=[_rows(tm, D), _rows(tm, D), _rows(tm, 2 * D), vec, vec, vec, sq, sq, sq,
                  _const((D, 2 * D), (0, 0)), _whole(odo)],
        out_specs=[_rows(tm, D), sq, _const((GROUPS, CHUNK), (0, 0)), vec, vec, vec,
                   _const((NSH, D, 512), (0, 0, 0)), _whole(odo)],
        out_shape=[_sds((s, D), F32), _sds((GROUPS, CHUNK, CHUNK), F32), _sds((GROUPS, CHUNK), F32),
                   _sds((1, D), F32), _sds((1, D), F32), _sds((1, D), F32), _sds((NSH, D, 512), BF),
                   _sds(odo.shape, BF)],
        scratch_shapes=[pltpu.VMEM((tm, D), BF), pltpu.VMEM((tm, 2 * D), F32),
                        pltpu.VMEM((tm, D), F32), pltpu.VMEM((D, 2 * D), F32), pltpu.VMEM((D, D), F32),
                        pltpu.VMEM((GROUPS, CHUNK, CHUNK), F32)],
        args=(h, dh, zpre, g, lng, lnb, ws, wst, bsb, wod, odo), carry=carry)


def _own_plane(shape):
    return pl.BlockSpec((1,) + shape, lambda i, chip_ref: (chip_ref[0], 0, 0), pipeline_mode=pl.Buffered(1))


def _pack_rows(chip_idx, w_down, w_gate_t, w_up_t, xq, xo, xk, xv, ev_out, od_out, first):
    nf = len(first)

    def body(chip_ref, wd_ref, wg_ref, wu_ref, xq_ref, xo_ref, xk_ref, xv_ref, ev_ref, od_ref, *rest):
        (evo_ref, odo_ref, dn0_ref, dn1_ref, wg0_ref, wg1_ref, wu0_ref, wu1_ref, xa0_ref,
         xa1_ref) = rest[nf:nf + 10]
        gathered = rest[nf + 10:2 * nf + 10]
        sems = rest[2 * nf + 10:]
        _gather_start(gathered, sems)
        evo_ref[0] = ev_ref[0].astype(BF)
        odo_ref[0] = od_ref[0].astype(BF)
        for src, outs in ((wd_ref, (dn0_ref, dn1_ref)), (wg_ref, (wg0_ref, wg1_ref)), (wu_ref, (wu0_ref, wu1_ref))):
            for layer, o_ref in enumerate(outs):
                o_ref[0] = src[layer].astype(BF)
        for layer, xa_ref in enumerate((xa0_ref, xa1_ref)):
            for item, src in ((XQ, xq_ref), (XO, xo_ref), (XK, xk_ref), (XV, xv_ref)):
                xa_ref[0, 256 * item:256 * (item + 1), :] = src[layer].astype(BF)
        _gather_forward(gathered, sems)
        _gather_drain(gathered, sems)

    ins = (w_down, w_gate_t, w_up_t, xq, xo, xk, xv, ev_out, od_out)
    shapes = [(256, D), (256, D)] + [(FSH, D)] * 6 + [(4 * 256, D), (4 * 256, D)]
    grid_spec = pltpu.PrefetchScalarGridSpec(
        num_scalar_prefetch=1, grid=(1,), in_specs=[_whole(a) for a in ins] + [ANY] * nf,
        out_specs=[_own_plane(sh) for sh in shapes] + [ANY] * nf,
        scratch_shapes=_gather_sems(nf))
    outs = pl.pallas_call(
        body, name="pack_rows", grid_spec=grid_spec,
        out_shape=[_sds((NSH,) + sh, BF) for sh in shapes] + [_sds(a.shape, a.dtype) for a in first],
        input_output_aliases={1 + len(ins) + t: len(shapes) + t for t in range(nf)},
        compiler_params=_params(),
    )(chip_idx, *ins, *first)
    return outs[:len(shapes)], outs[len(shapes):]


def _pack_cols(chip_idx, od_in, ev_in, conv_a, conv_b, ln_g, ln_b):
    def body(chip_ref, od_ref, ev_ref, ca_ref, cb_ref, lg_ref, lb_ref, cev_ref, cod_ref, sp_ref):
        cev_ref[0] = ev_ref[0].astype(BF)
        cod_ref[0] = od_ref[0].astype(BF)
        sp_ref[...] = jnp.zeros_like(sp_ref)
        sp_ref[0, 0:KA, 0:128] = ca_ref[0]
        sp_ref[0, 32:32 + KB, 0:128] = cb_ref[0]
        sp_ref[0, 40:41, :] = lg_ref[...]
        sp_ref[0, 41:42, :] = lb_ref[...]

    ins = (od_in, ev_in, conv_a, conv_b, ln_g, ln_b)
    shapes = [(D, 640), (D, 512)]
    grid_spec = pltpu.PrefetchScalarGridSpec(
        num_scalar_prefetch=1, grid=(1,), in_specs=[_whole(a) for a in ins],
        out_specs=[_own_plane(sh) for sh in shapes] + [_own_plane((SP_ROWS, 256))])
    return pl.pallas_call(
        body, name="pack_cols", grid_spec=grid_spec,
        out_shape=[_sds((NSH,) + sh, BF) for sh in shapes] + [_sds((NSH, SP_ROWS, 256), F32)],
        compiler_params=_params(),
    )(chip_idx, *ins)


def _assemble(piece, parts, name):
    tr = 256
    cols = piece.shape[2]

    def body(p_ref, *outs):
        for k in range(NSH):
            for (off, width), o_ref in zip(parts, outs):
                o_ref[:, width * k:width * (k + 1)] = p_ref[k, :, off:off + width]

    return _call(
        body, name=name, grid=(D // tr,),
        in_specs=[pl.BlockSpec((NSH, tr, cols), lambda i: (0, i, 0))],
        out_specs=[pl.BlockSpec((tr, NSH * width), lambda i: (i, 0)) for _, width in parts],
        out_shape=[_sds((D, NSH * width), BF) for _, width in parts],
        args=(piece,))


def _blocks_of(rows, cols, itemsize):
    rb = rows
    while rb * cols * itemsize > (2 << 20) and rb % 2 == 0 and (rb // 2) % 16 == 0:
        rb //= 2
    return rb


def _add_half(full, recv, c_idx, name):
    p, r, cols = full.shape
    rb = _blocks_of(r // 2, cols, 4)
    nb = (r // 2) // rb

    def body(c_ref, a_ref, b_ref, o_ref):
        o_ref[...] = (a_ref[...].astype(F32) + b_ref[...].astype(F32)).astype(o_ref.dtype)

    grid_spec = pltpu.PrefetchScalarGridSpec(
        num_scalar_prefetch=1, grid=(p, nb),
        in_specs=[pl.BlockSpec((1, rb, cols), lambda j, i, c_ref: (j, c_ref[0] * nb + i, 0)),
                  pl.BlockSpec((1, rb, cols), lambda j, i, c_ref: (j, i, 0))],
        out_specs=pl.BlockSpec((1, rb, cols), lambda j, i, c_ref: (j, i, 0)))
    return pl.pallas_call(
        body, name=name, grid_spec=grid_spec, out_shape=_sds(recv.shape, full.dtype),
        compiler_params=_params(2),
    )(c_idx, full, recv)


def _sum_chips(own, got, plane_c, name):
    _, r, cols = own.shape
    rb = _blocks_of(r, cols, 4)
    nb = r // rb

    def body(s_ref, a_ref, b_ref, o_ref):
        f = lambda v: v.astype(F32)
        o_ref[...] = (f(a_ref[0]) + f(b_ref[1])) + (f(b_ref[0]) + f(b_ref[2]))

    grid_spec = pltpu.PrefetchScalarGridSpec(
        num_scalar_prefetch=1, grid=(nb,),
        in_specs=[pl.BlockSpec((1, rb, cols), lambda i, s_ref: (s_ref[0], i, 0)),
                  pl.BlockSpec((3, rb, cols), lambda i, s_ref: (0, i, 0))],
        out_specs=pl.BlockSpec((rb, cols), lambda i, s_ref: (s_ref[1] * nb + i, 0)))
    return pl.pallas_call(
        body, name=name, grid_spec=grid_spec, out_shape=_sds((2 * r, cols), F32),
        compiler_params=_params(1),
    )(plane_c, own, got)


def _adamw_math(w, g, m, v):
    m = ADAM_B1 * m + (1.0 - ADAM_B1) * g
    v = ADAM_B2 * v + (1.0 - ADAM_B2) * (g * g)
    m_hat = m / (1.0 - ADAM_B1 ** ADAM_STEP)
    v_hat = v / (1.0 - ADAM_B2 ** ADAM_STEP)
    delta = -ADAM_LR * (m_hat / (jnp.sqrt(v_hat) + ADAM_EPS) + ADAM_WD * w)
    return delta, m, v


def _adamw_big(w, m, v, gsrcs, gblock, gidx, name, rb):
    nl, r, cols = w.shape
    assert len(gsrcs) == nl

    def body(w_ref, m_ref, v_ref, *rest):
        g_refs, (go_ref, d_ref, mo_ref, vo_ref) = rest[:nl], rest[nl:]
        g = g_refs[0][:, 0:cols]
        if nl == 2:
            g = jnp.where(pl.program_id(0) == 0, g, g_refs[1][:, 0:cols])
        delta, mn, vn = _adamw_math(w_ref[0], g, m_ref[0], v_ref[0])
        go_ref[0] = g
        d_ref[0] = delta
        mo_ref[0] = mn
        vo_ref[0] = vn

    wspec = pl.BlockSpec((1, rb, cols), lambda l, i: (l, i, 0))
    gspec = pl.BlockSpec(gblock, lambda l, i: gidx(i))
    return pl.pallas_call(
        body, name=name, grid=(nl, r // rb),
        in_specs=[wspec, wspec, wspec] + [gspec] * nl,
        out_specs=[wspec] * 4, out_shape=[_sds(w.shape, F32)] * 4,
        compiler_params=_params(2),
    )(w, m, v, *gsrcs)


_SMALL = (
    ("g_mix", (2, D), "a", (0, 2), 0, False),
    ("g_xattn", (2, D), "a", (2, 4), 0, False),
    ("g_mem", (2, D), "a", (4, 6), 0, False),
    ("g_ffn", (2, D), "a", (6, 8), 0, False),
    ("g_final", (1, D), "a", (8, 9), 0, False),
    ("ev_a_conv_w", (KA, 128), "a", (16, 16 + KA), 0, True),
    ("ev_a_conv_b", (1, DA), "a", (11, 12), 0, False),
    ("ev_a_ln_g", (1, DA), "a", (11, 12), DA, False),
    ("ev_a_ln_b", (1, DA), "a", (12, 13), 0, False),
    ("ev_b_conv_w", (KB, 128), "a", (48, 48 + KB), 0, True),
    ("ev_b_conv_b", (1, DA), "a", (12, 13), DA, False),
    ("od_c_ln_g", (1, 256), "a", (9, 10), 0, True),
    ("od_c_ln_b", (1, 256), "a", (10, 11), 0, True),
    ("od_w_s", (GROUPS * CHUNK, CHUNK), "b", (0, GROUPS * CHUNK), 0, False),
    ("od_b_s", (GROUPS, CHUNK), "a", (56, 64), 0, False),
)


def _adamw_small(ga, gb, wmv):
    ns = len(_SMALL)

    def body(*refs):
        ga_ref, gb_ref = refs[0], refs[1]
        ins = refs[2:2 + 3 * ns]
        outs = refs[2 + 3 * ns:]
        x, y, _ = _pos()
        chip = 2 * x + y
        for j, (_, shape, pack, (r0, r1), col, sharded) in enumerate(_SMALL):
            src = ga_ref if pack == "a" else gb_ref
            width = shape[1]
            if sharded:
                g = jnp.zeros(shape, F32)
                for k in range(NSH):
                    blk = src[r0:r1, col + width * k:col + width * (k + 1)]
                    g = g + jnp.where(chip == k, blk, 0.0)
            else:
                g = src[r0:r1, col:col + width]
            w_ref, m_ref, v_ref = ins[3 * j:3 * j + 3]
            delta, mn, vn = _adamw_math(w_ref[...], g, m_ref[...], v_ref[...])
            go_ref, d_ref, mo_ref, vo_ref = outs[4 * j:4 * j + 4]
            go_ref[...] = g
            d_ref[...] = delta
            mo_ref[...] = mn
            vo_ref[...] = vn

    vm = pl.BlockSpec(memory_space=pltpu.VMEM)
    flat = [a for trio in wmv for a in trio]
    out_shape = []
    for (_, shape, *_rest) in _SMALL:
        out_shape += [_sds(shape, F32)] * 4
    outs = pl.pallas_call(
        body, name="adamw_small",
        in_specs=[vm] * (2 + 3 * ns), out_specs=[vm] * (4 * ns), out_shape=out_shape,
    )(ga, gb, *flat)
    return [tuple(outs[4 * j:4 * j + 4]) for j in range(ns)]


def _small_params(gs, W):
    unshard = lambda a: jnp.transpose(a, (1, 0, 2)).reshape(a.shape[1], NSH * a.shape[2])
    od_w_s, od_b_s = W["od_w_s"], W["od_b_s"]
    return dict(
        g_mix=W["g_mix"], g_xattn=W["g_xattn"], g_mem=W["g_mem"], g_ffn=W["g_ffn"],
        g_final=W["g_final"].reshape(1, D),
        wa=unshard(gs[:, 0:32, 0:128]), ba=W["ev_a_conv_b"], lng_a=W["ev_a_ln_g"], lnb_a=W["ev_a_ln_b"],
        wb=unshard(gs[:, 32:40, 0:128]), bb=W["ev_b_conv_b"],
        lng_c=unshard(gs[:, 40:41, :]), lnb_c=unshard(gs[:, 41:42, :]),
        ws=od_w_s[0].astype(BF), wst=jnp.swapaxes(od_w_s[0], 1, 2).astype(BF),
        bsb=jnp.broadcast_to(od_b_s[0][:, :, None], (GROUPS, CHUNK, CHUNK)),
    )


LOSS_ROW = 13


def _small_grad_packs(g, loss):
    z = lambda r, c: jnp.zeros((r, c), F32)
    wide = lambda a: jnp.concatenate([a, z(a.shape[0], D - a.shape[1])], axis=1)
    rows = [g["g_mix"], g["g_xattn"], g["g_mem"], g["g_ffn"], g["g_final"], g["od_c_ln_g"], g["od_c_ln_b"],
            jnp.concatenate([g["ev_a_conv_b"], g["ev_a_ln_g"]], axis=1),
            jnp.concatenate([g["ev_a_ln_b"], g["ev_b_conv_b"]], axis=1),
            wide(loss), z(2, D), wide(g["ev_a_conv_w"]), wide(g["ev_b_conv_w"]), wide(g["od_b_s"])]
    a = jnp.concatenate(rows, axis=0)
    assert a.shape == (SA_ROWS, D), a.shape
    return a, g["od_w_s"].reshape(GROUPS * CHUNK, CHUNK)


def _step(x, mem, tgt, W, P, c_idx, tm_f, tm_b, tm_w):
    row = lambda a, i: a[i:i + 1]

    sm = _small_params(P["gs"], W)
    (wev,) = _assemble(P["cev"], [(0, 640)], "assemble_ev")
    z, evo, xa0 = _ev_in_fwd(x, row(sm["g_mix"], 0), wev, 2 * tm_f, carry=("gather", [P["evo"], P["xa0"]]))
    h1, a2, c2, cat, wg0, wu0 = _ev_mix_fwd(
        x, z, sm["wa"], sm["ba"], sm["lng_a"], sm["lnb_a"], sm["wb"], sm["bb"], evo, tm_f,
        carry=("gather", [P["wg0"], P["wu0"]]))
    k0, v0 = _kv_proj(mem, row(sm["g_mem"], 0), xa0, 0)
    h2, q0, dn0, cod = _xattn_fwd(h1, row(sm["g_xattn"], 0), k0, v0, xa0, 0, 2 * tm_f,
                                  carry=("gather", [P["dn0"], P["cod"]]))
    (wod,) = _assemble(cod, [(0, 512)], "assemble_od")
    h3, gt0, up0, odo, xa1, wg1 = _ffn_fwd(h2, row(sm["g_ffn"], 0), wg0, wu0, dn0, 0, tm_f,
                                           carry=("gather", [P["odo"], P["xa1"], P["wg1"]]))
    h4, zpre, wu1, dn1 = _sgu_fwd(h3, row(sm["g_mix"], 1), sm["lng_c"], sm["lnb_c"], sm["ws"], sm["bsb"], wod,
                                  odo, tm_f, carry=("gather", [P["wu1"], P["dn1"]]))
    k1, v1 = _kv_proj(mem, row(sm["g_mem"], 1), xa1, 1)
    h5, q1 = _xattn_fwd(h4, row(sm["g_xattn"], 1), k1, v1, xa1, 1, 2 * tm_f)
    dh6, loss, dg_final, gt1, up1 = _ffn_fwd_loss(h5, row(sm["g_ffn"], 1), wg1, wu1, dn1, tgt, sm["g_final"], 1,
                                                  tm_f)

    g = {"g_final": dg_final}
    parts, gots = {}, {}
    def add(name, piece, recv):
        parts[name] = _add_half(piece, recv, c_idx, f"rs_add_{name}")
        return parts[name]

    def ffn_wgrads(layer, dgt, dup, act, nb, dh_out, carry=None):
        g_ffn, *carried = _ffn_wgrad(dgt, nb, f"ffn_wgrad_gate{layer}", tm_w, 0, carry=carry)
        (g_ffn,) = _ffn_wgrad(dup, nb, f"ffn_wgrad_up{layer}", tm_w, 1, into=g_ffn)
        (g_ffn,) = _ffn_wgrad(act, dh_out, f"ffn_wgrad_down{layer}", tm_w, 2, into=g_ffn)
        return g_ffn, carried

    dh5, dg_ffn1, dgt, dup, act, nb = _ffn_bwd_dx(h5, dh6, gt1, up1, row(sm["g_ffn"], 1), wg1, wu1, dn1, 1, tm_b)
    g_ffn1, _ = ffn_wgrads(1, dgt, dup, act, nb, dh6)
    dh4, dk, dv, dg_xa1, g_xa1, recv = _xattn_bwd(
        h4, dh5, q1, row(sm["g_xattn"], 1), k1, v1, xa1, 1, tm_f, carry=("swap", [g_ffn1]))
    add("ffn1", g_ffn1, recv)
    dg_mem1, g_xa1 = _kv_proj_bwd(mem, row(sm["g_mem"], 1), dk, dv, xa1, g_xa1, 1)
    (dh3, g["od_w_s"], g["od_b_s"], g["od_c_ln_g"], g["od_c_ln_b"], dg_mix1, g_cod, g_odo, gots["ffn1"],
     recv) = _sgu_bwd(
        h3, dh4, zpre, row(sm["g_mix"], 1), sm["lng_c"], sm["lnb_c"], sm["ws"], sm["wst"], sm["bsb"], wod, odo, tm_b,
        carry=[("xchg", [parts["ffn1"]], []), ("swap", [g_xa1])])
    add("xa1", g_xa1, recv)
    dh2, dg_ffn0, dgt, dup, act, nb, gots["xa1"], recv_cod, recv_odo = _ffn_bwd_dx(
        h2, dh3, gt0, up0, row(sm["g_ffn"], 0), wg0, wu0, dn0, 0, tm_b,
        carry=[("xchg", [parts["xa1"]], []), ("swap", [g_cod, g_odo])])
    add("cod", g_cod, recv_cod)
    add("odo", g_odo, recv_odo)
    g_ffn0, (gots["cod"], gots["odo"]) = ffn_wgrads(0, dgt, dup, act, nb, dh3,
                                                    carry=("xchg", [parts["cod"], parts["odo"]], []))
    dh1, dk, dv, dg_xa0, g_xa0, recv = _xattn_bwd(
        h1, dh2, q0, row(sm["g_xattn"], 0), k0, v0, xa0, 0, tm_f, carry=("swap", [g_ffn0]))
    add("ffn0", g_ffn0, recv)
    dg_mem0, g_xa0 = _kv_proj_bwd(mem, row(sm["g_mem"], 0), dk, dv, xa0, g_xa0, 0)
    da2, dc2, dgb, g["ev_a_ln_g"], g["ev_a_ln_b"], g_evo, recv = _ev_mix_bwd(
        dh1, cat, a2, c2, z, sm["lng_a"], sm["lnb_a"], evo, tm_f, carry=("swap", [g_xa0]))
    add("xa0", g_xa0, recv)
    handle, token = _comm_start([parts["ffn0"], parts["xa0"]], [], [g_evo], "ev")
    dx, g["ev_a_conv_w"], g["ev_a_conv_b"], g["ev_b_conv_w"], g["ev_b_conv_b"], dg_mix0, g_cev = _ev_in_bwd(
        x, dh1, z, da2, dc2, dgb, row(sm["g_mix"], 0), sm["wa"], sm["wb"], wev, tm_b, token)
    (parts["ffn0"], parts["xa0"], g_evo), (gots["ffn0"], gots["xa0"], recv) = _comm_wait(handle, g_cev)
    add("evo", g_evo, recv)
    g["g_mix"] = jnp.concatenate([dg_mix0, dg_mix1], axis=0)
    g["g_xattn"] = jnp.concatenate([dg_xa0, dg_xa1], axis=0)
    g["g_mem"] = jnp.concatenate([dg_mem0, dg_mem1], axis=0)
    g["g_ffn"] = jnp.concatenate([dg_ffn0, dg_ffn1], axis=0)
    ga, gb = _small_grad_packs(g, loss)
    last = (g_cev, ga[None], gb[None])
    for name, piece, recv in zip(("cev", "ga", "gb"), last, _sibling_swap(last, "rs_swap_last")):
        add(name, piece, recv)
    return dx, parts, gots


def kernel(x, mem, g_mix, g_xattn, g_mem, g_ffn, g_final, ev_w_in, ev_a_conv_w, ev_a_conv_b, ev_a_ln_g, ev_a_ln_b, ev_b_conv_w, ev_b_conv_b, ev_w_out, od_w_in, od_c_ln_g, od_c_ln_b, od_w_s, od_b_s, od_w_out, xa_w_q, xa_w_k, xa_w_v, xa_w_o, ffn_w_gate, ffn_w_up, ffn_w_down, loss_target, m_g_mix, m_g_xattn, m_g_mem, m_g_ffn, m_g_final, m_ev_w_in, m_ev_a_conv_w, m_ev_a_conv_b, m_ev_a_ln_g, m_ev_a_ln_b, m_ev_b_conv_w, m_ev_b_conv_b, m_ev_w_out, m_od_w_in, m_od_c_ln_g, m_od_c_ln_b, m_od_w_s, m_od_b_s, m_od_w_out, m_xa_w_q, m_xa_w_k, m_xa_w_v, m_xa_w_o, m_ffn_w_gate, m_ffn_w_up, m_ffn_w_down, v_g_mix, v_g_xattn, v_g_mem, v_g_ffn, v_g_final, v_ev_w_in, v_ev_a_conv_w, v_ev_a_conv_b, v_ev_a_ln_g, v_ev_a_ln_b, v_ev_b_conv_w, v_ev_b_conv_b, v_ev_w_out, v_od_w_in, v_od_c_ln_g, v_od_c_ln_b, v_od_w_s, v_od_b_s, v_od_w_out, v_xa_w_q, v_xa_w_k, v_xa_w_v, v_xa_w_o, v_ffn_w_gate, v_ffn_w_up, v_ffn_w_down):
    W = dict(g_mix=g_mix, g_xattn=g_xattn, g_mem=g_mem, g_ffn=g_ffn, g_final=g_final, ev_w_in=ev_w_in,
             ev_a_conv_w=ev_a_conv_w, ev_a_conv_b=ev_a_conv_b, ev_a_ln_g=ev_a_ln_g, ev_a_ln_b=ev_a_ln_b,
             ev_b_conv_w=ev_b_conv_w, ev_b_conv_b=ev_b_conv_b, ev_w_out=ev_w_out, od_w_in=od_w_in,
             od_c_ln_g=od_c_ln_g, od_c_ln_b=od_c_ln_b, od_w_s=od_w_s, od_b_s=od_b_s, od_w_out=od_w_out,
             xa_w_q=xa_w_q, xa_w_k=xa_w_k, xa_w_v=xa_w_v, xa_w_o=xa_w_o, ffn_w_gate=ffn_w_gate,
             ffn_w_up=ffn_w_up, ffn_w_down=ffn_w_down)
    M = dict(g_mix=m_g_mix, g_xattn=m_g_xattn, g_mem=m_g_mem, g_ffn=m_g_ffn, g_final=m_g_final, ev_w_in=m_ev_w_in,
             ev_a_conv_w=m_ev_a_conv_w, ev_a_conv_b=m_ev_a_conv_b, ev_a_ln_g=m_ev_a_ln_g, ev_a_ln_b=m_ev_a_ln_b,
             ev_b_conv_w=m_ev_b_conv_w, ev_b_conv_b=m_ev_b_conv_b, ev_w_out=m_ev_w_out, od_w_in=m_od_w_in,
             od_c_ln_g=m_od_c_ln_g, od_c_ln_b=m_od_c_ln_b, od_w_s=m_od_w_s, od_b_s=m_od_b_s, od_w_out=m_od_w_out,
             xa_w_q=m_xa_w_q, xa_w_k=m_xa_w_k, xa_w_v=m_xa_w_v, xa_w_o=m_xa_w_o, ffn_w_gate=m_ffn_w_gate,
             ffn_w_up=m_ffn_w_up, ffn_w_down=m_ffn_w_down)
    V = dict(g_mix=v_g_mix, g_xattn=v_g_xattn, g_mem=v_g_mem, g_ffn=v_g_ffn, g_final=v_g_final, ev_w_in=v_ev_w_in,
             ev_a_conv_w=v_ev_a_conv_w, ev_a_conv_b=v_ev_a_conv_b, ev_a_ln_g=v_ev_a_ln_g, ev_a_ln_b=v_ev_a_ln_b,
             ev_b_conv_w=v_ev_b_conv_w, ev_b_conv_b=v_ev_b_conv_b, ev_w_out=v_ev_w_out, od_w_in=v_od_w_in,
             od_c_ln_g=v_od_c_ln_g, od_c_ln_b=v_od_c_ln_b, od_w_s=v_od_w_s, od_b_s=v_od_b_s, od_w_out=v_od_w_out,
             xa_w_q=v_xa_w_q, xa_w_k=v_xa_w_k, xa_w_v=v_xa_w_v, xa_w_o=v_xa_w_o, ffn_w_gate=v_ffn_w_gate,
             ffn_w_up=v_ffn_w_up, ffn_w_down=v_ffn_w_down)
    order = list(W)
    cx, cy, cc = _pos()
    chip = (2 * cx + cy).astype(jnp.int32)
    cc = cc.astype(jnp.int32)
    chip_idx = jnp.reshape(chip, (1,))
    c_idx = jnp.reshape(cc, (1,))

    tr = lambda a: jnp.swapaxes(a, 1, 2)
    P = {}
    cev, P["cod"], gs = _pack_cols(chip_idx, od_w_in, ev_w_in, ev_a_conv_w, ev_b_conv_w, od_c_ln_g, od_c_ln_b)
    ((P["evo"], P["odo"], P["dn0"], P["dn1"], P["wg0"], P["wg1"], P["wu0"], P["wu1"], P["xa0"], P["xa1"]),
     (P["cev"], P["gs"])) = _pack_rows(chip_idx, ffn_w_down, tr(ffn_w_gate), tr(ffn_w_up), xa_w_q, xa_w_o,
                                       xa_w_k, xa_w_v, ev_w_out, od_w_out, first=[cev, gs])

    dx, parts, gots = _step(x[0], mem[0], loss_target[0], W, P, c_idx, TM_FWD, TM_BWD, TM_WGRAD)

    chip_c = jnp.stack([chip, cc])
    zero_c = jnp.stack([jnp.zeros((), jnp.int32), cc])
    def sums(names):
        return [_sum_chips(parts[n], gots[n], zero_c if n in ("ga", "gb") else chip_c, f"rs_sum_{n}")
                for n in names]

    last = ("evo", "cev", "ga", "gb")
    handle, token = _comm_start([parts["evo"], parts["cev"]], [parts["ga"][0], parts["gb"][0]], [], "last")
    big_names, rest_names = ["ffn1", "ffn0"], ["xa1", "xa0", "cod", "odo"]
    join_big, token = _join_start(sums(big_names), "ffn", after=(token,))
    join_rest, token = _join_start(sums(rest_names), "rest", after=(token,))
    red = dict(zip(big_names, _join_wait(join_big, token)))

    res = {}

    def adamw(name, srcs, gblock, gidx, rb):
        t = tr if name in ("ffn_w_gate", "ffn_w_up") else (lambda a: a)
        outs = _adamw_big(t(W[name]), t(M[name]), t(V[name]), srcs, gblock, gidx, f"adamw_{name}", rb)
        res[name] = tuple(t(o) for o in outs)

    ffn = [red["ffn0"], red["ffn1"]]
    adamw("ffn_w_gate", ffn, (FSH, D), lambda i: (0, 0), FSH)
    adamw("ffn_w_up", ffn, (FSH, D), lambda i: (1, 0), FSH)
    adamw("ffn_w_down", ffn, (FSH, D), lambda i: (2, 0), FSH)
    red.update(zip(rest_names, _join_wait(join_rest, res["ffn_w_down"][1])))
    xa = [red["xa0"], red["xa1"]]
    adamw("xa_w_q", xa, (256, D), lambda i: (XQ, 0), 256)
    adamw("xa_w_o", xa, (256, D), lambda i: (XO, 0), 256)
    adamw("xa_w_k", xa, (256, D), lambda i: (XK, 0), 256)
    adamw("xa_w_v", xa, (256, D), lambda i: (XV, 0), 256)
    adamw("od_w_out", [red["odo"]], (256, D), lambda i: (0, 0), 256)
    adamw("od_w_in", [red["cod"]], (256, 512), lambda i: (i, 0), 256)

    srcs_done, got_last = _comm_wait(handle, res["od_w_in"][1])
    parts.update(zip(last, srcs_done[:2] + [a[None] for a in srcs_done[2:]]))
    gots.update(zip(last, got_last))
    red.update(zip(last, _sibling_join(sums(last), "rs_join_last")))
    adamw("ev_w_out", [red["evo"]], (256, D), lambda i: (0, 0), 256)
    adamw("ev_w_in", [red["cev"]], (256, 640), lambda i: (i, 0), 256)
    shape2 = {name: shape for name, shape, *_ in _SMALL}
    wmv = [tuple(d[name].reshape(shape2[name]) for d in (W, M, V)) for name, *_ in _SMALL]
    for (name, *_), outs in zip(_SMALL, _adamw_small(red["ga"], red["gb"], wmv)):
        res[name] = tuple(o.reshape(W[name].shape) for o in outs)

    grad_x = dx[None]
    loss = red["ga"][LOSS_ROW, 0]
    return (loss, grad_x, *[res[n][0] for n in order], *[res[n][1] for n in order],
            *[res[n][2] for n in order], *[res[n][3] for n in order])
```

```python
import jax
import jax.numpy as jnp
from jax import lax
from jax.experimental import pallas as pl
from jax.experimental.pallas import tpu as pltpu

BF = jnp.bfloat16
F32 = jnp.float32

D = 1024
DA = 512
DZ = 2560
DFF = 2816
NSH = 4
FSH = DFF // NSH
FF_CHUNKS = ((0, 768), (768, 1536), (1536, 2304), (2304, 2816))
NMEM = 256
HEADS = 4
HD = D // HEADS
CHUNK = 128
GROUPS = 8
KA = 31
KB = 3
HALO = 16
RC = 32
RMS_EPS = 1e-6
LN_EPS = 1e-5
ATT_SCALE = HD ** -0.5
TM_FWD = 512
TM_BWD = 256
TM_WGRAD = 1024

ADAM_LR = 0.001
ADAM_B1 = 0.9
ADAM_B2 = 0.999
ADAM_EPS = 1e-08
ADAM_WD = 0.01
ADAM_STEP = 10

XQ, XO, XK, XV = 0, 1, 2, 3
SP_ROWS = 48
SA_ROWS = 64

MESH = pl.DeviceIdType.MESH
ANY = pl.BlockSpec(memory_space=pl.ANY)


def _dot(a, b):
    return jnp.dot(a, b, preferred_element_type=F32)


def _dot_nt(a, b):
    return lax.dot_general(a, b, (((1,), (1,)), ((), ())), preferred_element_type=F32)


def _dot_tn(a, b):
    return lax.dot_general(a, b, (((0,), (0,)), ((), ())), preferred_element_type=F32)


def _rms(h):
    r = lax.rsqrt(jnp.mean(h * h, axis=-1, keepdims=True) + RMS_EPS)
    return h * r, r


def _rms_bwd(dn, hhat, r, g):
    dhh = dn * g
    return r * (dhh - hhat * jnp.mean(dhh * hhat, axis=-1, keepdims=True))


def _ln(x):
    mu = jnp.mean(x, axis=-1, keepdims=True)
    xc = x - mu
    rstd = lax.rsqrt(jnp.mean(xc * xc, axis=-1, keepdims=True) + LN_EPS)
    return xc * rstd, rstd


def _ln_bwd(dy, y, rstd):
    return rstd * (dy - jnp.mean(dy, axis=-1, keepdims=True) - y * jnp.mean(dy * y, axis=-1, keepdims=True))


def _colsum(x):
    return jnp.sum(x, axis=0, keepdims=True)


def _sigmoid(x):
    return 1.0 / (1.0 + jnp.exp(-x))


_GELU_C = 0.7978845608028654
_GELU_A = 0.044715


def _gelu(x):
    t = jnp.tanh(_GELU_C * (x + _GELU_A * x * x * x))
    return 0.5 * x * (1.0 + t)


def _gelu_grad(x):
    t = jnp.tanh(_GELU_C * (x + _GELU_A * x * x * x))
    return 0.5 * (1.0 + t) + 0.5 * x * (1.0 - t * t) * _GELU_C * (1.0 + 3.0 * _GELU_A * x * x)


def _first(i):
    return i == 0


def _last(i):
    return i == pl.num_programs(0) - 1


def _const(shape, idx):
    return pl.BlockSpec(shape, lambda *_, _idx=tuple(idx): _idx, pipeline_mode=pl.Buffered(1))


def _whole(a):
    return _const(a.shape, (0,) * a.ndim)


def _rows(tm, width, col=0):
    return pl.BlockSpec((tm, width), lambda i, _c=col: (i, _c))


def _params(ndim=1, vmem_mb=56):
    return pltpu.CompilerParams(dimension_semantics=("arbitrary",) * ndim, vmem_limit_bytes=vmem_mb << 20)


def _sds(shape, dtype):
    return jax.ShapeDtypeStruct(shape, dtype)


def _pos():
    return lax.axis_index("x"), lax.axis_index("y"), lax.axis_index("c")


def _other_chips(x, y):
    return [(1 - x, y), (x, 1 - y), (1 - x, 1 - y)]


def _half(ref, hc, lead=()):
    r = ref.shape[-2] // 2
    return ref.at[(*lead, pl.ds(pl.multiple_of(hc * r, 8), r), slice(None))]


def _rcopy(src, dst, ssem, rsem, to):
    return pltpu.make_async_remote_copy(src_ref=src, dst_ref=dst, send_sem=ssem, recv_sem=rsem,
                                        device_id=to, device_id_type=MESH)


def _gather_start(bufs, sems):
    ici_s, ici_r, _, _ = sems
    x, y, c = _pos()
    me = 2 * x + y
    for k, (cx, cy) in enumerate(_other_chips(x, y)):
        for t, buf in enumerate(bufs):
            mine = _half(buf, c, (me,))
            _rcopy(mine, mine, ici_s.at[3 * t + k], ici_r.at[3 * t + k], (cx, cy, c)).start()


def _gather_forward(bufs, sems):
    ici_s, ici_r, d2d_s, d2d_r = sems
    x, y, c = _pos()
    sib = (x, y, 1 - c)
    for k, (cx, cy) in enumerate(_other_chips(x, y)):
        for t, buf in enumerate(bufs):
            landed = _half(buf, c, (2 * cx + cy,))
            _rcopy(landed, landed, ici_s.at[3 * t + k], ici_r.at[3 * t + k], (cx, cy, c)).wait_recv()
            _rcopy(landed, landed, d2d_s.at[3 * t + k], d2d_r.at[3 * t + k], sib).start()


def _gather_drain(bufs, sems):
    ici_s, ici_r, d2d_s, d2d_r = sems
    x, y, c = _pos()
    me = 2 * x + y
    sib = (x, y, 1 - c)
    chips = _other_chips(x, y)
    for k, (cx, cy) in enumerate(chips):
        for t, buf in enumerate(bufs):
            got = _half(buf, 1 - c, (2 * cx + cy,))
            _rcopy(got, got, d2d_s.at[3 * t + k], d2d_r.at[3 * t + k], sib).wait_recv()
    for k, (cx, cy) in enumerate(chips):
        for t, buf in enumerate(bufs):
            mine = _half(buf, c, (me,))
            _rcopy(mine, mine, ici_s.at[3 * t + k], ici_r.at[3 * t + k], (cx, cy, c)).wait_send()
            landed = _half(buf, c, (2 * cx + cy,))
            _rcopy(landed, landed, d2d_s.at[3 * t + k], d2d_r.at[3 * t + k], sib).wait_send()


def _xchg_copies(srcs, dsts, nbig, sems):
    ssem, rsem = sems
    x, y, c = _pos()
    cps = []
    for k, (cx, cy) in enumerate(_other_chips(x, y)):
        for t, (src, dst) in enumerate(zip(srcs, dsts)):
            s = src.at[2 * cx + cy] if t < nbig else src
            cps.append(_rcopy(s, dst.at[k], ssem.at[3 * t + k], rsem.at[3 * t + k], (cx, cy, c)))
    return cps


def _gather_sems(n):
    return [pltpu.SemaphoreType.DMA((3 * n,))] * 4


def _xchg_sems(n):
    return [pltpu.SemaphoreType.DMA((3 * n,))] * 2


def _xchg_out_shapes(items):
    return [_sds((3,) + a.shape[-2:], a.dtype) for a in items]


def _swap_copies(srcs, dsts, sems):
    ssem, rsem = sems
    x, y, c = _pos()
    cps = []
    for t, (src, dst) in enumerate(zip(srcs, dsts)):
        lead = (slice(None),) * (len(src.shape) - 2)
        cps.append(_rcopy(_half(src, 1 - c, lead), dst, ssem.at[t], rsem.at[t], (x, y, 1 - c)))
    return cps


def _swap_out_shapes(items):
    return [_sds(a.shape[:-2] + (a.shape[-2] // 2, a.shape[-1]), a.dtype) for a in items]


def _call(body, *, name, grid, in_specs, out_specs, out_shape, args, scratch_shapes=(), aliases=None, carry=None):
    in_specs, out_specs, out_shape = list(in_specs), list(out_specs), list(out_shape)
    args = list(args)
    aliases = dict(aliases or {})
    n_in, n_out, n_scr = len(args), len(out_shape), len(scratch_shapes)
    carries = [] if not carry else ([carry] if isinstance(carry, tuple) else list(carry))
    plans = []
    cin, cout_shape, sems = [], [], []
    for c in carries:
        kind = c[0]
        items = list(c[1]) + (list(c[2]) if kind == "xchg" else [])
        if kind == "gather":
            shapes, sm = [_sds(a.shape, a.dtype) for a in items], _gather_sems(len(items))
            for t in range(len(items)):
                aliases[n_in + len(cin) + t] = n_out + len(cin) + t
        elif kind == "xchg":
            shapes, sm = _xchg_out_shapes(items), _xchg_sems(len(items))
        else:
            shapes, sm = _swap_out_shapes(items), [pltpu.SemaphoreType.DMA((len(items),))] * 2
        plans.append((kind, len(cin), len(items), len(sems), len(sm), len(c[1])))
        cin += items
        cout_shape += shapes
        sems += sm
    nc = len(cin)

    def wrapped(*refs):
        ins = refs[:n_in]
        c_in = refs[n_in:n_in + nc]
        outs = refs[n_in + nc:n_in + nc + n_out]
        c_out = refs[n_in + nc + n_out:n_in + 2 * nc + n_out]
        scr = refs[n_in + 2 * nc + n_out:n_in + 2 * nc + n_out + n_scr]
        sem = refs[n_in + 2 * nc + n_out + n_scr:]
        i = pl.program_id(0)

        def copies(plan):
            kind, o0, no, s0, ns, nbig = plan
            if kind == "xchg":
                return _xchg_copies(c_in[o0:o0 + no], c_out[o0:o0 + no], nbig, sem[s0:s0 + ns])
            return _swap_copies(c_in[o0:o0 + no], c_out[o0:o0 + no], sem[s0:s0 + ns])

        if plans:
            @pl.when(_first(i))
            def _():
                for plan in plans:
                    kind, o0, no, s0, ns, _ = plan
                    if kind == "gather":
                        _gather_start(c_out[o0:o0 + no], sem[s0:s0 + ns])
                    else:
                        for cp in copies(plan):
                            cp.start()
        body(*ins, *outs, *scr)
        for plan in plans:
            kind, o0, no, s0, ns, _ = plan
            if kind == "gather":
                @pl.when(i == max(grid[0] - 3, 0))
                def _(o0=o0, no=no, s0=s0, ns=ns):
                    _gather_forward(c_out[o0:o0 + no], sem[s0:s0 + ns])
        if plans:
            @pl.when(_last(i))
            def _():
                for plan in plans:
                    kind, o0, no, s0, ns, _ = plan
                    if kind == "gather":
                        _gather_drain(c_out[o0:o0 + no], sem[s0:s0 + ns])
                    else:
                        for cp in copies(plan):
                            cp.wait()

    outs = pl.pallas_call(
        wrapped, name=name, grid=grid,
        in_specs=in_specs + [ANY] * nc, out_specs=out_specs + [ANY] * nc,
        out_shape=out_shape + cout_shape,
        scratch_shapes=list(scratch_shapes) + sems,
        input_output_aliases=aliases,
        compiler_params=_params(len(grid)),
    )(*args, *cin)
    return list(outs)


def _sibling_swap(items, name):
    n = len(items)

    def body(*refs):
        src = refs[:n]
        dst = refs[n:2 * n]
        ssem, rsem = refs[2 * n:]
        x, y, c = _pos()
        sib = (x, y, 1 - c)
        cps = []
        for t in range(n):
            lead = (slice(None),) * (len(src[t].shape) - 2)
            cp = _rcopy(_half(src[t], 1 - c, lead), dst[t], ssem.at[t], rsem.at[t], sib)
            cp.start()
            cps.append(cp)
        for cp in cps:
            cp.wait()

    return pl.pallas_call(
        body, name=name, in_specs=[ANY] * n, out_specs=[ANY] * n,
        out_shape=[_sds(a.shape[:-2] + (a.shape[-2] // 2, a.shape[-1]), a.dtype) for a in items],
        scratch_shapes=[pltpu.SemaphoreType.DMA((n,)), pltpu.SemaphoreType.DMA((n,))],
    )(*items)


_HBM = pl.BlockSpec(memory_space=pltpu.HBM)
_SEM = pl.BlockSpec(memory_space=pltpu.SEMAPHORE)
_EFFECT = pltpu.SideEffectType.DATAFLOW_SIDE_EFFECTING


def _split_copies(src, land, nbig, nx, sems):
    return (_xchg_copies(src[:nx], land[:nx], nbig, sems[0:2]) if nx else []) + \
        (_swap_copies(src[nx:], land[nx:], sems[2:4]) if len(src) > nx else [])


def _comm_start(big, small, swaps, tag):
    xs = list(big) + list(small)
    items = xs + list(swaps)
    n, nx = len(items), len(xs)

    def body(*refs):
        for cp in _split_copies(refs[:n], refs[n:2 * n], len(big), nx, refs[2 * n:2 * n + 4]):
            cp.start()
        refs[-1][...] = jnp.zeros_like(refs[-1])

    lands = _xchg_out_shapes(xs) + _swap_out_shapes(swaps)
    sem_shapes = [pltpu.SemaphoreType.DMA((max(3 * nx, 1),))] * 2 + [pltpu.SemaphoreType.DMA((max(n - nx, 1),))] * 2
    hbm = lambda a: pltpu.with_memory_space_constraint(a, pltpu.HBM)
    outs = pl.pallas_call(
        body, name=f"rs_start_{tag}",
        out_shape=sem_shapes + [pltpu.HBM(a.shape, a.dtype) for a in items]
        + [pltpu.HBM(a.shape, a.dtype) for a in lands] + [_sds((8, 128), F32)],
        in_specs=[_HBM] * (2 * n), out_specs=[_SEM] * 4 + [_HBM] * (2 * n) + [pl.BlockSpec(memory_space=pltpu.VMEM)],
        input_output_aliases={t: 4 + t for t in range(2 * n)},
        compiler_params=pltpu.CompilerParams(has_side_effects=_EFFECT),
    )(*[hbm(a) for a in items], *[hbm(lax.empty(a.shape, a.dtype)) for a in lands])
    handle = (list(outs[:4]), list(outs[4:4 + n]), list(outs[4 + n:4 + 2 * n]), len(big), nx, tag)
    return handle, outs[-1]


def _comm_wait(handle, after):
    sems, srcs, lands, nbig, nx, tag = handle
    n = len(srcs)

    def body(*refs):
        for cp in _split_copies(refs[:n], refs[n:2 * n], nbig, nx, refs[2 * n:2 * n + 4]):
            cp.wait_send()
            cp.wait_recv()

    outs = pl.pallas_call(
        body, name=f"rs_wait_{tag}",
        out_shape=[pltpu.HBM(a.shape, a.dtype) for a in srcs] + [pltpu.HBM(a.shape, a.dtype) for a in lands],
        in_specs=[_HBM] * (2 * n) + [_SEM] * 4 + [ANY], out_specs=[_HBM] * (2 * n),
        input_output_aliases={t: t for t in range(2 * n)},
        compiler_params=pltpu.CompilerParams(has_side_effects=_EFFECT),
    )(*srcs, *lands, *sems, after)
    return list(outs[:n]), list(outs[n:])


def _join_copies(items, sems):
    ssem, rsem = sems
    x, y, c = _pos()
    return [_rcopy(_half(a, c), _half(a, c), ssem.at[t], rsem.at[t], (x, y, 1 - c)) for t, a in enumerate(items)]


def _join_start(items, tag, after=()):
    n = len(items)

    def body(*refs):
        for cp in _join_copies(refs[:n], refs[n + len(after):n + len(after) + 2]):
            cp.start()
        refs[-1][...] = jnp.zeros_like(refs[-1])

    hbm = lambda a: pltpu.with_memory_space_constraint(a, pltpu.HBM)
    outs = pl.pallas_call(
        body, name=f"rs_join_start_{tag}",
        out_shape=[pltpu.SemaphoreType.DMA((n,))] * 2 + [pltpu.HBM(a.shape, a.dtype) for a in items]
        + [_sds((8, 128), F32)],
        in_specs=[_HBM] * n + [ANY] * len(after),
        out_specs=[_SEM] * 2 + [_HBM] * n + [pl.BlockSpec(memory_space=pltpu.VMEM)],
        input_output_aliases={t: 2 + t for t in range(n)},
        compiler_params=pltpu.CompilerParams(has_side_effects=_EFFECT),
    )(*[hbm(a) for a in items], *after)
    return (list(outs[:2]), list(outs[2:2 + n]), tag), outs[-1]


def _join_wait(handle, after):
    sems, items, tag = handle
    n = len(items)

    def body(*refs):
        x, y, c = _pos()
        ssem, rsem = refs[n:n + 2]
        for t, cp in enumerate(_join_copies(refs[:n], (ssem, rsem))):
            cp.wait_send()
            theirs = _half(refs[t], 1 - c)
            _rcopy(theirs, theirs, ssem.at[t], rsem.at[t], (x, y, 1 - c)).wait_recv()

    outs = pl.pallas_call(
        body, name=f"rs_join_wait_{tag}",
        out_shape=[pltpu.HBM(a.shape, a.dtype) for a in items],
        in_specs=[_HBM] * n + [_SEM] * 2 + [ANY], out_specs=[_HBM] * n,
        input_output_aliases={t: t for t in range(n)},
        compiler_params=pltpu.CompilerParams(has_side_effects=_EFFECT),
    )(*items, *sems, after)
    return list(outs)


def _sibling_join(items, name, after=()):
    n = len(items)

    def body(*refs):
        dst = refs[n + len(after):2 * n + len(after)]
        ssem, rsem = refs[2 * n + len(after):]
        x, y, c = _pos()
        sib = (x, y, 1 - c)
        cps = []
        for t in range(n):
            mine = _half(dst[t], c)
            cp = _rcopy(mine, mine, ssem.at[t], rsem.at[t], sib)
            cp.start()
            cps.append(cp)
        for t, cp in enumerate(cps):
            theirs = _half(dst[t], 1 - c)
            _rcopy(theirs, theirs, ssem.at[t], rsem.at[t], sib).wait_recv()
            cp.wait_send()

    outs = pl.pallas_call(
        body, name=name, in_specs=[ANY] * (n + len(after)), out_specs=[ANY] * n,
        out_shape=[_sds(a.shape, a.dtype) for a in items],
        input_output_aliases={t: t for t in range(n)},
        scratch_shapes=[pltpu.SemaphoreType.DMA((n,))] * 2,
    )(*items, *after)
    return list(outs)


def _ev_in_fwd(h, g, wev, tm, carry=None):
    s = h.shape[0]

    def body(h_ref, g_ref, w_ref, z_ref):
        hhat, _ = _rms(h_ref[...])
        n = (hhat * g_ref[...]).astype(BF)
        z_ref[...] = _dot(n, w_ref[...]).astype(BF)

    return _call(
        body, name="ev_in_fwd", grid=(s // tm,),
        in_specs=[_rows(tm, D), _const((1, D), (0, 0)), _const((D, DZ), (0, 0))],
        out_specs=[_rows(tm, DZ)], out_shape=[_sds((s, DZ), BF)],
        args=(h, g, wev), carry=carry)


def _halo_specs(tm, width, s, col=0):
    per = tm // HALO
    nh = s // HALO
    prev = pl.BlockSpec((HALO, width), lambda i, _c=col: (jnp.maximum(i * per - 1, 0), _c))
    nxt = pl.BlockSpec((HALO, width), lambda i, _c=col: (jnp.minimum((i + 1) * per, nh - 1), _c))
    return prev, nxt


def _fill_pad(pad_ref, prev, main, nxt, i, tm):
    pad_ref[pl.ds(0, HALO), :] = jnp.where(_first(i), 0.0, prev)
    pad_ref[pl.ds(HALO, tm), :] = main
    pad_ref[pl.ds(HALO + tm, HALO), :] = jnp.where(_last(i), 0.0, nxt)


def _shift8(xs_ref, pad_ref, tm):
    for j in range(8):
        xs_ref[j] = pad_ref[pl.ds(j, tm + 2 * HALO - 8), :]


def _tap(xs_ref, r, off):
    return xs_ref[off % 8, pl.ds(r + 8 * (off // 8), RC), :]


def _glu_a(z):
    return z[:, :DA] * _sigmoid(z[:, DA:2 * DA])


def _gate_c(z):
    return z[:, 2048:] * z[:, 1024:1536]


def _ev_mix_fwd(h, z, wa, ba, lng, lnb, wb, bb, evo, tm, carry=None):
    s = h.shape[0]

    def body(h_ref, z_ref, zp_ref, zn_ref, wa_ref, ba_ref, lng_ref, lnb_ref, wb_ref, bb_ref, wo_ref,
             h1_ref, a2_ref, c2_ref, cat_ref, apad, cpad, c2f, xs):
        i = pl.program_id(0)
        zm = z_ref[...].astype(F32)
        zp = zp_ref[...].astype(F32)
        zn = zn_ref[...].astype(F32)
        _fill_pad(apad, _glu_a(zp), _glu_a(zm), _glu_a(zn), i, tm)
        _fill_pad(cpad, _gate_c(zp), _gate_c(zm), _gate_c(zn), i, tm)
        _shift8(xs, apad, tm)
        for r in range(0, tm, RC):
            acc = jnp.zeros((RC, DA), F32) + ba_ref[...]
            for k in range(KA):
                acc = acc + _tap(xs, r, k + 1) * wa_ref[pl.ds(k, 1), :]
            a2_ref[pl.ds(r, RC), :] = acc
            acc = jnp.zeros((RC, DA), F32) + bb_ref[...]
            for k in range(KB):
                acc = acc + cpad[pl.ds(r + HALO - 1 + k, RC), :] * wb_ref[pl.ds(k, 1), :]
            c2f[pl.ds(r, RC), :] = acc
        y, _ = _ln(a2_ref[...])
        a3 = y * lng_ref[...] + lnb_ref[...]
        cat_ref[:, :DA] = (a3 * _sigmoid(a3)).astype(BF)
        c2 = c2f[...]
        c2_ref[...] = c2.astype(BF)
        cat_ref[:, DA:] = (zm[:, 1536:2048] * c2).astype(BF)
        h1_ref[...] = h_ref[...] + _dot(cat_ref[...], wo_ref[...].reshape(D, D))

    zp_spec, zn_spec = _halo_specs(tm, DZ, s)
    small = lambda rows: _const((rows, DA), (0, 0))
    return _call(
        body, name="ev_mix_fwd", grid=(s // tm,),
        in_specs=[_rows(tm, D), _rows(tm, DZ), zp_spec, zn_spec, small(32), small(1), small(1), small(1),
                  small(8), small(1), _whole(evo)],
        out_specs=[_rows(tm, D), _rows(tm, DA), _rows(tm, DA), _rows(tm, D)],
        out_shape=[_sds((s, D), F32), _sds((s, DA), F32), _sds((s, DA), BF), _sds((s, D), BF)],
        scratch_shapes=[pltpu.VMEM((tm + 2 * HALO, DA), F32), pltpu.VMEM((tm + 2 * HALO, DA), F32),
                        pltpu.VMEM((tm, DA), F32), pltpu.VMEM((8, tm + 2 * HALO - 8, DA), F32)],
        args=(h, z, z, z, wa, ba, lng, lnb, wb, bb, evo), carry=carry)


def _ev_mix_bwd(dh, cat, a2, c2, z, lng, lnb, evo, tm, carry=None):
    s = dh.shape[0]

    def body(dh_ref, cat_ref, a2_ref, c2_ref, zgb_ref, lng_ref, lnb_ref, wo_ref,
             da2_ref, dc2_ref, dgb_ref, dlng_ref, dlnb_ref, gwo_ref, wacc):
        i = pl.program_id(0)

        @pl.when(_first(i))
        def _():
            wacc[...] = jnp.zeros_like(wacc)
            dlng_ref[...] = jnp.zeros_like(dlng_ref)
            dlnb_ref[...] = jnp.zeros_like(dlnb_ref)

        dhb = dh_ref[...].astype(BF)
        dcat = _dot_nt(dhb, wo_ref[...].reshape(D, D))
        wacc[...] += _dot_tn(cat_ref[...], dhb)
        y, rstd = _ln(a2_ref[...])
        a3 = y * lng_ref[...] + lnb_ref[...]
        sg = _sigmoid(a3)
        da3 = dcat[:, :DA] * (sg * (1.0 + a3 * (1.0 - sg)))
        dlng_ref[...] += _colsum(da3 * y)
        dlnb_ref[...] += _colsum(da3)
        da2_ref[...] = _ln_bwd(da3 * lng_ref[...], y, rstd).astype(BF)
        db = dcat[:, DA:]
        dc2_ref[...] = (db * zgb_ref[...].astype(F32)).astype(BF)
        dgb_ref[...] = (db * c2_ref[...].astype(F32)).astype(BF)

        @pl.when(_last(i))
        def _():
            gwo_ref[...] = wacc[...].astype(BF).reshape(NSH, 256, D)

    small = _const((1, DA), (0, 0))
    return _call(
        body, name="ev_mix_bwd", grid=(s // tm,),
        in_specs=[_rows(tm, D), _rows(tm, D), _rows(tm, DA), _rows(tm, DA), _rows(tm, DA, 3), small, small,
                  _whole(evo)],
        out_specs=[_rows(tm, DA), _rows(tm, DA), _rows(tm, DA), small, small, _whole(evo)],
        out_shape=[_sds((s, DA), BF), _sds((s, DA), BF), _sds((s, DA), BF), _sds((1, DA), F32),
                   _sds((1, DA), F32), _sds(evo.shape, BF)],
        scratch_shapes=[pltpu.VMEM((D, D), F32)],
        args=(dh, cat, a2, c2, z, lng, lnb, evo), carry=carry)


def _ev_in_bwd(h, dh, z, da2, dc2, dgb, g, wa, wb, wev, tm, after):
    s = h.shape[0]

    def body(h_ref, dh_ref, z_ref, zp_ref, zn_ref, da_ref, dap_ref, dan_ref, dc_ref, dcp_ref, dcn_ref,
             dgb_ref, g_ref, wa_ref, wb_ref, w_ref, after_ref,
             dh0_ref, dwa_ref, dba_ref, dwb_ref, dbb_ref, dg_ref, gw_ref,
             apad, cpad, dapad, dcpad, dz, wacc, dwa_acc, dwb_acc, xs):
        i = pl.program_id(0)

        @pl.when(_first(i))
        def _():
            wacc[...] = jnp.zeros_like(wacc)
            dwa_acc[...] = jnp.zeros_like(dwa_acc)
            dwb_acc[...] = jnp.zeros_like(dwb_acc)
            dba_ref[...] = jnp.zeros_like(dba_ref)
            dbb_ref[...] = jnp.zeros_like(dbb_ref)
            dg_ref[...] = jnp.zeros_like(dg_ref)

        zm = z_ref[...].astype(F32)
        zp = zp_ref[...].astype(F32)
        zn = zn_ref[...].astype(F32)
        _fill_pad(apad, _glu_a(zp), _glu_a(zm), _glu_a(zn), i, tm)
        _fill_pad(cpad, _gate_c(zp), _gate_c(zm), _gate_c(zn), i, tm)
        da2 = da_ref[...].astype(F32)
        dc2 = dc_ref[...].astype(F32)
        _fill_pad(dapad, dap_ref[...].astype(F32), da2, dan_ref[...].astype(F32), i, tm)
        _fill_pad(dcpad, dcp_ref[...].astype(F32), dc2, dcn_ref[...].astype(F32), i, tm)
        dba_ref[...] += _colsum(da2)
        dbb_ref[...] += _colsum(dc2)

        sg = _sigmoid(zm[:, DA:2 * DA])
        aval = zm[:, :DA]
        _shift8(xs, dapad, tm)
        for r in range(0, tm, RC):
            acc = jnp.zeros((RC, DA), F32)
            for k in range(KA):
                acc = acc + _tap(xs, r, 2 * HALO - 1 - k) * wa_ref[pl.ds(k, 1), :]
            sgr = sg[r:r + RC]
            dz[pl.ds(r, RC), 0:DA] = (acc * sgr).astype(BF)
            dz[pl.ds(r, RC), DA:2 * DA] = (acc * aval[r:r + RC] * sgr * (1.0 - sgr)).astype(BF)
            acc = jnp.zeros((RC, DA), F32)
            for k in range(KB):
                acc = acc + dcpad[pl.ds(r + HALO + 1 - k, RC), :] * wb_ref[pl.ds(k, 1), :]
            dz[pl.ds(r, RC), 1024:1536] = (acc * zm[r:r + RC, 2048:]).astype(BF)
            dz[pl.ds(r, RC), 2048:2560] = (acc * zm[r:r + RC, 1024:1536]).astype(BF)
        dz[:, 1536:2048] = dgb_ref[...]

        _shift8(xs, apad, tm)
        for k in range(KA):
            acc = jnp.zeros((8, DA), F32)
            for r in range(0, tm, RC):
                p = _tap(xs, r, k + 1) * dapad[pl.ds(r + HALO, RC), :]
                acc = acc + p.reshape(RC // 8, 8, DA).sum(axis=0)
            dwa_acc[k] += acc
        for k in range(KB):
            acc = jnp.zeros((8, DA), F32)
            for r in range(0, tm, RC):
                p = cpad[pl.ds(r + HALO - 1 + k, RC), :] * dcpad[pl.ds(r + HALO, RC), :]
                acc = acc + p.reshape(RC // 8, 8, DA).sum(axis=0)
            dwb_acc[k] += acc

        gg = g_ref[...]
        hhat, rr = _rms(h_ref[...])
        n = (hhat * gg).astype(BF)
        dzb = dz[...]
        wacc[...] += _dot_tn(n, dzb)
        dn = _dot_nt(dzb, w_ref[...])
        dg_ref[...] += _colsum(dn * hhat)
        dh0_ref[...] = dh_ref[...] + _rms_bwd(dn, hhat, rr, gg)

        @pl.when(_last(i))
        def _():
            for k in range(NSH):
                gw_ref[k] = wacc[:, 640 * k:640 * (k + 1)].astype(BF)
            dwa_ref[...] = jnp.sum(dwa_acc[...], axis=1)
            dwb_ref[...] = jnp.sum(dwb_acc[...], axis=1)

    zp_spec, zn_spec = _halo_specs(tm, DZ, s)
    hp_spec, hn_spec = _halo_specs(tm, DA, s)
    small = lambda rows: _const((rows, DA), (0, 0))
    return _call(
        body, name="ev_in_bwd", grid=(s // tm,),
        in_specs=[_rows(tm, D), _rows(tm, D), _rows(tm, DZ), zp_spec, zn_spec,
                  _rows(tm, DA), hp_spec, hn_spec, _rows(tm, DA), hp_spec, hn_spec, _rows(tm, DA),
                  _const((1, D), (0, 0)), small(32), small(8), _const((D, DZ), (0, 0)), ANY],
        out_specs=[_rows(tm, D), small(32), small(1), small(8), small(1), _const((1, D), (0, 0)),
                   _const((NSH, D, 640), (0, 0, 0))],
        out_shape=[_sds((s, D), F32), _sds((32, DA), F32), _sds((1, DA), F32), _sds((8, DA), F32),
                   _sds((1, DA), F32), _sds((1, D), F32), _sds((NSH, D, 640), BF)],
        scratch_shapes=[pltpu.VMEM((tm + 2 * HALO, DA), F32)] * 4 + [
            pltpu.VMEM((tm, DZ), BF), pltpu.VMEM((D, DZ), F32),
            pltpu.VMEM((32, 8, DA), F32), pltpu.VMEM((8, 8, DA), F32),
            pltpu.VMEM((8, tm + 2 * HALO - 8, DA), F32)],
        args=(h, dh, z, z, z, da2, da2, da2, dc2, dc2, dc2, dgb, g, wa, wb, wev, after))


def _xa_spec(item):
    return _const((NSH, 256, D), (0, item, 0))


def _kv_proj(mem, g, xa, layer):
    def body(mem_ref, g_ref, wk_ref, wv_ref, k_ref, v_ref):
        mhat, _ = _rms(mem_ref[...])
        mn = (mhat * g_ref[...]).astype(BF)
        k_ref[...] = _dot(mn, wk_ref[...].reshape(D, D)).astype(BF)
        v_ref[...] = _dot(mn, wv_ref[...].reshape(D, D)).astype(BF)

    full = _const((NMEM, D), (0, 0))
    return _call(
        body, name=f"kv_proj{layer}", grid=(1,),
        in_specs=[full, _const((1, D), (0, 0)), _xa_spec(XK), _xa_spec(XV)],
        out_specs=[full, full], out_shape=[_sds((NMEM, D), BF), _sds((NMEM, D), BF)],
        args=(mem, g, xa, xa))


def _softmax_rows(sc):
    m = jnp.max(sc, axis=-1, keepdims=True)
    e = jnp.exp(sc - m)
    return e / jnp.sum(e, axis=-1, keepdims=True)


def _xattn_fwd(h, g, k, v, xa, layer, tm, carry=None):
    s = h.shape[0]

    def body(h_ref, g_ref, k_ref, v_ref, wq_ref, wo_ref, ho_ref, q_ref, o_scr):
        hh = h_ref[...]
        hhat, _ = _rms(hh)
        n = (hhat * g_ref[...]).astype(BF)
        q = _dot(n, wq_ref[...].reshape(D, D)).astype(BF)
        q_ref[...] = q
        for hd in range(HEADS):
            cs = slice(HD * hd, HD * (hd + 1))
            p = _softmax_rows(_dot_nt(q[:, cs], k_ref[:, cs]) * ATT_SCALE)
            o_scr[:, cs] = _dot(p.astype(BF), v_ref[:, cs]).astype(BF)
        ho_ref[...] = hh + _dot(o_scr[...], wo_ref[...].reshape(D, D))

    kvs = _const((NMEM, D), (0, 0))
    return _call(
        body, name=f"xattn_fwd{layer}", grid=(s // tm,),
        in_specs=[_rows(tm, D), _const((1, D), (0, 0)), kvs, kvs, _xa_spec(XQ), _xa_spec(XO)],
        out_specs=[_rows(tm, D), _rows(tm, D)],
        out_shape=[_sds((s, D), F32), _sds((s, D), BF)],
        scratch_shapes=[pltpu.VMEM((tm, D), BF)],
        args=(h, g, k, v, xa, xa), carry=carry)


def _xattn_bwd(h, dh, q, g, k, v, xa, layer, tm, carry=None):
    s = h.shape[0]

    def body(h_ref, dh_ref, q_ref, g_ref, k_ref, v_ref, wq_ref, wo_ref,
             dhi_ref, dk_ref, dv_ref, dg_ref, gqo_ref, o_scr, dq_scr, qacc, oacc):
        i = pl.program_id(0)

        @pl.when(_first(i))
        def _():
            qacc[...] = jnp.zeros_like(qacc)
            oacc[...] = jnp.zeros_like(oacc)
            dk_ref[...] = jnp.zeros_like(dk_ref)
            dv_ref[...] = jnp.zeros_like(dv_ref)
            dg_ref[...] = jnp.zeros_like(dg_ref)

        gg = g_ref[...]
        dho = dh_ref[...]
        dhb = dho.astype(BF)
        q = q_ref[...]
        do = _dot_nt(dhb, wo_ref[...].reshape(D, D)).astype(BF)
        for hd in range(HEADS):
            cs = slice(HD * hd, HD * (hd + 1))
            kh = k_ref[:, cs]
            vh = v_ref[:, cs]
            p = _softmax_rows(_dot_nt(q[:, cs], kh) * ATT_SCALE)
            pb = p.astype(BF)
            o_scr[:, cs] = _dot(pb, vh).astype(BF)
            doh = do[:, cs]
            dp = _dot_nt(doh, vh)
            dv_ref[:, cs] += _dot_tn(pb, doh)
            ds = (p * (dp - jnp.sum(dp * p, axis=-1, keepdims=True)) * ATT_SCALE).astype(BF)
            dq_scr[:, cs] = _dot(ds, kh).astype(BF)
            dk_ref[:, cs] += _dot_tn(ds, q[:, cs])
        oacc[...] += _dot_tn(o_scr[...], dhb)
        hhat, rr = _rms(h_ref[...])
        n = (hhat * gg).astype(BF)
        dq = dq_scr[...]
        qacc[...] += _dot_tn(n, dq)
        dn = _dot_nt(dq, wq_ref[...].reshape(D, D))
        dg_ref[...] += _colsum(dn * hhat)
        dhi_ref[...] = dho + _rms_bwd(dn, hhat, rr, gg)

        @pl.when(_last(i))
        def _():
            gqo_ref[:, 0:256, :] = qacc[...].astype(BF).reshape(NSH, 256, D)
            gqo_ref[:, 256:512, :] = oacc[...].astype(BF).reshape(NSH, 256, D)

    kvs = _const((NMEM, D), (0, 0))
    return _call(
        body, name=f"xattn_bwd{layer}", grid=(s // tm,),
        in_specs=[_rows(tm, D), _rows(tm, D), _rows(tm, D), _const((1, D), (0, 0)), kvs, kvs,
                  _xa_spec(XQ), _xa_spec(XO)],
        out_specs=[_rows(tm, D), kvs, kvs, _const((1, D), (0, 0)), _const((NSH, 512, D), (0, 0, 0))],
        out_shape=[_sds((s, D), F32), _sds((NMEM, D), F32), _sds((NMEM, D), F32), _sds((1, D), F32),
                   _sds(xa.shape, BF)],
        scratch_shapes=[pltpu.VMEM((tm, D), BF), pltpu.VMEM((tm, D), BF), pltpu.VMEM((D, D), F32),
                        pltpu.VMEM((D, D), F32)],
        args=(h, dh, q, g, k, v, xa, xa), carry=carry)


def _kv_proj_bwd(mem, g, dk, dv, xa, gxa, layer):
    def body(mem_ref, g_ref, dk_ref, dv_ref, wk_ref, wv_ref, gxa_in, dg_ref, gkv_ref):
        mhat, _ = _rms(mem_ref[...])
        mn = (mhat * g_ref[...]).astype(BF)
        dkb = dk_ref[...].astype(BF)
        dvb = dv_ref[...].astype(BF)
        gkv_ref[:, 0:256, :] = _dot_tn(mn, dkb).astype(BF).reshape(NSH, 256, D)
        gkv_ref[:, 256:512, :] = _dot_tn(mn, dvb).astype(BF).reshape(NSH, 256, D)
        dmn = _dot_nt(dkb, wk_ref[...].reshape(D, D)) + _dot_nt(dvb, wv_ref[...].reshape(D, D))
        dg_ref[...] = _colsum(dmn * mhat)

    full = _const((NMEM, D), (0, 0))
    return _call(
        body, name=f"kv_proj_bwd{layer}", grid=(1,),
        in_specs=[full, _const((1, D), (0, 0)), full, full, _xa_spec(XK), _xa_spec(XV), ANY],
        out_specs=[_const((1, D), (0, 0)), _const((NSH, 512, D), (0, 1, 0))],
        out_shape=[_sds((1, D), F32), _sds(gxa.shape, BF)],
        aliases={6: 1},
        args=(mem, g, dk, dv, xa, xa, gxa))


def _ffn_block(h_ref, g_ref, wg_ref, wu_ref, wd_ref, gt_ref, up_ref):
    hh = h_ref[...]
    hhat, _ = _rms(hh)
    n = (hhat * g_ref[...]).astype(BF)
    out = hh
    for c0, c1 in FF_CHUNKS:
        gt = _dot_nt(n, wg_ref[c0:c1, :])
        up = _dot_nt(n, wu_ref[c0:c1, :])
        gt_ref[:, c0:c1] = gt.astype(BF)
        up_ref[:, c0:c1] = up.astype(BF)
        act = (gt * _sigmoid(gt) * up).astype(BF)
        out = out + _dot(act, wd_ref[c0:c1, :])
    return out


def _ffn_fwd(h, g, wg, wu, dn, layer, tm, carry=None):
    s = h.shape[0]
    wg, wu, dn = (a.reshape(DFF, D) for a in (wg, wu, dn))

    def body(h_ref, g_ref, wg_ref, wu_ref, wd_ref, ho_ref, gt_ref, up_ref):
        ho_ref[...] = _ffn_block(h_ref, g_ref, wg_ref, wu_ref, wd_ref, gt_ref, up_ref)

    return _call(
        body, name=f"ffn_fwd{layer}", grid=(s // tm,),
        in_specs=[_rows(tm, D), _const((1, D), (0, 0)), _whole(wg), _whole(wu), _whole(dn)],
        out_specs=[_rows(tm, D), _rows(tm, DFF), _rows(tm, DFF)],
        out_shape=[_sds((s, D), F32), _sds((s, DFF), BF), _sds((s, DFF), BF)],
        args=(h, g, wg, wu, dn), carry=carry)


def _ffn_fwd_loss(h, g, wg, wu, dn, tgt, g_final, layer, tm):
    s = h.shape[0]
    wg, wu, dn = (a.reshape(DFF, D) for a in (wg, wu, dn))

    def body(h_ref, g_ref, wg_ref, wu_ref, wd_ref, t_ref, gf_ref, dh_ref, loss_ref, dg_ref, gt_ref, up_ref):
        i = pl.program_id(0)

        @pl.when(_first(i))
        def _():
            loss_ref[...] = jnp.zeros_like(loss_ref)
            dg_ref[...] = jnp.zeros_like(dg_ref)

        out = _ffn_block(h_ref, g_ref, wg_ref, wu_ref, wd_ref, gt_ref, up_ref)
        gg = gf_ref[...]
        hhat, rr = _rms(out)
        diff = hhat * gg - t_ref[...]
        loss_ref[...] += jnp.sum(diff * diff) * (0.5 / D)
        dout = diff * (1.0 / D)
        dg_ref[...] += _colsum(dout * hhat)
        dh_ref[...] = _rms_bwd(dout, hhat, rr, gg)

    vec = _const((1, D), (0, 0))
    return _call(
        body, name=f"ffn_fwd_loss{layer}", grid=(s // tm,),
        in_specs=[_rows(tm, D), vec, _whole(wg), _whole(wu), _whole(dn), _rows(tm, D), vec],
        out_specs=[_rows(tm, D), _const((1, 128), (0, 0)), vec, _rows(tm, DFF), _rows(tm, DFF)],
        out_shape=[_sds((s, D), F32), _sds((1, 128), F32), _sds((1, D), F32), _sds((s, DFF), BF),
                   _sds((s, DFF), BF)],
        args=(h, g, wg, wu, dn, tgt, g_final))


def _ffn_bwd_dx(h, dh, gt, up, g, wg, wu, dn, layer, tm, carry=None):
    s = h.shape[0]
    wg, wu, dn = (a.reshape(DFF, D) for a in (wg, wu, dn))

    def body(h_ref, dh_ref, gt_ref, up_ref, g_ref, wg_ref, wu_ref, wd_ref,
             dhi_ref, dg_ref, dgt_ref, dup_ref, act_ref, n_ref):
        i = pl.program_id(0)

        @pl.when(_first(i))
        def _():
            dg_ref[...] = jnp.zeros_like(dg_ref)

        gg = g_ref[...]
        dho = dh_ref[...]
        dhb = dho.astype(BF)
        dn_ = jnp.zeros((tm, D), F32)
        for c0, c1 in FF_CHUNKS:
            gtv = gt_ref[:, c0:c1].astype(F32)
            upv = up_ref[:, c0:c1].astype(F32)
            sg = _sigmoid(gtv)
            silu = gtv * sg
            dact = _dot_nt(dhb, wd_ref[c0:c1, :])
            dgt = (dact * upv * (sg * (1.0 + gtv * (1.0 - sg)))).astype(BF)
            dup = (dact * silu).astype(BF)
            dgt_ref[:, c0:c1] = dgt
            dup_ref[:, c0:c1] = dup
            act_ref[:, c0:c1] = (silu * upv).astype(BF)
            dn_ = dn_ + _dot(dgt, wg_ref[c0:c1, :]) + _dot(dup, wu_ref[c0:c1, :])
        hhat, rr = _rms(h_ref[...])
        n_ref[...] = (hhat * gg).astype(BF)
        dg_ref[...] += _colsum(dn_ * hhat)
        dhi_ref[...] = dho + _rms_bwd(dn_, hhat, rr, gg)

    wide = _rows(tm, DFF)
    return _call(
        body, name=f"ffn_bwd_dx{layer}", grid=(s // tm,),
        in_specs=[_rows(tm, D), _rows(tm, D), wide, wide, _const((1, D), (0, 0)),
                  _whole(wg), _whole(wu), _whole(dn)],
        out_specs=[_rows(tm, D), _const((1, D), (0, 0)), wide, wide, wide, _rows(tm, D)],
        out_shape=[_sds((s, D), F32), _sds((1, D), F32), _sds((s, DFF), BF), _sds((s, DFF), BF),
                   _sds((s, DFF), BF), _sds((s, D), BF)],
        args=(h, dh, gt, up, g, wg, wu, dn), carry=carry)


def _ffn_wgrad(a, b, name, tm, slot, into=None, carry=None):
    s = a.shape[0]
    tm = min(tm, s)

    def body(a_ref, b_ref, *rest):
        o_ref, acc = rest[-2:]
        i = pl.program_id(0)

        @pl.when(_first(i))
        def _():
            acc[...] = jnp.zeros_like(acc)

        acc[...] += _dot_tn(a_ref[...], b_ref[...].astype(BF))

        @pl.when(_last(i))
        def _():
            o_ref[...] = acc[...].astype(BF).reshape(NSH, FSH, D)

    chained = into is not None
    return _call(
        body, name=name, grid=(s // tm,),
        in_specs=[_rows(tm, DFF), _rows(tm, D)] + [ANY] * chained,
        out_specs=[_const((NSH, FSH, D), (0, slot, 0))],
        out_shape=[_sds((NSH, 3 * FSH, D), BF)],
        scratch_shapes=[pltpu.VMEM((DFF, D), F32)],
        aliases={2: 0} if chained else None,
        args=(a, b) + ((into,) if chained else ()), carry=carry)


def _sgu_fwd(h, g, lng, lnb, ws, bsb, wod, odo, tm, carry=None):
    s = h.shape[0]

    def body(h_ref, g_ref, lng_ref, lnb_ref, ws_ref, bsb_ref, wi_ref, wo_ref, ho_ref, zp_ref, y_scr):
        hh = h_ref[...]
        hhat, _ = _rms(hh)
        n = (hhat * g_ref[...]).astype(BF)
        zpre = _dot(n, wi_ref[...])
        zp_ref[...] = zpre.astype(BF)
        z = _gelu(zpre)
        u = z[:, :D]
        y, _ = _ln(z[:, D:])
        vn = (y * lng_ref[...] + lnb_ref[...]).astype(BF)
        for c in range(tm // CHUNK):
            rs = slice(CHUNK * c, CHUNK * (c + 1))
            for gi in range(GROUPS):
                cs = slice(CHUNK * gi, CHUNK * (gi + 1))
                sv = _dot(ws_ref[gi], vn[rs, cs]) + bsb_ref[gi]
                y_scr[rs, cs] = (u[rs, cs] * sv).astype(BF)
        ho_ref[...] = hh + _dot(y_scr[...], wo_ref[...].reshape(D, D))

    vec = _const((1, D), (0, 0))
    sq = _const((GROUPS, CHUNK, CHUNK), (0, 0, 0))
    return _call(
        body, name="sgu_fwd", grid=(s // tm,),
        in_specs=[_rows(tm, D), vec, vec, vec, sq, sq, _const((D, 2 * D), (0, 0)), _whole(odo)],
        out_specs=[_rows(tm, D), _rows(tm, 2 * D)],
        out_shape=[_sds((s, D), F32), _sds((s, 2 * D), BF)],
        scratch_shapes=[pltpu.VMEM((tm, D), BF)],
        args=(h, g, lng, lnb, ws, bsb, wod, odo), carry=carry)


def _sgu_bwd(h, dh, zpre, g, lng, lnb, ws, wst, bsb, wod, odo, tm, carry=None):
    s = h.shape[0]

    def body(h_ref, dh_ref, zp_ref, g_ref, lng_ref, lnb_ref, ws_ref, wst_ref, bsb_ref, wi_ref, wo_ref,
             dhi_ref, dws_ref, dbs_ref, dlng_ref, dlnb_ref, dg_ref, gwi_ref, gwo_ref,
             y_scr, dz_scr, dvn_scr, iacc, oacc, bacc):
        i = pl.program_id(0)

        @pl.when(_first(i))
        def _():
            for ref in (iacc, oacc, bacc, dws_ref, dlng_ref, dlnb_ref, dg_ref):
                ref[...] = jnp.zeros_like(ref)

        gg = g_ref[...]
        dho = dh_ref[...]
        dhb = dho.astype(BF)
        zpre_v = zp_ref[...].astype(F32)
        z = _gelu(zpre_v)
        u = z[:, :D]
        yl, rstd = _ln(z[:, D:])
        vn = (yl * lng_ref[...] + lnb_ref[...]).astype(BF)
        dy = _dot_nt(dhb, wo_ref[...].reshape(D, D))
        for c in range(tm // CHUNK):
            rs = slice(CHUNK * c, CHUNK * (c + 1))
            for gi in range(GROUPS):
                cs = slice(CHUNK * gi, CHUNK * (gi + 1))
                vb = vn[rs, cs]
                sv = _dot(ws_ref[gi], vb) + bsb_ref[gi]
                ub = u[rs, cs]
                dyb = dy[rs, cs]
                y_scr[rs, cs] = (ub * sv).astype(BF)
                dz_scr[rs, cs] = dyb * sv
                dsv = dyb * ub
                bacc[gi] += dsv
                dsvb = dsv.astype(BF)
                dws_ref[gi] += _dot_nt(dsvb, vb)
                dvn_scr[rs, cs] = _dot(wst_ref[gi], dsvb)
        oacc[...] += _dot_tn(y_scr[...], dhb)
        dvn = dvn_scr[...]
        dlng_ref[...] += _colsum(dvn * yl)
        dlnb_ref[...] += _colsum(dvn)
        dz_scr[:, D:] = _ln_bwd(dvn * lng_ref[...], yl, rstd)
        dzb = (dz_scr[...] * _gelu_grad(zpre_v)).astype(BF)
        hhat, rr = _rms(h_ref[...])
        n = (hhat * gg).astype(BF)
        iacc[...] += _dot_tn(n, dzb)
        dn = _dot_nt(dzb, wi_ref[...])
        dg_ref[...] += _colsum(dn * hhat)
        dhi_ref[...] = dho + _rms_bwd(dn, hhat, rr, gg)

        @pl.when(_last(i))
        def _():
            for k in range(NSH):
                gwi_ref[k] = iacc[:, 512 * k:512 * (k + 1)].astype(BF)
            gwo_ref[...] = oacc[...].astype(BF).reshape(NSH, 256, D)
            dbs_ref[...] = jnp.sum(bacc[...], axis=-1)

    vec = _const((1, D), (0, 0))
    sq = _const((GROUPS, CHUNK, CHUNK), (0, 0, 0))
    return _call(
        body, name="sgu_bwd", grid=(s // tm,),
        in_specs=[_rows(tm, D), _rows(tm, D), _rows(tm, 2 * D), vec, vec, vec, sq, sq, sq,
                  _const((D, 2 * D), (0, 0)), _whole(odo)],
        out_specs=[_rows(tm, D), sq, _const((GROUPS, CHUNK), (0, 0)), vec, vec, vec,
                   _const((NSH, D, 512), (0, 0, 0)), _whole(odo)],
        out_shape=[_sds((s, D), F32), _sds((GROUPS, CHUNK, CHUNK), F32), _sds((GROUPS, CHUNK), F32),
                   _sds((1, D), F32), _sds((1, D), F32), _sds((1, D), F32), _sds((NSH, D, 512), BF),
                   _sds(odo.shape, BF)],
        scratch_shapes=[pltpu.VMEM((tm, D), BF), pltpu.VMEM((tm, 2 * D), F32),
                        pltpu.VMEM((tm, D), F32), pltpu.VMEM((D, 2 * D), F32), pltpu.VMEM((D, D), F32),
                        pltpu.VMEM((GROUPS, CHUNK, CHUNK), F32)],
        args=(h, dh, zpre, g, lng, lnb, ws, wst, bsb, wod, odo), carry=carry)


def _own_plane(shape):
    return pl.BlockSpec((1,) + shape, lambda i, chip_ref: (chip_ref[0], 0, 0), pipeline_mode=pl.Buffered(1))


def _pack_rows(chip_idx, w_down, w_gate_t, w_up_t, xq, xo, xk, xv, ev_out, od_out, first):
    nf = len(first)

    def body(chip_ref, wd_ref, wg_ref, wu_ref, xq_ref, xo_ref, xk_ref, xv_ref, ev_ref, od_ref, *rest):
        (evo_ref, odo_ref, dn0_ref, dn1_ref, wg0_ref, wg1_ref, wu0_ref, wu1_ref, xa0_ref,
         xa1_ref) = rest[nf:nf + 10]
        gathered = rest[nf + 10:2 * nf + 10]
        sems = rest[2 * nf + 10:]
        _gather_start(gathered, sems)
        evo_ref[0] = ev_ref[0].astype(BF)
        odo_ref[0] = od_ref[0].astype(BF)
        for src, outs in ((wd_ref, (dn0_ref, dn1_ref)), (wg_ref, (wg0_ref, wg1_ref)), (wu_ref, (wu0_ref, wu1_ref))):
            for layer, o_ref in enumerate(outs):
                o_ref[0] = src[layer].astype(BF)
        for layer, xa_ref in enumerate((xa0_ref, xa1_ref)):
            for item, src in ((XQ, xq_ref), (XO, xo_ref), (XK, xk_ref), (XV, xv_ref)):
                xa_ref[0, 256 * item:256 * (item + 1), :] = src[layer].astype(BF)
        _gather_forward(gathered, sems)
        _gather_drain(gathered, sems)

    ins = (w_down, w_gate_t, w_up_t, xq, xo, xk, xv, ev_out, od_out)
    shapes = [(256, D), (256, D)] + [(FSH, D)] * 6 + [(4 * 256, D), (4 * 256, D)]
    grid_spec = pltpu.PrefetchScalarGridSpec(
        num_scalar_prefetch=1, grid=(1,), in_specs=[_whole(a) for a in ins] + [ANY] * nf,
        out_specs=[_own_plane(sh) for sh in shapes] + [ANY] * nf,
        scratch_shapes=_gather_sems(nf))
    outs = pl.pallas_call(
        body, name="pack_rows", grid_spec=grid_spec,
        out_shape=[_sds((NSH,) + sh, BF) for sh in shapes] + [_sds(a.shape, a.dtype) for a in first],
        input_output_aliases={1 + len(ins) + t: len(shapes) + t for t in range(nf)},
        compiler_params=_params(),
    )(chip_idx, *ins, *first)
    return outs[:len(shapes)], outs[len(shapes):]


def _pack_cols(chip_idx, od_in, ev_in, conv_a, conv_b, ln_g, ln_b):
    def body(chip_ref, od_ref, ev_ref, ca_ref, cb_ref, lg_ref, lb_ref, cev_ref, cod_ref, sp_ref):
        cev_ref[0] = ev_ref[0].astype(BF)
        cod_ref[0] = od_ref[0].astype(BF)
        sp_ref[...] = jnp.zeros_like(sp_ref)
        sp_ref[0, 0:KA, 0:128] = ca_ref[0]
        sp_ref[0, 32:32 + KB, 0:128] = cb_ref[0]
        sp_ref[0, 40:41, :] = lg_ref[...]
        sp_ref[0, 41:42, :] = lb_ref[...]

    ins = (od_in, ev_in, conv_a, conv_b, ln_g, ln_b)
    shapes = [(D, 640), (D, 512)]
    grid_spec = pltpu.PrefetchScalarGridSpec(
        num_scalar_prefetch=1, grid=(1,), in_specs=[_whole(a) for a in ins],
        out_specs=[_own_plane(sh) for sh in shapes] + [_own_plane((SP_ROWS, 256))])
    return pl.pallas_call(
        body, name="pack_cols", grid_spec=grid_spec,
        out_shape=[_sds((NSH,) + sh, BF) for sh in shapes] + [_sds((NSH, SP_ROWS, 256), F32)],
        compiler_params=_params(),
    )(chip_idx, *ins)


def _assemble(piece, parts, name):
    tr = 256
    cols = piece.shape[2]

    def body(p_ref, *outs):
        for k in range(NSH):
            for (off, width), o_ref in zip(parts, outs):
                o_ref[:, width * k:width * (k + 1)] = p_ref[k, :, off:off + width]

    return _call(
        body, name=name, grid=(D // tr,),
        in_specs=[pl.BlockSpec((NSH, tr, cols), lambda i: (0, i, 0))],
        out_specs=[pl.BlockSpec((tr, NSH * width), lambda i: (i, 0)) for _, width in parts],
        out_shape=[_sds((D, NSH * width), BF) for _, width in parts],
        args=(piece,))


def _blocks_of(rows, cols, itemsize):
    rb = rows
    while rb * cols * itemsize > (2 << 20) and rb % 2 == 0 and (rb // 2) % 16 == 0:
        rb //= 2
    return rb


def _add_half(full, recv, c_idx, name):
    p, r, cols = full.shape
    rb = _blocks_of(r // 2, cols, 4)
    nb = (r // 2) // rb

    def body(c_ref, a_ref, b_ref, o_ref):
        o_ref[...] = (a_ref[...].astype(F32) + b_ref[...].astype(F32)).astype(o_ref.dtype)

    grid_spec = pltpu.PrefetchScalarGridSpec(
        num_scalar_prefetch=1, grid=(p, nb),
        in_specs=[pl.BlockSpec((1, rb, cols), lambda j, i, c_ref: (j, c_ref[0] * nb + i, 0)),
                  pl.BlockSpec((1, rb, cols), lambda j, i, c_ref: (j, i, 0))],
        out_specs=pl.BlockSpec((1, rb, cols), lambda j, i, c_ref: (j, i, 0)))
    return pl.pallas_call(
        body, name=name, grid_spec=grid_spec, out_shape=_sds(recv.shape, full.dtype),
        compiler_params=_params(2),
    )(c_idx, full, recv)


def _sum_chips(own, got, plane_c, name):
    _, r, cols = own.shape
    rb = _blocks_of(r, cols, 4)
    nb = r // rb

    def body(s_ref, a_ref, b_ref, o_ref):
        f = lambda v: v.astype(F32)
        o_ref[...] = (f(a_ref[0]) + f(b_ref[1])) + (f(b_ref[0]) + f(b_ref[2]))

    grid_spec = pltpu.PrefetchScalarGridSpec(
        num_scalar_prefetch=1, grid=(nb,),
        in_specs=[pl.BlockSpec((1, rb, cols), lambda i, s_ref: (s_ref[0], i, 0)),
                  pl.BlockSpec((3, rb, cols), lambda i, s_ref: (0, i, 0))],
        out_specs=pl.BlockSpec((rb, cols), lambda i, s_ref: (s_ref[1] * nb + i, 0)))
    return pl.pallas_call(
        body, name=name, grid_spec=grid_spec, out_shape=_sds((2 * r, cols), F32),
        compiler_params=_params(1),
    )(plane_c, own, got)


def _adamw_math(w, g, m, v):
    m = ADAM_B1 * m + (1.0 - ADAM_B1) * g
    v = ADAM_B2 * v + (1.0 - ADAM_B2) * (g * g)
    m_hat = m / (1.0 - ADAM_B1 ** ADAM_STEP)
    v_hat = v / (1.0 - ADAM_B2 ** ADAM_STEP)
    delta = -ADAM_LR * (m_hat / (jnp.sqrt(v_hat) + ADAM_EPS) + ADAM_WD * w)
    return delta, m, v


def _adamw_big(w, m, v, gsrcs, gblock, gidx, name, rb):
    nl, r, cols = w.shape
    assert len(gsrcs) == nl

    def body(w_ref, m_ref, v_ref, *rest):
        g_refs, (go_ref, d_ref, mo_ref, vo_ref) = rest[:nl], rest[nl:]
        g = g_refs[0][:, 0:cols]
        if nl == 2:
            g = jnp.where(pl.program_id(0) == 0, g, g_refs[1][:, 0:cols])
        delta, mn, vn = _adamw_math(w_ref[0], g, m_ref[0], v_ref[0])
        go_ref[0] = g
        d_ref[0] = delta
        mo_ref[0] = mn
        vo_ref[0] = vn

    wspec = pl.BlockSpec((1, rb, cols), lambda l, i: (l, i, 0))
    gspec = pl.BlockSpec(gblock, lambda l, i: gidx(i))
    return pl.pallas_call(
        body, name=name, grid=(nl, r // rb),
        in_specs=[wspec, wspec, wspec] + [gspec] * nl,
        out_specs=[wspec] * 4, out_shape=[_sds(w.shape, F32)] * 4,
        compiler_params=_params(2),
    )(w, m, v, *gsrcs)


_SMALL = (
    ("g_mix", (2, D), "a", (0, 2), 0, False),
    ("g_xattn", (2, D), "a", (2, 4), 0, False),
    ("g_mem", (2, D), "a", (4, 6), 0, False),
    ("g_ffn", (2, D), "a", (6, 8), 0, False),
    ("g_final", (1, D), "a", (8, 9), 0, False),
    ("ev_a_conv_w", (KA, 128), "a", (16, 16 + KA), 0, True),
    ("ev_a_conv_b", (1, DA), "a", (11, 12), 0, False),
    ("ev_a_ln_g", (1, DA), "a", (11, 12), DA, False),
    ("ev_a_ln_b", (1, DA), "a", (12, 13), 0, False),
    ("ev_b_conv_w", (KB, 128), "a", (48, 48 + KB), 0, True),
    ("ev_b_conv_b", (1, DA), "a", (12, 13), DA, False),
    ("od_c_ln_g", (1, 256), "a", (9, 10), 0, True),
    ("od_c_ln_b", (1, 256), "a", (10, 11), 0, True),
    ("od_w_s", (GROUPS * CHUNK, CHUNK), "b", (0, GROUPS * CHUNK), 0, False),
    ("od_b_s", (GROUPS, CHUNK), "a", (56, 64), 0, False),
)


def _adamw_small(ga, gb, wmv):
    ns = len(_SMALL)

    def body(*refs):
        ga_ref, gb_ref = refs[0], refs[1]
        ins = refs[2:2 + 3 * ns]
        outs = refs[2 + 3 * ns:]
        x, y, _ = _pos()
        chip = 2 * x + y
        for j, (_, shape, pack, (r0, r1), col, sharded) in enumerate(_SMALL):
            src = ga_ref if pack == "a" else gb_ref
            width = shape[1]
            if sharded:
                g = jnp.zeros(shape, F32)
                for k in range(NSH):
                    blk = src[r0:r1, col + width * k:col + width * (k + 1)]
                    g = g + jnp.where(chip == k, blk, 0.0)
            else:
                g = src[r0:r1, col:col + width]
            w_ref, m_ref, v_ref = ins[3 * j:3 * j + 3]
            delta, mn, vn = _adamw_math(w_ref[...], g, m_ref[...], v_ref[...])
            go_ref, d_ref, mo_ref, vo_ref = outs[4 * j:4 * j + 4]
            go_ref[...] = g
            d_ref[...] = delta
            mo_ref[...] = mn
            vo_ref[...] = vn

    vm = pl.BlockSpec(memory_space=pltpu.VMEM)
    flat = [a for trio in wmv for a in trio]
    out_shape = []
    for (_, shape, *_rest) in _SMALL:
        out_shape += [_sds(shape, F32)] * 4
    outs = pl.pallas_call(
        body, name="adamw_small",
        in_specs=[vm] * (2 + 3 * ns), out_specs=[vm] * (4 * ns), out_shape=out_shape,
    )(ga, gb, *flat)
    return [tuple(outs[4 * j:4 * j + 4]) for j in range(ns)]


def _small_params(gs, W):
    unshard = lambda a: jnp.transpose(a, (1, 0, 2)).reshape(a.shape[1], NSH * a.shape[2])
    od_w_s, od_b_s = W["od_w_s"], W["od_b_s"]
    return dict(
        g_mix=W["g_mix"], g_xattn=W["g_xattn"], g_mem=W["g_mem"], g_ffn=W["g_ffn"],
        g_final=W["g_final"].reshape(1, D),
        wa=unshard(gs[:, 0:32, 0:128]), ba=W["ev_a_conv_b"], lng_a=W["ev_a_ln_g"], lnb_a=W["ev_a_ln_b"],
        wb=unshard(gs[:, 32:40, 0:128]), bb=W["ev_b_conv_b"],
        lng_c=unshard(gs[:, 40:41, :]), lnb_c=unshard(gs[:, 41:42, :]),
        ws=od_w_s[0].astype(BF), wst=jnp.swapaxes(od_w_s[0], 1, 2).astype(BF),
        bsb=jnp.broadcast_to(od_b_s[0][:, :, None], (GROUPS, CHUNK, CHUNK)),
    )


LOSS_ROW = 13


def _small_grad_packs(g, loss):
    z = lambda r, c: jnp.zeros((r, c), F32)
    wide = lambda a: jnp.concatenate([a, z(a.shape[0], D - a.shape[1])], axis=1)
    rows = [g["g_mix"], g["g_xattn"], g["g_mem"], g["g_ffn"], g["g_final"], g["od_c_ln_g"], g["od_c_ln_b"],
            jnp.concatenate([g["ev_a_conv_b"], g["ev_a_ln_g"]], axis=1),
            jnp.concatenate([g["ev_a_ln_b"], g["ev_b_conv_b"]], axis=1),
            wide(loss), z(2, D), wide(g["ev_a_conv_w"]), wide(g["ev_b_conv_w"]), wide(g["od_b_s"])]
    a = jnp.concatenate(rows, axis=0)
    assert a.shape == (SA_ROWS, D), a.shape
    return a, g["od_w_s"].reshape(GROUPS * CHUNK, CHUNK)


def _step(x, mem, tgt, W, P, c_idx, tm_f, tm_b, tm_w):
    row = lambda a, i: a[i:i + 1]

    sm = _small_params(P["gs"], W)
    (wev,) = _assemble(P["cev"], [(0, 640)], "assemble_ev")
    z, evo, wg0 = _ev_in_fwd(x, row(sm["g_mix"], 0), wev, 2 * tm_f, carry=("gather", [P["evo"], P["wg0"]]))
    h1, a2, c2, cat, xa0, wu0 = _ev_mix_fwd(
        x, z, sm["wa"], sm["ba"], sm["lng_a"], sm["lnb_a"], sm["wb"], sm["bb"], evo, tm_f,
        carry=("gather", [P["xa0"], P["wu0"]]))
    k0, v0 = _kv_proj(mem, row(sm["g_mem"], 0), xa0, 0)
    h2, q0, dn0 = _xattn_fwd(h1, row(sm["g_xattn"], 0), k0, v0, xa0, 0, 2 * tm_f, carry=("gather", [P["dn0"]]))
    h3, gt0, up0, odo, xa1, wg1, cod = _ffn_fwd(h2, row(sm["g_ffn"], 0), wg0, wu0, dn0, 0, tm_f,
                                                carry=("gather", [P["odo"], P["xa1"], P["wg1"], P["cod"]]))
    (wod,) = _assemble(cod, [(0, 512)], "assemble_od")
    h4, zpre, wu1, dn1 = _sgu_fwd(h3, row(sm["g_mix"], 1), sm["lng_c"], sm["lnb_c"], sm["ws"], sm["bsb"], wod,
                                  odo, tm_f, carry=("gather", [P["wu1"], P["dn1"]]))
    k1, v1 = _kv_proj(mem, row(sm["g_mem"], 1), xa1, 1)
    h5, q1 = _xattn_fwd(h4, row(sm["g_xattn"], 1), k1, v1, xa1, 1, 2 * tm_f)
    dh6, loss, dg_final, gt1, up1 = _ffn_fwd_loss(h5, row(sm["g_ffn"], 1), wg1, wu1, dn1, tgt, sm["g_final"], 1,
                                                  tm_f)

    g = {"g_final": dg_final}
    parts, gots = {}, {}
    def add(name, piece, recv):
        parts[name] = _add_half(piece, recv, c_idx, f"rs_add_{name}")
        return parts[name]

    def ffn_wgrads(layer, dgt, dup, act, nb, dh_out, carry=None):
        g_ffn, *carried = _ffn_wgrad(dgt, nb, f"ffn_wgrad_gate{layer}", tm_w, 0, carry=carry)
        (g_ffn,) = _ffn_wgrad(dup, nb, f"ffn_wgrad_up{layer}", tm_w, 1, into=g_ffn)
        (g_ffn,) = _ffn_wgrad(act, dh_out, f"ffn_wgrad_down{layer}", tm_w, 2, into=g_ffn)
        return g_ffn, carried

    dh5, dg_ffn1, dgt, dup, act, nb = _ffn_bwd_dx(h5, dh6, gt1, up1, row(sm["g_ffn"], 1), wg1, wu1, dn1, 1, tm_b)
    g_ffn1, _ = ffn_wgrads(1, dgt, dup, act, nb, dh6)
    dh4, dk, dv, dg_xa1, g_xa1, recv = _xattn_bwd(
        h4, dh5, q1, row(sm["g_xattn"], 1), k1, v1, xa1, 1, tm_f, carry=("swap", [g_ffn1]))
    add("ffn1", g_ffn1, recv)
    dg_mem1, g_xa1 = _kv_proj_bwd(mem, row(sm["g_mem"], 1), dk, dv, xa1, g_xa1, 1)
    (dh3, g["od_w_s"], g["od_b_s"], g["od_c_ln_g"], g["od_c_ln_b"], dg_mix1, g_cod, g_odo, gots["ffn1"],
     recv) = _sgu_bwd(
        h3, dh4, zpre, row(sm["g_mix"], 1), sm["lng_c"], sm["lnb_c"], sm["ws"], sm["wst"], sm["bsb"], wod, odo, tm_b,
        carry=[("xchg", [parts["ffn1"]], []), ("swap", [g_xa1])])
    add("xa1", g_xa1, recv)
    dh2, dg_ffn0, dgt, dup, act, nb, gots["xa1"], recv_cod, recv_odo = _ffn_bwd_dx(
        h2, dh3, gt0, up0, row(sm["g_ffn"], 0), wg0, wu0, dn0, 0, tm_b,
        carry=[("xchg", [parts["xa1"]], []), ("swap", [g_cod, g_odo])])
    add("cod", g_cod, recv_cod)
    add("odo", g_odo, recv_odo)
    g_ffn0, (gots["cod"], gots["odo"]) = ffn_wgrads(0, dgt, dup, act, nb, dh3,
                                                    carry=("xchg", [parts["cod"], parts["odo"]], []))
    dh1, dk, dv, dg_xa0, g_xa0, recv = _xattn_bwd(
        h1, dh2, q0, row(sm["g_xattn"], 0), k0, v0, xa0, 0, tm_f, carry=("swap", [g_ffn0]))
    add("ffn0", g_ffn0, recv)
    dg_mem0, g_xa0 = _kv_proj_bwd(mem, row(sm["g_mem"], 0), dk, dv, xa0, g_xa0, 0)
    da2, dc2, dgb, g["ev_a_ln_g"], g["ev_a_ln_b"], g_evo, recv = _ev_mix_bwd(
        dh1, cat, a2, c2, z, sm["lng_a"], sm["lnb_a"], evo, tm_f, carry=("swap", [g_xa0]))
    add("xa0", g_xa0, recv)
    handle, token = _comm_start([parts["ffn0"], parts["xa0"]], [], [g_evo], "ev")
    dx, g["ev_a_conv_w"], g["ev_a_conv_b"], g["ev_b_conv_w"], g["ev_b_conv_b"], dg_mix0, g_cev = _ev_in_bwd(
        x, dh1, z, da2, dc2, dgb, row(sm["g_mix"], 0), sm["wa"], sm["wb"], wev, tm_b, token)
    (parts["ffn0"], parts["xa0"], g_evo), (gots["ffn0"], gots["xa0"], recv) = _comm_wait(handle, g_cev)
    add("evo", g_evo, recv)
    g["g_mix"] = jnp.concatenate([dg_mix0, dg_mix1], axis=0)
    g["g_xattn"] = jnp.concatenate([dg_xa0, dg_xa1], axis=0)
    g["g_mem"] = jnp.concatenate([dg_mem0, dg_mem1], axis=0)
    g["g_ffn"] = jnp.concatenate([dg_ffn0, dg_ffn1], axis=0)
    ga, gb = _small_grad_packs(g, loss)
    last = (g_cev, ga[None], gb[None])
    for name, piece, recv in zip(("cev", "ga", "gb"), last, _sibling_swap(last, "rs_swap_last")):
        add(name, piece, recv)
    return dx, parts, gots


def kernel(x, mem, g_mix, g_xattn, g_mem, g_ffn, g_final, ev_w_in, ev_a_conv_w, ev_a_conv_b, ev_a_ln_g, ev_a_ln_b, ev_b_conv_w, ev_b_conv_b, ev_w_out, od_w_in, od_c_ln_g, od_c_ln_b, od_w_s, od_b_s, od_w_out, xa_w_q, xa_w_k, xa_w_v, xa_w_o, ffn_w_gate, ffn_w_up, ffn_w_down, loss_target, m_g_mix, m_g_xattn, m_g_mem, m_g_ffn, m_g_final, m_ev_w_in, m_ev_a_conv_w, m_ev_a_conv_b, m_ev_a_ln_g, m_ev_a_ln_b, m_ev_b_conv_w, m_ev_b_conv_b, m_ev_w_out, m_od_w_in, m_od_c_ln_g, m_od_c_ln_b, m_od_w_s, m_od_b_s, m_od_w_out, m_xa_w_q, m_xa_w_k, m_xa_w_v, m_xa_w_o, m_ffn_w_gate, m_ffn_w_up, m_ffn_w_down, v_g_mix, v_g_xattn, v_g_mem, v_g_ffn, v_g_final, v_ev_w_in, v_ev_a_conv_w, v_ev_a_conv_b, v_ev_a_ln_g, v_ev_a_ln_b, v_ev_b_conv_w, v_ev_b_conv_b, v_ev_w_out, v_od_w_in, v_od_c_ln_g, v_od_c_ln_b, v_od_w_s, v_od_b_s, v_od_w_out, v_xa_w_q, v_xa_w_k, v_xa_w_v, v_xa_w_o, v_ffn_w_gate, v_ffn_w_up, v_ffn_w_down):
    W = dict(g_mix=g_mix, g_xattn=g_xattn, g_mem=g_mem, g_ffn=g_ffn, g_final=g_final, ev_w_in=ev_w_in,
             ev_a_conv_w=ev_a_conv_w, ev_a_conv_b=ev_a_conv_b, ev_a_ln_g=ev_a_ln_g, ev_a_ln_b=ev_a_ln_b,
             ev_b_conv_w=ev_b_conv_w, ev_b_conv_b=ev_b_conv_b, ev_w_out=ev_w_out, od_w_in=od_w_in,
             od_c_ln_g=od_c_ln_g, od_c_ln_b=od_c_ln_b, od_w_s=od_w_s, od_b_s=od_b_s, od_w_out=od_w_out,
             xa_w_q=xa_w_q, xa_w_k=xa_w_k, xa_w_v=xa_w_v, xa_w_o=xa_w_o, ffn_w_gate=ffn_w_gate,
             ffn_w_up=ffn_w_up, ffn_w_down=ffn_w_down)
    M = dict(g_mix=m_g_mix, g_xattn=m_g_xattn, g_mem=m_g_mem, g_ffn=m_g_ffn, g_final=m_g_final, ev_w_in=m_ev_w_in,
             ev_a_conv_w=m_ev_a_conv_w, ev_a_conv_b=m_ev_a_conv_b, ev_a_ln_g=m_ev_a_ln_g, ev_a_ln_b=m_ev_a_ln_b,
             ev_b_conv_w=m_ev_b_conv_w, ev_b_conv_b=m_ev_b_conv_b, ev_w_out=m_ev_w_out, od_w_in=m_od_w_in,
             od_c_ln_g=m_od_c_ln_g, od_c_ln_b=m_od_c_ln_b, od_w_s=m_od_w_s, od_b_s=m_od_b_s, od_w_out=m_od_w_out,
             xa_w_q=m_xa_w_q, xa_w_k=m_xa_w_k, xa_w_v=m_xa_w_v, xa_w_o=m_xa_w_o, ffn_w_gate=m_ffn_w_gate,
             ffn_w_up=m_ffn_w_up, ffn_w_down=m_ffn_w_down)
    V = dict(g_mix=v_g_mix, g_xattn=v_g_xattn, g_mem=v_g_mem, g_ffn=v_g_ffn, g_final=v_g_final, ev_w_in=v_ev_w_in,
             ev_a_conv_w=v_ev_a_conv_w, ev_a_conv_b=v_ev_a_conv_b, ev_a_ln_g=v_ev_a_ln_g, ev_a_ln_b=v_ev_a_ln_b,
             ev_b_conv_w=v_ev_b_conv_w, ev_b_conv_b=v_ev_b_conv_b, ev_w_out=v_ev_w_out, od_w_in=v_od_w_in,
             od_c_ln_g=v_od_c_ln_g, od_c_ln_b=v_od_c_ln_b, od_w_s=v_od_w_s, od_b_s=v_od_b_s, od_w_out=v_od_w_out,
             xa_w_q=v_xa_w_q, xa_w_k=v_xa_w_k, xa_w_v=v_xa_w_v, xa_w_o=v_xa_w_o, ffn_w_gate=v_ffn_w_gate,
             ffn_w_up=v_ffn_w_up, ffn_w_down=v_ffn_w_down)
    order = list(W)
    cx, cy, cc = _pos()
    chip = (2 * cx + cy).astype(jnp.int32)
    cc = cc.astype(jnp.int32)
    chip_idx = jnp.reshape(chip, (1,))
    c_idx = jnp.reshape(cc, (1,))

    tr = lambda a: jnp.swapaxes(a, 1, 2)
    P = {}
    cev, P["cod"], gs = _pack_cols(chip_idx, od_w_in, ev_w_in, ev_a_conv_w, ev_b_conv_w, od_c_ln_g, od_c_ln_b)
    ((P["evo"], P["odo"], P["dn0"], P["dn1"], P["wg0"], P["wg1"], P["wu0"], P["wu1"], P["xa0"], P["xa1"]),
     (P["cev"], P["gs"])) = _pack_rows(chip_idx, ffn_w_down, tr(ffn_w_gate), tr(ffn_w_up), xa_w_q, xa_w_o,
                                       xa_w_k, xa_w_v, ev_w_out, od_w_out, first=[cev, gs])

    dx, parts, gots = _step(x[0], mem[0], loss_target[0], W, P, c_idx, TM_FWD, TM_BWD, TM_WGRAD)

    chip_c = jnp.stack([chip, cc])
    zero_c = jnp.stack([jnp.zeros((), jnp.int32), cc])
    def sums(names):
        return [_sum_chips(parts[n], gots[n], zero_c if n in ("ga", "gb") else chip_c, f"rs_sum_{n}")
                for n in names]

    last = ("evo", "cev", "ga", "gb")
    handle, token = _comm_start([parts["evo"], parts["cev"]], [parts["ga"][0], parts["gb"][0]], [], "last")
    big_names, rest_names = ["ffn1", "ffn0"], ["xa1", "xa0", "cod", "odo"]
    join_big, token = _join_start(sums(big_names), "ffn", after=(token,))
    join_rest, token = _join_start(sums(rest_names), "rest", after=(token,))
    red = dict(zip(big_names, _join_wait(join_big, token)))

    res = {}

    def adamw(name, srcs, gblock, gidx, rb):
        t = tr if name in ("ffn_w_gate", "ffn_w_up") else (lambda a: a)
        outs = _adamw_big(t(W[name]), t(M[name]), t(V[name]), srcs, gblock, gidx, f"adamw_{name}", rb)
        res[name] = tuple(t(o) for o in outs)

    ffn = [red["ffn0"], red["ffn1"]]
    adamw("ffn_w_gate", ffn, (FSH, D), lambda i: (0, 0), FSH)
    adamw("ffn_w_up", ffn, (FSH, D), lambda i: (1, 0), FSH)
    adamw("ffn_w_down", ffn, (FSH, D), lambda i: (2, 0), FSH)
    red.update(zip(rest_names, _join_wait(join_rest, res["ffn_w_down"][1])))
    xa = [red["xa0"], red["xa1"]]
    adamw("xa_w_q", xa, (256, D), lambda i: (XQ, 0), 256)
    adamw("xa_w_o", xa, (256, D), lambda i: (XO, 0), 256)
    adamw("xa_w_k", xa, (256, D), lambda i: (XK, 0), 256)
    adamw("xa_w_v", xa, (256, D), lambda i: (XV, 0), 256)
    adamw("od_w_out", [red["odo"]], (256, D), lambda i: (0, 0), 256)
    adamw("od_w_in", [red["cod"]], (256, 512), lambda i: (i, 0), 256)

    srcs_done, got_last = _comm_wait(handle, res["od_w_in"][1])
    parts.update(zip(last, srcs_done[:2] + [a[None] for a in srcs_done[2:]]))
    gots.update(zip(last, got_last))
    red.update(zip(last, _sibling_join(sums(last), "rs_join_last")))
    adamw("ev_w_out", [red["evo"]], (256, D), lambda i: (0, 0), 256)
    adamw("ev_w_in", [red["cev"]], (256, 640), lambda i: (i, 0), 256)
    shape2 = {name: shape for name, shape, *_ in _SMALL}
    wmv = [tuple(d[name].reshape(shape2[name]) for d in (W, M, V)) for name, *_ in _SMALL]
    for (name, *_), outs in zip(_SMALL, _adamw_small(red["ga"], red["gb"], wmv)):
        res[name] = tuple(o.reshape(W[name].shape) for o in outs)

    grad_x = dx[None]
    loss = red["ga"][LOSS_ROW, 0]
    return (loss, grad_x, *[res[n][0] for n in order], *[res[n][1] for n in order],
            *[res[n][2] for n in order], *[res[n][3] for n in order])
```

```python
import jax
import jax.numpy as jnp
from jax import lax
from jax.experimental import pallas as pl
from jax.experimental.pallas import tpu as pltpu

BF = jnp.bfloat16
F32 = jnp.float32

D = 1024
DA = 512
DZ = 2560
DFF = 2816
NSH = 4
FSH = DFF // NSH
FF_CHUNKS = ((0, 768), (768, 1536), (1536, 2304), (2304, 2816))
NMEM = 256
HEADS = 4
HD = D // HEADS
CHUNK = 128
GROUPS = 8
KA = 31
KB = 3
HALO = 16
RC = 32
RMS_EPS = 1e-6
LN_EPS = 1e-5
ATT_SCALE = HD ** -0.5
TM_FWD = 512
TM_BWD = 256
TM_WGRAD = 1024

ADAM_LR = 0.001
ADAM_B1 = 0.9
ADAM_B2 = 0.999
ADAM_EPS = 1e-08
ADAM_WD = 0.01
ADAM_STEP = 10

XQ, XO, XK, XV = 0, 1, 2, 3
SP_ROWS = 48
SA_ROWS = 64

MESH = pl.DeviceIdType.MESH
ANY = pl.BlockSpec(memory_space=pl.ANY)


def _dot(a, b):
    return jnp.dot(a, b, preferred_element_type=F32)


def _dot_nt(a, b):
    return lax.dot_general(a, b, (((1,), (1,)), ((), ())), preferred_element_type=F32)


def _dot_tn(a, b):
    return lax.dot_general(a, b, (((0,), (0,)), ((), ())), preferred_element_type=F32)


def _rms(h):
    r = lax.rsqrt(jnp.mean(h * h, axis=-1, keepdims=True) + RMS_EPS)
    return h * r, r


def _rms_bwd(dn, hhat, r, g):
    dhh = dn * g
    return r * (dhh - hhat * jnp.mean(dhh * hhat, axis=-1, keepdims=True))


def _ln(x):
    mu = jnp.mean(x, axis=-1, keepdims=True)
    xc = x - mu
    rstd = lax.rsqrt(jnp.mean(xc * xc, axis=-1, keepdims=True) + LN_EPS)
    return xc * rstd, rstd


def _ln_bwd(dy, y, rstd):
    return rstd * (dy - jnp.mean(dy, axis=-1, keepdims=True) - y * jnp.mean(dy * y, axis=-1, keepdims=True))


def _colsum(x):
    return jnp.sum(x, axis=0, keepdims=True)


def _sigmoid(x):
    return 1.0 / (1.0 + jnp.exp(-x))


_GELU_C = 0.7978845608028654
_GELU_A = 0.044715


def _gelu(x):
    t = jnp.tanh(_GELU_C * (x + _GELU_A * x * x * x))
    return 0.5 * x * (1.0 + t)


def _gelu_grad(x):
    t = jnp.tanh(_GELU_C * (x + _GELU_A * x * x * x))
    return 0.5 * (1.0 + t) + 0.5 * x * (1.0 - t * t) * _GELU_C * (1.0 + 3.0 * _GELU_A * x * x)


def _first(i):
    return i == 0


def _last(i):
    return i == pl.num_programs(0) - 1


def _const(shape, idx):
    return pl.BlockSpec(shape, lambda *_, _idx=tuple(idx): _idx, pipeline_mode=pl.Buffered(1))


def _whole(a):
    return _const(a.shape, (0,) * a.ndim)


def _rows(tm, width, col=0):
    return pl.BlockSpec((tm, width), lambda i, _c=col: (i, _c))


def _params(ndim=1, vmem_mb=56):
    return pltpu.CompilerParams(dimension_semantics=("arbitrary",) * ndim, vmem_limit_bytes=vmem_mb << 20)


def _sds(shape, dtype):
    return jax.ShapeDtypeStruct(shape, dtype)


def _pos():
    return lax.axis_index("x"), lax.axis_index("y"), lax.axis_index("c")


def _other_chips(x, y):
    return [(1 - x, y), (x, 1 - y), (1 - x, 1 - y)]


def _half(ref, hc, lead=()):
    r = ref.shape[-2] // 2
    return ref.at[(*lead, pl.ds(pl.multiple_of(hc * r, 8), r), slice(None))]


def _rcopy(src, dst, ssem, rsem, to):
    return pltpu.make_async_remote_copy(src_ref=src, dst_ref=dst, send_sem=ssem, recv_sem=rsem,
                                        device_id=to, device_id_type=MESH)


def _gather_start(bufs, sems):
    ici_s, ici_r, _, _ = sems
    x, y, c = _pos()
    me = 2 * x + y
    for k, (cx, cy) in enumerate(_other_chips(x, y)):
        for t, buf in enumerate(bufs):
            mine = _half(buf, c, (me,))
            _rcopy(mine, mine, ici_s.at[3 * t + k], ici_r.at[3 * t + k], (cx, cy, c)).start()


def _gather_forward(bufs, sems):
    ici_s, ici_r, d2d_s, d2d_r = sems
    x, y, c = _pos()
    sib = (x, y, 1 - c)
    for k, (cx, cy) in enumerate(_other_chips(x, y)):
        for t, buf in enumerate(bufs):
            landed = _half(buf, c, (2 * cx + cy,))
            _rcopy(landed, landed, ici_s.at[3 * t + k], ici_r.at[3 * t + k], (cx, cy, c)).wait_recv()
            _rcopy(landed, landed, d2d_s.at[3 * t + k], d2d_r.at[3 * t + k], sib).start()


def _gather_drain(bufs, sems):
    ici_s, ici_r, d2d_s, d2d_r = sems
    x, y, c = _pos()
    me = 2 * x + y
    sib = (x, y, 1 - c)
    chips = _other_chips(x, y)
    for k, (cx, cy) in enumerate(chips):
        for t, buf in enumerate(bufs):
            got = _half(buf, 1 - c, (2 * cx + cy,))
            _rcopy(got, got, d2d_s.at[3 * t + k], d2d_r.at[3 * t + k], sib).wait_recv()
    for k, (cx, cy) in enumerate(chips):
        for t, buf in enumerate(bufs):
            mine = _half(buf, c, (me,))
            _rcopy(mine, mine, ici_s.at[3 * t + k], ici_r.at[3 * t + k], (cx, cy, c)).wait_send()
            landed = _half(buf, c, (2 * cx + cy,))
            _rcopy(landed, landed, d2d_s.at[3 * t + k], d2d_r.at[3 * t + k], sib).wait_send()


def _xchg_copies(srcs, dsts, nbig, sems):
    ssem, rsem = sems
    x, y, c = _pos()
    cps = []
    for k, (cx, cy) in enumerate(_other_chips(x, y)):
        for t, (src, dst) in enumerate(zip(srcs, dsts)):
            s = src.at[2 * cx + cy] if t < nbig else src
            cps.append(_rcopy(s, dst.at[k], ssem.at[3 * t + k], rsem.at[3 * t + k], (cx, cy, c)))
    return cps


def _gather_sems(n):
    return [pltpu.SemaphoreType.DMA((3 * n,))] * 4


def _xchg_sems(n):
    return [pltpu.SemaphoreType.DMA((3 * n,))] * 2


def _xchg_out_shapes(items):
    return [_sds((3,) + a.shape[-2:], a.dtype) for a in items]


def _swap_copies(srcs, dsts, sems):
    ssem, rsem = sems
    x, y, c = _pos()
    cps = []
    for t, (src, dst) in enumerate(zip(srcs, dsts)):
        lead = (slice(None),) * (len(src.shape) - 2)
        cps.append(_rcopy(_half(src, 1 - c, lead), dst, ssem.at[t], rsem.at[t], (x, y, 1 - c)))
    return cps


def _swap_out_shapes(items):
    return [_sds(a.shape[:-2] + (a.shape[-2] // 2, a.shape[-1]), a.dtype) for a in items]


def _call(body, *, name, grid, in_specs, out_specs, out_shape, args, scratch_shapes=(), aliases=None, carry=None):
    in_specs, out_specs, out_shape = list(in_specs), list(out_specs), list(out_shape)
    args = list(args)
    aliases = dict(aliases or {})
    n_in, n_out, n_scr = len(args), len(out_shape), len(scratch_shapes)
    carries = [] if not carry else ([carry] if isinstance(carry, tuple) else list(carry))
    plans = []
    cin, cout_shape, sems = [], [], []
    for c in carries:
        kind = c[0]
        items = list(c[1]) + (list(c[2]) if kind == "xchg" else [])
        if kind == "gather":
            shapes, sm = [_sds(a.shape, a.dtype) for a in items], _gather_sems(len(items))
            for t in range(len(items)):
                aliases[n_in + len(cin) + t] = n_out + len(cin) + t
        elif kind == "xchg":
            shapes, sm = _xchg_out_shapes(items), _xchg_sems(len(items))
        else:
            shapes, sm = _swap_out_shapes(items), [pltpu.SemaphoreType.DMA((len(items),))] * 2
        plans.append((kind, len(cin), len(items), len(sems), len(sm), len(c[1])))
        cin += items
        cout_shape += shapes
        sems += sm
    nc = len(cin)

    def wrapped(*refs):
        ins = refs[:n_in]
        c_in = refs[n_in:n_in + nc]
        outs = refs[n_in + nc:n_in + nc + n_out]
        c_out = refs[n_in + nc + n_out:n_in + 2 * nc + n_out]
        scr = refs[n_in + 2 * nc + n_out:n_in + 2 * nc + n_out + n_scr]
        sem = refs[n_in + 2 * nc + n_out + n_scr:]
        i = pl.program_id(0)

        def copies(plan):
            kind, o0, no, s0, ns, nbig = plan
            if kind == "xchg":
                return _xchg_copies(c_in[o0:o0 + no], c_out[o0:o0 + no], nbig, sem[s0:s0 + ns])
            return _swap_copies(c_in[o0:o0 + no], c_out[o0:o0 + no], sem[s0:s0 + ns])

        if plans:
            @pl.when(_first(i))
            def _():
                for plan in plans:
                    kind, o0, no, s0, ns, _ = plan
                    if kind == "gather":
                        _gather_start(c_out[o0:o0 + no], sem[s0:s0 + ns])
                    else:
                        for cp in copies(plan):
                            cp.start()
        body(*ins, *outs, *scr)
        for plan in plans:
            kind, o0, no, s0, ns, _ = plan
            if kind == "gather":
                @pl.when(i == max(grid[0] - 2, 0))
                def _(o0=o0, no=no, s0=s0, ns=ns):
                    _gather_forward(c_out[o0:o0 + no], sem[s0:s0 + ns])
        if plans:
            @pl.when(_last(i))
            def _():
                for plan in plans:
                    kind, o0, no, s0, ns, _ = plan
                    if kind == "gather":
                        _gather_drain(c_out[o0:o0 + no], sem[s0:s0 + ns])
                    else:
                        for cp in copies(plan):
                            cp.wait()

    outs = pl.pallas_call(
        wrapped, name=name, grid=grid,
        in_specs=in_specs + [ANY] * nc, out_specs=out_specs + [ANY] * nc,
        out_shape=out_shape + cout_shape,
        scratch_shapes=list(scratch_shapes) + sems,
        input_output_aliases=aliases,
        compiler_params=_params(len(grid)),
    )(*args, *cin)
    return list(outs)


def _sibling_swap(items, name):
    n = len(items)

    def body(*refs):
        src = refs[:n]
        dst = refs[n:2 * n]
        ssem, rsem = refs[2 * n:]
        x, y, c = _pos()
        sib = (x, y, 1 - c)
        cps = []
        for t in range(n):
            lead = (slice(None),) * (len(src[t].shape) - 2)
            cp = _rcopy(_half(src[t], 1 - c, lead), dst[t], ssem.at[t], rsem.at[t], sib)
            cp.start()
            cps.append(cp)
        for cp in cps:
            cp.wait()

    return pl.pallas_call(
        body, name=name, in_specs=[ANY] * n, out_specs=[ANY] * n,
        out_shape=[_sds(a.shape[:-2] + (a.shape[-2] // 2, a.shape[-1]), a.dtype) for a in items],
        scratch_shapes=[pltpu.SemaphoreType.DMA((n,)), pltpu.SemaphoreType.DMA((n,))],
    )(*items)


_HBM = pl.BlockSpec(memory_space=pltpu.HBM)
_SEM = pl.BlockSpec(memory_space=pltpu.SEMAPHORE)
_EFFECT = pltpu.SideEffectType.DATAFLOW_SIDE_EFFECTING


def _split_copies(src, land, nbig, nx, sems):
    return (_xchg_copies(src[:nx], land[:nx], nbig, sems[0:2]) if nx else []) + \
        (_swap_copies(src[nx:], land[nx:], sems[2:4]) if len(src) > nx else [])


def _comm_start(big, small, swaps, tag):
    xs = list(big) + list(small)
    items = xs + list(swaps)
    n, nx = len(items), len(xs)

    def body(*refs):
        for cp in _split_copies(refs[:n], refs[n:2 * n], len(big), nx, refs[2 * n:2 * n + 4]):
            cp.start()
        refs[-1][...] = jnp.zeros_like(refs[-1])

    lands = _xchg_out_shapes(xs) + _swap_out_shapes(swaps)
    sem_shapes = [pltpu.SemaphoreType.DMA((max(3 * nx, 1),))] * 2 + [pltpu.SemaphoreType.DMA((max(n - nx, 1),))] * 2
    hbm = lambda a: pltpu.with_memory_space_constraint(a, pltpu.HBM)
    outs = pl.pallas_call(
        body, name=f"rs_start_{tag}",
        out_shape=sem_shapes + [pltpu.HBM(a.shape, a.dtype) for a in items]
        + [pltpu.HBM(a.shape, a.dtype) for a in lands] + [_sds((8, 128), F32)],
        in_specs=[_HBM] * (2 * n), out_specs=[_SEM] * 4 + [_HBM] * (2 * n) + [pl.BlockSpec(memory_space=pltpu.VMEM)],
        input_output_aliases={t: 4 + t for t in range(2 * n)},
        compiler_params=pltpu.CompilerParams(has_side_effects=_EFFECT),
    )(*[hbm(a) for a in items], *[hbm(lax.empty(a.shape, a.dtype)) for a in lands])
    handle = (list(outs[:4]), list(outs[4:4 + n]), list(outs[4 + n:4 + 2 * n]), len(big), nx, tag)
    return handle, outs[-1]


def _comm_wait(handle, after):
    sems, srcs, lands, nbig, nx, tag = handle
    n = len(srcs)

    def body(*refs):
        for cp in _split_copies(refs[:n], refs[n:2 * n], nbig, nx, refs[2 * n:2 * n + 4]):
            cp.wait_send()
            cp.wait_recv()

    outs = pl.pallas_call(
        body, name=f"rs_wait_{tag}",
        out_shape=[pltpu.HBM(a.shape, a.dtype) for a in srcs] + [pltpu.HBM(a.shape, a.dtype) for a in lands],
        in_specs=[_HBM] * (2 * n) + [_SEM] * 4 + [ANY], out_specs=[_HBM] * (2 * n),
        input_output_aliases={t: t for t in range(2 * n)},
        compiler_params=pltpu.CompilerParams(has_side_effects=_EFFECT),
    )(*srcs, *lands, *sems, after)
    return list(outs[:n]), list(outs[n:])


def _join_copies(items, sems):
    ssem, rsem = sems
    x, y, c = _pos()
    return [_rcopy(_half(a, c), _half(a, c), ssem.at[t], rsem.at[t], (x, y, 1 - c)) for t, a in enumerate(items)]


def _join_start(items, tag, after=()):
    n = len(items)

    def body(*refs):
        for cp in _join_copies(refs[:n], refs[n + len(after):n + len(after) + 2]):
            cp.start()
        refs[-1][...] = jnp.zeros_like(refs[-1])

    hbm = lambda a: pltpu.with_memory_space_constraint(a, pltpu.HBM)
    outs = pl.pallas_call(
        body, name=f"rs_join_start_{tag}",
        out_shape=[pltpu.SemaphoreType.DMA((n,))] * 2 + [pltpu.HBM(a.shape, a.dtype) for a in items]
        + [_sds((8, 128), F32)],
        in_specs=[_HBM] * n + [ANY] * len(after),
        out_specs=[_SEM] * 2 + [_HBM] * n + [pl.BlockSpec(memory_space=pltpu.VMEM)],
        input_output_aliases={t: 2 + t for t in range(n)},
        compiler_params=pltpu.CompilerParams(has_side_effects=_EFFECT),
    )(*[hbm(a) for a in items], *after)
    return (list(outs[:2]), list(outs[2:2 + n]), tag), outs[-1]


def _join_wait(handle, after):
    sems, items, tag = handle
    n = len(items)

    def body(*refs):
        x, y, c = _pos()
        ssem, rsem = refs[n:n + 2]
        for t, cp in enumerate(_join_copies(refs[:n], (ssem, rsem))):
            cp.wait_send()
            theirs = _half(refs[t], 1 - c)
            _rcopy(theirs, theirs, ssem.at[t], rsem.at[t], (x, y, 1 - c)).wait_recv()

    outs = pl.pallas_call(
        body, name=f"rs_join_wait_{tag}",
        out_shape=[pltpu.HBM(a.shape, a.dtype) for a in items],
        in_specs=[_HBM] * n + [_SEM] * 2 + [ANY], out_specs=[_HBM] * n,
        input_output_aliases={t: t for t in range(n)},
        compiler_params=pltpu.CompilerParams(has_side_effects=_EFFECT),
    )(*items, *sems, after)
    return list(outs)


def _sibling_join(items, name, after=()):
    n = len(items)

    def body(*refs):
        dst = refs[n + len(after):2 * n + len(after)]
        ssem, rsem = refs[2 * n + len(after):]
        x, y, c = _pos()
        sib = (x, y, 1 - c)
        cps = []
        for t in range(n):
            mine = _half(dst[t], c)
            cp = _rcopy(mine, mine, ssem.at[t], rsem.at[t], sib)
            cp.start()
            cps.append(cp)
        for t, cp in enumerate(cps):
            theirs = _half(dst[t], 1 - c)
            _rcopy(theirs, theirs, ssem.at[t], rsem.at[t], sib).wait_recv()
            cp.wait_send()

    outs = pl.pallas_call(
        body, name=name, in_specs=[ANY] * (n + len(after)), out_specs=[ANY] * n,
        out_shape=[_sds(a.shape, a.dtype) for a in items],
        input_output_aliases={t: t for t in range(n)},
        scratch_shapes=[pltpu.SemaphoreType.DMA((n,))] * 2,
    )(*items, *after)
    return list(outs)


def _ev_in_fwd(h, g, wev, tm, carry=None):
    s = h.shape[0]

    def body(h_ref, g_ref, w_ref, z_ref):
        hhat, _ = _rms(h_ref[...])
        n = (hhat * g_ref[...]).astype(BF)
        z_ref[...] = _dot(n, w_ref[...]).astype(BF)

    return _call(
        body, name="ev_in_fwd", grid=(s // tm,),
        in_specs=[_rows(tm, D), _const((1, D), (0, 0)), _const((D, DZ), (0, 0))],
        out_specs=[_rows(tm, DZ)], out_shape=[_sds((s, DZ), BF)],
        args=(h, g, wev), carry=carry)


def _halo_specs(tm, width, s, col=0):
    per = tm // HALO
    nh = s // HALO
    prev = pl.BlockSpec((HALO, width), lambda i, _c=col: (jnp.maximum(i * per - 1, 0), _c))
    nxt = pl.BlockSpec((HALO, width), lambda i, _c=col: (jnp.minimum((i + 1) * per, nh - 1), _c))
    return prev, nxt


def _fill_pad(pad_ref, prev, main, nxt, i, tm):
    pad_ref[pl.ds(0, HALO), :] = jnp.where(_first(i), 0.0, prev)
    pad_ref[pl.ds(HALO, tm), :] = main
    pad_ref[pl.ds(HALO + tm, HALO), :] = jnp.where(_last(i), 0.0, nxt)


def _shift8(xs_ref, pad_ref, tm):
    for j in range(8):
        xs_ref[j] = pad_ref[pl.ds(j, tm + 2 * HALO - 8), :]


def _tap(xs_ref, r, off):
    return xs_ref[off % 8, pl.ds(r + 8 * (off // 8), RC), :]


def _glu_a(z):
    return z[:, :DA] * _sigmoid(z[:, DA:2 * DA])


def _gate_c(z):
    return z[:, 2048:] * z[:, 1024:1536]


def _ev_mix_fwd(h, z, wa, ba, lng, lnb, wb, bb, evo, tm, carry=None):
    s = h.shape[0]

    def body(h_ref, z_ref, zp_ref, zn_ref, wa_ref, ba_ref, lng_ref, lnb_ref, wb_ref, bb_ref, wo_ref,
             h1_ref, a2_ref, c2_ref, cat_ref, apad, cpad, c2f, xs):
        i = pl.program_id(0)
        zm = z_ref[...].astype(F32)
        zp = zp_ref[...].astype(F32)
        zn = zn_ref[...].astype(F32)
        _fill_pad(apad, _glu_a(zp), _glu_a(zm), _glu_a(zn), i, tm)
        _fill_pad(cpad, _gate_c(zp), _gate_c(zm), _gate_c(zn), i, tm)
        _shift8(xs, apad, tm)
        for r in range(0, tm, RC):
            acc = jnp.zeros((RC, DA), F32) + ba_ref[...]
            for k in range(KA):
                acc = acc + _tap(xs, r, k + 1) * wa_ref[pl.ds(k, 1), :]
            a2_ref[pl.ds(r, RC), :] = acc
            acc = jnp.zeros((RC, DA), F32) + bb_ref[...]
            for k in range(KB):
                acc = acc + cpad[pl.ds(r + HALO - 1 + k, RC), :] * wb_ref[pl.ds(k, 1), :]
            c2f[pl.ds(r, RC), :] = acc
        y, _ = _ln(a2_ref[...])
        a3 = y * lng_ref[...] + lnb_ref[...]
        cat_ref[:, :DA] = (a3 * _sigmoid(a3)).astype(BF)
        c2 = c2f[...]
        c2_ref[...] = c2.astype(BF)
        cat_ref[:, DA:] = (zm[:, 1536:2048] * c2).astype(BF)
        h1_ref[...] = h_ref[...] + _dot(cat_ref[...], wo_ref[...].reshape(D, D))

    zp_spec, zn_spec = _halo_specs(tm, DZ, s)
    small = lambda rows: _const((rows, DA), (0, 0))
    return _call(
        body, name="ev_mix_fwd", grid=(s // tm,),
        in_specs=[_rows(tm, D), _rows(tm, DZ), zp_spec, zn_spec, small(32), small(1), small(1), small(1),
                  small(8), small(1), _whole(evo)],
        out_specs=[_rows(tm, D), _rows(tm, DA), _rows(tm, DA), _rows(tm, D)],
        out_shape=[_sds((s, D), F32), _sds((s, DA), F32), _sds((s, DA), BF), _sds((s, D), BF)],
        scratch_shapes=[pltpu.VMEM((tm + 2 * HALO, DA), F32), pltpu.VMEM((tm + 2 * HALO, DA), F32),
                        pltpu.VMEM((tm, DA), F32), pltpu.VMEM((8, tm + 2 * HALO - 8, DA), F32)],
        args=(h, z, z, z, wa, ba, lng, lnb, wb, bb, evo), carry=carry)


def _ev_mix_bwd(dh, cat, a2, c2, z, lng, lnb, evo, tm, carry=None):
    s = dh.shape[0]

    def body(dh_ref, cat_ref, a2_ref, c2_ref, zgb_ref, lng_ref, lnb_ref, wo_ref,
             da2_ref, dc2_ref, dgb_ref, dlng_ref, dlnb_ref, gwo_ref, wacc):
        i = pl.program_id(0)

        @pl.when(_first(i))
        def _():
            wacc[...] = jnp.zeros_like(wacc)
            dlng_ref[...] = jnp.zeros_like(dlng_ref)
            dlnb_ref[...] = jnp.zeros_like(dlnb_ref)

        dhb = dh_ref[...].astype(BF)
        dcat = _dot_nt(dhb, wo_ref[...].reshape(D, D))
        wacc[...] += _dot_tn(cat_ref[...], dhb)
        y, rstd = _ln(a2_ref[...])
        a3 = y * lng_ref[...] + lnb_ref[...]
        sg = _sigmoid(a3)
        da3 = dcat[:, :DA] * (sg * (1.0 + a3 * (1.0 - sg)))
        dlng_ref[...] += _colsum(da3 * y)
        dlnb_ref[...] += _colsum(da3)
        da2_ref[...] = _ln_bwd(da3 * lng_ref[...], y, rstd).astype(BF)
        db = dcat[:, DA:]
        dc2_ref[...] = (db * zgb_ref[...].astype(F32)).astype(BF)
        dgb_ref[...] = (db * c2_ref[...].astype(F32)).astype(BF)

        @pl.when(_last(i))
        def _():
            gwo_ref[...] = wacc[...].astype(BF).reshape(NSH, 256, D)

    small = _const((1, DA), (0, 0))
    return _call(
        body, name="ev_mix_bwd", grid=(s // tm,),
        in_specs=[_rows(tm, D), _rows(tm, D), _rows(tm, DA), _rows(tm, DA), _rows(tm, DA, 3), small, small,
                  _whole(evo)],
        out_specs=[_rows(tm, DA), _rows(tm, DA), _rows(tm, DA), small, small, _whole(evo)],
        out_shape=[_sds((s, DA), BF), _sds((s, DA), BF), _sds((s, DA), BF), _sds((1, DA), F32),
                   _sds((1, DA), F32), _sds(evo.shape, BF)],
        scratch_shapes=[pltpu.VMEM((D, D), F32)],
        args=(dh, cat, a2, c2, z, lng, lnb, evo), carry=carry)


def _ev_in_bwd(h, dh, z, da2, dc2, dgb, g, wa, wb, wev, tm, after):
    s = h.shape[0]

    def body(h_ref, dh_ref, z_ref, zp_ref, zn_ref, da_ref, dap_ref, dan_ref, dc_ref, dcp_ref, dcn_ref,
             dgb_ref, g_ref, wa_ref, wb_ref, w_ref, after_ref,
             dh0_ref, dwa_ref, dba_ref, dwb_ref, dbb_ref, dg_ref, gw_ref,
             apad, cpad, dapad, dcpad, dz, wacc, dwa_acc, dwb_acc, xs):
        i = pl.program_id(0)

        @pl.when(_first(i))
        def _():
            wacc[...] = jnp.zeros_like(wacc)
            dwa_acc[...] = jnp.zeros_like(dwa_acc)
            dwb_acc[...] = jnp.zeros_like(dwb_acc)
            dba_ref[...] = jnp.zeros_like(dba_ref)
            dbb_ref[...] = jnp.zeros_like(dbb_ref)
            dg_ref[...] = jnp.zeros_like(dg_ref)

        zm = z_ref[...].astype(F32)
        zp = zp_ref[...].astype(F32)
        zn = zn_ref[...].astype(F32)
        _fill_pad(apad, _glu_a(zp), _glu_a(zm), _glu_a(zn), i, tm)
        _fill_pad(cpad, _gate_c(zp), _gate_c(zm), _gate_c(zn), i, tm)
        da2 = da_ref[...].astype(F32)
        dc2 = dc_ref[...].astype(F32)
        _fill_pad(dapad, dap_ref[...].astype(F32), da2, dan_ref[...].astype(F32), i, tm)
        _fill_pad(dcpad, dcp_ref[...].astype(F32), dc2, dcn_ref[...].astype(F32), i, tm)
        dba_ref[...] += _colsum(da2)
        dbb_ref[...] += _colsum(dc2)

        sg = _sigmoid(zm[:, DA:2 * DA])
        aval = zm[:, :DA]
        _shift8(xs, dapad, tm)
        for r in range(0, tm, RC):
            acc = jnp.zeros((RC, DA), F32)
            for k in range(KA):
                acc = acc + _tap(xs, r, 2 * HALO - 1 - k) * wa_ref[pl.ds(k, 1), :]
            sgr = sg[r:r + RC]
            dz[pl.ds(r, RC), 0:DA] = (acc * sgr).astype(BF)
            dz[pl.ds(r, RC), DA:2 * DA] = (acc * aval[r:r + RC] * sgr * (1.0 - sgr)).astype(BF)
            acc = jnp.zeros((RC, DA), F32)
            for k in range(KB):
                acc = acc + dcpad[pl.ds(r + HALO + 1 - k, RC), :] * wb_ref[pl.ds(k, 1), :]
            dz[pl.ds(r, RC), 1024:1536] = (acc * zm[r:r + RC, 2048:]).astype(BF)
            dz[pl.ds(r, RC), 2048:2560] = (acc * zm[r:r + RC, 1024:1536]).astype(BF)
        dz[:, 1536:2048] = dgb_ref[...]

        _shift8(xs, apad, tm)
        for k in range(KA):
            acc = jnp.zeros((8, DA), F32)
            for r in range(0, tm, RC):
                p = _tap(xs, r, k + 1) * dapad[pl.ds(r + HALO, RC), :]
                acc = acc + p.reshape(RC // 8, 8, DA).sum(axis=0)
            dwa_acc[k] += acc
        for k in range(KB):
            acc = jnp.zeros((8, DA), F32)
            for r in range(0, tm, RC):
                p = cpad[pl.ds(r + HALO - 1 + k, RC), :] * dcpad[pl.ds(r + HALO, RC), :]
                acc = acc + p.reshape(RC // 8, 8, DA).sum(axis=0)
            dwb_acc[k] += acc

        gg = g_ref[...]
        hhat, rr = _rms(h_ref[...])
        n = (hhat * gg).astype(BF)
        dzb = dz[...]
        wacc[...] += _dot_tn(n, dzb)
        dn = _dot_nt(dzb, w_ref[...])
        dg_ref[...] += _colsum(dn * hhat)
        dh0_ref[...] = dh_ref[...] + _rms_bwd(dn, hhat, rr, gg)

        @pl.when(_last(i))
        def _():
            for k in range(NSH):
                gw_ref[k] = wacc[:, 640 * k:640 * (k + 1)].astype(BF)
            dwa_ref[...] = jnp.sum(dwa_acc[...], axis=1)
            dwb_ref[...] = jnp.sum(dwb_acc[...], axis=1)

    zp_spec, zn_spec = _halo_specs(tm, DZ, s)
    hp_spec, hn_spec = _halo_specs(tm, DA, s)
    small = lambda rows: _const((rows, DA), (0, 0))
    return _call(
        body, name="ev_in_bwd", grid=(s // tm,),
        in_specs=[_rows(tm, D), _rows(tm, D), _rows(tm, DZ), zp_spec, zn_spec,
                  _rows(tm, DA), hp_spec, hn_spec, _rows(tm, DA), hp_spec, hn_spec, _rows(tm, DA),
                  _const((1, D), (0, 0)), small(32), small(8), _const((D, DZ), (0, 0)), ANY],
        out_specs=[_rows(tm, D), small(32), small(1), small(8), small(1), _const((1, D), (0, 0)),
                   _const((NSH, D, 640), (0, 0, 0))],
        out_shape=[_sds((s, D), F32), _sds((32, DA), F32), _sds((1, DA), F32), _sds((8, DA), F32),
                   _sds((1, DA), F32), _sds((1, D), F32), _sds((NSH, D, 640), BF)],
        scratch_shapes=[pltpu.VMEM((tm + 2 * HALO, DA), F32)] * 4 + [
            pltpu.VMEM((tm, DZ), BF), pltpu.VMEM((D, DZ), F32),
            pltpu.VMEM((32, 8, DA), F32), pltpu.VMEM((8, 8, DA), F32),
            pltpu.VMEM((8, tm + 2 * HALO - 8, DA), F32)],
        args=(h, dh, z, z, z, da2, da2, da2, dc2, dc2, dc2, dgb, g, wa, wb, wev, after))


def _xa_spec(item):
    return _const((NSH, 256, D), (0, item, 0))


def _kv_proj(mem, g, xa, layer):
    def body(mem_ref, g_ref, wk_ref, wv_ref, k_ref, v_ref):
        mhat, _ = _rms(mem_ref[...])
        mn = (mhat * g_ref[...]).astype(BF)
        k_ref[...] = _dot(mn, wk_ref[...].reshape(D, D)).astype(BF)
        v_ref[...] = _dot(mn, wv_ref[...].reshape(D, D)).astype(BF)

    full = _const((NMEM, D), (0, 0))
    return _call(
        body, name=f"kv_proj{layer}", grid=(1,),
        in_specs=[full, _const((1, D), (0, 0)), _xa_spec(XK), _xa_spec(XV)],
        out_specs=[full, full], out_shape=[_sds((NMEM, D), BF), _sds((NMEM, D), BF)],
        args=(mem, g, xa, xa))


def _softmax_rows(sc):
    m = jnp.max(sc, axis=-1, keepdims=True)
    e = jnp.exp(sc - m)
    return e / jnp.sum(e, axis=-1, keepdims=True)


def _xattn_fwd(h, g, k, v, xa, layer, tm, carry=None):
    s = h.shape[0]

    def body(h_ref, g_ref, k_ref, v_ref, wq_ref, wo_ref, ho_ref, q_ref, o_scr):
        hh = h_ref[...]
        hhat, _ = _rms(hh)
        n = (hhat * g_ref[...]).astype(BF)
        q = _dot(n, wq_ref[...].reshape(D, D)).astype(BF)
        q_ref[...] = q
        for hd in range(HEADS):
            cs = slice(HD * hd, HD * (hd + 1))
            p = _softmax_rows(_dot_nt(q[:, cs], k_ref[:, cs]) * ATT_SCALE)
            o_scr[:, cs] = _dot(p.astype(BF), v_ref[:, cs]).astype(BF)
        ho_ref[...] = hh + _dot(o_scr[...], wo_ref[...].reshape(D, D))

    kvs = _const((NMEM, D), (0, 0))
    return _call(
        body, name=f"xattn_fwd{layer}", grid=(s // tm,),
        in_specs=[_rows(tm, D), _const((1, D), (0, 0)), kvs, kvs, _xa_spec(XQ), _xa_spec(XO)],
        out_specs=[_rows(tm, D), _rows(tm, D)],
        out_shape=[_sds((s, D), F32), _sds((s, D), BF)],
        scratch_shapes=[pltpu.VMEM((tm, D), BF)],
        args=(h, g, k, v, xa, xa), carry=carry)


def _xattn_bwd(h, dh, q, g, k, v, xa, layer, tm, carry=None):
    s = h.shape[0]

    def body(h_ref, dh_ref, q_ref, g_ref, k_ref, v_ref, wq_ref, wo_ref,
             dhi_ref, dk_ref, dv_ref, dg_ref, gqo_ref, o_scr, dq_scr, qacc, oacc):
        i = pl.program_id(0)

        @pl.when(_first(i))
        def _():
            qacc[...] = jnp.zeros_like(qacc)
            oacc[...] = jnp.zeros_like(oacc)
            dk_ref[...] = jnp.zeros_like(dk_ref)
            dv_ref[...] = jnp.zeros_like(dv_ref)
            dg_ref[...] = jnp.zeros_like(dg_ref)

        gg = g_ref[...]
        dho = dh_ref[...]
        dhb = dho.astype(BF)
        q = q_ref[...]
        do = _dot_nt(dhb, wo_ref[...].reshape(D, D)).astype(BF)
        for hd in range(HEADS):
            cs = slice(HD * hd, HD * (hd + 1))
            kh = k_ref[:, cs]
            vh = v_ref[:, cs]
            p = _softmax_rows(_dot_nt(q[:, cs], kh) * ATT_SCALE)
            pb = p.astype(BF)
            o_scr[:, cs] = _dot(pb, vh).astype(BF)
            doh = do[:, cs]
            dp = _dot_nt(doh, vh)
            dv_ref[:, cs] += _dot_tn(pb, doh)
            ds = (p * (dp - jnp.sum(dp * p, axis=-1, keepdims=True)) * ATT_SCALE).astype(BF)
            dq_scr[:, cs] = _dot(ds, kh).astype(BF)
            dk_ref[:, cs] += _dot_tn(ds, q[:, cs])
        oacc[...] += _dot_tn(o_scr[...], dhb)
        hhat, rr = _rms(h_ref[...])
        n = (hhat * gg).astype(BF)
        dq = dq_scr[...]
        qacc[...] += _dot_tn(n, dq)
        dn = _dot_nt(dq, wq_ref[...].reshape(D, D))
        dg_ref[...] += _colsum(dn * hhat)
        dhi_ref[...] = dho + _rms_bwd(dn, hhat, rr, gg)

        @pl.when(_last(i))
        def _():
            gqo_ref[:, 0:256, :] = qacc[...].astype(BF).reshape(NSH, 256, D)
            gqo_ref[:, 256:512, :] = oacc[...].astype(BF).reshape(NSH, 256, D)

    kvs = _const((NMEM, D), (0, 0))
    return _call(
        body, name=f"xattn_bwd{layer}", grid=(s // tm,),
        in_specs=[_rows(tm, D), _rows(tm, D), _rows(tm, D), _const((1, D), (0, 0)), kvs, kvs,
                  _xa_spec(XQ), _xa_spec(XO)],
        out_specs=[_rows(tm, D), kvs, kvs, _const((1, D), (0, 0)), _const((NSH, 512, D), (0, 0, 0))],
        out_shape=[_sds((s, D), F32), _sds((NMEM, D), F32), _sds((NMEM, D), F32), _sds((1, D), F32),
                   _sds(xa.shape, BF)],
        scratch_shapes=[pltpu.VMEM((tm, D), BF), pltpu.VMEM((tm, D), BF), pltpu.VMEM((D, D), F32),
                        pltpu.VMEM((D, D), F32)],
        args=(h, dh, q, g, k, v, xa, xa), carry=carry)


def _kv_proj_bwd(mem, g, dk, dv, xa, gxa, layer):
    def body(mem_ref, g_ref, dk_ref, dv_ref, wk_ref, wv_ref, gxa_in, dg_ref, gkv_ref):
        mhat, _ = _rms(mem_ref[...])
        mn = (mhat * g_ref[...]).astype(BF)
        dkb = dk_ref[...].astype(BF)
        dvb = dv_ref[...].astype(BF)
        gkv_ref[:, 0:256, :] = _dot_tn(mn, dkb).astype(BF).reshape(NSH, 256, D)
        gkv_ref[:, 256:512, :] = _dot_tn(mn, dvb).astype(BF).reshape(NSH, 256, D)
        dmn = _dot_nt(dkb, wk_ref[...].reshape(D, D)) + _dot_nt(dvb, wv_ref[...].reshape(D, D))
        dg_ref[...] = _colsum(dmn * mhat)

    full = _const((NMEM, D), (0, 0))
    return _call(
        body, name=f"kv_proj_bwd{layer}", grid=(1,),
        in_specs=[full, _const((1, D), (0, 0)), full, full, _xa_spec(XK), _xa_spec(XV), ANY],
        out_specs=[_const((1, D), (0, 0)), _const((NSH, 512, D), (0, 1, 0))],
        out_shape=[_sds((1, D), F32), _sds(gxa.shape, BF)],
        aliases={6: 1},
        args=(mem, g, dk, dv, xa, xa, gxa))


def _ffn_block(h_ref, g_ref, wg_ref, wu_ref, wd_ref, gt_ref, up_ref):
    hh = h_ref[...]
    hhat, _ = _rms(hh)
    n = (hhat * g_ref[...]).astype(BF)
    out = hh
    for c0, c1 in FF_CHUNKS:
        gt = _dot_nt(n, wg_ref[c0:c1, :])
        up = _dot_nt(n, wu_ref[c0:c1, :])
        gt_ref[:, c0:c1] = gt.astype(BF)
        up_ref[:, c0:c1] = up.astype(BF)
        act = (gt * _sigmoid(gt) * up).astype(BF)
        out = out + _dot(act, wd_ref[c0:c1, :])
    return out


def _ffn_fwd(h, g, wg, wu, dn, layer, tm, carry=None):
    s = h.shape[0]
    wg, wu, dn = (a.reshape(DFF, D) for a in (wg, wu, dn))

    def body(h_ref, g_ref, wg_ref, wu_ref, wd_ref, ho_ref, gt_ref, up_ref):
        ho_ref[...] = _ffn_block(h_ref, g_ref, wg_ref, wu_ref, wd_ref, gt_ref, up_ref)

    return _call(
        body, name=f"ffn_fwd{layer}", grid=(s // tm,),
        in_specs=[_rows(tm, D), _const((1, D), (0, 0)), _whole(wg), _whole(wu), _whole(dn)],
        out_specs=[_rows(tm, D), _rows(tm, DFF), _rows(tm, DFF)],
        out_shape=[_sds((s, D), F32), _sds((s, DFF), BF), _sds((s, DFF), BF)],
        args=(h, g, wg, wu, dn), carry=carry)


def _ffn_fwd_loss(h, g, wg, wu, dn, tgt, g_final, layer, tm):
    s = h.shape[0]
    wg, wu, dn = (a.reshape(DFF, D) for a in (wg, wu, dn))

    def body(h_ref, g_ref, wg_ref, wu_ref, wd_ref, t_ref, gf_ref, dh_ref, loss_ref, dg_ref, gt_ref, up_ref):
        i = pl.program_id(0)

        @pl.when(_first(i))
        def _():
            loss_ref[...] = jnp.zeros_like(loss_ref)
            dg_ref[...] = jnp.zeros_like(dg_ref)

        out = _ffn_block(h_ref, g_ref, wg_ref, wu_ref, wd_ref, gt_ref, up_ref)
        gg = gf_ref[...]
        hhat, rr = _rms(out)
        diff = hhat * gg - t_ref[...]
        loss_ref[...] += jnp.sum(diff * diff) * (0.5 / D)
        dout = diff * (1.0 / D)
        dg_ref[...] += _colsum(dout * hhat)
        dh_ref[...] = _rms_bwd(dout, hhat, rr, gg)

    vec = _const((1, D), (0, 0))
    return _call(
        body, name=f"ffn_fwd_loss{layer}", grid=(s // tm,),
        in_specs=[_rows(tm, D), vec, _whole(wg), _whole(wu), _whole(dn), _rows(tm, D), vec],
        out_specs=[_rows(tm, D), _const((1, 128), (0, 0)), vec, _rows(tm, DFF), _rows(tm, DFF)],
        out_shape=[_sds((s, D), F32), _sds((1, 128), F32), _sds((1, D), F32), _sds((s, DFF), BF),
                   _sds((s, DFF), BF)],
        args=(h, g, wg, wu, dn, tgt, g_final))


def _ffn_bwd_dx(h, dh, gt, up, g, wg, wu, dn, layer, tm, carry=None):
    s = h.shape[0]
    wg, wu, dn = (a.reshape(DFF, D) for a in (wg, wu, dn))

    def body(h_ref, dh_ref, gt_ref, up_ref, g_ref, wg_ref, wu_ref, wd_ref,
             dhi_ref, dg_ref, dgt_ref, dup_ref, act_ref, n_ref):
        i = pl.program_id(0)

        @pl.when(_first(i))
        def _():
            dg_ref[...] = jnp.zeros_like(dg_ref)

        gg = g_ref[...]
        dho = dh_ref[...]
        dhb = dho.astype(BF)
        dn_ = jnp.zeros((tm, D), F32)
        for c0, c1 in FF_CHUNKS:
            gtv = gt_ref[:, c0:c1].astype(F32)
            upv = up_ref[:, c0:c1].astype(F32)
            sg = _sigmoid(gtv)
            silu = gtv * sg
            dact = _dot_nt(dhb, wd_ref[c0:c1, :])
            dgt = (dact * upv * (sg * (1.0 + gtv * (1.0 - sg)))).astype(BF)
            dup = (dact * silu).astype(BF)
            dgt_ref[:, c0:c1] = dgt
            dup_ref[:, c0:c1] = dup
            act_ref[:, c0:c1] = (silu * upv).astype(BF)
            dn_ = dn_ + _dot(dgt, wg_ref[c0:c1, :]) + _dot(dup, wu_ref[c0:c1, :])
        hhat, rr = _rms(h_ref[...])
        n_ref[...] = (hhat * gg).astype(BF)
        dg_ref[...] += _colsum(dn_ * hhat)
        dhi_ref[...] = dho + _rms_bwd(dn_, hhat, rr, gg)

    wide = _rows(tm, DFF)
    return _call(
        body, name=f"ffn_bwd_dx{layer}", grid=(s // tm,),
        in_specs=[_rows(tm, D), _rows(tm, D), wide, wide, _const((1, D), (0, 0)),
                  _whole(wg), _whole(wu), _whole(dn)],
        out_specs=[_rows(tm, D), _const((1, D), (0, 0)), wide, wide, wide, _rows(tm, D)],
        out_shape=[_sds((s, D), F32), _sds((1, D), F32), _sds((s, DFF), BF), _sds((s, DFF), BF),
                   _sds((s, DFF), BF), _sds((s, D), BF)],
        args=(h, dh, gt, up, g, wg, wu, dn), carry=carry)


def _ffn_wgrad(a, b, name, tm, slot, into=None, carry=None):
    s = a.shape[0]
    tm = min(tm, s)

    def body(a_ref, b_ref, *rest):
        o_ref, acc = rest[-2:]
        i = pl.program_id(0)

        @pl.when(_first(i))
        def _():
            acc[...] = jnp.zeros_like(acc)

        acc[...] += _dot_tn(a_ref[...], b_ref[...].astype(BF))

        @pl.when(_last(i))
        def _():
            o_ref[...] = acc[...].astype(BF).reshape(NSH, FSH, D)

    chained = into is not None
    return _call(
        body, name=name, grid=(s // tm,),
        in_specs=[_rows(tm, DFF), _rows(tm, D)] + [ANY] * chained,
        out_specs=[_const((NSH, FSH, D), (0, slot, 0))],
        out_shape=[_sds((NSH, 3 * FSH, D), BF)],
        scratch_shapes=[pltpu.VMEM((DFF, D), F32)],
        aliases={2: 0} if chained else None,
        args=(a, b) + ((into,) if chained else ()), carry=carry)


def _sgu_fwd(h, g, lng, lnb, ws, bsb, wod, odo, tm, carry=None):
    s = h.shape[0]

    def body(h_ref, g_ref, lng_ref, lnb_ref, ws_ref, bsb_ref, wi_ref, wo_ref, ho_ref, zp_ref, y_scr):
        hh = h_ref[...]
        hhat, _ = _rms(hh)
        n = (hhat * g_ref[...]).astype(BF)
        zpre = _dot(n, wi_ref[...])
        zp_ref[...] = zpre.astype(BF)
        z = _gelu(zpre)
        u = z[:, :D]
        y, _ = _ln(z[:, D:])
        vn = (y * lng_ref[...] + lnb_ref[...]).astype(BF)
        for c in range(tm // CHUNK):
            rs = slice(CHUNK * c, CHUNK * (c + 1))
            for gi in range(GROUPS):
                cs = slice(CHUNK * gi, CHUNK * (gi + 1))
                sv = _dot(ws_ref[gi], vn[rs, cs]) + bsb_ref[gi]
                y_scr[rs, cs] = (u[rs, cs] * sv).astype(BF)
        ho_ref[...] = hh + _dot(y_scr[...], wo_ref[...].reshape(D, D))

    vec = _const((1, D), (0, 0))
    sq = _const((GROUPS, CHUNK, CHUNK), (0, 0, 0))
    return _call(
        body, name="sgu_fwd", grid=(s // tm,),
        in_specs=[_rows(tm, D), vec, vec, vec, sq, sq, _const((D, 2 * D), (0, 0)), _whole(odo)],
        out_specs=[_rows(tm, D), _rows(tm, 2 * D)],
        out_shape=[_sds((s, D), F32), _sds((s, 2 * D), BF)],
        scratch_shapes=[pltpu.VMEM((tm, D), BF)],
        args=(h, g, lng, lnb, ws, bsb, wod, odo), carry=carry)


def _sgu_bwd(h, dh, zpre, g, lng, lnb, ws, wst, bsb, wod, odo, tm, carry=None):
    s = h.shape[0]

    def body(h_ref, dh_ref, zp_ref, g_ref, lng_ref, lnb_ref, ws_ref, wst_ref, bsb_ref, wi_ref, wo_ref,
             dhi_ref, dws_ref, dbs_ref, dlng_ref, dlnb_ref, dg_ref, gwi_ref, gwo_ref,
             y_scr, dz_scr, dvn_scr, iacc, oacc, bacc):
        i = pl.program_id(0)

        @pl.when(_first(i))
        def _():
            for ref in (iacc, oacc, bacc, dws_ref, dlng_ref, dlnb_ref, dg_ref):
                ref[...] = jnp.zeros_like(ref)

        gg = g_ref[...]
        dho = dh_ref[...]
        dhb = dho.astype(BF)
        zpre_v = zp_ref[...].astype(F32)
        z = _gelu(zpre_v)
        u = z[:, :D]
        yl, rstd = _ln(z[:, D:])
        vn = (yl * lng_ref[...] + lnb_ref[...]).astype(BF)
        dy = _dot_nt(dhb, wo_ref[...].reshape(D, D))
        for c in range(tm // CHUNK):
            rs = slice(CHUNK * c, CHUNK * (c + 1))
            for gi in range(GROUPS):
                cs = slice(CHUNK * gi, CHUNK * (gi + 1))
                vb = vn[rs, cs]
                sv = _dot(ws_ref[gi], vb) + bsb_ref[gi]
                ub = u[rs, cs]
                dyb = dy[rs, cs]
                y_scr[rs, cs] = (ub * sv).astype(BF)
                dz_scr[rs, cs] = dyb * sv
                dsv = dyb * ub
                bacc[gi] += dsv
                dsvb = dsv.astype(BF)
                dws_ref[gi] += _dot_nt(dsvb, vb)
                dvn_scr[rs, cs] = _dot(wst_ref[gi], dsvb)
        oacc[...] += _dot_tn(y_scr[...], dhb)
        dvn = dvn_scr[...]
        dlng_ref[...] += _colsum(dvn * yl)
        dlnb_ref[...] += _colsum(dvn)
        dz_scr[:, D:] = _ln_bwd(dvn * lng_ref[...], yl, rstd)
        dzb = (dz_scr[...] * _gelu_grad(zpre_v)).astype(BF)
        hhat, rr = _rms(h_ref[...])
        n = (hhat * gg).astype(BF)
        iacc[...] += _dot_tn(n, dzb)
        dn = _dot_nt(dzb, wi_ref[...])
        dg_ref[...] += _colsum(dn * hhat)
        dhi_ref[...] = dho + _rms_bwd(dn, hhat, rr, gg)

        @pl.when(_last(i))
        def _():
            for k in range(NSH):
                gwi_ref[k] = iacc[:, 512 * k:512 * (k + 1)].astype(BF)
            gwo_ref[...] = oacc[...].astype(BF).reshape(NSH, 256, D)
            dbs_ref[...] = jnp.sum(bacc[...], axis=-1)

    vec = _const((1, D), (0, 0))
    sq = _const((GROUPS, CHUNK, CHUNK), (0, 0, 0))
    return _call(
        body, name="sgu_bwd", grid=(s // tm,),
        in_specs=[_rows(tm, D), _rows(tm, D), _rows(tm, 2 * D), vec, vec, vec, sq, sq, sq,
                  _const((D, 2 * D), (0, 0)), _whole(odo)],
        out_specs=[_rows(tm, D), sq, _const((GROUPS, CHUNK), (0, 0)), vec, vec, vec,
                   _const((NSH, D, 512), (0, 0, 0)), _whole(odo)],
        out_shape=[_sds((s, D), F32), _sds((GROUPS, CHUNK, CHUNK), F32), _sds((GROUPS, CHUNK), F32),
                   _sds((1, D), F32), _sds((1, D), F32), _sds((1, D), F32), _sds((NSH, D, 512), BF),
                   _sds(odo.shape, BF)],
        scratch_shapes=[pltpu.VMEM((tm, D), BF), pltpu.VMEM((tm, 2 * D), F32),
                        pltpu.VMEM((tm, D), F32), pltpu.VMEM((D, 2 * D), F32), pltpu.VMEM((D, D), F32),
                        pltpu.VMEM((GROUPS, CHUNK, CHUNK), F32)],
        args=(h, dh, zpre, g, lng, lnb, ws, wst, bsb, wod, odo), carry=carry)


def _own_plane(shape):
    return pl.BlockSpec((1,) + shape, lambda i, chip_ref: (chip_ref[0], 0, 0), pipeline_mode=pl.Buffered(1))


def _pack_rows(chip_idx, w_down, w_gate_t, w_up_t, xq, xo, xk, xv, ev_out, od_out, first):
    nf = len(first)

    def body(chip_ref, wd_ref, wg_ref, wu_ref, xq_ref, xo_ref, xk_ref, xv_ref, ev_ref, od_ref, *rest):
        (evo_ref, odo_ref, dn0_ref, dn1_ref, wg0_ref, wg1_ref, wu0_ref, wu1_ref, xa0_ref,
         xa1_ref) = rest[nf:nf + 10]
        gathered = rest[nf + 10:2 * nf + 10]
        sems = rest[2 * nf + 10:]
        _gather_start(gathered, sems)
        evo_ref[0] = ev_ref[0].astype(BF)
        odo_ref[0] = od_ref[0].astype(BF)
        for src, outs in ((wd_ref, (dn0_ref, dn1_ref)), (wg_ref, (wg0_ref, wg1_ref)), (wu_ref, (wu0_ref, wu1_ref))):
            for layer, o_ref in enumerate(outs):
                o_ref[0] = src[layer].astype(BF)
        for layer, xa_ref in enumerate((xa0_ref, xa1_ref)):
            for item, src in ((XQ, xq_ref), (XO, xo_ref), (XK, xk_ref), (XV, xv_ref)):
                xa_ref[0, 256 * item:256 * (item + 1), :] = src[layer].astype(BF)
        _gather_forward(gathered, sems)
        _gather_drain(gathered, sems)

    ins = (w_down, w_gate_t, w_up_t, xq, xo, xk, xv, ev_out, od_out)
    shapes = [(256, D), (256, D)] + [(FSH, D)] * 6 + [(4 * 256, D), (4 * 256, D)]
    grid_spec = pltpu.PrefetchScalarGridSpec(
        num_scalar_prefetch=1, grid=(1,), in_specs=[_whole(a) for a in ins] + [ANY] * nf,
        out_specs=[_own_plane(sh) for sh in shapes] + [ANY] * nf,
        scratch_shapes=_gather_sems(nf))
    outs = pl.pallas_call(
        body, name="pack_rows", grid_spec=grid_spec,
        out_shape=[_sds((NSH,) + sh, BF) for sh in shapes] + [_sds(a.shape, a.dtype) for a in first],
        input_output_aliases={1 + len(ins) + t: len(shapes) + t for t in range(nf)},
        compiler_params=_params(),
    )(chip_idx, *ins, *first)
    return outs[:len(shapes)], outs[len(shapes):]


def _pack_cols(chip_idx, od_in, ev_in, conv_a, conv_b, ln_g, ln_b):
    def body(chip_ref, od_ref, ev_ref, ca_ref, cb_ref, lg_ref, lb_ref, cev_ref, cod_ref, sp_ref):
        cev_ref[0] = ev_ref[0].astype(BF)
        cod_ref[0] = od_ref[0].astype(BF)
        sp_ref[...] = jnp.zeros_like(sp_ref)
        sp_ref[0, 0:KA, 0:128] = ca_ref[0]
        sp_ref[0, 32:32 + KB, 0:128] = cb_ref[0]
        sp_ref[0, 40:41, :] = lg_ref[...]
        sp_ref[0, 41:42, :] = lb_ref[...]

    ins = (od_in, ev_in, conv_a, conv_b, ln_g, ln_b)
    shapes = [(D, 640), (D, 512)]
    grid_spec = pltpu.PrefetchScalarGridSpec(
        num_scalar_prefetch=1, grid=(1,), in_specs=[_whole(a) for a in ins],
        out_specs=[_own_plane(sh) for sh in shapes] + [_own_plane((SP_ROWS, 256))])
    return pl.pallas_call(
        body, name="pack_cols", grid_spec=grid_spec,
        out_shape=[_sds((NSH,) + sh, BF) for sh in shapes] + [_sds((NSH, SP_ROWS, 256), F32)],
        compiler_params=_params(),
    )(chip_idx, *ins)


def _assemble(piece, parts, name):
    tr = 256
    cols = piece.shape[2]

    def body(p_ref, *outs):
        for k in range(NSH):
            for (off, width), o_ref in zip(parts, outs):
                o_ref[:, width * k:width * (k + 1)] = p_ref[k, :, off:off + width]

    return _call(
        body, name=name, grid=(D // tr,),
        in_specs=[pl.BlockSpec((NSH, tr, cols), lambda i: (0, i, 0))],
        out_specs=[pl.BlockSpec((tr, NSH * width), lambda i: (i, 0)) for _, width in parts],
        out_shape=[_sds((D, NSH * width), BF) for _, width in parts],
        args=(piece,))


def _blocks_of(rows, cols, itemsize):
    rb = rows
    while rb * cols * itemsize > (2 << 20) and rb % 2 == 0 and (rb // 2) % 16 == 0:
        rb //= 2
    return rb


def _add_half(full, recv, c_idx, name):
    p, r, cols = full.shape
    rb = _blocks_of(r // 2, cols, 4)
    nb = (r // 2) // rb

    def body(c_ref, a_ref, b_ref, o_ref):
        o_ref[...] = (a_ref[...].astype(F32) + b_ref[...].astype(F32)).astype(o_ref.dtype)

    grid_spec = pltpu.PrefetchScalarGridSpec(
        num_scalar_prefetch=1, grid=(p, nb),
        in_specs=[pl.BlockSpec((1, rb, cols), lambda j, i, c_ref: (j, c_ref[0] * nb + i, 0)),
                  pl.BlockSpec((1, rb, cols), lambda j, i, c_ref: (j, i, 0))],
        out_specs=pl.BlockSpec((1, rb, cols), lambda j, i, c_ref: (j, i, 0)))
    return pl.pallas_call(
        body, name=name, grid_spec=grid_spec, out_shape=_sds(recv.shape, full.dtype),
        compiler_params=_params(2),
    )(c_idx, full, recv)


def _sum_chips(own, got, plane_c, name):
    _, r, cols = own.shape
    rb = _blocks_of(r, cols, 4)
    nb = r // rb

    def body(s_ref, a_ref, b_ref, o_ref):
        f = lambda v: v.astype(F32)
        o_ref[...] = (f(a_ref[0]) + f(b_ref[1])) + (f(b_ref[0]) + f(b_ref[2]))

    grid_spec = pltpu.PrefetchScalarGridSpec(
        num_scalar_prefetch=1, grid=(nb,),
        in_specs=[pl.BlockSpec((1, rb, cols), lambda i, s_ref: (s_ref[0], i, 0)),
                  pl.BlockSpec((3, rb, cols), lambda i, s_ref: (0, i, 0))],
        out_specs=pl.BlockSpec((rb, cols), lambda i, s_ref: (s_ref[1] * nb + i, 0)))
    return pl.pallas_call(
        body, name=name, grid_spec=grid_spec, out_shape=_sds((2 * r, cols), F32),
        compiler_params=_params(1),
    )(plane_c, own, got)


def _adamw_math(w, g, m, v):
    m = ADAM_B1 * m + (1.0 - ADAM_B1) * g
    v = ADAM_B2 * v + (1.0 - ADAM_B2) * (g * g)
    m_hat = m / (1.0 - ADAM_B1 ** ADAM_STEP)
    v_hat = v / (1.0 - ADAM_B2 ** ADAM_STEP)
    delta = -ADAM_LR * (m_hat / (jnp.sqrt(v_hat) + ADAM_EPS) + ADAM_WD * w)
    return delta, m, v


def _adamw_big(w, m, v, gsrcs, gblock, gidx, name, rb):
    nl, r, cols = w.shape
    assert len(gsrcs) == nl

    def body(w_ref, m_ref, v_ref, *rest):
        g_refs, (go_ref, d_ref, mo_ref, vo_ref) = rest[:nl], rest[nl:]
        g = g_refs[0][:, 0:cols]
        if nl == 2:
            g = jnp.where(pl.program_id(0) == 0, g, g_refs[1][:, 0:cols])
        delta, mn, vn = _adamw_math(w_ref[0], g, m_ref[0], v_ref[0])
        go_ref[0] = g
        d_ref[0] = delta
        mo_ref[0] = mn
        vo_ref[0] = vn

    wspec = pl.BlockSpec((1, rb, cols), lambda l, i: (l, i, 0))
    gspec = pl.BlockSpec(gblock, lambda l, i: gidx(i))
    return pl.pallas_call(
        body, name=name, grid=(nl, r // rb),
        in_specs=[wspec, wspec, wspec] + [gspec] * nl,
        out_specs=[wspec] * 4, out_shape=[_sds(w.shape, F32)] * 4,
        compiler_params=_params(2),
    )(w, m, v, *gsrcs)


def _adamw_rows(ws, ms, vs, gsrcs, name, rb=64):
    n = len(ws)
    nl, r, cols = ws[0].shape
    nb = r // rb

    def body(*refs):
        w_refs, m_refs, v_refs = refs[:n], refs[n:2 * n], refs[2 * n:3 * n]
        g_refs = refs[3 * n:3 * n + nl * n]
        outs = refs[3 * n + nl * n:]
        for j in range(n):
            g = g_refs[j][...]
            for l in range(1, nl):
                g = jnp.where(pl.program_id(0) == l, g_refs[l * n + j][...], g)
            delta, mn, vn = _adamw_math(w_refs[j][0], g, m_refs[j][0], v_refs[j][0])
            for o_ref, val in zip(outs[4 * j:4 * j + 4], (g, delta, mn, vn)):
                o_ref[0] = val

    wspec = pl.BlockSpec((1, rb, cols), lambda l, i: (l, i, 0))
    gspecs = [pl.BlockSpec((rb, cols), lambda l, i, _j=j: (_j * nb + i, 0)) for _ in range(nl) for j in range(n)]
    outs = pl.pallas_call(
        body, name=name, grid=(nl, nb),
        in_specs=[wspec] * (3 * n) + gspecs,
        out_specs=[wspec] * (4 * n), out_shape=[_sds(ws[0].shape, F32)] * (4 * n),
        compiler_params=_params(2),
    )(*ws, *ms, *vs, *[gsrcs[l] for l in range(nl) for _ in range(n)])
    return [tuple(outs[4 * j:4 * j + 4]) for j in range(n)]


_SMALL = (
    ("g_mix", (2, D), "a", (0, 2), 0, False),
    ("g_xattn", (2, D), "a", (2, 4), 0, False),
    ("g_mem", (2, D), "a", (4, 6), 0, False),
    ("g_ffn", (2, D), "a", (6, 8), 0, False),
    ("g_final", (1, D), "a", (8, 9), 0, False),
    ("ev_a_conv_w", (KA, 128), "a", (16, 16 + KA), 0, True),
    ("ev_a_conv_b", (1, DA), "a", (11, 12), 0, False),
    ("ev_a_ln_g", (1, DA), "a", (11, 12), DA, False),
    ("ev_a_ln_b", (1, DA), "a", (12, 13), 0, False),
    ("ev_b_conv_w", (KB, 128), "a", (48, 48 + KB), 0, True),
    ("ev_b_conv_b", (1, DA), "a", (12, 13), DA, False),
    ("od_c_ln_g", (1, 256), "a", (9, 10), 0, True),
    ("od_c_ln_b", (1, 256), "a", (10, 11), 0, True),
    ("od_w_s", (GROUPS * CHUNK, CHUNK), "b", (0, GROUPS * CHUNK), 0, False),
    ("od_b_s", (GROUPS, CHUNK), "a", (56, 64), 0, False),
)


def _adamw_small(ga, gb, wmv):
    ns = len(_SMALL)

    def body(*refs):
        ga_ref, gb_ref = refs[0], refs[1]
        ins = refs[2:2 + 3 * ns]
        outs = refs[2 + 3 * ns:]
        x, y, _ = _pos()
        chip = 2 * x + y
        for j, (_, shape, pack, (r0, r1), col, sharded) in enumerate(_SMALL):
            src = ga_ref if pack == "a" else gb_ref
            width = shape[1]
            if sharded:
                g = jnp.zeros(shape, F32)
                for k in range(NSH):
                    blk = src[r0:r1, col + width * k:col + width * (k + 1)]
                    g = g + jnp.where(chip == k, blk, 0.0)
            else:
                g = src[r0:r1, col:col + width]
            w_ref, m_ref, v_ref = ins[3 * j:3 * j + 3]
            delta, mn, vn = _adamw_math(w_ref[...], g, m_ref[...], v_ref[...])
            go_ref, d_ref, mo_ref, vo_ref = outs[4 * j:4 * j + 4]
            go_ref[...] = g
            d_ref[...] = delta
            mo_ref[...] = mn
            vo_ref[...] = vn

    vm = pl.BlockSpec(memory_space=pltpu.VMEM)
    flat = [a for trio in wmv for a in trio]
    out_shape = []
    for (_, shape, *_rest) in _SMALL:
        out_shape += [_sds(shape, F32)] * 4
    outs = pl.pallas_call(
        body, name="adamw_small",
        in_specs=[vm] * (2 + 3 * ns), out_specs=[vm] * (4 * ns), out_shape=out_shape,
    )(ga, gb, *flat)
    return [tuple(outs[4 * j:4 * j + 4]) for j in range(ns)]


def _small_params(gs, W):
    unshard = lambda a: jnp.transpose(a, (1, 0, 2)).reshape(a.shape[1], NSH * a.shape[2])
    od_w_s, od_b_s = W["od_w_s"], W["od_b_s"]
    return dict(
        g_mix=W["g_mix"], g_xattn=W["g_xattn"], g_mem=W["g_mem"], g_ffn=W["g_ffn"],
        g_final=W["g_final"].reshape(1, D),
        wa=unshard(gs[:, 0:32, 0:128]), ba=W["ev_a_conv_b"], lng_a=W["ev_a_ln_g"], lnb_a=W["ev_a_ln_b"],
        wb=unshard(gs[:, 32:40, 0:128]), bb=W["ev_b_conv_b"],
        lng_c=unshard(gs[:, 40:41, :]), lnb_c=unshard(gs[:, 41:42, :]),
        ws=od_w_s[0].astype(BF), wst=jnp.swapaxes(od_w_s[0], 1, 2).astype(BF),
        bsb=jnp.broadcast_to(od_b_s[0][:, :, None], (GROUPS, CHUNK, CHUNK)),
    )


LOSS_ROW = 13


def _small_grad_packs(g, loss):
    z = lambda r, c: jnp.zeros((r, c), F32)
    wide = lambda a: jnp.concatenate([a, z(a.shape[0], D - a.shape[1])], axis=1)
    rows = [g["g_mix"], g["g_xattn"], g["g_mem"], g["g_ffn"], g["g_final"], g["od_c_ln_g"], g["od_c_ln_b"],
            jnp.concatenate([g["ev_a_conv_b"], g["ev_a_ln_g"]], axis=1),
            jnp.concatenate([g["ev_a_ln_b"], g["ev_b_conv_b"]], axis=1),
            wide(loss), z(2, D), wide(g["ev_a_conv_w"]), wide(g["ev_b_conv_w"]), wide(g["od_b_s"])]
    a = jnp.concatenate(rows, axis=0)
    assert a.shape == (SA_ROWS, D), a.shape
    return a, g["od_w_s"].reshape(GROUPS * CHUNK, CHUNK)


def _step(x, mem, tgt, W, P, c_idx, tm_f, tm_b, tm_w):
    row = lambda a, i: a[i:i + 1]

    sm = _small_params(P["gs"], W)
    (wev,) = _assemble(P["cev"], [(0, 640)], "assemble_ev")
    z, evo, wg0 = _ev_in_fwd(x, row(sm["g_mix"], 0), wev, 2 * tm_f, carry=("gather", [P["evo"], P["wg0"]]))
    h1, a2, c2, cat, xa0, wu0 = _ev_mix_fwd(
        x, z, sm["wa"], sm["ba"], sm["lng_a"], sm["lnb_a"], sm["wb"], sm["bb"], evo, tm_f,
        carry=("gather", [P["xa0"], P["wu0"]]))
    k0, v0 = _kv_proj(mem, row(sm["g_mem"], 0), xa0, 0)
    h2, q0, dn0 = _xattn_fwd(h1, row(sm["g_xattn"], 0), k0, v0, xa0, 0, 2 * tm_f, carry=("gather", [P["dn0"]]))
    h3, gt0, up0, odo, xa1, wg1, cod = _ffn_fwd(h2, row(sm["g_ffn"], 0), wg0, wu0, dn0, 0, tm_f,
                                                carry=("gather", [P["odo"], P["xa1"], P["wg1"], P["cod"]]))
    (wod,) = _assemble(cod, [(0, 512)], "assemble_od")
    h4, zpre, wu1, dn1 = _sgu_fwd(h3, row(sm["g_mix"], 1), sm["lng_c"], sm["lnb_c"], sm["ws"], sm["bsb"], wod,
                                  odo, tm_f, carry=("gather", [P["wu1"], P["dn1"]]))
    k1, v1 = _kv_proj(mem, row(sm["g_mem"], 1), xa1, 1)
    h5, q1 = _xattn_fwd(h4, row(sm["g_xattn"], 1), k1, v1, xa1, 1, 2 * tm_f)
    dh6, loss, dg_final, gt1, up1 = _ffn_fwd_loss(h5, row(sm["g_ffn"], 1), wg1, wu1, dn1, tgt, sm["g_final"], 1,
                                                  tm_f)

    g = {"g_final": dg_final}
    parts, gots = {}, {}
    def add(name, piece, recv):
        parts[name] = _add_half(piece, recv, c_idx, f"rs_add_{name}")
        return parts[name]

    def ffn_wgrads(layer, dgt, dup, act, nb, dh_out, carry=None):
        g_ffn, *carried = _ffn_wgrad(dgt, nb, f"ffn_wgrad_gate{layer}", tm_w, 0, carry=carry)
        (g_ffn,) = _ffn_wgrad(dup, nb, f"ffn_wgrad_up{layer}", tm_w, 1, into=g_ffn)
        (g_ffn,) = _ffn_wgrad(act, dh_out, f"ffn_wgrad_down{layer}", tm_w, 2, into=g_ffn)
        return g_ffn, carried

    dh5, dg_ffn1, dgt, dup, act, nb = _ffn_bwd_dx(h5, dh6, gt1, up1, row(sm["g_ffn"], 1), wg1, wu1, dn1, 1, tm_b)
    g_ffn1, _ = ffn_wgrads(1, dgt, dup, act, nb, dh6)
    dh4, dk, dv, dg_xa1, g_xa1, recv = _xattn_bwd(
        h4, dh5, q1, row(sm["g_xattn"], 1), k1, v1, xa1, 1, tm_f, carry=("swap", [g_ffn1]))
    add("ffn1", g_ffn1, recv)
    dg_mem1, g_xa1 = _kv_proj_bwd(mem, row(sm["g_mem"], 1), dk, dv, xa1, g_xa1, 1)
    (dh3, g["od_w_s"], g["od_b_s"], g["od_c_ln_g"], g["od_c_ln_b"], dg_mix1, g_cod, g_odo, gots["ffn1"],
     recv) = _sgu_bwd(
        h3, dh4, zpre, row(sm["g_mix"], 1), sm["lng_c"], sm["lnb_c"], sm["ws"], sm["wst"], sm["bsb"], wod, odo, tm_b,
        carry=[("xchg", [parts["ffn1"]], []), ("swap", [g_xa1])])
    add("xa1", g_xa1, recv)
    dh2, dg_ffn0, dgt, dup, act, nb, gots["xa1"], recv_cod, recv_odo = _ffn_bwd_dx(
        h2, dh3, gt0, up0, row(sm["g_ffn"], 0), wg0, wu0, dn0, 0, tm_b,
        carry=[("xchg", [parts["xa1"]], []), ("swap", [g_cod, g_odo])])
    add("cod", g_cod, recv_cod)
    add("odo", g_odo, recv_odo)
    g_ffn0, (gots["cod"], gots["odo"]) = ffn_wgrads(0, dgt, dup, act, nb, dh3,
                                                    carry=("xchg", [parts["cod"], parts["odo"]], []))
    dh1, dk, dv, dg_xa0, g_xa0, recv = _xattn_bwd(
        h1, dh2, q0, row(sm["g_xattn"], 0), k0, v0, xa0, 0, tm_f, carry=("swap", [g_ffn0]))
    add("ffn0", g_ffn0, recv)
    dg_mem0, g_xa0 = _kv_proj_bwd(mem, row(sm["g_mem"], 0), dk, dv, xa0, g_xa0, 0)
    da2, dc2, dgb, g["ev_a_ln_g"], g["ev_a_ln_b"], g_evo, recv = _ev_mix_bwd(
        dh1, cat, a2, c2, z, sm["lng_a"], sm["lnb_a"], evo, tm_f, carry=("swap", [g_xa0]))
    add("xa0", g_xa0, recv)
    handle, token = _comm_start([parts["ffn0"], parts["xa0"]], [], [g_evo], "ev")
    dx, g["ev_a_conv_w"], g["ev_a_conv_b"], g["ev_b_conv_w"], g["ev_b_conv_b"], dg_mix0, g_cev = _ev_in_bwd(
        x, dh1, z, da2, dc2, dgb, row(sm["g_mix"], 0), sm["wa"], sm["wb"], wev, tm_b, token)
    (parts["ffn0"], parts["xa0"], g_evo), (gots["ffn0"], gots["xa0"], recv) = _comm_wait(handle, g_cev)
    add("evo", g_evo, recv)
    g["g_mix"] = jnp.concatenate([dg_mix0, dg_mix1], axis=0)
    g["g_xattn"] = jnp.concatenate([dg_xa0, dg_xa1], axis=0)
    g["g_mem"] = jnp.concatenate([dg_mem0, dg_mem1], axis=0)
    g["g_ffn"] = jnp.concatenate([dg_ffn0, dg_ffn1], axis=0)
    ga, gb = _small_grad_packs(g, loss)
    last = (g_cev, ga[None], gb[None])
    for name, piece, recv in zip(("cev", "ga", "gb"), last, _sibling_swap(last, "rs_swap_last")):
        add(name, piece, recv)
    return dx, parts, gots


def kernel(x, mem, g_mix, g_xattn, g_mem, g_ffn, g_final, ev_w_in, ev_a_conv_w, ev_a_conv_b, ev_a_ln_g, ev_a_ln_b, ev_b_conv_w, ev_b_conv_b, ev_w_out, od_w_in, od_c_ln_g, od_c_ln_b, od_w_s, od_b_s, od_w_out, xa_w_q, xa_w_k, xa_w_v, xa_w_o, ffn_w_gate, ffn_w_up, ffn_w_down, loss_target, m_g_mix, m_g_xattn, m_g_mem, m_g_ffn, m_g_final, m_ev_w_in, m_ev_a_conv_w, m_ev_a_conv_b, m_ev_a_ln_g, m_ev_a_ln_b, m_ev_b_conv_w, m_ev_b_conv_b, m_ev_w_out, m_od_w_in, m_od_c_ln_g, m_od_c_ln_b, m_od_w_s, m_od_b_s, m_od_w_out, m_xa_w_q, m_xa_w_k, m_xa_w_v, m_xa_w_o, m_ffn_w_gate, m_ffn_w_up, m_ffn_w_down, v_g_mix, v_g_xattn, v_g_mem, v_g_ffn, v_g_final, v_ev_w_in, v_ev_a_conv_w, v_ev_a_conv_b, v_ev_a_ln_g, v_ev_a_ln_b, v_ev_b_conv_w, v_ev_b_conv_b, v_ev_w_out, v_od_w_in, v_od_c_ln_g, v_od_c_ln_b, v_od_w_s, v_od_b_s, v_od_w_out, v_xa_w_q, v_xa_w_k, v_xa_w_v, v_xa_w_o, v_ffn_w_gate, v_ffn_w_up, v_ffn_w_down):
    W = dict(g_mix=g_mix, g_xattn=g_xattn, g_mem=g_mem, g_ffn=g_ffn, g_final=g_final, ev_w_in=ev_w_in,
             ev_a_conv_w=ev_a_conv_w, ev_a_conv_b=ev_a_conv_b, ev_a_ln_g=ev_a_ln_g, ev_a_ln_b=ev_a_ln_b,
             ev_b_conv_w=ev_b_conv_w, ev_b_conv_b=ev_b_conv_b, ev_w_out=ev_w_out, od_w_in=od_w_in,
             od_c_ln_g=od_c_ln_g, od_c_ln_b=od_c_ln_b, od_w_s=od_w_s, od_b_s=od_b_s, od_w_out=od_w_out,
             xa_w_q=xa_w_q, xa_w_k=xa_w_k, xa_w_v=xa_w_v, xa_w_o=xa_w_o, ffn_w_gate=ffn_w_gate,
             ffn_w_up=ffn_w_up, ffn_w_down=ffn_w_down)
    M = dict(g_mix=m_g_mix, g_xattn=m_g_xattn, g_mem=m_g_mem, g_ffn=m_g_ffn, g_final=m_g_final, ev_w_in=m_ev_w_in,
             ev_a_conv_w=m_ev_a_conv_w, ev_a_conv_b=m_ev_a_conv_b, ev_a_ln_g=m_ev_a_ln_g, ev_a_ln_b=m_ev_a_ln_b,
             ev_b_conv_w=m_ev_b_conv_w, ev_b_conv_b=m_ev_b_conv_b, ev_w_out=m_ev_w_out, od_w_in=m_od_w_in,
             od_c_ln_g=m_od_c_ln_g, od_c_ln_b=m_od_c_ln_b, od_w_s=m_od_w_s, od_b_s=m_od_b_s, od_w_out=m_od_w_out,
             xa_w_q=m_xa_w_q, xa_w_k=m_xa_w_k, xa_w_v=m_xa_w_v, xa_w_o=m_xa_w_o, ffn_w_gate=m_ffn_w_gate,
             ffn_w_up=m_ffn_w_up, ffn_w_down=m_ffn_w_down)
    V = dict(g_mix=v_g_mix, g_xattn=v_g_xattn, g_mem=v_g_mem, g_ffn=v_g_ffn, g_final=v_g_final, ev_w_in=v_ev_w_in,
             ev_a_conv_w=v_ev_a_conv_w, ev_a_conv_b=v_ev_a_conv_b, ev_a_ln_g=v_ev_a_ln_g, ev_a_ln_b=v_ev_a_ln_b,
             ev_b_conv_w=v_ev_b_conv_w, ev_b_conv_b=v_ev_b_conv_b, ev_w_out=v_ev_w_out, od_w_in=v_od_w_in,
             od_c_ln_g=v_od_c_ln_g, od_c_ln_b=v_od_c_ln_b, od_w_s=v_od_w_s, od_b_s=v_od_b_s, od_w_out=v_od_w_out,
             xa_w_q=v_xa_w_q, xa_w_k=v_xa_w_k, xa_w_v=v_xa_w_v, xa_w_o=v_xa_w_o, ffn_w_gate=v_ffn_w_gate,
             ffn_w_up=v_ffn_w_up, ffn_w_down=v_ffn_w_down)
    order = list(W)
    cx, cy, cc = _pos()
    chip = (2 * cx + cy).astype(jnp.int32)
    cc = cc.astype(jnp.int32)
    chip_idx = jnp.reshape(chip, (1,))
    c_idx = jnp.reshape(cc, (1,))

    tr = lambda a: jnp.swapaxes(a, 1, 2)
    P = {}
    cev, P["cod"], gs = _pack_cols(chip_idx, od_w_in, ev_w_in, ev_a_conv_w, ev_b_conv_w, od_c_ln_g, od_c_ln_b)
    ((P["evo"], P["odo"], P["dn0"], P["dn1"], P["wg0"], P["wg1"], P["wu0"], P["wu1"], P["xa0"], P["xa1"]),
     (P["cev"], P["gs"])) = _pack_rows(chip_idx, ffn_w_down, tr(ffn_w_gate), tr(ffn_w_up), xa_w_q, xa_w_o,
                                       xa_w_k, xa_w_v, ev_w_out, od_w_out, first=[cev, gs])

    dx, parts, gots = _step(x[0], mem[0], loss_target[0], W, P, c_idx, TM_FWD, TM_BWD, TM_WGRAD)

    chip_c = jnp.stack([chip, cc])
    zero_c = jnp.stack([jnp.zeros((), jnp.int32), cc])
    def sums(names):
        return [_sum_chips(parts[n], gots[n], zero_c if n in ("ga", "gb") else chip_c, f"rs_sum_{n}")
                for n in names]

    last = ("evo", "cev", "ga", "gb")
    handle, token = _comm_start([parts["evo"], parts["cev"]], [parts["ga"][0], parts["gb"][0]], [], "last")
    big_names, rest_names = ["ffn1", "ffn0"], ["xa1", "xa0", "cod", "odo"]
    join_big, token = _join_start(sums(big_names), "ffn", after=(token,))
    join_rest, token = _join_start(sums(rest_names), "rest", after=(token,))
    red = dict(zip(big_names, _join_wait(join_big, token)))

    res = {}

    def adamw(name, srcs, gblock, gidx, rb):
        t = tr if name in ("ffn_w_gate", "ffn_w_up") else (lambda a: a)
        outs = _adamw_big(t(W[name]), t(M[name]), t(V[name]), srcs, gblock, gidx, f"adamw_{name}", rb)
        res[name] = tuple(t(o) for o in outs)

    ffn = [red["ffn0"], red["ffn1"]]
    adamw("ffn_w_gate", ffn, (FSH, D), lambda i: (0, 0), FSH)
    adamw("ffn_w_up", ffn, (FSH, D), lambda i: (1, 0), FSH)
    adamw("ffn_w_down", ffn, (FSH, D), lambda i: (2, 0), FSH)
    red.update(zip(rest_names, _join_wait(join_rest, res["ffn_w_down"][1])))
    xa_names = ("xa_w_q", "xa_w_o", "xa_w_k", "xa_w_v")
    assert (XQ, XO, XK, XV) == (0, 1, 2, 3)
    for name, outs in zip(xa_names, _adamw_rows([W[n] for n in xa_names], [M[n] for n in xa_names],
                                                [V[n] for n in xa_names], [red["xa0"], red["xa1"]], "adamw_xa")):
        res[name] = outs
    adamw("od_w_out", [red["odo"]], (256, D), lambda i: (0, 0), 256)
    adamw("od_w_in", [red["cod"]], (256, 512), lambda i: (i, 0), 256)

    srcs_done, got_last = _comm_wait(handle, res["od_w_in"][1])
    parts.update(zip(last, srcs_done[:2] + [a[None] for a in srcs_done[2:]]))
    gots.update(zip(last, got_last))
    red.update(zip(last, _sibling_join(sums(last), "rs_join_last")))
    adamw("ev_w_out", [red["evo"]], (256, D), lambda i: (0, 0), 256)
    adamw("ev_w_in", [red["cev"]], (256, 640), lambda i: (i, 0), 256)
    shape2 = {name: shape for name, shape, *_ in _SMALL}
    wmv = [tuple(d[name].reshape(shape2[name]) for d in (W, M, V)) for name, *_ in _SMALL]
    for (name, *_), outs in zip(_SMALL, _adamw_small(red["ga"], red["gb"], wmv)):
        res[name] = tuple(o.reshape(W[name].shape) for o in outs)

    grad_x = dx[None]
    loss = red["ga"][LOSS_ROW, 0]
    return (loss, grad_x, *[res[n][0] for n in order], *[res[n][1] for n in order],
            *[res[n][2] for n in order], *[res[n][3] for n in order])
```

```python
import jax
import jax.numpy as jnp
from jax import lax
from jax.experimental import pallas as pl
from jax.experimental.pallas import tpu as pltpu

BF = jnp.bfloat16
F32 = jnp.float32

D = 1024
DA = 512
DZ = 2560
DFF = 2816
NSH = 4
FSH = DFF // NSH
FF_CHUNKS = ((0, 768), (768, 1536), (1536, 2304), (2304, 2816))
NMEM = 256
HEADS = 4
HD = D // HEADS
CHUNK = 128
GROUPS = 8
KA = 31
KB = 3
HALO = 16
RC = 32
RMS_EPS = 1e-6
LN_EPS = 1e-5
ATT_SCALE = HD ** -0.5
TM_FWD = 512
TM_BWD = 256
TM_WGRAD = 1024

ADAM_LR = 0.001
ADAM_B1 = 0.9
ADAM_B2 = 0.999
ADAM_EPS = 1e-08
ADAM_WD = 0.01
ADAM_STEP = 10

XQ, XO, XK, XV = 0, 1, 2, 3
SP_ROWS = 48
SA_ROWS = 64

MESH = pl.DeviceIdType.MESH
ANY = pl.BlockSpec(memory_space=pl.ANY)


def _dot(a, b):
    return jnp.dot(a, b, preferred_element_type=F32)


def _dot_nt(a, b):
    return lax.dot_general(a, b, (((1,), (1,)), ((), ())), preferred_element_type=F32)


def _dot_tn(a, b):
    return lax.dot_general(a, b, (((0,), (0,)), ((), ())), preferred_element_type=F32)


def _rms(h):
    r = lax.rsqrt(jnp.mean(h * h, axis=-1, keepdims=True) + RMS_EPS)
    return h * r, r


def _rms_bwd(dn, hhat, r, g):
    dhh = dn * g
    return r * (dhh - hhat * jnp.mean(dhh * hhat, axis=-1, keepdims=True))


def _ln(x):
    mu = jnp.mean(x, axis=-1, keepdims=True)
    xc = x - mu
    rstd = lax.rsqrt(jnp.mean(xc * xc, axis=-1, keepdims=True) + LN_EPS)
    return xc * rstd, rstd


def _ln_bwd(dy, y, rstd):
    return rstd * (dy - jnp.mean(dy, axis=-1, keepdims=True) - y * jnp.mean(dy * y, axis=-1, keepdims=True))


def _colsum(x):
    return jnp.sum(x, axis=0, keepdims=True)


def _sigmoid(x):
    return 1.0 / (1.0 + jnp.exp(-x))


_GELU_C = 0.7978845608028654
_GELU_A = 0.044715


def _gelu(x):
    t = jnp.tanh(_GELU_C * (x + _GELU_A * x * x * x))
    return 0.5 * x * (1.0 + t)


def _gelu_grad(x):
    t = jnp.tanh(_GELU_C * (x + _GELU_A * x * x * x))
    return 0.5 * (1.0 + t) + 0.5 * x * (1.0 - t * t) * _GELU_C * (1.0 + 3.0 * _GELU_A * x * x)


def _first(i):
    return i == 0


def _last(i):
    return i == pl.num_programs(0) - 1


def _const(shape, idx):
    return pl.BlockSpec(shape, lambda *_, _idx=tuple(idx): _idx, pipeline_mode=pl.Buffered(1))


def _whole(a):
    return _const(a.shape, (0,) * a.ndim)


def _rows(tm, width, col=0):
    return pl.BlockSpec((tm, width), lambda i, _c=col: (i, _c))


def _params(ndim=1, vmem_mb=56):
    return pltpu.CompilerParams(dimension_semantics=("arbitrary",) * ndim, vmem_limit_bytes=vmem_mb << 20)


def _sds(shape, dtype):
    return jax.ShapeDtypeStruct(shape, dtype)


def _pos():
    return lax.axis_index("x"), lax.axis_index("y"), lax.axis_index("c")


def _other_chips(x, y):
    return [(1 - x, y), (x, 1 - y), (1 - x, 1 - y)]


def _half(ref, hc, lead=()):
    r = ref.shape[-2] // 2
    return ref.at[(*lead, pl.ds(pl.multiple_of(hc * r, 8), r), slice(None))]


def _rcopy(src, dst, ssem, rsem, to):
    return pltpu.make_async_remote_copy(src_ref=src, dst_ref=dst, send_sem=ssem, recv_sem=rsem,
                                        device_id=to, device_id_type=MESH)


def _gather_start(bufs, sems):
    ici_s, ici_r, _, _ = sems
    x, y, c = _pos()
    me = 2 * x + y
    for k, (cx, cy) in enumerate(_other_chips(x, y)):
        for t, buf in enumerate(bufs):
            mine = _half(buf, c, (me,))
            _rcopy(mine, mine, ici_s.at[3 * t + k], ici_r.at[3 * t + k], (cx, cy, c)).start()


def _gather_forward(bufs, sems):
    ici_s, ici_r, d2d_s, d2d_r = sems
    x, y, c = _pos()
    sib = (x, y, 1 - c)
    for k, (cx, cy) in enumerate(_other_chips(x, y)):
        for t, buf in enumerate(bufs):
            landed = _half(buf, c, (2 * cx + cy,))
            _rcopy(landed, landed, ici_s.at[3 * t + k], ici_r.at[3 * t + k], (cx, cy, c)).wait_recv()
            _rcopy(landed, landed, d2d_s.at[3 * t + k], d2d_r.at[3 * t + k], sib).start()


def _gather_drain(bufs, sems):
    ici_s, ici_r, d2d_s, d2d_r = sems
    x, y, c = _pos()
    me = 2 * x + y
    sib = (x, y, 1 - c)
    chips = _other_chips(x, y)
    for k, (cx, cy) in enumerate(chips):
        for t, buf in enumerate(bufs):
            got = _half(buf, 1 - c, (2 * cx + cy,))
            _rcopy(got, got, d2d_s.at[3 * t + k], d2d_r.at[3 * t + k], sib).wait_recv()
    for k, (cx, cy) in enumerate(chips):
        for t, buf in enumerate(bufs):
            mine = _half(buf, c, (me,))
            _rcopy(mine, mine, ici_s.at[3 * t + k], ici_r.at[3 * t + k], (cx, cy, c)).wait_send()
            landed = _half(buf, c, (2 * cx + cy,))
            _rcopy(landed, landed, d2d_s.at[3 * t + k], d2d_r.at[3 * t + k], sib).wait_send()


def _xchg_copies(srcs, dsts, nbig, sems):
    ssem, rsem = sems
    x, y, c = _pos()
    cps = []
    for k, (cx, cy) in enumerate(_other_chips(x, y)):
        for t, (src, dst) in enumerate(zip(srcs, dsts)):
            s = src.at[2 * cx + cy] if t < nbig else src
            cps.append(_rcopy(s, dst.at[k], ssem.at[3 * t + k], rsem.at[3 * t + k], (cx, cy, c)))
    return cps


def _gather_sems(n):
    return [pltpu.SemaphoreType.DMA((3 * n,))] * 4


def _xchg_sems(n):
    return [pltpu.SemaphoreType.DMA((3 * n,))] * 2


def _xchg_out_shapes(items):
    return [_sds((3,) + a.shape[-2:], a.dtype) for a in items]


def _swap_copies(srcs, dsts, sems):
    ssem, rsem = sems
    x, y, c = _pos()
    cps = []
    for t, (src, dst) in enumerate(zip(srcs, dsts)):
        lead = (slice(None),) * (len(src.shape) - 2)
        cps.append(_rcopy(_half(src, 1 - c, lead), dst, ssem.at[t], rsem.at[t], (x, y, 1 - c)))
    return cps


def _swap_out_shapes(items):
    return [_sds(a.shape[:-2] + (a.shape[-2] // 2, a.shape[-1]), a.dtype) for a in items]


def _call(body, *, name, grid, in_specs, out_specs, out_shape, args, scratch_shapes=(), aliases=None, carry=None):
    in_specs, out_specs, out_shape = list(in_specs), list(out_specs), list(out_shape)
    args = list(args)
    aliases = dict(aliases or {})
    n_in, n_out, n_scr = len(args), len(out_shape), len(scratch_shapes)
    carries = [] if not carry else ([carry] if isinstance(carry, tuple) else list(carry))
    plans = []
    cin, cout_shape, sems = [], [], []
    for c in carries:
        kind = c[0]
        items = list(c[1]) + (list(c[2]) if kind == "xchg" else [])
        if kind == "gather":
            shapes, sm = [_sds(a.shape, a.dtype) for a in items], _gather_sems(len(items))
            for t in range(len(items)):
                aliases[n_in + len(cin) + t] = n_out + len(cin) + t
        elif kind == "xchg":
            shapes, sm = _xchg_out_shapes(items), _xchg_sems(len(items))
        else:
            shapes, sm = _swap_out_shapes(items), [pltpu.SemaphoreType.DMA((len(items),))] * 2
        plans.append((kind, len(cin), len(items), len(sems), len(sm), len(c[1])))
        cin += items
        cout_shape += shapes
        sems += sm
    nc = len(cin)

    def wrapped(*refs):
        ins = refs[:n_in]
        c_in = refs[n_in:n_in + nc]
        outs = refs[n_in + nc:n_in + nc + n_out]
        c_out = refs[n_in + nc + n_out:n_in + 2 * nc + n_out]
        scr = refs[n_in + 2 * nc + n_out:n_in + 2 * nc + n_out + n_scr]
        sem = refs[n_in + 2 * nc + n_out + n_scr:]
        i = pl.program_id(0)

        def copies(plan):
            kind, o0, no, s0, ns, nbig = plan
            if kind == "xchg":
                return _xchg_copies(c_in[o0:o0 + no], c_out[o0:o0 + no], nbig, sem[s0:s0 + ns])
            return _swap_copies(c_in[o0:o0 + no], c_out[o0:o0 + no], sem[s0:s0 + ns])

        if plans:
            @pl.when(_first(i))
            def _():
                for plan in plans:
                    kind, o0, no, s0, ns, _ = plan
                    if kind == "gather":
                        _gather_start(c_out[o0:o0 + no], sem[s0:s0 + ns])
                    else:
                        for cp in copies(plan):
                            cp.start()
        body(*ins, *outs, *scr)
        for plan in plans:
            kind, o0, no, s0, ns, _ = plan
            if kind == "gather":
                @pl.when(i == max(grid[0] - 2, 0))
                def _(o0=o0, no=no, s0=s0, ns=ns):
                    _gather_forward(c_out[o0:o0 + no], sem[s0:s0 + ns])
        if plans:
            @pl.when(_last(i))
            def _():
                for plan in plans:
                    kind, o0, no, s0, ns, _ = plan
                    if kind == "gather":
                        _gather_drain(c_out[o0:o0 + no], sem[s0:s0 + ns])
                    else:
                        for cp in copies(plan):
                            cp.wait()

    outs = pl.pallas_call(
        wrapped, name=name, grid=grid,
        in_specs=in_specs + [ANY] * nc, out_specs=out_specs + [ANY] * nc,
        out_shape=out_shape + cout_shape,
        scratch_shapes=list(scratch_shapes) + sems,
        input_output_aliases=aliases,
        compiler_params=_params(len(grid)),
    )(*args, *cin)
    return list(outs)


def _sibling_swap(items, name):
    n = len(items)

    def body(*refs):
        src = refs[:n]
        dst = refs[n:2 * n]
        ssem, rsem = refs[2 * n:]
        x, y, c = _pos()
        sib = (x, y, 1 - c)
        cps = []
        for t in range(n):
            lead = (slice(None),) * (len(src[t].shape) - 2)
            cp = _rcopy(_half(src[t], 1 - c, lead), dst[t], ssem.at[t], rsem.at[t], sib)
            cp.start()
            cps.append(cp)
        for cp in cps:
            cp.wait()

    return pl.pallas_call(
        body, name=name, in_specs=[ANY] * n, out_specs=[ANY] * n,
        out_shape=[_sds(a.shape[:-2] + (a.shape[-2] // 2, a.shape[-1]), a.dtype) for a in items],
        scratch_shapes=[pltpu.SemaphoreType.DMA((n,)), pltpu.SemaphoreType.DMA((n,))],
    )(*items)


_HBM = pl.BlockSpec(memory_space=pltpu.HBM)
_SEM = pl.BlockSpec(memory_space=pltpu.SEMAPHORE)
_EFFECT = pltpu.SideEffectType.DATAFLOW_SIDE_EFFECTING


def _split_copies(src, land, nbig, nx, sems):
    return (_xchg_copies(src[:nx], land[:nx], nbig, sems[0:2]) if nx else []) + \
        (_swap_copies(src[nx:], land[nx:], sems[2:4]) if len(src) > nx else [])


def _comm_start(big, small, swaps, tag):
    xs = list(big) + list(small)
    items = xs + list(swaps)
    n, nx = len(items), len(xs)

    def body(*refs):
        for cp in _split_copies(refs[:n], refs[n:2 * n], len(big), nx, refs[2 * n:2 * n + 4]):
            cp.start()
        refs[-1][...] = jnp.zeros_like(refs[-1])

    lands = _xchg_out_shapes(xs) + _swap_out_shapes(swaps)
    sem_shapes = [pltpu.SemaphoreType.DMA((max(3 * nx, 1),))] * 2 + [pltpu.SemaphoreType.DMA((max(n - nx, 1),))] * 2
    hbm = lambda a: pltpu.with_memory_space_constraint(a, pltpu.HBM)
    outs = pl.pallas_call(
        body, name=f"rs_start_{tag}",
        out_shape=sem_shapes + [pltpu.HBM(a.shape, a.dtype) for a in items]
        + [pltpu.HBM(a.shape, a.dtype) for a in lands] + [_sds((8, 128), F32)],
        in_specs=[_HBM] * (2 * n), out_specs=[_SEM] * 4 + [_HBM] * (2 * n) + [pl.BlockSpec(memory_space=pltpu.VMEM)],
        input_output_aliases={t: 4 + t for t in range(2 * n)},
        compiler_params=pltpu.CompilerParams(has_side_effects=_EFFECT),
    )(*[hbm(a) for a in items], *[hbm(lax.empty(a.shape, a.dtype)) for a in lands])
    handle = (list(outs[:4]), list(outs[4:4 + n]), list(outs[4 + n:4 + 2 * n]), len(big), nx, tag)
    return handle, outs[-1]


def _comm_wait(handle, after):
    sems, srcs, lands, nbig, nx, tag = handle
    n = len(srcs)

    def body(*refs):
        for cp in _split_copies(refs[:n], refs[n:2 * n], nbig, nx, refs[2 * n:2 * n + 4]):
            cp.wait_send()
            cp.wait_recv()

    outs = pl.pallas_call(
        body, name=f"rs_wait_{tag}",
        out_shape=[pltpu.HBM(a.shape, a.dtype) for a in srcs] + [pltpu.HBM(a.shape, a.dtype) for a in lands],
        in_specs=[_HBM] * (2 * n) + [_SEM] * 4 + [ANY], out_specs=[_HBM] * (2 * n),
        input_output_aliases={t: t for t in range(2 * n)},
        compiler_params=pltpu.CompilerParams(has_side_effects=_EFFECT),
    )(*srcs, *lands, *sems, after)
    return list(outs[:n]), list(outs[n:])


def _join_copies(items, sems):
    ssem, rsem = sems
    x, y, c = _pos()
    return [_rcopy(_half(a, c), _half(a, c), ssem.at[t], rsem.at[t], (x, y, 1 - c)) for t, a in enumerate(items)]


def _join_start(items, tag, after=()):
    n = len(items)

    def body(*refs):
        for cp in _join_copies(refs[:n], refs[n + len(after):n + len(after) + 2]):
            cp.start()
        refs[-1][...] = jnp.zeros_like(refs[-1])

    hbm = lambda a: pltpu.with_memory_space_constraint(a, pltpu.HBM)
    outs = pl.pallas_call(
        body, name=f"rs_join_start_{tag}",
        out_shape=[pltpu.SemaphoreType.DMA((n,))] * 2 + [pltpu.HBM(a.shape, a.dtype) for a in items]
        + [_sds((8, 128), F32)],
        in_specs=[_HBM] * n + [ANY] * len(after),
        out_specs=[_SEM] * 2 + [_HBM] * n + [pl.BlockSpec(memory_space=pltpu.VMEM)],
        input_output_aliases={t: 2 + t for t in range(n)},
        compiler_params=pltpu.CompilerParams(has_side_effects=_EFFECT),
    )(*[hbm(a) for a in items], *after)
    return (list(outs[:2]), list(outs[2:2 + n]), tag), outs[-1]


def _join_wait(handle, after):
    sems, items, tag = handle
    n = len(items)

    def body(*refs):
        x, y, c = _pos()
        ssem, rsem = refs[n:n + 2]
        for t, cp in enumerate(_join_copies(refs[:n], (ssem, rsem))):
            cp.wait_send()
            theirs = _half(refs[t], 1 - c)
            _rcopy(theirs, theirs, ssem.at[t], rsem.at[t], (x, y, 1 - c)).wait_recv()

    outs = pl.pallas_call(
        body, name=f"rs_join_wait_{tag}",
        out_shape=[pltpu.HBM(a.shape, a.dtype) for a in items],
        in_specs=[_HBM] * n + [_SEM] * 2 + [ANY], out_specs=[_HBM] * n,
        input_output_aliases={t: t for t in range(n)},
        compiler_params=pltpu.CompilerParams(has_side_effects=_EFFECT),
    )(*items, *sems, after)
    return list(outs)


def _sibling_join(items, name, after=()):
    n = len(items)

    def body(*refs):
        dst = refs[n + len(after):2 * n + len(after)]
        ssem, rsem = refs[2 * n + len(after):]
        x, y, c = _pos()
        sib = (x, y, 1 - c)
        cps = []
        for t in range(n):
            mine = _half(dst[t], c)
            cp = _rcopy(mine, mine, ssem.at[t], rsem.at[t], sib)
            cp.start()
            cps.append(cp)
        for t, cp in enumerate(cps):
            theirs = _half(dst[t], 1 - c)
            _rcopy(theirs, theirs, ssem.at[t], rsem.at[t], sib).wait_recv()
            cp.wait_send()

    outs = pl.pallas_call(
        body, name=name, in_specs=[ANY] * (n + len(after)), out_specs=[ANY] * n,
        out_shape=[_sds(a.shape, a.dtype) for a in items],
        input_output_aliases={t: t for t in range(n)},
        scratch_shapes=[pltpu.SemaphoreType.DMA((n,))] * 2,
    )(*items, *after)
    return list(outs)


def _ev_in_fwd(h, g, wev, tm, carry=None):
    s = h.shape[0]

    def body(h_ref, g_ref, w_ref, z_ref):
        hhat, _ = _rms(h_ref[...])
        n = (hhat * g_ref[...]).astype(BF)
        z_ref[...] = _dot(n, w_ref[...]).astype(BF)

    return _call(
        body, name="ev_in_fwd", grid=(s // tm,),
        in_specs=[_rows(tm, D), _const((1, D), (0, 0)), _const((D, DZ), (0, 0))],
        out_specs=[_rows(tm, DZ)], out_shape=[_sds((s, DZ), BF)],
        args=(h, g, wev), carry=carry)


def _halo_specs(tm, width, s, col=0):
    per = tm // HALO
    nh = s // HALO
    prev = pl.BlockSpec((HALO, width), lambda i, _c=col: (jnp.maximum(i * per - 1, 0), _c))
    nxt = pl.BlockSpec((HALO, width), lambda i, _c=col: (jnp.minimum((i + 1) * per, nh - 1), _c))
    return prev, nxt


def _fill_pad(pad_ref, prev, main, nxt, i, tm):
    pad_ref[pl.ds(0, HALO), :] = jnp.where(_first(i), 0.0, prev)
    pad_ref[pl.ds(HALO, tm), :] = main
    pad_ref[pl.ds(HALO + tm, HALO), :] = jnp.where(_last(i), 0.0, nxt)


def _shift8(xs_ref, pad_ref, tm):
    for j in range(8):
        xs_ref[j] = pad_ref[pl.ds(j, tm + 2 * HALO - 8), :]


def _tap(xs_ref, r, off):
    return xs_ref[off % 8, pl.ds(r + 8 * (off // 8), RC), :]


def _glu_a(z):
    return z[:, :DA] * _sigmoid(z[:, DA:2 * DA])


def _gate_c(z):
    return z[:, 2048:] * z[:, 1024:1536]


def _ev_mix_fwd(h, z, wa, ba, lng, lnb, wb, bb, evo, tm, carry=None):
    s = h.shape[0]

    def body(h_ref, z_ref, zp_ref, zn_ref, wa_ref, ba_ref, lng_ref, lnb_ref, wb_ref, bb_ref, wo_ref,
             h1_ref, a2_ref, c2_ref, cat_ref, apad, cpad, c2f, xs):
        i = pl.program_id(0)
        zm = z_ref[...].astype(F32)
        zp = zp_ref[...].astype(F32)
        zn = zn_ref[...].astype(F32)
        _fill_pad(apad, _glu_a(zp), _glu_a(zm), _glu_a(zn), i, tm)
        _fill_pad(cpad, _gate_c(zp), _gate_c(zm), _gate_c(zn), i, tm)
        _shift8(xs, apad, tm)
        for r in range(0, tm, RC):
            acc = jnp.zeros((RC, DA), F32) + ba_ref[...]
            for k in range(KA):
                acc = acc + _tap(xs, r, k + 1) * wa_ref[pl.ds(k, 1), :]
            a2_ref[pl.ds(r, RC), :] = acc
            acc = jnp.zeros((RC, DA), F32) + bb_ref[...]
            for k in range(KB):
                acc = acc + cpad[pl.ds(r + HALO - 1 + k, RC), :] * wb_ref[pl.ds(k, 1), :]
            c2f[pl.ds(r, RC), :] = acc
        y, _ = _ln(a2_ref[...])
        a3 = y * lng_ref[...] + lnb_ref[...]
        cat_ref[:, :DA] = (a3 * _sigmoid(a3)).astype(BF)
        c2 = c2f[...]
        c2_ref[...] = c2.astype(BF)
        cat_ref[:, DA:] = (zm[:, 1536:2048] * c2).astype(BF)
        h1_ref[...] = h_ref[...] + _dot(cat_ref[...], wo_ref[...].reshape(D, D))

    zp_spec, zn_spec = _halo_specs(tm, DZ, s)
    small = lambda rows: _const((rows, DA), (0, 0))
    return _call(
        body, name="ev_mix_fwd", grid=(s // tm,),
        in_specs=[_rows(tm, D), _rows(tm, DZ), zp_spec, zn_spec, small(32), small(1), small(1), small(1),
                  small(8), small(1), _whole(evo)],
        out_specs=[_rows(tm, D), _rows(tm, DA), _rows(tm, DA), _rows(tm, D)],
        out_shape=[_sds((s, D), F32), _sds((s, DA), F32), _sds((s, DA), BF), _sds((s, D), BF)],
        scratch_shapes=[pltpu.VMEM((tm + 2 * HALO, DA), F32), pltpu.VMEM((tm + 2 * HALO, DA), F32),
                        pltpu.VMEM((tm, DA), F32), pltpu.VMEM((8, tm + 2 * HALO - 8, DA), F32)],
        args=(h, z, z, z, wa, ba, lng, lnb, wb, bb, evo), carry=carry)


def _ev_mix_bwd(dh, cat, a2, c2, z, lng, lnb, evo, tm, carry=None):
    s = dh.shape[0]

    def body(dh_ref, cat_ref, a2_ref, c2_ref, zgb_ref, lng_ref, lnb_ref, wo_ref,
             da2_ref, dc2_ref, dgb_ref, dlng_ref, dlnb_ref, gwo_ref, wacc):
        i = pl.program_id(0)

        @pl.when(_first(i))
        def _():
            wacc[...] = jnp.zeros_like(wacc)
            dlng_ref[...] = jnp.zeros_like(dlng_ref)
            dlnb_ref[...] = jnp.zeros_like(dlnb_ref)

        dhb = dh_ref[...].astype(BF)
        dcat = _dot_nt(dhb, wo_ref[...].reshape(D, D))
        wacc[...] += _dot_tn(cat_ref[...], dhb)
        y, rstd = _ln(a2_ref[...])
        a3 = y * lng_ref[...] + lnb_ref[...]
        sg = _sigmoid(a3)
        da3 = dcat[:, :DA] * (sg * (1.0 + a3 * (1.0 - sg)))
        dlng_ref[...] += _colsum(da3 * y)
        dlnb_ref[...] += _colsum(da3)
        da2_ref[...] = _ln_bwd(da3 * lng_ref[...], y, rstd).astype(BF)
        db = dcat[:, DA:]
        dc2_ref[...] = (db * zgb_ref[...].astype(F32)).astype(BF)
        dgb_ref[...] = (db * c2_ref[...].astype(F32)).astype(BF)

        @pl.when(_last(i))
        def _():
            gwo_ref[...] = wacc[...].astype(BF).reshape(NSH, 256, D)

    small = _const((1, DA), (0, 0))
    return _call(
        body, name="ev_mix_bwd", grid=(s // tm,),
        in_specs=[_rows(tm, D), _rows(tm, D), _rows(tm, DA), _rows(tm, DA), _rows(tm, DA, 3), small, small,
                  _whole(evo)],
        out_specs=[_rows(tm, DA), _rows(tm, DA), _rows(tm, DA), small, small, _whole(evo)],
        out_shape=[_sds((s, DA), BF), _sds((s, DA), BF), _sds((s, DA), BF), _sds((1, DA), F32),
                   _sds((1, DA), F32), _sds(evo.shape, BF)],
        scratch_shapes=[pltpu.VMEM((D, D), F32)],
        args=(dh, cat, a2, c2, z, lng, lnb, evo), carry=carry)


def _ev_in_bwd(h, dh, z, da2, dc2, dgb, g, wa, wb, wev, tm, after):
    s = h.shape[0]

    def body(h_ref, dh_ref, z_ref, da_ref, dap_ref, dan_ref, dc_ref, dcp_ref, dcn_ref,
             dgb_ref, g_ref, wa_ref, wb_ref, w_ref, after_ref,
             dh0_ref, dwa_ref, dba_ref, dwb_ref, dbb_ref, dg_ref, gw_ref,
             dapad, dcpad, dz, wacc, dwa_acc, dwb_acc, xs):
        i = pl.program_id(0)

        @pl.when(_first(i))
        def _():
            wacc[...] = jnp.zeros_like(wacc)
            dwa_acc[...] = jnp.zeros_like(dwa_acc)
            dwb_acc[...] = jnp.zeros_like(dwb_acc)
            dba_ref[...] = jnp.zeros_like(dba_ref)
            dbb_ref[...] = jnp.zeros_like(dbb_ref)
            dg_ref[...] = jnp.zeros_like(dg_ref)

        da2 = da_ref[...].astype(F32)
        dc2 = dc_ref[...].astype(F32)
        _fill_pad(dapad, dap_ref[...].astype(F32), da2, dan_ref[...].astype(F32), i, tm)
        _fill_pad(dcpad, dcp_ref[...].astype(F32), dc2, dcn_ref[...].astype(F32), i, tm)
        dba_ref[...] += _colsum(da2)
        dbb_ref[...] += _colsum(dc2)

        _shift8(xs, dapad, tm)
        for r in range(0, tm, RC):
            zcol = lambda lo: z_ref[pl.ds(r, RC), lo:lo + DA].astype(F32)
            aval = zcol(0)
            sgr = _sigmoid(zcol(DA))
            a = aval * sgr
            acc = jnp.zeros((RC, DA), F32)
            for k in range(KA):
                t = _tap(xs, r, 2 * HALO - 1 - k)
                acc = acc + t * wa_ref[pl.ds(k, 1), :]
                dwa_acc[k] += (t * a).reshape(RC // 8, 8, DA).sum(axis=0)
            dz[pl.ds(r, RC), 0:DA] = (acc * sgr).astype(BF)
            dz[pl.ds(r, RC), DA:2 * DA] = (acc * aval * sgr * (1.0 - sgr)).astype(BF)
            hb = zcol(1024)
            gc = zcol(2048)
            c = gc * hb
            acc = jnp.zeros((RC, DA), F32)
            for k in range(KB):
                t = dcpad[pl.ds(r + HALO + 1 - k, RC), :]
                acc = acc + t * wb_ref[pl.ds(k, 1), :]
                dwb_acc[k] += (t * c).reshape(RC // 8, 8, DA).sum(axis=0)
            dz[pl.ds(r, RC), 1024:1536] = (acc * gc).astype(BF)
            dz[pl.ds(r, RC), 2048:2560] = (acc * hb).astype(BF)
        dz[:, 1536:2048] = dgb_ref[...]

        gg = g_ref[...]
        hhat, rr = _rms(h_ref[...])
        n = (hhat * gg).astype(BF)
        dzb = dz[...]
        wacc[...] += _dot_tn(n, dzb)
        dn = _dot_nt(dzb, w_ref[...])
        dg_ref[...] += _colsum(dn * hhat)
        dh0_ref[...] = dh_ref[...] + _rms_bwd(dn, hhat, rr, gg)

        @pl.when(_last(i))
        def _():
            for k in range(NSH):
                gw_ref[k] = wacc[:, 640 * k:640 * (k + 1)].astype(BF)
            dwa_ref[...] = jnp.sum(dwa_acc[...], axis=1)
            dwb_ref[...] = jnp.sum(dwb_acc[...], axis=1)

    hp_spec, hn_spec = _halo_specs(tm, DA, s)
    small = lambda rows: _const((rows, DA), (0, 0))
    return _call(
        body, name="ev_in_bwd", grid=(s // tm,),
        in_specs=[_rows(tm, D), _rows(tm, D), _rows(tm, DZ),
                  _rows(tm, DA), hp_spec, hn_spec, _rows(tm, DA), hp_spec, hn_spec, _rows(tm, DA),
                  _const((1, D), (0, 0)), small(32), small(8), _const((D, DZ), (0, 0)), ANY],
        out_specs=[_rows(tm, D), small(32), small(1), small(8), small(1), _const((1, D), (0, 0)),
                   _const((NSH, D, 640), (0, 0, 0))],
        out_shape=[_sds((s, D), F32), _sds((32, DA), F32), _sds((1, DA), F32), _sds((8, DA), F32),
                   _sds((1, DA), F32), _sds((1, D), F32), _sds((NSH, D, 640), BF)],
        scratch_shapes=[pltpu.VMEM((tm + 2 * HALO, DA), F32)] * 2 + [
            pltpu.VMEM((tm, DZ), BF), pltpu.VMEM((D, DZ), F32),
            pltpu.VMEM((32, 8, DA), F32), pltpu.VMEM((8, 8, DA), F32),
            pltpu.VMEM((8, tm + 2 * HALO - 8, DA), F32)],
        args=(h, dh, z, da2, da2, da2, dc2, dc2, dc2, dgb, g, wa, wb, wev, after))


def _xa_spec(item):
    return _const((NSH, 256, D), (0, item, 0))


def _kv_proj(mem, g, xa, layer):
    def body(mem_ref, g_ref, wk_ref, wv_ref, k_ref, v_ref):
        mhat, _ = _rms(mem_ref[...])
        mn = (mhat * g_ref[...]).astype(BF)
        k_ref[...] = _dot(mn, wk_ref[...].reshape(D, D)).astype(BF)
        v_ref[...] = _dot(mn, wv_ref[...].reshape(D, D)).astype(BF)

    full = _const((NMEM, D), (0, 0))
    return _call(
        body, name=f"kv_proj{layer}", grid=(1,),
        in_specs=[full, _const((1, D), (0, 0)), _xa_spec(XK), _xa_spec(XV)],
        out_specs=[full, full], out_shape=[_sds((NMEM, D), BF), _sds((NMEM, D), BF)],
        args=(mem, g, xa, xa))


def _softmax_rows(sc):
    m = jnp.max(sc, axis=-1, keepdims=True)
    e = jnp.exp(sc - m)
    return e / jnp.sum(e, axis=-1, keepdims=True)


def _xattn_fwd(h, g, k, v, xa, layer, tm, carry=None):
    s = h.shape[0]

    def body(h_ref, g_ref, k_ref, v_ref, wq_ref, wo_ref, ho_ref, q_ref, o_scr):
        hh = h_ref[...]
        hhat, _ = _rms(hh)
        n = (hhat * g_ref[...]).astype(BF)
        q = _dot(n, wq_ref[...].reshape(D, D)).astype(BF)
        q_ref[...] = q
        for hd in range(HEADS):
            cs = slice(HD * hd, HD * (hd + 1))
            p = _softmax_rows(_dot_nt(q[:, cs], k_ref[:, cs]) * ATT_SCALE)
            o_scr[:, cs] = _dot(p.astype(BF), v_ref[:, cs]).astype(BF)
        ho_ref[...] = hh + _dot(o_scr[...], wo_ref[...].reshape(D, D))

    kvs = _const((NMEM, D), (0, 0))
    return _call(
        body, name=f"xattn_fwd{layer}", grid=(s // tm,),
        in_specs=[_rows(tm, D), _const((1, D), (0, 0)), kvs, kvs, _xa_spec(XQ), _xa_spec(XO)],
        out_specs=[_rows(tm, D), _rows(tm, D)],
        out_shape=[_sds((s, D), F32), _sds((s, D), BF)],
        scratch_shapes=[pltpu.VMEM((tm, D), BF)],
        args=(h, g, k, v, xa, xa), carry=carry)


def _xattn_bwd(h, dh, q, g, k, v, xa, layer, tm, carry=None):
    s = h.shape[0]

    def body(h_ref, dh_ref, q_ref, g_ref, k_ref, v_ref, wq_ref, wo_ref,
             dhi_ref, dk_ref, dv_ref, dg_ref, gqo_ref, o_scr, dq_scr, qacc, oacc):
        i = pl.program_id(0)

        @pl.when(_first(i))
        def _():
            qacc[...] = jnp.zeros_like(qacc)
            oacc[...] = jnp.zeros_like(oacc)
            dk_ref[...] = jnp.zeros_like(dk_ref)
            dv_ref[...] = jnp.zeros_like(dv_ref)
            dg_ref[...] = jnp.zeros_like(dg_ref)

        gg = g_ref[...]
        dho = dh_ref[...]
        dhb = dho.astype(BF)
        q = q_ref[...]
        do = _dot_nt(dhb, wo_ref[...].reshape(D, D)).astype(BF)
        for hd in range(HEADS):
            cs = slice(HD * hd, HD * (hd + 1))
            kh = k_ref[:, cs]
            vh = v_ref[:, cs]
            p = _softmax_rows(_dot_nt(q[:, cs], kh) * ATT_SCALE)
            pb = p.astype(BF)
            o_scr[:, cs] = _dot(pb, vh).astype(BF)
            doh = do[:, cs]
            dp = _dot_nt(doh, vh)
            dv_ref[:, cs] += _dot_tn(pb, doh)
            ds = (p * (dp - jnp.sum(dp * p, axis=-1, keepdims=True)) * ATT_SCALE).astype(BF)
            dq_scr[:, cs] = _dot(ds, kh).astype(BF)
            dk_ref[:, cs] += _dot_tn(ds, q[:, cs])
        oacc[...] += _dot_tn(o_scr[...], dhb)
        hhat, rr = _rms(h_ref[...])
        n = (hhat * gg).astype(BF)
        dq = dq_scr[...]
        qacc[...] += _dot_tn(n, dq)
        dn = _dot_nt(dq, wq_ref[...].reshape(D, D))
        dg_ref[...] += _colsum(dn * hhat)
        dhi_ref[...] = dho + _rms_bwd(dn, hhat, rr, gg)

        @pl.when(_last(i))
        def _():
            gqo_ref[:, 0:256, :] = qacc[...].astype(BF).reshape(NSH, 256, D)
            gqo_ref[:, 256:512, :] = oacc[...].astype(BF).reshape(NSH, 256, D)

    kvs = _const((NMEM, D), (0, 0))
    return _call(
        body, name=f"xattn_bwd{layer}", grid=(s // tm,),
        in_specs=[_rows(tm, D), _rows(tm, D), _rows(tm, D), _const((1, D), (0, 0)), kvs, kvs,
                  _xa_spec(XQ), _xa_spec(XO)],
        out_specs=[_rows(tm, D), kvs, kvs, _const((1, D), (0, 0)), _const((NSH, 512, D), (0, 0, 0))],
        out_shape=[_sds((s, D), F32), _sds((NMEM, D), F32), _sds((NMEM, D), F32), _sds((1, D), F32),
                   _sds(xa.shape, BF)],
        scratch_shapes=[pltpu.VMEM((tm, D), BF), pltpu.VMEM((tm, D), BF), pltpu.VMEM((D, D), F32),
                        pltpu.VMEM((D, D), F32)],
        args=(h, dh, q, g, k, v, xa, xa), carry=carry)


def _kv_proj_bwd(mem, g, dk, dv, xa, gxa, layer):
    def body(mem_ref, g_ref, dk_ref, dv_ref, wk_ref, wv_ref, gxa_in, dg_ref, gkv_ref):
        mhat, _ = _rms(mem_ref[...])
        mn = (mhat * g_ref[...]).astype(BF)
        dkb = dk_ref[...].astype(BF)
        dvb = dv_ref[...].astype(BF)
        gkv_ref[:, 0:256, :] = _dot_tn(mn, dkb).astype(BF).reshape(NSH, 256, D)
        gkv_ref[:, 256:512, :] = _dot_tn(mn, dvb).astype(BF).reshape(NSH, 256, D)
        dmn = _dot_nt(dkb, wk_ref[...].reshape(D, D)) + _dot_nt(dvb, wv_ref[...].reshape(D, D))
        dg_ref[...] = _colsum(dmn * mhat)

    full = _const((NMEM, D), (0, 0))
    return _call(
        body, name=f"kv_proj_bwd{layer}", grid=(1,),
        in_specs=[full, _const((1, D), (0, 0)), full, full, _xa_spec(XK), _xa_spec(XV), ANY],
        out_specs=[_const((1, D), (0, 0)), _const((NSH, 512, D), (0, 1, 0))],
        out_shape=[_sds((1, D), F32), _sds(gxa.shape, BF)],
        aliases={6: 1},
        args=(mem, g, dk, dv, xa, xa, gxa))


def _ffn_block(h_ref, g_ref, wg_ref, wu_ref, wd_ref, gt_ref, up_ref):
    hh = h_ref[...]
    hhat, _ = _rms(hh)
    n = (hhat * g_ref[...]).astype(BF)
    out = hh
    for c0, c1 in FF_CHUNKS:
        gt = _dot_nt(n, wg_ref[c0:c1, :])
        up = _dot_nt(n, wu_ref[c0:c1, :])
        gt_ref[:, c0:c1] = gt.astype(BF)
        up_ref[:, c0:c1] = up.astype(BF)
        act = (gt * _sigmoid(gt) * up).astype(BF)
        out = out + _dot(act, wd_ref[c0:c1, :])
    return out


def _ffn_fwd(h, g, wg, wu, dn, layer, tm, carry=None):
    s = h.shape[0]
    wg, wu, dn = (a.reshape(DFF, D) for a in (wg, wu, dn))

    def body(h_ref, g_ref, wg_ref, wu_ref, wd_ref, ho_ref, gt_ref, up_ref):
        ho_ref[...] = _ffn_block(h_ref, g_ref, wg_ref, wu_ref, wd_ref, gt_ref, up_ref)

    return _call(
        body, name=f"ffn_fwd{layer}", grid=(s // tm,),
        in_specs=[_rows(tm, D), _const((1, D), (0, 0)), _whole(wg), _whole(wu), _whole(dn)],
        out_specs=[_rows(tm, D), _rows(tm, DFF), _rows(tm, DFF)],
        out_shape=[_sds((s, D), F32), _sds((s, DFF), BF), _sds((s, DFF), BF)],
        args=(h, g, wg, wu, dn), carry=carry)


def _ffn_fwd_loss(h, g, wg, wu, dn, tgt, g_final, layer, tm):
    s = h.shape[0]
    wg, wu, dn = (a.reshape(DFF, D) for a in (wg, wu, dn))

    def body(h_ref, g_ref, wg_ref, wu_ref, wd_ref, t_ref, gf_ref, dh_ref, loss_ref, dg_ref, gt_ref, up_ref):
        i = pl.program_id(0)

        @pl.when(_first(i))
        def _():
            loss_ref[...] = jnp.zeros_like(loss_ref)
            dg_ref[...] = jnp.zeros_like(dg_ref)

        out = _ffn_block(h_ref, g_ref, wg_ref, wu_ref, wd_ref, gt_ref, up_ref)
        gg = gf_ref[...]
        hhat, rr = _rms(out)
        diff = hhat * gg - t_ref[...]
        loss_ref[...] += jnp.sum(diff * diff) * (0.5 / D)
        dout = diff * (1.0 / D)
        dg_ref[...] += _colsum(dout * hhat)
        dh_ref[...] = _rms_bwd(dout, hhat, rr, gg)

    vec = _const((1, D), (0, 0))
    return _call(
        body, name=f"ffn_fwd_loss{layer}", grid=(s // tm,),
        in_specs=[_rows(tm, D), vec, _whole(wg), _whole(wu), _whole(dn), _rows(tm, D), vec],
        out_specs=[_rows(tm, D), _const((1, 128), (0, 0)), vec, _rows(tm, DFF), _rows(tm, DFF)],
        out_shape=[_sds((s, D), F32), _sds((1, 128), F32), _sds((1, D), F32), _sds((s, DFF), BF),
                   _sds((s, DFF), BF)],
        args=(h, g, wg, wu, dn, tgt, g_final))


def _ffn_bwd_dx(h, dh, gt, up, g, wg, wu, dn, layer, tm, carry=None):
    s = h.shape[0]
    wg, wu, dn = (a.reshape(DFF, D) for a in (wg, wu, dn))

    def body(h_ref, dh_ref, gt_ref, up_ref, g_ref, wg_ref, wu_ref, wd_ref,
             dhi_ref, dg_ref, dgt_ref, dup_ref, act_ref, n_ref):
        i = pl.program_id(0)

        @pl.when(_first(i))
        def _():
            dg_ref[...] = jnp.zeros_like(dg_ref)

        gg = g_ref[...]
        dho = dh_ref[...]
        dhb = dho.astype(BF)
        dn_ = jnp.zeros((tm, D), F32)
        for c0, c1 in FF_CHUNKS:
            gtv = gt_ref[:, c0:c1].astype(F32)
            upv = up_ref[:, c0:c1].astype(F32)
            sg = _sigmoid(gtv)
            silu = gtv * sg
            dact = _dot_nt(dhb, wd_ref[c0:c1, :])
            dgt = (dact * upv * (sg * (1.0 + gtv * (1.0 - sg)))).astype(BF)
            dup = (dact * silu).astype(BF)
            dgt_ref[:, c0:c1] = dgt
            dup_ref[:, c0:c1] = dup
            act_ref[:, c0:c1] = (silu * upv).astype(BF)
            dn_ = dn_ + _dot(dgt, wg_ref[c0:c1, :]) + _dot(dup, wu_ref[c0:c1, :])
        hhat, rr = _rms(h_ref[...])
        n_ref[...] = (hhat * gg).astype(BF)
        dg_ref[...] += _colsum(dn_ * hhat)
        dhi_ref[...] = dho + _rms_bwd(dn_, hhat, rr, gg)

    wide = _rows(tm, DFF)
    return _call(
        body, name=f"ffn_bwd_dx{layer}", grid=(s // tm,),
        in_specs=[_rows(tm, D), _rows(tm, D), wide, wide, _const((1, D), (0, 0)),
                  _whole(wg), _whole(wu), _whole(dn)],
        out_specs=[_rows(tm, D), _const((1, D), (0, 0)), wide, wide, wide, _rows(tm, D)],
        out_shape=[_sds((s, D), F32), _sds((1, D), F32), _sds((s, DFF), BF), _sds((s, DFF), BF),
                   _sds((s, DFF), BF), _sds((s, D), BF)],
        args=(h, dh, gt, up, g, wg, wu, dn), carry=carry)


def _ffn_wgrad(a, b, name, tm, slot, into=None, carry=None):
    s = a.shape[0]
    tm = min(tm, s)

    def body(a_ref, b_ref, *rest):
        o_ref, acc = rest[-2:]
        i = pl.program_id(0)

        @pl.when(_first(i))
        def _():
            acc[...] = jnp.zeros_like(acc)

        acc[...] += _dot_tn(a_ref[...], b_ref[...].astype(BF))

        @pl.when(_last(i))
        def _():
            o_ref[...] = acc[...].astype(BF).reshape(NSH, FSH, D)

    chained = into is not None
    return _call(
        body, name=name, grid=(s // tm,),
        in_specs=[_rows(tm, DFF), _rows(tm, D)] + [ANY] * chained,
        out_specs=[_const((NSH, FSH, D), (0, slot, 0))],
        out_shape=[_sds((NSH, 3 * FSH, D), BF)],
        scratch_shapes=[pltpu.VMEM((DFF, D), F32)],
        aliases={2: 0} if chained else None,
        args=(a, b) + ((into,) if chained else ()), carry=carry)


def _sgu_fwd(h, g, lng, lnb, ws, bsb, wod, odo, tm, carry=None):
    s = h.shape[0]

    def body(h_ref, g_ref, lng_ref, lnb_ref, ws_ref, bsb_ref, wi_ref, wo_ref, ho_ref, zp_ref, y_scr):
        hh = h_ref[...]
        hhat, _ = _rms(hh)
        n = (hhat * g_ref[...]).astype(BF)
        zpre = _dot(n, wi_ref[...])
        zp_ref[...] = zpre.astype(BF)
        z = _gelu(zpre)
        u = z[:, :D]
        y, _ = _ln(z[:, D:])
        vn = (y * lng_ref[...] + lnb_ref[...]).astype(BF)
        for c in range(tm // CHUNK):
            rs = slice(CHUNK * c, CHUNK * (c + 1))
            for gi in range(GROUPS):
                cs = slice(CHUNK * gi, CHUNK * (gi + 1))
                sv = _dot(ws_ref[gi], vn[rs, cs]) + bsb_ref[gi]
                y_scr[rs, cs] = (u[rs, cs] * sv).astype(BF)
        ho_ref[...] = hh + _dot(y_scr[...], wo_ref[...].reshape(D, D))

    vec = _const((1, D), (0, 0))
    sq = _const((GROUPS, CHUNK, CHUNK), (0, 0, 0))
    return _call(
        body, name="sgu_fwd", grid=(s // tm,),
        in_specs=[_rows(tm, D), vec, vec, vec, sq, sq, _const((D, 2 * D), (0, 0)), _whole(odo)],
        out_specs=[_rows(tm, D), _rows(tm, 2 * D)],
        out_shape=[_sds((s, D), F32), _sds((s, 2 * D), BF)],
        scratch_shapes=[pltpu.VMEM((tm, D), BF)],
        args=(h, g, lng, lnb, ws, bsb, wod, odo), carry=carry)


def _sgu_bwd(h, dh, zpre, g, lng, lnb, ws, wst, bsb, wod, odo, tm, carry=None):
    s = h.shape[0]

    def body(h_ref, dh_ref, zp_ref, g_ref, lng_ref, lnb_ref, ws_ref, wst_ref, bsb_ref, wi_ref, wo_ref,
             dhi_ref, dws_ref, dbs_ref, dlng_ref, dlnb_ref, dg_ref, gwi_ref, gwo_ref,
             y_scr, dz_scr, dvn_scr, iacc, oacc, bacc):
        i = pl.program_id(0)

        @pl.when(_first(i))
        def _():
            for ref in (iacc, oacc, bacc, dws_ref, dlng_ref, dlnb_ref, dg_ref):
                ref[...] = jnp.zeros_like(ref)

        gg = g_ref[...]
        dho = dh_ref[...]
        dhb = dho.astype(BF)
        zpre_v = zp_ref[...].astype(F32)
        z = _gelu(zpre_v)
        u = z[:, :D]
        yl, rstd = _ln(z[:, D:])
        vn = (yl * lng_ref[...] + lnb_ref[...]).astype(BF)
        dy = _dot_nt(dhb, wo_ref[...].reshape(D, D))
        for c in range(tm // CHUNK):
            rs = slice(CHUNK * c, CHUNK * (c + 1))
            for gi in range(GROUPS):
                cs = slice(CHUNK * gi, CHUNK * (gi + 1))
                vb = vn[rs, cs]
                sv = _dot(ws_ref[gi], vb) + bsb_ref[gi]
                ub = u[rs, cs]
                dyb = dy[rs, cs]
                y_scr[rs, cs] = (ub * sv).astype(BF)
                dz_scr[rs, cs] = dyb * sv
                dsv = dyb * ub
                bacc[gi] += dsv
                dsvb = dsv.astype(BF)
                dws_ref[gi] += _dot_nt(dsvb, vb)
                dvn_scr[rs, cs] = _dot(wst_ref[gi], dsvb)
        oacc[...] += _dot_tn(y_scr[...], dhb)
        dvn = dvn_scr[...]
        dlng_ref[...] += _colsum(dvn * yl)
        dlnb_ref[...] += _colsum(dvn)
        dz_scr[:, D:] = _ln_bwd(dvn * lng_ref[...], yl, rstd)
        dzb = (dz_scr[...] * _gelu_grad(zpre_v)).astype(BF)
        hhat, rr = _rms(h_ref[...])
        n = (hhat * gg).astype(BF)
        iacc[...] += _dot_tn(n, dzb)
        dn = _dot_nt(dzb, wi_ref[...])
        dg_ref[...] += _colsum(dn * hhat)
        dhi_ref[...] = dho + _rms_bwd(dn, hhat, rr, gg)

        @pl.when(_last(i))
        def _():
            for k in range(NSH):
                gwi_ref[k] = iacc[:, 512 * k:512 * (k + 1)].astype(BF)
            gwo_ref[...] = oacc[...].astype(BF).reshape(NSH, 256, D)
            dbs_ref[...] = jnp.sum(bacc[...], axis=-1)

    vec = _const((1, D), (0, 0))
    sq = _const((GROUPS, CHUNK, CHUNK), (0, 0, 0))
    return _call(
        body, name="sgu_bwd", grid=(s // tm,),
        in_specs=[_rows(tm, D), _rows(tm, D), _rows(tm, 2 * D), vec, vec, vec, sq, sq, sq,
                  _const((D, 2 * D), (0, 0)), _whole(odo)],
        out_specs=[_rows(tm, D), sq, _const((GROUPS, CHUNK), (0, 0)), vec, vec, vec,
                   _const((NSH, D, 512), (0, 0, 0)), _whole(odo)],
        out_shape=[_sds((s, D), F32), _sds((GROUPS, CHUNK, CHUNK), F32), _sds((GROUPS, CHUNK), F32),
                   _sds((1, D), F32), _sds((1, D), F32), _sds((1, D), F32), _sds((NSH, D, 512), BF),
                   _sds(odo.shape, BF)],
        scratch_shapes=[pltpu.VMEM((tm, D), BF), pltpu.VMEM((tm, 2 * D), F32),
                        pltpu.VMEM((tm, D), F32), pltpu.VMEM((D, 2 * D), F32), pltpu.VMEM((D, D), F32),
                        pltpu.VMEM((GROUPS, CHUNK, CHUNK), F32)],
        args=(h, dh, zpre, g, lng, lnb, ws, wst, bsb, wod, odo), carry=carry)


def _own_plane(shape):
    return pl.BlockSpec((1,) + shape, lambda i, chip_ref: (chip_ref[0], 0, 0), pipeline_mode=pl.Buffered(1))


def _pack_rows(chip_idx, w_down, w_gate_t, w_up_t, xq, xo, xk, xv, ev_out, od_out, first):
    nf = len(first)

    def body(chip_ref, wd_ref, wg_ref, wu_ref, xq_ref, xo_ref, xk_ref, xv_ref, ev_ref, od_ref, *rest):
        (evo_ref, odo_ref, dn0_ref, dn1_ref, wg0_ref, wg1_ref, wu0_ref, wu1_ref, xa0_ref,
         xa1_ref) = rest[nf:nf + 10]
        gathered = rest[nf + 10:2 * nf + 10]
        sems = rest[2 * nf + 10:]
        _gather_start(gathered, sems)
        evo_ref[0] = ev_ref[0].astype(BF)
        odo_ref[0] = od_ref[0].astype(BF)
        for src, outs in ((wd_ref, (dn0_ref, dn1_ref)), (wg_ref, (wg0_ref, wg1_ref)), (wu_ref, (wu0_ref, wu1_ref))):
            for layer, o_ref in enumerate(outs):
                o_ref[0] = src[layer].astype(BF)
        for layer, xa_ref in enumerate((xa0_ref, xa1_ref)):
            for item, src in ((XQ, xq_ref), (XO, xo_ref), (XK, xk_ref), (XV, xv_ref)):
                xa_ref[0, 256 * item:256 * (item + 1), :] = src[layer].astype(BF)
        _gather_forward(gathered, sems)
        _gather_drain(gathered, sems)

    ins = (w_down, w_gate_t, w_up_t, xq, xo, xk, xv, ev_out, od_out)
    shapes = [(256, D), (256, D)] + [(FSH, D)] * 6 + [(4 * 256, D), (4 * 256, D)]
    grid_spec = pltpu.PrefetchScalarGridSpec(
        num_scalar_prefetch=1, grid=(1,), in_specs=[_whole(a) for a in ins] + [ANY] * nf,
        out_specs=[_own_plane(sh) for sh in shapes] + [ANY] * nf,
        scratch_shapes=_gather_sems(nf))
    outs = pl.pallas_call(
        body, name="pack_rows", grid_spec=grid_spec,
        out_shape=[_sds((NSH,) + sh, BF) for sh in shapes] + [_sds(a.shape, a.dtype) for a in first],
        input_output_aliases={1 + len(ins) + t: len(shapes) + t for t in range(nf)},
        compiler_params=_params(),
    )(chip_idx, *ins, *first)
    return outs[:len(shapes)], outs[len(shapes):]


def _pack_cols(chip_idx, od_in, ev_in, conv_a, conv_b, ln_g, ln_b):
    def body(chip_ref, od_ref, ev_ref, ca_ref, cb_ref, lg_ref, lb_ref, cev_ref, cod_ref, sp_ref):
        cev_ref[0] = ev_ref[0].astype(BF)
        cod_ref[0] = od_ref[0].astype(BF)
        sp_ref[...] = jnp.zeros_like(sp_ref)
        sp_ref[0, 0:KA, 0:128] = ca_ref[0]
        sp_ref[0, 32:32 + KB, 0:128] = cb_ref[0]
        sp_ref[0, 40:41, :] = lg_ref[...]
        sp_ref[0, 41:42, :] = lb_ref[...]

    ins = (od_in, ev_in, conv_a, conv_b, ln_g, ln_b)
    shapes = [(D, 640), (D, 512)]
    grid_spec = pltpu.PrefetchScalarGridSpec(
        num_scalar_prefetch=1, grid=(1,), in_specs=[_whole(a) for a in ins],
        out_specs=[_own_plane(sh) for sh in shapes] + [_own_plane((SP_ROWS, 256))])
    return pl.pallas_call(
        body, name="pack_cols", grid_spec=grid_spec,
        out_shape=[_sds((NSH,) + sh, BF) for sh in shapes] + [_sds((NSH, SP_ROWS, 256), F32)],
        compiler_params=_params(),
    )(chip_idx, *ins)


def _assemble(piece, parts, name):
    tr = 256
    cols = piece.shape[2]

    def body(p_ref, *outs):
        for k in range(NSH):
            for (off, width), o_ref in zip(parts, outs):
                o_ref[:, width * k:width * (k + 1)] = p_ref[k, :, off:off + width]

    return _call(
        body, name=name, grid=(D // tr,),
        in_specs=[pl.BlockSpec((NSH, tr, cols), lambda i: (0, i, 0))],
        out_specs=[pl.BlockSpec((tr, NSH * width), lambda i: (i, 0)) for _, width in parts],
        out_shape=[_sds((D, NSH * width), BF) for _, width in parts],
        args=(piece,))


def _blocks_of(rows, cols, itemsize):
    rb = rows
    while rb * cols * itemsize > (2 << 20) and rb % 2 == 0 and (rb // 2) % 16 == 0:
        rb //= 2
    return rb


def _add_half(full, recv, c_idx, name):
    p, r, cols = full.shape
    rb = _blocks_of(r // 2, cols, 4)
    nb = (r // 2) // rb

    def body(c_ref, a_ref, b_ref, o_ref):
        o_ref[...] = (a_ref[...].astype(F32) + b_ref[...].astype(F32)).astype(o_ref.dtype)

    grid_spec = pltpu.PrefetchScalarGridSpec(
        num_scalar_prefetch=1, grid=(p, nb),
        in_specs=[pl.BlockSpec((1, rb, cols), lambda j, i, c_ref: (j, c_ref[0] * nb + i, 0)),
                  pl.BlockSpec((1, rb, cols), lambda j, i, c_ref: (j, i, 0))],
        out_specs=pl.BlockSpec((1, rb, cols), lambda j, i, c_ref: (j, i, 0)))
    return pl.pallas_call(
        body, name=name, grid_spec=grid_spec, out_shape=_sds(recv.shape, full.dtype),
        compiler_params=_params(2),
    )(c_idx, full, recv)


def _sum_chips(own, got, plane_c, name):
    _, r, cols = own.shape
    rb = _blocks_of(r, cols, 4)
    nb = r // rb

    def body(s_ref, a_ref, b_ref, o_ref):
        f = lambda v: v.astype(F32)
        o_ref[...] = (f(a_ref[0]) + f(b_ref[1])) + (f(b_ref[0]) + f(b_ref[2]))

    grid_spec = pltpu.PrefetchScalarGridSpec(
        num_scalar_prefetch=1, grid=(nb,),
        in_specs=[pl.BlockSpec((1, rb, cols), lambda i, s_ref: (s_ref[0], i, 0)),
                  pl.BlockSpec((3, rb, cols), lambda i, s_ref: (0, i, 0))],
        out_specs=pl.BlockSpec((rb, cols), lambda i, s_ref: (s_ref[1] * nb + i, 0)))
    return pl.pallas_call(
        body, name=name, grid_spec=grid_spec, out_shape=_sds((2 * r, cols), F32),
        compiler_params=_params(1),
    )(plane_c, own, got)


def _adamw_math(w, g, m, v):
    m = ADAM_B1 * m + (1.0 - ADAM_B1) * g
    v = ADAM_B2 * v + (1.0 - ADAM_B2) * (g * g)
    m_hat = m / (1.0 - ADAM_B1 ** ADAM_STEP)
    v_hat = v / (1.0 - ADAM_B2 ** ADAM_STEP)
    delta = -ADAM_LR * (m_hat / (jnp.sqrt(v_hat) + ADAM_EPS) + ADAM_WD * w)
    return delta, m, v


def _adamw_big(w, m, v, gsrcs, gblock, gidx, name, rb):
    nl, r, cols = w.shape
    assert len(gsrcs) == nl

    def body(w_ref, m_ref, v_ref, *rest):
        g_refs, (go_ref, d_ref, mo_ref, vo_ref) = rest[:nl], rest[nl:]
        g = g_refs[0][:, 0:cols]
        if nl == 2:
            g = jnp.where(pl.program_id(0) == 0, g, g_refs[1][:, 0:cols])
        delta, mn, vn = _adamw_math(w_ref[0], g, m_ref[0], v_ref[0])
        go_ref[0] = g
        d_ref[0] = delta
        mo_ref[0] = mn
        vo_ref[0] = vn

    wspec = pl.BlockSpec((1, rb, cols), lambda l, i: (l, i, 0))
    gspec = pl.BlockSpec(gblock, lambda l, i: gidx(i))
    return pl.pallas_call(
        body, name=name, grid=(nl, r // rb),
        in_specs=[wspec, wspec, wspec] + [gspec] * nl,
        out_specs=[wspec] * 4, out_shape=[_sds(w.shape, F32)] * 4,
        compiler_params=_params(2),
    )(w, m, v, *gsrcs)


def _adamw_rows(ws, ms, vs, gsrcs, name, rb=64):
    n = len(ws)
    nl, r, cols = ws[0].shape
    nb = r // rb

    def body(*refs):
        w_refs, m_refs, v_refs = refs[:n], refs[n:2 * n], refs[2 * n:3 * n]
        g_refs = refs[3 * n:3 * n + nl * n]
        outs = refs[3 * n + nl * n:]
        for j in range(n):
            g = g_refs[j][...]
            for l in range(1, nl):
                g = jnp.where(pl.program_id(0) == l, g_refs[l * n + j][...], g)
            delta, mn, vn = _adamw_math(w_refs[j][0], g, m_refs[j][0], v_refs[j][0])
            for o_ref, val in zip(outs[4 * j:4 * j + 4], (g, delta, mn, vn)):
                o_ref[0] = val

    wspec = pl.BlockSpec((1, rb, cols), lambda l, i: (l, i, 0))
    gspecs = [pl.BlockSpec((rb, cols), lambda l, i, _j=j: (_j * nb + i, 0)) for _ in range(nl) for j in range(n)]
    outs = pl.pallas_call(
        body, name=name, grid=(nl, nb),
        in_specs=[wspec] * (3 * n) + gspecs,
        out_specs=[wspec] * (4 * n), out_shape=[_sds(ws[0].shape, F32)] * (4 * n),
        compiler_params=_params(2),
    )(*ws, *ms, *vs, *[gsrcs[l] for l in range(nl) for _ in range(n)])
    return [tuple(outs[4 * j:4 * j + 4]) for j in range(n)]


_SMALL = (
    ("g_mix", (2, D), "a", (0, 2), 0, False),
    ("g_xattn", (2, D), "a", (2, 4), 0, False),
    ("g_mem", (2, D), "a", (4, 6), 0, False),
    ("g_ffn", (2, D), "a", (6, 8), 0, False),
    ("g_final", (1, D), "a", (8, 9), 0, False),
    ("ev_a_conv_w", (KA, 128), "a", (16, 16 + KA), 0, True),
    ("ev_a_conv_b", (1, DA), "a", (11, 12), 0, False),
    ("ev_a_ln_g", (1, DA), "a", (11, 12), DA, False),
    ("ev_a_ln_b", (1, DA), "a", (12, 13), 0, False),
    ("ev_b_conv_w", (KB, 128), "a", (48, 48 + KB), 0, True),
    ("ev_b_conv_b", (1, DA), "a", (12, 13), DA, False),
    ("od_c_ln_g", (1, 256), "a", (9, 10), 0, True),
    ("od_c_ln_b", (1, 256), "a", (10, 11), 0, True),
    ("od_w_s", (GROUPS * CHUNK, CHUNK), "b", (0, GROUPS * CHUNK), 0, False),
    ("od_b_s", (GROUPS, CHUNK), "a", (56, 64), 0, False),
)


def _adamw_small(ga, gb, wmv):
    ns = len(_SMALL)

    def body(*refs):
        ga_ref, gb_ref = refs[0], refs[1]
        ins = refs[2:2 + 3 * ns]
        outs = refs[2 + 3 * ns:]
        x, y, _ = _pos()
        chip = 2 * x + y
        for j, (_, shape, pack, (r0, r1), col, sharded) in enumerate(_SMALL):
            src = ga_ref if pack == "a" else gb_ref
            width = shape[1]
            if sharded:
                g = jnp.zeros(shape, F32)
                for k in range(NSH):
                    blk = src[r0:r1, col + width * k:col + width * (k + 1)]
                    g = g + jnp.where(chip == k, blk, 0.0)
            else:
                g = src[r0:r1, col:col + width]
            w_ref, m_ref, v_ref = ins[3 * j:3 * j + 3]
            delta, mn, vn = _adamw_math(w_ref[...], g, m_ref[...], v_ref[...])
            go_ref, d_ref, mo_ref, vo_ref = outs[4 * j:4 * j + 4]
            go_ref[...] = g
            d_ref[...] = delta
            mo_ref[...] = mn
            vo_ref[...] = vn

    vm = pl.BlockSpec(memory_space=pltpu.VMEM)
    flat = [a for trio in wmv for a in trio]
    out_shape = []
    for (_, shape, *_rest) in _SMALL:
        out_shape += [_sds(shape, F32)] * 4
    outs = pl.pallas_call(
        body, name="adamw_small",
        in_specs=[vm] * (2 + 3 * ns), out_specs=[vm] * (4 * ns), out_shape=out_shape,
    )(ga, gb, *flat)
    return [tuple(outs[4 * j:4 * j + 4]) for j in range(ns)]


def _small_params(gs, W):
    unshard = lambda a: jnp.transpose(a, (1, 0, 2)).reshape(a.shape[1], NSH * a.shape[2])
    od_w_s, od_b_s = W["od_w_s"], W["od_b_s"]
    return dict(
        g_mix=W["g_mix"], g_xattn=W["g_xattn"], g_mem=W["g_mem"], g_ffn=W["g_ffn"],
        g_final=W["g_final"].reshape(1, D),
        wa=unshard(gs[:, 0:32, 0:128]), ba=W["ev_a_conv_b"], lng_a=W["ev_a_ln_g"], lnb_a=W["ev_a_ln_b"],
        wb=unshard(gs[:, 32:40, 0:128]), bb=W["ev_b_conv_b"],
        lng_c=unshard(gs[:, 40:41, :]), lnb_c=unshard(gs[:, 41:42, :]),
        ws=od_w_s[0].astype(BF), wst=jnp.swapaxes(od_w_s[0], 1, 2).astype(BF),
        bsb=jnp.broadcast_to(od_b_s[0][:, :, None], (GROUPS, CHUNK, CHUNK)),
    )


LOSS_ROW = 13


def _small_grad_packs(g, loss):
    z = lambda r, c: jnp.zeros((r, c), F32)
    wide = lambda a: jnp.concatenate([a, z(a.shape[0], D - a.shape[1])], axis=1)
    rows = [g["g_mix"], g["g_xattn"], g["g_mem"], g["g_ffn"], g["g_final"], g["od_c_ln_g"], g["od_c_ln_b"],
            jnp.concatenate([g["ev_a_conv_b"], g["ev_a_ln_g"]], axis=1),
            jnp.concatenate([g["ev_a_ln_b"], g["ev_b_conv_b"]], axis=1),
            wide(loss), z(2, D), wide(g["ev_a_conv_w"]), wide(g["ev_b_conv_w"]), wide(g["od_b_s"])]
    a = jnp.concatenate(rows, axis=0)
    assert a.shape == (SA_ROWS, D), a.shape
    return a, g["od_w_s"].reshape(GROUPS * CHUNK, CHUNK)


def _step(x, mem, tgt, W, P, c_idx, tm_f, tm_b, tm_w):
    row = lambda a, i: a[i:i + 1]

    sm = _small_params(P["gs"], W)
    (wev,) = _assemble(P["cev"], [(0, 640)], "assemble_ev")
    z, evo, wg0 = _ev_in_fwd(x, row(sm["g_mix"], 0), wev, 2 * tm_f, carry=("gather", [P["evo"], P["wg0"]]))
    h1, a2, c2, cat, xa0, wu0 = _ev_mix_fwd(
        x, z, sm["wa"], sm["ba"], sm["lng_a"], sm["lnb_a"], sm["wb"], sm["bb"], evo, tm_f,
        carry=("gather", [P["xa0"], P["wu0"]]))
    k0, v0 = _kv_proj(mem, row(sm["g_mem"], 0), xa0, 0)
    h2, q0, dn0 = _xattn_fwd(h1, row(sm["g_xattn"], 0), k0, v0, xa0, 0, 2 * tm_f, carry=("gather", [P["dn0"]]))
    h3, gt0, up0, odo, xa1, wg1, cod = _ffn_fwd(h2, row(sm["g_ffn"], 0), wg0, wu0, dn0, 0, tm_f,
                                                carry=("gather", [P["odo"], P["xa1"], P["wg1"], P["cod"]]))
    (wod,) = _assemble(cod, [(0, 512)], "assemble_od")
    h4, zpre, wu1, dn1 = _sgu_fwd(h3, row(sm["g_mix"], 1), sm["lng_c"], sm["lnb_c"], sm["ws"], sm["bsb"], wod,
                                  odo, tm_f, carry=("gather", [P["wu1"], P["dn1"]]))
    k1, v1 = _kv_proj(mem, row(sm["g_mem"], 1), xa1, 1)
    h5, q1 = _xattn_fwd(h4, row(sm["g_xattn"], 1), k1, v1, xa1, 1, 2 * tm_f)
    dh6, loss, dg_final, gt1, up1 = _ffn_fwd_loss(h5, row(sm["g_ffn"], 1), wg1, wu1, dn1, tgt, sm["g_final"], 1,
                                                  tm_f)

    g = {"g_final": dg_final}
    parts, gots = {}, {}
    def add(name, piece, recv):
        parts[name] = _add_half(piece, recv, c_idx, f"rs_add_{name}")
        return parts[name]

    def ffn_wgrads(layer, dgt, dup, act, nb, dh_out, carry=None):
        g_ffn, *carried = _ffn_wgrad(dgt, nb, f"ffn_wgrad_gate{layer}", tm_w, 0, carry=carry)
        (g_ffn,) = _ffn_wgrad(dup, nb, f"ffn_wgrad_up{layer}", tm_w, 1, into=g_ffn)
        (g_ffn,) = _ffn_wgrad(act, dh_out, f"ffn_wgrad_down{layer}", tm_w, 2, into=g_ffn)
        return g_ffn, carried

    dh5, dg_ffn1, dgt, dup, act, nb = _ffn_bwd_dx(h5, dh6, gt1, up1, row(sm["g_ffn"], 1), wg1, wu1, dn1, 1, tm_b)
    g_ffn1, _ = ffn_wgrads(1, dgt, dup, act, nb, dh6)
    dh4, dk, dv, dg_xa1, g_xa1, recv = _xattn_bwd(
        h4, dh5, q1, row(sm["g_xattn"], 1), k1, v1, xa1, 1, tm_f, carry=("swap", [g_ffn1]))
    add("ffn1", g_ffn1, recv)
    dg_mem1, g_xa1 = _kv_proj_bwd(mem, row(sm["g_mem"], 1), dk, dv, xa1, g_xa1, 1)
    (dh3, g["od_w_s"], g["od_b_s"], g["od_c_ln_g"], g["od_c_ln_b"], dg_mix1, g_cod, g_odo, gots["ffn1"],
     recv) = _sgu_bwd(
        h3, dh4, zpre, row(sm["g_mix"], 1), sm["lng_c"], sm["lnb_c"], sm["ws"], sm["wst"], sm["bsb"], wod, odo, tm_b,
        carry=[("xchg", [parts["ffn1"]], []), ("swap", [g_xa1])])
    add("xa1", g_xa1, recv)
    dh2, dg_ffn0, dgt, dup, act, nb, gots["xa1"], recv_cod, recv_odo = _ffn_bwd_dx(
        h2, dh3, gt0, up0, row(sm["g_ffn"], 0), wg0, wu0, dn0, 0, tm_b,
        carry=[("xchg", [parts["xa1"]], []), ("swap", [g_cod, g_odo])])
    add("cod", g_cod, recv_cod)
    add("odo", g_odo, recv_odo)
    g_ffn0, (gots["cod"], gots["odo"]) = ffn_wgrads(0, dgt, dup, act, nb, dh3,
                                                    carry=("xchg", [parts["cod"], parts["odo"]], []))
    dh1, dk, dv, dg_xa0, g_xa0, recv = _xattn_bwd(
        h1, dh2, q0, row(sm["g_xattn"], 0), k0, v0, xa0, 0, tm_f, carry=("swap", [g_ffn0]))
    add("ffn0", g_ffn0, recv)
    dg_mem0, g_xa0 = _kv_proj_bwd(mem, row(sm["g_mem"], 0), dk, dv, xa0, g_xa0, 0)
    da2, dc2, dgb, g["ev_a_ln_g"], g["ev_a_ln_b"], g_evo, recv = _ev_mix_bwd(
        dh1, cat, a2, c2, z, sm["lng_a"], sm["lnb_a"], evo, tm_f, carry=("swap", [g_xa0]))
    add("xa0", g_xa0, recv)
    handle, token = _comm_start([parts["ffn0"], parts["xa0"]], [], [g_evo], "ev")
    dx, g["ev_a_conv_w"], g["ev_a_conv_b"], g["ev_b_conv_w"], g["ev_b_conv_b"], dg_mix0, g_cev = _ev_in_bwd(
        x, dh1, z, da2, dc2, dgb, row(sm["g_mix"], 0), sm["wa"], sm["wb"], wev, tm_f, token)
    (parts["ffn0"], parts["xa0"], g_evo), (gots["ffn0"], gots["xa0"], recv) = _comm_wait(handle, g_cev)
    add("evo", g_evo, recv)
    g["g_mix"] = jnp.concatenate([dg_mix0, dg_mix1], axis=0)
    g["g_xattn"] = jnp.concatenate([dg_xa0, dg_xa1], axis=0)
    g["g_mem"] = jnp.concatenate([dg_mem0, dg_mem1], axis=0)
    g["g_ffn"] = jnp.concatenate([dg_ffn0, dg_ffn1], axis=0)
    ga, gb = _small_grad_packs(g, loss)
    last = (g_cev, ga[None], gb[None])
    for name, piece, recv in zip(("cev", "ga", "gb"), last, _sibling_swap(last, "rs_swap_last")):
        add(name, piece, recv)
    return dx, parts, gots


def kernel(x, mem, g_mix, g_xattn, g_mem, g_ffn, g_final, ev_w_in, ev_a_conv_w, ev_a_conv_b, ev_a_ln_g, ev_a_ln_b, ev_b_conv_w, ev_b_conv_b, ev_w_out, od_w_in, od_c_ln_g, od_c_ln_b, od_w_s, od_b_s, od_w_out, xa_w_q, xa_w_k, xa_w_v, xa_w_o, ffn_w_gate, ffn_w_up, ffn_w_down, loss_target, m_g_mix, m_g_xattn, m_g_mem, m_g_ffn, m_g_final, m_ev_w_in, m_ev_a_conv_w, m_ev_a_conv_b, m_ev_a_ln_g, m_ev_a_ln_b, m_ev_b_conv_w, m_ev_b_conv_b, m_ev_w_out, m_od_w_in, m_od_c_ln_g, m_od_c_ln_b, m_od_w_s, m_od_b_s, m_od_w_out, m_xa_w_q, m_xa_w_k, m_xa_w_v, m_xa_w_o, m_ffn_w_gate, m_ffn_w_up, m_ffn_w_down, v_g_mix, v_g_xattn, v_g_mem, v_g_ffn, v_g_final, v_ev_w_in, v_ev_a_conv_w, v_ev_a_conv_b, v_ev_a_ln_g, v_ev_a_ln_b, v_ev_b_conv_w, v_ev_b_conv_b, v_ev_w_out, v_od_w_in, v_od_c_ln_g, v_od_c_ln_b, v_od_w_s, v_od_b_s, v_od_w_out, v_xa_w_q, v_xa_w_k, v_xa_w_v, v_xa_w_o, v_ffn_w_gate, v_ffn_w_up, v_ffn_w_down):
    W = dict(g_mix=g_mix, g_xattn=g_xattn, g_mem=g_mem, g_ffn=g_ffn, g_final=g_final, ev_w_in=ev_w_in,
             ev_a_conv_w=ev_a_conv_w, ev_a_conv_b=ev_a_conv_b, ev_a_ln_g=ev_a_ln_g, ev_a_ln_b=ev_a_ln_b,
             ev_b_conv_w=ev_b_conv_w, ev_b_conv_b=ev_b_conv_b, ev_w_out=ev_w_out, od_w_in=od_w_in,
             od_c_ln_g=od_c_ln_g, od_c_ln_b=od_c_ln_b, od_w_s=od_w_s, od_b_s=od_b_s, od_w_out=od_w_out,
             xa_w_q=xa_w_q, xa_w_k=xa_w_k, xa_w_v=xa_w_v, xa_w_o=xa_w_o, ffn_w_gate=ffn_w_gate,
             ffn_w_up=ffn_w_up, ffn_w_down=ffn_w_down)
    M = dict(g_mix=m_g_mix, g_xattn=m_g_xattn, g_mem=m_g_mem, g_ffn=m_g_ffn, g_final=m_g_final, ev_w_in=m_ev_w_in,
             ev_a_conv_w=m_ev_a_conv_w, ev_a_conv_b=m_ev_a_conv_b, ev_a_ln_g=m_ev_a_ln_g, ev_a_ln_b=m_ev_a_ln_b,
             ev_b_conv_w=m_ev_b_conv_w, ev_b_conv_b=m_ev_b_conv_b, ev_w_out=m_ev_w_out, od_w_in=m_od_w_in,
             od_c_ln_g=m_od_c_ln_g, od_c_ln_b=m_od_c_ln_b, od_w_s=m_od_w_s, od_b_s=m_od_b_s, od_w_out=m_od_w_out,
             xa_w_q=m_xa_w_q, xa_w_k=m_xa_w_k, xa_w_v=m_xa_w_v, xa_w_o=m_xa_w_o, ffn_w_gate=m_ffn_w_gate,
             ffn_w_up=m_ffn_w_up, ffn_w_down=m_ffn_w_down)
    V = dict(g_mix=v_g_mix, g_xattn=v_g_xattn, g_mem=v_g_mem, g_ffn=v_g_ffn, g_final=v_g_final, ev_w_in=v_ev_w_in,
             ev_a_conv_w=v_ev_a_conv_w, ev_a_conv_b=v_ev_a_conv_b, ev_a_ln_g=v_ev_a_ln_g, ev_a_ln_b=v_ev_a_ln_b,
             ev_b_conv_w=v_ev_b_conv_w, ev_b_conv_b=v_ev_b_conv_b, ev_w_out=v_ev_w_out, od_w_in=v_od_w_in,
             od_c_ln_g=v_od_c_ln_g, od_c_ln_b=v_od_c_ln_b, od_w_s=v_od_w_s, od_b_s=v_od_b_s, od_w_out=v_od_w_out,
             xa_w_q=v_xa_w_q, xa_w_k=v_xa_w_k, xa_w_v=v_xa_w_v, xa_w_o=v_xa_w_o, ffn_w_gate=v_ffn_w_gate,
             ffn_w_up=v_ffn_w_up, ffn_w_down=v_ffn_w_down)
    order = list(W)
    cx, cy, cc = _pos()
    chip = (2 * cx + cy).astype(jnp.int32)
    cc = cc.astype(jnp.int32)
    chip_idx = jnp.reshape(chip, (1,))
    c_idx = jnp.reshape(cc, (1,))

    tr = lambda a: jnp.swapaxes(a, 1, 2)
    P = {}
    cev, P["cod"], gs = _pack_cols(chip_idx, od_w_in, ev_w_in, ev_a_conv_w, ev_b_conv_w, od_c_ln_g, od_c_ln_b)
    ((P["evo"], P["odo"], P["dn0"], P["dn1"], P["wg0"], P["wg1"], P["wu0"], P["wu1"], P["xa0"], P["xa1"]),
     (P["cev"], P["gs"])) = _pack_rows(chip_idx, ffn_w_down, tr(ffn_w_gate), tr(ffn_w_up), xa_w_q, xa_w_o,
                                       xa_w_k, xa_w_v, ev_w_out, od_w_out, first=[cev, gs])

    dx, parts, gots = _step(x[0], mem[0], loss_target[0], W, P, c_idx, TM_FWD, TM_BWD, TM_WGRAD)

    chip_c = jnp.stack([chip, cc])
    zero_c = jnp.stack([jnp.zeros((), jnp.int32), cc])
    def sums(names):
        return [_sum_chips(parts[n], gots[n], zero_c if n in ("ga", "gb") else chip_c, f"rs_sum_{n}")
                for n in names]

    last = ("evo", "cev", "ga", "gb")
    handle, token = _comm_start([parts["evo"], parts["cev"]], [parts["ga"][0], parts["gb"][0]], [], "last")
    big_names, rest_names = ["ffn1", "ffn0"], ["xa1", "xa0", "cod", "odo"]
    join_big, token = _join_start(sums(big_names), "ffn", after=(token,))
    join_rest, token = _join_start(sums(rest_names), "rest", after=(token,))
    red = dict(zip(big_names, _join_wait(join_big, token)))

    res = {}

    def adamw(name, srcs, gblock, gidx, rb):
        t = tr if name in ("ffn_w_gate", "ffn_w_up") else (lambda a: a)
        outs = _adamw_big(t(W[name]), t(M[name]), t(V[name]), srcs, gblock, gidx, f"adamw_{name}", rb)
        res[name] = tuple(t(o) for o in outs)

    ffn = [red["ffn0"], red["ffn1"]]
    adamw("ffn_w_gate", ffn, (FSH, D), lambda i: (0, 0), FSH)
    adamw("ffn_w_up", ffn, (FSH, D), lambda i: (1, 0), FSH)
    adamw("ffn_w_down", ffn, (FSH, D), lambda i: (2, 0), FSH)
    red.update(zip(rest_names, _join_wait(join_rest, res["ffn_w_down"][1])))
    xa_names = ("xa_w_q", "xa_w_o", "xa_w_k", "xa_w_v")
    assert (XQ, XO, XK, XV) == (0, 1, 2, 3)
    for name, outs in zip(xa_names, _adamw_rows([W[n] for n in xa_names], [M[n] for n in xa_names],
                                                [V[n] for n in xa_names], [red["xa0"], red["xa1"]], "adamw_xa")):
        res[name] = outs
    adamw("od_w_out", [red["odo"]], (256, D), lambda i: (0, 0), 256)
    adamw("od_w_in", [red["cod"]], (256, 512), lambda i: (i, 0), 256)

    srcs_done, got_last = _comm_wait(handle, res["od_w_in"][1])
    parts.update(zip(last, srcs_done[:2] + [a[None] for a in srcs_done[2:]]))
    gots.update(zip(last, got_last))
    red.update(zip(last, _sibling_join(sums(last), "rs_join_last")))
    adamw("ev_w_out", [red["evo"]], (256, D), lambda i: (0, 0), 256)
    adamw("ev_w_in", [red["cev"]], (256, 640), lambda i: (i, 0), 256)
    shape2 = {name: shape for name, shape, *_ in _SMALL}
    wmv = [tuple(d[name].reshape(shape2[name]) for d in (W, M, V)) for name, *_ in _SMALL]
    for (name, *_), outs in zip(_SMALL, _adamw_small(red["ga"], red["gb"], wmv)):
        res[name] = tuple(o.reshape(W[name].shape) for o in outs)

    grad_x = dx[None]
    loss = red["ga"][LOSS_ROW, 0]
    return (loss, grad_x, *[res[n][0] for n in order], *[res[n][1] for n in order],
            *[res[n][2] for n in order], *[res[n][3] for n in order])
```

```python
import jax
import jax.numpy as jnp
from jax import lax
from jax.experimental import pallas as pl
from jax.experimental.pallas import tpu as pltpu

BF = jnp.bfloat16
F32 = jnp.float32

D = 1024
DA = 512
DZ = 2560
DFF = 2816
NSH = 4
FSH = DFF // NSH
FF_CHUNKS = ((0, 768), (768, 1536), (1536, 2304), (2304, 2816))
NMEM = 256
HEADS = 4
HD = D // HEADS
CHUNK = 128
GROUPS = 8
KA = 31
KB = 3
HALO = 16
RC = 32
RMS_EPS = 1e-6
LN_EPS = 1e-5
ATT_SCALE = HD ** -0.5
TM_FWD = 512
TM_BWD = 256
TM_WGRAD = 1024

ADAM_LR = 0.001
ADAM_B1 = 0.9
ADAM_B2 = 0.999
ADAM_EPS = 1e-08
ADAM_WD = 0.01
ADAM_STEP = 10

XQ, XO, XK, XV = 0, 1, 2, 3
SP_ROWS = 48
SA_ROWS = 64

MESH = pl.DeviceIdType.MESH
ANY = pl.BlockSpec(memory_space=pl.ANY)


def _dot(a, b):
    return jnp.dot(a, b, preferred_element_type=F32)


def _dot_nt(a, b):
    return lax.dot_general(a, b, (((1,), (1,)), ((), ())), preferred_element_type=F32)


def _dot_tn(a, b):
    return lax.dot_general(a, b, (((0,), (0,)), ((), ())), preferred_element_type=F32)


def _rms(h):
    r = lax.rsqrt(jnp.mean(h * h, axis=-1, keepdims=True) + RMS_EPS)
    return h * r, r


def _rms_bwd(dn, hhat, r, g):
    dhh = dn * g
    return r * (dhh - hhat * jnp.mean(dhh * hhat, axis=-1, keepdims=True))


def _ln(x):
    mu = jnp.mean(x, axis=-1, keepdims=True)
    xc = x - mu
    rstd = lax.rsqrt(jnp.mean(xc * xc, axis=-1, keepdims=True) + LN_EPS)
    return xc * rstd, rstd


def _ln_bwd(dy, y, rstd):
    return rstd * (dy - jnp.mean(dy, axis=-1, keepdims=True) - y * jnp.mean(dy * y, axis=-1, keepdims=True))


def _colsum(x):
    return jnp.sum(x, axis=0, keepdims=True)


def _sigmoid(x):
    return 1.0 / (1.0 + jnp.exp(-x))


_GELU_C = 0.7978845608028654
_GELU_A = 0.044715


def _gelu(x):
    t = jnp.tanh(_GELU_C * (x + _GELU_A * x * x * x))
    return 0.5 * x * (1.0 + t)


def _gelu_grad(x):
    t = jnp.tanh(_GELU_C * (x + _GELU_A * x * x * x))
    return 0.5 * (1.0 + t) + 0.5 * x * (1.0 - t * t) * _GELU_C * (1.0 + 3.0 * _GELU_A * x * x)


def _first(i):
    return i == 0


def _last(i):
    return i == pl.num_programs(0) - 1


def _const(shape, idx):
    return pl.BlockSpec(shape, lambda *_, _idx=tuple(idx): _idx, pipeline_mode=pl.Buffered(1))


def _whole(a):
    return _const(a.shape, (0,) * a.ndim)


def _rows(tm, width, col=0):
    return pl.BlockSpec((tm, width), lambda i, _c=col: (i, _c))


def _params(ndim=1, vmem_mb=56):
    return pltpu.CompilerParams(dimension_semantics=("arbitrary",) * ndim, vmem_limit_bytes=vmem_mb << 20)


def _sds(shape, dtype):
    return jax.ShapeDtypeStruct(shape, dtype)


def _pos():
    return lax.axis_index("x"), lax.axis_index("y"), lax.axis_index("c")


def _other_chips(x, y):
    return [(1 - x, y), (x, 1 - y), (1 - x, 1 - y)]


def _half(ref, hc, lead=()):
    r = ref.shape[-2] // 2
    return ref.at[(*lead, pl.ds(pl.multiple_of(hc * r, 8), r), slice(None))]


def _rcopy(src, dst, ssem, rsem, to):
    return pltpu.make_async_remote_copy(src_ref=src, dst_ref=dst, send_sem=ssem, recv_sem=rsem,
                                        device_id=to, device_id_type=MESH)


def _gather_start(bufs, sems):
    ici_s, ici_r, _, _ = sems
    x, y, c = _pos()
    me = 2 * x + y
    for k, (cx, cy) in enumerate(_other_chips(x, y)):
        for t, buf in enumerate(bufs):
            mine = _half(buf, c, (me,))
            _rcopy(mine, mine, ici_s.at[3 * t + k], ici_r.at[3 * t + k], (cx, cy, c)).start()


def _gather_forward(bufs, sems):
    ici_s, ici_r, d2d_s, d2d_r = sems
    x, y, c = _pos()
    sib = (x, y, 1 - c)
    for k, (cx, cy) in enumerate(_other_chips(x, y)):
        for t, buf in enumerate(bufs):
            landed = _half(buf, c, (2 * cx + cy,))
            _rcopy(landed, landed, ici_s.at[3 * t + k], ici_r.at[3 * t + k], (cx, cy, c)).wait_recv()
            _rcopy(landed, landed, d2d_s.at[3 * t + k], d2d_r.at[3 * t + k], sib).start()


def _gather_drain(bufs, sems):
    ici_s, ici_r, d2d_s, d2d_r = sems
    x, y, c = _pos()
    me = 2 * x + y
    sib = (x, y, 1 - c)
    chips = _other_chips(x, y)
    for k, (cx, cy) in enumerate(chips):
        for t, buf in enumerate(bufs):
            got = _half(buf, 1 - c, (2 * cx + cy,))
            _rcopy(got, got, d2d_s.at[3 * t + k], d2d_r.at[3 * t + k], sib).wait_recv()
    for k, (cx, cy) in enumerate(chips):
        for t, buf in enumerate(bufs):
            mine = _half(buf, c, (me,))
            _rcopy(mine, mine, ici_s.at[3 * t + k], ici_r.at[3 * t + k], (cx, cy, c)).wait_send()
            landed = _half(buf, c, (2 * cx + cy,))
            _rcopy(landed, landed, d2d_s.at[3 * t + k], d2d_r.at[3 * t + k], sib).wait_send()


def _xchg_copies(srcs, dsts, nbig, sems):
    ssem, rsem = sems
    x, y, c = _pos()
    cps = []
    for k, (cx, cy) in enumerate(_other_chips(x, y)):
        for t, (src, dst) in enumerate(zip(srcs, dsts)):
            s = src.at[2 * cx + cy] if t < nbig else src
            cps.append(_rcopy(s, dst.at[k], ssem.at[3 * t + k], rsem.at[3 * t + k], (cx, cy, c)))
    return cps


def _gather_sems(n):
    return [pltpu.SemaphoreType.DMA((3 * n,))] * 4


def _xchg_sems(n):
    return [pltpu.SemaphoreType.DMA((3 * n,))] * 2


def _xchg_out_shapes(items):
    return [_sds((3,) + a.shape[-2:], a.dtype) for a in items]


def _swap_copies(srcs, dsts, sems):
    ssem, rsem = sems
    x, y, c = _pos()
    cps = []
    for t, (src, dst) in enumerate(zip(srcs, dsts)):
        lead = (slice(None),) * (len(src.shape) - 2)
        cps.append(_rcopy(_half(src, 1 - c, lead), dst, ssem.at[t], rsem.at[t], (x, y, 1 - c)))
    return cps


def _swap_out_shapes(items):
    return [_sds(a.shape[:-2] + (a.shape[-2] // 2, a.shape[-1]), a.dtype) for a in items]


def _call(body, *, name, grid, in_specs, out_specs, out_shape, args, scratch_shapes=(), aliases=None, carry=None):
    in_specs, out_specs, out_shape = list(in_specs), list(out_specs), list(out_shape)
    args = list(args)
    aliases = dict(aliases or {})
    n_in, n_out, n_scr = len(args), len(out_shape), len(scratch_shapes)
    carries = [] if not carry else ([carry] if isinstance(carry, tuple) else list(carry))
    plans = []
    cin, cout_shape, sems = [], [], []
    for c in carries:
        kind = c[0]
        items = list(c[1]) + (list(c[2]) if kind == "xchg" else [])
        if kind == "gather":
            shapes, sm = [_sds(a.shape, a.dtype) for a in items], _gather_sems(len(items))
            for t in range(len(items)):
                aliases[n_in + len(cin) + t] = n_out + len(cin) + t
        elif kind == "xchg":
            shapes, sm = _xchg_out_shapes(items), _xchg_sems(len(items))
        else:
            shapes, sm = _swap_out_shapes(items), [pltpu.SemaphoreType.DMA((len(items),))] * 2
        plans.append((kind, len(cin), len(items), len(sems), len(sm), len(c[1])))
        cin += items
        cout_shape += shapes
        sems += sm
    nc = len(cin)

    def wrapped(*refs):
        ins = refs[:n_in]
        c_in = refs[n_in:n_in + nc]
        outs = refs[n_in + nc:n_in + nc + n_out]
        c_out = refs[n_in + nc + n_out:n_in + 2 * nc + n_out]
        scr = refs[n_in + 2 * nc + n_out:n_in + 2 * nc + n_out + n_scr]
        sem = refs[n_in + 2 * nc + n_out + n_scr:]
        i = pl.program_id(0)

        def copies(plan):
            kind, o0, no, s0, ns, nbig = plan
            if kind == "xchg":
                return _xchg_copies(c_in[o0:o0 + no], c_out[o0:o0 + no], nbig, sem[s0:s0 + ns])
            return _swap_copies(c_in[o0:o0 + no], c_out[o0:o0 + no], sem[s0:s0 + ns])

        if plans:
            @pl.when(_first(i))
            def _():
                for plan in plans:
                    kind, o0, no, s0, ns, _ = plan
                    if kind == "gather":
                        _gather_start(c_out[o0:o0 + no], sem[s0:s0 + ns])
                    else:
                        for cp in copies(plan):
                            cp.start()
        body(*ins, *outs, *scr)
        for plan in plans:
            kind, o0, no, s0, ns, _ = plan
            if kind == "gather":
                @pl.when(i == max(grid[0] - 2, 0))
                def _(o0=o0, no=no, s0=s0, ns=ns):
                    _gather_forward(c_out[o0:o0 + no], sem[s0:s0 + ns])
        if plans:
            @pl.when(_last(i))
            def _():
                for plan in plans:
                    kind, o0, no, s0, ns, _ = plan
                    if kind == "gather":
                        _gather_drain(c_out[o0:o0 + no], sem[s0:s0 + ns])
                    else:
                        for cp in copies(plan):
                            cp.wait()

    outs = pl.pallas_call(
        wrapped, name=name, grid=grid,
        in_specs=in_specs + [ANY] * nc, out_specs=out_specs + [ANY] * nc,
        out_shape=out_shape + cout_shape,
        scratch_shapes=list(scratch_shapes) + sems,
        input_output_aliases=aliases,
        compiler_params=_params(len(grid)),
    )(*args, *cin)
    return list(outs)


def _sibling_swap(items, name):
    n = len(items)

    def body(*refs):
        src = refs[:n]
        dst = refs[n:2 * n]
        ssem, rsem = refs[2 * n:]
        x, y, c = _pos()
        sib = (x, y, 1 - c)
        cps = []
        for t in range(n):
            lead = (slice(None),) * (len(src[t].shape) - 2)
            cp = _rcopy(_half(src[t], 1 - c, lead), dst[t], ssem.at[t], rsem.at[t], sib)
            cp.start()
            cps.append(cp)
        for cp in cps:
            cp.wait()

    return pl.pallas_call(
        body, name=name, in_specs=[ANY] * n, out_specs=[ANY] * n,
        out_shape=[_sds(a.shape[:-2] + (a.shape[-2] // 2, a.shape[-1]), a.dtype) for a in items],
        scratch_shapes=[pltpu.SemaphoreType.DMA((n,)), pltpu.SemaphoreType.DMA((n,))],
    )(*items)


_HBM = pl.BlockSpec(memory_space=pltpu.HBM)
_SEM = pl.BlockSpec(memory_space=pltpu.SEMAPHORE)
_EFFECT = pltpu.SideEffectType.DATAFLOW_SIDE_EFFECTING


def _split_copies(src, land, nbig, nx, sems):
    return (_xchg_copies(src[:nx], land[:nx], nbig, sems[0:2]) if nx else []) + \
        (_swap_copies(src[nx:], land[nx:], sems[2:4]) if len(src) > nx else [])


def _comm_start(big, small, swaps, tag):
    xs = list(big) + list(small)
    items = xs + list(swaps)
    n, nx = len(items), len(xs)

    def body(*refs):
        for cp in _split_copies(refs[:n], refs[n:2 * n], len(big), nx, refs[2 * n:2 * n + 4]):
            cp.start()
        refs[-1][...] = jnp.zeros_like(refs[-1])

    lands = _xchg_out_shapes(xs) + _swap_out_shapes(swaps)
    sem_shapes = [pltpu.SemaphoreType.DMA((max(3 * nx, 1),))] * 2 + [pltpu.SemaphoreType.DMA((max(n - nx, 1),))] * 2
    hbm = lambda a: pltpu.with_memory_space_constraint(a, pltpu.HBM)
    outs = pl.pallas_call(
        body, name=f"rs_start_{tag}",
        out_shape=sem_shapes + [pltpu.HBM(a.shape, a.dtype) for a in items]
        + [pltpu.HBM(a.shape, a.dtype) for a in lands] + [_sds((8, 128), F32)],
        in_specs=[_HBM] * (2 * n), out_specs=[_SEM] * 4 + [_HBM] * (2 * n) + [pl.BlockSpec(memory_space=pltpu.VMEM)],
        input_output_aliases={t: 4 + t for t in range(2 * n)},
        compiler_params=pltpu.CompilerParams(has_side_effects=_EFFECT),
    )(*[hbm(a) for a in items], *[hbm(lax.empty(a.shape, a.dtype)) for a in lands])
    handle = (list(outs[:4]), list(outs[4:4 + n]), list(outs[4 + n:4 + 2 * n]), len(big), nx, tag)
    return handle, outs[-1]


def _comm_wait(handle, after):
    sems, srcs, lands, nbig, nx, tag = handle
    n = len(srcs)

    def body(*refs):
        for cp in _split_copies(refs[:n], refs[n:2 * n], nbig, nx, refs[2 * n:2 * n + 4]):
            cp.wait_send()
            cp.wait_recv()

    outs = pl.pallas_call(
        body, name=f"rs_wait_{tag}",
        out_shape=[pltpu.HBM(a.shape, a.dtype) for a in srcs] + [pltpu.HBM(a.shape, a.dtype) for a in lands],
        in_specs=[_HBM] * (2 * n) + [_SEM] * 4 + [ANY], out_specs=[_HBM] * (2 * n),
        input_output_aliases={t: t for t in range(2 * n)},
        compiler_params=pltpu.CompilerParams(has_side_effects=_EFFECT),
    )(*srcs, *lands, *sems, after)
    return list(outs[:n]), list(outs[n:])


def _join_copies(items, sems):
    ssem, rsem = sems
    x, y, c = _pos()
    return [_rcopy(_half(a, c), _half(a, c), ssem.at[t], rsem.at[t], (x, y, 1 - c)) for t, a in enumerate(items)]


def _join_start(items, tag, after=()):
    n = len(items)

    def body(*refs):
        for cp in _join_copies(refs[:n], refs[n + len(after):n + len(after) + 2]):
            cp.start()
        refs[-1][...] = jnp.zeros_like(refs[-1])

    hbm = lambda a: pltpu.with_memory_space_constraint(a, pltpu.HBM)
    outs = pl.pallas_call(
        body, name=f"rs_join_start_{tag}",
        out_shape=[pltpu.SemaphoreType.DMA((n,))] * 2 + [pltpu.HBM(a.shape, a.dtype) for a in items]
        + [_sds((8, 128), F32)],
        in_specs=[_HBM] * n + [ANY] * len(after),
        out_specs=[_SEM] * 2 + [_HBM] * n + [pl.BlockSpec(memory_space=pltpu.VMEM)],
        input_output_aliases={t: 2 + t for t in range(n)},
        compiler_params=pltpu.CompilerParams(has_side_effects=_EFFECT),
    )(*[hbm(a) for a in items], *after)
    return (list(outs[:2]), list(outs[2:2 + n]), tag), outs[-1]


def _join_wait(handle, after):
    sems, items, tag = handle
    n = len(items)

    def body(*refs):
        x, y, c = _pos()
        ssem, rsem = refs[n:n + 2]
        for t, cp in enumerate(_join_copies(refs[:n], (ssem, rsem))):
            cp.wait_send()
            theirs = _half(refs[t], 1 - c)
            _rcopy(theirs, theirs, ssem.at[t], rsem.at[t], (x, y, 1 - c)).wait_recv()

    outs = pl.pallas_call(
        body, name=f"rs_join_wait_{tag}",
        out_shape=[pltpu.HBM(a.shape, a.dtype) for a in items],
        in_specs=[_HBM] * n + [_SEM] * 2 + [ANY], out_specs=[_HBM] * n,
        input_output_aliases={t: t for t in range(n)},
        compiler_params=pltpu.CompilerParams(has_side_effects=_EFFECT),
    )(*items, *sems, after)
    return list(outs)


def _sibling_join(items, name, after=()):
    n = len(items)

    def body(*refs):
        dst = refs[n + len(after):2 * n + len(after)]
        ssem, rsem = refs[2 * n + len(after):]
        x, y, c = _pos()
        sib = (x, y, 1 - c)
        cps = []
        for t in range(n):
            mine = _half(dst[t], c)
            cp = _rcopy(mine, mine, ssem.at[t], rsem.at[t], sib)
            cp.start()
            cps.append(cp)
        for t, cp in enumerate(cps):
            theirs = _half(dst[t], 1 - c)
            _rcopy(theirs, theirs, ssem.at[t], rsem.at[t], sib).wait_recv()
            cp.wait_send()

    outs = pl.pallas_call(
        body, name=name, in_specs=[ANY] * (n + len(after)), out_specs=[ANY] * n,
        out_shape=[_sds(a.shape, a.dtype) for a in items],
        input_output_aliases={t: t for t in range(n)},
        scratch_shapes=[pltpu.SemaphoreType.DMA((n,))] * 2,
    )(*items, *after)
    return list(outs)


def _ev_in_fwd(h, g, wev, tm, carry=None):
    s = h.shape[0]

    def body(h_ref, g_ref, w_ref, z_ref):
        hhat, _ = _rms(h_ref[...])
        n = (hhat * g_ref[...]).astype(BF)
        z_ref[...] = _dot(n, w_ref[...]).astype(BF)

    return _call(
        body, name="ev_in_fwd", grid=(s // tm,),
        in_specs=[_rows(tm, D), _const((1, D), (0, 0)), _const((D, DZ), (0, 0))],
        out_specs=[_rows(tm, DZ)], out_shape=[_sds((s, DZ), BF)],
        args=(h, g, wev), carry=carry)


def _halo_specs(tm, width, s, col=0):
    per = tm // HALO
    nh = s // HALO
    prev = pl.BlockSpec((HALO, width), lambda i, _c=col: (jnp.maximum(i * per - 1, 0), _c))
    nxt = pl.BlockSpec((HALO, width), lambda i, _c=col: (jnp.minimum((i + 1) * per, nh - 1), _c))
    return prev, nxt


def _fill_pad(pad_ref, prev, main, nxt, i, tm):
    pad_ref[pl.ds(0, HALO), :] = jnp.where(_first(i), 0.0, prev)
    pad_ref[pl.ds(HALO, tm), :] = main
    pad_ref[pl.ds(HALO + tm, HALO), :] = jnp.where(_last(i), 0.0, nxt)


def _shift8(xs_ref, pad_ref, tm):
    for j in range(8):
        xs_ref[j] = pad_ref[pl.ds(j, tm + 2 * HALO - 8), :]


def _tap(xs_ref, r, off):
    return xs_ref[off % 8, pl.ds(r + 8 * (off // 8), RC), :]


def _glu_a(z):
    return z[:, :DA] * _sigmoid(z[:, DA:2 * DA])


def _gate_c(z):
    return z[:, 2048:] * z[:, 1024:1536]


def _ev_mix_fwd(h, z, wa, ba, lng, lnb, wb, bb, evo, tm, carry=None):
    s = h.shape[0]

    def body(h_ref, z_ref, zp_ref, zn_ref, wa_ref, ba_ref, lng_ref, lnb_ref, wb_ref, bb_ref, wo_ref,
             h1_ref, a2_ref, c2_ref, cat_ref, apad, cpad, c2f, xs):
        i = pl.program_id(0)
        zm = z_ref[...].astype(F32)
        zp = zp_ref[...].astype(F32)
        zn = zn_ref[...].astype(F32)
        _fill_pad(apad, _glu_a(zp), _glu_a(zm), _glu_a(zn), i, tm)
        _fill_pad(cpad, _gate_c(zp), _gate_c(zm), _gate_c(zn), i, tm)
        _shift8(xs, apad, tm)
        for r in range(0, tm, RC):
            acc = jnp.zeros((RC, DA), F32) + ba_ref[...]
            for k in range(KA):
                acc = acc + _tap(xs, r, k + 1) * wa_ref[pl.ds(k, 1), :]
            a2_ref[pl.ds(r, RC), :] = acc
            acc = jnp.zeros((RC, DA), F32) + bb_ref[...]
            for k in range(KB):
                acc = acc + cpad[pl.ds(r + HALO - 1 + k, RC), :] * wb_ref[pl.ds(k, 1), :]
            c2f[pl.ds(r, RC), :] = acc
        y, _ = _ln(a2_ref[...])
        a3 = y * lng_ref[...] + lnb_ref[...]
        cat_ref[:, :DA] = (a3 * _sigmoid(a3)).astype(BF)
        c2 = c2f[...]
        c2_ref[...] = c2.astype(BF)
        cat_ref[:, DA:] = (zm[:, 1536:2048] * c2).astype(BF)
        h1_ref[...] = h_ref[...] + _dot(cat_ref[...], wo_ref[...].reshape(D, D))

    zp_spec, zn_spec = _halo_specs(tm, DZ, s)
    small = lambda rows: _const((rows, DA), (0, 0))
    return _call(
        body, name="ev_mix_fwd", grid=(s // tm,),
        in_specs=[_rows(tm, D), _rows(tm, DZ), zp_spec, zn_spec, small(32), small(1), small(1), small(1),
                  small(8), small(1), _whole(evo)],
        out_specs=[_rows(tm, D), _rows(tm, DA), _rows(tm, DA), _rows(tm, D)],
        out_shape=[_sds((s, D), F32), _sds((s, DA), F32), _sds((s, DA), BF), _sds((s, D), BF)],
        scratch_shapes=[pltpu.VMEM((tm + 2 * HALO, DA), F32), pltpu.VMEM((tm + 2 * HALO, DA), F32),
                        pltpu.VMEM((tm, DA), F32), pltpu.VMEM((8, tm + 2 * HALO - 8, DA), F32)],
        args=(h, z, z, z, wa, ba, lng, lnb, wb, bb, evo), carry=carry)


def _ev_mix_bwd(dh, cat, a2, c2, z, lng, lnb, evo, tm, carry=None):
    s = dh.shape[0]

    def body(dh_ref, cat_ref, a2_ref, c2_ref, zgb_ref, lng_ref, lnb_ref, wo_ref,
             da2_ref, dc2_ref, dgb_ref, dlng_ref, dlnb_ref, gwo_ref, wacc):
        i = pl.program_id(0)

        @pl.when(_first(i))
        def _():
            wacc[...] = jnp.zeros_like(wacc)
            dlng_ref[...] = jnp.zeros_like(dlng_ref)
            dlnb_ref[...] = jnp.zeros_like(dlnb_ref)

        dhb = dh_ref[...].astype(BF)
        dcat = _dot_nt(dhb, wo_ref[...].reshape(D, D))
        wacc[...] += _dot_tn(cat_ref[...], dhb)
        y, rstd = _ln(a2_ref[...])
        a3 = y * lng_ref[...] + lnb_ref[...]
        sg = _sigmoid(a3)
        da3 = dcat[:, :DA] * (sg * (1.0 + a3 * (1.0 - sg)))
        dlng_ref[...] += _colsum(da3 * y)
        dlnb_ref[...] += _colsum(da3)
        da2_ref[...] = _ln_bwd(da3 * lng_ref[...], y, rstd).astype(BF)
        db = dcat[:, DA:]
        dc2_ref[...] = (db * zgb_ref[...].astype(F32)).astype(BF)
        dgb_ref[...] = (db * c2_ref[...].astype(F32)).astype(BF)

        @pl.when(_last(i))
        def _():
            gwo_ref[...] = wacc[...].astype(BF).reshape(NSH, 256, D)

    small = _const((1, DA), (0, 0))
    return _call(
        body, name="ev_mix_bwd", grid=(s // tm,),
        in_specs=[_rows(tm, D), _rows(tm, D), _rows(tm, DA), _rows(tm, DA), _rows(tm, DA, 3), small, small,
                  _whole(evo)],
        out_specs=[_rows(tm, DA), _rows(tm, DA), _rows(tm, DA), small, small, _whole(evo)],
        out_shape=[_sds((s, DA), BF), _sds((s, DA), BF), _sds((s, DA), BF), _sds((1, DA), F32),
                   _sds((1, DA), F32), _sds(evo.shape, BF)],
        scratch_shapes=[pltpu.VMEM((D, D), F32)],
        args=(dh, cat, a2, c2, z, lng, lnb, evo), carry=carry)


def _ev_in_bwd(h, dh, z, da2, dc2, dgb, g, wa, wb, wev, tm, after):
    s = h.shape[0]

    def body(h_ref, dh_ref, z_ref, da_ref, dap_ref, dan_ref, dc_ref, dcp_ref, dcn_ref,
             dgb_ref, g_ref, wa_ref, wb_ref, w_ref, after_ref,
             dh0_ref, dwa_ref, dba_ref, dwb_ref, dbb_ref, dg_ref, gw_ref,
             dapad, dcpad, dz, wacc, dwa_acc, dwb_acc, xs):
        i = pl.program_id(0)

        @pl.when(_first(i))
        def _():
            wacc[...] = jnp.zeros_like(wacc)
            dwa_acc[...] = jnp.zeros_like(dwa_acc)
            dwb_acc[...] = jnp.zeros_like(dwb_acc)
            dba_ref[...] = jnp.zeros_like(dba_ref)
            dbb_ref[...] = jnp.zeros_like(dbb_ref)
            dg_ref[...] = jnp.zeros_like(dg_ref)

        da2 = da_ref[...].astype(F32)
        dc2 = dc_ref[...].astype(F32)
        _fill_pad(dapad, dap_ref[...].astype(F32), da2, dan_ref[...].astype(F32), i, tm)
        _fill_pad(dcpad, dcp_ref[...].astype(F32), dc2, dcn_ref[...].astype(F32), i, tm)
        dba_ref[...] += _colsum(da2)
        dbb_ref[...] += _colsum(dc2)

        _shift8(xs, dapad, tm)
        for r in range(0, tm, RC):
            zcol = lambda lo: z_ref[pl.ds(r, RC), lo:lo + DA].astype(F32)
            aval = zcol(0)
            sgr = _sigmoid(zcol(DA))
            a = aval * sgr
            acc = jnp.zeros((RC, DA), F32)
            for k in range(KA):
                t = _tap(xs, r, 2 * HALO - 1 - k)
                acc = acc + t * wa_ref[pl.ds(k, 1), :]
                dwa_acc[k] += (t * a).reshape(RC // 8, 8, DA).sum(axis=0)
            dz[pl.ds(r, RC), 0:DA] = (acc * sgr).astype(BF)
            dz[pl.ds(r, RC), DA:2 * DA] = (acc * aval * sgr * (1.0 - sgr)).astype(BF)
            hb = zcol(1024)
            gc = zcol(2048)
            c = gc * hb
            acc = jnp.zeros((RC, DA), F32)
            for k in range(KB):
                t = dcpad[pl.ds(r + HALO + 1 - k, RC), :]
                acc = acc + t * wb_ref[pl.ds(k, 1), :]
                dwb_acc[k] += (t * c).reshape(RC // 8, 8, DA).sum(axis=0)
            dz[pl.ds(r, RC), 1024:1536] = (acc * gc).astype(BF)
            dz[pl.ds(r, RC), 2048:2560] = (acc * hb).astype(BF)
        dz[:, 1536:2048] = dgb_ref[...]

        gg = g_ref[...]
        hhat, rr = _rms(h_ref[...])
        n = (hhat * gg).astype(BF)
        dzb = dz[...]
        wacc[...] += _dot_tn(n, dzb)
        dn = _dot_nt(dzb, w_ref[...])
        dg_ref[...] += _colsum(dn * hhat)
        dh0_ref[...] = dh_ref[...] + _rms_bwd(dn, hhat, rr, gg)

        @pl.when(_last(i))
        def _():
            for k in range(NSH):
                gw_ref[k] = wacc[:, 640 * k:640 * (k + 1)].astype(BF)
            dwa_ref[...] = jnp.sum(dwa_acc[...], axis=1)
            dwb_ref[...] = jnp.sum(dwb_acc[...], axis=1)

    hp_spec, hn_spec = _halo_specs(tm, DA, s)
    small = lambda rows: _const((rows, DA), (0, 0))
    return _call(
        body, name="ev_in_bwd", grid=(s // tm,),
        in_specs=[_rows(tm, D), _rows(tm, D), _rows(tm, DZ),
                  _rows(tm, DA), hp_spec, hn_spec, _rows(tm, DA), hp_spec, hn_spec, _rows(tm, DA),
                  _const((1, D), (0, 0)), small(32), small(8), _const((D, DZ), (0, 0)), ANY],
        out_specs=[_rows(tm, D), small(32), small(1), small(8), small(1), _const((1, D), (0, 0)),
                   _const((NSH, D, 640), (0, 0, 0))],
        out_shape=[_sds((s, D), F32), _sds((32, DA), F32), _sds((1, DA), F32), _sds((8, DA), F32),
                   _sds((1, DA), F32), _sds((1, D), F32), _sds((NSH, D, 640), BF)],
        scratch_shapes=[pltpu.VMEM((tm + 2 * HALO, DA), F32)] * 2 + [
            pltpu.VMEM((tm, DZ), BF), pltpu.VMEM((D, DZ), F32),
            pltpu.VMEM((32, 8, DA), F32), pltpu.VMEM((8, 8, DA), F32),
            pltpu.VMEM((8, tm + 2 * HALO - 8, DA), F32)],
        args=(h, dh, z, da2, da2, da2, dc2, dc2, dc2, dgb, g, wa, wb, wev, after))


def _xa_spec(item):
    return _const((NSH, 256, D), (0, item, 0))


def _kv_proj(mem, g, xa, layer):
    def body(mem_ref, g_ref, wk_ref, wv_ref, k_ref, v_ref):
        mhat, _ = _rms(mem_ref[...])
        mn = (mhat * g_ref[...]).astype(BF)
        k_ref[...] = _dot(mn, wk_ref[...].reshape(D, D)).astype(BF)
        v_ref[...] = _dot(mn, wv_ref[...].reshape(D, D)).astype(BF)

    full = _const((NMEM, D), (0, 0))
    return _call(
        body, name=f"kv_proj{layer}", grid=(1,),
        in_specs=[full, _const((1, D), (0, 0)), _xa_spec(XK), _xa_spec(XV)],
        out_specs=[full, full], out_shape=[_sds((NMEM, D), BF), _sds((NMEM, D), BF)],
        args=(mem, g, xa, xa))


def _softmax_rows(sc):
    m = jnp.max(sc, axis=-1, keepdims=True)
    e = jnp.exp(sc - m)
    return e / jnp.sum(e, axis=-1, keepdims=True)


def _xattn_fwd(h, g, k, v, xa, layer, tm, carry=None):
    s = h.shape[0]

    def body(h_ref, g_ref, k_ref, v_ref, wq_ref, wo_ref, ho_ref, q_ref, o_scr):
        hh = h_ref[...]
        hhat, _ = _rms(hh)
        n = (hhat * g_ref[...]).astype(BF)
        q = _dot(n, wq_ref[...].reshape(D, D)).astype(BF)
        q_ref[...] = q
        for hd in range(HEADS):
            cs = slice(HD * hd, HD * (hd + 1))
            p = _softmax_rows(_dot_nt(q[:, cs], k_ref[:, cs]) * ATT_SCALE)
            o_scr[:, cs] = _dot(p.astype(BF), v_ref[:, cs]).astype(BF)
        ho_ref[...] = hh + _dot(o_scr[...], wo_ref[...].reshape(D, D))

    kvs = _const((NMEM, D), (0, 0))
    return _call(
        body, name=f"xattn_fwd{layer}", grid=(s // tm,),
        in_specs=[_rows(tm, D), _const((1, D), (0, 0)), kvs, kvs, _xa_spec(XQ), _xa_spec(XO)],
        out_specs=[_rows(tm, D), _rows(tm, D)],
        out_shape=[_sds((s, D), F32), _sds((s, D), BF)],
        scratch_shapes=[pltpu.VMEM((tm, D), BF)],
        args=(h, g, k, v, xa, xa), carry=carry)


def _xattn_bwd(h, dh, q, g, k, v, xa, layer, tm, carry=None):
    s = h.shape[0]

    def body(h_ref, dh_ref, q_ref, g_ref, k_ref, v_ref, wq_ref, wo_ref,
             dhi_ref, dk_ref, dv_ref, dg_ref, gqo_ref, o_scr, dq_scr, qacc, oacc):
        i = pl.program_id(0)

        @pl.when(_first(i))
        def _():
            qacc[...] = jnp.zeros_like(qacc)
            oacc[...] = jnp.zeros_like(oacc)
            dk_ref[...] = jnp.zeros_like(dk_ref)
            dv_ref[...] = jnp.zeros_like(dv_ref)
            dg_ref[...] = jnp.zeros_like(dg_ref)

        gg = g_ref[...]
        dho = dh_ref[...]
        dhb = dho.astype(BF)
        q = q_ref[...]
        do = _dot_nt(dhb, wo_ref[...].reshape(D, D)).astype(BF)
        for hd in range(HEADS):
            cs = slice(HD * hd, HD * (hd + 1))
            kh = k_ref[:, cs]
            vh = v_ref[:, cs]
            p = _softmax_rows(_dot_nt(q[:, cs], kh) * ATT_SCALE)
            pb = p.astype(BF)
            o_scr[:, cs] = _dot(pb, vh).astype(BF)
            doh = do[:, cs]
            dp = _dot_nt(doh, vh)
            dv_ref[:, cs] += _dot_tn(pb, doh)
            ds = (p * (dp - jnp.sum(dp * p, axis=-1, keepdims=True)) * ATT_SCALE).astype(BF)
            dq_scr[:, cs] = _dot(ds, kh).astype(BF)
            dk_ref[:, cs] += _dot_tn(ds, q[:, cs])
        oacc[...] += _dot_tn(o_scr[...], dhb)
        hhat, rr = _rms(h_ref[...])
        n = (hhat * gg).astype(BF)
        dq = dq_scr[...]
        qacc[...] += _dot_tn(n, dq)
        dn = _dot_nt(dq, wq_ref[...].reshape(D, D))
        dg_ref[...] += _colsum(dn * hhat)
        dhi_ref[...] = dho + _rms_bwd(dn, hhat, rr, gg)

        @pl.when(_last(i))
        def _():
            gqo_ref[:, 0:256, :] = qacc[...].astype(BF).reshape(NSH, 256, D)
            gqo_ref[:, 256:512, :] = oacc[...].astype(BF).reshape(NSH, 256, D)

    kvs = _const((NMEM, D), (0, 0))
    return _call(
        body, name=f"xattn_bwd{layer}", grid=(s // tm,),
        in_specs=[_rows(tm, D), _rows(tm, D), _rows(tm, D), _const((1, D), (0, 0)), kvs, kvs,
                  _xa_spec(XQ), _xa_spec(XO)],
        out_specs=[_rows(tm, D), kvs, kvs, _const((1, D), (0, 0)), _const((NSH, 512, D), (0, 0, 0))],
        out_shape=[_sds((s, D), F32), _sds((NMEM, D), F32), _sds((NMEM, D), F32), _sds((1, D), F32),
                   _sds(xa.shape, BF)],
        scratch_shapes=[pltpu.VMEM((tm, D), BF), pltpu.VMEM((tm, D), BF), pltpu.VMEM((D, D), F32),
                        pltpu.VMEM((D, D), F32)],
        args=(h, dh, q, g, k, v, xa, xa), carry=carry)


def _kv_proj_bwd(mem, g, dk, dv, xa, gxa, layer):
    def body(mem_ref, g_ref, dk_ref, dv_ref, wk_ref, wv_ref, gxa_in, dg_ref, gkv_ref):
        mhat, _ = _rms(mem_ref[...])
        mn = (mhat * g_ref[...]).astype(BF)
        dkb = dk_ref[...].astype(BF)
        dvb = dv_ref[...].astype(BF)
        gkv_ref[:, 0:256, :] = _dot_tn(mn, dkb).astype(BF).reshape(NSH, 256, D)
        gkv_ref[:, 256:512, :] = _dot_tn(mn, dvb).astype(BF).reshape(NSH, 256, D)
        dmn = _dot_nt(dkb, wk_ref[...].reshape(D, D)) + _dot_nt(dvb, wv_ref[...].reshape(D, D))
        dg_ref[...] = _colsum(dmn * mhat)

    full = _const((NMEM, D), (0, 0))
    return _call(
        body, name=f"kv_proj_bwd{layer}", grid=(1,),
        in_specs=[full, _const((1, D), (0, 0)), full, full, _xa_spec(XK), _xa_spec(XV), ANY],
        out_specs=[_const((1, D), (0, 0)), _const((NSH, 512, D), (0, 1, 0))],
        out_shape=[_sds((1, D), F32), _sds(gxa.shape, BF)],
        aliases={6: 1},
        args=(mem, g, dk, dv, xa, xa, gxa))


def _ffn_block(h_ref, g_ref, wg_ref, wu_ref, wd_ref, gt_ref, up_ref):
    hh = h_ref[...]
    hhat, _ = _rms(hh)
    n = (hhat * g_ref[...]).astype(BF)
    out = hh
    for c0, c1 in FF_CHUNKS:
        gt = _dot_nt(n, wg_ref[c0:c1, :])
        up = _dot_nt(n, wu_ref[c0:c1, :])
        gt_ref[:, c0:c1] = gt.astype(BF)
        up_ref[:, c0:c1] = up.astype(BF)
        act = (gt * _sigmoid(gt) * up).astype(BF)
        out = out + _dot(act, wd_ref[c0:c1, :])
    return out


def _ffn_fwd(h, g, wg, wu, dn, layer, tm, carry=None):
    s = h.shape[0]
    wg, wu, dn = (a.reshape(DFF, D) for a in (wg, wu, dn))

    def body(h_ref, g_ref, wg_ref, wu_ref, wd_ref, ho_ref, gt_ref, up_ref):
        ho_ref[...] = _ffn_block(h_ref, g_ref, wg_ref, wu_ref, wd_ref, gt_ref, up_ref)

    return _call(
        body, name=f"ffn_fwd{layer}", grid=(s // tm,),
        in_specs=[_rows(tm, D), _const((1, D), (0, 0)), _whole(wg), _whole(wu), _whole(dn)],
        out_specs=[_rows(tm, D), _rows(tm, DFF), _rows(tm, DFF)],
        out_shape=[_sds((s, D), F32), _sds((s, DFF), BF), _sds((s, DFF), BF)],
        args=(h, g, wg, wu, dn), carry=carry)


def _ffn_fwd_loss(h, g, wg, wu, dn, tgt, g_final, layer, tm):
    s = h.shape[0]
    wg, wu, dn = (a.reshape(DFF, D) for a in (wg, wu, dn))

    def body(h_ref, g_ref, wg_ref, wu_ref, wd_ref, t_ref, gf_ref, dh_ref, loss_ref, dg_ref, gt_ref, up_ref):
        i = pl.program_id(0)

        @pl.when(_first(i))
        def _():
            loss_ref[...] = jnp.zeros_like(loss_ref)
            dg_ref[...] = jnp.zeros_like(dg_ref)

        out = _ffn_block(h_ref, g_ref, wg_ref, wu_ref, wd_ref, gt_ref, up_ref)
        gg = gf_ref[...]
        hhat, rr = _rms(out)
        diff = hhat * gg - t_ref[...]
        loss_ref[...] += jnp.sum(diff * diff) * (0.5 / D)
        dout = diff * (1.0 / D)
        dg_ref[...] += _colsum(dout * hhat)
        dh_ref[...] = _rms_bwd(dout, hhat, rr, gg)

    vec = _const((1, D), (0, 0))
    return _call(
        body, name=f"ffn_fwd_loss{layer}", grid=(s // tm,),
        in_specs=[_rows(tm, D), vec, _whole(wg), _whole(wu), _whole(dn), _rows(tm, D), vec],
        out_specs=[_rows(tm, D), _const((1, 128), (0, 0)), vec, _rows(tm, DFF), _rows(tm, DFF)],
        out_shape=[_sds((s, D), F32), _sds((1, 128), F32), _sds((1, D), F32), _sds((s, DFF), BF),
                   _sds((s, DFF), BF)],
        args=(h, g, wg, wu, dn, tgt, g_final))


def _ffn_bwd_dx(h, dh, gt, up, g, wg, wu, dn, layer, tm, carry=None):
    s = h.shape[0]
    wg, wu, dn = (a.reshape(DFF, D) for a in (wg, wu, dn))

    def body(h_ref, dh_ref, gt_ref, up_ref, g_ref, wg_ref, wu_ref, wd_ref,
             dhi_ref, dg_ref, dgt_ref, dup_ref, act_ref, n_ref):
        i = pl.program_id(0)

        @pl.when(_first(i))
        def _():
            dg_ref[...] = jnp.zeros_like(dg_ref)

        gg = g_ref[...]
        dho = dh_ref[...]
        dhb = dho.astype(BF)
        dn_ = jnp.zeros((tm, D), F32)
        for c0, c1 in FF_CHUNKS:
            gtv = gt_ref[:, c0:c1].astype(F32)
            upv = up_ref[:, c0:c1].astype(F32)
            sg = _sigmoid(gtv)
            silu = gtv * sg
            dact = _dot_nt(dhb, wd_ref[c0:c1, :])
            dgt = (dact * upv * (sg * (1.0 + gtv * (1.0 - sg)))).astype(BF)
            dup = (dact * silu).astype(BF)
            dgt_ref[:, c0:c1] = dgt
            dup_ref[:, c0:c1] = dup
            act_ref[:, c0:c1] = (silu * upv).astype(BF)
            dn_ = dn_ + _dot(dgt, wg_ref[c0:c1, :]) + _dot(dup, wu_ref[c0:c1, :])
        hhat, rr = _rms(h_ref[...])
        n_ref[...] = (hhat * gg).astype(BF)
        dg_ref[...] += _colsum(dn_ * hhat)
        dhi_ref[...] = dho + _rms_bwd(dn_, hhat, rr, gg)

    wide = _rows(tm, DFF)
    return _call(
        body, name=f"ffn_bwd_dx{layer}", grid=(s // tm,),
        in_specs=[_rows(tm, D), _rows(tm, D), wide, wide, _const((1, D), (0, 0)),
                  _whole(wg), _whole(wu), _whole(dn)],
        out_specs=[_rows(tm, D), _const((1, D), (0, 0)), wide, wide, wide, _rows(tm, D)],
        out_shape=[_sds((s, D), F32), _sds((1, D), F32), _sds((s, DFF), BF), _sds((s, DFF), BF),
                   _sds((s, DFF), BF), _sds((s, D), BF)],
        args=(h, dh, gt, up, g, wg, wu, dn), carry=carry)


def _ffn_wgrad(a, b, name, tm, slot, into=None, carry=None):
    s = a.shape[0]
    tm = min(tm, s)

    def body(a_ref, b_ref, *rest):
        o_ref, acc = rest[-2:]
        i = pl.program_id(0)

        @pl.when(_first(i))
        def _():
            acc[...] = jnp.zeros_like(acc)

        acc[...] += _dot_tn(a_ref[...], b_ref[...].astype(BF))

        @pl.when(_last(i))
        def _():
            o_ref[...] = acc[...].astype(BF).reshape(NSH, FSH, D)

    chained = into is not None
    return _call(
        body, name=name, grid=(s // tm,),
        in_specs=[_rows(tm, DFF), _rows(tm, D)] + [ANY] * chained,
        out_specs=[_const((NSH, FSH, D), (0, slot, 0))],
        out_shape=[_sds((NSH, 3 * FSH, D), BF)],
        scratch_shapes=[pltpu.VMEM((DFF, D), F32)],
        aliases={2: 0} if chained else None,
        args=(a, b) + ((into,) if chained else ()), carry=carry)


def _sgu_fwd(h, g, lng, lnb, ws, bsb, wod, odo, tm, carry=None):
    s = h.shape[0]

    def body(h_ref, g_ref, lng_ref, lnb_ref, ws_ref, bsb_ref, wi_ref, wo_ref, ho_ref, zp_ref, y_scr):
        hh = h_ref[...]
        hhat, _ = _rms(hh)
        n = (hhat * g_ref[...]).astype(BF)
        zpre = _dot(n, wi_ref[...])
        zp_ref[...] = zpre.astype(BF)
        z = _gelu(zpre)
        u = z[:, :D]
        y, _ = _ln(z[:, D:])
        vn = (y * lng_ref[...] + lnb_ref[...]).astype(BF)
        for c in range(tm // CHUNK):
            rs = slice(CHUNK * c, CHUNK * (c + 1))
            for gi in range(GROUPS):
                cs = slice(CHUNK * gi, CHUNK * (gi + 1))
                sv = _dot(ws_ref[gi], vn[rs, cs]) + bsb_ref[gi]
                y_scr[rs, cs] = (u[rs, cs] * sv).astype(BF)
        ho_ref[...] = hh + _dot(y_scr[...], wo_ref[...].reshape(D, D))

    vec = _const((1, D), (0, 0))
    sq = _const((GROUPS, CHUNK, CHUNK), (0, 0, 0))
    return _call(
        body, name="sgu_fwd", grid=(s // tm,),
        in_specs=[_rows(tm, D), vec, vec, vec, sq, sq, _const((D, 2 * D), (0, 0)), _whole(odo)],
        out_specs=[_rows(tm, D), _rows(tm, 2 * D)],
        out_shape=[_sds((s, D), F32), _sds((s, 2 * D), BF)],
        scratch_shapes=[pltpu.VMEM((tm, D), BF)],
        args=(h, g, lng, lnb, ws, bsb, wod, odo), carry=carry)


def _sgu_bwd(h, dh, zpre, g, lng, lnb, ws, wst, bsb, wod, odo, tm, carry=None):
    s = h.shape[0]

    def body(h_ref, dh_ref, zp_ref, g_ref, lng_ref, lnb_ref, ws_ref, wst_ref, bsb_ref, wi_ref, wo_ref,
             dhi_ref, dws_ref, dbs_ref, dlng_ref, dlnb_ref, dg_ref, gwi_ref, gwo_ref,
             y_scr, dz_scr, dvn_scr, iacc, oacc, bacc):
        i = pl.program_id(0)

        @pl.when(_first(i))
        def _():
            for ref in (iacc, oacc, bacc, dws_ref, dlng_ref, dlnb_ref, dg_ref):
                ref[...] = jnp.zeros_like(ref)

        gg = g_ref[...]
        dho = dh_ref[...]
        dhb = dho.astype(BF)
        zpre_v = zp_ref[...].astype(F32)
        z = _gelu(zpre_v)
        u = z[:, :D]
        yl, rstd = _ln(z[:, D:])
        vn = (yl * lng_ref[...] + lnb_ref[...]).astype(BF)
        dy = _dot_nt(dhb, wo_ref[...].reshape(D, D))
        for c in range(tm // CHUNK):
            rs = slice(CHUNK * c, CHUNK * (c + 1))
            for gi in range(GROUPS):
                cs = slice(CHUNK * gi, CHUNK * (gi + 1))
                vb = vn[rs, cs]
                sv = _dot(ws_ref[gi], vb) + bsb_ref[gi]
                ub = u[rs, cs]
                dyb = dy[rs, cs]
                y_scr[rs, cs] = (ub * sv).astype(BF)
                dz_scr[rs, cs] = dyb * sv
                dsv = dyb * ub
                bacc[gi] += dsv
                dsvb = dsv.astype(BF)
                dws_ref[gi] += _dot_nt(dsvb, vb)
                dvn_scr[rs, cs] = _dot(wst_ref[gi], dsvb)
        oacc[...] += _dot_tn(y_scr[...], dhb)
        dvn = dvn_scr[...]
        dlng_ref[...] += _colsum(dvn * yl)
        dlnb_ref[...] += _colsum(dvn)
        dz_scr[:, D:] = _ln_bwd(dvn * lng_ref[...], yl, rstd)
        dzb = (dz_scr[...] * _gelu_grad(zpre_v)).astype(BF)
        hhat, rr = _rms(h_ref[...])
        n = (hhat * gg).astype(BF)
        iacc[...] += _dot_tn(n, dzb)
        dn = _dot_nt(dzb, wi_ref[...])
        dg_ref[...] += _colsum(dn * hhat)
        dhi_ref[...] = dho + _rms_bwd(dn, hhat, rr, gg)

        @pl.when(_last(i))
        def _():
            for k in range(NSH):
                gwi_ref[k] = iacc[:, 512 * k:512 * (k + 1)].astype(BF)
            gwo_ref[...] = oacc[...].astype(BF).reshape(NSH, 256, D)
            dbs_ref[...] = jnp.sum(bacc[...], axis=-1)

    vec = _const((1, D), (0, 0))
    sq = _const((GROUPS, CHUNK, CHUNK), (0, 0, 0))
    return _call(
        body, name="sgu_bwd", grid=(s // tm,),
        in_specs=[_rows(tm, D), _rows(tm, D), _rows(tm, 2 * D), vec, vec, vec, sq, sq, sq,
                  _const((D, 2 * D), (0, 0)), _whole(odo)],
        out_specs=[_rows(tm, D), sq, _const((GROUPS, CHUNK), (0, 0)), vec, vec, vec,
                   _const((NSH, D, 512), (0, 0, 0)), _whole(odo)],
        out_shape=[_sds((s, D), F32), _sds((GROUPS, CHUNK, CHUNK), F32), _sds((GROUPS, CHUNK), F32),
                   _sds((1, D), F32), _sds((1, D), F32), _sds((1, D), F32), _sds((NSH, D, 512), BF),
                   _sds(odo.shape, BF)],
        scratch_shapes=[pltpu.VMEM((tm, D), BF), pltpu.VMEM((tm, 2 * D), F32),
                        pltpu.VMEM((tm, D), F32), pltpu.VMEM((D, 2 * D), F32), pltpu.VMEM((D, D), F32),
                        pltpu.VMEM((GROUPS, CHUNK, CHUNK), F32)],
        args=(h, dh, zpre, g, lng, lnb, ws, wst, bsb, wod, odo), carry=carry)


def _own_plane(shape):
    return pl.BlockSpec((1,) + shape, lambda i, chip_ref: (chip_ref[0], 0, 0), pipeline_mode=pl.Buffered(1))


def _pack_rows(chip_idx, w_down, w_gate_t, w_up_t, xq, xo, xk, xv, ev_out, od_out, first):
    nf = len(first)

    def body(chip_ref, wd_ref, wg_ref, wu_ref, xq_ref, xo_ref, xk_ref, xv_ref, ev_ref, od_ref, *rest):
        (evo_ref, odo_ref, dn0_ref, dn1_ref, wg0_ref, wg1_ref, wu0_ref, wu1_ref, xa0_ref,
         xa1_ref) = rest[nf:nf + 10]
        gathered = rest[nf + 10:2 * nf + 10]
        sems = rest[2 * nf + 10:]
        _gather_start(gathered, sems)
        evo_ref[0] = ev_ref[0].astype(BF)
        odo_ref[0] = od_ref[0].astype(BF)
        for src, outs in ((wd_ref, (dn0_ref, dn1_ref)), (wg_ref, (wg0_ref, wg1_ref)), (wu_ref, (wu0_ref, wu1_ref))):
            for layer, o_ref in enumerate(outs):
                o_ref[0] = src[layer].astype(BF)
        for layer, xa_ref in enumerate((xa0_ref, xa1_ref)):
            for item, src in ((XQ, xq_ref), (XO, xo_ref), (XK, xk_ref), (XV, xv_ref)):
                xa_ref[0, 256 * item:256 * (item + 1), :] = src[layer].astype(BF)
        _gather_forward(gathered, sems)
        _gather_drain(gathered, sems)

    ins = (w_down, w_gate_t, w_up_t, xq, xo, xk, xv, ev_out, od_out)
    shapes = [(256, D), (256, D)] + [(FSH, D)] * 6 + [(4 * 256, D), (4 * 256, D)]
    grid_spec = pltpu.PrefetchScalarGridSpec(
        num_scalar_prefetch=1, grid=(1,), in_specs=[_whole(a) for a in ins] + [ANY] * nf,
        out_specs=[_own_plane(sh) for sh in shapes] + [ANY] * nf,
        scratch_shapes=_gather_sems(nf))
    outs = pl.pallas_call(
        body, name="pack_rows", grid_spec=grid_spec,
        out_shape=[_sds((NSH,) + sh, BF) for sh in shapes] + [_sds(a.shape, a.dtype) for a in first],
        input_output_aliases={1 + len(ins) + t: len(shapes) + t for t in range(nf)},
        compiler_params=_params(),
    )(chip_idx, *ins, *first)
    return outs[:len(shapes)], outs[len(shapes):]


def _pack_cols(chip_idx, od_in, ev_in, conv_a, conv_b, ln_g, ln_b):
    def body(chip_ref, od_ref, ev_ref, ca_ref, cb_ref, lg_ref, lb_ref, cev_ref, cod_ref, sp_ref):
        cev_ref[0] = ev_ref[0].astype(BF)
        cod_ref[0] = od_ref[0].astype(BF)
        sp_ref[...] = jnp.zeros_like(sp_ref)
        sp_ref[0, 0:KA, 0:128] = ca_ref[0]
        sp_ref[0, 32:32 + KB, 0:128] = cb_ref[0]
        sp_ref[0, 40:41, :] = lg_ref[...]
        sp_ref[0, 41:42, :] = lb_ref[...]

    ins = (od_in, ev_in, conv_a, conv_b, ln_g, ln_b)
    shapes = [(D, 640), (D, 512)]
    grid_spec = pltpu.PrefetchScalarGridSpec(
        num_scalar_prefetch=1, grid=(1,), in_specs=[_whole(a) for a in ins],
        out_specs=[_own_plane(sh) for sh in shapes] + [_own_plane((SP_ROWS, 256))])
    return pl.pallas_call(
        body, name="pack_cols", grid_spec=grid_spec,
        out_shape=[_sds((NSH,) + sh, BF) for sh in shapes] + [_sds((NSH, SP_ROWS, 256), F32)],
        compiler_params=_params(),
    )(chip_idx, *ins)


def _assemble(piece, parts, name):
    tr = 256
    cols = piece.shape[2]

    def body(p_ref, *outs):
        for k in range(NSH):
            for (off, width), o_ref in zip(parts, outs):
                o_ref[:, width * k:width * (k + 1)] = p_ref[k, :, off:off + width]

    return _call(
        body, name=name, grid=(D // tr,),
        in_specs=[pl.BlockSpec((NSH, tr, cols), lambda i: (0, i, 0))],
        out_specs=[pl.BlockSpec((tr, NSH * width), lambda i: (i, 0)) for _, width in parts],
        out_shape=[_sds((D, NSH * width), BF) for _, width in parts],
        args=(piece,))


def _blocks_of(rows, cols, itemsize):
    rb = rows
    while rb * cols * itemsize > (2 << 20) and rb % 2 == 0 and (rb // 2) % 16 == 0:
        rb //= 2
    return rb


def _add_half(full, recv, c_idx, name):
    p, r, cols = full.shape
    rb = _blocks_of(r // 2, cols, 4)
    nb = (r // 2) // rb

    def body(c_ref, a_ref, b_ref, o_ref):
        o_ref[...] = (a_ref[...].astype(F32) + b_ref[...].astype(F32)).astype(o_ref.dtype)

    grid_spec = pltpu.PrefetchScalarGridSpec(
        num_scalar_prefetch=1, grid=(p, nb),
        in_specs=[pl.BlockSpec((1, rb, cols), lambda j, i, c_ref: (j, c_ref[0] * nb + i, 0)),
                  pl.BlockSpec((1, rb, cols), lambda j, i, c_ref: (j, i, 0))],
        out_specs=pl.BlockSpec((1, rb, cols), lambda j, i, c_ref: (j, i, 0)))
    return pl.pallas_call(
        body, name=name, grid_spec=grid_spec, out_shape=_sds(recv.shape, full.dtype),
        compiler_params=_params(2),
    )(c_idx, full, recv)


def _sum_chips(own, got, plane_c, name):
    _, r, cols = own.shape
    rb = _blocks_of(r, cols, 4)
    nb = r // rb

    def body(s_ref, a_ref, b_ref, o_ref):
        f = lambda v: v.astype(F32)
        o_ref[...] = (f(a_ref[0]) + f(b_ref[1])) + (f(b_ref[0]) + f(b_ref[2]))

    grid_spec = pltpu.PrefetchScalarGridSpec(
        num_scalar_prefetch=1, grid=(nb,),
        in_specs=[pl.BlockSpec((1, rb, cols), lambda i, s_ref: (s_ref[0], i, 0)),
                  pl.BlockSpec((3, rb, cols), lambda i, s_ref: (0, i, 0))],
        out_specs=pl.BlockSpec((rb, cols), lambda i, s_ref: (s_ref[1] * nb + i, 0)))
    return pl.pallas_call(
        body, name=name, grid_spec=grid_spec, out_shape=_sds((2 * r, cols), F32),
        compiler_params=_params(1),
    )(plane_c, own, got)


def _adamw_math(w, g, m, v):
    m = ADAM_B1 * m + (1.0 - ADAM_B1) * g
    v = ADAM_B2 * v + (1.0 - ADAM_B2) * (g * g)
    m_hat = m / (1.0 - ADAM_B1 ** ADAM_STEP)
    v_hat = v / (1.0 - ADAM_B2 ** ADAM_STEP)
    delta = -ADAM_LR * (m_hat / (jnp.sqrt(v_hat) + ADAM_EPS) + ADAM_WD * w)
    return delta, m, v


def _adamw_big(w, m, v, gsrcs, gblock, gidx, name, rb):
    nl, r, cols = w.shape
    assert len(gsrcs) == nl

    def body(w_ref, m_ref, v_ref, *rest):
        g_refs, (go_ref, d_ref, mo_ref, vo_ref) = rest[:nl], rest[nl:]
        g = g_refs[0][:, 0:cols]
        if nl == 2:
            g = jnp.where(pl.program_id(0) == 0, g, g_refs[1][:, 0:cols])
        delta, mn, vn = _adamw_math(w_ref[0], g, m_ref[0], v_ref[0])
        go_ref[0] = g
        d_ref[0] = delta
        mo_ref[0] = mn
        vo_ref[0] = vn

    wspec = pl.BlockSpec((1, rb, cols), lambda l, i: (l, i, 0))
    gspec = pl.BlockSpec(gblock, lambda l, i: gidx(i))
    return pl.pallas_call(
        body, name=name, grid=(nl, r // rb),
        in_specs=[wspec, wspec, wspec] + [gspec] * nl,
        out_specs=[wspec] * 4, out_shape=[_sds(w.shape, F32)] * 4,
        compiler_params=_params(2),
    )(w, m, v, *gsrcs)


def _adamw_rows(ws, ms, vs, gsrcs, name, rb=64):
    n = len(ws)
    nl, r, cols = ws[0].shape
    nb = r // rb

    def body(*refs):
        w_refs, m_refs, v_refs = refs[:n], refs[n:2 * n], refs[2 * n:3 * n]
        g_refs = refs[3 * n:3 * n + nl * n]
        outs = refs[3 * n + nl * n:]
        for j in range(n):
            g = g_refs[j][...]
            for l in range(1, nl):
                g = jnp.where(pl.program_id(0) == l, g_refs[l * n + j][...], g)
            delta, mn, vn = _adamw_math(w_refs[j][0], g, m_refs[j][0], v_refs[j][0])
            for o_ref, val in zip(outs[4 * j:4 * j + 4], (g, delta, mn, vn)):
                o_ref[0] = val

    wspec = pl.BlockSpec((1, rb, cols), lambda l, i: (l, i, 0))
    gspecs = [pl.BlockSpec((rb, cols), lambda l, i, _j=j: (_j * nb + i, 0)) for _ in range(nl) for j in range(n)]
    outs = pl.pallas_call(
        body, name=name, grid=(nl, nb),
        in_specs=[wspec] * (3 * n) + gspecs,
        out_specs=[wspec] * (4 * n), out_shape=[_sds(ws[0].shape, F32)] * (4 * n),
        compiler_params=_params(2),
    )(*ws, *ms, *vs, *[gsrcs[l] for l in range(nl) for _ in range(n)])
    return [tuple(outs[4 * j:4 * j + 4]) for j in range(n)]


_SMALL = (
    ("g_mix", (2, D), "a", (0, 2), 0, False),
    ("g_xattn", (2, D), "a", (2, 4), 0, False),
    ("g_mem", (2, D), "a", (4, 6), 0, False),
    ("g_ffn", (2, D), "a", (6, 8), 0, False),
    ("g_final", (1, D), "a", (8, 9), 0, False),
    ("ev_a_conv_w", (KA, 128), "a", (16, 16 + KA), 0, True),
    ("ev_a_conv_b", (1, DA), "a", (11, 12), 0, False),
    ("ev_a_ln_g", (1, DA), "a", (11, 12), DA, False),
    ("ev_a_ln_b", (1, DA), "a", (12, 13), 0, False),
    ("ev_b_conv_w", (KB, 128), "a", (48, 48 + KB), 0, True),
    ("ev_b_conv_b", (1, DA), "a", (12, 13), DA, False),
    ("od_c_ln_g", (1, 256), "a", (9, 10), 0, True),
    ("od_c_ln_b", (1, 256), "a", (10, 11), 0, True),
    ("od_w_s", (GROUPS * CHUNK, CHUNK), "b", (0, GROUPS * CHUNK), 0, False),
    ("od_b_s", (GROUPS, CHUNK), "a", (56, 64), 0, False),
)


def _adamw_small(ga, gb, wmv):
    ns = len(_SMALL)

    def body(*refs):
        ga_ref, gb_ref = refs[0], refs[1]
        ins = refs[2:2 + 3 * ns]
        outs = refs[2 + 3 * ns:]
        x, y, _ = _pos()
        chip = 2 * x + y
        for j, (_, shape, pack, (r0, r1), col, sharded) in enumerate(_SMALL):
            src = ga_ref if pack == "a" else gb_ref
            width = shape[1]
            if sharded:
                g = jnp.zeros(shape, F32)
                for k in range(NSH):
                    blk = src[r0:r1, col + width * k:col + width * (k + 1)]
                    g = g + jnp.where(chip == k, blk, 0.0)
            else:
                g = src[r0:r1, col:col + width]
            w_ref, m_ref, v_ref = ins[3 * j:3 * j + 3]
            delta, mn, vn = _adamw_math(w_ref[...], g, m_ref[...], v_ref[...])
            go_ref, d_ref, mo_ref, vo_ref = outs[4 * j:4 * j + 4]
            go_ref[...] = g
            d_ref[...] = delta
            mo_ref[...] = mn
            vo_ref[...] = vn

    vm = pl.BlockSpec(memory_space=pltpu.VMEM)
    flat = [a for trio in wmv for a in trio]
    out_shape = []
    for (_, shape, *_rest) in _SMALL:
        out_shape += [_sds(shape, F32)] * 4
    outs = pl.pallas_call(
        body, name="adamw_small",
        in_specs=[vm] * (2 + 3 * ns), out_specs=[vm] * (4 * ns), out_shape=out_shape,
    )(ga, gb, *flat)
    return [tuple(outs[4 * j:4 * j + 4]) for j in range(ns)]


def _small_params(gs, W):
    unshard = lambda a: jnp.transpose(a, (1, 0, 2)).reshape(a.shape[1], NSH * a.shape[2])
    od_w_s, od_b_s = W["od_w_s"], W["od_b_s"]
    return dict(
        g_mix=W["g_mix"], g_xattn=W["g_xattn"], g_mem=W["g_mem"], g_ffn=W["g_ffn"],
        g_final=W["g_final"].reshape(1, D),
        wa=unshard(gs[:, 0:32, 0:128]), ba=W["ev_a_conv_b"], lng_a=W["ev_a_ln_g"], lnb_a=W["ev_a_ln_b"],
        wb=unshard(gs[:, 32:40, 0:128]), bb=W["ev_b_conv_b"],
        lng_c=unshard(gs[:, 40:41, :]), lnb_c=unshard(gs[:, 41:42, :]),
        ws=od_w_s[0].astype(BF), wst=jnp.swapaxes(od_w_s[0], 1, 2).astype(BF),
        bsb=jnp.broadcast_to(od_b_s[0][:, :, None], (GROUPS, CHUNK, CHUNK)),
    )


LOSS_ROW = 13


def _small_grad_packs(g, loss):
    z = lambda r, c: jnp.zeros((r, c), F32)
    wide = lambda a: jnp.concatenate([a, z(a.shape[0], D - a.shape[1])], axis=1)
    rows = [g["g_mix"], g["g_xattn"], g["g_mem"], g["g_ffn"], g["g_final"], g["od_c_ln_g"], g["od_c_ln_b"],
            jnp.concatenate([g["ev_a_conv_b"], g["ev_a_ln_g"]], axis=1),
            jnp.concatenate([g["ev_a_ln_b"], g["ev_b_conv_b"]], axis=1),
            wide(loss), z(2, D), wide(g["ev_a_conv_w"]), wide(g["ev_b_conv_w"]), wide(g["od_b_s"])]
    a = jnp.concatenate(rows, axis=0)
    assert a.shape == (SA_ROWS, D), a.shape
    return a, g["od_w_s"].reshape(GROUPS * CHUNK, CHUNK)


def _step(x, mem, tgt, W, P, c_idx, tm_f, tm_b, tm_w):
    row = lambda a, i: a[i:i + 1]

    sm = _small_params(P["gs"], W)
    (wev,) = _assemble(P["cev"], [(0, 640)], "assemble_ev")
    z, evo, wg0 = _ev_in_fwd(x, row(sm["g_mix"], 0), wev, 2 * tm_f, carry=("gather", [P["evo"], P["wg0"]]))
    h1, a2, c2, cat, xa0, wu0 = _ev_mix_fwd(
        x, z, sm["wa"], sm["ba"], sm["lng_a"], sm["lnb_a"], sm["wb"], sm["bb"], evo, 2 * tm_f,
        carry=("gather", [P["xa0"], P["wu0"]]))
    k0, v0 = _kv_proj(mem, row(sm["g_mem"], 0), xa0, 0)
    h2, q0, dn0 = _xattn_fwd(h1, row(sm["g_xattn"], 0), k0, v0, xa0, 0, 2 * tm_f, carry=("gather", [P["dn0"]]))
    h3, gt0, up0, odo, xa1, wg1, cod = _ffn_fwd(h2, row(sm["g_ffn"], 0), wg0, wu0, dn0, 0, tm_f,
                                                carry=("gather", [P["odo"], P["xa1"], P["wg1"], P["cod"]]))
    (wod,) = _assemble(cod, [(0, 512)], "assemble_od")
    h4, zpre, wu1, dn1 = _sgu_fwd(h3, row(sm["g_mix"], 1), sm["lng_c"], sm["lnb_c"], sm["ws"], sm["bsb"], wod,
                                  odo, 2 * tm_f, carry=("gather", [P["wu1"], P["dn1"]]))
    k1, v1 = _kv_proj(mem, row(sm["g_mem"], 1), xa1, 1)
    h5, q1 = _xattn_fwd(h4, row(sm["g_xattn"], 1), k1, v1, xa1, 1, 2 * tm_f)
    dh6, loss, dg_final, gt1, up1 = _ffn_fwd_loss(h5, row(sm["g_ffn"], 1), wg1, wu1, dn1, tgt, sm["g_final"], 1,
                                                  tm_f)

    g = {"g_final": dg_final}
    parts, gots = {}, {}
    def add(name, piece, recv):
        parts[name] = _add_half(piece, recv, c_idx, f"rs_add_{name}")
        return parts[name]

    def ffn_wgrads(layer, dgt, dup, act, nb, dh_out, carry=None):
        g_ffn, *carried = _ffn_wgrad(dgt, nb, f"ffn_wgrad_gate{layer}", tm_w, 0, carry=carry)
        (g_ffn,) = _ffn_wgrad(dup, nb, f"ffn_wgrad_up{layer}", tm_w, 1, into=g_ffn)
        (g_ffn,) = _ffn_wgrad(act, dh_out, f"ffn_wgrad_down{layer}", tm_w, 2, into=g_ffn)
        return g_ffn, carried

    dh5, dg_ffn1, dgt, dup, act, nb = _ffn_bwd_dx(h5, dh6, gt1, up1, row(sm["g_ffn"], 1), wg1, wu1, dn1, 1, tm_b)
    g_ffn1, _ = ffn_wgrads(1, dgt, dup, act, nb, dh6)
    dh4, dk, dv, dg_xa1, g_xa1, recv = _xattn_bwd(
        h4, dh5, q1, row(sm["g_xattn"], 1), k1, v1, xa1, 1, tm_f, carry=("swap", [g_ffn1]))
    add("ffn1", g_ffn1, recv)
    dg_mem1, g_xa1 = _kv_proj_bwd(mem, row(sm["g_mem"], 1), dk, dv, xa1, g_xa1, 1)
    (dh3, g["od_w_s"], g["od_b_s"], g["od_c_ln_g"], g["od_c_ln_b"], dg_mix1, g_cod, g_odo, gots["ffn1"],
     recv) = _sgu_bwd(
        h3, dh4, zpre, row(sm["g_mix"], 1), sm["lng_c"], sm["lnb_c"], sm["ws"], sm["wst"], sm["bsb"], wod, odo, tm_b,
        carry=[("xchg", [parts["ffn1"]], []), ("swap", [g_xa1])])
    add("xa1", g_xa1, recv)
    dh2, dg_ffn0, dgt, dup, act, nb, gots["xa1"], recv_cod, recv_odo = _ffn_bwd_dx(
        h2, dh3, gt0, up0, row(sm["g_ffn"], 0), wg0, wu0, dn0, 0, tm_b,
        carry=[("xchg", [parts["xa1"]], []), ("swap", [g_cod, g_odo])])
    add("cod", g_cod, recv_cod)
    add("odo", g_odo, recv_odo)
    g_ffn0, (gots["cod"], gots["odo"]) = ffn_wgrads(0, dgt, dup, act, nb, dh3,
                                                    carry=("xchg", [parts["cod"], parts["odo"]], []))
    dh1, dk, dv, dg_xa0, g_xa0, recv = _xattn_bwd(
        h1, dh2, q0, row(sm["g_xattn"], 0), k0, v0, xa0, 0, tm_f, carry=("swap", [g_ffn0]))
    add("ffn0", g_ffn0, recv)
    dg_mem0, g_xa0 = _kv_proj_bwd(mem, row(sm["g_mem"], 0), dk, dv, xa0, g_xa0, 0)
    da2, dc2, dgb, g["ev_a_ln_g"], g["ev_a_ln_b"], g_evo, recv = _ev_mix_bwd(
        dh1, cat, a2, c2, z, sm["lng_a"], sm["lnb_a"], evo, 2 * tm_f, carry=("swap", [g_xa0]))
    add("xa0", g_xa0, recv)
    handle, token = _comm_start([parts["ffn0"], parts["xa0"]], [], [g_evo], "ev")
    dx, g["ev_a_conv_w"], g["ev_a_conv_b"], g["ev_b_conv_w"], g["ev_b_conv_b"], dg_mix0, g_cev = _ev_in_bwd(
        x, dh1, z, da2, dc2, dgb, row(sm["g_mix"], 0), sm["wa"], sm["wb"], wev, tm_f, token)
    (parts["ffn0"], parts["xa0"], g_evo), (gots["ffn0"], gots["xa0"], recv) = _comm_wait(handle, g_cev)
    add("evo", g_evo, recv)
    g["g_mix"] = jnp.concatenate([dg_mix0, dg_mix1], axis=0)
    g["g_xattn"] = jnp.concatenate([dg_xa0, dg_xa1], axis=0)
    g["g_mem"] = jnp.concatenate([dg_mem0, dg_mem1], axis=0)
    g["g_ffn"] = jnp.concatenate([dg_ffn0, dg_ffn1], axis=0)
    ga, gb = _small_grad_packs(g, loss)
    last = (g_cev, ga[None], gb[None])
    for name, piece, recv in zip(("cev", "ga", "gb"), last, _sibling_swap(last, "rs_swap_last")):
        add(name, piece, recv)
    return dx, parts, gots


def kernel(x, mem, g_mix, g_xattn, g_mem, g_ffn, g_final, ev_w_in, ev_a_conv_w, ev_a_conv_b, ev_a_ln_g, ev_a_ln_b, ev_b_conv_w, ev_b_conv_b, ev_w_out, od_w_in, od_c_ln_g, od_c_ln_b, od_w_s, od_b_s, od_w_out, xa_w_q, xa_w_k, xa_w_v, xa_w_o, ffn_w_gate, ffn_w_up, ffn_w_down, loss_target, m_g_mix, m_g_xattn, m_g_mem, m_g_ffn, m_g_final, m_ev_w_in, m_ev_a_conv_w, m_ev_a_conv_b, m_ev_a_ln_g, m_ev_a_ln_b, m_ev_b_conv_w, m_ev_b_conv_b, m_ev_w_out, m_od_w_in, m_od_c_ln_g, m_od_c_ln_b, m_od_w_s, m_od_b_s, m_od_w_out, m_xa_w_q, m_xa_w_k, m_xa_w_v, m_xa_w_o, m_ffn_w_gate, m_ffn_w_up, m_ffn_w_down, v_g_mix, v_g_xattn, v_g_mem, v_g_ffn, v_g_final, v_ev_w_in, v_ev_a_conv_w, v_ev_a_conv_b, v_ev_a_ln_g, v_ev_a_ln_b, v_ev_b_conv_w, v_ev_b_conv_b, v_ev_w_out, v_od_w_in, v_od_c_ln_g, v_od_c_ln_b, v_od_w_s, v_od_b_s, v_od_w_out, v_xa_w_q, v_xa_w_k, v_xa_w_v, v_xa_w_o, v_ffn_w_gate, v_ffn_w_up, v_ffn_w_down):
    W = dict(g_mix=g_mix, g_xattn=g_xattn, g_mem=g_mem, g_ffn=g_ffn, g_final=g_final, ev_w_in=ev_w_in,
             ev_a_conv_w=ev_a_conv_w, ev_a_conv_b=ev_a_conv_b, ev_a_ln_g=ev_a_ln_g, ev_a_ln_b=ev_a_ln_b,
             ev_b_conv_w=ev_b_conv_w, ev_b_conv_b=ev_b_conv_b, ev_w_out=ev_w_out, od_w_in=od_w_in,
             od_c_ln_g=od_c_ln_g, od_c_ln_b=od_c_ln_b, od_w_s=od_w_s, od_b_s=od_b_s, od_w_out=od_w_out,
             xa_w_q=xa_w_q, xa_w_k=xa_w_k, xa_w_v=xa_w_v, xa_w_o=xa_w_o, ffn_w_gate=ffn_w_gate,
             ffn_w_up=ffn_w_up, ffn_w_down=ffn_w_down)
    M = dict(g_mix=m_g_mix, g_xattn=m_g_xattn, g_mem=m_g_mem, g_ffn=m_g_ffn, g_final=m_g_final, ev_w_in=m_ev_w_in,
             ev_a_conv_w=m_ev_a_conv_w, ev_a_conv_b=m_ev_a_conv_b, ev_a_ln_g=m_ev_a_ln_g, ev_a_ln_b=m_ev_a_ln_b,
             ev_b_conv_w=m_ev_b_conv_w, ev_b_conv_b=m_ev_b_conv_b, ev_w_out=m_ev_w_out, od_w_in=m_od_w_in,
             od_c_ln_g=m_od_c_ln_g, od_c_ln_b=m_od_c_ln_b, od_w_s=m_od_w_s, od_b_s=m_od_b_s, od_w_out=m_od_w_out,
             xa_w_q=m_xa_w_q, xa_w_k=m_xa_w_k, xa_w_v=m_xa_w_v, xa_w_o=m_xa_w_o, ffn_w_gate=m_ffn_w_gate,
             ffn_w_up=m_ffn_w_up, ffn_w_down=m_ffn_w_down)
    V = dict(g_mix=v_g_mix, g_xattn=v_g_xattn, g_mem=v_g_mem, g_ffn=v_g_ffn, g_final=v_g_final, ev_w_in=v_ev_w_in,
             ev_a_conv_w=v_ev_a_conv_w, ev_a_conv_b=v_ev_a_conv_b, ev_a_ln_g=v_ev_a_ln_g, ev_a_ln_b=v_ev_a_ln_b,
             ev_b_conv_w=v_ev_b_conv_w, ev_b_conv_b=v_ev_b_conv_b, ev_w_out=v_ev_w_out, od_w_in=v_od_w_in,
             od_c_ln_g=v_od_c_ln_g, od_c_ln_b=v_od_c_ln_b, od_w_s=v_od_w_s, od_b_s=v_od_b_s, od_w_out=v_od_w_out,
             xa_w_q=v_xa_w_q, xa_w_k=v_xa_w_k, xa_w_v=v_xa_w_v, xa_w_o=v_xa_w_o, ffn_w_gate=v_ffn_w_gate,
             ffn_w_up=v_ffn_w_up, ffn_w_down=v_ffn_w_down)
    order = list(W)
    cx, cy, cc = _pos()
    chip = (2 * cx + cy).astype(jnp.int32)
    cc = cc.astype(jnp.int32)
    chip_idx = jnp.reshape(chip, (1,))
    c_idx = jnp.reshape(cc, (1,))

    tr = lambda a: jnp.swapaxes(a, 1, 2)
    P = {}
    cev, P["cod"], gs = _pack_cols(chip_idx, od_w_in, ev_w_in, ev_a_conv_w, ev_b_conv_w, od_c_ln_g, od_c_ln_b)
    ((P["evo"], P["odo"], P["dn0"], P["dn1"], P["wg0"], P["wg1"], P["wu0"], P["wu1"], P["xa0"], P["xa1"]),
     (P["cev"], P["gs"])) = _pack_rows(chip_idx, ffn_w_down, tr(ffn_w_gate), tr(ffn_w_up), xa_w_q, xa_w_o,
                                       xa_w_k, xa_w_v, ev_w_out, od_w_out, first=[cev, gs])

    dx, parts, gots = _step(x[0], mem[0], loss_target[0], W, P, c_idx, TM_FWD, TM_BWD, TM_WGRAD)

    chip_c = jnp.stack([chip, cc])
    zero_c = jnp.stack([jnp.zeros((), jnp.int32), cc])
    def sums(names):
        return [_sum_chips(parts[n], gots[n], zero_c if n in ("ga", "gb") else chip_c, f"rs_sum_{n}")
                for n in names]

    last = ("evo", "cev", "ga", "gb")
    handle, token = _comm_start([parts["evo"], parts["cev"]], [parts["ga"][0], parts["gb"][0]], [], "last")
    big_names, rest_names = ["ffn1", "ffn0"], ["xa1", "xa0", "cod", "odo"]
    join_big, token = _join_start(sums(big_names), "ffn", after=(token,))
    join_rest, token = _join_start(sums(rest_names), "rest", after=(token,))
    red = dict(zip(big_names, _join_wait(join_big, token)))

    res = {}

    def adamw(name, srcs, gblock, gidx, rb):
        t = tr if name in ("ffn_w_gate", "ffn_w_up") else (lambda a: a)
        outs = _adamw_big(t(W[name]), t(M[name]), t(V[name]), srcs, gblock, gidx, f"adamw_{name}", rb)
        res[name] = tuple(t(o) for o in outs)

    ffn = [red["ffn0"], red["ffn1"]]
    adamw("ffn_w_gate", ffn, (FSH, D), lambda i: (0, 0), FSH)
    adamw("ffn_w_up", ffn, (FSH, D), lambda i: (1, 0), FSH)
    adamw("ffn_w_down", ffn, (FSH, D), lambda i: (2, 0), FSH)
    red.update(zip(rest_names, _join_wait(join_rest, res["ffn_w_down"][1])))
    xa_names = ("xa_w_q", "xa_w_o", "xa_w_k", "xa_w_v")
    assert (XQ, XO, XK, XV) == (0, 1, 2, 3)
    for name, outs in zip(xa_names, _adamw_rows([W[n] for n in xa_names], [M[n] for n in xa_names],
                                                [V[n] for n in xa_names], [red["xa0"], red["xa1"]], "adamw_xa")):
        res[name] = outs
    adamw("od_w_out", [red["odo"]], (256, D), lambda i: (0, 0), 256)
    adamw("od_w_in", [red["cod"]], (256, 512), lambda i: (i, 0), 256)

    srcs_done, got_last = _comm_wait(handle, res["od_w_in"][1])
    parts.update(zip(last, srcs_done[:2] + [a[None] for a in srcs_done[2:]]))
    gots.update(zip(last, got_last))
    red.update(zip(last, _sibling_join(sums(last), "rs_join_last")))
    adamw("ev_w_out", [red["evo"]], (256, D), lambda i: (0, 0), 256)
    adamw("ev_w_in", [red["cev"]], (256, 640), lambda i: (i, 0), 256)
    shape2 = {name: shape for name, shape, *_ in _SMALL}
    wmv = [tuple(d[name].reshape(shape2[name]) for d in (W, M, V)) for name, *_ in _SMALL]
    for (name, *_), outs in zip(_SMALL, _adamw_small(red["ga"], red["gb"], wmv)):
        res[name] = tuple(o.reshape(W[name].shape) for o in outs)

    grad_x = dx[None]
    loss = red["ga"][LOSS_ROW, 0]
    return (loss, grad_x, *[res[n][0] for n in order], *[res[n][1] for n in order],
            *[res[n][2] for n in order], *[res[n][3] for n in order])
```

```python
import jax
import jax.numpy as jnp
from jax import lax
from jax.experimental import pallas as pl
from jax.experimental.pallas import tpu as pltpu

BF = jnp.bfloat16
F32 = jnp.float32

D = 1024
DA = 512
DZ = 2560
DFF = 2816
NSH = 4
FSH = DFF // NSH
FF_CHUNKS = ((0, 768), (768, 1536), (1536, 2304), (2304, 2816))
NMEM = 256
HEADS = 4
HD = D // HEADS
CHUNK = 128
GROUPS = 8
KA = 31
KB = 3
HALO = 16
RC = 32
RMS_EPS = 1e-6
LN_EPS = 1e-5
ATT_SCALE = HD ** -0.5
TM_FWD = 512
TM_BWD = 256
TM_WGRAD = 1024

ADAM_LR = 0.001
ADAM_B1 = 0.9
ADAM_B2 = 0.999
ADAM_EPS = 1e-08
ADAM_WD = 0.01
ADAM_STEP = 10

XQ, XO, XK, XV = 0, 1, 2, 3
SP_ROWS = 48
SA_ROWS = 64

MESH = pl.DeviceIdType.MESH
ANY = pl.BlockSpec(memory_space=pl.ANY)


def _dot(a, b):
    return jnp.dot(a, b, preferred_element_type=F32)


def _dot_nt(a, b):
    return lax.dot_general(a, b, (((1,), (1,)), ((), ())), preferred_element_type=F32)


def _dot_tn(a, b):
    return lax.dot_general(a, b, (((0,), (0,)), ((), ())), preferred_element_type=F32)


def _rms(h):
    r = lax.rsqrt(jnp.mean(h * h, axis=-1, keepdims=True) + RMS_EPS)
    return h * r, r


def _rms_bwd(dn, hhat, r, g):
    dhh = dn * g
    return r * (dhh - hhat * jnp.mean(dhh * hhat, axis=-1, keepdims=True))


def _ln(x):
    mu = jnp.mean(x, axis=-1, keepdims=True)
    xc = x - mu
    rstd = lax.rsqrt(jnp.mean(xc * xc, axis=-1, keepdims=True) + LN_EPS)
    return xc * rstd, rstd


def _ln_bwd(dy, y, rstd):
    return rstd * (dy - jnp.mean(dy, axis=-1, keepdims=True) - y * jnp.mean(dy * y, axis=-1, keepdims=True))


def _colsum(x):
    return jnp.sum(x, axis=0, keepdims=True)


def _sigmoid(x):
    return 1.0 / (1.0 + jnp.exp(-x))


_GELU_C = 0.7978845608028654
_GELU_A = 0.044715


def _gelu(x):
    t = jnp.tanh(_GELU_C * (x + _GELU_A * x * x * x))
    return 0.5 * x * (1.0 + t)


def _gelu_grad(x):
    t = jnp.tanh(_GELU_C * (x + _GELU_A * x * x * x))
    return 0.5 * (1.0 + t) + 0.5 * x * (1.0 - t * t) * _GELU_C * (1.0 + 3.0 * _GELU_A * x * x)


def _first(i):
    return i == 0


def _last(i):
    return i == pl.num_programs(0) - 1


def _const(shape, idx):
    return pl.BlockSpec(shape, lambda *_, _idx=tuple(idx): _idx, pipeline_mode=pl.Buffered(1))


def _whole(a):
    return _const(a.shape, (0,) * a.ndim)


def _rows(tm, width, col=0):
    return pl.BlockSpec((tm, width), lambda i, _c=col: (i, _c))


def _params(ndim=1, vmem_mb=56):
    return pltpu.CompilerParams(dimension_semantics=("arbitrary",) * ndim, vmem_limit_bytes=vmem_mb << 20)


def _sds(shape, dtype):
    return jax.ShapeDtypeStruct(shape, dtype)


def _pos():
    return lax.axis_index("x"), lax.axis_index("y"), lax.axis_index("c")


def _other_chips(x, y):
    return [(1 - x, y), (x, 1 - y), (1 - x, 1 - y)]


def _half(ref, hc, lead=()):
    r = ref.shape[-2] // 2
    return ref.at[(*lead, pl.ds(pl.multiple_of(hc * r, 8), r), slice(None))]


def _rcopy(src, dst, ssem, rsem, to):
    return pltpu.make_async_remote_copy(src_ref=src, dst_ref=dst, send_sem=ssem, recv_sem=rsem,
                                        device_id=to, device_id_type=MESH)


def _gather_start(bufs, sems):
    ici_s, ici_r, _, _ = sems
    x, y, c = _pos()
    me = 2 * x + y
    for k, (cx, cy) in enumerate(_other_chips(x, y)):
        for t, buf in enumerate(bufs):
            mine = _half(buf, c, (me,))
            _rcopy(mine, mine, ici_s.at[3 * t + k], ici_r.at[3 * t + k], (cx, cy, c)).start()


def _gather_forward(bufs, sems):
    ici_s, ici_r, d2d_s, d2d_r = sems
    x, y, c = _pos()
    sib = (x, y, 1 - c)
    for k, (cx, cy) in enumerate(_other_chips(x, y)):
        for t, buf in enumerate(bufs):
            landed = _half(buf, c, (2 * cx + cy,))
            _rcopy(landed, landed, ici_s.at[3 * t + k], ici_r.at[3 * t + k], (cx, cy, c)).wait_recv()
            _rcopy(landed, landed, d2d_s.at[3 * t + k], d2d_r.at[3 * t + k], sib).start()


def _gather_drain(bufs, sems):
    ici_s, ici_r, d2d_s, d2d_r = sems
    x, y, c = _pos()
    me = 2 * x + y
    sib = (x, y, 1 - c)
    chips = _other_chips(x, y)
    for k, (cx, cy) in enumerate(chips):
        for t, buf in enumerate(bufs):
            got = _half(buf, 1 - c, (2 * cx + cy,))
            _rcopy(got, got, d2d_s.at[3 * t + k], d2d_r.at[3 * t + k], sib).wait_recv()
    for k, (cx, cy) in enumerate(chips):
        for t, buf in enumerate(bufs):
            mine = _half(buf, c, (me,))
            _rcopy(mine, mine, ici_s.at[3 * t + k], ici_r.at[3 * t + k], (cx, cy, c)).wait_send()
            landed = _half(buf, c, (2 * cx + cy,))
            _rcopy(landed, landed, d2d_s.at[3 * t + k], d2d_r.at[3 * t + k], sib).wait_send()


def _xchg_copies(srcs, dsts, nbig, sems):
    ssem, rsem = sems
    x, y, c = _pos()
    cps = []
    for k, (cx, cy) in enumerate(_other_chips(x, y)):
        for t, (src, dst) in enumerate(zip(srcs, dsts)):
            s = src.at[2 * cx + cy] if t < nbig else src
            cps.append(_rcopy(s, dst.at[k], ssem.at[3 * t + k], rsem.at[3 * t + k], (cx, cy, c)))
    return cps


def _gather_sems(n):
    return [pltpu.SemaphoreType.DMA((3 * n,))] * 4


def _xchg_sems(n):
    return [pltpu.SemaphoreType.DMA((3 * n,))] * 2


def _xchg_out_shapes(items):
    return [_sds((3,) + a.shape[-2:], a.dtype) for a in items]


def _swap_copies(srcs, dsts, sems):
    ssem, rsem = sems
    x, y, c = _pos()
    cps = []
    for t, (src, dst) in enumerate(zip(srcs, dsts)):
        lead = (slice(None),) * (len(src.shape) - 2)
        cps.append(_rcopy(_half(src, 1 - c, lead), dst, ssem.at[t], rsem.at[t], (x, y, 1 - c)))
    return cps


def _swap_out_shapes(items):
    return [_sds(a.shape[:-2] + (a.shape[-2] // 2, a.shape[-1]), a.dtype) for a in items]


def _call(body, *, name, grid, in_specs, out_specs, out_shape, args, scratch_shapes=(), aliases=None, carry=None):
    in_specs, out_specs, out_shape = list(in_specs), list(out_specs), list(out_shape)
    args = list(args)
    aliases = dict(aliases or {})
    n_in, n_out, n_scr = len(args), len(out_shape), len(scratch_shapes)
    carries = [] if not carry else ([carry] if isinstance(carry, tuple) else list(carry))
    plans = []
    cin, cout_shape, sems = [], [], []
    for c in carries:
        kind = c[0]
        items = list(c[1]) + (list(c[2]) if kind == "xchg" else [])
        if kind == "gather":
            shapes, sm = [_sds(a.shape, a.dtype) for a in items], _gather_sems(len(items))
            for t in range(len(items)):
                aliases[n_in + len(cin) + t] = n_out + len(cin) + t
        elif kind == "xchg":
            shapes, sm = _xchg_out_shapes(items), _xchg_sems(len(items))
        else:
            shapes, sm = _swap_out_shapes(items), [pltpu.SemaphoreType.DMA((len(items),))] * 2
        plans.append((kind, len(cin), len(items), len(sems), len(sm), len(c[1])))
        cin += items
        cout_shape += shapes
        sems += sm
    nc = len(cin)

    def wrapped(*refs):
        ins = refs[:n_in]
        c_in = refs[n_in:n_in + nc]
        outs = refs[n_in + nc:n_in + nc + n_out]
        c_out = refs[n_in + nc + n_out:n_in + 2 * nc + n_out]
        scr = refs[n_in + 2 * nc + n_out:n_in + 2 * nc + n_out + n_scr]
        sem = refs[n_in + 2 * nc + n_out + n_scr:]
        i = pl.program_id(0)

        def copies(plan):
            kind, o0, no, s0, ns, nbig = plan
            if kind == "xchg":
                return _xchg_copies(c_in[o0:o0 + no], c_out[o0:o0 + no], nbig, sem[s0:s0 + ns])
            return _swap_copies(c_in[o0:o0 + no], c_out[o0:o0 + no], sem[s0:s0 + ns])

        if plans:
            @pl.when(_first(i))
            def _():
                for plan in plans:
                    kind, o0, no, s0, ns, _ = plan
                    if kind == "gather":
                        _gather_start(c_out[o0:o0 + no], sem[s0:s0 + ns])
                    else:
                        for cp in copies(plan):
                            cp.start()
        body(*ins, *outs, *scr)
        for plan in plans:
            kind, o0, no, s0, ns, _ = plan
            if kind == "gather":
                @pl.when(i == max(grid[0] - 2, 0))
                def _(o0=o0, no=no, s0=s0, ns=ns):
                    _gather_forward(c_out[o0:o0 + no], sem[s0:s0 + ns])
        if plans:
            @pl.when(_last(i))
            def _():
                for plan in plans:
                    kind, o0, no, s0, ns, _ = plan
                    if kind == "gather":
                        _gather_drain(c_out[o0:o0 + no], sem[s0:s0 + ns])
                    else:
                        for cp in copies(plan):
                            cp.wait()

    outs = pl.pallas_call(
        wrapped, name=name, grid=grid,
        in_specs=in_specs + [ANY] * nc, out_specs=out_specs + [ANY] * nc,
        out_shape=out_shape + cout_shape,
        scratch_shapes=list(scratch_shapes) + sems,
        input_output_aliases=aliases,
        compiler_params=_params(len(grid)),
    )(*args, *cin)
    return list(outs)


def _sibling_swap(items, name):
    n = len(items)

    def body(*refs):
        src = refs[:n]
        dst = refs[n:2 * n]
        ssem, rsem = refs[2 * n:]
        x, y, c = _pos()
        sib = (x, y, 1 - c)
        cps = []
        for t in range(n):
            lead = (slice(None),) * (len(src[t].shape) - 2)
            cp = _rcopy(_half(src[t], 1 - c, lead), dst[t], ssem.at[t], rsem.at[t], sib)
            cp.start()
            cps.append(cp)
        for cp in cps:
            cp.wait()

    return pl.pallas_call(
        body, name=name, in_specs=[ANY] * n, out_specs=[ANY] * n,
        out_shape=[_sds(a.shape[:-2] + (a.shape[-2] // 2, a.shape[-1]), a.dtype) for a in items],
        scratch_shapes=[pltpu.SemaphoreType.DMA((n,)), pltpu.SemaphoreType.DMA((n,))],
    )(*items)


_HBM = pl.BlockSpec(memory_space=pltpu.HBM)
_SEM = pl.BlockSpec(memory_space=pltpu.SEMAPHORE)
_EFFECT = pltpu.SideEffectType.DATAFLOW_SIDE_EFFECTING


def _split_copies(src, land, nbig, nx, sems):
    return (_xchg_copies(src[:nx], land[:nx], nbig, sems[0:2]) if nx else []) + \
        (_swap_copies(src[nx:], land[nx:], sems[2:4]) if len(src) > nx else [])


def _comm_start(big, small, swaps, tag):
    xs = list(big) + list(small)
    items = xs + list(swaps)
    n, nx = len(items), len(xs)

    def body(*refs):
        for cp in _split_copies(refs[:n], refs[n:2 * n], len(big), nx, refs[2 * n:2 * n + 4]):
            cp.start()
        refs[-1][...] = jnp.zeros_like(refs[-1])

    lands = _xchg_out_shapes(xs) + _swap_out_shapes(swaps)
    sem_shapes = [pltpu.SemaphoreType.DMA((max(3 * nx, 1),))] * 2 + [pltpu.SemaphoreType.DMA((max(n - nx, 1),))] * 2
    hbm = lambda a: pltpu.with_memory_space_constraint(a, pltpu.HBM)
    outs = pl.pallas_call(
        body, name=f"rs_start_{tag}",
        out_shape=sem_shapes + [pltpu.HBM(a.shape, a.dtype) for a in items]
        + [pltpu.HBM(a.shape, a.dtype) for a in lands] + [_sds((8, 128), F32)],
        in_specs=[_HBM] * (2 * n), out_specs=[_SEM] * 4 + [_HBM] * (2 * n) + [pl.BlockSpec(memory_space=pltpu.VMEM)],
        input_output_aliases={t: 4 + t for t in range(2 * n)},
        compiler_params=pltpu.CompilerParams(has_side_effects=_EFFECT),
    )(*[hbm(a) for a in items], *[hbm(lax.empty(a.shape, a.dtype)) for a in lands])
    handle = (list(outs[:4]), list(outs[4:4 + n]), list(outs[4 + n:4 + 2 * n]), len(big), nx, tag)
    return handle, outs[-1]


def _comm_wait(handle, after):
    sems, srcs, lands, nbig, nx, tag = handle
    n = len(srcs)

    def body(*refs):
        for cp in _split_copies(refs[:n], refs[n:2 * n], nbig, nx, refs[2 * n:2 * n + 4]):
            cp.wait_send()
            cp.wait_recv()

    outs = pl.pallas_call(
        body, name=f"rs_wait_{tag}",
        out_shape=[pltpu.HBM(a.shape, a.dtype) for a in srcs] + [pltpu.HBM(a.shape, a.dtype) for a in lands],
        in_specs=[_HBM] * (2 * n) + [_SEM] * 4 + [ANY], out_specs=[_HBM] * (2 * n),
        input_output_aliases={t: t for t in range(2 * n)},
        compiler_params=pltpu.CompilerParams(has_side_effects=_EFFECT),
    )(*srcs, *lands, *sems, after)
    return list(outs[:n]), list(outs[n:])


def _join_copies(items, sems):
    ssem, rsem = sems
    x, y, c = _pos()
    return [_rcopy(_half(a, c), _half(a, c), ssem.at[t], rsem.at[t], (x, y, 1 - c)) for t, a in enumerate(items)]


def _join_start(items, tag, after=()):
    n = len(items)

    def body(*refs):
        for cp in _join_copies(refs[:n], refs[n + len(after):n + len(after) + 2]):
            cp.start()
        refs[-1][...] = jnp.zeros_like(refs[-1])

    hbm = lambda a: pltpu.with_memory_space_constraint(a, pltpu.HBM)
    outs = pl.pallas_call(
        body, name=f"rs_join_start_{tag}",
        out_shape=[pltpu.SemaphoreType.DMA((n,))] * 2 + [pltpu.HBM(a.shape, a.dtype) for a in items]
        + [_sds((8, 128), F32)],
        in_specs=[_HBM] * n + [ANY] * len(after),
        out_specs=[_SEM] * 2 + [_HBM] * n + [pl.BlockSpec(memory_space=pltpu.VMEM)],
        input_output_aliases={t: 2 + t for t in range(n)},
        compiler_params=pltpu.CompilerParams(has_side_effects=_EFFECT),
    )(*[hbm(a) for a in items], *after)
    return (list(outs[:2]), list(outs[2:2 + n]), tag), outs[-1]


def _join_wait(handle, after):
    sems, items, tag = handle
    n = len(items)

    def body(*refs):
        x, y, c = _pos()
        ssem, rsem = refs[n:n + 2]
        for t, cp in enumerate(_join_copies(refs[:n], (ssem, rsem))):
            cp.wait_send()
            theirs = _half(refs[t], 1 - c)
            _rcopy(theirs, theirs, ssem.at[t], rsem.at[t], (x, y, 1 - c)).wait_recv()

    outs = pl.pallas_call(
        body, name=f"rs_join_wait_{tag}",
        out_shape=[pltpu.HBM(a.shape, a.dtype) for a in items],
        in_specs=[_HBM] * n + [_SEM] * 2 + [ANY], out_specs=[_HBM] * n,
        input_output_aliases={t: t for t in range(n)},
        compiler_params=pltpu.CompilerParams(has_side_effects=_EFFECT),
    )(*items, *sems, after)
    return list(outs)


def _sibling_join(items, name, after=()):
    n = len(items)

    def body(*refs):
        dst = refs[n + len(after):2 * n + len(after)]
        ssem, rsem = refs[2 * n + len(after):]
        x, y, c = _pos()
        sib = (x, y, 1 - c)
        cps = []
        for t in range(n):
            mine = _half(dst[t], c)
            cp = _rcopy(mine, mine, ssem.at[t], rsem.at[t], sib)
            cp.start()
            cps.append(cp)
        for t, cp in enumerate(cps):
            theirs = _half(dst[t], 1 - c)
            _rcopy(theirs, theirs, ssem.at[t], rsem.at[t], sib).wait_recv()
            cp.wait_send()

    outs = pl.pallas_call(
        body, name=name, in_specs=[ANY] * (n + len(after)), out_specs=[ANY] * n,
        out_shape=[_sds(a.shape, a.dtype) for a in items],
        input_output_aliases={t: t for t in range(n)},
        scratch_shapes=[pltpu.SemaphoreType.DMA((n,))] * 2,
    )(*items, *after)
    return list(outs)


def _ev_in_fwd(h, g, wev, tm, carry=None):
    s = h.shape[0]

    def body(h_ref, g_ref, w_ref, z_ref):
        hhat, _ = _rms(h_ref[...])
        n = (hhat * g_ref[...]).astype(BF)
        z_ref[...] = _dot(n, w_ref[...]).astype(BF)

    return _call(
        body, name="ev_in_fwd", grid=(s // tm,),
        in_specs=[_rows(tm, D), _const((1, D), (0, 0)), _const((D, DZ), (0, 0))],
        out_specs=[_rows(tm, DZ)], out_shape=[_sds((s, DZ), BF)],
        args=(h, g, wev), carry=carry)


def _halo_specs(tm, width, s, col=0):
    per = tm // HALO
    nh = s // HALO
    prev = pl.BlockSpec((HALO, width), lambda i, _c=col: (jnp.maximum(i * per - 1, 0), _c))
    nxt = pl.BlockSpec((HALO, width), lambda i, _c=col: (jnp.minimum((i + 1) * per, nh - 1), _c))
    return prev, nxt


def _fill_pad(pad_ref, prev, main, nxt, i, tm):
    pad_ref[pl.ds(0, HALO), :] = jnp.where(_first(i), 0.0, prev)
    pad_ref[pl.ds(HALO, tm), :] = main
    pad_ref[pl.ds(HALO + tm, HALO), :] = jnp.where(_last(i), 0.0, nxt)


def _shift8(xs_ref, pad_ref, tm):
    for j in range(8):
        xs_ref[j] = pad_ref[pl.ds(j, tm + 2 * HALO - 8), :]


def _tap(xs_ref, r, off):
    return xs_ref[off % 8, pl.ds(r + 8 * (off // 8), RC), :]


def _glu_a(z):
    return z[:, :DA] * _sigmoid(z[:, DA:2 * DA])


def _gate_c(z):
    return z[:, 2048:] * z[:, 1024:1536]


def _ev_mix_fwd(h, z, wa, ba, lng, lnb, wb, bb, evo, tm, carry=None):
    s = h.shape[0]

    def body(h_ref, z_ref, zp_ref, zn_ref, wa_ref, ba_ref, lng_ref, lnb_ref, wb_ref, bb_ref, wo_ref,
             h1_ref, a2_ref, c2_ref, cat_ref, apad, cpad, c2f, xs):
        i = pl.program_id(0)
        zm = z_ref[...].astype(F32)
        zp = zp_ref[...].astype(F32)
        zn = zn_ref[...].astype(F32)
        _fill_pad(apad, _glu_a(zp), _glu_a(zm), _glu_a(zn), i, tm)
        _fill_pad(cpad, _gate_c(zp), _gate_c(zm), _gate_c(zn), i, tm)
        _shift8(xs, apad, tm)
        for r in range(0, tm, RC):
            acc = jnp.zeros((RC, DA), F32) + ba_ref[...]
            for k in range(KA):
                acc = acc + _tap(xs, r, k + 1) * wa_ref[pl.ds(k, 1), :]
            a2_ref[pl.ds(r, RC), :] = acc
            acc = jnp.zeros((RC, DA), F32) + bb_ref[...]
            for k in range(KB):
                acc = acc + cpad[pl.ds(r + HALO - 1 + k, RC), :] * wb_ref[pl.ds(k, 1), :]
            c2f[pl.ds(r, RC), :] = acc
        y, _ = _ln(a2_ref[...])
        a3 = y * lng_ref[...] + lnb_ref[...]
        cat_ref[:, :DA] = (a3 * _sigmoid(a3)).astype(BF)
        c2 = c2f[...]
        c2_ref[...] = c2.astype(BF)
        cat_ref[:, DA:] = (zm[:, 1536:2048] * c2).astype(BF)
        h1_ref[...] = h_ref[...] + _dot(cat_ref[...], wo_ref[...].reshape(D, D))

    zp_spec, zn_spec = _halo_specs(tm, DZ, s)
    small = lambda rows: _const((rows, DA), (0, 0))
    return _call(
        body, name="ev_mix_fwd", grid=(s // tm,),
        in_specs=[_rows(tm, D), _rows(tm, DZ), zp_spec, zn_spec, small(32), small(1), small(1), small(1),
                  small(8), small(1), _whole(evo)],
        out_specs=[_rows(tm, D), _rows(tm, DA), _rows(tm, DA), _rows(tm, D)],
        out_shape=[_sds((s, D), F32), _sds((s, DA), F32), _sds((s, DA), BF), _sds((s, D), BF)],
        scratch_shapes=[pltpu.VMEM((tm + 2 * HALO, DA), F32), pltpu.VMEM((tm + 2 * HALO, DA), F32),
                        pltpu.VMEM((tm, DA), F32), pltpu.VMEM((8, tm + 2 * HALO - 8, DA), F32)],
        args=(h, z, z, z, wa, ba, lng, lnb, wb, bb, evo), carry=carry)


def _ev_mix_bwd(dh, cat, a2, c2, z, lng, lnb, evo, tm, carry=None):
    s = dh.shape[0]

    def body(dh_ref, cat_ref, a2_ref, c2_ref, zgb_ref, lng_ref, lnb_ref, wo_ref,
             da2_ref, dc2_ref, dgb_ref, dlng_ref, dlnb_ref, gwo_ref, wacc):
        i = pl.program_id(0)

        @pl.when(_first(i))
        def _():
            wacc[...] = jnp.zeros_like(wacc)
            dlng_ref[...] = jnp.zeros_like(dlng_ref)
            dlnb_ref[...] = jnp.zeros_like(dlnb_ref)

        dhb = dh_ref[...].astype(BF)
        dcat = _dot_nt(dhb, wo_ref[...].reshape(D, D))
        wacc[...] += _dot_tn(cat_ref[...], dhb)
        y, rstd = _ln(a2_ref[...])
        a3 = y * lng_ref[...] + lnb_ref[...]
        sg = _sigmoid(a3)
        da3 = dcat[:, :DA] * (sg * (1.0 + a3 * (1.0 - sg)))
        dlng_ref[...] += _colsum(da3 * y)
        dlnb_ref[...] += _colsum(da3)
        da2_ref[...] = _ln_bwd(da3 * lng_ref[...], y, rstd).astype(BF)
        db = dcat[:, DA:]
        dc2_ref[...] = (db * zgb_ref[...].astype(F32)).astype(BF)
        dgb_ref[...] = (db * c2_ref[...].astype(F32)).astype(BF)

        @pl.when(_last(i))
        def _():
            gwo_ref[...] = wacc[...].astype(BF).reshape(NSH, 256, D)

    small = _const((1, DA), (0, 0))
    return _call(
        body, name="ev_mix_bwd", grid=(s // tm,),
        in_specs=[_rows(tm, D), _rows(tm, D), _rows(tm, DA), _rows(tm, DA), _rows(tm, DA, 3), small, small,
                  _whole(evo)],
        out_specs=[_rows(tm, DA), _rows(tm, DA), _rows(tm, DA), small, small, _whole(evo)],
        out_shape=[_sds((s, DA), BF), _sds((s, DA), BF), _sds((s, DA), BF), _sds((1, DA), F32),
                   _sds((1, DA), F32), _sds(evo.shape, BF)],
        scratch_shapes=[pltpu.VMEM((D, D), F32)],
        args=(dh, cat, a2, c2, z, lng, lnb, evo), carry=carry)


def _ev_in_bwd(h, dh, z, da2, dc2, dgb, g, wa, wb, wev, tm, after):
    s = h.shape[0]

    def body(h_ref, dh_ref, z_ref, da_ref, dap_ref, dan_ref, dc_ref, dcp_ref, dcn_ref,
             dgb_ref, g_ref, wa_ref, wb_ref, w_ref, after_ref,
             dh0_ref, dwa_ref, dba_ref, dwb_ref, dbb_ref, dg_ref, gw_ref,
             dapad, dcpad, dz, wacc, dwa_acc, dwb_acc, xs):
        i = pl.program_id(0)

        @pl.when(_first(i))
        def _():
            wacc[...] = jnp.zeros_like(wacc)
            dwa_acc[...] = jnp.zeros_like(dwa_acc)
            dwb_acc[...] = jnp.zeros_like(dwb_acc)
            dba_ref[...] = jnp.zeros_like(dba_ref)
            dbb_ref[...] = jnp.zeros_like(dbb_ref)
            dg_ref[...] = jnp.zeros_like(dg_ref)

        da2 = da_ref[...].astype(F32)
        dc2 = dc_ref[...].astype(F32)
        _fill_pad(dapad, dap_ref[...].astype(F32), da2, dan_ref[...].astype(F32), i, tm)
        _fill_pad(dcpad, dcp_ref[...].astype(F32), dc2, dcn_ref[...].astype(F32), i, tm)
        dba_ref[...] += _colsum(da2)
        dbb_ref[...] += _colsum(dc2)

        _shift8(xs, dapad, tm)
        for r in range(0, tm, RC):
            zcol = lambda lo: z_ref[pl.ds(r, RC), lo:lo + DA].astype(F32)
            aval = zcol(0)
            sgr = _sigmoid(zcol(DA))
            a = aval * sgr
            acc = jnp.zeros((RC, DA), F32)
            for k in range(KA):
                t = _tap(xs, r, 2 * HALO - 1 - k)
                acc = acc + t * wa_ref[pl.ds(k, 1), :]
                dwa_acc[k] += (t * a).reshape(RC // 8, 8, DA).sum(axis=0)
            dz[pl.ds(r, RC), 0:DA] = (acc * sgr).astype(BF)
            dz[pl.ds(r, RC), DA:2 * DA] = (acc * aval * sgr * (1.0 - sgr)).astype(BF)
            hb = zcol(1024)
            gc = zcol(2048)
            c = gc * hb
            acc = jnp.zeros((RC, DA), F32)
            for k in range(KB):
                t = dcpad[pl.ds(r + HALO + 1 - k, RC), :]
                acc = acc + t * wb_ref[pl.ds(k, 1), :]
                dwb_acc[k] += (t * c).reshape(RC // 8, 8, DA).sum(axis=0)
            dz[pl.ds(r, RC), 1024:1536] = (acc * gc).astype(BF)
            dz[pl.ds(r, RC), 2048:2560] = (acc * hb).astype(BF)
        dz[:, 1536:2048] = dgb_ref[...]

        gg = g_ref[...]
        hhat, rr = _rms(h_ref[...])
        n = (hhat * gg).astype(BF)
        dzb = dz[...]
        wacc[...] += _dot_tn(n, dzb)
        dn = _dot_nt(dzb, w_ref[...])
        dg_ref[...] += _colsum(dn * hhat)
        dh0_ref[...] = dh_ref[...] + _rms_bwd(dn, hhat, rr, gg)

        @pl.when(_last(i))
        def _():
            for k in range(NSH):
                gw_ref[k] = wacc[:, 640 * k:640 * (k + 1)].astype(BF)
            dwa_ref[...] = jnp.sum(dwa_acc[...], axis=1)
            dwb_ref[...] = jnp.sum(dwb_acc[...], axis=1)

    hp_spec, hn_spec = _halo_specs(tm, DA, s)
    small = lambda rows: _const((rows, DA), (0, 0))
    return _call(
        body, name="ev_in_bwd", grid=(s // tm,),
        in_specs=[_rows(tm, D), _rows(tm, D), _rows(tm, DZ),
                  _rows(tm, DA), hp_spec, hn_spec, _rows(tm, DA), hp_spec, hn_spec, _rows(tm, DA),
                  _const((1, D), (0, 0)), small(32), small(8), _const((D, DZ), (0, 0)), ANY],
        out_specs=[_rows(tm, D), small(32), small(1), small(8), small(1), _const((1, D), (0, 0)),
                   _const((NSH, D, 640), (0, 0, 0))],
        out_shape=[_sds((s, D), F32), _sds((32, DA), F32), _sds((1, DA), F32), _sds((8, DA), F32),
                   _sds((1, DA), F32), _sds((1, D), F32), _sds((NSH, D, 640), BF)],
        scratch_shapes=[pltpu.VMEM((tm + 2 * HALO, DA), F32)] * 2 + [
            pltpu.VMEM((tm, DZ), BF), pltpu.VMEM((D, DZ), F32),
            pltpu.VMEM((32, 8, DA), F32), pltpu.VMEM((8, 8, DA), F32),
            pltpu.VMEM((8, tm + 2 * HALO - 8, DA), F32)],
        args=(h, dh, z, da2, da2, da2, dc2, dc2, dc2, dgb, g, wa, wb, wev, after))


def _xa_spec(item):
    return _const((NSH, 256, D), (0, item, 0))


def _kv_proj(mem, g, xa, layer):
    def body(mem_ref, g_ref, wk_ref, wv_ref, k_ref, v_ref):
        mhat, _ = _rms(mem_ref[...])
        mn = (mhat * g_ref[...]).astype(BF)
        k_ref[...] = _dot(mn, wk_ref[...].reshape(D, D)).astype(BF)
        v_ref[...] = _dot(mn, wv_ref[...].reshape(D, D)).astype(BF)

    full = _const((NMEM, D), (0, 0))
    return _call(
        body, name=f"kv_proj{layer}", grid=(1,),
        in_specs=[full, _const((1, D), (0, 0)), _xa_spec(XK), _xa_spec(XV)],
        out_specs=[full, full], out_shape=[_sds((NMEM, D), BF), _sds((NMEM, D), BF)],
        args=(mem, g, xa, xa))


def _softmax_rows(sc):
    m = jnp.max(sc, axis=-1, keepdims=True)
    e = jnp.exp(sc - m)
    return e / jnp.sum(e, axis=-1, keepdims=True)


def _xattn_fwd(h, g, k, v, xa, layer, tm, carry=None):
    s = h.shape[0]

    def body(h_ref, g_ref, k_ref, v_ref, wq_ref, wo_ref, ho_ref, q_ref, o_scr):
        hh = h_ref[...]
        hhat, _ = _rms(hh)
        n = (hhat * g_ref[...]).astype(BF)
        q = _dot(n, wq_ref[...].reshape(D, D)).astype(BF)
        q_ref[...] = q
        for hd in range(HEADS):
            cs = slice(HD * hd, HD * (hd + 1))
            p = _softmax_rows(_dot_nt(q[:, cs], k_ref[:, cs]) * ATT_SCALE)
            o_scr[:, cs] = _dot(p.astype(BF), v_ref[:, cs]).astype(BF)
        ho_ref[...] = hh + _dot(o_scr[...], wo_ref[...].reshape(D, D))

    kvs = _const((NMEM, D), (0, 0))
    return _call(
        body, name=f"xattn_fwd{layer}", grid=(s // tm,),
        in_specs=[_rows(tm, D), _const((1, D), (0, 0)), kvs, kvs, _xa_spec(XQ), _xa_spec(XO)],
        out_specs=[_rows(tm, D), _rows(tm, D)],
        out_shape=[_sds((s, D), F32), _sds((s, D), BF)],
        scratch_shapes=[pltpu.VMEM((tm, D), BF)],
        args=(h, g, k, v, xa, xa), carry=carry)


def _xattn_bwd(h, dh, q, g, k, v, xa, layer, tm, carry=None):
    s = h.shape[0]

    def body(h_ref, dh_ref, q_ref, g_ref, k_ref, v_ref, wq_ref, wo_ref,
             dhi_ref, dk_ref, dv_ref, dg_ref, gqo_ref, o_scr, dq_scr, qacc, oacc):
        i = pl.program_id(0)

        @pl.when(_first(i))
        def _():
            qacc[...] = jnp.zeros_like(qacc)
            oacc[...] = jnp.zeros_like(oacc)
            dk_ref[...] = jnp.zeros_like(dk_ref)
            dv_ref[...] = jnp.zeros_like(dv_ref)
            dg_ref[...] = jnp.zeros_like(dg_ref)

        gg = g_ref[...]
        dho = dh_ref[...]
        dhb = dho.astype(BF)
        q = q_ref[...]
        do = _dot_nt(dhb, wo_ref[...].reshape(D, D)).astype(BF)
        for hd in range(HEADS):
            cs = slice(HD * hd, HD * (hd + 1))
            kh = k_ref[:, cs]
            vh = v_ref[:, cs]
            p = _softmax_rows(_dot_nt(q[:, cs], kh) * ATT_SCALE)
            pb = p.astype(BF)
            o_scr[:, cs] = _dot(pb, vh).astype(BF)
            doh = do[:, cs]
            dp = _dot_nt(doh, vh)
            dv_ref[:, cs] += _dot_tn(pb, doh)
            ds = (p * (dp - jnp.sum(dp * p, axis=-1, keepdims=True)) * ATT_SCALE).astype(BF)
            dq_scr[:, cs] = _dot(ds, kh).astype(BF)
            dk_ref[:, cs] += _dot_tn(ds, q[:, cs])
        oacc[...] += _dot_tn(o_scr[...], dhb)
        hhat, rr = _rms(h_ref[...])
        n = (hhat * gg).astype(BF)
        dq = dq_scr[...]
        qacc[...] += _dot_tn(n, dq)
        dn = _dot_nt(dq, wq_ref[...].reshape(D, D))
        dg_ref[...] += _colsum(dn * hhat)
        dhi_ref[...] = dho + _rms_bwd(dn, hhat, rr, gg)

        @pl.when(_last(i))
        def _():
            gqo_ref[:, 0:256, :] = qacc[...].astype(BF).reshape(NSH, 256, D)
            gqo_ref[:, 256:512, :] = oacc[...].astype(BF).reshape(NSH, 256, D)

    kvs = _const((NMEM, D), (0, 0))
    return _call(
        body, name=f"xattn_bwd{layer}", grid=(s // tm,),
        in_specs=[_rows(tm, D), _rows(tm, D), _rows(tm, D), _const((1, D), (0, 0)), kvs, kvs,
                  _xa_spec(XQ), _xa_spec(XO)],
        out_specs=[_rows(tm, D), kvs, kvs, _const((1, D), (0, 0)), _const((NSH, 512, D), (0, 0, 0))],
        out_shape=[_sds((s, D), F32), _sds((NMEM, D), F32), _sds((NMEM, D), F32), _sds((1, D), F32),
                   _sds(xa.shape, BF)],
        scratch_shapes=[pltpu.VMEM((tm, D), BF), pltpu.VMEM((tm, D), BF), pltpu.VMEM((D, D), F32),
                        pltpu.VMEM((D, D), F32)],
        args=(h, dh, q, g, k, v, xa, xa), carry=carry)


def _kv_proj_bwd(mem, g, dk, dv, xa, gxa, layer):
    def body(mem_ref, g_ref, dk_ref, dv_ref, wk_ref, wv_ref, gxa_in, dg_ref, gkv_ref):
        mhat, _ = _rms(mem_ref[...])
        mn = (mhat * g_ref[...]).astype(BF)
        dkb = dk_ref[...].astype(BF)
        dvb = dv_ref[...].astype(BF)
        gkv_ref[:, 0:256, :] = _dot_tn(mn, dkb).astype(BF).reshape(NSH, 256, D)
        gkv_ref[:, 256:512, :] = _dot_tn(mn, dvb).astype(BF).reshape(NSH, 256, D)
        dmn = _dot_nt(dkb, wk_ref[...].reshape(D, D)) + _dot_nt(dvb, wv_ref[...].reshape(D, D))
        dg_ref[...] = _colsum(dmn * mhat)

    full = _const((NMEM, D), (0, 0))
    return _call(
        body, name=f"kv_proj_bwd{layer}", grid=(1,),
        in_specs=[full, _const((1, D), (0, 0)), full, full, _xa_spec(XK), _xa_spec(XV), ANY],
        out_specs=[_const((1, D), (0, 0)), _const((NSH, 512, D), (0, 1, 0))],
        out_shape=[_sds((1, D), F32), _sds(gxa.shape, BF)],
        aliases={6: 1},
        args=(mem, g, dk, dv, xa, xa, gxa))


def _ffn_block(h_ref, g_ref, wg_ref, wu_ref, wd_ref, gt_ref, up_ref):
    hh = h_ref[...]
    hhat, _ = _rms(hh)
    n = (hhat * g_ref[...]).astype(BF)
    out = hh
    for c0, c1 in FF_CHUNKS:
        gt = _dot_nt(n, wg_ref[c0:c1, :])
        up = _dot_nt(n, wu_ref[c0:c1, :])
        gt_ref[:, c0:c1] = gt.astype(BF)
        up_ref[:, c0:c1] = up.astype(BF)
        act = (gt * _sigmoid(gt) * up).astype(BF)
        out = out + _dot(act, wd_ref[c0:c1, :])
    return out


def _ffn_fwd(h, g, wg, wu, dn, layer, tm, carry=None):
    s = h.shape[0]
    wg, wu, dn = (a.reshape(DFF, D) for a in (wg, wu, dn))

    def body(h_ref, g_ref, wg_ref, wu_ref, wd_ref, ho_ref, gt_ref, up_ref):
        ho_ref[...] = _ffn_block(h_ref, g_ref, wg_ref, wu_ref, wd_ref, gt_ref, up_ref)

    return _call(
        body, name=f"ffn_fwd{layer}", grid=(s // tm,),
        in_specs=[_rows(tm, D), _const((1, D), (0, 0)), _whole(wg), _whole(wu), _whole(dn)],
        out_specs=[_rows(tm, D), _rows(tm, DFF), _rows(tm, DFF)],
        out_shape=[_sds((s, D), F32), _sds((s, DFF), BF), _sds((s, DFF), BF)],
        args=(h, g, wg, wu, dn), carry=carry)


def _ffn_fwd_loss(h, g, wg, wu, dn, tgt, g_final, layer, tm):
    s = h.shape[0]
    wg, wu, dn = (a.reshape(DFF, D) for a in (wg, wu, dn))

    def body(h_ref, g_ref, wg_ref, wu_ref, wd_ref, t_ref, gf_ref, dh_ref, loss_ref, dg_ref, gt_ref, up_ref):
        i = pl.program_id(0)

        @pl.when(_first(i))
        def _():
            loss_ref[...] = jnp.zeros_like(loss_ref)
            dg_ref[...] = jnp.zeros_like(dg_ref)

        out = _ffn_block(h_ref, g_ref, wg_ref, wu_ref, wd_ref, gt_ref, up_ref)
        gg = gf_ref[...]
        hhat, rr = _rms(out)
        diff = hhat * gg - t_ref[...]
        loss_ref[...] += jnp.sum(diff * diff) * (0.5 / D)
        dout = diff * (1.0 / D)
        dg_ref[...] += _colsum(dout * hhat)
        dh_ref[...] = _rms_bwd(dout, hhat, rr, gg)

    vec = _const((1, D), (0, 0))
    return _call(
        body, name=f"ffn_fwd_loss{layer}", grid=(s // tm,),
        in_specs=[_rows(tm, D), vec, _whole(wg), _whole(wu), _whole(dn), _rows(tm, D), vec],
        out_specs=[_rows(tm, D), _const((1, 128), (0, 0)), vec, _rows(tm, DFF), _rows(tm, DFF)],
        out_shape=[_sds((s, D), F32), _sds((1, 128), F32), _sds((1, D), F32), _sds((s, DFF), BF),
                   _sds((s, DFF), BF)],
        args=(h, g, wg, wu, dn, tgt, g_final))


def _ffn_bwd_dx(h, dh, gt, up, g, wg, wu, dn, layer, tm, carry=None):
    s = h.shape[0]
    wg, wu, dn = (a.reshape(DFF, D) for a in (wg, wu, dn))

    def body(h_ref, dh_ref, gt_ref, up_ref, g_ref, wg_ref, wu_ref, wd_ref,
             dhi_ref, dg_ref, dgt_ref, dup_ref, act_ref, n_ref):
        i = pl.program_id(0)

        @pl.when(_first(i))
        def _():
            dg_ref[...] = jnp.zeros_like(dg_ref)

        gg = g_ref[...]
        dho = dh_ref[...]
        dhb = dho.astype(BF)
        dn_ = jnp.zeros((tm, D), F32)
        for c0, c1 in FF_CHUNKS:
            gtv = gt_ref[:, c0:c1].astype(F32)
            upv = up_ref[:, c0:c1].astype(F32)
            sg = _sigmoid(gtv)
            silu = gtv * sg
            dact = _dot_nt(dhb, wd_ref[c0:c1, :])
            dgt = (dact * upv * (sg * (1.0 + gtv * (1.0 - sg)))).astype(BF)
            dup = (dact * silu).astype(BF)
            dgt_ref[:, c0:c1] = dgt
            dup_ref[:, c0:c1] = dup
            act_ref[:, c0:c1] = (silu * upv).astype(BF)
            dn_ = dn_ + _dot(dgt, wg_ref[c0:c1, :]) + _dot(dup, wu_ref[c0:c1, :])
        hhat, rr = _rms(h_ref[...])
        n_ref[...] = (hhat * gg).astype(BF)
        dg_ref[...] += _colsum(dn_ * hhat)
        dhi_ref[...] = dho + _rms_bwd(dn_, hhat, rr, gg)

    wide = _rows(tm, DFF)
    return _call(
        body, name=f"ffn_bwd_dx{layer}", grid=(s // tm,),
        in_specs=[_rows(tm, D), _rows(tm, D), wide, wide, _const((1, D), (0, 0)),
                  _whole(wg), _whole(wu), _whole(dn)],
        out_specs=[_rows(tm, D), _const((1, D), (0, 0)), wide, wide, wide, _rows(tm, D)],
        out_shape=[_sds((s, D), F32), _sds((1, D), F32), _sds((s, DFF), BF), _sds((s, DFF), BF),
                   _sds((s, DFF), BF), _sds((s, D), BF)],
        args=(h, dh, gt, up, g, wg, wu, dn), carry=carry)


def _ffn_wgrad(a, b, name, tm, slot, into=None, carry=None):
    s = a.shape[0]
    tm = min(tm, s)

    def body(a_ref, b_ref, *rest):
        o_ref, acc = rest[-2:]
        i = pl.program_id(0)

        @pl.when(_first(i))
        def _():
            acc[...] = jnp.zeros_like(acc)

        acc[...] += _dot_tn(a_ref[...], b_ref[...].astype(BF))

        @pl.when(_last(i))
        def _():
            o_ref[...] = acc[...].astype(BF).reshape(NSH, FSH, D)

    chained = into is not None
    return _call(
        body, name=name, grid=(s // tm,),
        in_specs=[_rows(tm, DFF), _rows(tm, D)] + [ANY] * chained,
        out_specs=[_const((NSH, FSH, D), (0, slot, 0))],
        out_shape=[_sds((NSH, 3 * FSH, D), BF)],
        scratch_shapes=[pltpu.VMEM((DFF, D), F32)],
        aliases={2: 0} if chained else None,
        args=(a, b) + ((into,) if chained else ()), carry=carry)


def _sgu_fwd(h, g, lng, lnb, wst, bsb, wod, odo, tm, carry=None):
    s = h.shape[0]
    nc = tm // CHUNK

    def body(h_ref, g_ref, lng_ref, lnb_ref, wst_ref, bsb_ref, wi_ref, wo_ref, ho_ref, zp_ref, y_scr):
        hh = h_ref[...]
        hhat, _ = _rms(hh)
        n = (hhat * g_ref[...]).astype(BF)
        zpre = _dot(n, wi_ref[...])
        zp_ref[...] = zpre.astype(BF)
        z = _gelu(zpre)
        u = z[:, :D]
        y, _ = _ln(z[:, D:])
        vn = y * lng_ref[...] + lnb_ref[...]
        for gi in range(GROUPS):
            cs = slice(CHUNK * gi, CHUNK * (gi + 1))
            vt = jnp.swapaxes(vn[:, cs].reshape(nc, CHUNK, CHUNK), 1, 2).reshape(tm, CHUNK)
            svt = _dot(vt.astype(BF), wst_ref[gi]).reshape(nc, CHUNK, CHUNK)
            sv = jnp.swapaxes(svt, 1, 2) + bsb_ref[gi][None]
            y_scr[:, cs] = (u[:, cs] * sv.reshape(tm, CHUNK)).astype(BF)
        ho_ref[...] = hh + _dot(y_scr[...], wo_ref[...].reshape(D, D))

    vec = _const((1, D), (0, 0))
    sq = _const((GROUPS, CHUNK, CHUNK), (0, 0, 0))
    return _call(
        body, name="sgu_fwd", grid=(s // tm,),
        in_specs=[_rows(tm, D), vec, vec, vec, sq, sq, _const((D, 2 * D), (0, 0)), _whole(odo)],
        out_specs=[_rows(tm, D), _rows(tm, 2 * D)],
        out_shape=[_sds((s, D), F32), _sds((s, 2 * D), BF)],
        scratch_shapes=[pltpu.VMEM((tm, D), BF)],
        args=(h, g, lng, lnb, wst, bsb, wod, odo), carry=carry)


def _sgu_bwd(h, dh, zpre, g, lng, lnb, ws, wst, bsb, wod, odo, tm, carry=None):
    s = h.shape[0]

    def body(h_ref, dh_ref, zp_ref, g_ref, lng_ref, lnb_ref, ws_ref, wst_ref, bsb_ref, wi_ref, wo_ref,
             dhi_ref, dws_ref, dbs_ref, dlng_ref, dlnb_ref, dg_ref, gwi_ref, gwo_ref,
             y_scr, dz_scr, dvn_scr, iacc, oacc, bacc):
        i = pl.program_id(0)

        @pl.when(_first(i))
        def _():
            for ref in (iacc, oacc, bacc, dws_ref, dlng_ref, dlnb_ref, dg_ref):
                ref[...] = jnp.zeros_like(ref)

        gg = g_ref[...]
        dho = dh_ref[...]
        dhb = dho.astype(BF)
        zpre_v = zp_ref[...].astype(F32)
        z = _gelu(zpre_v)
        u = z[:, :D]
        yl, rstd = _ln(z[:, D:])
        vn = (yl * lng_ref[...] + lnb_ref[...]).astype(BF)
        dy = _dot_nt(dhb, wo_ref[...].reshape(D, D))
        for c in range(tm // CHUNK):
            rs = slice(CHUNK * c, CHUNK * (c + 1))
            for gi in range(GROUPS):
                cs = slice(CHUNK * gi, CHUNK * (gi + 1))
                vb = vn[rs, cs]
                sv = _dot(ws_ref[gi], vb) + bsb_ref[gi]
                ub = u[rs, cs]
                dyb = dy[rs, cs]
                y_scr[rs, cs] = (ub * sv).astype(BF)
                dz_scr[rs, cs] = dyb * sv
                dsv = dyb * ub
                bacc[gi] += dsv
                dsvb = dsv.astype(BF)
                dws_ref[gi] += _dot_nt(dsvb, vb)
                dvn_scr[rs, cs] = _dot(wst_ref[gi], dsvb)
        oacc[...] += _dot_tn(y_scr[...], dhb)
        dvn = dvn_scr[...]
        dlng_ref[...] += _colsum(dvn * yl)
        dlnb_ref[...] += _colsum(dvn)
        dz_scr[:, D:] = _ln_bwd(dvn * lng_ref[...], yl, rstd)
        dzb = (dz_scr[...] * _gelu_grad(zpre_v)).astype(BF)
        hhat, rr = _rms(h_ref[...])
        n = (hhat * gg).astype(BF)
        iacc[...] += _dot_tn(n, dzb)
        dn = _dot_nt(dzb, wi_ref[...])
        dg_ref[...] += _colsum(dn * hhat)
        dhi_ref[...] = dho + _rms_bwd(dn, hhat, rr, gg)

        @pl.when(_last(i))
        def _():
            for k in range(NSH):
                gwi_ref[k] = iacc[:, 512 * k:512 * (k + 1)].astype(BF)
            gwo_ref[...] = oacc[...].astype(BF).reshape(NSH, 256, D)
            dbs_ref[...] = jnp.sum(bacc[...], axis=-1)

    vec = _const((1, D), (0, 0))
    sq = _const((GROUPS, CHUNK, CHUNK), (0, 0, 0))
    return _call(
        body, name="sgu_bwd", grid=(s // tm,),
        in_specs=[_rows(tm, D), _rows(tm, D), _rows(tm, 2 * D), vec, vec, vec, sq, sq, sq,
                  _const((D, 2 * D), (0, 0)), _whole(odo)],
        out_specs=[_rows(tm, D), sq, _const((GROUPS, CHUNK), (0, 0)), vec, vec, vec,
                   _const((NSH, D, 512), (0, 0, 0)), _whole(odo)],
        out_shape=[_sds((s, D), F32), _sds((GROUPS, CHUNK, CHUNK), F32), _sds((GROUPS, CHUNK), F32),
                   _sds((1, D), F32), _sds((1, D), F32), _sds((1, D), F32), _sds((NSH, D, 512), BF),
                   _sds(odo.shape, BF)],
        scratch_shapes=[pltpu.VMEM((tm, D), BF), pltpu.VMEM((tm, 2 * D), F32),
                        pltpu.VMEM((tm, D), F32), pltpu.VMEM((D, 2 * D), F32), pltpu.VMEM((D, D), F32),
                        pltpu.VMEM((GROUPS, CHUNK, CHUNK), F32)],
        args=(h, dh, zpre, g, lng, lnb, ws, wst, bsb, wod, odo), carry=carry)


def _own_plane(shape):
    return pl.BlockSpec((1,) + shape, lambda i, chip_ref: (chip_ref[0], 0, 0), pipeline_mode=pl.Buffered(1))


def _pack_rows(chip_idx, w_down, w_gate_t, w_up_t, xq, xo, xk, xv, ev_out, od_out, first):
    nf = len(first)

    def body(chip_ref, wd_ref, wg_ref, wu_ref, xq_ref, xo_ref, xk_ref, xv_ref, ev_ref, od_ref, *rest):
        (evo_ref, odo_ref, dn0_ref, dn1_ref, wg0_ref, wg1_ref, wu0_ref, wu1_ref, xa0_ref,
         xa1_ref) = rest[nf:nf + 10]
        gathered = rest[nf + 10:2 * nf + 10]
        sems = rest[2 * nf + 10:]
        _gather_start(gathered, sems)
        evo_ref[0] = ev_ref[0].astype(BF)
        odo_ref[0] = od_ref[0].astype(BF)
        for src, outs in ((wd_ref, (dn0_ref, dn1_ref)), (wg_ref, (wg0_ref, wg1_ref)), (wu_ref, (wu0_ref, wu1_ref))):
            for layer, o_ref in enumerate(outs):
                o_ref[0] = src[layer].astype(BF)
        for layer, xa_ref in enumerate((xa0_ref, xa1_ref)):
            for item, src in ((XQ, xq_ref), (XO, xo_ref), (XK, xk_ref), (XV, xv_ref)):
                xa_ref[0, 256 * item:256 * (item + 1), :] = src[layer].astype(BF)
        _gather_forward(gathered, sems)
        _gather_drain(gathered, sems)

    ins = (w_down, w_gate_t, w_up_t, xq, xo, xk, xv, ev_out, od_out)
    shapes = [(256, D), (256, D)] + [(FSH, D)] * 6 + [(4 * 256, D), (4 * 256, D)]
    grid_spec = pltpu.PrefetchScalarGridSpec(
        num_scalar_prefetch=1, grid=(1,), in_specs=[_whole(a) for a in ins] + [ANY] * nf,
        out_specs=[_own_plane(sh) for sh in shapes] + [ANY] * nf,
        scratch_shapes=_gather_sems(nf))
    outs = pl.pallas_call(
        body, name="pack_rows", grid_spec=grid_spec,
        out_shape=[_sds((NSH,) + sh, BF) for sh in shapes] + [_sds(a.shape, a.dtype) for a in first],
        input_output_aliases={1 + len(ins) + t: len(shapes) + t for t in range(nf)},
        compiler_params=_params(),
    )(chip_idx, *ins, *first)
    return outs[:len(shapes)], outs[len(shapes):]


def _pack_cols(chip_idx, od_in, ev_in, conv_a, conv_b, ln_g, ln_b):
    def body(chip_ref, od_ref, ev_ref, ca_ref, cb_ref, lg_ref, lb_ref, cev_ref, cod_ref, sp_ref):
        cev_ref[0] = ev_ref[0].astype(BF)
        cod_ref[0] = od_ref[0].astype(BF)
        sp_ref[...] = jnp.zeros_like(sp_ref)
        sp_ref[0, 0:KA, 0:128] = ca_ref[0]
        sp_ref[0, 32:32 + KB, 0:128] = cb_ref[0]
        sp_ref[0, 40:41, :] = lg_ref[...]
        sp_ref[0, 41:42, :] = lb_ref[...]

    ins = (od_in, ev_in, conv_a, conv_b, ln_g, ln_b)
    shapes = [(D, 640), (D, 512)]
    grid_spec = pltpu.PrefetchScalarGridSpec(
        num_scalar_prefetch=1, grid=(1,), in_specs=[_whole(a) for a in ins],
        out_specs=[_own_plane(sh) for sh in shapes] + [_own_plane((SP_ROWS, 256))])
    return pl.pallas_call(
        body, name="pack_cols", grid_spec=grid_spec,
        out_shape=[_sds((NSH,) + sh, BF) for sh in shapes] + [_sds((NSH, SP_ROWS, 256), F32)],
        compiler_params=_params(),
    )(chip_idx, *ins)


def _assemble(piece, parts, name):
    tr = 256
    cols = piece.shape[2]

    def body(p_ref, *outs):
        for k in range(NSH):
            for (off, width), o_ref in zip(parts, outs):
                o_ref[:, width * k:width * (k + 1)] = p_ref[k, :, off:off + width]

    return _call(
        body, name=name, grid=(D // tr,),
        in_specs=[pl.BlockSpec((NSH, tr, cols), lambda i: (0, i, 0))],
        out_specs=[pl.BlockSpec((tr, NSH * width), lambda i: (i, 0)) for _, width in parts],
        out_shape=[_sds((D, NSH * width), BF) for _, width in parts],
        args=(piece,))


def _blocks_of(rows, cols, itemsize):
    rb = rows
    while rb * cols * itemsize > (2 << 20) and rb % 2 == 0 and (rb // 2) % 16 == 0:
        rb //= 2
    return rb


def _add_half(full, recv, c_idx, name):
    p, r, cols = full.shape
    rb = _blocks_of(r // 2, cols, 4)
    nb = (r // 2) // rb

    def body(c_ref, a_ref, b_ref, o_ref):
        o_ref[...] = (a_ref[...].astype(F32) + b_ref[...].astype(F32)).astype(o_ref.dtype)

    grid_spec = pltpu.PrefetchScalarGridSpec(
        num_scalar_prefetch=1, grid=(p, nb),
        in_specs=[pl.BlockSpec((1, rb, cols), lambda j, i, c_ref: (j, c_ref[0] * nb + i, 0)),
                  pl.BlockSpec((1, rb, cols), lambda j, i, c_ref: (j, i, 0))],
        out_specs=pl.BlockSpec((1, rb, cols), lambda j, i, c_ref: (j, i, 0)))
    return pl.pallas_call(
        body, name=name, grid_spec=grid_spec, out_shape=_sds(recv.shape, full.dtype),
        compiler_params=_params(2),
    )(c_idx, full, recv)


def _sum_chips(own, got, plane_c, name):
    _, r, cols = own.shape
    rb = _blocks_of(r, cols, 4)
    nb = r // rb

    def body(s_ref, a_ref, b_ref, o_ref):
        f = lambda v: v.astype(F32)
        o_ref[...] = (f(a_ref[0]) + f(b_ref[1])) + (f(b_ref[0]) + f(b_ref[2]))

    grid_spec = pltpu.PrefetchScalarGridSpec(
        num_scalar_prefetch=1, grid=(nb,),
        in_specs=[pl.BlockSpec((1, rb, cols), lambda i, s_ref: (s_ref[0], i, 0)),
                  pl.BlockSpec((3, rb, cols), lambda i, s_ref: (0, i, 0))],
        out_specs=pl.BlockSpec((rb, cols), lambda i, s_ref: (s_ref[1] * nb + i, 0)))
    return pl.pallas_call(
        body, name=name, grid_spec=grid_spec, out_shape=_sds((2 * r, cols), F32),
        compiler_params=_params(1),
    )(plane_c, own, got)


def _adamw_math(w, g, m, v):
    m = ADAM_B1 * m + (1.0 - ADAM_B1) * g
    v = ADAM_B2 * v + (1.0 - ADAM_B2) * (g * g)
    m_hat = m / (1.0 - ADAM_B1 ** ADAM_STEP)
    v_hat = v / (1.0 - ADAM_B2 ** ADAM_STEP)
    delta = -ADAM_LR * (m_hat / (jnp.sqrt(v_hat) + ADAM_EPS) + ADAM_WD * w)
    return delta, m, v


def _adamw_big(w, m, v, gsrcs, gblock, gidx, name, rb):
    nl, r, cols = w.shape
    assert len(gsrcs) == nl

    def body(w_ref, m_ref, v_ref, *rest):
        g_refs, (go_ref, d_ref, mo_ref, vo_ref) = rest[:nl], rest[nl:]
        g = g_refs[0][:, 0:cols]
        if nl == 2:
            g = jnp.where(pl.program_id(0) == 0, g, g_refs[1][:, 0:cols])
        delta, mn, vn = _adamw_math(w_ref[0], g, m_ref[0], v_ref[0])
        go_ref[0] = g
        d_ref[0] = delta
        mo_ref[0] = mn
        vo_ref[0] = vn

    wspec = pl.BlockSpec((1, rb, cols), lambda l, i: (l, i, 0))
    gspec = pl.BlockSpec(gblock, lambda l, i: gidx(i))
    return pl.pallas_call(
        body, name=name, grid=(nl, r // rb),
        in_specs=[wspec, wspec, wspec] + [gspec] * nl,
        out_specs=[wspec] * 4, out_shape=[_sds(w.shape, F32)] * 4,
        compiler_params=_params(2),
    )(w, m, v, *gsrcs)


def _adamw_rows(ws, ms, vs, gsrcs, name, rb=64):
    n = len(ws)
    nl, r, cols = ws[0].shape
    nb = r // rb

    def body(*refs):
        w_refs, m_refs, v_refs = refs[:n], refs[n:2 * n], refs[2 * n:3 * n]
        g_refs = refs[3 * n:3 * n + nl * n]
        outs = refs[3 * n + nl * n:]
        for j in range(n):
            g = g_refs[j][...]
            for l in range(1, nl):
                g = jnp.where(pl.program_id(0) == l, g_refs[l * n + j][...], g)
            delta, mn, vn = _adamw_math(w_refs[j][0], g, m_refs[j][0], v_refs[j][0])
            for o_ref, val in zip(outs[4 * j:4 * j + 4], (g, delta, mn, vn)):
                o_ref[0] = val

    wspec = pl.BlockSpec((1, rb, cols), lambda l, i: (l, i, 0))
    gspecs = [pl.BlockSpec((rb, cols), lambda l, i, _j=j: (_j * nb + i, 0)) for _ in range(nl) for j in range(n)]
    outs = pl.pallas_call(
        body, name=name, grid=(nl, nb),
        in_specs=[wspec] * (3 * n) + gspecs,
        out_specs=[wspec] * (4 * n), out_shape=[_sds(ws[0].shape, F32)] * (4 * n),
        compiler_params=_params(2),
    )(*ws, *ms, *vs, *[gsrcs[l] for l in range(nl) for _ in range(n)])
    return [tuple(outs[4 * j:4 * j + 4]) for j in range(n)]


_SMALL = (
    ("g_mix", (2, D), "a", (0, 2), 0, False),
    ("g_xattn", (2, D), "a", (2, 4), 0, False),
    ("g_mem", (2, D), "a", (4, 6), 0, False),
    ("g_ffn", (2, D), "a", (6, 8), 0, False),
    ("g_final", (1, D), "a", (8, 9), 0, False),
    ("ev_a_conv_w", (KA, 128), "a", (16, 16 + KA), 0, True),
    ("ev_a_conv_b", (1, DA), "a", (11, 12), 0, False),
    ("ev_a_ln_g", (1, DA), "a", (11, 12), DA, False),
    ("ev_a_ln_b", (1, DA), "a", (12, 13), 0, False),
    ("ev_b_conv_w", (KB, 128), "a", (48, 48 + KB), 0, True),
    ("ev_b_conv_b", (1, DA), "a", (12, 13), DA, False),
    ("od_c_ln_g", (1, 256), "a", (9, 10), 0, True),
    ("od_c_ln_b", (1, 256), "a", (10, 11), 0, True),
    ("od_w_s", (GROUPS * CHUNK, CHUNK), "b", (0, GROUPS * CHUNK), 0, False),
    ("od_b_s", (GROUPS, CHUNK), "a", (56, 64), 0, False),
)


def _adamw_small(ga, gb, wmv):
    ns = len(_SMALL)

    def body(*refs):
        ga_ref, gb_ref = refs[0], refs[1]
        ins = refs[2:2 + 3 * ns]
        outs = refs[2 + 3 * ns:]
        x, y, _ = _pos()
        chip = 2 * x + y
        for j, (_, shape, pack, (r0, r1), col, sharded) in enumerate(_SMALL):
            src = ga_ref if pack == "a" else gb_ref
            width = shape[1]
            if sharded:
                g = jnp.zeros(shape, F32)
                for k in range(NSH):
                    blk = src[r0:r1, col + width * k:col + width * (k + 1)]
                    g = g + jnp.where(chip == k, blk, 0.0)
            else:
                g = src[r0:r1, col:col + width]
            w_ref, m_ref, v_ref = ins[3 * j:3 * j + 3]
            delta, mn, vn = _adamw_math(w_ref[...], g, m_ref[...], v_ref[...])
            go_ref, d_ref, mo_ref, vo_ref = outs[4 * j:4 * j + 4]
            go_ref[...] = g
            d_ref[...] = delta
            mo_ref[...] = mn
            vo_ref[...] = vn

    vm = pl.BlockSpec(memory_space=pltpu.VMEM)
    flat = [a for trio in wmv for a in trio]
    out_shape = []
    for (_, shape, *_rest) in _SMALL:
        out_shape += [_sds(shape, F32)] * 4
    outs = pl.pallas_call(
        body, name="adamw_small",
        in_specs=[vm] * (2 + 3 * ns), out_specs=[vm] * (4 * ns), out_shape=out_shape,
    )(ga, gb, *flat)
    return [tuple(outs[4 * j:4 * j + 4]) for j in range(ns)]


def _small_params(gs, W):
    unshard = lambda a: jnp.transpose(a, (1, 0, 2)).reshape(a.shape[1], NSH * a.shape[2])
    od_w_s, od_b_s = W["od_w_s"], W["od_b_s"]
    return dict(
        g_mix=W["g_mix"], g_xattn=W["g_xattn"], g_mem=W["g_mem"], g_ffn=W["g_ffn"],
        g_final=W["g_final"].reshape(1, D),
        wa=unshard(gs[:, 0:32, 0:128]), ba=W["ev_a_conv_b"], lng_a=W["ev_a_ln_g"], lnb_a=W["ev_a_ln_b"],
        wb=unshard(gs[:, 32:40, 0:128]), bb=W["ev_b_conv_b"],
        lng_c=unshard(gs[:, 40:41, :]), lnb_c=unshard(gs[:, 41:42, :]),
        ws=od_w_s[0].astype(BF), wst=jnp.swapaxes(od_w_s[0], 1, 2).astype(BF),
        bsb=jnp.broadcast_to(od_b_s[0][:, :, None], (GROUPS, CHUNK, CHUNK)),
    )


LOSS_ROW = 13


def _small_grad_packs(g, loss):
    z = lambda r, c: jnp.zeros((r, c), F32)
    wide = lambda a: jnp.concatenate([a, z(a.shape[0], D - a.shape[1])], axis=1)
    rows = [g["g_mix"], g["g_xattn"], g["g_mem"], g["g_ffn"], g["g_final"], g["od_c_ln_g"], g["od_c_ln_b"],
            jnp.concatenate([g["ev_a_conv_b"], g["ev_a_ln_g"]], axis=1),
            jnp.concatenate([g["ev_a_ln_b"], g["ev_b_conv_b"]], axis=1),
            wide(loss), z(2, D), wide(g["ev_a_conv_w"]), wide(g["ev_b_conv_w"]), wide(g["od_b_s"])]
    a = jnp.concatenate(rows, axis=0)
    assert a.shape == (SA_ROWS, D), a.shape
    return a, g["od_w_s"].reshape(GROUPS * CHUNK, CHUNK)


def _step(x, mem, tgt, W, P, c_idx, tm_f, tm_b, tm_w):
    row = lambda a, i: a[i:i + 1]

    sm = _small_params(P["gs"], W)
    (wev,) = _assemble(P["cev"], [(0, 640)], "assemble_ev")
    z, evo, wg0 = _ev_in_fwd(x, row(sm["g_mix"], 0), wev, 2 * tm_f, carry=("gather", [P["evo"], P["wg0"]]))
    h1, a2, c2, cat, xa0, wu0 = _ev_mix_fwd(
        x, z, sm["wa"], sm["ba"], sm["lng_a"], sm["lnb_a"], sm["wb"], sm["bb"], evo, 2 * tm_f,
        carry=("gather", [P["xa0"], P["wu0"]]))
    k0, v0 = _kv_proj(mem, row(sm["g_mem"], 0), xa0, 0)
    h2, q0, dn0 = _xattn_fwd(h1, row(sm["g_xattn"], 0), k0, v0, xa0, 0, 2 * tm_f, carry=("gather", [P["dn0"]]))
    h3, gt0, up0, odo, xa1, wg1, cod = _ffn_fwd(h2, row(sm["g_ffn"], 0), wg0, wu0, dn0, 0, tm_f,
                                                carry=("gather", [P["odo"], P["xa1"], P["wg1"], P["cod"]]))
    (wod,) = _assemble(cod, [(0, 512)], "assemble_od")
    h4, zpre, wu1, dn1 = _sgu_fwd(h3, row(sm["g_mix"], 1), sm["lng_c"], sm["lnb_c"], sm["wst"], sm["bsb"], wod,
                                  odo, 2 * tm_f, carry=("gather", [P["wu1"], P["dn1"]]))
    k1, v1 = _kv_proj(mem, row(sm["g_mem"], 1), xa1, 1)
    h5, q1 = _xattn_fwd(h4, row(sm["g_xattn"], 1), k1, v1, xa1, 1, 2 * tm_f)
    dh6, loss, dg_final, gt1, up1 = _ffn_fwd_loss(h5, row(sm["g_ffn"], 1), wg1, wu1, dn1, tgt, sm["g_final"], 1,
                                                  tm_f)

    g = {"g_final": dg_final}
    parts, gots = {}, {}
    def add(name, piece, recv):
        parts[name] = _add_half(piece, recv, c_idx, f"rs_add_{name}")
        return parts[name]

    def ffn_wgrads(layer, dgt, dup, act, nb, dh_out, carry=None):
        g_ffn, *carried = _ffn_wgrad(dgt, nb, f"ffn_wgrad_gate{layer}", tm_w, 0, carry=carry)
        (g_ffn,) = _ffn_wgrad(dup, nb, f"ffn_wgrad_up{layer}", tm_w, 1, into=g_ffn)
        (g_ffn,) = _ffn_wgrad(act, dh_out, f"ffn_wgrad_down{layer}", tm_w, 2, into=g_ffn)
        return g_ffn, carried

    dh5, dg_ffn1, dgt, dup, act, nb = _ffn_bwd_dx(h5, dh6, gt1, up1, row(sm["g_ffn"], 1), wg1, wu1, dn1, 1, tm_b)
    g_ffn1, _ = ffn_wgrads(1, dgt, dup, act, nb, dh6)
    dh4, dk, dv, dg_xa1, g_xa1, recv = _xattn_bwd(
        h4, dh5, q1, row(sm["g_xattn"], 1), k1, v1, xa1, 1, tm_f, carry=("swap", [g_ffn1]))
    add("ffn1", g_ffn1, recv)
    dg_mem1, g_xa1 = _kv_proj_bwd(mem, row(sm["g_mem"], 1), dk, dv, xa1, g_xa1, 1)
    (dh3, g["od_w_s"], g["od_b_s"], g["od_c_ln_g"], g["od_c_ln_b"], dg_mix1, g_cod, g_odo, gots["ffn1"],
     recv) = _sgu_bwd(
        h3, dh4, zpre, row(sm["g_mix"], 1), sm["lng_c"], sm["lnb_c"], sm["ws"], sm["wst"], sm["bsb"], wod, odo, tm_b,
        carry=[("xchg", [parts["ffn1"]], []), ("swap", [g_xa1])])
    add("xa1", g_xa1, recv)
    dh2, dg_ffn0, dgt, dup, act, nb, gots["xa1"], recv_cod, recv_odo = _ffn_bwd_dx(
        h2, dh3, gt0, up0, row(sm["g_ffn"], 0), wg0, wu0, dn0, 0, tm_b,
        carry=[("xchg", [parts["xa1"]], []), ("swap", [g_cod, g_odo])])
    add("cod", g_cod, recv_cod)
    add("odo", g_odo, recv_odo)
    g_ffn0, (gots["cod"], gots["odo"]) = ffn_wgrads(0, dgt, dup, act, nb, dh3,
                                                    carry=("xchg", [parts["cod"], parts["odo"]], []))
    dh1, dk, dv, dg_xa0, g_xa0, recv = _xattn_bwd(
        h1, dh2, q0, row(sm["g_xattn"], 0), k0, v0, xa0, 0, tm_f, carry=("swap", [g_ffn0]))
    add("ffn0", g_ffn0, recv)
    dg_mem0, g_xa0 = _kv_proj_bwd(mem, row(sm["g_mem"], 0), dk, dv, xa0, g_xa0, 0)
    da2, dc2, dgb, g["ev_a_ln_g"], g["ev_a_ln_b"], g_evo, recv = _ev_mix_bwd(
        dh1, cat, a2, c2, z, sm["lng_a"], sm["lnb_a"], evo, 2 * tm_f, carry=("swap", [g_xa0]))
    add("xa0", g_xa0, recv)
    handle, token = _comm_start([parts["ffn0"], parts["xa0"]], [], [g_evo], "ev")
    dx, g["ev_a_conv_w"], g["ev_a_conv_b"], g["ev_b_conv_w"], g["ev_b_conv_b"], dg_mix0, g_cev = _ev_in_bwd(
        x, dh1, z, da2, dc2, dgb, row(sm["g_mix"], 0), sm["wa"], sm["wb"], wev, tm_f, token)
    (parts["ffn0"], parts["xa0"], g_evo), (gots["ffn0"], gots["xa0"], recv) = _comm_wait(handle, g_cev)
    add("evo", g_evo, recv)
    g["g_mix"] = jnp.concatenate([dg_mix0, dg_mix1], axis=0)
    g["g_xattn"] = jnp.concatenate([dg_xa0, dg_xa1], axis=0)
    g["g_mem"] = jnp.concatenate([dg_mem0, dg_mem1], axis=0)
    g["g_ffn"] = jnp.concatenate([dg_ffn0, dg_ffn1], axis=0)
    ga, gb = _small_grad_packs(g, loss)
    last = (g_cev, ga[None], gb[None])
    for name, piece, recv in zip(("cev", "ga", "gb"), last, _sibling_swap(last, "rs_swap_last")):
        add(name, piece, recv)
    return dx, parts, gots


def kernel(x, mem, g_mix, g_xattn, g_mem, g_ffn, g_final, ev_w_in, ev_a_conv_w, ev_a_conv_b, ev_a_ln_g, ev_a_ln_b, ev_b_conv_w, ev_b_conv_b, ev_w_out, od_w_in, od_c_ln_g, od_c_ln_b, od_w_s, od_b_s, od_w_out, xa_w_q, xa_w_k, xa_w_v, xa_w_o, ffn_w_gate, ffn_w_up, ffn_w_down, loss_target, m_g_mix, m_g_xattn, m_g_mem, m_g_ffn, m_g_final, m_ev_w_in, m_ev_a_conv_w, m_ev_a_conv_b, m_ev_a_ln_g, m_ev_a_ln_b, m_ev_b_conv_w, m_ev_b_conv_b, m_ev_w_out, m_od_w_in, m_od_c_ln_g, m_od_c_ln_b, m_od_w_s, m_od_b_s, m_od_w_out, m_xa_w_q, m_xa_w_k, m_xa_w_v, m_xa_w_o, m_ffn_w_gate, m_ffn_w_up, m_ffn_w_down, v_g_mix, v_g_xattn, v_g_mem, v_g_ffn, v_g_final, v_ev_w_in, v_ev_a_conv_w, v_ev_a_conv_b, v_ev_a_ln_g, v_ev_a_ln_b, v_ev_b_conv_w, v_ev_b_conv_b, v_ev_w_out, v_od_w_in, v_od_c_ln_g, v_od_c_ln_b, v_od_w_s, v_od_b_s, v_od_w_out, v_xa_w_q, v_xa_w_k, v_xa_w_v, v_xa_w_o, v_ffn_w_gate, v_ffn_w_up, v_ffn_w_down):
    W = dict(g_mix=g_mix, g_xattn=g_xattn, g_mem=g_mem, g_ffn=g_ffn, g_final=g_final, ev_w_in=ev_w_in,
             ev_a_conv_w=ev_a_conv_w, ev_a_conv_b=ev_a_conv_b, ev_a_ln_g=ev_a_ln_g, ev_a_ln_b=ev_a_ln_b,
             ev_b_conv_w=ev_b_conv_w, ev_b_conv_b=ev_b_conv_b, ev_w_out=ev_w_out, od_w_in=od_w_in,
             od_c_ln_g=od_c_ln_g, od_c_ln_b=od_c_ln_b, od_w_s=od_w_s, od_b_s=od_b_s, od_w_out=od_w_out,
             xa_w_q=xa_w_q, xa_w_k=xa_w_k, xa_w_v=xa_w_v, xa_w_o=xa_w_o, ffn_w_gate=ffn_w_gate,
             ffn_w_up=ffn_w_up, ffn_w_down=ffn_w_down)
    M = dict(g_mix=m_g_mix, g_xattn=m_g_xattn, g_mem=m_g_mem, g_ffn=m_g_ffn, g_final=m_g_final, ev_w_in=m_ev_w_in,
             ev_a_conv_w=m_ev_a_conv_w, ev_a_conv_b=m_ev_a_conv_b, ev_a_ln_g=m_ev_a_ln_g, ev_a_ln_b=m_ev_a_ln_b,
             ev_b_conv_w=m_ev_b_conv_w, ev_b_conv_b=m_ev_b_conv_b, ev_w_out=m_ev_w_out, od_w_in=m_od_w_in,
             od_c_ln_g=m_od_c_ln_g, od_c_ln_b=m_od_c_ln_b, od_w_s=m_od_w_s, od_b_s=m_od_b_s, od_w_out=m_od_w_out,
             xa_w_q=m_xa_w_q, xa_w_k=m_xa_w_k, xa_w_v=m_xa_w_v, xa_w_o=m_xa_w_o, ffn_w_gate=m_ffn_w_gate,
             ffn_w_up=m_ffn_w_up, ffn_w_down=m_ffn_w_down)
    V = dict(g_mix=v_g_mix, g_xattn=v_g_xattn, g_mem=v_g_mem, g_ffn=v_g_ffn, g_final=v_g_final, ev_w_in=v_ev_w_in,
             ev_a_conv_w=v_ev_a_conv_w, ev_a_conv_b=v_ev_a_conv_b, ev_a_ln_g=v_ev_a_ln_g, ev_a_ln_b=v_ev_a_ln_b,
             ev_b_conv_w=v_ev_b_conv_w, ev_b_conv_b=v_ev_b_conv_b, ev_w_out=v_ev_w_out, od_w_in=v_od_w_in,
             od_c_ln_g=v_od_c_ln_g, od_c_ln_b=v_od_c_ln_b, od_w_s=v_od_w_s, od_b_s=v_od_b_s, od_w_out=v_od_w_out,
             xa_w_q=v_xa_w_q, xa_w_k=v_xa_w_k, xa_w_v=v_xa_w_v, xa_w_o=v_xa_w_o, ffn_w_gate=v_ffn_w_gate,
             ffn_w_up=v_ffn_w_up, ffn_w_down=v_ffn_w_down)
    order = list(W)
    cx, cy, cc = _pos()
    chip = (2 * cx + cy).astype(jnp.int32)
    cc = cc.astype(jnp.int32)
    chip_idx = jnp.reshape(chip, (1,))
    c_idx = jnp.reshape(cc, (1,))

    tr = lambda a: jnp.swapaxes(a, 1, 2)
    P = {}
    cev, P["cod"], gs = _pack_cols(chip_idx, od_w_in, ev_w_in, ev_a_conv_w, ev_b_conv_w, od_c_ln_g, od_c_ln_b)
    ((P["evo"], P["odo"], P["dn0"], P["dn1"], P["wg0"], P["wg1"], P["wu0"], P["wu1"], P["xa0"], P["xa1"]),
     (P["cev"], P["gs"])) = _pack_rows(chip_idx, ffn_w_down, tr(ffn_w_gate), tr(ffn_w_up), xa_w_q, xa_w_o,
                                       xa_w_k, xa_w_v, ev_w_out, od_w_out, first=[cev, gs])

    dx, parts, gots = _step(x[0], mem[0], loss_target[0], W, P, c_idx, TM_FWD, TM_BWD, TM_WGRAD)

    chip_c = jnp.stack([chip, cc])
    zero_c = jnp.stack([jnp.zeros((), jnp.int32), cc])
    def sums(names):
        return [_sum_chips(parts[n], gots[n], zero_c if n in ("ga", "gb") else chip_c, f"rs_sum_{n}")
                for n in names]

    last = ("evo", "cev", "ga", "gb")
    handle, token = _comm_start([parts["evo"], parts["cev"]], [parts["ga"][0], parts["gb"][0]], [], "last")
    big_names, rest_names = ["ffn1", "ffn0"], ["xa1", "xa0", "cod", "odo"]
    join_big, token = _join_start(sums(big_names), "ffn", after=(token,))
    join_rest, token = _join_start(sums(rest_names), "rest", after=(token,))
    red = dict(zip(big_names, _join_wait(join_big, token)))

    res = {}

    def adamw(name, srcs, gblock, gidx, rb):
        t = tr if name in ("ffn_w_gate", "ffn_w_up") else (lambda a: a)
        outs = _adamw_big(t(W[name]), t(M[name]), t(V[name]), srcs, gblock, gidx, f"adamw_{name}", rb)
        res[name] = tuple(t(o) for o in outs)

    ffn = [red["ffn0"], red["ffn1"]]
    adamw("ffn_w_gate", ffn, (FSH, D), lambda i: (0, 0), FSH)
    adamw("ffn_w_up", ffn, (FSH, D), lambda i: (1, 0), FSH)
    adamw("ffn_w_down", ffn, (FSH, D), lambda i: (2, 0), FSH)
    red.update(zip(rest_names, _join_wait(join_rest, res["ffn_w_down"][1])))
    xa_names = ("xa_w_q", "xa_w_o", "xa_w_k", "xa_w_v")
    assert (XQ, XO, XK, XV) == (0, 1, 2, 3)
    for name, outs in zip(xa_names, _adamw_rows([W[n] for n in xa_names], [M[n] for n in xa_names],
                                                [V[n] for n in xa_names], [red["xa0"], red["xa1"]], "adamw_xa")):
        res[name] = outs
    adamw("od_w_out", [red["odo"]], (256, D), lambda i: (0, 0), 256)
    adamw("od_w_in", [red["cod"]], (256, 512), lambda i: (i, 0), 256)

    srcs_done, got_last = _comm_wait(handle, res["od_w_in"][1])
    parts.update(zip(last, srcs_done[:2] + [a[None] for a in srcs_done[2:]]))
    gots.update(zip(last, got_last))
    red.update(zip(last, _sibling_join(sums(last), "rs_join_last")))
    adamw("ev_w_out", [red["evo"]], (256, D), lambda i: (0, 0), 256)
    adamw("ev_w_in", [red["cev"]], (256, 640), lambda i: (i, 0), 256)
    shape2 = {name: shape for name, shape, *_ in _SMALL}
    wmv = [tuple(d[name].reshape(shape2[name]) for d in (W, M, V)) for name, *_ in _SMALL]
    for (name, *_), outs in zip(_SMALL, _adamw_small(red["ga"], red["gb"], wmv)):
        res[name] = tuple(o.reshape(W[name].shape) for o in outs)

    grad_x = dx[None]
    loss = red["ga"][LOSS_ROW, 0]
    return (loss, grad_x, *[res[n][0] for n in order], *[res[n][1] for n in order],
            *[res[n][2] for n in order], *[res[n][3] for n in order])
```

```python
import jax
import jax.numpy as jnp
from jax import lax
from jax.experimental import pallas as pl
from jax.experimental.pallas import tpu as pltpu

BF = jnp.bfloat16
F32 = jnp.float32

D = 1024
DA = 512
DZ = 2560
DFF = 2816
NSH = 4
FSH = DFF // NSH
FF_CHUNKS = ((0, 768), (768, 1536), (1536, 2304), (2304, 2816))
NMEM = 256
HEADS = 4
HD = D // HEADS
CHUNK = 128
GROUPS = 8
KA = 31
KB = 3
HALO = 16
RC = 32
RMS_EPS = 1e-6
LN_EPS = 1e-5
ATT_SCALE = HD ** -0.5
TM_FWD = 512
TM_BWD = 256
TM_WGRAD = 1024

ADAM_LR = 0.001
ADAM_B1 = 0.9
ADAM_B2 = 0.999
ADAM_EPS = 1e-08
ADAM_WD = 0.01
ADAM_STEP = 10

XQ, XO, XK, XV = 0, 1, 2, 3
SP_ROWS = 48
SA_ROWS = 64

MESH = pl.DeviceIdType.MESH
ANY = pl.BlockSpec(memory_space=pl.ANY)


def _dot(a, b):
    return jnp.dot(a, b, preferred_element_type=F32)


def _dot_nt(a, b):
    return lax.dot_general(a, b, (((1,), (1,)), ((), ())), preferred_element_type=F32)


def _dot_tn(a, b):
    return lax.dot_general(a, b, (((0,), (0,)), ((), ())), preferred_element_type=F32)


def _rms(h):
    r = lax.rsqrt(jnp.mean(h * h, axis=-1, keepdims=True) + RMS_EPS)
    return h * r, r


def _rms_bwd(dn, hhat, r, g):
    dhh = dn * g
    return r * (dhh - hhat * jnp.mean(dhh * hhat, axis=-1, keepdims=True))


def _ln(x):
    mu = jnp.mean(x, axis=-1, keepdims=True)
    xc = x - mu
    rstd = lax.rsqrt(jnp.mean(xc * xc, axis=-1, keepdims=True) + LN_EPS)
    return xc * rstd, rstd


def _ln_bwd(dy, y, rstd):
    return rstd * (dy - jnp.mean(dy, axis=-1, keepdims=True) - y * jnp.mean(dy * y, axis=-1, keepdims=True))


def _colsum(x):
    return jnp.sum(x, axis=0, keepdims=True)


def _sigmoid(x):
    return 1.0 / (1.0 + jnp.exp(-x))


_GELU_C = 0.7978845608028654
_GELU_A = 0.044715


def _gelu(x):
    t = jnp.tanh(_GELU_C * (x + _GELU_A * x * x * x))
    return 0.5 * x * (1.0 + t)


def _gelu_grad(x):
    t = jnp.tanh(_GELU_C * (x + _GELU_A * x * x * x))
    return 0.5 * (1.0 + t) + 0.5 * x * (1.0 - t * t) * _GELU_C * (1.0 + 3.0 * _GELU_A * x * x)


def _first(i):
    return i == 0


def _last(i):
    return i == pl.num_programs(0) - 1


def _const(shape, idx):
    return pl.BlockSpec(shape, lambda *_, _idx=tuple(idx): _idx, pipeline_mode=pl.Buffered(1))


def _whole(a):
    return _const(a.shape, (0,) * a.ndim)


def _rows(tm, width, col=0):
    return pl.BlockSpec((tm, width), lambda i, _c=col: (i, _c))


def _params(ndim=1, vmem_mb=56):
    return pltpu.CompilerParams(dimension_semantics=("arbitrary",) * ndim, vmem_limit_bytes=vmem_mb << 20)


def _sds(shape, dtype):
    return jax.ShapeDtypeStruct(shape, dtype)


def _pos():
    return lax.axis_index("x"), lax.axis_index("y"), lax.axis_index("c")


def _other_chips(x, y):
    return [(1 - x, y), (x, 1 - y), (1 - x, 1 - y)]


def _half(ref, hc, lead=()):
    r = ref.shape[-2] // 2
    return ref.at[(*lead, pl.ds(pl.multiple_of(hc * r, 8), r), slice(None))]


def _rcopy(src, dst, ssem, rsem, to):
    return pltpu.make_async_remote_copy(src_ref=src, dst_ref=dst, send_sem=ssem, recv_sem=rsem,
                                        device_id=to, device_id_type=MESH)


def _gather_start(bufs, sems):
    ici_s, ici_r, _, _ = sems
    x, y, c = _pos()
    me = 2 * x + y
    for k, (cx, cy) in enumerate(_other_chips(x, y)):
        for t, buf in enumerate(bufs):
            mine = _half(buf, c, (me,))
            _rcopy(mine, mine, ici_s.at[3 * t + k], ici_r.at[3 * t + k], (cx, cy, c)).start()


def _gather_forward(bufs, sems):
    ici_s, ici_r, d2d_s, d2d_r = sems
    x, y, c = _pos()
    sib = (x, y, 1 - c)
    for k, (cx, cy) in enumerate(_other_chips(x, y)):
        for t, buf in enumerate(bufs):
            landed = _half(buf, c, (2 * cx + cy,))
            _rcopy(landed, landed, ici_s.at[3 * t + k], ici_r.at[3 * t + k], (cx, cy, c)).wait_recv()
            _rcopy(landed, landed, d2d_s.at[3 * t + k], d2d_r.at[3 * t + k], sib).start()


def _gather_drain(bufs, sems):
    ici_s, ici_r, d2d_s, d2d_r = sems
    x, y, c = _pos()
    me = 2 * x + y
    sib = (x, y, 1 - c)
    chips = _other_chips(x, y)
    for k, (cx, cy) in enumerate(chips):
        for t, buf in enumerate(bufs):
            got = _half(buf, 1 - c, (2 * cx + cy,))
            _rcopy(got, got, d2d_s.at[3 * t + k], d2d_r.at[3 * t + k], sib).wait_recv()
    for k, (cx, cy) in enumerate(chips):
        for t, buf in enumerate(bufs):
            mine = _half(buf, c, (me,))
            _rcopy(mine, mine, ici_s.at[3 * t + k], ici_r.at[3 * t + k], (cx, cy, c)).wait_send()
            landed = _half(buf, c, (2 * cx + cy,))
            _rcopy(landed, landed, d2d_s.at[3 * t + k], d2d_r.at[3 * t + k], sib).wait_send()


def _xchg_copies(srcs, dsts, nbig, sems):
    ssem, rsem = sems
    x, y, c = _pos()
    cps = []
    for k, (cx, cy) in enumerate(_other_chips(x, y)):
        for t, (src, dst) in enumerate(zip(srcs, dsts)):
            s = src.at[2 * cx + cy] if t < nbig else src
            cps.append(_rcopy(s, dst.at[k], ssem.at[3 * t + k], rsem.at[3 * t + k], (cx, cy, c)))
    return cps


def _gather_sems(n):
    return [pltpu.SemaphoreType.DMA((3 * n,))] * 4


def _xchg_sems(n):
    return [pltpu.SemaphoreType.DMA((3 * n,))] * 2


def _xchg_out_shapes(items):
    return [_sds((3,) + a.shape[-2:], a.dtype) for a in items]


def _swap_copies(srcs, dsts, sems):
    ssem, rsem = sems
    x, y, c = _pos()
    cps = []
    for t, (src, dst) in enumerate(zip(srcs, dsts)):
        lead = (slice(None),) * (len(src.shape) - 2)
        cps.append(_rcopy(_half(src, 1 - c, lead), dst, ssem.at[t], rsem.at[t], (x, y, 1 - c)))
    return cps


def _swap_out_shapes(items):
    return [_sds(a.shape[:-2] + (a.shape[-2] // 2, a.shape[-1]), a.dtype) for a in items]


def _call(body, *, name, grid, in_specs, out_specs, out_shape, args, scratch_shapes=(), aliases=None, carry=None):
    in_specs, out_specs, out_shape = list(in_specs), list(out_specs), list(out_shape)
    args = list(args)
    aliases = dict(aliases or {})
    n_in, n_out, n_scr = len(args), len(out_shape), len(scratch_shapes)
    carries = [] if not carry else ([carry] if isinstance(carry, tuple) else list(carry))
    plans = []
    cin, cout_shape, sems = [], [], []
    for c in carries:
        kind = c[0]
        items = list(c[1]) + (list(c[2]) if kind == "xchg" else [])
        if kind == "gather":
            shapes, sm = [_sds(a.shape, a.dtype) for a in items], _gather_sems(len(items))
            for t in range(len(items)):
                aliases[n_in + len(cin) + t] = n_out + len(cin) + t
        elif kind == "xchg":
            shapes, sm = _xchg_out_shapes(items), _xchg_sems(len(items))
        else:
            shapes, sm = _swap_out_shapes(items), [pltpu.SemaphoreType.DMA((len(items),))] * 2
        plans.append((kind, len(cin), len(items), len(sems), len(sm), len(c[1])))
        cin += items
        cout_shape += shapes
        sems += sm
    nc = len(cin)

    def wrapped(*refs):
        ins = refs[:n_in]
        c_in = refs[n_in:n_in + nc]
        outs = refs[n_in + nc:n_in + nc + n_out]
        c_out = refs[n_in + nc + n_out:n_in + 2 * nc + n_out]
        scr = refs[n_in + 2 * nc + n_out:n_in + 2 * nc + n_out + n_scr]
        sem = refs[n_in + 2 * nc + n_out + n_scr:]
        i = pl.program_id(0)

        def copies(plan):
            kind, o0, no, s0, ns, nbig = plan
            if kind == "xchg":
                return _xchg_copies(c_in[o0:o0 + no], c_out[o0:o0 + no], nbig, sem[s0:s0 + ns])
            return _swap_copies(c_in[o0:o0 + no], c_out[o0:o0 + no], sem[s0:s0 + ns])

        if plans:
            @pl.when(_first(i))
            def _():
                for plan in plans:
                    kind, o0, no, s0, ns, _ = plan
                    if kind == "gather":
                        _gather_start(c_out[o0:o0 + no], sem[s0:s0 + ns])
                    else:
                        for cp in copies(plan):
                            cp.start()
        body(*ins, *outs, *scr)
        for plan in plans:
            kind, o0, no, s0, ns, _ = plan
            if kind == "gather":
                @pl.when(i == max(grid[0] - 2, 0))
                def _(o0=o0, no=no, s0=s0, ns=ns):
                    _gather_forward(c_out[o0:o0 + no], sem[s0:s0 + ns])
        if plans:
            @pl.when(_last(i))
            def _():
                for plan in plans:
                    kind, o0, no, s0, ns, _ = plan
                    if kind == "gather":
                        _gather_drain(c_out[o0:o0 + no], sem[s0:s0 + ns])
                    else:
                        for cp in copies(plan):
                            cp.wait()

    outs = pl.pallas_call(
        wrapped, name=name, grid=grid,
        in_specs=in_specs + [ANY] * nc, out_specs=out_specs + [ANY] * nc,
        out_shape=out_shape + cout_shape,
        scratch_shapes=list(scratch_shapes) + sems,
        input_output_aliases=aliases,
        compiler_params=_params(len(grid)),
    )(*args, *cin)
    return list(outs)


def _sibling_swap(items, name):
    n = len(items)

    def body(*refs):
        src = refs[:n]
        dst = refs[n:2 * n]
        ssem, rsem = refs[2 * n:]
        x, y, c = _pos()
        sib = (x, y, 1 - c)
        cps = []
        for t in range(n):
            lead = (slice(None),) * (len(src[t].shape) - 2)
            cp = _rcopy(_half(src[t], 1 - c, lead), dst[t], ssem.at[t], rsem.at[t], sib)
            cp.start()
            cps.append(cp)
        for cp in cps:
            cp.wait()

    return pl.pallas_call(
        body, name=name, in_specs=[ANY] * n, out_specs=[ANY] * n,
        out_shape=[_sds(a.shape[:-2] + (a.shape[-2] // 2, a.shape[-1]), a.dtype) for a in items],
        scratch_shapes=[pltpu.SemaphoreType.DMA((n,)), pltpu.SemaphoreType.DMA((n,))],
    )(*items)


_HBM = pl.BlockSpec(memory_space=pltpu.HBM)
_SEM = pl.BlockSpec(memory_space=pltpu.SEMAPHORE)
_EFFECT = pltpu.SideEffectType.DATAFLOW_SIDE_EFFECTING


def _split_copies(src, land, nbig, nx, sems):
    return (_xchg_copies(src[:nx], land[:nx], nbig, sems[0:2]) if nx else []) + \
        (_swap_copies(src[nx:], land[nx:], sems[2:4]) if len(src) > nx else [])


def _comm_start(big, small, swaps, tag):
    xs = list(big) + list(small)
    items = xs + list(swaps)
    n, nx = len(items), len(xs)

    def body(*refs):
        for cp in _split_copies(refs[:n], refs[n:2 * n], len(big), nx, refs[2 * n:2 * n + 4]):
            cp.start()
        refs[-1][...] = jnp.zeros_like(refs[-1])

    lands = _xchg_out_shapes(xs) + _swap_out_shapes(swaps)
    sem_shapes = [pltpu.SemaphoreType.DMA((max(3 * nx, 1),))] * 2 + [pltpu.SemaphoreType.DMA((max(n - nx, 1),))] * 2
    hbm = lambda a: pltpu.with_memory_space_constraint(a, pltpu.HBM)
    outs = pl.pallas_call(
        body, name=f"rs_start_{tag}",
        out_shape=sem_shapes + [pltpu.HBM(a.shape, a.dtype) for a in items]
        + [pltpu.HBM(a.shape, a.dtype) for a in lands] + [_sds((8, 128), F32)],
        in_specs=[_HBM] * (2 * n), out_specs=[_SEM] * 4 + [_HBM] * (2 * n) + [pl.BlockSpec(memory_space=pltpu.VMEM)],
        input_output_aliases={t: 4 + t for t in range(2 * n)},
        compiler_params=pltpu.CompilerParams(has_side_effects=_EFFECT),
    )(*[hbm(a) for a in items], *[hbm(lax.empty(a.shape, a.dtype)) for a in lands])
    handle = (list(outs[:4]), list(outs[4:4 + n]), list(outs[4 + n:4 + 2 * n]), len(big), nx, tag)
    return handle, outs[-1]


def _comm_wait(handle, after):
    sems, srcs, lands, nbig, nx, tag = handle
    n = len(srcs)

    def body(*refs):
        for cp in _split_copies(refs[:n], refs[n:2 * n], nbig, nx, refs[2 * n:2 * n + 4]):
            cp.wait_send()
            cp.wait_recv()

    outs = pl.pallas_call(
        body, name=f"rs_wait_{tag}",
        out_shape=[pltpu.HBM(a.shape, a.dtype) for a in srcs] + [pltpu.HBM(a.shape, a.dtype) for a in lands],
        in_specs=[_HBM] * (2 * n) + [_SEM] * 4 + [ANY], out_specs=[_HBM] * (2 * n),
        input_output_aliases={t: t for t in range(2 * n)},
        compiler_params=pltpu.CompilerParams(has_side_effects=_EFFECT),
    )(*srcs, *lands, *sems, after)
    return list(outs[:n]), list(outs[n:])


def _join_copies(items, sems):
    ssem, rsem = sems
    x, y, c = _pos()
    return [_rcopy(_half(a, c), _half(a, c), ssem.at[t], rsem.at[t], (x, y, 1 - c)) for t, a in enumerate(items)]


def _join_start(items, tag, after=()):
    n = len(items)

    def body(*refs):
        for cp in _join_copies(refs[:n], refs[n + len(after):n + len(after) + 2]):
            cp.start()
        refs[-1][...] = jnp.zeros_like(refs[-1])

    hbm = lambda a: pltpu.with_memory_space_constraint(a, pltpu.HBM)
    outs = pl.pallas_call(
        body, name=f"rs_join_start_{tag}",
        out_shape=[pltpu.SemaphoreType.DMA((n,))] * 2 + [pltpu.HBM(a.shape, a.dtype) for a in items]
        + [_sds((8, 128), F32)],
        in_specs=[_HBM] * n + [ANY] * len(after),
        out_specs=[_SEM] * 2 + [_HBM] * n + [pl.BlockSpec(memory_space=pltpu.VMEM)],
        input_output_aliases={t: 2 + t for t in range(n)},
        compiler_params=pltpu.CompilerParams(has_side_effects=_EFFECT),
    )(*[hbm(a) for a in items], *after)
    return (list(outs[:2]), list(outs[2:2 + n]), tag), outs[-1]


def _join_wait(handle, after):
    sems, items, tag = handle
    n = len(items)

    def body(*refs):
        x, y, c = _pos()
        ssem, rsem = refs[n:n + 2]
        for t, cp in enumerate(_join_copies(refs[:n], (ssem, rsem))):
            cp.wait_send()
            theirs = _half(refs[t], 1 - c)
            _rcopy(theirs, theirs, ssem.at[t], rsem.at[t], (x, y, 1 - c)).wait_recv()

    outs = pl.pallas_call(
        body, name=f"rs_join_wait_{tag}",
        out_shape=[pltpu.HBM(a.shape, a.dtype) for a in items],
        in_specs=[_HBM] * n + [_SEM] * 2 + [ANY], out_specs=[_HBM] * n,
        input_output_aliases={t: t for t in range(n)},
        compiler_params=pltpu.CompilerParams(has_side_effects=_EFFECT),
    )(*items, *sems, after)
    return list(outs)


def _sibling_join(items, name, after=()):
    n = len(items)

    def body(*refs):
        dst = refs[n + len(after):2 * n + len(after)]
        ssem, rsem = refs[2 * n + len(after):]
        x, y, c = _pos()
        sib = (x, y, 1 - c)
        cps = []
        for t in range(n):
            mine = _half(dst[t], c)
            cp = _rcopy(mine, mine, ssem.at[t], rsem.at[t], sib)
            cp.start()
            cps.append(cp)
        for t, cp in enumerate(cps):
            theirs = _half(dst[t], 1 - c)
            _rcopy(theirs, theirs, ssem.at[t], rsem.at[t], sib).wait_recv()
            cp.wait_send()

    outs = pl.pallas_call(
        body, name=name, in_specs=[ANY] * (n + len(after)), out_specs=[ANY] * n,
        out_shape=[_sds(a.shape, a.dtype) for a in items],
        input_output_aliases={t: t for t in range(n)},
        scratch_shapes=[pltpu.SemaphoreType.DMA((n,))] * 2,
    )(*items, *after)
    return list(outs)


def _ev_in_fwd(h, g, wev, tm, carry=None):
    s = h.shape[0]

    def body(h_ref, g_ref, w_ref, z_ref):
        hhat, _ = _rms(h_ref[...])
        n = (hhat * g_ref[...]).astype(BF)
        z_ref[...] = _dot(n, w_ref[...]).astype(BF)

    return _call(
        body, name="ev_in_fwd", grid=(s // tm,),
        in_specs=[_rows(tm, D), _const((1, D), (0, 0)), _const((D, DZ), (0, 0))],
        out_specs=[_rows(tm, DZ)], out_shape=[_sds((s, DZ), BF)],
        args=(h, g, wev), carry=carry)


def _halo_specs(tm, width, s, col=0):
    per = tm // HALO
    nh = s // HALO
    prev = pl.BlockSpec((HALO, width), lambda i, _c=col: (jnp.maximum(i * per - 1, 0), _c))
    nxt = pl.BlockSpec((HALO, width), lambda i, _c=col: (jnp.minimum((i + 1) * per, nh - 1), _c))
    return prev, nxt


def _fill_pad(pad_ref, prev, main, nxt, i, tm):
    pad_ref[pl.ds(0, HALO), :] = jnp.where(_first(i), 0.0, prev)
    pad_ref[pl.ds(HALO, tm), :] = main
    pad_ref[pl.ds(HALO + tm, HALO), :] = jnp.where(_last(i), 0.0, nxt)


def _shift8(xs_ref, pad_ref, tm):
    for j in range(8):
        xs_ref[j] = pad_ref[pl.ds(j, tm + 2 * HALO - 8), :]


def _tap(xs_ref, r, off):
    return xs_ref[off % 8, pl.ds(r + 8 * (off // 8), RC), :]


def _glu_a(z):
    return z[:, :DA] * _sigmoid(z[:, DA:2 * DA])


def _gate_c(z):
    return z[:, 2048:] * z[:, 1024:1536]


def _ev_mix_fwd(h, z, wa, ba, lng, lnb, wb, bb, evo, tm, carry=None):
    s = h.shape[0]

    def body(h_ref, z_ref, zp_ref, zn_ref, wa_ref, ba_ref, lng_ref, lnb_ref, wb_ref, bb_ref, wo_ref,
             h1_ref, a2_ref, c2_ref, cat_ref, apad, cpad, c2f, xs):
        i = pl.program_id(0)
        zm = z_ref[...].astype(F32)
        zp = zp_ref[...].astype(F32)
        zn = zn_ref[...].astype(F32)
        _fill_pad(apad, _glu_a(zp), _glu_a(zm), _glu_a(zn), i, tm)
        _fill_pad(cpad, _gate_c(zp), _gate_c(zm), _gate_c(zn), i, tm)
        _shift8(xs, apad, tm)
        for r in range(0, tm, RC):
            acc = jnp.zeros((RC, DA), F32) + ba_ref[...]
            for k in range(KA):
                acc = acc + _tap(xs, r, k + 1) * wa_ref[pl.ds(k, 1), :]
            a2_ref[pl.ds(r, RC), :] = acc
            acc = jnp.zeros((RC, DA), F32) + bb_ref[...]
            for k in range(KB):
                acc = acc + cpad[pl.ds(r + HALO - 1 + k, RC), :] * wb_ref[pl.ds(k, 1), :]
            c2f[pl.ds(r, RC), :] = acc
        y, _ = _ln(a2_ref[...])
        a3 = y * lng_ref[...] + lnb_ref[...]
        cat_ref[:, :DA] = (a3 * _sigmoid(a3)).astype(BF)
        c2 = c2f[...]
        c2_ref[...] = c2.astype(BF)
        cat_ref[:, DA:] = (zm[:, 1536:2048] * c2).astype(BF)
        h1_ref[...] = h_ref[...] + _dot(cat_ref[...], wo_ref[...].reshape(D, D))

    zp_spec, zn_spec = _halo_specs(tm, DZ, s)
    small = lambda rows: _const((rows, DA), (0, 0))
    return _call(
        body, name="ev_mix_fwd", grid=(s // tm,),
        in_specs=[_rows(tm, D), _rows(tm, DZ), zp_spec, zn_spec, small(32), small(1), small(1), small(1),
                  small(8), small(1), _whole(evo)],
        out_specs=[_rows(tm, D), _rows(tm, DA), _rows(tm, DA), _rows(tm, D)],
        out_shape=[_sds((s, D), F32), _sds((s, DA), F32), _sds((s, DA), BF), _sds((s, D), BF)],
        scratch_shapes=[pltpu.VMEM((tm + 2 * HALO, DA), F32), pltpu.VMEM((tm + 2 * HALO, DA), F32),
                        pltpu.VMEM((tm, DA), F32), pltpu.VMEM((8, tm + 2 * HALO - 8, DA), F32)],
        args=(h, z, z, z, wa, ba, lng, lnb, wb, bb, evo), carry=carry)


def _ev_mix_bwd(dh, cat, a2, c2, z, lng, lnb, evo, tm, carry=None):
    s = dh.shape[0]

    def body(dh_ref, cat_ref, a2_ref, c2_ref, zgb_ref, lng_ref, lnb_ref, wo_ref,
             da2_ref, dc2_ref, dgb_ref, dlng_ref, dlnb_ref, gwo_ref, wacc):
        i = pl.program_id(0)

        @pl.when(_first(i))
        def _():
            wacc[...] = jnp.zeros_like(wacc)
            dlng_ref[...] = jnp.zeros_like(dlng_ref)
            dlnb_ref[...] = jnp.zeros_like(dlnb_ref)

        dhb = dh_ref[...].astype(BF)
        dcat = _dot_nt(dhb, wo_ref[...].reshape(D, D))
        wacc[...] += _dot_tn(cat_ref[...], dhb)
        y, rstd = _ln(a2_ref[...])
        a3 = y * lng_ref[...] + lnb_ref[...]
        sg = _sigmoid(a3)
        da3 = dcat[:, :DA] * (sg * (1.0 + a3 * (1.0 - sg)))
        dlng_ref[...] += _colsum(da3 * y)
        dlnb_ref[...] += _colsum(da3)
        da2_ref[...] = _ln_bwd(da3 * lng_ref[...], y, rstd).astype(BF)
        db = dcat[:, DA:]
        dc2_ref[...] = (db * zgb_ref[...].astype(F32)).astype(BF)
        dgb_ref[...] = (db * c2_ref[...].astype(F32)).astype(BF)

        @pl.when(_last(i))
        def _():
            gwo_ref[...] = wacc[...].astype(BF).reshape(NSH, 256, D)

    small = _const((1, DA), (0, 0))
    return _call(
        body, name="ev_mix_bwd", grid=(s // tm,),
        in_specs=[_rows(tm, D), _rows(tm, D), _rows(tm, DA), _rows(tm, DA), _rows(tm, DA, 3), small, small,
                  _whole(evo)],
        out_specs=[_rows(tm, DA), _rows(tm, DA), _rows(tm, DA), small, small, _whole(evo)],
        out_shape=[_sds((s, DA), BF), _sds((s, DA), BF), _sds((s, DA), BF), _sds((1, DA), F32),
                   _sds((1, DA), F32), _sds(evo.shape, BF)],
        scratch_shapes=[pltpu.VMEM((D, D), F32)],
        args=(dh, cat, a2, c2, z, lng, lnb, evo), carry=carry)


def _ev_in_bwd(h, dh, z, da2, dc2, dgb, g, wa, wb, wev, tm, after):
    s = h.shape[0]

    def body(h_ref, dh_ref, z_ref, da_ref, dap_ref, dan_ref, dc_ref, dcp_ref, dcn_ref,
             dgb_ref, g_ref, wa_ref, wb_ref, w_ref, after_ref,
             dh0_ref, dwa_ref, dba_ref, dwb_ref, dbb_ref, dg_ref, gw_ref,
             dapad, dcpad, dz, wacc, dwa_acc, dwb_acc, xs):
        i = pl.program_id(0)

        @pl.when(_first(i))
        def _():
            wacc[...] = jnp.zeros_like(wacc)
            dwa_acc[...] = jnp.zeros_like(dwa_acc)
            dwb_acc[...] = jnp.zeros_like(dwb_acc)
            dba_ref[...] = jnp.zeros_like(dba_ref)
            dbb_ref[...] = jnp.zeros_like(dbb_ref)
            dg_ref[...] = jnp.zeros_like(dg_ref)

        da2 = da_ref[...].astype(F32)
        dc2 = dc_ref[...].astype(F32)
        _fill_pad(dapad, dap_ref[...].astype(F32), da2, dan_ref[...].astype(F32), i, tm)
        _fill_pad(dcpad, dcp_ref[...].astype(F32), dc2, dcn_ref[...].astype(F32), i, tm)
        dba_ref[...] += _colsum(da2)
        dbb_ref[...] += _colsum(dc2)

        _shift8(xs, dapad, tm)
        for r in range(0, tm, RC):
            zcol = lambda lo: z_ref[pl.ds(r, RC), lo:lo + DA].astype(F32)
            aval = zcol(0)
            sgr = _sigmoid(zcol(DA))
            a = aval * sgr
            acc = jnp.zeros((RC, DA), F32)
            for k in range(KA):
                t = _tap(xs, r, 2 * HALO - 1 - k)
                acc = acc + t * wa_ref[pl.ds(k, 1), :]
                dwa_acc[k] += (t * a).reshape(RC // 8, 8, DA).sum(axis=0)
            dz[pl.ds(r, RC), 0:DA] = (acc * sgr).astype(BF)
            dz[pl.ds(r, RC), DA:2 * DA] = (acc * aval * sgr * (1.0 - sgr)).astype(BF)
            hb = zcol(1024)
            gc = zcol(2048)
            c = gc * hb
            acc = jnp.zeros((RC, DA), F32)
            for k in range(KB):
                t = dcpad[pl.ds(r + HALO + 1 - k, RC), :]
                acc = acc + t * wb_ref[pl.ds(k, 1), :]
                dwb_acc[k] += (t * c).reshape(RC // 8, 8, DA).sum(axis=0)
            dz[pl.ds(r, RC), 1024:1536] = (acc * gc).astype(BF)
            dz[pl.ds(r, RC), 2048:2560] = (acc * hb).astype(BF)
        dz[:, 1536:2048] = dgb_ref[...]

        gg = g_ref[...]
        hhat, rr = _rms(h_ref[...])
        n = (hhat * gg).astype(BF)
        dzb = dz[...]
        wacc[...] += _dot_tn(n, dzb)
        dn = _dot_nt(dzb, w_ref[...])
        dg_ref[...] += _colsum(dn * hhat)
        dh0_ref[...] = dh_ref[...] + _rms_bwd(dn, hhat, rr, gg)

        @pl.when(_last(i))
        def _():
            for k in range(NSH):
                gw_ref[k] = wacc[:, 640 * k:640 * (k + 1)].astype(BF)
            dwa_ref[...] = jnp.sum(dwa_acc[...], axis=1)
            dwb_ref[...] = jnp.sum(dwb_acc[...], axis=1)

    hp_spec, hn_spec = _halo_specs(tm, DA, s)
    small = lambda rows: _const((rows, DA), (0, 0))
    return _call(
        body, name="ev_in_bwd", grid=(s // tm,),
        in_specs=[_rows(tm, D), _rows(tm, D), _rows(tm, DZ),
                  _rows(tm, DA), hp_spec, hn_spec, _rows(tm, DA), hp_spec, hn_spec, _rows(tm, DA),
                  _const((1, D), (0, 0)), small(32), small(8), _const((D, DZ), (0, 0)), ANY],
        out_specs=[_rows(tm, D), small(32), small(1), small(8), small(1), _const((1, D), (0, 0)),
                   _const((NSH, D, 640), (0, 0, 0))],
        out_shape=[_sds((s, D), F32), _sds((32, DA), F32), _sds((1, DA), F32), _sds((8, DA), F32),
                   _sds((1, DA), F32), _sds((1, D), F32), _sds((NSH, D, 640), BF)],
        scratch_shapes=[pltpu.VMEM((tm + 2 * HALO, DA), F32)] * 2 + [
            pltpu.VMEM((tm, DZ), BF), pltpu.VMEM((D, DZ), F32),
            pltpu.VMEM((32, 8, DA), F32), pltpu.VMEM((8, 8, DA), F32),
            pltpu.VMEM((8, tm + 2 * HALO - 8, DA), F32)],
        args=(h, dh, z, da2, da2, da2, dc2, dc2, dc2, dgb, g, wa, wb, wev, after))


def _xa_spec(item):
    return _const((NSH, 256, D), (0, item, 0))


def _kv_proj(mem, g, xa, layer):
    def body(mem_ref, g_ref, wk_ref, wv_ref, k_ref, v_ref):
        mhat, _ = _rms(mem_ref[...])
        mn = (mhat * g_ref[...]).astype(BF)
        k_ref[...] = _dot(mn, wk_ref[...].reshape(D, D)).astype(BF)
        v_ref[...] = _dot(mn, wv_ref[...].reshape(D, D)).astype(BF)

    full = _const((NMEM, D), (0, 0))
    return _call(
        body, name=f"kv_proj{layer}", grid=(1,),
        in_specs=[full, _const((1, D), (0, 0)), _xa_spec(XK), _xa_spec(XV)],
        out_specs=[full, full], out_shape=[_sds((NMEM, D), BF), _sds((NMEM, D), BF)],
        args=(mem, g, xa, xa))


def _softmax_rows(sc):
    m = jnp.max(sc, axis=-1, keepdims=True)
    e = jnp.exp(sc - m)
    return e / jnp.sum(e, axis=-1, keepdims=True)


def _xattn_fwd(h, g, k, v, xa, layer, tm, carry=None):
    s = h.shape[0]

    def body(h_ref, g_ref, k_ref, v_ref, wq_ref, wo_ref, ho_ref, q_ref, o_scr):
        hh = h_ref[...]
        hhat, _ = _rms(hh)
        n = (hhat * g_ref[...]).astype(BF)
        q = _dot(n, wq_ref[...].reshape(D, D)).astype(BF)
        q_ref[...] = q
        for hd in range(HEADS):
            cs = slice(HD * hd, HD * (hd + 1))
            p = _softmax_rows(_dot_nt(q[:, cs], k_ref[:, cs]) * ATT_SCALE)
            o_scr[:, cs] = _dot(p.astype(BF), v_ref[:, cs]).astype(BF)
        ho_ref[...] = hh + _dot(o_scr[...], wo_ref[...].reshape(D, D))

    kvs = _const((NMEM, D), (0, 0))
    return _call(
        body, name=f"xattn_fwd{layer}", grid=(s // tm,),
        in_specs=[_rows(tm, D), _const((1, D), (0, 0)), kvs, kvs, _xa_spec(XQ), _xa_spec(XO)],
        out_specs=[_rows(tm, D), _rows(tm, D)],
        out_shape=[_sds((s, D), F32), _sds((s, D), BF)],
        scratch_shapes=[pltpu.VMEM((tm, D), BF)],
        args=(h, g, k, v, xa, xa), carry=carry)


def _xattn_bwd(h, dh, q, g, k, v, xa, layer, tm, carry=None):
    s = h.shape[0]

    def body(h_ref, dh_ref, q_ref, g_ref, k_ref, v_ref, wq_ref, wo_ref,
             dhi_ref, dk_ref, dv_ref, dg_ref, gqo_ref, o_scr, dq_scr, qacc, oacc):
        i = pl.program_id(0)

        @pl.when(_first(i))
        def _():
            qacc[...] = jnp.zeros_like(qacc)
            oacc[...] = jnp.zeros_like(oacc)
            dk_ref[...] = jnp.zeros_like(dk_ref)
            dv_ref[...] = jnp.zeros_like(dv_ref)
            dg_ref[...] = jnp.zeros_like(dg_ref)

        gg = g_ref[...]
        dho = dh_ref[...]
        dhb = dho.astype(BF)
        q = q_ref[...]
        do = _dot_nt(dhb, wo_ref[...].reshape(D, D)).astype(BF)
        for hd in range(HEADS):
            cs = slice(HD * hd, HD * (hd + 1))
            kh = k_ref[:, cs]
            vh = v_ref[:, cs]
            p = _softmax_rows(_dot_nt(q[:, cs], kh) * ATT_SCALE)
            pb = p.astype(BF)
            o_scr[:, cs] = _dot(pb, vh).astype(BF)
            doh = do[:, cs]
            dp = _dot_nt(doh, vh)
            dv_ref[:, cs] += _dot_tn(pb, doh)
            ds = (p * (dp - jnp.sum(dp * p, axis=-1, keepdims=True)) * ATT_SCALE).astype(BF)
            dq_scr[:, cs] = _dot(ds, kh).astype(BF)
            dk_ref[:, cs] += _dot_tn(ds, q[:, cs])
        oacc[...] += _dot_tn(o_scr[...], dhb)
        hhat, rr = _rms(h_ref[...])
        n = (hhat * gg).astype(BF)
        dq = dq_scr[...]
        qacc[...] += _dot_tn(n, dq)
        dn = _dot_nt(dq, wq_ref[...].reshape(D, D))
        dg_ref[...] += _colsum(dn * hhat)
        dhi_ref[...] = dho + _rms_bwd(dn, hhat, rr, gg)

        @pl.when(_last(i))
        def _():
            gqo_ref[:, 0:256, :] = qacc[...].astype(BF).reshape(NSH, 256, D)
            gqo_ref[:, 256:512, :] = oacc[...].astype(BF).reshape(NSH, 256, D)

    kvs = _const((NMEM, D), (0, 0))
    return _call(
        body, name=f"xattn_bwd{layer}", grid=(s // tm,),
        in_specs=[_rows(tm, D), _rows(tm, D), _rows(tm, D), _const((1, D), (0, 0)), kvs, kvs,
                  _xa_spec(XQ), _xa_spec(XO)],
        out_specs=[_rows(tm, D), kvs, kvs, _const((1, D), (0, 0)), _const((NSH, 512, D), (0, 0, 0))],
        out_shape=[_sds((s, D), F32), _sds((NMEM, D), F32), _sds((NMEM, D), F32), _sds((1, D), F32),
                   _sds(xa.shape, BF)],
        scratch_shapes=[pltpu.VMEM((tm, D), BF), pltpu.VMEM((tm, D), BF), pltpu.VMEM((D, D), F32),
                        pltpu.VMEM((D, D), F32)],
        args=(h, dh, q, g, k, v, xa, xa), carry=carry)


def _kv_proj_bwd(mem, g, dk, dv, xa, gxa, layer):
    def body(mem_ref, g_ref, dk_ref, dv_ref, wk_ref, wv_ref, gxa_in, dg_ref, gkv_ref):
        mhat, _ = _rms(mem_ref[...])
        mn = (mhat * g_ref[...]).astype(BF)
        dkb = dk_ref[...].astype(BF)
        dvb = dv_ref[...].astype(BF)
        gkv_ref[:, 0:256, :] = _dot_tn(mn, dkb).astype(BF).reshape(NSH, 256, D)
        gkv_ref[:, 256:512, :] = _dot_tn(mn, dvb).astype(BF).reshape(NSH, 256, D)
        dmn = _dot_nt(dkb, wk_ref[...].reshape(D, D)) + _dot_nt(dvb, wv_ref[...].reshape(D, D))
        dg_ref[...] = _colsum(dmn * mhat)

    full = _const((NMEM, D), (0, 0))
    return _call(
        body, name=f"kv_proj_bwd{layer}", grid=(1,),
        in_specs=[full, _const((1, D), (0, 0)), full, full, _xa_spec(XK), _xa_spec(XV), ANY],
        out_specs=[_const((1, D), (0, 0)), _const((NSH, 512, D), (0, 1, 0))],
        out_shape=[_sds((1, D), F32), _sds(gxa.shape, BF)],
        aliases={6: 1},
        args=(mem, g, dk, dv, xa, xa, gxa))


def _ffn_block(h_ref, g_ref, wg_ref, wu_ref, wd_ref, gt_ref, up_ref):
    hh = h_ref[...]
    hhat, _ = _rms(hh)
    n = (hhat * g_ref[...]).astype(BF)
    out = hh
    for c0, c1 in FF_CHUNKS:
        gt = _dot_nt(n, wg_ref[c0:c1, :])
        up = _dot_nt(n, wu_ref[c0:c1, :])
        gt_ref[:, c0:c1] = gt.astype(BF)
        up_ref[:, c0:c1] = up.astype(BF)
        act = (gt * _sigmoid(gt) * up).astype(BF)
        out = out + _dot(act, wd_ref[c0:c1, :])
    return out


def _ffn_fwd(h, g, wg, wu, dn, layer, tm, carry=None):
    s = h.shape[0]
    wg, wu, dn = (a.reshape(DFF, D) for a in (wg, wu, dn))

    def body(h_ref, g_ref, wg_ref, wu_ref, wd_ref, ho_ref, gt_ref, up_ref):
        ho_ref[...] = _ffn_block(h_ref, g_ref, wg_ref, wu_ref, wd_ref, gt_ref, up_ref)

    return _call(
        body, name=f"ffn_fwd{layer}", grid=(s // tm,),
        in_specs=[_rows(tm, D), _const((1, D), (0, 0)), _whole(wg), _whole(wu), _whole(dn)],
        out_specs=[_rows(tm, D), _rows(tm, DFF), _rows(tm, DFF)],
        out_shape=[_sds((s, D), F32), _sds((s, DFF), BF), _sds((s, DFF), BF)],
        args=(h, g, wg, wu, dn), carry=carry)


def _ffn_fwd_loss(h, g, wg, wu, dn, tgt, g_final, layer, tm):
    s = h.shape[0]
    wg, wu, dn = (a.reshape(DFF, D) for a in (wg, wu, dn))

    def body(h_ref, g_ref, wg_ref, wu_ref, wd_ref, t_ref, gf_ref, dh_ref, loss_ref, dg_ref, gt_ref, up_ref):
        i = pl.program_id(0)

        @pl.when(_first(i))
        def _():
            loss_ref[...] = jnp.zeros_like(loss_ref)
            dg_ref[...] = jnp.zeros_like(dg_ref)

        out = _ffn_block(h_ref, g_ref, wg_ref, wu_ref, wd_ref, gt_ref, up_ref)
        gg = gf_ref[...]
        hhat, rr = _rms(out)
        diff = hhat * gg - t_ref[...]
        loss_ref[...] += jnp.sum(diff * diff) * (0.5 / D)
        dout = diff * (1.0 / D)
        dg_ref[...] += _colsum(dout * hhat)
        dh_ref[...] = _rms_bwd(dout, hhat, rr, gg)

    vec = _const((1, D), (0, 0))
    return _call(
        body, name=f"ffn_fwd_loss{layer}", grid=(s // tm,),
        in_specs=[_rows(tm, D), vec, _whole(wg), _whole(wu), _whole(dn), _rows(tm, D), vec],
        out_specs=[_rows(tm, D), _const((1, 128), (0, 0)), vec, _rows(tm, DFF), _rows(tm, DFF)],
        out_shape=[_sds((s, D), F32), _sds((1, 128), F32), _sds((1, D), F32), _sds((s, DFF), BF),
                   _sds((s, DFF), BF)],
        args=(h, g, wg, wu, dn, tgt, g_final))


def _ffn_bwd_dx(h, dh, gt, up, g, wg, wu, dn, layer, tm, carry=None):
    s = h.shape[0]
    wg, wu, dn = (a.reshape(DFF, D) for a in (wg, wu, dn))

    def body(h_ref, dh_ref, gt_ref, up_ref, g_ref, wg_ref, wu_ref, wd_ref,
             dhi_ref, dg_ref, dgt_ref, dup_ref, act_ref, n_ref):
        i = pl.program_id(0)

        @pl.when(_first(i))
        def _():
            dg_ref[...] = jnp.zeros_like(dg_ref)

        gg = g_ref[...]
        dho = dh_ref[...]
        dhb = dho.astype(BF)
        dn_ = jnp.zeros((tm, D), F32)
        for c0, c1 in FF_CHUNKS:
            gtv = gt_ref[:, c0:c1].astype(F32)
            upv = up_ref[:, c0:c1].astype(F32)
            sg = _sigmoid(gtv)
            silu = gtv * sg
            dact = _dot_nt(dhb, wd_ref[c0:c1, :])
            dgt = (dact * upv * (sg * (1.0 + gtv * (1.0 - sg)))).astype(BF)
            dup = (dact * silu).astype(BF)
            dgt_ref[:, c0:c1] = dgt
            dup_ref[:, c0:c1] = dup
            act_ref[:, c0:c1] = (silu * upv).astype(BF)
            dn_ = dn_ + _dot(dgt, wg_ref[c0:c1, :]) + _dot(dup, wu_ref[c0:c1, :])
        hhat, rr = _rms(h_ref[...])
        n_ref[...] = (hhat * gg).astype(BF)
        dg_ref[...] += _colsum(dn_ * hhat)
        dhi_ref[...] = dho + _rms_bwd(dn_, hhat, rr, gg)

    wide = _rows(tm, DFF)
    return _call(
        body, name=f"ffn_bwd_dx{layer}", grid=(s // tm,),
        in_specs=[_rows(tm, D), _rows(tm, D), wide, wide, _const((1, D), (0, 0)),
                  _whole(wg), _whole(wu), _whole(dn)],
        out_specs=[_rows(tm, D), _const((1, D), (0, 0)), wide, wide, wide, _rows(tm, D)],
        out_shape=[_sds((s, D), F32), _sds((1, D), F32), _sds((s, DFF), BF), _sds((s, DFF), BF),
                   _sds((s, DFF), BF), _sds((s, D), BF)],
        args=(h, dh, gt, up, g, wg, wu, dn), carry=carry)


def _ffn_wgrad(a, b, name, tm, slot, into=None, carry=None):
    s = a.shape[0]
    tm = min(tm, s)

    def body(a_ref, b_ref, *rest):
        o_ref, acc = rest[-2:]
        i = pl.program_id(0)

        @pl.when(_first(i))
        def _():
            acc[...] = jnp.zeros_like(acc)

        acc[...] += _dot_tn(a_ref[...], b_ref[...].astype(BF))

        @pl.when(_last(i))
        def _():
            o_ref[...] = acc[...].astype(BF).reshape(NSH, FSH, D)

    chained = into is not None
    return _call(
        body, name=name, grid=(s // tm,),
        in_specs=[_rows(tm, DFF), _rows(tm, D)] + [ANY] * chained,
        out_specs=[_const((NSH, FSH, D), (0, slot, 0))],
        out_shape=[_sds((NSH, 3 * FSH, D), BF)],
        scratch_shapes=[pltpu.VMEM((DFF, D), F32)],
        aliases={2: 0} if chained else None,
        args=(a, b) + ((into,) if chained else ()), carry=carry)


def _sgu_fwd(h, g, lng, lnb, wst, bsb, wod, odo, tm, carry=None):
    s = h.shape[0]
    nc = tm // CHUNK

    def body(h_ref, g_ref, lng_ref, lnb_ref, wst_ref, bsb_ref, wi_ref, wo_ref, ho_ref, zp_ref, y_scr):
        hh = h_ref[...]
        hhat, _ = _rms(hh)
        n = (hhat * g_ref[...]).astype(BF)
        zpre = _dot(n, wi_ref[...])
        zp_ref[...] = zpre.astype(BF)
        z = _gelu(zpre)
        u = z[:, :D]
        y, _ = _ln(z[:, D:])
        vn = y * lng_ref[...] + lnb_ref[...]
        for gi in range(GROUPS):
            cs = slice(CHUNK * gi, CHUNK * (gi + 1))
            vt = jnp.swapaxes(vn[:, cs].astype(BF).reshape(nc, CHUNK, CHUNK), 1, 2).reshape(tm, CHUNK)
            svt = _dot(vt, wst_ref[gi]).reshape(nc, CHUNK, CHUNK)
            sv = jnp.swapaxes(svt, 1, 2) + bsb_ref[gi][None]
            y_scr[:, cs] = (u[:, cs] * sv.reshape(tm, CHUNK)).astype(BF)
        ho_ref[...] = hh + _dot(y_scr[...], wo_ref[...].reshape(D, D))

    vec = _const((1, D), (0, 0))
    sq = _const((GROUPS, CHUNK, CHUNK), (0, 0, 0))
    return _call(
        body, name="sgu_fwd", grid=(s // tm,),
        in_specs=[_rows(tm, D), vec, vec, vec, sq, sq, _const((D, 2 * D), (0, 0)), _whole(odo)],
        out_specs=[_rows(tm, D), _rows(tm, 2 * D)],
        out_shape=[_sds((s, D), F32), _sds((s, 2 * D), BF)],
        scratch_shapes=[pltpu.VMEM((tm, D), BF)],
        args=(h, g, lng, lnb, wst, bsb, wod, odo), carry=carry)


def _sgu_bwd(h, dh, zpre, g, lng, lnb, ws, wst, bsb, wod, odo, tm, carry=None):
    s = h.shape[0]

    def body(h_ref, dh_ref, zp_ref, g_ref, lng_ref, lnb_ref, ws_ref, wst_ref, bsb_ref, wi_ref, wo_ref,
             dhi_ref, dws_ref, dbs_ref, dlng_ref, dlnb_ref, dg_ref, gwi_ref, gwo_ref,
             y_scr, dz_scr, dvn_scr, iacc, oacc, bacc):
        i = pl.program_id(0)

        @pl.when(_first(i))
        def _():
            for ref in (iacc, oacc, bacc, dws_ref, dlng_ref, dlnb_ref, dg_ref):
                ref[...] = jnp.zeros_like(ref)

        gg = g_ref[...]
        dho = dh_ref[...]
        dhb = dho.astype(BF)
        zpre_v = zp_ref[...].astype(F32)
        z = _gelu(zpre_v)
        u = z[:, :D]
        yl, rstd = _ln(z[:, D:])
        vn = (yl * lng_ref[...] + lnb_ref[...]).astype(BF)
        dy = _dot_nt(dhb, wo_ref[...].reshape(D, D))
        for c in range(tm // CHUNK):
            rs = slice(CHUNK * c, CHUNK * (c + 1))
            for gi in range(GROUPS):
                cs = slice(CHUNK * gi, CHUNK * (gi + 1))
                vb = vn[rs, cs]
                sv = _dot(ws_ref[gi], vb) + bsb_ref[gi]
                ub = u[rs, cs]
                dyb = dy[rs, cs]
                y_scr[rs, cs] = (ub * sv).astype(BF)
                dz_scr[rs, cs] = dyb * sv
                dsv = dyb * ub
                bacc[gi] += dsv
                dsvb = dsv.astype(BF)
                dws_ref[gi] += _dot_nt(dsvb, vb)
                dvn_scr[rs, cs] = _dot(wst_ref[gi], dsvb)
        oacc[...] += _dot_tn(y_scr[...], dhb)
        dvn = dvn_scr[...]
        dlng_ref[...] += _colsum(dvn * yl)
        dlnb_ref[...] += _colsum(dvn)
        dz_scr[:, D:] = _ln_bwd(dvn * lng_ref[...], yl, rstd)
        dzb = (dz_scr[...] * _gelu_grad(zpre_v)).astype(BF)
        hhat, rr = _rms(h_ref[...])
        n = (hhat * gg).astype(BF)
        iacc[...] += _dot_tn(n, dzb)
        dn = _dot_nt(dzb, wi_ref[...])
        dg_ref[...] += _colsum(dn * hhat)
        dhi_ref[...] = dho + _rms_bwd(dn, hhat, rr, gg)

        @pl.when(_last(i))
        def _():
            for k in range(NSH):
                gwi_ref[k] = iacc[:, 512 * k:512 * (k + 1)].astype(BF)
            gwo_ref[...] = oacc[...].astype(BF).reshape(NSH, 256, D)
            dbs_ref[...] = jnp.sum(bacc[...], axis=-1)

    vec = _const((1, D), (0, 0))
    sq = _const((GROUPS, CHUNK, CHUNK), (0, 0, 0))
    return _call(
        body, name="sgu_bwd", grid=(s // tm,),
        in_specs=[_rows(tm, D), _rows(tm, D), _rows(tm, 2 * D), vec, vec, vec, sq, sq, sq,
                  _const((D, 2 * D), (0, 0)), _whole(odo)],
        out_specs=[_rows(tm, D), sq, _const((GROUPS, CHUNK), (0, 0)), vec, vec, vec,
                   _const((NSH, D, 512), (0, 0, 0)), _whole(odo)],
        out_shape=[_sds((s, D), F32), _sds((GROUPS, CHUNK, CHUNK), F32), _sds((GROUPS, CHUNK), F32),
                   _sds((1, D), F32), _sds((1, D), F32), _sds((1, D), F32), _sds((NSH, D, 512), BF),
                   _sds(odo.shape, BF)],
        scratch_shapes=[pltpu.VMEM((tm, D), BF), pltpu.VMEM((tm, 2 * D), F32),
                        pltpu.VMEM((tm, D), F32), pltpu.VMEM((D, 2 * D), F32), pltpu.VMEM((D, D), F32),
                        pltpu.VMEM((GROUPS, CHUNK, CHUNK), F32)],
        args=(h, dh, zpre, g, lng, lnb, ws, wst, bsb, wod, odo), carry=carry)


def _own_plane(shape):
    return pl.BlockSpec((1,) + shape, lambda i, chip_ref: (chip_ref[0], 0, 0), pipeline_mode=pl.Buffered(1))


def _pack_rows(chip_idx, w_down, w_gate_t, w_up_t, xq, xo, xk, xv, ev_out, od_out, first):
    nf = len(first)

    def body(chip_ref, wd_ref, wg_ref, wu_ref, xq_ref, xo_ref, xk_ref, xv_ref, ev_ref, od_ref, *rest):
        (evo_ref, odo_ref, dn0_ref, dn1_ref, wg0_ref, wg1_ref, wu0_ref, wu1_ref, xa0_ref,
         xa1_ref) = rest[nf:nf + 10]
        gathered = rest[nf + 10:2 * nf + 10]
        sems = rest[2 * nf + 10:]
        _gather_start(gathered, sems)
        evo_ref[0] = ev_ref[0].astype(BF)
        odo_ref[0] = od_ref[0].astype(BF)
        for src, outs in ((wd_ref, (dn0_ref, dn1_ref)), (wg_ref, (wg0_ref, wg1_ref)), (wu_ref, (wu0_ref, wu1_ref))):
            for layer, o_ref in enumerate(outs):
                o_ref[0] = src[layer].astype(BF)
        for layer, xa_ref in enumerate((xa0_ref, xa1_ref)):
            for item, src in ((XQ, xq_ref), (XO, xo_ref), (XK, xk_ref), (XV, xv_ref)):
                xa_ref[0, 256 * item:256 * (item + 1), :] = src[layer].astype(BF)
        _gather_forward(gathered, sems)
        _gather_drain(gathered, sems)

    ins = (w_down, w_gate_t, w_up_t, xq, xo, xk, xv, ev_out, od_out)
    shapes = [(256, D), (256, D)] + [(FSH, D)] * 6 + [(4 * 256, D), (4 * 256, D)]
    grid_spec = pltpu.PrefetchScalarGridSpec(
        num_scalar_prefetch=1, grid=(1,), in_specs=[_whole(a) for a in ins] + [ANY] * nf,
        out_specs=[_own_plane(sh) for sh in shapes] + [ANY] * nf,
        scratch_shapes=_gather_sems(nf))
    outs = pl.pallas_call(
        body, name="pack_rows", grid_spec=grid_spec,
        out_shape=[_sds((NSH,) + sh, BF) for sh in shapes] + [_sds(a.shape, a.dtype) for a in first],
        input_output_aliases={1 + len(ins) + t: len(shapes) + t for t in range(nf)},
        compiler_params=_params(),
    )(chip_idx, *ins, *first)
    return outs[:len(shapes)], outs[len(shapes):]


def _pack_cols(chip_idx, od_in, ev_in, conv_a, conv_b, ln_g, ln_b):
    def body(chip_ref, od_ref, ev_ref, ca_ref, cb_ref, lg_ref, lb_ref, cev_ref, cod_ref, sp_ref):
        cev_ref[0] = ev_ref[0].astype(BF)
        cod_ref[0] = od_ref[0].astype(BF)
        sp_ref[...] = jnp.zeros_like(sp_ref)
        sp_ref[0, 0:KA, 0:128] = ca_ref[0]
        sp_ref[0, 32:32 + KB, 0:128] = cb_ref[0]
        sp_ref[0, 40:41, :] = lg_ref[...]
        sp_ref[0, 41:42, :] = lb_ref[...]

    ins = (od_in, ev_in, conv_a, conv_b, ln_g, ln_b)
    shapes = [(D, 640), (D, 512)]
    grid_spec = pltpu.PrefetchScalarGridSpec(
        num_scalar_prefetch=1, grid=(1,), in_specs=[_whole(a) for a in ins],
        out_specs=[_own_plane(sh) for sh in shapes] + [_own_plane((SP_ROWS, 256))])
    return pl.pallas_call(
        body, name="pack_cols", grid_spec=grid_spec,
        out_shape=[_sds((NSH,) + sh, BF) for sh in shapes] + [_sds((NSH, SP_ROWS, 256), F32)],
        compiler_params=_params(),
    )(chip_idx, *ins)


def _assemble(piece, parts, name):
    tr = 256
    cols = piece.shape[2]

    def body(p_ref, *outs):
        for k in range(NSH):
            for (off, width), o_ref in zip(parts, outs):
                o_ref[:, width * k:width * (k + 1)] = p_ref[k, :, off:off + width]

    return _call(
        body, name=name, grid=(D // tr,),
        in_specs=[pl.BlockSpec((NSH, tr, cols), lambda i: (0, i, 0))],
        out_specs=[pl.BlockSpec((tr, NSH * width), lambda i: (i, 0)) for _, width in parts],
        out_shape=[_sds((D, NSH * width), BF) for _, width in parts],
        args=(piece,))


def _blocks_of(rows, cols, itemsize):
    rb = rows
    while rb * cols * itemsize > (2 << 20) and rb % 2 == 0 and (rb // 2) % 16 == 0:
        rb //= 2
    return rb


def _add_half(full, recv, c_idx, name):
    p, r, cols = full.shape
    rb = _blocks_of(r // 2, cols, 4)
    nb = (r // 2) // rb

    def body(c_ref, a_ref, b_ref, o_ref):
        o_ref[...] = (a_ref[...].astype(F32) + b_ref[...].astype(F32)).astype(o_ref.dtype)

    grid_spec = pltpu.PrefetchScalarGridSpec(
        num_scalar_prefetch=1, grid=(p, nb),
        in_specs=[pl.BlockSpec((1, rb, cols), lambda j, i, c_ref: (j, c_ref[0] * nb + i, 0)),
                  pl.BlockSpec((1, rb, cols), lambda j, i, c_ref: (j, i, 0))],
        out_specs=pl.BlockSpec((1, rb, cols), lambda j, i, c_ref: (j, i, 0)))
    return pl.pallas_call(
        body, name=name, grid_spec=grid_spec, out_shape=_sds(recv.shape, full.dtype),
        compiler_params=_params(2),
    )(c_idx, full, recv)


def _sum_chips(own, got, plane_c, name):
    _, r, cols = own.shape
    rb = _blocks_of(r, cols, 4)
    nb = r // rb

    def body(s_ref, a_ref, b_ref, o_ref):
        f = lambda v: v.astype(F32)
        o_ref[...] = (f(a_ref[0]) + f(b_ref[1])) + (f(b_ref[0]) + f(b_ref[2]))

    grid_spec = pltpu.PrefetchScalarGridSpec(
        num_scalar_prefetch=1, grid=(nb,),
        in_specs=[pl.BlockSpec((1, rb, cols), lambda i, s_ref: (s_ref[0], i, 0)),
                  pl.BlockSpec((3, rb, cols), lambda i, s_ref: (0, i, 0))],
        out_specs=pl.BlockSpec((rb, cols), lambda i, s_ref: (s_ref[1] * nb + i, 0)))
    return pl.pallas_call(
        body, name=name, grid_spec=grid_spec, out_shape=_sds((2 * r, cols), F32),
        compiler_params=_params(1),
    )(plane_c, own, got)


def _adamw_math(w, g, m, v):
    m = ADAM_B1 * m + (1.0 - ADAM_B1) * g
    v = ADAM_B2 * v + (1.0 - ADAM_B2) * (g * g)
    m_hat = m / (1.0 - ADAM_B1 ** ADAM_STEP)
    v_hat = v / (1.0 - ADAM_B2 ** ADAM_STEP)
    delta = -ADAM_LR * (m_hat / (jnp.sqrt(v_hat) + ADAM_EPS) + ADAM_WD * w)
    return delta, m, v


def _adamw_big(w, m, v, gsrcs, gblock, gidx, name, rb):
    nl, r, cols = w.shape
    assert len(gsrcs) == nl

    def body(w_ref, m_ref, v_ref, *rest):
        g_refs, (go_ref, d_ref, mo_ref, vo_ref) = rest[:nl], rest[nl:]
        g = g_refs[0][:, 0:cols]
        if nl == 2:
            g = jnp.where(pl.program_id(0) == 0, g, g_refs[1][:, 0:cols])
        delta, mn, vn = _adamw_math(w_ref[0], g, m_ref[0], v_ref[0])
        go_ref[0] = g
        d_ref[0] = delta
        mo_ref[0] = mn
        vo_ref[0] = vn

    wspec = pl.BlockSpec((1, rb, cols), lambda l, i: (l, i, 0))
    gspec = pl.BlockSpec(gblock, lambda l, i: gidx(i))
    return pl.pallas_call(
        body, name=name, grid=(nl, r // rb),
        in_specs=[wspec, wspec, wspec] + [gspec] * nl,
        out_specs=[wspec] * 4, out_shape=[_sds(w.shape, F32)] * 4,
        compiler_params=_params(2),
    )(w, m, v, *gsrcs)


def _adamw_rows(ws, ms, vs, gsrcs, name, rb=64):
    n = len(ws)
    nl, r, cols = ws[0].shape
    nb = r // rb

    def body(*refs):
        w_refs, m_refs, v_refs = refs[:n], refs[n:2 * n], refs[2 * n:3 * n]
        g_refs = refs[3 * n:3 * n + nl * n]
        outs = refs[3 * n + nl * n:]
        for j in range(n):
            g = g_refs[j][...]
            for l in range(1, nl):
                g = jnp.where(pl.program_id(0) == l, g_refs[l * n + j][...], g)
            delta, mn, vn = _adamw_math(w_refs[j][0], g, m_refs[j][0], v_refs[j][0])
            for o_ref, val in zip(outs[4 * j:4 * j + 4], (g, delta, mn, vn)):
                o_ref[0] = val

    wspec = pl.BlockSpec((1, rb, cols), lambda l, i: (l, i, 0))
    gspecs = [pl.BlockSpec((rb, cols), lambda l, i, _j=j: (_j * nb + i, 0)) for _ in range(nl) for j in range(n)]
    outs = pl.pallas_call(
        body, name=name, grid=(nl, nb),
        in_specs=[wspec] * (3 * n) + gspecs,
        out_specs=[wspec] * (4 * n), out_shape=[_sds(ws[0].shape, F32)] * (4 * n),
        compiler_params=_params(2),
    )(*ws, *ms, *vs, *[gsrcs[l] for l in range(nl) for _ in range(n)])
    return [tuple(outs[4 * j:4 * j + 4]) for j in range(n)]


_SMALL = (
    ("g_mix", (2, D), "a", (0, 2), 0, False),
    ("g_xattn", (2, D), "a", (2, 4), 0, False),
    ("g_mem", (2, D), "a", (4, 6), 0, False),
    ("g_ffn", (2, D), "a", (6, 8), 0, False),
    ("g_final", (1, D), "a", (8, 9), 0, False),
    ("ev_a_conv_w", (KA, 128), "a", (16, 16 + KA), 0, True),
    ("ev_a_conv_b", (1, DA), "a", (11, 12), 0, False),
    ("ev_a_ln_g", (1, DA), "a", (11, 12), DA, False),
    ("ev_a_ln_b", (1, DA), "a", (12, 13), 0, False),
    ("ev_b_conv_w", (KB, 128), "a", (48, 48 + KB), 0, True),
    ("ev_b_conv_b", (1, DA), "a", (12, 13), DA, False),
    ("od_c_ln_g", (1, 256), "a", (9, 10), 0, True),
    ("od_c_ln_b", (1, 256), "a", (10, 11), 0, True),
    ("od_w_s", (GROUPS * CHUNK, CHUNK), "b", (0, GROUPS * CHUNK), 0, False),
    ("od_b_s", (GROUPS, CHUNK), "a", (56, 64), 0, False),
)


def _adamw_small(ga, gb, wmv):
    ns = len(_SMALL)

    def body(*refs):
        ga_ref, gb_ref = refs[0], refs[1]
        ins = refs[2:2 + 3 * ns]
        outs = refs[2 + 3 * ns:]
        x, y, _ = _pos()
        chip = 2 * x + y
        for j, (_, shape, pack, (r0, r1), col, sharded) in enumerate(_SMALL):
            src = ga_ref if pack == "a" else gb_ref
            width = shape[1]
            if sharded:
                g = jnp.zeros(shape, F32)
                for k in range(NSH):
                    blk = src[r0:r1, col + width * k:col + width * (k + 1)]
                    g = g + jnp.where(chip == k, blk, 0.0)
            else:
                g = src[r0:r1, col:col + width]
            w_ref, m_ref, v_ref = ins[3 * j:3 * j + 3]
            delta, mn, vn = _adamw_math(w_ref[...], g, m_ref[...], v_ref[...])
            go_ref, d_ref, mo_ref, vo_ref = outs[4 * j:4 * j + 4]
            go_ref[...] = g
            d_ref[...] = delta
            mo_ref[...] = mn
            vo_ref[...] = vn

    vm = pl.BlockSpec(memory_space=pltpu.VMEM)
    flat = [a for trio in wmv for a in trio]
    out_shape = []
    for (_, shape, *_rest) in _SMALL:
        out_shape += [_sds(shape, F32)] * 4
    outs = pl.pallas_call(
        body, name="adamw_small",
        in_specs=[vm] * (2 + 3 * ns), out_specs=[vm] * (4 * ns), out_shape=out_shape,
    )(ga, gb, *flat)
    return [tuple(outs[4 * j:4 * j + 4]) for j in range(ns)]


def _small_params(gs, W):
    unshard = lambda a: jnp.transpose(a, (1, 0, 2)).reshape(a.shape[1], NSH * a.shape[2])
    od_w_s, od_b_s = W["od_w_s"], W["od_b_s"]
    return dict(
        g_mix=W["g_mix"], g_xattn=W["g_xattn"], g_mem=W["g_mem"], g_ffn=W["g_ffn"],
        g_final=W["g_final"].reshape(1, D),
        wa=unshard(gs[:, 0:32, 0:128]), ba=W["ev_a_conv_b"], lng_a=W["ev_a_ln_g"], lnb_a=W["ev_a_ln_b"],
        wb=unshard(gs[:, 32:40, 0:128]), bb=W["ev_b_conv_b"],
        lng_c=unshard(gs[:, 40:41, :]), lnb_c=unshard(gs[:, 41:42, :]),
        ws=od_w_s[0].astype(BF), wst=jnp.swapaxes(od_w_s[0], 1, 2).astype(BF),
        bsb=jnp.broadcast_to(od_b_s[0][:, :, None], (GROUPS, CHUNK, CHUNK)),
    )


LOSS_ROW = 13


def _small_grad_packs(g, loss):
    z = lambda r, c: jnp.zeros((r, c), F32)
    wide = lambda a: jnp.concatenate([a, z(a.shape[0], D - a.shape[1])], axis=1)
    rows = [g["g_mix"], g["g_xattn"], g["g_mem"], g["g_ffn"], g["g_final"], g["od_c_ln_g"], g["od_c_ln_b"],
            jnp.concatenate([g["ev_a_conv_b"], g["ev_a_ln_g"]], axis=1),
            jnp.concatenate([g["ev_a_ln_b"], g["ev_b_conv_b"]], axis=1),
            wide(loss), z(2, D), wide(g["ev_a_conv_w"]), wide(g["ev_b_conv_w"]), wide(g["od_b_s"])]
    a = jnp.concatenate(rows, axis=0)
    assert a.shape == (SA_ROWS, D), a.shape
    return a, g["od_w_s"].reshape(GROUPS * CHUNK, CHUNK)


def _step(x, mem, tgt, W, P, c_idx, tm_f, tm_b, tm_w):
    row = lambda a, i: a[i:i + 1]

    sm = _small_params(P["gs"], W)
    (wev,) = _assemble(P["cev"], [(0, 640)], "assemble_ev")
    z, evo, wg0 = _ev_in_fwd(x, row(sm["g_mix"], 0), wev, 2 * tm_f, carry=("gather", [P["evo"], P["wg0"]]))
    h1, a2, c2, cat, xa0, wu0 = _ev_mix_fwd(
        x, z, sm["wa"], sm["ba"], sm["lng_a"], sm["lnb_a"], sm["wb"], sm["bb"], evo, 2 * tm_f,
        carry=("gather", [P["xa0"], P["wu0"]]))
    k0, v0 = _kv_proj(mem, row(sm["g_mem"], 0), xa0, 0)
    h2, q0, dn0 = _xattn_fwd(h1, row(sm["g_xattn"], 0), k0, v0, xa0, 0, 2 * tm_f, carry=("gather", [P["dn0"]]))
    h3, gt0, up0, odo, xa1, wg1, cod = _ffn_fwd(h2, row(sm["g_ffn"], 0), wg0, wu0, dn0, 0, tm_f,
                                                carry=("gather", [P["odo"], P["xa1"], P["wg1"], P["cod"]]))
    (wod,) = _assemble(cod, [(0, 512)], "assemble_od")
    h4, zpre, wu1, dn1 = _sgu_fwd(h3, row(sm["g_mix"], 1), sm["lng_c"], sm["lnb_c"], sm["wst"], sm["bsb"], wod,
                                  odo, 2 * tm_f, carry=("gather", [P["wu1"], P["dn1"]]))
    k1, v1 = _kv_proj(mem, row(sm["g_mem"], 1), xa1, 1)
    h5, q1 = _xattn_fwd(h4, row(sm["g_xattn"], 1), k1, v1, xa1, 1, 2 * tm_f)
    dh6, loss, dg_final, gt1, up1 = _ffn_fwd_loss(h5, row(sm["g_ffn"], 1), wg1, wu1, dn1, tgt, sm["g_final"], 1,
                                                  tm_f)

    g = {"g_final": dg_final}
    parts, gots = {}, {}
    def add(name, piece, recv):
        parts[name] = _add_half(piece, recv, c_idx, f"rs_add_{name}")
        return parts[name]

    def ffn_wgrads(layer, dgt, dup, act, nb, dh_out, carry=None):
        g_ffn, *carried = _ffn_wgrad(dgt, nb, f"ffn_wgrad_gate{layer}", tm_w, 0, carry=carry)
        (g_ffn,) = _ffn_wgrad(dup, nb, f"ffn_wgrad_up{layer}", tm_w, 1, into=g_ffn)
        (g_ffn,) = _ffn_wgrad(act, dh_out, f"ffn_wgrad_down{layer}", tm_w, 2, into=g_ffn)
        return g_ffn, carried

    dh5, dg_ffn1, dgt, dup, act, nb = _ffn_bwd_dx(h5, dh6, gt1, up1, row(sm["g_ffn"], 1), wg1, wu1, dn1, 1, tm_b)
    g_ffn1, _ = ffn_wgrads(1, dgt, dup, act, nb, dh6)
    dh4, dk, dv, dg_xa1, g_xa1, recv = _xattn_bwd(
        h4, dh5, q1, row(sm["g_xattn"], 1), k1, v1, xa1, 1, tm_f, carry=("swap", [g_ffn1]))
    add("ffn1", g_ffn1, recv)
    dg_mem1, g_xa1 = _kv_proj_bwd(mem, row(sm["g_mem"], 1), dk, dv, xa1, g_xa1, 1)
    (dh3, g["od_w_s"], g["od_b_s"], g["od_c_ln_g"], g["od_c_ln_b"], dg_mix1, g_cod, g_odo, gots["ffn1"],
     recv) = _sgu_bwd(
        h3, dh4, zpre, row(sm["g_mix"], 1), sm["lng_c"], sm["lnb_c"], sm["ws"], sm["wst"], sm["bsb"], wod, odo, tm_b,
        carry=[("xchg", [parts["ffn1"]], []), ("swap", [g_xa1])])
    add("xa1", g_xa1, recv)
    dh2, dg_ffn0, dgt, dup, act, nb, gots["xa1"], recv_cod, recv_odo = _ffn_bwd_dx(
        h2, dh3, gt0, up0, row(sm["g_ffn"], 0), wg0, wu0, dn0, 0, tm_b,
        carry=[("xchg", [parts["xa1"]], []), ("swap", [g_cod, g_odo])])
    add("cod", g_cod, recv_cod)
    add("odo", g_odo, recv_odo)
    g_ffn0, (gots["cod"], gots["odo"]) = ffn_wgrads(0, dgt, dup, act, nb, dh3,
                                                    carry=("xchg", [parts["cod"], parts["odo"]], []))
    dh1, dk, dv, dg_xa0, g_xa0, recv = _xattn_bwd(
        h1, dh2, q0, row(sm["g_xattn"], 0), k0, v0, xa0, 0, tm_f, carry=("swap", [g_ffn0]))
    add("ffn0", g_ffn0, recv)
    dg_mem0, g_xa0 = _kv_proj_bwd(mem, row(sm["g_mem"], 0), dk, dv, xa0, g_xa0, 0)
    da2, dc2, dgb, g["ev_a_ln_g"], g["ev_a_ln_b"], g_evo, recv = _ev_mix_bwd(
        dh1, cat, a2, c2, z, sm["lng_a"], sm["lnb_a"], evo, 2 * tm_f, carry=("swap", [g_xa0]))
    add("xa0", g_xa0, recv)
    handle, token = _comm_start([parts["ffn0"], parts["xa0"]], [], [g_evo], "ev")
    dx, g["ev_a_conv_w"], g["ev_a_conv_b"], g["ev_b_conv_w"], g["ev_b_conv_b"], dg_mix0, g_cev = _ev_in_bwd(
        x, dh1, z, da2, dc2, dgb, row(sm["g_mix"], 0), sm["wa"], sm["wb"], wev, tm_f, token)
    (parts["ffn0"], parts["xa0"], g_evo), (gots["ffn0"], gots["xa0"], recv) = _comm_wait(handle, g_cev)
    add("evo", g_evo, recv)
    g["g_mix"] = jnp.concatenate([dg_mix0, dg_mix1], axis=0)
    g["g_xattn"] = jnp.concatenate([dg_xa0, dg_xa1], axis=0)
    g["g_mem"] = jnp.concatenate([dg_mem0, dg_mem1], axis=0)
    g["g_ffn"] = jnp.concatenate([dg_ffn0, dg_ffn1], axis=0)
    ga, gb = _small_grad_packs(g, loss)
    last = (g_cev, ga[None], gb[None])
    for name, piece, recv in zip(("cev", "ga", "gb"), last, _sibling_swap(last, "rs_swap_last")):
        add(name, piece, recv)
    return dx, parts, gots


def kernel(x, mem, g_mix, g_xattn, g_mem, g_ffn, g_final, ev_w_in, ev_a_conv_w, ev_a_conv_b, ev_a_ln_g, ev_a_ln_b, ev_b_conv_w, ev_b_conv_b, ev_w_out, od_w_in, od_c_ln_g, od_c_ln_b, od_w_s, od_b_s, od_w_out, xa_w_q, xa_w_k, xa_w_v, xa_w_o, ffn_w_gate, ffn_w_up, ffn_w_down, loss_target, m_g_mix, m_g_xattn, m_g_mem, m_g_ffn, m_g_final, m_ev_w_in, m_ev_a_conv_w, m_ev_a_conv_b, m_ev_a_ln_g, m_ev_a_ln_b, m_ev_b_conv_w, m_ev_b_conv_b, m_ev_w_out, m_od_w_in, m_od_c_ln_g, m_od_c_ln_b, m_od_w_s, m_od_b_s, m_od_w_out, m_xa_w_q, m_xa_w_k, m_xa_w_v, m_xa_w_o, m_ffn_w_gate, m_ffn_w_up, m_ffn_w_down, v_g_mix, v_g_xattn, v_g_mem, v_g_ffn, v_g_final, v_ev_w_in, v_ev_a_conv_w, v_ev_a_conv_b, v_ev_a_ln_g, v_ev_a_ln_b, v_ev_b_conv_w, v_ev_b_conv_b, v_ev_w_out, v_od_w_in, v_od_c_ln_g, v_od_c_ln_b, v_od_w_s, v_od_b_s, v_od_w_out, v_xa_w_q, v_xa_w_k, v_xa_w_v, v_xa_w_o, v_ffn_w_gate, v_ffn_w_up, v_ffn_w_down):
    W = dict(g_mix=g_mix, g_xattn=g_xattn, g_mem=g_mem, g_ffn=g_ffn, g_final=g_final, ev_w_in=ev_w_in,
             ev_a_conv_w=ev_a_conv_w, ev_a_conv_b=ev_a_conv_b, ev_a_ln_g=ev_a_ln_g, ev_a_ln_b=ev_a_ln_b,
             ev_b_conv_w=ev_b_conv_w, ev_b_conv_b=ev_b_conv_b, ev_w_out=ev_w_out, od_w_in=od_w_in,
             od_c_ln_g=od_c_ln_g, od_c_ln_b=od_c_ln_b, od_w_s=od_w_s, od_b_s=od_b_s, od_w_out=od_w_out,
             xa_w_q=xa_w_q, xa_w_k=xa_w_k, xa_w_v=xa_w_v, xa_w_o=xa_w_o, ffn_w_gate=ffn_w_gate,
             ffn_w_up=ffn_w_up, ffn_w_down=ffn_w_down)
    M = dict(g_mix=m_g_mix, g_xattn=m_g_xattn, g_mem=m_g_mem, g_ffn=m_g_ffn, g_final=m_g_final, ev_w_in=m_ev_w_in,
             ev_a_conv_w=m_ev_a_conv_w, ev_a_conv_b=m_ev_a_conv_b, ev_a_ln_g=m_ev_a_ln_g, ev_a_ln_b=m_ev_a_ln_b,
             ev_b_conv_w=m_ev_b_conv_w, ev_b_conv_b=m_ev_b_conv_b, ev_w_out=m_ev_w_out, od_w_in=m_od_w_in,
             od_c_ln_g=m_od_c_ln_g, od_c_ln_b=m_od_c_ln_b, od_w_s=m_od_w_s, od_b_s=m_od_b_s, od_w_out=m_od_w_out,
             xa_w_q=m_xa_w_q, xa_w_k=m_xa_w_k, xa_w_v=m_xa_w_v, xa_w_o=m_xa_w_o, ffn_w_gate=m_ffn_w_gate,
             ffn_w_up=m_ffn_w_up, ffn_w_down=m_ffn_w_down)
    V = dict(g_mix=v_g_mix, g_xattn=v_g_xattn, g_mem=v_g_mem, g_ffn=v_g_ffn, g_final=v_g_final, ev_w_in=v_ev_w_in,
             ev_a_conv_w=v_ev_a_conv_w, ev_a_conv_b=v_ev_a_conv_b, ev_a_ln_g=v_ev_a_ln_g, ev_a_ln_b=v_ev_a_ln_b,
             ev_b_conv_w=v_ev_b_conv_w, ev_b_conv_b=v_ev_b_conv_b, ev_w_out=v_ev_w_out, od_w_in=v_od_w_in,
             od_c_ln_g=v_od_c_ln_g, od_c_ln_b=v_od_c_ln_b, od_w_s=v_od_w_s, od_b_s=v_od_b_s, od_w_out=v_od_w_out,
             xa_w_q=v_xa_w_q, xa_w_k=v_xa_w_k, xa_w_v=v_xa_w_v, xa_w_o=v_xa_w_o, ffn_w_gate=v_ffn_w_gate,
             ffn_w_up=v_ffn_w_up, ffn_w_down=v_ffn_w_down)
    order = list(W)
    cx, cy, cc = _pos()
    chip = (2 * cx + cy).astype(jnp.int32)
    cc = cc.astype(jnp.int32)
    chip_idx = jnp.reshape(chip, (1,))
    c_idx = jnp.reshape(cc, (1,))

    tr = lambda a: jnp.swapaxes(a, 1, 2)
    P = {}
    cev, P["cod"], gs = _pack_cols(chip_idx, od_w_in, ev_w_in, ev_a_conv_w, ev_b_conv_w, od_c_ln_g, od_c_ln_b)
    ((P["evo"], P["odo"], P["dn0"], P["dn1"], P["wg0"], P["wg1"], P["wu0"], P["wu1"], P["xa0"], P["xa1"]),
     (P["cev"], P["gs"])) = _pack_rows(chip_idx, ffn_w_down, tr(ffn_w_gate), tr(ffn_w_up), xa_w_q, xa_w_o,
                                       xa_w_k, xa_w_v, ev_w_out, od_w_out, first=[cev, gs])

    dx, parts, gots = _step(x[0], mem[0], loss_target[0], W, P, c_idx, TM_FWD, TM_BWD, TM_WGRAD)

    chip_c = jnp.stack([chip, cc])
    zero_c = jnp.stack([jnp.zeros((), jnp.int32), cc])
    def sums(names):
        return [_sum_chips(parts[n], gots[n], zero_c if n in ("ga", "gb") else chip_c, f"rs_sum_{n}")
                for n in names]

    last = ("evo", "cev", "ga", "gb")
    handle, token = _comm_start([parts["evo"], parts["cev"]], [parts["ga"][0], parts["gb"][0]], [], "last")
    big_names, rest_names = ["ffn1", "ffn0"], ["xa1", "xa0", "cod", "odo"]
    join_big, token = _join_start(sums(big_names), "ffn", after=(token,))
    join_rest, token = _join_start(sums(rest_names), "rest", after=(token,))
    red = dict(zip(big_names, _join_wait(join_big, token)))

    res = {}

    def adamw(name, srcs, gblock, gidx, rb):
        t = tr if name in ("ffn_w_gate", "ffn_w_up") else (lambda a: a)
        outs = _adamw_big(t(W[name]), t(M[name]), t(V[name]), srcs, gblock, gidx, f"adamw_{name}", rb)
        res[name] = tuple(t(o) for o in outs)

    ffn = [red["ffn0"], red["ffn1"]]
    adamw("ffn_w_gate", ffn, (FSH, D), lambda i: (0, 0), FSH)
    adamw("ffn_w_up", ffn, (FSH, D), lambda i: (1, 0), FSH)
    adamw("ffn_w_down", ffn, (FSH, D), lambda i: (2, 0), FSH)
    red.update(zip(rest_names, _join_wait(join_rest, res["ffn_w_down"][1])))
    xa_names = ("xa_w_q", "xa_w_o", "xa_w_k", "xa_w_v")
    assert (XQ, XO, XK, XV) == (0, 1, 2, 3)
    for name, outs in zip(xa_names, _adamw_rows([W[n] for n in xa_names], [M[n] for n in xa_names],
                                                [V[n] for n in xa_names], [red["xa0"], red["xa1"]], "adamw_xa")):
        res[name] = outs
    adamw("od_w_out", [red["odo"]], (256, D), lambda i: (0, 0), 256)
    adamw("od_w_in", [red["cod"]], (256, 512), lambda i: (i, 0), 256)

    srcs_done, got_last = _comm_wait(handle, res["od_w_in"][1])
    parts.update(zip(last, srcs_done[:2] + [a[None] for a in srcs_done[2:]]))
    gots.update(zip(last, got_last))
    red.update(zip(last, _sibling_join(sums(last), "rs_join_last")))
    adamw("ev_w_out", [red["evo"]], (256, D), lambda i: (0, 0), 256)
    adamw("ev_w_in", [red["cev"]], (256, 640), lambda i: (i, 0), 256)
    shape2 = {name: shape for name, shape, *_ in _SMALL}
    wmv = [tuple(d[name].reshape(shape2[name]) for d in (W, M, V)) for name, *_ in _SMALL]
    for (name, *_), outs in zip(_SMALL, _adamw_small(red["ga"], red["gb"], wmv)):
        res[name] = tuple(o.reshape(W[name].shape) for o in outs)

    grad_x = dx[None]
    loss = red["ga"][LOSS_ROW, 0]
    return (loss, grad_x, *[res[n][0] for n in order], *[res[n][1] for n in order],
            *[res[n][2] for n in order], *[res[n][3] for n in order])
```
